```python
import jax, jax.numpy as jnp
from jax import lax
import numpy as np

D_MODEL = 1024
BATCH = 2
SEQ = 8192
DEPTH = 2

GRID_W = 64
NA_HEADS = 8
NA_HEAD_DIM = 64
NA_WIDTH = NA_HEADS * NA_HEAD_DIM
WIN_H_MAX = 8
WIN_W = 16
GLA_HEADS = 4
GLA_DK = 64
GLA_DV = 128
GLA_KEY_WIDTH = GLA_HEADS * GLA_DK
GLA_VAL_WIDTH = GLA_HEADS * GLA_DV
GLA_GATE_RANK = 16
GLA_GATE_NORMALIZER = 16.0
GLA_CHUNK = 64
MIX_WIDTH = NA_WIDTH + GLA_VAL_WIDTH
IN_SIZES = (NA_WIDTH, NA_WIDTH, NA_WIDTH, GLA_KEY_WIDTH, GLA_KEY_WIDTH, GLA_VAL_WIDTH, GLA_VAL_WIDTH, GLA_GATE_RANK, GLA_GATE_RANK)
IN_PROJ_WIDTH = sum(IN_SIZES)
N_GROUPS = 4
EXPERTS_PER_GROUP = 8
N_EXPERTS = N_GROUPS * EXPERTS_PER_GROUP
TOP_K_IN_GROUP = 2
D_FF_EXPERT = 512
RMS_EPS = 1e-6

kernel_name = 'hybrid_na_gla_hiermoe_encoder'


def rmsnorm(x, g):
    xf = x.astype(jnp.float32)
    y = xf * lax.rsqrt(jnp.mean(xf * xf, axis=-1, keepdims=True) + RMS_EPS)
    return (y * g.astype(jnp.float32)).astype(x.dtype)


def neighbourhood_attention(q, k, v, rpb):
    B, T, H, dh = q.shape
    rows = T // GRID_W
    kh = min(WIN_H_MAX, rows)
    r = jnp.arange(rows)
    row_idx = jnp.clip(r - kh // 2, 0, rows - kh)[:, None] + jnp.arange(kh)[None, :]
    c = jnp.arange(GRID_W)
    col_idx = jnp.clip(c - WIN_W // 2, 0, GRID_W - WIN_W)[:, None] + jnp.arange(WIN_W)[None, :]
    qg = q.reshape(B, rows, GRID_W, H, dh) * (dh ** -0.5)
    kg = k.reshape(B, rows, GRID_W, H, dh)
    vg = v.reshape(B, rows, GRID_W, H, dh)
    k_rows = kg[:, row_idx]
    v_rows = vg[:, row_idx]
    s_rows = jnp.einsum('brwhd,brixhd->bhrwix', qg, k_rows)
    sel = (col_idx[:, :, None] == jnp.arange(GRID_W)[None, None, :]).astype(q.dtype)
    s_win = jnp.einsum('bhrwix,wjx->bhrwij', s_rows, sel)
    roff = row_idx - r[:, None] + (WIN_H_MAX - 1)
    coff = col_idx - c[:, None] + (WIN_W - 1)
    bias = rpb[:, roff[:, None, :, None], coff[None, :, None, :]]
    logits = s_win.astype(jnp.float32) + bias[None].astype(jnp.float32)
    p = jax.nn.softmax(logits.reshape(B, H, rows, GRID_W, kh * WIN_W), axis=-1)
    p = p.reshape(B, H, rows, GRID_W, kh, WIN_W).astype(v.dtype)
    p_rows = jnp.einsum('bhrwij,wjx->bhrwix', p, sel)
    out = jnp.einsum('bhrwix,brixhd->brwhd', p_rows, v_rows)
    return out.reshape(B, T, H * dh)


def gla_chunked(q, k, v, g, strict):
    B, H, T, dk = q.shape
    dv = v.shape[-1]
    n = T // GLA_CHUNK
    q = q.reshape(B, H, n, GLA_CHUNK, dk)
    k = k.reshape(B, H, n, GLA_CHUNK, dk)
    v = v.reshape(B, H, n, GLA_CHUNK, dv)
    g = g.reshape(B, H, n, GLA_CHUNK, dk)
    b = jnp.cumsum(g, axis=3)
    b_ref = b[:, :, :, GLA_CHUNK // 2 - 1:GLA_CHUNK // 2, :]
    q_rel = q * jnp.exp(b - b_ref)
    k_rel = k * jnp.exp(b_ref - b)
    mask = jnp.tril(jnp.ones((GLA_CHUNK, GLA_CHUNK), dtype=bool), -1 if strict else 0)
    scores = jnp.where(mask, jnp.einsum('bhnik,bhnjk->bhnij', q_rel, k_rel), 0.0)
    o_intra = jnp.einsum('bhnij,bhnjv->bhniv', scores, v)
    b_last = b[:, :, :, -1:, :]
    chunk_state = jnp.einsum('bhnjk,bhnjv->bhnkv', k * jnp.exp(b_last - b), v)
    chunk_decay = jnp.exp(b_last[:, :, :, 0, :])

    def step(S, inp):
        dec, st = inp
        return dec[..., None] * S + st, S

    S0 = jnp.zeros((B, H, dk, dv), jnp.float32)
    _, S_prev = lax.scan(step, S0, (jnp.moveaxis(chunk_decay, 2, 0), jnp.moveaxis(chunk_state, 2, 0)))
    S_prev = jnp.moveaxis(S_prev, 0, 2)
    o_inter = jnp.einsum('bhnik,bhnkv->bhniv', q * jnp.exp(b), S_prev)
    return (o_intra + o_inter).reshape(B, H, T, dv)


def gla_mixer(q, k, v, r, glr_f, glr_b, w_g2_f, b_g_f, w_g2_b, b_g_b, norm_g):
    B, T, _ = q.shape
    f32 = jnp.float32
    heads = lambda a, d: a.astype(f32).reshape(B, T, GLA_HEADS, d).transpose(0, 2, 1, 3)
    qh = heads(q, GLA_DK) * (GLA_DK ** -0.5)
    kh = heads(k, GLA_DK)
    vh = heads(v, GLA_DV)
    g_f = heads(jax.nn.log_sigmoid(glr_f.astype(f32) @ w_g2_f.astype(f32) + b_g_f.astype(f32)) / GLA_GATE_NORMALIZER, GLA_DK)
    g_b = heads(jax.nn.log_sigmoid(glr_b.astype(f32) @ w_g2_b.astype(f32) + b_g_b.astype(f32)) / GLA_GATE_NORMALIZER, GLA_DK)
    o_fwd = gla_chunked(qh, kh, vh, g_f, strict=False)
    flip = lambda a: jnp.flip(a, axis=2)
    o_bwd = flip(gla_chunked(flip(qh), flip(kh), flip(vh), flip(g_b), strict=True))
    o = o_fwd + o_bwd
    o = o * lax.rsqrt(jnp.mean(o * o, axis=-1, keepdims=True) + RMS_EPS) * norm_g.astype(f32)
    o = o.transpose(0, 2, 1, 3).reshape(B, T, GLA_VAL_WIDTH).astype(r.dtype)
    return o * jax.nn.silu(r)


def hybrid_mixer(h, w_in, w_g2_f, b_g_f, w_g2_b, b_g_b, gla_norm_g, rpb, w_out):
    B, T, _ = h.shape
    proj = h @ w_in
    cuts = [int(c) for c in np.cumsum(IN_SIZES)[:-1]]
    na_q, na_k, na_v, g_q, g_k, g_v, g_r, g_lf, g_lb = jnp.split(proj, cuts, axis=-1)
    nh = lambda a: a.reshape(B, T, NA_HEADS, NA_HEAD_DIM)
    y_na = neighbourhood_attention(nh(na_q), nh(na_k), nh(na_v), rpb)
    y_gla = gla_mixer(g_q, g_k, g_v, g_r, g_lf, g_lb, w_g2_f, b_g_f, w_g2_b, b_g_b, gla_norm_g)
    return jnp.concatenate([y_na, y_gla], axis=-1) @ w_out


def hierarchical_moe(h, w_grp, b_grp, w_exp, b_exp, w_gate, w_up, w_down):
    B, T, D = h.shape
    x = h.reshape(B * T, D)
    f32 = jnp.float32
    grp_logits = (x @ w_grp + b_grp).astype(f32)
    grp_prob = jax.nn.softmax(grp_logits, axis=-1)
    g_sel = jnp.argmax(grp_logits, axis=-1)
    grp_w = jnp.take_along_axis(grp_prob, g_sel[:, None], axis=-1)
    exp_logits = (x @ w_exp + b_exp).astype(f32).reshape(B * T, N_GROUPS, EXPERTS_PER_GROUP)
    in_grp = jnp.take_along_axis(exp_logits, g_sel[:, None, None], axis=1)[:, 0]
    top_v, top_i = lax.top_k(in_grp, TOP_K_IN_GROUP)
    top_w = jax.nn.softmax(top_v, axis=-1) * grp_w
    expert_id = g_sel[:, None] * EXPERTS_PER_GROUP + top_i
    combine = jnp.sum(jax.nn.one_hot(expert_id, N_EXPERTS, dtype=f32) * top_w[..., None], axis=1).astype(x.dtype)
    y = jnp.zeros_like(x)
    for e in range(N_EXPERTS):
        hid = jax.nn.silu(x @ w_gate[e]) * (x @ w_up[e])
        y = y + combine[:, e:e + 1] * (hid @ w_down[e])
    return y.reshape(B, T, D)


def setup_inputs(seed: int = 0) -> dict:
    key = jax.random.key(seed)
    ks = jax.random.split(key, 20)
    nrm = lambda k, shape, scale: jax.random.normal(k, shape, jnp.float32) * scale
    L = DEPTH
    return {
        'x': nrm(ks[0], (BATCH, SEQ, D_MODEL), 1.0),
        'norm_mix_g': 1.0 + nrm(ks[1], (L, D_MODEL), 0.01),
        'w_in': nrm(ks[2], (L, D_MODEL, IN_PROJ_WIDTH), D_MODEL ** -0.5),
        'w_g2_f': nrm(ks[3], (L, GLA_GATE_RANK, GLA_KEY_WIDTH), GLA_GATE_RANK ** -0.5),
        'b_g_f': nrm(ks[4], (L, GLA_KEY_WIDTH), 0.1),
        'w_g2_b': nrm(ks[5], (L, GLA_GATE_RANK, GLA_KEY_WIDTH), GLA_GATE_RANK ** -0.5),
        'b_g_b': nrm(ks[6], (L, GLA_KEY_WIDTH), 0.1),
        'gla_norm_g': 1.0 + nrm(ks[7], (L, GLA_DV), 0.01),
        'rpb': nrm(ks[8], (L, NA_HEADS, 2 * WIN_H_MAX - 1, 2 * WIN_W - 1), 0.02),
        'w_out': nrm(ks[9], (L, MIX_WIDTH, D_MODEL), MIX_WIDTH ** -0.5),
        'norm_ffn_g': 1.0 + nrm(ks[10], (L, D_MODEL), 0.01),
        'w_grp': nrm(ks[11], (L, D_MODEL, N_GROUPS), D_MODEL ** -0.5),
        'b_grp': nrm(ks[12], (L, N_GROUPS), 0.01),
        'w_exp': nrm(ks[13], (L, D_MODEL, N_EXPERTS), D_MODEL ** -0.5),
        'b_exp': nrm(ks[14], (L, N_EXPERTS), 0.01),
        'w_gate': nrm(ks[15], (L, N_EXPERTS, D_MODEL, D_FF_EXPERT), D_MODEL ** -0.5),
        'w_up': nrm(ks[16], (L, N_EXPERTS, D_MODEL, D_FF_EXPERT), D_MODEL ** -0.5),
        'w_down': nrm(ks[17], (L, N_EXPERTS, D_FF_EXPERT, D_MODEL), D_FF_EXPERT ** -0.5),
        'final_norm_g': 1.0 + nrm(ks[18], (D_MODEL,), 0.01),
    }


def reference(x, norm_mix_g, w_in, w_g2_f, b_g_f, w_g2_b, b_g_b, gla_norm_g, rpb, w_out,
              norm_ffn_g, w_grp, b_grp, w_exp, b_exp, w_gate, w_up, w_down, final_norm_g):
    for l in range(DEPTH):
        h = rmsnorm(x, norm_mix_g[l])
        x = x + hybrid_mixer(h, w_in[l], w_g2_f[l], b_g_f[l], w_g2_b[l], b_g_b[l], gla_norm_g[l], rpb[l], w_out[l])
        h = rmsnorm(x, norm_ffn_g[l])
        x = x + hierarchical_moe(h, w_grp[l], b_grp[l], w_exp[l], b_exp[l], w_gate[l], w_up[l], w_down[l])
    return rmsnorm(x, final_norm_g)
```

```python
import functools

import jax
import jax.numpy as jnp
from jax import lax
from jax.experimental import pallas as pl
from jax.experimental.pallas import tpu as pltpu

F32 = jnp.float32
BF16 = jnp.bfloat16

GRID_W = 64
NA_HEADS = 8
NA_HEAD_DIM = 64
NA_WIDTH = NA_HEADS * NA_HEAD_DIM
WIN_H_MAX = 8
WIN_W = 16
GLA_HEADS = 4
GLA_DK = 64
GLA_DV = 128
GLA_KEY_WIDTH = GLA_HEADS * GLA_DK
GLA_VAL_WIDTH = GLA_HEADS * GLA_DV
GLA_GATE_RANK = 16
GLA_GATE_NORMALIZER = 16.0
GLA_CHUNK = 64
N_GROUPS = 4
EXPERTS_PER_GROUP = 8
N_EXPERTS = N_GROUPS * EXPERTS_PER_GROUP
RMS_EPS = 1e-6

LANES = 128
VMEM_LIMIT_BYTES = 56 * 1024 * 1024

MASK_VALUE = -1e30

IN_PROJ_ROWS = 512
NA_ROW_BLOCK = 8
GLA_STEP_CHUNKS = 4
OUT_ROWS = 512
EXPERT_ROWS = 256
COMBINE_ROWS = 256


def _params(*sem):
    return pltpu.CompilerParams(dimension_semantics=sem, vmem_limit_bytes=VMEM_LIMIT_BYTES)


def _rms(x, g):
    return x * lax.rsqrt(jnp.mean(x * x, axis=-1, keepdims=True) + RMS_EPS) * g


def _in_proj_kernel(x_ref, g_ref, w_na_ref, w_qk_ref, w_v_ref, w_r_ref, w_lr_ref,
                    w_g2_ref, b_g_ref, na_ref, qk_ref, v_ref, r_ref, gate_ref):
    h = _rms(x_ref[...], g_ref[...]).astype(BF16)
    na_ref[...] = jnp.dot(h, w_na_ref[...], preferred_element_type=F32).astype(BF16)
    qk_ref[...] = jnp.dot(h, w_qk_ref[...], preferred_element_type=F32)
    v_ref[...] = jnp.dot(h, w_v_ref[...], preferred_element_type=F32).astype(BF16)
    r_ref[...] = jnp.dot(h, w_r_ref[...], preferred_element_type=F32)
    lr = jnp.dot(h, w_lr_ref[...], preferred_element_type=F32)
    z = jnp.dot(lr, w_g2_ref[...], preferred_element_type=F32,
                precision=lax.Precision.HIGHEST) + b_g_ref[...]
    log_sig = jnp.minimum(z, 0.0) - jnp.log(1.0 + jnp.exp(-jnp.abs(z)))
    gate_ref[...] = log_sig * (1.0 / GLA_GATE_NORMALIZER)


def _in_proj(x, g, w_na, w_qk, w_v, w_r, w_lr, w_g2, b_g):
    n, d = x.shape
    rows = IN_PROJ_ROWS
    row_spec = lambda width: pl.BlockSpec((rows, width), lambda i: (i, 0))
    full = lambda a: pl.BlockSpec(a.shape, lambda i: (0,) * a.ndim)
    return pl.pallas_call(
        _in_proj_kernel,
        grid=(n // rows,),
        in_specs=[row_spec(d), full(g), full(w_na), full(w_qk), full(w_v), full(w_r), full(w_lr),
                  full(w_g2), full(b_g)],
        out_specs=[row_spec(3 * NA_WIDTH), row_spec(2 * GLA_KEY_WIDTH), row_spec(GLA_VAL_WIDTH),
                   row_spec(GLA_VAL_WIDTH), row_spec(2 * GLA_KEY_WIDTH)],
        out_shape=[jax.ShapeDtypeStruct((n, 3 * NA_WIDTH), BF16),
                   jax.ShapeDtypeStruct((n, 2 * GLA_KEY_WIDTH), F32),
                   jax.ShapeDtypeStruct((n, GLA_VAL_WIDTH), BF16),
                   jax.ShapeDtypeStruct((n, GLA_VAL_WIDTH), F32),
                   jax.ShapeDtypeStruct((n, 2 * GLA_KEY_WIDTH), F32)],
        compiler_params=_params("arbitrary"),
        name="in_proj",
    )(x, g, w_na, w_qk, w_v, w_r, w_lr, w_g2, b_g)


def _na_bias_table(rpb, rows):
    kh = min(WIN_H_MAX, rows)
    cls = jnp.arange(kh)[:, None, None, None]
    i = jnp.arange(kh)[None, :, None, None]
    w = jnp.arange(GRID_W)[None, None, :, None]
    x = jnp.arange(GRID_W)[None, None, None, :]
    cs = jnp.clip(w - WIN_W // 2, 0, GRID_W - WIN_W)
    valid = (x >= cs) & (x < cs + WIN_W)
    roff = jnp.broadcast_to(i - cls + (WIN_H_MAX - 1), (kh, kh, GRID_W, GRID_W))
    coff = jnp.broadcast_to(jnp.clip(x - w + (WIN_W - 1), 0, 2 * WIN_W - 2), (kh, kh, GRID_W, GRID_W))
    bias = rpb.astype(F32)[:, roff, coff]
    bias = jnp.where(valid[None], bias, MASK_VALUE)
    bias = bias.transpose(1, 0, 3, 2, 4).reshape(kh, NA_HEADS // 2, 2 * GRID_W, kh * GRID_W)
    return bias


def _na_kernel(q_ref, kp_ref, kc_ref, kn_ref, vp_ref, vc_ref, vn_ref, tbl_ref, o_ref,
               kbuf, vbuf, *, rows, kh):
    rb = NA_ROW_BLOCK
    j = pl.program_id(1)
    blk = rb * GRID_W
    for s, (k_src, v_src) in enumerate(((kp_ref, vp_ref), (kc_ref, vc_ref), (kn_ref, vn_ref))):
        kbuf[s * blk:(s + 1) * blk, :] = k_src[...].reshape(blk, NA_WIDTH)
        vbuf[s * blk:(s + 1) * blk, :] = v_src[...].reshape(blk, NA_WIDTH)
    lane = lax.broadcasted_iota(jnp.int32, (GRID_W, LANES), 1)
    first = lane < NA_HEAD_DIM

    def row_body(lr, carry):
        r = j * rb + lr
        start = jnp.clip(r - kh // 2, 0, rows - kh)
        cls = r - start
        local = pl.multiple_of((start - (j - 1) * rb) * GRID_W, GRID_W)
        q_row = q_ref[lr]
        k_win = kbuf[pl.ds(local, kh * GRID_W), :]
        v_win = vbuf[pl.ds(local, kh * GRID_W), :]
        outs = []
        for p in range(NA_HEADS // 2):
            sl = slice(p * LANES, (p + 1) * LANES)
            q_pair = q_row[:, sl]
            zero = jnp.zeros_like(q_pair)
            q_bd = jnp.concatenate([jnp.where(first, q_pair, zero), jnp.where(first, zero, q_pair)], axis=0)
            s = lax.dot_general(q_bd, k_win[:, sl], (((1,), (1,)), ((), ())), preferred_element_type=F32)
            s = s + tbl_ref[cls, p]
            m = jnp.max(s, axis=-1, keepdims=True)
            e = jnp.exp(s - m)
            denom = jnp.sum(e, axis=-1, keepdims=True)
            o = jnp.dot(e.astype(BF16), v_win[:, sl], preferred_element_type=F32) / denom
            outs.append(jnp.where(first, o[:GRID_W], o[GRID_W:]))
        o_ref[lr] = jnp.concatenate(outs, axis=-1).astype(o_ref.dtype)
        return carry

    lax.fori_loop(0, rb, row_body, 0)


def _na(na_qkv, tbl, batch, rows):
    kh = min(WIN_H_MAX, rows)
    rb = NA_ROW_BLOCK
    nblk = rows // rb
    x4 = na_qkv.reshape(batch, rows, GRID_W, 3 * NA_WIDTH)
    blk = (None, rb, GRID_W, NA_WIDTH)
    prev = lambda j: jnp.maximum(j - 1, 0)
    nxt = lambda j: jnp.minimum(j + 1, nblk - 1)
    specs = [pl.BlockSpec(blk, lambda b, j: (b, j, 0, 0))]
    for col in (1, 2):
        specs += [pl.BlockSpec(blk, lambda b, j, col=col: (b, prev(j), 0, col)),
                  pl.BlockSpec(blk, lambda b, j, col=col: (b, j, 0, col)),
                  pl.BlockSpec(blk, lambda b, j, col=col: (b, nxt(j), 0, col))]
    specs.append(pl.BlockSpec(tbl.shape, lambda b, j: (0, 0, 0, 0)))
    out = pl.pallas_call(
        functools.partial(_na_kernel, rows=rows, kh=kh),
        grid=(batch, nblk),
        in_specs=specs,
        out_specs=pl.BlockSpec(blk, lambda b, j: (b, j, 0, 0)),
        out_shape=jax.ShapeDtypeStruct((batch, rows, GRID_W, NA_WIDTH), BF16),
        scratch_shapes=[pltpu.VMEM((3 * rb * GRID_W, NA_WIDTH), BF16),
                        pltpu.VMEM((3 * rb * GRID_W, NA_WIDTH), BF16)],
        compiler_params=_params("arbitrary", "arbitrary"),
        name="na",
    )(x4, x4, x4, x4, x4, x4, x4, tbl)
    return out.reshape(batch * rows * GRID_W, NA_WIDTH)


def _block_diag_mask(row_block, col_block, nblocks):
    shape = (row_block * nblocks, col_block * nblocks)
    r = lax.shift_right_logical(lax.broadcasted_iota(jnp.int32, shape, 0), row_block.bit_length() - 1)
    c = lax.shift_right_logical(lax.broadcasted_iota(jnp.int32, shape, 1), col_block.bit_length() - 1)
    return r == c


def _gla_chunk(q, k, v, g, state_ref, *, backward):
    c = GLA_CHUNK
    ti = lax.broadcasted_iota(jnp.int32, (c, c), 0)
    tj = lax.broadcasted_iota(jnp.int32, (c, c), 1)
    if backward:
        cum = (tj >= ti).astype(F32)
        ref_row, last_row = c // 2, 0
    else:
        cum = (tj <= ti).astype(F32)
        ref_row, last_row = c // 2 - 1, c - 1
    b = jnp.dot(cum, g, preferred_element_type=F32, precision=lax.Precision.HIGHEST)
    b_ref = b[ref_row:ref_row + 1, :]
    b_last = b[last_row:last_row + 1, :]
    q_rel = (q * jnp.exp(b - b_ref)).astype(BF16)
    k_rel = (k * jnp.exp(b_ref - b)).astype(BF16)
    k_dec = (k * jnp.exp(b_last - b)).astype(BF16)
    q_dec = (q * jnp.exp(b)).astype(BF16)
    decay = jnp.exp(b_last)

    kk_mask = _block_diag_mask(c, GLA_DK, GLA_HEADS)
    k_bd = jnp.where(kk_mask, jnp.concatenate([k_rel] * GLA_HEADS, axis=0), jnp.zeros((), BF16))
    s = lax.dot_general(q_rel, k_bd, (((1,), (1,)), ((), ())), preferred_element_type=F32)
    si = lax.broadcasted_iota(jnp.int32, (c, c * GLA_HEADS), 0)
    sj = lax.broadcasted_iota(jnp.int32, (c, c * GLA_HEADS), 1) & (c - 1)
    keep = (sj > si) if backward else (sj <= si)
    p = jnp.where(keep, s, 0.0).astype(BF16)
    kv_mask = _block_diag_mask(c, GLA_DV, GLA_HEADS)
    v_bd = jnp.where(kv_mask, jnp.concatenate([v] * GLA_HEADS, axis=0), jnp.zeros((), BF16))
    o_intra = jnp.dot(p, v_bd, preferred_element_type=F32)

    state = state_ref[...]
    o_inter = lax.dot_general(q_dec, state.astype(BF16), (((1,), (1,)), ((), ())),
                              preferred_element_type=F32)
    vt = v.astype(F32).T.astype(BF16)
    upd = jnp.dot(vt, k_dec, preferred_element_type=F32)
    vk_mask = _block_diag_mask(GLA_DV, GLA_DK, GLA_HEADS)
    state_ref[...] = state * decay + jnp.where(vk_mask, upd, 0.0)
    return o_intra + o_inter


def _gla_kernel(qk_f_ref, v_f_ref, g_f_ref, qk_b_ref, v_b_ref, g_b_ref, o_f_ref, o_b_ref,
                st_f, st_b):
    @pl.when(pl.program_id(1) == 0)
    def _():
        st_f[...] = jnp.zeros_like(st_f)
        st_b[...] = jnp.zeros_like(st_b)

    c = GLA_CHUNK
    kw = GLA_KEY_WIDTH
    for n in range(GLA_STEP_CHUNKS):
        sl = slice(n * c, (n + 1) * c)
        o_f_ref[sl, :] = _gla_chunk(qk_f_ref[sl, :kw], qk_f_ref[sl, kw:], v_f_ref[sl, :], g_f_ref[sl, :],
                                    st_f, backward=False)
        m = GLA_STEP_CHUNKS - 1 - n
        sl = slice(m * c, (m + 1) * c)
        o_b_ref[sl, :] = _gla_chunk(qk_b_ref[sl, :kw], qk_b_ref[sl, kw:], v_b_ref[sl, :], g_b_ref[sl, :],
                                    st_b, backward=True)


def _gla(qk, v, gates, batch, seq):
    step = GLA_STEP_CHUNKS * GLA_CHUNK
    nblk = seq // step
    qk3 = qk.reshape(batch, seq, 2 * GLA_KEY_WIDTH)
    v3 = v.reshape(batch, seq, GLA_VAL_WIDTH)
    g3 = gates.reshape(batch, seq, 2 * GLA_KEY_WIDTH)
    fwd = lambda b, n: (b, n, 0)
    bwd = lambda b, n: (b, nblk - 1 - n, 0)
    bwd_gate = lambda b, n: (b, nblk - 1 - n, 1)
    o_f, o_b = pl.pallas_call(
        _gla_kernel,
        grid=(batch, nblk),
        in_specs=[pl.BlockSpec((None, step, 2 * GLA_KEY_WIDTH), fwd),
                  pl.BlockSpec((None, step, GLA_VAL_WIDTH), fwd),
                  pl.BlockSpec((None, step, GLA_KEY_WIDTH), fwd),
                  pl.BlockSpec((None, step, 2 * GLA_KEY_WIDTH), bwd),
                  pl.BlockSpec((None, step, GLA_VAL_WIDTH), bwd),
                  pl.BlockSpec((None, step, GLA_KEY_WIDTH), bwd_gate)],
        out_specs=[pl.BlockSpec((None, step, GLA_VAL_WIDTH), fwd),
                   pl.BlockSpec((None, step, GLA_VAL_WIDTH), bwd)],
        out_shape=[jax.ShapeDtypeStruct((batch, seq, GLA_VAL_WIDTH), F32)] * 2,
        scratch_shapes=[pltpu.VMEM((GLA_VAL_WIDTH, GLA_KEY_WIDTH), F32),
                        pltpu.VMEM((GLA_VAL_WIDTH, GLA_KEY_WIDTH), F32)],
        compiler_params=_params("arbitrary", "arbitrary"),
        name="gla",
    )(qk3, v3, g3, qk3, v3, g3)
    return o_f.reshape(batch * seq, GLA_VAL_WIDTH), o_b.reshape(batch * seq, GLA_VAL_WIDTH)


RT_E1, RT_E2, RT_RANK1, RT_RANK2, RT_W1, RT_W2 = range(6)
GROUP_LANE0 = N_EXPERTS


def _out_route_kernel(na_ref, of_ref, ob_ref, r_ref, x_ref, w_na_ref, w_gla_ref, gn_ref, fn_ref,
                      w_rt_ref, b_rt_ref, x1_ref, h2_ref, rt_ref, cnt_ref, carry):
    @pl.when(pl.program_id(0) == 0)
    def _():
        carry[...] = jnp.zeros_like(carry)

    rows = x_ref.shape[0]
    o = of_ref[...] + ob_ref[...]
    r = r_ref[...]
    parts = []
    for h in range(GLA_HEADS):
        sl = slice(h * GLA_DV, (h + 1) * GLA_DV)
        parts.append(_rms(o[:, sl], gn_ref[...]) * (r[:, sl] * jax.nn.sigmoid(r[:, sl])))
    y_gla = jnp.concatenate(parts, axis=-1).astype(BF16)
    x1 = (x_ref[...] + jnp.dot(na_ref[...], w_na_ref[...], preferred_element_type=F32)
          + jnp.dot(y_gla, w_gla_ref[...], preferred_element_type=F32))
    x1_ref[...] = x1
    h2 = _rms(x1, fn_ref[...])
    h2_ref[...] = h2

    logits = jnp.dot(h2, w_rt_ref[...], preferred_element_type=F32,
                     precision=lax.Precision.HIGHEST) + b_rt_ref[...]
    lane_i = lax.broadcasted_iota(jnp.int32, (rows, LANES), 1)
    lane = lane_i.astype(F32)
    lane_grp = lax.shift_right_logical(lane_i, EXPERTS_PER_GROUP.bit_length() - 1).astype(F32)
    neg = jnp.float32(-jnp.inf)
    no_lane = jnp.float32(LANES)
    is_grp = (lane_i >= GROUP_LANE0) & (lane_i < GROUP_LANE0 + N_GROUPS)
    g_logit = jnp.where(is_grp, logits, neg)
    g_max = jnp.max(g_logit, axis=-1, keepdims=True)
    g_sel = jnp.min(jnp.where(is_grp & (g_logit == g_max), lane, no_lane), axis=-1, keepdims=True) - GROUP_LANE0
    grp_w = 1.0 / jnp.sum(jnp.where(is_grp, jnp.exp(g_logit - g_max), 0.0), axis=-1, keepdims=True)
    in_grp = (lane_i < N_EXPERTS) & (lane_grp == g_sel)
    e_logit = jnp.where(in_grp, logits, neg)
    v1 = jnp.max(e_logit, axis=-1, keepdims=True)
    i1 = jnp.min(jnp.where(in_grp & (e_logit == v1), lane, no_lane), axis=-1, keepdims=True)
    rest = in_grp & (lane != i1)
    e_logit2 = jnp.where(rest, logits, neg)
    v2 = jnp.max(e_logit2, axis=-1, keepdims=True)
    i2 = jnp.min(jnp.where(rest & (e_logit2 == v2), lane, no_lane), axis=-1, keepdims=True)
    t = jnp.exp(v2 - v1)
    w1 = grp_w / (1.0 + t)
    w2 = grp_w * t / (1.0 + t)

    sel1 = lane == i1
    sel2 = lane == i2
    onehot = jnp.where(sel1 | sel2, 1.0, 0.0)
    ti = lax.broadcasted_iota(jnp.int32, (rows, rows), 0)
    tj = lax.broadcasted_iota(jnp.int32, (rows, rows), 1)
    before = jnp.where(tj < ti, 1.0, 0.0).astype(BF16)
    ranks = jnp.dot(before, onehot.astype(BF16), preferred_element_type=F32) + carry[0:1, :]
    rank1 = jnp.sum(jnp.where(sel1, ranks, 0.0), axis=-1, keepdims=True)
    rank2 = jnp.sum(jnp.where(sel2, ranks, 0.0), axis=-1, keepdims=True)
    new_carry = carry[0:1, :] + jnp.sum(onehot, axis=0, keepdims=True)
    carry[...] = jnp.broadcast_to(new_carry, carry.shape)
    cnt_ref[...] = jnp.broadcast_to(new_carry, cnt_ref.shape)

    rec = jnp.zeros((rows, LANES), F32)
    for idx, val in ((RT_E1, i1), (RT_E2, i2), (RT_RANK1, rank1), (RT_RANK2, rank2), (RT_W1, w1), (RT_W2, w2)):
        rec = jnp.where(lane_i == idx, val, rec)
    rt_ref[...] = rec


def _out_route(y_na, o_f, o_b, r, x, w_na, w_gla, gn, fn, w_rt, b_rt):
    n, d = x.shape
    rows = OUT_ROWS
    row_spec = lambda width: pl.BlockSpec((rows, width), lambda i: (i, 0))
    full = lambda a: pl.BlockSpec(a.shape, lambda i: (0,) * a.ndim)
    return pl.pallas_call(
        _out_route_kernel,
        grid=(n // rows,),
        in_specs=[row_spec(NA_WIDTH), row_spec(GLA_VAL_WIDTH), row_spec(GLA_VAL_WIDTH),
                  row_spec(GLA_VAL_WIDTH), row_spec(d), full(w_na), full(w_gla), full(gn), full(fn),
                  full(w_rt), full(b_rt)],
        out_specs=[row_spec(d), row_spec(d), row_spec(LANES), pl.BlockSpec((8, LANES), lambda i: (0, 0))],
        out_shape=[jax.ShapeDtypeStruct((n, d), F32), jax.ShapeDtypeStruct((n, d), F32),
                   jax.ShapeDtypeStruct((n, LANES), F32), jax.ShapeDtypeStruct((8, LANES), F32)],
        scratch_shapes=[pltpu.VMEM((8, LANES), F32)],
        compiler_params=_params("arbitrary"),
        name="out_route",
    )(y_na, o_f, o_b, r, x, w_na, w_gla, gn, fn, w_rt, b_rt)


def _gather_rows(src_hbm, dst, sem, idx_ref, base, count):
    def body(k, carry):
        tok = idx_ref[base + k]
        pltpu.make_async_copy(src_hbm.at[pl.ds(tok, 1), :], dst.at[pl.ds(k, 1), :], sem).start()
        return carry
    lax.fori_loop(0, count, body, 0)


def _wait_rows(src_hbm, dst, sem):
    pltpu.make_async_copy(src_hbm.at[pl.ds(0, dst.shape[0]), :], dst, sem).wait()


def _experts_kernel(tile_expert_ref, n_tiles_ref, src_ref, h_hbm, wg_ref, wu_ref, wd_ref, y_ref,
                    xbuf, sem):
    i = pl.program_id(0)
    rows = EXPERT_ROWS
    n_tiles = n_tiles_ref[0]
    slot = i % 2

    @pl.when(i == 0)
    def _():
        _gather_rows(h_hbm, xbuf.at[0], sem.at[0], src_ref, 0, rows)

    @pl.when(i + 1 < n_tiles)
    def _():
        _gather_rows(h_hbm, xbuf.at[1 - slot], sem.at[1 - slot], src_ref, (i + 1) * rows, rows)

    @pl.when(i < n_tiles)
    def _():
        _wait_rows(h_hbm, xbuf.at[slot], sem.at[slot])
        x = xbuf[slot].astype(BF16)
        gate = jnp.dot(x, wg_ref[...], preferred_element_type=F32)
        up = jnp.dot(x, wu_ref[...], preferred_element_type=F32)
        hid = (gate * jax.nn.sigmoid(gate) * up).astype(BF16)
        y_ref[...] = jnp.dot(hid, wd_ref[...], preferred_element_type=F32)

    @pl.when(i >= n_tiles)
    def _():
        y_ref[...] = jnp.zeros_like(y_ref)


def _experts(tile_expert, n_tiles, src_tok, h2, wg, wu, wd):
    n, d = h2.shape
    rows = EXPERT_ROWS
    max_tiles = tile_expert.shape[0]
    dff = wg.shape[-1]
    grid_spec = pltpu.PrefetchScalarGridSpec(
        num_scalar_prefetch=3,
        grid=(max_tiles,),
        in_specs=[pl.BlockSpec(memory_space=pl.ANY),
                  pl.BlockSpec((None, d, dff), lambda i, te, nt, src: (te[i], 0, 0)),
                  pl.BlockSpec((None, d, dff), lambda i, te, nt, src: (te[i], 0, 0)),
                  pl.BlockSpec((None, dff, d), lambda i, te, nt, src: (te[i], 0, 0))],
        out_specs=pl.BlockSpec((rows, d), lambda i, te, nt, src: (i, 0)),
        scratch_shapes=[pltpu.VMEM((2, rows, d), F32), pltpu.SemaphoreType.DMA((2,))],
    )
    return pl.pallas_call(
        _experts_kernel,
        grid_spec=grid_spec,
        out_shape=jax.ShapeDtypeStruct((max_tiles * rows, d), F32),
        compiler_params=_params("arbitrary"),
        name="experts",
    )(tile_expert, n_tiles, src_tok, h2, wg, wu, wd)


def _combine_kernel(pos_ref, ys_hbm, x1_ref, rt_ref, g_ref, o_ref, buf, sem, *, final_norm):
    i = pl.program_id(0)
    rows = COMBINE_ROWS
    n_steps = pl.num_programs(0)
    slot = i % 2

    def start(step, s):
        _gather_rows(ys_hbm, buf.at[s, 0], sem.at[s], pos_ref, step * 2 * rows, rows)
        _gather_rows(ys_hbm, buf.at[s, 1], sem.at[s], pos_ref, step * 2 * rows + rows, rows)

    @pl.when(i == 0)
    def _():
        start(0, 0)

    @pl.when(i + 1 < n_steps)
    def _():
        start(i + 1, 1 - slot)

    _wait_rows(ys_hbm, buf.at[slot, 0], sem.at[slot])
    _wait_rows(ys_hbm, buf.at[slot, 1], sem.at[slot])
    rt = rt_ref[...]
    lane = lax.broadcasted_iota(jnp.int32, rt.shape, 1)
    w1 = jnp.sum(jnp.where(lane == RT_W1, rt, 0.0), axis=-1, keepdims=True)
    w2 = jnp.sum(jnp.where(lane == RT_W2, rt, 0.0), axis=-1, keepdims=True)
    first_is_lower = (jnp.sum(jnp.where(lane == RT_E1, rt, 0.0), axis=-1, keepdims=True)
                      < jnp.sum(jnp.where(lane == RT_E2, rt, 0.0), axis=-1, keepdims=True))
    a = w1 * buf[slot, 0]
    b = w2 * buf[slot, 1]
    y = jnp.where(first_is_lower, a, b) + jnp.where(first_is_lower, b, a)
    x2 = x1_ref[...] + y
    if final_norm:
        x2 = _rms(x2, g_ref[...])
    o_ref[...] = x2


def _combine(pos, ys, x1, rt, g, final_norm):
    n, d = x1.shape
    rows = COMBINE_ROWS
    grid_spec = pltpu.PrefetchScalarGridSpec(
        num_scalar_prefetch=1,
        grid=(n // rows,),
        in_specs=[pl.BlockSpec(memory_space=pl.ANY),
                  pl.BlockSpec((rows, d), lambda i, pos: (i, 0)),
                  pl.BlockSpec((rows, LANES), lambda i, pos: (i, 0)),
                  pl.BlockSpec(g.shape, lambda i, pos: (0, 0))],
        out_specs=pl.BlockSpec((rows, d), lambda i, pos: (i, 0)),
        scratch_shapes=[pltpu.VMEM((2, 2, rows, d), F32), pltpu.SemaphoreType.DMA((2,))],
    )
    return pl.pallas_call(
        functools.partial(_combine_kernel, final_norm=final_norm),
        grid_spec=grid_spec,
        out_shape=jax.ShapeDtypeStruct((n, d), F32),
        compiler_params=_params("arbitrary"),
        name="combine",
    )(pos, ys, x1, rt, g)


def _dispatch_plan(rt, counts, n):
    rows = EXPERT_ROWS
    max_tiles = (2 * n) // rows + N_EXPERTS
    cnt = counts[0, :N_EXPERTS].astype(jnp.int32)
    tiles = (cnt + rows - 1) // rows
    tile_end = jnp.cumsum(tiles)
    offs = (tile_end - tiles) * rows
    n_tiles = tile_end[-1]
    e1 = rt[:, RT_E1].astype(jnp.int32)
    e2 = rt[:, RT_E2].astype(jnp.int32)
    pos1 = offs[e1] + rt[:, RT_RANK1].astype(jnp.int32)
    pos2 = offs[e2] + rt[:, RT_RANK2].astype(jnp.int32)
    tok = jnp.arange(n, dtype=jnp.int32)
    src_tok = jnp.zeros((max_tiles * rows,), jnp.int32).at[pos1].set(tok).at[pos2].set(tok)
    tile_ids = jnp.arange(max_tiles, dtype=jnp.int32)
    tile_expert = jnp.searchsorted(tile_end, jnp.minimum(tile_ids, n_tiles - 1), side="right")
    tile_expert = jnp.minimum(tile_expert, N_EXPERTS - 1).astype(jnp.int32)
    pos = jnp.stack([pos1.reshape(-1, COMBINE_ROWS), pos2.reshape(-1, COMBINE_ROWS)], axis=1).reshape(-1)
    return tile_expert, n_tiles.reshape(1).astype(jnp.int32), src_tok, pos


def _pad_lanes(a, width):
    return jnp.pad(a, ((0, 0), (0, width - a.shape[1])))


def kernel(x, norm_mix_g, w_in, w_g2_f, b_g_f, w_g2_b, b_g_b, gla_norm_g, rpb, w_out, norm_ffn_g, w_grp, b_grp,
           w_exp, b_exp, w_gate, w_up, w_down, final_norm_g):
    batch, seq, d = x.shape
    n = batch * seq
    rows = seq // GRID_W
    depth = w_in.shape[0]
    xf = x.reshape(n, d)
    c_na = 3 * NA_WIDTH
    c_qk = c_na + 2 * GLA_KEY_WIDTH
    c_v = c_qk + GLA_VAL_WIDTH
    c_r = c_v + GLA_VAL_WIDTH
    for l in range(depth):
        wl = w_in[l]
        q_scale = jnp.concatenate([jnp.full((NA_WIDTH,), NA_HEAD_DIM ** -0.5, F32),
                                   jnp.ones((2 * NA_WIDTH,), F32)])
        w_na = (wl[:, :c_na] * q_scale).astype(BF16)
        qk_scale = jnp.concatenate([jnp.full((GLA_KEY_WIDTH,), GLA_DK ** -0.5, F32),
                                    jnp.ones((GLA_KEY_WIDTH,), F32)])
        w_qk = (wl[:, c_na:c_qk] * qk_scale).astype(BF16)
        w_v = wl[:, c_qk:c_v].astype(BF16)
        w_r = wl[:, c_v:c_r].astype(BF16)
        w_lr = _pad_lanes(wl[:, c_r:], LANES).astype(BF16)
        w_g2 = jnp.zeros((LANES, 2 * GLA_KEY_WIDTH), F32)
        w_g2 = w_g2.at[:GLA_GATE_RANK, :GLA_KEY_WIDTH].set(w_g2_f[l])
        w_g2 = w_g2.at[GLA_GATE_RANK:2 * GLA_GATE_RANK, GLA_KEY_WIDTH:].set(w_g2_b[l])
        b_g = jnp.concatenate([b_g_f[l], b_g_b[l]])[None, :]
        na_qkv, gqk, gv, gr, gates = _in_proj(xf, norm_mix_g[l][None, :], w_na, w_qk, w_v, w_r, w_lr, w_g2, b_g)

        y_na = _na(na_qkv, _na_bias_table(rpb[l], rows), batch, rows)
        o_f, o_b = _gla(gqk, gv, gates, batch, seq)

        w_o = w_out[l].astype(BF16)
        w_rt = _pad_lanes(jnp.concatenate([w_exp[l], w_grp[l]], axis=1), LANES)
        b_rt = _pad_lanes(jnp.concatenate([b_exp[l], b_grp[l]])[None, :], LANES)
        x1, h2, rt, counts = _out_route(y_na, o_f, o_b, gr, xf, w_o[:NA_WIDTH], w_o[NA_WIDTH:],
                                        gla_norm_g[l][None, :], norm_ffn_g[l][None, :], w_rt, b_rt)

        tile_expert, n_tiles, src_tok, pos = _dispatch_plan(rt, counts, n)
        ys = _experts(tile_expert, n_tiles, src_tok, h2, w_gate[l].astype(BF16), w_up[l].astype(BF16),
                      w_down[l].astype(BF16))
        last = l == depth - 1
        xf = _combine(pos, ys, x1, rt, final_norm_g[None, :], final_norm=last)
    return xf.reshape(batch, seq, d)
```

```python
import functools

import jax
import jax.numpy as jnp
from jax import lax
from jax.experimental import pallas as pl
from jax.experimental.pallas import tpu as pltpu

F32 = jnp.float32
BF16 = jnp.bfloat16

GRID_W = 64
NA_HEADS = 8
NA_HEAD_DIM = 64
NA_WIDTH = NA_HEADS * NA_HEAD_DIM
WIN_H_MAX = 8
WIN_W = 16
GLA_HEADS = 4
GLA_DK = 64
GLA_DV = 128
GLA_KEY_WIDTH = GLA_HEADS * GLA_DK
GLA_VAL_WIDTH = GLA_HEADS * GLA_DV
GLA_GATE_RANK = 16
GLA_GATE_NORMALIZER = 16.0
GLA_CHUNK = 64
N_GROUPS = 4
EXPERTS_PER_GROUP = 8
N_EXPERTS = N_GROUPS * EXPERTS_PER_GROUP
RMS_EPS = 1e-6

LANES = 128
SUBLANES = 8
VMEM_LIMIT_BYTES = 56 * 1024 * 1024

MASK_VALUE = -1e30

IN_PROJ_ROWS = 512
NA_ROW_BLOCK = 8
GLA_STEP_CHUNKS = 4
OUT_ROWS = 512
EXPERT_ROWS = 256
COMBINE_ROWS = 256
GATHER_UNROLL = 8


def _params(*sem):
    return pltpu.CompilerParams(dimension_semantics=sem, vmem_limit_bytes=VMEM_LIMIT_BYTES)


def _rms(x, g):
    return x * lax.rsqrt(jnp.mean(x * x, axis=-1, keepdims=True) + RMS_EPS) * g


def _in_proj_kernel(x_ref, g_ref, w_na_ref, w_qk_ref, w_v_ref, w_r_ref, w_lr_ref,
                    w_g2_ref, b_g_ref, na_ref, qk_ref, v_ref, r_ref, gate_ref):
    h = _rms(x_ref[...], g_ref[...]).astype(BF16)
    na_ref[...] = jnp.dot(h, w_na_ref[...], preferred_element_type=F32).astype(BF16)
    qk_ref[...] = jnp.dot(h, w_qk_ref[...], preferred_element_type=F32)
    v_ref[...] = jnp.dot(h, w_v_ref[...], preferred_element_type=F32).astype(BF16)
    r_ref[...] = jnp.dot(h, w_r_ref[...], preferred_element_type=F32)
    lr = jnp.dot(h, w_lr_ref[...], preferred_element_type=F32)
    z = jnp.dot(lr, w_g2_ref[...], preferred_element_type=F32,
                precision=lax.Precision.HIGHEST) + b_g_ref[...]
    log_sig = jnp.minimum(z, 0.0) - jnp.log(1.0 + jnp.exp(-jnp.abs(z)))
    gate_ref[...] = log_sig * (1.0 / GLA_GATE_NORMALIZER)


def _in_proj(x, g, w_na, w_qk, w_v, w_r, w_lr, w_g2, b_g):
    n, d = x.shape
    rows = IN_PROJ_ROWS
    row_spec = lambda width: pl.BlockSpec((rows, width), lambda i: (i, 0))
    full = lambda a: pl.BlockSpec(a.shape, lambda i: (0,) * a.ndim)
    return pl.pallas_call(
        _in_proj_kernel,
        grid=(n // rows,),
        in_specs=[row_spec(d), full(g), full(w_na), full(w_qk), full(w_v), full(w_r), full(w_lr),
                  full(w_g2), full(b_g)],
        out_specs=[row_spec(3 * NA_WIDTH), row_spec(2 * GLA_KEY_WIDTH), row_spec(GLA_VAL_WIDTH),
                   row_spec(GLA_VAL_WIDTH), row_spec(2 * GLA_KEY_WIDTH)],
        out_shape=[jax.ShapeDtypeStruct((n, 3 * NA_WIDTH), BF16),
                   jax.ShapeDtypeStruct((n, 2 * GLA_KEY_WIDTH), F32),
                   jax.ShapeDtypeStruct((n, GLA_VAL_WIDTH), BF16),
                   jax.ShapeDtypeStruct((n, GLA_VAL_WIDTH), F32),
                   jax.ShapeDtypeStruct((n, 2 * GLA_KEY_WIDTH), F32)],
        compiler_params=_params("arbitrary"),
        name="in_proj",
    )(x, g, w_na, w_qk, w_v, w_r, w_lr, w_g2, b_g)


def _na_bias_table(rpb, rows):
    kh = min(WIN_H_MAX, rows)
    cls = jnp.arange(kh)[:, None, None]
    i = jnp.arange(kh)[None, :, None]
    w = jnp.arange(GRID_W)[:, None, None]
    x = jnp.arange(GRID_W)[None, :, None]
    cs = jnp.clip(w - WIN_W // 2, 0, GRID_W - WIN_W)
    valid = (x >= cs) & (x < cs + WIN_W)
    row_sel = (jnp.arange(2 * WIN_H_MAX - 1)[None, None, :] == i - cls + (WIN_H_MAX - 1)).astype(F32)
    col_sel = (valid & (jnp.arange(2 * WIN_W - 1)[None, None, :] == x - w + (WIN_W - 1))).astype(F32)
    hi = lax.Precision.HIGHEST
    by_col = jnp.einsum("hrc,wxc->hrwx", rpb.astype(F32), col_sel, precision=hi)
    bias = jnp.einsum("kir,hrwx->khwix", row_sel, by_col, precision=hi)
    bias = jnp.where(valid[None, None, :, None, :, 0], bias, MASK_VALUE)
    return bias.reshape(kh, NA_HEADS // 2, 2 * GRID_W, kh * GRID_W)


def _na_kernel(q_ref, kp_ref, kc_ref, kn_ref, vp_ref, vc_ref, vn_ref, tbl_ref, o_ref,
               kbuf, vbuf, *, rows, kh):
    rb = NA_ROW_BLOCK
    j = pl.program_id(1)
    blk = rb * GRID_W
    for s, (k_src, v_src) in enumerate(((kp_ref, vp_ref), (kc_ref, vc_ref), (kn_ref, vn_ref))):
        kbuf[s * blk:(s + 1) * blk, :] = k_src[...].reshape(blk, NA_WIDTH)
        vbuf[s * blk:(s + 1) * blk, :] = v_src[...].reshape(blk, NA_WIDTH)
    lane = lax.broadcasted_iota(jnp.int32, (GRID_W, LANES), 1)
    first = lane < NA_HEAD_DIM

    def row_body(lr, carry):
        r = j * rb + lr
        start = jnp.clip(r - kh // 2, 0, rows - kh)
        cls = r - start
        local = pl.multiple_of((start - (j - 1) * rb) * GRID_W, GRID_W)
        q_row = q_ref[lr]
        k_win = kbuf[pl.ds(local, kh * GRID_W), :]
        v_win = vbuf[pl.ds(local, kh * GRID_W), :]
        outs = []
        for p in range(NA_HEADS // 2):
            sl = slice(p * LANES, (p + 1) * LANES)
            q_pair = q_row[:, sl]
            zero = jnp.zeros_like(q_pair)
            q_bd = jnp.concatenate([jnp.where(first, q_pair, zero), jnp.where(first, zero, q_pair)], axis=0)
            s = lax.dot_general(q_bd, k_win[:, sl], (((1,), (1,)), ((), ())), preferred_element_type=F32)
            s = s + tbl_ref[cls, p]
            m = jnp.max(s, axis=-1, keepdims=True)
            e = jnp.exp(s - m)
            denom = jnp.sum(e, axis=-1, keepdims=True)
            o = jnp.dot(e.astype(BF16), v_win[:, sl], preferred_element_type=F32) / denom
            outs.append(jnp.where(first, o[:GRID_W], o[GRID_W:]))
        o_ref[lr] = jnp.concatenate(outs, axis=-1).astype(o_ref.dtype)
        return carry

    lax.fori_loop(0, rb, row_body, 0)


def _na(na_qkv, tbl, batch, rows):
    kh = min(WIN_H_MAX, rows)
    rb = NA_ROW_BLOCK
    nblk = rows // rb
    x4 = na_qkv.reshape(batch, rows, GRID_W, 3 * NA_WIDTH)
    blk = (None, rb, GRID_W, NA_WIDTH)
    prev = lambda j: jnp.maximum(j - 1, 0)
    nxt = lambda j: jnp.minimum(j + 1, nblk - 1)
    specs = [pl.BlockSpec(blk, lambda b, j: (b, j, 0, 0))]
    for col in (1, 2):
        specs += [pl.BlockSpec(blk, lambda b, j, col=col: (b, prev(j), 0, col)),
                  pl.BlockSpec(blk, lambda b, j, col=col: (b, j, 0, col)),
                  pl.BlockSpec(blk, lambda b, j, col=col: (b, nxt(j), 0, col))]
    specs.append(pl.BlockSpec(tbl.shape, lambda b, j: (0, 0, 0, 0)))
    out = pl.pallas_call(
        functools.partial(_na_kernel, rows=rows, kh=kh),
        grid=(batch, nblk),
        in_specs=specs,
        out_specs=pl.BlockSpec(blk, lambda b, j: (b, j, 0, 0)),
        out_shape=jax.ShapeDtypeStruct((batch, rows, GRID_W, NA_WIDTH), BF16),
        scratch_shapes=[pltpu.VMEM((3 * rb * GRID_W, NA_WIDTH), BF16),
                        pltpu.VMEM((3 * rb * GRID_W, NA_WIDTH), BF16)],
        compiler_params=_params("arbitrary", "arbitrary"),
        name="na",
    )(x4, x4, x4, x4, x4, x4, x4, tbl)
    return out.reshape(batch * rows * GRID_W, NA_WIDTH)


def _block_diag_mask(row_block, col_block, nblocks):
    shape = (row_block * nblocks, col_block * nblocks)
    r = lax.shift_right_logical(lax.broadcasted_iota(jnp.int32, shape, 0), row_block.bit_length() - 1)
    c = lax.shift_right_logical(lax.broadcasted_iota(jnp.int32, shape, 1), col_block.bit_length() - 1)
    return r == c


def _gla_chunk(q, k, v, g, state_ref, *, backward):
    c = GLA_CHUNK
    ti = lax.broadcasted_iota(jnp.int32, (c, c), 0)
    tj = lax.broadcasted_iota(jnp.int32, (c, c), 1)
    if backward:
        cum = (tj >= ti).astype(F32)
        ref_row, last_row = c // 2, 0
    else:
        cum = (tj <= ti).astype(F32)
        ref_row, last_row = c // 2 - 1, c - 1
    b = jnp.dot(cum, g, preferred_element_type=F32, precision=lax.Precision.HIGHEST)
    b_ref = b[ref_row:ref_row + 1, :]
    b_last = b[last_row:last_row + 1, :]
    q_rel = (q * jnp.exp(b - b_ref)).astype(BF16)
    k_rel = (k * jnp.exp(b_ref - b)).astype(BF16)
    k_dec = (k * jnp.exp(b_last - b)).astype(BF16)
    q_dec = (q * jnp.exp(b)).astype(BF16)
    decay = jnp.exp(b_last)

    kk_mask = _block_diag_mask(c, GLA_DK, GLA_HEADS)
    k_bd = jnp.where(kk_mask, jnp.concatenate([k_rel] * GLA_HEADS, axis=0), jnp.zeros((), BF16))
    s = lax.dot_general(q_rel, k_bd, (((1,), (1,)), ((), ())), preferred_element_type=F32)
    si = lax.broadcasted_iota(jnp.int32, (c, c * GLA_HEADS), 0)
    sj = lax.broadcasted_iota(jnp.int32, (c, c * GLA_HEADS), 1) & (c - 1)
    keep = (sj > si) if backward else (sj <= si)
    p = jnp.where(keep, s, 0.0).astype(BF16)
    kv_mask = _block_diag_mask(c, GLA_DV, GLA_HEADS)
    v_bd = jnp.where(kv_mask, jnp.concatenate([v] * GLA_HEADS, axis=0), jnp.zeros((), BF16))
    o_intra = jnp.dot(p, v_bd, preferred_element_type=F32)

    state = state_ref[...]
    o_inter = lax.dot_general(q_dec, state.astype(BF16), (((1,), (1,)), ((), ())),
                              preferred_element_type=F32)
    vt = v.astype(F32).T.astype(BF16)
    upd = jnp.dot(vt, k_dec, preferred_element_type=F32)
    vk_mask = _block_diag_mask(GLA_DV, GLA_DK, GLA_HEADS)
    state_ref[...] = state * decay + jnp.where(vk_mask, upd, 0.0)
    return o_intra + o_inter


def _gla_kernel(qk_f_ref, v_f_ref, g_f_ref, qk_b_ref, v_b_ref, g_b_ref, o_f_ref, o_b_ref,
                st_f, st_b):
    @pl.when(pl.program_id(1) == 0)
    def _():
        st_f[...] = jnp.zeros_like(st_f)
        st_b[...] = jnp.zeros_like(st_b)

    c = GLA_CHUNK
    kw = GLA_KEY_WIDTH
    for n in range(GLA_STEP_CHUNKS):
        sl = slice(n * c, (n + 1) * c)
        o_f_ref[sl, :] = _gla_chunk(qk_f_ref[sl, :kw], qk_f_ref[sl, kw:], v_f_ref[sl, :], g_f_ref[sl, :],
                                    st_f, backward=False)
        m = GLA_STEP_CHUNKS - 1 - n
        sl = slice(m * c, (m + 1) * c)
        o_b_ref[sl, :] = _gla_chunk(qk_b_ref[sl, :kw], qk_b_ref[sl, kw:], v_b_ref[sl, :], g_b_ref[sl, :],
                                    st_b, backward=True)


def _gla(qk, v, gates, batch, seq):
    step = GLA_STEP_CHUNKS * GLA_CHUNK
    nblk = seq // step
    qk3 = qk.reshape(batch, seq, 2 * GLA_KEY_WIDTH)
    v3 = v.reshape(batch, seq, GLA_VAL_WIDTH)
    g3 = gates.reshape(batch, seq, 2 * GLA_KEY_WIDTH)
    fwd = lambda b, n: (b, n, 0)
    bwd = lambda b, n: (b, nblk - 1 - n, 0)
    bwd_gate = lambda b, n: (b, nblk - 1 - n, 1)
    o_f, o_b = pl.pallas_call(
        _gla_kernel,
        grid=(batch, nblk),
        in_specs=[pl.BlockSpec((None, step, 2 * GLA_KEY_WIDTH), fwd),
                  pl.BlockSpec((None, step, GLA_VAL_WIDTH), fwd),
                  pl.BlockSpec((None, step, GLA_KEY_WIDTH), fwd),
                  pl.BlockSpec((None, step, 2 * GLA_KEY_WIDTH), bwd),
                  pl.BlockSpec((None, step, GLA_VAL_WIDTH), bwd),
                  pl.BlockSpec((None, step, GLA_KEY_WIDTH), bwd_gate)],
        out_specs=[pl.BlockSpec((None, step, GLA_VAL_WIDTH), fwd),
                   pl.BlockSpec((None, step, GLA_VAL_WIDTH), bwd)],
        out_shape=[jax.ShapeDtypeStruct((batch, seq, GLA_VAL_WIDTH), F32)] * 2,
        scratch_shapes=[pltpu.VMEM((GLA_VAL_WIDTH, GLA_KEY_WIDTH), F32),
                        pltpu.VMEM((GLA_VAL_WIDTH, GLA_KEY_WIDTH), F32)],
        compiler_params=_params("arbitrary", "arbitrary"),
        name="gla",
    )(qk3, v3, g3, qk3, v3, g3)
    return o_f.reshape(batch * seq, GLA_VAL_WIDTH), o_b.reshape(batch * seq, GLA_VAL_WIDTH)


def _store_slabs(ref, x):
    rows = x.shape[0]
    for s in range(SUBLANES):
        ref[pl.ds(s, rows, stride=SUBLANES), :] = x[:, s * LANES:(s + 1) * LANES]


def _load_slabs(ref, rows):
    return jnp.concatenate([ref[pl.ds(s, rows, stride=SUBLANES), :] for s in range(SUBLANES)], axis=1)


def _slab_copy(src_hbm, dst, sem, src_row, k):
    dst_row = k * SUBLANES if isinstance(k, int) else pl.multiple_of(k * SUBLANES, SUBLANES)
    return pltpu.make_async_copy(src_hbm.at[pl.ds(pl.multiple_of(src_row, SUBLANES), SUBLANES), :],
                                 dst.at[pl.ds(dst_row, SUBLANES), :], sem)


def _wait_slabs(src_hbm, dst, sem):
    pltpu.make_async_copy(src_hbm.at[pl.ds(0, dst.shape[0]), :], dst, sem).wait()


RT_E1, RT_E2, RT_RANK1, RT_RANK2, RT_W1, RT_W2 = range(6)
GROUP_LANE0 = N_EXPERTS


def _out_route_kernel(na_ref, of_ref, ob_ref, r_ref, x_ref, w_na_ref, w_gla_ref, gn_ref, fn_ref,
                      w_rt_ref, b_rt_ref, x1_ref, h2_ref, rt_ref, cnt_ref, carry):
    @pl.when(pl.program_id(0) == 0)
    def _():
        carry[...] = jnp.zeros_like(carry)

    rows = x_ref.shape[0]
    o = of_ref[...] + ob_ref[...]
    r = r_ref[...]
    parts = []
    for h in range(GLA_HEADS):
        sl = slice(h * GLA_DV, (h + 1) * GLA_DV)
        parts.append(_rms(o[:, sl], gn_ref[...]) * (r[:, sl] * jax.nn.sigmoid(r[:, sl])))
    y_gla = jnp.concatenate(parts, axis=-1).astype(BF16)
    x1 = (x_ref[...] + jnp.dot(na_ref[...], w_na_ref[...], preferred_element_type=F32)
          + jnp.dot(y_gla, w_gla_ref[...], preferred_element_type=F32))
    x1_ref[...] = x1
    h2 = _rms(x1, fn_ref[...])
    _store_slabs(h2_ref, h2)

    h_hi = h2.astype(BF16)
    h_lo = (h2 - h_hi.astype(F32)).astype(BF16)
    logits = jnp.dot(jnp.concatenate([h_hi, h_lo, h_hi], axis=1), w_rt_ref[...],
                     preferred_element_type=F32) + b_rt_ref[...]
    lane_i = lax.broadcasted_iota(jnp.int32, (rows, LANES), 1)
    lane = lane_i.astype(F32)
    lane_grp = lax.shift_right_logical(lane_i, EXPERTS_PER_GROUP.bit_length() - 1).astype(F32)
    neg = jnp.float32(-jnp.inf)
    no_lane = jnp.float32(LANES)
    is_grp = (lane_i >= GROUP_LANE0) & (lane_i < GROUP_LANE0 + N_GROUPS)
    g_logit = jnp.where(is_grp, logits, neg)
    g_max = jnp.max(g_logit, axis=-1, keepdims=True)
    g_sel = jnp.min(jnp.where(is_grp & (g_logit == g_max), lane, no_lane), axis=-1, keepdims=True) - GROUP_LANE0
    grp_w = 1.0 / jnp.sum(jnp.where(is_grp, jnp.exp(g_logit - g_max), 0.0), axis=-1, keepdims=True)
    in_grp = (lane_i < N_EXPERTS) & (lane_grp == g_sel)
    e_logit = jnp.where(in_grp, logits, neg)
    v1 = jnp.max(e_logit, axis=-1, keepdims=True)
    i1 = jnp.min(jnp.where(in_grp & (e_logit == v1), lane, no_lane), axis=-1, keepdims=True)
    rest = in_grp & (lane != i1)
    e_logit2 = jnp.where(rest, logits, neg)
    v2 = jnp.max(e_logit2, axis=-1, keepdims=True)
    i2 = jnp.min(jnp.where(rest & (e_logit2 == v2), lane, no_lane), axis=-1, keepdims=True)
    t = jnp.exp(v2 - v1)
    w1 = grp_w / (1.0 + t)
    w2 = grp_w * t / (1.0 + t)

    sel1 = lane == i1
    sel2 = lane == i2
    onehot = jnp.where(sel1 | sel2, 1.0, 0.0)
    ti = lax.broadcasted_iota(jnp.int32, (rows, rows), 0)
    tj = lax.broadcasted_iota(jnp.int32, (rows, rows), 1)
    before = jnp.where(tj < ti, 1.0, 0.0).astype(BF16)
    ranks = jnp.dot(before, onehot.astype(BF16), preferred_element_type=F32) + carry[0:1, :]
    rank1 = jnp.sum(jnp.where(sel1, ranks, 0.0), axis=-1, keepdims=True)
    rank2 = jnp.sum(jnp.where(sel2, ranks, 0.0), axis=-1, keepdims=True)
    new_carry = carry[0:1, :] + jnp.sum(onehot, axis=0, keepdims=True)
    carry[...] = jnp.broadcast_to(new_carry, carry.shape)
    cnt_ref[...] = jnp.broadcast_to(new_carry, cnt_ref.shape)

    rec = jnp.zeros((rows, LANES), F32)
    for idx, val in ((RT_E1, i1), (RT_E2, i2), (RT_RANK1, rank1), (RT_RANK2, rank2), (RT_W1, w1), (RT_W2, w2)):
        rec = jnp.where(lane_i == idx, val, rec)
    rt_ref[...] = rec


def _out_route(y_na, o_f, o_b, r, x, w_na, w_gla, gn, fn, w_rt, b_rt):
    n, d = x.shape
    rows = OUT_ROWS
    row_spec = lambda width: pl.BlockSpec((rows, width), lambda i: (i, 0))
    full = lambda a: pl.BlockSpec(a.shape, lambda i: (0,) * a.ndim)
    return pl.pallas_call(
        _out_route_kernel,
        grid=(n // rows,),
        in_specs=[row_spec(NA_WIDTH), row_spec(GLA_VAL_WIDTH), row_spec(GLA_VAL_WIDTH),
                  row_spec(GLA_VAL_WIDTH), row_spec(d), full(w_na), full(w_gla), full(gn), full(fn),
                  full(w_rt), full(b_rt)],
        out_specs=[row_spec(d), pl.BlockSpec((rows * SUBLANES, LANES), lambda i: (i, 0)), row_spec(LANES),
                   pl.BlockSpec((SUBLANES, LANES), lambda i: (0, 0))],
        out_shape=[jax.ShapeDtypeStruct((n, d), F32), jax.ShapeDtypeStruct((n * SUBLANES, LANES), F32),
                   jax.ShapeDtypeStruct((n, LANES), F32), jax.ShapeDtypeStruct((SUBLANES, LANES), F32)],
        scratch_shapes=[pltpu.VMEM((8, LANES), F32)],
        compiler_params=_params("arbitrary"),
        name="out_route",
    )(y_na, o_f, o_b, r, x, w_na, w_gla, gn, fn, w_rt, b_rt)


def _start_slab_gather(src_hbm, dst, sem, row_ref, base, count, *, unrolled):
    if unrolled:
        for k in range(count):
            _slab_copy(src_hbm, dst, sem, row_ref[base + k], k).start()
    else:
        def body(k, carry):
            _slab_copy(src_hbm, dst, sem, row_ref[base + k], k).start()
            return carry
        lax.fori_loop(0, count, body, 0, unroll=GATHER_UNROLL)


def _experts_kernel(tile_expert_ref, n_tiles_ref, src_ref, h_hbm, wg_ref, wu_ref, wd_ref, y_ref,
                    xbuf, wg_s, wu_s, wd_s, sem):
    i = pl.program_id(0)
    rows = EXPERT_ROWS
    n_tiles = n_tiles_ref[0]
    slot = i % 2

    @pl.when(i == 0)
    def _():
        _start_slab_gather(h_hbm, xbuf.at[0], sem.at[0], src_ref, 0, rows, unrolled=False)

    @pl.when(i < n_tiles)
    def _():
        @pl.when((i == 0) | (tile_expert_ref[i] != tile_expert_ref[jnp.maximum(i - 1, 0)]))
        def _():
            wg_s[...] = wg_ref[...].astype(BF16)
            wu_s[...] = wu_ref[...].astype(BF16)
            wd_s[...] = wd_ref[...].astype(BF16)

        _wait_slabs(h_hbm, xbuf.at[slot], sem.at[slot])
        _start_slab_gather(h_hbm, xbuf.at[1 - slot], sem.at[1 - slot], src_ref, (i + 1) * rows, rows,
                           unrolled=True)
        x = _load_slabs(xbuf.at[slot], rows).astype(BF16)
        gate = jnp.dot(x, wg_s[...], preferred_element_type=F32)
        up = jnp.dot(x, wu_s[...], preferred_element_type=F32)
        hid = (gate * jax.nn.sigmoid(gate) * up).astype(BF16)
        _store_slabs(y_ref, jnp.dot(hid, wd_s[...], preferred_element_type=F32))

    @pl.when(i == n_tiles)
    def _():
        _wait_slabs(h_hbm, xbuf.at[slot], sem.at[slot])

    @pl.when((i >= n_tiles) & (i < pl.num_programs(0) - 1))
    def _():
        y_ref[...] = jnp.zeros_like(y_ref)


def _experts(tile_expert, n_tiles, src_rows, h2_slabs, wg, wu, wd):
    rows = EXPERT_ROWS
    steps = tile_expert.shape[0]
    _, d, dff = wg.shape
    assert d == SUBLANES * LANES
    grid_spec = pltpu.PrefetchScalarGridSpec(
        num_scalar_prefetch=3,
        grid=(steps,),
        in_specs=[pl.BlockSpec(memory_space=pl.ANY),
                  pl.BlockSpec((None, d, dff), lambda i, te, nt, src: (te[i], 0, 0)),
                  pl.BlockSpec((None, d, dff), lambda i, te, nt, src: (te[i], 0, 0)),
                  pl.BlockSpec((None, dff, d), lambda i, te, nt, src: (te[i], 0, 0))],
        out_specs=pl.BlockSpec((rows * SUBLANES, LANES),
                               lambda i, te, nt, src: (jnp.minimum(i, steps - 2), 0)),
        scratch_shapes=[pltpu.VMEM((2, rows * SUBLANES, LANES), F32),
                        pltpu.VMEM((d, dff), BF16), pltpu.VMEM((d, dff), BF16), pltpu.VMEM((dff, d), BF16),
                        pltpu.SemaphoreType.DMA((2,))],
    )
    return pl.pallas_call(
        _experts_kernel,
        grid_spec=grid_spec,
        out_shape=jax.ShapeDtypeStruct(((steps - 1) * rows * SUBLANES, LANES), F32),
        compiler_params=_params("arbitrary"),
        name="experts",
    )(tile_expert, n_tiles, src_rows, h2_slabs, wg, wu, wd)


def _combine_kernel(pos_ref, ys_hbm, x1_ref, rt_ref, g_ref, o_ref, buf, sem, *, final_norm):
    i = pl.program_id(0)
    rows = COMBINE_ROWS
    n_steps = pl.num_programs(0)
    slot = i % 2

    def start(step, s):
        for j in range(2):
            _start_slab_gather(ys_hbm, buf.at[s, j], sem.at[s], pos_ref, (step * 2 + j) * rows, rows,
                               unrolled=False)

    @pl.when(i == 0)
    def _():
        start(0, 0)

    @pl.when(i + 1 < n_steps)
    def _():
        start(i + 1, 1 - slot)

    _wait_slabs(ys_hbm, buf.at[slot, 0], sem.at[slot])
    _wait_slabs(ys_hbm, buf.at[slot, 1], sem.at[slot])
    rt = rt_ref[...]
    lane = lax.broadcasted_iota(jnp.int32, rt.shape, 1)
    w1 = jnp.sum(jnp.where(lane == RT_W1, rt, 0.0), axis=-1, keepdims=True)
    w2 = jnp.sum(jnp.where(lane == RT_W2, rt, 0.0), axis=-1, keepdims=True)
    y = w1 * _load_slabs(buf.at[slot, 0], rows) + w2 * _load_slabs(buf.at[slot, 1], rows)
    x2 = x1_ref[...] + y
    if final_norm:
        x2 = _rms(x2, g_ref[...])
    o_ref[...] = x2


def _combine(pos, ys, x1, rt, g, final_norm):
    n, d = x1.shape
    rows = COMBINE_ROWS
    grid_spec = pltpu.PrefetchScalarGridSpec(
        num_scalar_prefetch=1,
        grid=(n // rows,),
        in_specs=[pl.BlockSpec(memory_space=pl.ANY),
                  pl.BlockSpec((rows, d), lambda i, pos: (i, 0)),
                  pl.BlockSpec((rows, LANES), lambda i, pos: (i, 0)),
                  pl.BlockSpec(g.shape, lambda i, pos: (0, 0))],
        out_specs=pl.BlockSpec((rows, d), lambda i, pos: (i, 0)),
        scratch_shapes=[pltpu.VMEM((2, 2, rows * SUBLANES, LANES), F32), pltpu.SemaphoreType.DMA((2,))],
    )
    return pl.pallas_call(
        functools.partial(_combine_kernel, final_norm=final_norm),
        grid_spec=grid_spec,
        out_shape=jax.ShapeDtypeStruct((n, d), F32),
        compiler_params=_params("arbitrary"),
        name="combine",
    )(pos, ys, x1, rt, g)


def _dispatch_plan(rt, counts, n):
    rows = EXPERT_ROWS
    steps = (2 * n) // rows + N_EXPERTS + 1
    cnt = counts[0, :N_EXPERTS].astype(jnp.int32)
    tiles = (cnt + rows - 1) // rows
    tile_end = jnp.cumsum(tiles)
    offs = (tile_end - tiles) * rows
    n_tiles = tile_end[-1]
    e1 = rt[:, RT_E1].astype(jnp.int32)
    e2 = rt[:, RT_E2].astype(jnp.int32)
    pos1 = offs[e1] + rt[:, RT_RANK1].astype(jnp.int32)
    pos2 = offs[e2] + rt[:, RT_RANK2].astype(jnp.int32)
    tok_row = jnp.arange(n, dtype=jnp.int32) * SUBLANES
    src_rows = jnp.zeros((steps * rows,), jnp.int32).at[pos1].set(tok_row).at[pos2].set(tok_row)
    step_ids = jnp.arange(steps, dtype=jnp.int32)
    tile_expert = jnp.sum(tile_end[None, :] <= jnp.minimum(step_ids, n_tiles - 1)[:, None], axis=1)
    tile_expert = jnp.minimum(tile_expert, N_EXPERTS - 1).astype(jnp.int32)
    pos = jnp.stack([pos1.reshape(-1, COMBINE_ROWS), pos2.reshape(-1, COMBINE_ROWS)], axis=1).reshape(-1)
    return tile_expert, n_tiles.reshape(1).astype(jnp.int32), src_rows, pos * SUBLANES


def _pad_lanes(a, width):
    return jnp.pad(a, ((0, 0), (0, width - a.shape[1])))


def kernel(x, norm_mix_g, w_in, w_g2_f, b_g_f, w_g2_b, b_g_b, gla_norm_g, rpb, w_out, norm_ffn_g, w_grp, b_grp,
           w_exp, b_exp, w_gate, w_up, w_down, final_norm_g):
    batch, seq, d = x.shape
    n = batch * seq
    rows = seq // GRID_W
    depth = w_in.shape[0]
    xf = x.reshape(n, d)
    c_na = 3 * NA_WIDTH
    c_qk = c_na + 2 * GLA_KEY_WIDTH
    c_v = c_qk + GLA_VAL_WIDTH
    c_r = c_v + GLA_VAL_WIDTH
    for l in range(depth):
        wl = w_in[l]
        q_scale = jnp.concatenate([jnp.full((NA_WIDTH,), NA_HEAD_DIM ** -0.5, F32),
                                   jnp.ones((2 * NA_WIDTH,), F32)])
        w_na = (wl[:, :c_na] * q_scale).astype(BF16)
        qk_scale = jnp.concatenate([jnp.full((GLA_KEY_WIDTH,), GLA_DK ** -0.5, F32),
                                    jnp.ones((GLA_KEY_WIDTH,), F32)])
        w_qk = (wl[:, c_na:c_qk] * qk_scale).astype(BF16)
        w_v = wl[:, c_qk:c_v].astype(BF16)
        w_r = wl[:, c_v:c_r].astype(BF16)
        w_lr = _pad_lanes(wl[:, c_r:], LANES).astype(BF16)
        w_g2 = jnp.zeros((LANES, 2 * GLA_KEY_WIDTH), F32)
        w_g2 = w_g2.at[:GLA_GATE_RANK, :GLA_KEY_WIDTH].set(w_g2_f[l])
        w_g2 = w_g2.at[GLA_GATE_RANK:2 * GLA_GATE_RANK, GLA_KEY_WIDTH:].set(w_g2_b[l])
        b_g = jnp.concatenate([b_g_f[l], b_g_b[l]])[None, :]
        na_qkv, gqk, gv, gr, gates = _in_proj(xf, norm_mix_g[l][None, :], w_na, w_qk, w_v, w_r, w_lr, w_g2, b_g)

        y_na = _na(na_qkv, _na_bias_table(rpb[l], rows), batch, rows)
        o_f, o_b = _gla(gqk, gv, gates, batch, seq)

        w_o = w_out[l].astype(BF16)
        w_rt = _pad_lanes(jnp.concatenate([w_exp[l], w_grp[l]], axis=1), LANES)
        w_rt_hi = w_rt.astype(BF16)
        w_rt_lo = (w_rt - w_rt_hi.astype(F32)).astype(BF16)
        w_rt3 = jnp.concatenate([w_rt_hi, w_rt_hi, w_rt_lo], axis=0)
        b_rt = _pad_lanes(jnp.concatenate([b_exp[l], b_grp[l]])[None, :], LANES)
        x1, h2_slabs, rt, counts = _out_route(y_na, o_f, o_b, gr, xf, w_o[:NA_WIDTH], w_o[NA_WIDTH:],
                                              gla_norm_g[l][None, :], norm_ffn_g[l][None, :], w_rt3, b_rt)

        tile_expert, n_tiles, src_rows, pos = _dispatch_plan(rt, counts, n)
        ys = _experts(tile_expert, n_tiles, src_rows, h2_slabs, w_gate[l], w_up[l], w_down[l])
        last = l == depth - 1
        xf = _combine(pos, ys, x1, rt, final_norm_g[None, :], final_norm=last)
    return xf.reshape(batch, seq, d)
```

```python
import functools

import jax
import jax.numpy as jnp
from jax import lax
from jax.experimental import pallas as pl
from jax.experimental.pallas import tpu as pltpu

F32 = jnp.float32
BF16 = jnp.bfloat16

GRID_W = 64
NA_HEADS = 8
NA_HEAD_DIM = 64
NA_WIDTH = NA_HEADS * NA_HEAD_DIM
WIN_H_MAX = 8
WIN_W = 16
GLA_HEADS = 4
GLA_DK = 64
GLA_DV = 128
GLA_KEY_WIDTH = GLA_HEADS * GLA_DK
GLA_VAL_WIDTH = GLA_HEADS * GLA_DV
GLA_GATE_RANK = 16
GLA_GATE_NORMALIZER = 16.0
GLA_CHUNK = 64
N_GROUPS = 4
EXPERTS_PER_GROUP = 8
N_EXPERTS = N_GROUPS * EXPERTS_PER_GROUP
RMS_EPS = 1e-6

LANES = 128
SUBLANES = 8
VMEM_LIMIT_BYTES = 56 * 1024 * 1024

MASK_VALUE = -1e30

IN_PROJ_ROWS = 512
NA_ROW_BLOCK = 8
GLA_STEP_CHUNKS = 4
OUT_ROWS = 512
EXPERT_ROWS = 256
DISPATCH_ROWS = 512
COMBINE_ROWS = DISPATCH_ROWS
GATHER_UNROLL = 8


def _params(*sem):
    return pltpu.CompilerParams(dimension_semantics=sem, vmem_limit_bytes=VMEM_LIMIT_BYTES)


def _rms(x, g):
    return x * lax.rsqrt(jnp.mean(x * x, axis=-1, keepdims=True) + RMS_EPS) * g


def _in_proj_kernel(x_ref, g_ref, w_na_ref, w_qk_ref, w_v_ref, w_r_ref, w_lr_ref,
                    w_g2_ref, b_g_ref, na_ref, qk_ref, v_ref, r_ref, gate_ref):
    h = _rms(x_ref[...], g_ref[...]).astype(BF16)
    na_ref[...] = jnp.dot(h, w_na_ref[...], preferred_element_type=F32).astype(BF16)
    qk_ref[...] = jnp.dot(h, w_qk_ref[...], preferred_element_type=F32)
    v_ref[...] = jnp.dot(h, w_v_ref[...], preferred_element_type=F32).astype(BF16)
    r_ref[...] = jnp.dot(h, w_r_ref[...], preferred_element_type=F32)
    lr = jnp.dot(h, w_lr_ref[...], preferred_element_type=F32)
    z = jnp.dot(lr, w_g2_ref[...], preferred_element_type=F32,
                precision=lax.Precision.HIGHEST) + b_g_ref[...]
    log_sig = jnp.minimum(z, 0.0) - jnp.log(1.0 + jnp.exp(-jnp.abs(z)))
    gate_ref[...] = log_sig * (1.0 / GLA_GATE_NORMALIZER)


def _in_proj(x, g, w_na, w_qk, w_v, w_r, w_lr, w_g2, b_g):
    n, d = x.shape
    rows = IN_PROJ_ROWS
    row_spec = lambda width: pl.BlockSpec((rows, width), lambda i: (i, 0))
    full = lambda a: pl.BlockSpec(a.shape, lambda i: (0,) * a.ndim)
    return pl.pallas_call(
        _in_proj_kernel,
        grid=(n // rows,),
        in_specs=[row_spec(d), full(g), full(w_na), full(w_qk), full(w_v), full(w_r), full(w_lr),
                  full(w_g2), full(b_g)],
        out_specs=[row_spec(3 * NA_WIDTH), row_spec(2 * GLA_KEY_WIDTH), row_spec(GLA_VAL_WIDTH),
                   row_spec(GLA_VAL_WIDTH), row_spec(2 * GLA_KEY_WIDTH)],
        out_shape=[jax.ShapeDtypeStruct((n, 3 * NA_WIDTH), BF16),
                   jax.ShapeDtypeStruct((n, 2 * GLA_KEY_WIDTH), F32),
                   jax.ShapeDtypeStruct((n, GLA_VAL_WIDTH), BF16),
                   jax.ShapeDtypeStruct((n, GLA_VAL_WIDTH), F32),
                   jax.ShapeDtypeStruct((n, 2 * GLA_KEY_WIDTH), F32)],
        compiler_params=_params("arbitrary"),
        name="in_proj",
    )(x, g, w_na, w_qk, w_v, w_r, w_lr, w_g2, b_g)


def _na_bias_table(rpb, rows):
    kh = min(WIN_H_MAX, rows)
    cls = jnp.arange(kh)[:, None, None]
    i = jnp.arange(kh)[None, :, None]
    w = jnp.arange(GRID_W)[:, None, None]
    x = jnp.arange(GRID_W)[None, :, None]
    cs = jnp.clip(w - WIN_W // 2, 0, GRID_W - WIN_W)
    valid = (x >= cs) & (x < cs + WIN_W)
    row_sel = (jnp.arange(2 * WIN_H_MAX - 1)[None, None, :] == i - cls + (WIN_H_MAX - 1)).astype(F32)
    col_sel = (valid & (jnp.arange(2 * WIN_W - 1)[None, None, :] == x - w + (WIN_W - 1))).astype(F32)
    hi = lax.Precision.HIGHEST
    by_col = jnp.einsum("hrc,wxc->hrwx", rpb.astype(F32), col_sel, precision=hi)
    bias = jnp.einsum("kir,hrwx->khwix", row_sel, by_col, precision=hi)
    bias = jnp.where(valid[None, None, :, None, :, 0], bias, MASK_VALUE)
    return bias.reshape(kh, NA_HEADS // 2, 2 * GRID_W, kh * GRID_W)


def _na_kernel(q_ref, kp_ref, kc_ref, kn_ref, vp_ref, vc_ref, vn_ref, tbl_ref, o_ref,
               kbuf, vbuf, *, rows, kh):
    rb = NA_ROW_BLOCK
    j = pl.program_id(1)
    blk = rb * GRID_W
    for s, (k_src, v_src) in enumerate(((kp_ref, vp_ref), (kc_ref, vc_ref), (kn_ref, vn_ref))):
        kbuf[s * blk:(s + 1) * blk, :] = k_src[...].reshape(blk, NA_WIDTH)
        vbuf[s * blk:(s + 1) * blk, :] = v_src[...].reshape(blk, NA_WIDTH)
    lane = lax.broadcasted_iota(jnp.int32, (GRID_W, LANES), 1)
    first = lane < NA_HEAD_DIM

    def row_body(lr, carry):
        r = j * rb + lr
        start = jnp.clip(r - kh // 2, 0, rows - kh)
        cls = r - start
        local = pl.multiple_of((start - (j - 1) * rb) * GRID_W, GRID_W)
        q_row = q_ref[lr]
        k_win = kbuf[pl.ds(local, kh * GRID_W), :]
        v_win = vbuf[pl.ds(local, kh * GRID_W), :]
        outs = []
        for p in range(NA_HEADS // 2):
            sl = slice(p * LANES, (p + 1) * LANES)
            q_pair = q_row[:, sl]
            zero = jnp.zeros_like(q_pair)
            q_bd = jnp.concatenate([jnp.where(first, q_pair, zero), jnp.where(first, zero, q_pair)], axis=0)
            s = lax.dot_general(q_bd, k_win[:, sl], (((1,), (1,)), ((), ())), preferred_element_type=F32)
            s = s + tbl_ref[cls, p]
            m = jnp.max(s, axis=-1, keepdims=True)
            e = jnp.exp(s - m)
            denom = jnp.sum(e, axis=-1, keepdims=True)
            o = jnp.dot(e.astype(BF16), v_win[:, sl], preferred_element_type=F32) / denom
            outs.append(jnp.where(first, o[:GRID_W], o[GRID_W:]))
        o_ref[lr] = jnp.concatenate(outs, axis=-1).astype(o_ref.dtype)
        return carry

    lax.fori_loop(0, rb, row_body, 0)


def _na(na_qkv, tbl, batch, rows):
    kh = min(WIN_H_MAX, rows)
    rb = NA_ROW_BLOCK
    nblk = rows // rb
    x4 = na_qkv.reshape(batch, rows, GRID_W, 3 * NA_WIDTH)
    blk = (None, rb, GRID_W, NA_WIDTH)
    prev = lambda j: jnp.maximum(j - 1, 0)
    nxt = lambda j: jnp.minimum(j + 1, nblk - 1)
    specs = [pl.BlockSpec(blk, lambda b, j: (b, j, 0, 0))]
    for col in (1, 2):
        specs += [pl.BlockSpec(blk, lambda b, j, col=col: (b, prev(j), 0, col)),
                  pl.BlockSpec(blk, lambda b, j, col=col: (b, j, 0, col)),
                  pl.BlockSpec(blk, lambda b, j, col=col: (b, nxt(j), 0, col))]
    specs.append(pl.BlockSpec(tbl.shape, lambda b, j: (0, 0, 0, 0)))
    out = pl.pallas_call(
        functools.partial(_na_kernel, rows=rows, kh=kh),
        grid=(batch, nblk),
        in_specs=specs,
        out_specs=pl.BlockSpec(blk, lambda b, j: (b, j, 0, 0)),
        out_shape=jax.ShapeDtypeStruct((batch, rows, GRID_W, NA_WIDTH), BF16),
        scratch_shapes=[pltpu.VMEM((3 * rb * GRID_W, NA_WIDTH), BF16),
                        pltpu.VMEM((3 * rb * GRID_W, NA_WIDTH), BF16)],
        compiler_params=_params("arbitrary", "arbitrary"),
        name="na",
    )(x4, x4, x4, x4, x4, x4, x4, tbl)
    return out.reshape(batch * rows * GRID_W, NA_WIDTH)


def _block_diag_mask(row_block, col_block, nblocks):
    shape = (row_block * nblocks, col_block * nblocks)
    r = lax.shift_right_logical(lax.broadcasted_iota(jnp.int32, shape, 0), row_block.bit_length() - 1)
    c = lax.shift_right_logical(lax.broadcasted_iota(jnp.int32, shape, 1), col_block.bit_length() - 1)
    return r == c


def _gla_chunk(q, k, v, g, state_ref, *, backward):
    c = GLA_CHUNK
    ti = lax.broadcasted_iota(jnp.int32, (c, c), 0)
    tj = lax.broadcasted_iota(jnp.int32, (c, c), 1)
    if backward:
        cum = (tj >= ti).astype(F32)
        ref_row, last_row = c // 2, 0
    else:
        cum = (tj <= ti).astype(F32)
        ref_row, last_row = c // 2 - 1, c - 1
    b = jnp.dot(cum, g, preferred_element_type=F32, precision=lax.Precision.HIGHEST)
    b_ref = b[ref_row:ref_row + 1, :]
    b_last = b[last_row:last_row + 1, :]
    q_rel = (q * jnp.exp(b - b_ref)).astype(BF16)
    k_rel = (k * jnp.exp(b_ref - b)).astype(BF16)
    k_dec = (k * jnp.exp(b_last - b)).astype(BF16)
    q_dec = (q * jnp.exp(b)).astype(BF16)
    decay = jnp.exp(b_last)

    kk_mask = _block_diag_mask(c, GLA_DK, GLA_HEADS)
    k_bd = jnp.where(kk_mask, jnp.concatenate([k_rel] * GLA_HEADS, axis=0), jnp.zeros((), BF16))
    s = lax.dot_general(q_rel, k_bd, (((1,), (1,)), ((), ())), preferred_element_type=F32)
    si = lax.broadcasted_iota(jnp.int32, (c, c * GLA_HEADS), 0)
    sj = lax.broadcasted_iota(jnp.int32, (c, c * GLA_HEADS), 1) & (c - 1)
    keep = (sj > si) if backward else (sj <= si)
    p = jnp.where(keep, s, 0.0).astype(BF16)
    kv_mask = _block_diag_mask(c, GLA_DV, GLA_HEADS)
    v_bd = jnp.where(kv_mask, jnp.concatenate([v] * GLA_HEADS, axis=0), jnp.zeros((), BF16))
    o_intra = jnp.dot(p, v_bd, preferred_element_type=F32)

    state = state_ref[...]
    o_inter = lax.dot_general(q_dec, state.astype(BF16), (((1,), (1,)), ((), ())),
                              preferred_element_type=F32)
    vt = v.astype(F32).T.astype(BF16)
    upd = jnp.dot(vt, k_dec, preferred_element_type=F32)
    vk_mask = _block_diag_mask(GLA_DV, GLA_DK, GLA_HEADS)
    state_ref[...] = state * decay + jnp.where(vk_mask, upd, 0.0)
    return o_intra + o_inter


def _gla_kernel(qk_f_ref, v_f_ref, g_f_ref, qk_b_ref, v_b_ref, g_b_ref, o_f_ref, o_b_ref,
                st_f, st_b):
    @pl.when(pl.program_id(1) == 0)
    def _():
        st_f[...] = jnp.zeros_like(st_f)
        st_b[...] = jnp.zeros_like(st_b)

    c = GLA_CHUNK
    kw = GLA_KEY_WIDTH
    for n in range(GLA_STEP_CHUNKS):
        sl = slice(n * c, (n + 1) * c)
        o_f_ref[sl, :] = _gla_chunk(qk_f_ref[sl, :kw], qk_f_ref[sl, kw:], v_f_ref[sl, :], g_f_ref[sl, :],
                                    st_f, backward=False)
        m = GLA_STEP_CHUNKS - 1 - n
        sl = slice(m * c, (m + 1) * c)
        o_b_ref[sl, :] = _gla_chunk(qk_b_ref[sl, :kw], qk_b_ref[sl, kw:], v_b_ref[sl, :], g_b_ref[sl, :],
                                    st_b, backward=True)


def _gla(qk, v, gates, batch, seq):
    step = GLA_STEP_CHUNKS * GLA_CHUNK
    nblk = seq // step
    qk3 = qk.reshape(batch, seq, 2 * GLA_KEY_WIDTH)
    v3 = v.reshape(batch, seq, GLA_VAL_WIDTH)
    g3 = gates.reshape(batch, seq, 2 * GLA_KEY_WIDTH)
    fwd = lambda b, n: (b, n, 0)
    bwd = lambda b, n: (b, nblk - 1 - n, 0)
    bwd_gate = lambda b, n: (b, nblk - 1 - n, 1)
    o_f, o_b = pl.pallas_call(
        _gla_kernel,
        grid=(batch, nblk),
        in_specs=[pl.BlockSpec((None, step, 2 * GLA_KEY_WIDTH), fwd),
                  pl.BlockSpec((None, step, GLA_VAL_WIDTH), fwd),
                  pl.BlockSpec((None, step, GLA_KEY_WIDTH), fwd),
                  pl.BlockSpec((None, step, 2 * GLA_KEY_WIDTH), bwd),
                  pl.BlockSpec((None, step, GLA_VAL_WIDTH), bwd),
                  pl.BlockSpec((None, step, GLA_KEY_WIDTH), bwd_gate)],
        out_specs=[pl.BlockSpec((None, step, GLA_VAL_WIDTH), fwd),
                   pl.BlockSpec((None, step, GLA_VAL_WIDTH), bwd)],
        out_shape=[jax.ShapeDtypeStruct((batch, seq, GLA_VAL_WIDTH), F32)] * 2,
        scratch_shapes=[pltpu.VMEM((GLA_VAL_WIDTH, GLA_KEY_WIDTH), F32),
                        pltpu.VMEM((GLA_VAL_WIDTH, GLA_KEY_WIDTH), F32)],
        compiler_params=_params("arbitrary", "arbitrary"),
        name="gla",
    )(qk3, v3, g3, qk3, v3, g3)
    return o_f.reshape(batch * seq, GLA_VAL_WIDTH), o_b.reshape(batch * seq, GLA_VAL_WIDTH)


def _store_slabs(ref, x):
    rows = x.shape[0]
    for s in range(SUBLANES):
        ref[pl.ds(s, rows, stride=SUBLANES), :] = x[:, s * LANES:(s + 1) * LANES]


def _load_slabs(ref, rows):
    return jnp.concatenate([ref[pl.ds(s, rows, stride=SUBLANES), :] for s in range(SUBLANES)], axis=1)


def _slab_copy(src_hbm, dst, sem, src_row, k):
    dst_row = k * SUBLANES if isinstance(k, int) else pl.multiple_of(k * SUBLANES, SUBLANES)
    return pltpu.make_async_copy(src_hbm.at[pl.ds(pl.multiple_of(src_row, SUBLANES), SUBLANES), :],
                                 dst.at[pl.ds(dst_row, SUBLANES), :], sem)


def _wait_slabs(src_hbm, dst, sem):
    pltpu.make_async_copy(src_hbm.at[pl.ds(0, dst.shape[0]), :], dst, sem).wait()


RT_E1, RT_E2, RT_RANK1, RT_RANK2, RT_W1, RT_W2 = range(6)
GROUP_LANE0 = N_EXPERTS


def _out_route_kernel(na_ref, of_ref, ob_ref, r_ref, x_ref, w_na_ref, w_gla_ref, gn_ref, fn_ref,
                      w_rt_ref, b_rt_ref, x1_ref, h2_ref, rt_ref, cnt_ref, carry):
    @pl.when(pl.program_id(0) == 0)
    def _():
        carry[...] = jnp.zeros_like(carry)

    rows = x_ref.shape[0]
    o = of_ref[...] + ob_ref[...]
    r = r_ref[...]
    parts = []
    for h in range(GLA_HEADS):
        sl = slice(h * GLA_DV, (h + 1) * GLA_DV)
        parts.append(_rms(o[:, sl], gn_ref[...]) * (r[:, sl] * jax.nn.sigmoid(r[:, sl])))
    y_gla = jnp.concatenate(parts, axis=-1).astype(BF16)
    x1 = (x_ref[...] + jnp.dot(na_ref[...], w_na_ref[...], preferred_element_type=F32)
          + jnp.dot(y_gla, w_gla_ref[...], preferred_element_type=F32))
    x1_ref[...] = x1
    h2 = _rms(x1, fn_ref[...])
    _store_slabs(h2_ref, h2)

    h_hi = h2.astype(BF16)
    h_lo = (h2 - h_hi.astype(F32)).astype(BF16)
    logits = jnp.dot(jnp.concatenate([h_hi, h_lo, h_hi], axis=1), w_rt_ref[...],
                     preferred_element_type=F32) + b_rt_ref[...]
    lane_i = lax.broadcasted_iota(jnp.int32, (rows, LANES), 1)
    lane = lane_i.astype(F32)
    lane_grp = lax.shift_right_logical(lane_i, EXPERTS_PER_GROUP.bit_length() - 1).astype(F32)
    neg = jnp.float32(-jnp.inf)
    no_lane = jnp.float32(LANES)
    is_grp = (lane_i >= GROUP_LANE0) & (lane_i < GROUP_LANE0 + N_GROUPS)
    g_logit = jnp.where(is_grp, logits, neg)
    g_max = jnp.max(g_logit, axis=-1, keepdims=True)
    g_sel = jnp.min(jnp.where(is_grp & (g_logit == g_max), lane, no_lane), axis=-1, keepdims=True) - GROUP_LANE0
    grp_w = 1.0 / jnp.sum(jnp.where(is_grp, jnp.exp(g_logit - g_max), 0.0), axis=-1, keepdims=True)
    in_grp = (lane_i < N_EXPERTS) & (lane_grp == g_sel)
    e_logit = jnp.where(in_grp, logits, neg)
    v1 = jnp.max(e_logit, axis=-1, keepdims=True)
    i1 = jnp.min(jnp.where(in_grp & (e_logit == v1), lane, no_lane), axis=-1, keepdims=True)
    rest = in_grp & (lane != i1)
    e_logit2 = jnp.where(rest, logits, neg)
    v2 = jnp.max(e_logit2, axis=-1, keepdims=True)
    i2 = jnp.min(jnp.where(rest & (e_logit2 == v2), lane, no_lane), axis=-1, keepdims=True)
    t = jnp.exp(v2 - v1)
    w1 = grp_w / (1.0 + t)
    w2 = grp_w * t / (1.0 + t)

    sel1 = lane == i1
    sel2 = lane == i2
    onehot = jnp.where(sel1 | sel2, 1.0, 0.0)
    ti = lax.broadcasted_iota(jnp.int32, (rows, rows), 0)
    tj = lax.broadcasted_iota(jnp.int32, (rows, rows), 1)
    before = jnp.where(tj < ti, 1.0, 0.0).astype(BF16)
    ranks = jnp.dot(before, onehot.astype(BF16), preferred_element_type=F32) + carry[0:1, :]
    rank1 = jnp.sum(jnp.where(sel1, ranks, 0.0), axis=-1, keepdims=True)
    rank2 = jnp.sum(jnp.where(sel2, ranks, 0.0), axis=-1, keepdims=True)
    new_carry = carry[0:1, :] + jnp.sum(onehot, axis=0, keepdims=True)
    carry[...] = jnp.broadcast_to(new_carry, carry.shape)
    cnt_ref[...] = jnp.broadcast_to(new_carry, cnt_ref.shape)

    rec = jnp.zeros((rows, LANES), F32)
    for idx, val in ((RT_E1, i1), (RT_E2, i2), (RT_RANK1, rank1), (RT_RANK2, rank2), (RT_W1, w1), (RT_W2, w2)):
        rec = jnp.where(lane_i == idx, val, rec)
    rt_ref[...] = rec


def _out_route(y_na, o_f, o_b, r, x, w_na, w_gla, gn, fn, w_rt, b_rt):
    n, d = x.shape
    rows = OUT_ROWS
    row_spec = lambda width: pl.BlockSpec((rows, width), lambda i: (i, 0))
    full = lambda a: pl.BlockSpec(a.shape, lambda i: (0,) * a.ndim)
    return pl.pallas_call(
        _out_route_kernel,
        grid=(n // rows,),
        in_specs=[row_spec(NA_WIDTH), row_spec(GLA_VAL_WIDTH), row_spec(GLA_VAL_WIDTH),
                  row_spec(GLA_VAL_WIDTH), row_spec(d), full(w_na), full(w_gla), full(gn), full(fn),
                  full(w_rt), full(b_rt)],
        out_specs=[row_spec(d), pl.BlockSpec((rows * SUBLANES, LANES), lambda i: (i, 0)), row_spec(LANES),
                   pl.BlockSpec((SUBLANES, LANES), lambda i: (0, 0))],
        out_shape=[jax.ShapeDtypeStruct((n, d), F32), jax.ShapeDtypeStruct((n * SUBLANES, LANES), F32),
                   jax.ShapeDtypeStruct((n, LANES), F32), jax.ShapeDtypeStruct((SUBLANES, LANES), F32)],
        scratch_shapes=[pltpu.VMEM((8, LANES), F32)],
        compiler_params=_params("arbitrary"),
        name="out_route",
    )(y_na, o_f, o_b, r, x, w_na, w_gla, gn, fn, w_rt, b_rt)


def _dispatch_kernel(pad_tile_ref, n_tiles_ref, rt_ref, offs_ref, h_hbm, pos_ref, xs_hbm,
                     zeros, pos_vmem, pos_smem, sem_zero, sem_pos, sem):
    i = pl.program_id(0)
    n_steps = pl.num_programs(0)
    db = DISPATCH_ROWS
    tile = EXPERT_ROWS * SUBLANES
    max_tiles = xs_hbm.shape[0] // tile

    @pl.when(i == 0)
    def _():
        zeros[...] = jnp.zeros_like(zeros)

        def zero_copy(t):
            return pltpu.make_async_copy(zeros, xs_hbm.at[pl.ds(pl.multiple_of(t * tile, tile), tile), :], sem_zero)

        def for_each_zero_tile(fn):
            def padded(e, carry):
                @pl.when(pad_tile_ref[e] >= 0)
                def _():
                    fn(zero_copy(pad_tile_ref[e]))
                return carry
            lax.fori_loop(0, N_EXPERTS, padded, 0)

            def tail(t, carry):
                fn(zero_copy(t))
                return carry
            lax.fori_loop(n_tiles_ref[0], max_tiles, tail, 0)

        for_each_zero_tile(lambda cp: cp.start())
        for_each_zero_tile(lambda cp: cp.wait())

    rt_t = rt_ref[...].T
    expert = lax.broadcasted_iota(jnp.int32, (N_EXPERTS, db), 0).astype(F32)
    pos = []
    for e_lane, rank_lane in ((RT_E1, RT_RANK1), (RT_E2, RT_RANK2)):
        seg = jnp.sum(jnp.where(expert == rt_t[e_lane:e_lane + 1, :], offs_ref[...], 0.0), axis=0, keepdims=True)
        pos.append((seg + rt_t[rank_lane:rank_lane + 1, :]) * SUBLANES)
    pos = jnp.concatenate(pos, axis=0).astype(jnp.int32)
    pos_ref[...] = pos
    pos_vmem[...] = jnp.concatenate([pos, jnp.zeros((SUBLANES - 2, db), jnp.int32)], axis=0)
    to_smem = pltpu.make_async_copy(pos_vmem, pos_smem, sem_pos)
    to_smem.start()
    to_smem.wait()

    def wait_block():
        pltpu.make_async_copy(h_hbm.at[pl.ds(0, 2 * db * SUBLANES), :],
                              xs_hbm.at[pl.ds(0, 2 * db * SUBLANES), :], sem).wait()

    @pl.when(i > 0)
    def _():
        wait_block()

    def body(k, carry):
        src = h_hbm.at[pl.ds(pl.multiple_of((i * db + k) * SUBLANES, SUBLANES), SUBLANES), :]
        for slot in range(2):
            dst_row = pl.multiple_of(pos_smem[slot, k], SUBLANES)
            pltpu.make_async_copy(src, xs_hbm.at[pl.ds(dst_row, SUBLANES), :], sem).start()
        return carry
    lax.fori_loop(0, db, body, 0, unroll=GATHER_UNROLL)

    @pl.when(i == n_steps - 1)
    def _():
        wait_block()


def _dispatch(pad_tile, n_tiles, rt, offs_b, h2_slabs, sorted_rows):
    n = rt.shape[0]
    db = DISPATCH_ROWS
    grid_spec = pltpu.PrefetchScalarGridSpec(
        num_scalar_prefetch=2,
        grid=(n // db,),
        in_specs=[pl.BlockSpec((db, LANES), lambda i, pt, nt: (i, 0)),
                  pl.BlockSpec(offs_b.shape, lambda i, pt, nt: (0, 0)),
                  pl.BlockSpec(memory_space=pl.ANY)],
        out_specs=[pl.BlockSpec((None, 2, db), lambda i, pt, nt: (i, 0, 0)),
                   pl.BlockSpec(memory_space=pl.ANY)],
        scratch_shapes=[pltpu.VMEM((EXPERT_ROWS * SUBLANES, LANES), F32),
                        pltpu.VMEM((SUBLANES, db), jnp.int32), pltpu.SMEM((SUBLANES, db), jnp.int32),
                        pltpu.SemaphoreType.DMA(()), pltpu.SemaphoreType.DMA(()), pltpu.SemaphoreType.DMA(())],
    )
    return pl.pallas_call(
        _dispatch_kernel,
        grid_spec=grid_spec,
        out_shape=[jax.ShapeDtypeStruct((n // db, 2, db), jnp.int32),
                   jax.ShapeDtypeStruct((sorted_rows * SUBLANES, LANES), F32)],
        compiler_params=_params("arbitrary"),
        name="dispatch",
    )(pad_tile, n_tiles, rt, offs_b, h2_slabs)


def _experts_kernel(tile_expert_ref, n_tiles_ref, x_ref, wg_ref, wu_ref, wd_ref, y_ref, wg_s, wu_s, wd_s):
    i = pl.program_id(0)
    rows = EXPERT_ROWS
    n_tiles = n_tiles_ref[0]

    @pl.when(i < n_tiles)
    def _():
        @pl.when((i == 0) | (tile_expert_ref[i] != tile_expert_ref[jnp.maximum(i - 1, 0)]))
        def _():
            wg_s[...] = wg_ref[...].astype(BF16)
            wu_s[...] = wu_ref[...].astype(BF16)
            wd_s[...] = wd_ref[...].astype(BF16)

        x = _load_slabs(x_ref, rows).astype(BF16)
        gate = jnp.dot(x, wg_s[...], preferred_element_type=F32)
        up = jnp.dot(x, wu_s[...], preferred_element_type=F32)
        hid = (gate * jax.nn.sigmoid(gate) * up).astype(BF16)
        _store_slabs(y_ref, jnp.dot(hid, wd_s[...], preferred_element_type=F32))

    @pl.when(i >= n_tiles)
    def _():
        y_ref[...] = jnp.zeros_like(y_ref)


def _experts(tile_expert, n_tiles, xs, wg, wu, wd, layer):
    rows = EXPERT_ROWS
    steps = tile_expert.shape[0]
    _, _, d, dff = wg.shape
    assert d == SUBLANES * LANES
    w_idx = lambda i, te, nt: (layer, te[i], 0, 0)
    grid_spec = pltpu.PrefetchScalarGridSpec(
        num_scalar_prefetch=2,
        grid=(steps,),
        in_specs=[pl.BlockSpec((rows * SUBLANES, LANES), lambda i, te, nt: (jnp.minimum(i, nt[0] - 1), 0)),
                  pl.BlockSpec((None, None, d, dff), w_idx),
                  pl.BlockSpec((None, None, d, dff), w_idx),
                  pl.BlockSpec((None, None, dff, d), w_idx)],
        out_specs=pl.BlockSpec((rows * SUBLANES, LANES), lambda i, te, nt: (i, 0)),
        scratch_shapes=[pltpu.VMEM((d, dff), BF16), pltpu.VMEM((d, dff), BF16), pltpu.VMEM((dff, d), BF16)],
    )
    return pl.pallas_call(
        _experts_kernel,
        grid_spec=grid_spec,
        out_shape=jax.ShapeDtypeStruct((steps * rows * SUBLANES, LANES), F32),
        compiler_params=_params("arbitrary"),
        name="experts",
    )(tile_expert, n_tiles, xs, wg, wu, wd)


def _start_slab_gather(src_hbm, dst, sem, row_ref, base, count):
    def body(k, carry):
        _slab_copy(src_hbm, dst, sem, row_ref[base + k], k).start()
        return carry
    lax.fori_loop(0, count, body, 0, unroll=GATHER_UNROLL)


def _combine_kernel(pos_ref, ys_hbm, x1_ref, rt_ref, g_ref, o_ref, buf, sem, *, final_norm):
    i = pl.program_id(0)
    rows = COMBINE_ROWS
    n_steps = pl.num_programs(0)
    slot = i % 2

    def start(step, s):
        for j in range(2):
            _start_slab_gather(ys_hbm, buf.at[s, j], sem.at[s], pos_ref, (step * 2 + j) * rows, rows)

    @pl.when(i == 0)
    def _():
        start(0, 0)

    @pl.when(i + 1 < n_steps)
    def _():
        start(i + 1, 1 - slot)

    _wait_slabs(ys_hbm, buf.at[slot, 0], sem.at[slot])
    _wait_slabs(ys_hbm, buf.at[slot, 1], sem.at[slot])
    rt = rt_ref[...]
    lane = lax.broadcasted_iota(jnp.int32, rt.shape, 1)
    w1 = jnp.sum(jnp.where(lane == RT_W1, rt, 0.0), axis=-1, keepdims=True)
    w2 = jnp.sum(jnp.where(lane == RT_W2, rt, 0.0), axis=-1, keepdims=True)
    y = w1 * _load_slabs(buf.at[slot, 0], rows) + w2 * _load_slabs(buf.at[slot, 1], rows)
    x2 = x1_ref[...] + y
    if final_norm:
        x2 = _rms(x2, g_ref[...])
    o_ref[...] = x2


def _combine(pos, ys, x1, rt, g, final_norm):
    n, d = x1.shape
    rows = COMBINE_ROWS
    grid_spec = pltpu.PrefetchScalarGridSpec(
        num_scalar_prefetch=1,
        grid=(n // rows,),
        in_specs=[pl.BlockSpec(memory_space=pl.ANY),
                  pl.BlockSpec((rows, d), lambda i, pos: (i, 0)),
                  pl.BlockSpec((rows, LANES), lambda i, pos: (i, 0)),
                  pl.BlockSpec(g.shape, lambda i, pos: (0, 0))],
        out_specs=pl.BlockSpec((rows, d), lambda i, pos: (i, 0)),
        scratch_shapes=[pltpu.VMEM((2, 2, rows * SUBLANES, LANES), F32), pltpu.SemaphoreType.DMA((2,))],
    )
    return pl.pallas_call(
        functools.partial(_combine_kernel, final_norm=final_norm),
        grid_spec=grid_spec,
        out_shape=jax.ShapeDtypeStruct((n, d), F32),
        compiler_params=_params("arbitrary"),
        name="combine",
    )(pos, ys, x1, rt, g)


def _dispatch_plan(counts, n):
    rows = EXPERT_ROWS
    steps = (2 * n) // rows + N_EXPERTS
    cnt = counts[0, :N_EXPERTS].astype(jnp.int32)
    tiles = (cnt + rows - 1) // rows
    tile_end = jnp.cumsum(tiles)
    tile_start = tile_end - tiles
    n_tiles = tile_end[-1]
    step_ids = jnp.arange(steps, dtype=jnp.int32)
    tile_expert = jnp.sum(tile_end[None, :] <= jnp.minimum(step_ids, n_tiles - 1)[:, None], axis=1)
    tile_expert = jnp.minimum(tile_expert, N_EXPERTS - 1).astype(jnp.int32)
    pad_tile = jnp.where(cnt % rows != 0, tile_end - 1, -1).astype(jnp.int32)
    offs_b = jnp.broadcast_to((tile_start * rows).astype(F32)[:, None], (N_EXPERTS, DISPATCH_ROWS))
    return tile_expert, n_tiles.reshape(1).astype(jnp.int32), pad_tile, offs_b, steps * rows


def _pad_lanes(a, width):
    return jnp.pad(a, ((0, 0), (0, width - a.shape[1])))


def kernel(x, norm_mix_g, w_in, w_g2_f, b_g_f, w_g2_b, b_g_b, gla_norm_g, rpb, w_out, norm_ffn_g, w_grp, b_grp,
           w_exp, b_exp, w_gate, w_up, w_down, final_norm_g):
    batch, seq, d = x.shape
    n = batch * seq
    rows = seq // GRID_W
    depth = w_in.shape[0]
    xf = x.reshape(n, d)
    c_na = 3 * NA_WIDTH
    c_qk = c_na + 2 * GLA_KEY_WIDTH
    c_v = c_qk + GLA_VAL_WIDTH
    c_r = c_v + GLA_VAL_WIDTH
    for l in range(depth):
        wl = w_in[l]
        q_scale = jnp.concatenate([jnp.full((NA_WIDTH,), NA_HEAD_DIM ** -0.5, F32),
                                   jnp.ones((2 * NA_WIDTH,), F32)])
        w_na = (wl[:, :c_na] * q_scale).astype(BF16)
        qk_scale = jnp.concatenate([jnp.full((GLA_KEY_WIDTH,), GLA_DK ** -0.5, F32),
                                    jnp.ones((GLA_KEY_WIDTH,), F32)])
        w_qk = (wl[:, c_na:c_qk] * qk_scale).astype(BF16)
        w_v = wl[:, c_qk:c_v].astype(BF16)
        w_r = wl[:, c_v:c_r].astype(BF16)
        w_lr = _pad_lanes(wl[:, c_r:], LANES).astype(BF16)
        w_g2 = jnp.zeros((LANES, 2 * GLA_KEY_WIDTH), F32)
        w_g2 = w_g2.at[:GLA_GATE_RANK, :GLA_KEY_WIDTH].set(w_g2_f[l])
        w_g2 = w_g2.at[GLA_GATE_RANK:2 * GLA_GATE_RANK, GLA_KEY_WIDTH:].set(w_g2_b[l])
        b_g = jnp.concatenate([b_g_f[l], b_g_b[l]])[None, :]
        na_qkv, gqk, gv, gr, gates = _in_proj(xf, norm_mix_g[l][None, :], w_na, w_qk, w_v, w_r, w_lr, w_g2, b_g)

        y_na = _na(na_qkv, _na_bias_table(rpb[l], rows), batch, rows)
        o_f, o_b = _gla(gqk, gv, gates, batch, seq)

        w_o = w_out[l].astype(BF16)
        w_rt = _pad_lanes(jnp.concatenate([w_exp[l], w_grp[l]], axis=1), LANES)
        w_rt_hi = w_rt.astype(BF16)
        w_rt_lo = (w_rt - w_rt_hi.astype(F32)).astype(BF16)
        w_rt3 = jnp.concatenate([w_rt_hi, w_rt_hi, w_rt_lo], axis=0)
        b_rt = _pad_lanes(jnp.concatenate([b_exp[l], b_grp[l]])[None, :], LANES)
        x1, h2_slabs, rt, counts = _out_route(y_na, o_f, o_b, gr, xf, w_o[:NA_WIDTH], w_o[NA_WIDTH:],
                                              gla_norm_g[l][None, :], norm_ffn_g[l][None, :], w_rt3, b_rt)

        tile_expert, n_tiles, pad_tile, offs_b, sorted_rows = _dispatch_plan(counts, n)
        pos, xs = _dispatch(pad_tile, n_tiles, rt, offs_b, h2_slabs, sorted_rows)
        ys = _experts(tile_expert, n_tiles, xs, w_gate, w_up, w_down, l)
        pos = pos.reshape(-1)
        last = l == depth - 1
        xf = _combine(pos, ys, x1, rt, final_norm_g[None, :], final_norm=last)
    return xf.reshape(batch, seq, d)
```

```python
import functools

import jax
import jax.numpy as jnp
from jax import lax
from jax.experimental import pallas as pl
from jax.experimental.pallas import tpu as pltpu

F32 = jnp.float32
BF16 = jnp.bfloat16

GRID_W = 64
NA_HEADS = 8
NA_HEAD_DIM = 64
NA_WIDTH = NA_HEADS * NA_HEAD_DIM
WIN_H_MAX = 8
WIN_W = 16
GLA_HEADS = 4
GLA_DK = 64
GLA_DV = 128
GLA_KEY_WIDTH = GLA_HEADS * GLA_DK
GLA_VAL_WIDTH = GLA_HEADS * GLA_DV
GLA_GATE_RANK = 16
GLA_GATE_NORMALIZER = 16.0
GLA_CHUNK = 64
N_GROUPS = 4
EXPERTS_PER_GROUP = 8
N_EXPERTS = N_GROUPS * EXPERTS_PER_GROUP
RMS_EPS = 1e-6

LANES = 128
SUBLANES = 8
VMEM_LIMIT_BYTES = 56 * 1024 * 1024

MASK_VALUE = -1e30

IN_PROJ_ROWS = 512
NA_ROW_BLOCK = 8
GLA_STEP_CHUNKS = 4
OUT_ROWS = 512
EXPERT_ROWS = 256
DISPATCH_ROWS = 512
COMBINE_ROWS = DISPATCH_ROWS
GATHER_UNROLL = 8


def _params(*sem):
    return pltpu.CompilerParams(dimension_semantics=sem, vmem_limit_bytes=VMEM_LIMIT_BYTES)


def _rms(x, g):
    return x * lax.rsqrt(jnp.mean(x * x, axis=-1, keepdims=True) + RMS_EPS) * g


def _in_proj_kernel(x_ref, g_ref, w_na_ref, w_qk_ref, w_v_ref, w_r_ref, w_lr_ref,
                    w_g2_ref, b_g_ref, na_ref, qk_ref, v_ref, r_ref, gate_ref):
    h = _rms(x_ref[...], g_ref[...]).astype(BF16)
    na_ref[...] = jnp.dot(h, w_na_ref[...], preferred_element_type=F32).astype(BF16)
    qk_ref[...] = jnp.dot(h, w_qk_ref[...], preferred_element_type=F32)
    v_ref[...] = jnp.dot(h, w_v_ref[...], preferred_element_type=F32).astype(BF16)
    r_ref[...] = jnp.dot(h, w_r_ref[...], preferred_element_type=F32)
    lr = jnp.dot(h, w_lr_ref[...], preferred_element_type=F32)
    z = jnp.dot(lr, w_g2_ref[...], preferred_element_type=F32,
                precision=lax.Precision.HIGHEST) + b_g_ref[...]
    log_sig = jnp.minimum(z, 0.0) - jnp.log(1.0 + jnp.exp(-jnp.abs(z)))
    gate_ref[...] = log_sig * (1.0 / GLA_GATE_NORMALIZER)


def _in_proj(x, g, w_na, w_qk, w_v, w_r, w_lr, w_g2, b_g):
    n, d = x.shape
    rows = IN_PROJ_ROWS
    row_spec = lambda width: pl.BlockSpec((rows, width), lambda i: (i, 0))
    full = lambda a: pl.BlockSpec(a.shape, lambda i: (0,) * a.ndim)
    return pl.pallas_call(
        _in_proj_kernel,
        grid=(n // rows,),
        in_specs=[row_spec(d), full(g), full(w_na), full(w_qk), full(w_v), full(w_r), full(w_lr),
                  full(w_g2), full(b_g)],
        out_specs=[row_spec(3 * NA_WIDTH), row_spec(2 * GLA_KEY_WIDTH), row_spec(GLA_VAL_WIDTH),
                   row_spec(GLA_VAL_WIDTH), row_spec(2 * GLA_KEY_WIDTH)],
        out_shape=[jax.ShapeDtypeStruct((n, 3 * NA_WIDTH), BF16),
                   jax.ShapeDtypeStruct((n, 2 * GLA_KEY_WIDTH), F32),
                   jax.ShapeDtypeStruct((n, GLA_VAL_WIDTH), BF16),
                   jax.ShapeDtypeStruct((n, GLA_VAL_WIDTH), F32),
                   jax.ShapeDtypeStruct((n, 2 * GLA_KEY_WIDTH), F32)],
        compiler_params=_params("arbitrary"),
        name="in_proj",
    )(x, g, w_na, w_qk, w_v, w_r, w_lr, w_g2, b_g)


def _na_bias_table(rpb, rows):
    kh = min(WIN_H_MAX, rows)
    cls = jnp.arange(kh)[:, None, None]
    i = jnp.arange(kh)[None, :, None]
    w = jnp.arange(GRID_W)[:, None, None]
    x = jnp.arange(GRID_W)[None, :, None]
    cs = jnp.clip(w - WIN_W // 2, 0, GRID_W - WIN_W)
    valid = (x >= cs) & (x < cs + WIN_W)
    row_sel = (jnp.arange(2 * WIN_H_MAX - 1)[None, None, :] == i - cls + (WIN_H_MAX - 1)).astype(F32)
    col_sel = (valid & (jnp.arange(2 * WIN_W - 1)[None, None, :] == x - w + (WIN_W - 1))).astype(F32)
    hi = lax.Precision.HIGHEST
    by_col = jnp.einsum("hrc,wxc->hrwx", rpb.astype(F32), col_sel, precision=hi)
    bias = jnp.einsum("kir,hrwx->khwix", row_sel, by_col, precision=hi)
    bias = jnp.where(valid[None, None, :, None, :, 0], bias, MASK_VALUE)
    return bias.reshape(kh, NA_HEADS // 2, 2 * GRID_W, kh * GRID_W)


def _na_kernel(q_ref, kp_ref, kc_ref, kn_ref, vp_ref, vc_ref, vn_ref, tbl_ref, o_ref,
               kbuf, vbuf, *, rows, kh):
    rb = NA_ROW_BLOCK
    j = pl.program_id(1)
    blk = rb * GRID_W
    for s, (k_src, v_src) in enumerate(((kp_ref, vp_ref), (kc_ref, vc_ref), (kn_ref, vn_ref))):
        kbuf[s * blk:(s + 1) * blk, :] = k_src[...].reshape(blk, NA_WIDTH)
        vbuf[s * blk:(s + 1) * blk, :] = v_src[...].reshape(blk, NA_WIDTH)
    lane = lax.broadcasted_iota(jnp.int32, (GRID_W, LANES), 1)
    first = lane < NA_HEAD_DIM

    def row_body(lr, carry):
        r = j * rb + lr
        start = jnp.clip(r - kh // 2, 0, rows - kh)
        cls = r - start
        local = pl.multiple_of((start - (j - 1) * rb) * GRID_W, GRID_W)
        q_row = q_ref[lr]
        k_win = kbuf[pl.ds(local, kh * GRID_W), :]
        v_win = vbuf[pl.ds(local, kh * GRID_W), :]
        outs = []
        for p in range(NA_HEADS // 2):
            sl = slice(p * LANES, (p + 1) * LANES)
            q_pair = q_row[:, sl]
            zero = jnp.zeros_like(q_pair)
            q_bd = jnp.concatenate([jnp.where(first, q_pair, zero), jnp.where(first, zero, q_pair)], axis=0)
            s = lax.dot_general(q_bd, k_win[:, sl], (((1,), (1,)), ((), ())), preferred_element_type=F32)
            s = s + tbl_ref[cls, p]
            m = jnp.max(s, axis=-1, keepdims=True)
            e = jnp.exp(s - m)
            denom = jnp.sum(e, axis=-1, keepdims=True)
            o = jnp.dot(e.astype(BF16), v_win[:, sl], preferred_element_type=F32) / denom
            outs.append(jnp.where(first, o[:GRID_W], o[GRID_W:]))
        o_ref[lr] = jnp.concatenate(outs, axis=-1).astype(o_ref.dtype)
        return carry

    lax.fori_loop(0, rb, row_body, 0)


def _na(na_qkv, tbl, batch, rows):
    kh = min(WIN_H_MAX, rows)
    rb = NA_ROW_BLOCK
    nblk = rows // rb
    x4 = na_qkv.reshape(batch, rows, GRID_W, 3 * NA_WIDTH)
    blk = (None, rb, GRID_W, NA_WIDTH)
    prev = lambda j: jnp.maximum(j - 1, 0)
    nxt = lambda j: jnp.minimum(j + 1, nblk - 1)
    specs = [pl.BlockSpec(blk, lambda b, j: (b, j, 0, 0))]
    for col in (1, 2):
        specs += [pl.BlockSpec(blk, lambda b, j, col=col: (b, prev(j), 0, col)),
                  pl.BlockSpec(blk, lambda b, j, col=col: (b, j, 0, col)),
                  pl.BlockSpec(blk, lambda b, j, col=col: (b, nxt(j), 0, col))]
    specs.append(pl.BlockSpec(tbl.shape, lambda b, j: (0, 0, 0, 0)))
    out = pl.pallas_call(
        functools.partial(_na_kernel, rows=rows, kh=kh),
        grid=(batch, nblk),
        in_specs=specs,
        out_specs=pl.BlockSpec(blk, lambda b, j: (b, j, 0, 0)),
        out_shape=jax.ShapeDtypeStruct((batch, rows, GRID_W, NA_WIDTH), BF16),
        scratch_shapes=[pltpu.VMEM((3 * rb * GRID_W, NA_WIDTH), BF16),
                        pltpu.VMEM((3 * rb * GRID_W, NA_WIDTH), BF16)],
        compiler_params=_params("arbitrary", "arbitrary"),
        name="na",
    )(x4, x4, x4, x4, x4, x4, x4, tbl)
    return out.reshape(batch * rows * GRID_W, NA_WIDTH)


def _block_diag_mask(row_block, col_block, nblocks):
    shape = (row_block * nblocks, col_block * nblocks)
    r = lax.shift_right_logical(lax.broadcasted_iota(jnp.int32, shape, 0), row_block.bit_length() - 1)
    c = lax.shift_right_logical(lax.broadcasted_iota(jnp.int32, shape, 1), col_block.bit_length() - 1)
    return r == c


def _gla_chunk(q, k, v, g, state_ref, *, backward):
    c = GLA_CHUNK
    ti = lax.broadcasted_iota(jnp.int32, (c, c), 0)
    tj = lax.broadcasted_iota(jnp.int32, (c, c), 1)
    if backward:
        cum = (tj >= ti).astype(F32)
        ref_row, last_row = c // 2, 0
    else:
        cum = (tj <= ti).astype(F32)
        ref_row, last_row = c // 2 - 1, c - 1
    b = jnp.dot(cum, g, preferred_element_type=F32, precision=lax.Precision.HIGHEST)
    b_ref = b[ref_row:ref_row + 1, :]
    b_last = b[last_row:last_row + 1, :]
    q_rel = (q * jnp.exp(b - b_ref)).astype(BF16)
    k_rel = (k * jnp.exp(b_ref - b)).astype(BF16)
    k_dec = (k * jnp.exp(b_last - b)).astype(BF16)
    q_dec = (q * jnp.exp(b)).astype(BF16)
    decay = jnp.exp(b_last)

    kk_mask = _block_diag_mask(c, GLA_DK, GLA_HEADS)
    k_bd = jnp.where(kk_mask, jnp.concatenate([k_rel] * GLA_HEADS, axis=0), jnp.zeros((), BF16))
    s = lax.dot_general(q_rel, k_bd, (((1,), (1,)), ((), ())), preferred_element_type=F32)
    si = lax.broadcasted_iota(jnp.int32, (c, c * GLA_HEADS), 0)
    sj = lax.broadcasted_iota(jnp.int32, (c, c * GLA_HEADS), 1) & (c - 1)
    keep = (sj > si) if backward else (sj <= si)
    p = jnp.where(keep, s, 0.0).astype(BF16)
    kv_mask = _block_diag_mask(c, GLA_DV, GLA_HEADS)
    v_bd = jnp.where(kv_mask, jnp.concatenate([v] * GLA_HEADS, axis=0), jnp.zeros((), BF16))
    o_intra = jnp.dot(p, v_bd, preferred_element_type=F32)

    state = state_ref[...]
    o_inter = lax.dot_general(q_dec, state.astype(BF16), (((1,), (1,)), ((), ())),
                              preferred_element_type=F32)
    vt = v.astype(F32).T.astype(BF16)
    upd = jnp.dot(vt, k_dec, preferred_element_type=F32)
    vk_mask = _block_diag_mask(GLA_DV, GLA_DK, GLA_HEADS)
    state_ref[...] = state * decay + jnp.where(vk_mask, upd, 0.0)
    return o_intra + o_inter


def _gla_kernel(qk_f_ref, v_f_ref, g_f_ref, qk_b_ref, v_b_ref, g_b_ref, o_f_ref, o_b_ref,
                st_f, st_b):
    @pl.when(pl.program_id(1) == 0)
    def _():
        st_f[...] = jnp.zeros_like(st_f)
        st_b[...] = jnp.zeros_like(st_b)

    c = GLA_CHUNK
    kw = GLA_KEY_WIDTH
    for n in range(GLA_STEP_CHUNKS):
        sl = slice(n * c, (n + 1) * c)
        o_f_ref[sl, :] = _gla_chunk(qk_f_ref[sl, :kw], qk_f_ref[sl, kw:], v_f_ref[sl, :], g_f_ref[sl, :],
                                    st_f, backward=False)
        m = GLA_STEP_CHUNKS - 1 - n
        sl = slice(m * c, (m + 1) * c)
        o_b_ref[sl, :] = _gla_chunk(qk_b_ref[sl, :kw], qk_b_ref[sl, kw:], v_b_ref[sl, :], g_b_ref[sl, :],
                                    st_b, backward=True)


def _gla(qk, v, gates, batch, seq):
    step = GLA_STEP_CHUNKS * GLA_CHUNK
    nblk = seq // step
    qk3 = qk.reshape(batch, seq, 2 * GLA_KEY_WIDTH)
    v3 = v.reshape(batch, seq, GLA_VAL_WIDTH)
    g3 = gates.reshape(batch, seq, 2 * GLA_KEY_WIDTH)
    fwd = lambda b, n: (b, n, 0)
    bwd = lambda b, n: (b, nblk - 1 - n, 0)
    bwd_gate = lambda b, n: (b, nblk - 1 - n, 1)
    o_f, o_b = pl.pallas_call(
        _gla_kernel,
        grid=(batch, nblk),
        in_specs=[pl.BlockSpec((None, step, 2 * GLA_KEY_WIDTH), fwd),
                  pl.BlockSpec((None, step, GLA_VAL_WIDTH), fwd),
                  pl.BlockSpec((None, step, GLA_KEY_WIDTH), fwd),
                  pl.BlockSpec((None, step, 2 * GLA_KEY_WIDTH), bwd),
                  pl.BlockSpec((None, step, GLA_VAL_WIDTH), bwd),
                  pl.BlockSpec((None, step, GLA_KEY_WIDTH), bwd_gate)],
        out_specs=[pl.BlockSpec((None, step, GLA_VAL_WIDTH), fwd),
                   pl.BlockSpec((None, step, GLA_VAL_WIDTH), bwd)],
        out_shape=[jax.ShapeDtypeStruct((batch, seq, GLA_VAL_WIDTH), F32)] * 2,
        scratch_shapes=[pltpu.VMEM((GLA_VAL_WIDTH, GLA_KEY_WIDTH), F32),
                        pltpu.VMEM((GLA_VAL_WIDTH, GLA_KEY_WIDTH), F32)],
        compiler_params=_params("arbitrary", "arbitrary"),
        name="gla",
    )(qk3, v3, g3, qk3, v3, g3)
    return o_f.reshape(batch * seq, GLA_VAL_WIDTH), o_b.reshape(batch * seq, GLA_VAL_WIDTH)


def _store_slabs(ref, x):
    rows = x.shape[0]
    for s in range(SUBLANES):
        ref[pl.ds(s, rows, stride=SUBLANES), :] = x[:, s * LANES:(s + 1) * LANES]


def _load_slabs(ref, rows):
    return jnp.concatenate([ref[pl.ds(s, rows, stride=SUBLANES), :] for s in range(SUBLANES)], axis=1)


def _slab_copy(src_hbm, dst, sem, src_row, k):
    dst_row = k * SUBLANES if isinstance(k, int) else pl.multiple_of(k * SUBLANES, SUBLANES)
    return pltpu.make_async_copy(src_hbm.at[pl.ds(pl.multiple_of(src_row, SUBLANES), SUBLANES), :],
                                 dst.at[pl.ds(dst_row, SUBLANES), :], sem)


def _wait_slabs(src_hbm, dst, sem):
    pltpu.make_async_copy(src_hbm.at[pl.ds(0, dst.shape[0]), :], dst, sem).wait()


RT_E1, RT_E2, RT_RANK1, RT_RANK2, RT_W1, RT_W2 = range(6)
GROUP_LANE0 = N_EXPERTS


def _out_route_kernel(na_ref, of_ref, ob_ref, r_ref, x_ref, w_na_ref, w_gla_ref, gn_ref, fn_ref,
                      w_rt_ref, b_rt_ref, x1_ref, h2_ref, rt_ref, cnt_ref, carry):
    @pl.when(pl.program_id(0) == 0)
    def _():
        carry[...] = jnp.zeros_like(carry)

    rows = x_ref.shape[0]
    o = of_ref[...] + ob_ref[...]
    r = r_ref[...]
    parts = []
    for h in range(GLA_HEADS):
        sl = slice(h * GLA_DV, (h + 1) * GLA_DV)
        parts.append(_rms(o[:, sl], gn_ref[...]) * (r[:, sl] * jax.nn.sigmoid(r[:, sl])))
    y_gla = jnp.concatenate(parts, axis=-1).astype(BF16)
    x1 = (x_ref[...] + jnp.dot(na_ref[...], w_na_ref[...], preferred_element_type=F32)
          + jnp.dot(y_gla, w_gla_ref[...], preferred_element_type=F32))
    x1_ref[...] = x1
    h2 = _rms(x1, fn_ref[...])
    _store_slabs(h2_ref, h2)

    h_hi = h2.astype(BF16)
    h_lo = (h2 - h_hi.astype(F32)).astype(BF16)
    logits = jnp.dot(jnp.concatenate([h_hi, h_lo, h_hi], axis=1), w_rt_ref[...],
                     preferred_element_type=F32) + b_rt_ref[...]
    lane_i = lax.broadcasted_iota(jnp.int32, (rows, LANES), 1)
    lane = lane_i.astype(F32)
    lane_grp = lax.shift_right_logical(lane_i, EXPERTS_PER_GROUP.bit_length() - 1).astype(F32)
    neg = jnp.float32(-jnp.inf)
    no_lane = jnp.float32(LANES)
    is_grp = (lane_i >= GROUP_LANE0) & (lane_i < GROUP_LANE0 + N_GROUPS)
    g_logit = jnp.where(is_grp, logits, neg)
    g_max = jnp.max(g_logit, axis=-1, keepdims=True)
    g_sel = jnp.min(jnp.where(is_grp & (g_logit == g_max), lane, no_lane), axis=-1, keepdims=True) - GROUP_LANE0
    grp_w = 1.0 / jnp.sum(jnp.where(is_grp, jnp.exp(g_logit - g_max), 0.0), axis=-1, keepdims=True)
    in_grp = (lane_i < N_EXPERTS) & (lane_grp == g_sel)
    e_logit = jnp.where(in_grp, logits, neg)
    v1 = jnp.max(e_logit, axis=-1, keepdims=True)
    i1 = jnp.min(jnp.where(in_grp & (e_logit == v1), lane, no_lane), axis=-1, keepdims=True)
    rest = in_grp & (lane != i1)
    e_logit2 = jnp.where(rest, logits, neg)
    v2 = jnp.max(e_logit2, axis=-1, keepdims=True)
    i2 = jnp.min(jnp.where(rest & (e_logit2 == v2), lane, no_lane), axis=-1, keepdims=True)
    t = jnp.exp(v2 - v1)
    w1 = grp_w / (1.0 + t)
    w2 = grp_w * t / (1.0 + t)

    sel1 = lane == i1
    sel2 = lane == i2
    onehot = jnp.where(sel1 | sel2, 1.0, 0.0)
    ti = lax.broadcasted_iota(jnp.int32, (rows, rows), 0)
    tj = lax.broadcasted_iota(jnp.int32, (rows, rows), 1)
    before = jnp.where(tj < ti, 1.0, 0.0).astype(BF16)
    ranks = jnp.dot(before, onehot.astype(BF16), preferred_element_type=F32) + carry[0:1, :]
    rank1 = jnp.sum(jnp.where(sel1, ranks, 0.0), axis=-1, keepdims=True)
    rank2 = jnp.sum(jnp.where(sel2, ranks, 0.0), axis=-1, keepdims=True)
    new_carry = carry[0:1, :] + jnp.sum(onehot, axis=0, keepdims=True)
    carry[...] = jnp.broadcast_to(new_carry, carry.shape)
    cnt_ref[...] = jnp.broadcast_to(new_carry, cnt_ref.shape)

    rec = jnp.zeros((rows, LANES), F32)
    for idx, val in ((RT_E1, i1), (RT_E2, i2), (RT_RANK1, rank1), (RT_RANK2, rank2), (RT_W1, w1), (RT_W2, w2)):
        rec = jnp.where(lane_i == idx, val, rec)
    rt_ref[...] = rec


def _out_route(y_na, o_f, o_b, r, x, w_na, w_gla, gn, fn, w_rt, b_rt):
    n, d = x.shape
    rows = OUT_ROWS
    row_spec = lambda width: pl.BlockSpec((rows, width), lambda i: (i, 0))
    full = lambda a: pl.BlockSpec(a.shape, lambda i: (0,) * a.ndim)
    return pl.pallas_call(
        _out_route_kernel,
        grid=(n // rows,),
        in_specs=[row_spec(NA_WIDTH), row_spec(GLA_VAL_WIDTH), row_spec(GLA_VAL_WIDTH),
                  row_spec(GLA_VAL_WIDTH), row_spec(d), full(w_na), full(w_gla), full(gn), full(fn),
                  full(w_rt), full(b_rt)],
        out_specs=[row_spec(d), pl.BlockSpec((rows * SUBLANES, LANES), lambda i: (i, 0)), row_spec(LANES),
                   pl.BlockSpec((SUBLANES, LANES), lambda i: (0, 0))],
        out_shape=[jax.ShapeDtypeStruct((n, d), F32), jax.ShapeDtypeStruct((n * SUBLANES, LANES), F32),
                   jax.ShapeDtypeStruct((n, LANES), F32), jax.ShapeDtypeStruct((SUBLANES, LANES), F32)],
        scratch_shapes=[pltpu.VMEM((8, LANES), F32)],
        compiler_params=_params("arbitrary"),
        name="out_route",
    )(y_na, o_f, o_b, r, x, w_na, w_gla, gn, fn, w_rt, b_rt)


def _dispatch_kernel(pad_tile_ref, n_tiles_ref, rt_ref, offs_ref, h_ref, pos_ref, xs_hbm,
                     zeros, pos_vmem, pos_smem, sem_zero, sem_pos, sem):
    i = pl.program_id(0)
    db = DISPATCH_ROWS
    tile = EXPERT_ROWS * SUBLANES
    max_tiles = xs_hbm.shape[0] // tile

    @pl.when(i == 0)
    def _():
        zeros[...] = jnp.zeros_like(zeros)

        def zero_copy(t):
            return pltpu.make_async_copy(zeros, xs_hbm.at[pl.ds(pl.multiple_of(t * tile, tile), tile), :], sem_zero)

        def for_each_zero_tile(fn):
            def padded(e, carry):
                @pl.when(pad_tile_ref[e] >= 0)
                def _():
                    fn(zero_copy(pad_tile_ref[e]))
                return carry
            lax.fori_loop(0, N_EXPERTS, padded, 0)

            def tail(t, carry):
                fn(zero_copy(t))
                return carry
            lax.fori_loop(n_tiles_ref[0], max_tiles, tail, 0)

        for_each_zero_tile(lambda cp: cp.start())
        for_each_zero_tile(lambda cp: cp.wait())

    rt_t = rt_ref[...].T
    expert = lax.broadcasted_iota(jnp.int32, (N_EXPERTS, db), 0).astype(F32)
    pos = []
    for e_lane, rank_lane in ((RT_E1, RT_RANK1), (RT_E2, RT_RANK2)):
        seg = jnp.sum(jnp.where(expert == rt_t[e_lane:e_lane + 1, :], offs_ref[...], 0.0), axis=0, keepdims=True)
        pos.append((seg + rt_t[rank_lane:rank_lane + 1, :]) * SUBLANES)
    pos = jnp.concatenate(pos, axis=0).astype(jnp.int32)
    pos_ref[...] = pos
    pos_vmem[...] = jnp.concatenate([pos, jnp.zeros((SUBLANES - 2, db), jnp.int32)], axis=0)
    to_smem = pltpu.make_async_copy(pos_vmem, pos_smem, sem_pos)
    to_smem.start()
    to_smem.wait()

    def body(k, carry):
        src = h_ref.at[pl.ds(pl.multiple_of(k * SUBLANES, SUBLANES), SUBLANES), :]
        for slot in range(2):
            dst_row = pl.multiple_of(pos_smem[slot, k], SUBLANES)
            pltpu.make_async_copy(src, xs_hbm.at[pl.ds(dst_row, SUBLANES), :], sem).start(priority=slot)
        return carry
    lax.fori_loop(0, db, body, 0, unroll=GATHER_UNROLL)
    for slot in range(2):
        pltpu.make_async_copy(h_ref, xs_hbm.at[pl.ds(0, db * SUBLANES), :], sem).wait()


def _dispatch(pad_tile, n_tiles, rt, offs_b, h2_slabs, sorted_rows):
    n = rt.shape[0]
    db = DISPATCH_ROWS
    grid_spec = pltpu.PrefetchScalarGridSpec(
        num_scalar_prefetch=2,
        grid=(n // db,),
        in_specs=[pl.BlockSpec((db, LANES), lambda i, pt, nt: (i, 0)),
                  pl.BlockSpec(offs_b.shape, lambda i, pt, nt: (0, 0)),
                  pl.BlockSpec((db * SUBLANES, LANES), lambda i, pt, nt: (i, 0))],
        out_specs=[pl.BlockSpec((None, 2, db), lambda i, pt, nt: (i, 0, 0)),
                   pl.BlockSpec(memory_space=pl.ANY)],
        scratch_shapes=[pltpu.VMEM((EXPERT_ROWS * SUBLANES, LANES), F32),
                        pltpu.VMEM((SUBLANES, db), jnp.int32), pltpu.SMEM((SUBLANES, db), jnp.int32),
                        pltpu.SemaphoreType.DMA(()), pltpu.SemaphoreType.DMA(()), pltpu.SemaphoreType.DMA(())],
    )
    return pl.pallas_call(
        _dispatch_kernel,
        grid_spec=grid_spec,
        out_shape=[jax.ShapeDtypeStruct((n // db, 2, db), jnp.int32),
                   jax.ShapeDtypeStruct((sorted_rows * SUBLANES, LANES), F32)],
        compiler_params=_params("arbitrary"),
        name="dispatch",
    )(pad_tile, n_tiles, rt, offs_b, h2_slabs)


def _experts_kernel(tile_expert_ref, n_tiles_ref, x_ref, wg_ref, wu_ref, wd_ref, y_ref, wg_s, wu_s, wd_s):
    i = pl.program_id(0)
    rows = EXPERT_ROWS
    n_tiles = n_tiles_ref[0]

    @pl.when(i < n_tiles)
    def _():
        @pl.when((i == 0) | (tile_expert_ref[i] != tile_expert_ref[jnp.maximum(i - 1, 0)]))
        def _():
            wg_s[...] = wg_ref[...].astype(BF16)
            wu_s[...] = wu_ref[...].astype(BF16)
            wd_s[...] = wd_ref[...].astype(BF16)

        x = _load_slabs(x_ref, rows).astype(BF16)
        gate = jnp.dot(x, wg_s[...], preferred_element_type=F32)
        up = jnp.dot(x, wu_s[...], preferred_element_type=F32)
        hid = (gate * jax.nn.sigmoid(gate) * up).astype(BF16)
        _store_slabs(y_ref, jnp.dot(hid, wd_s[...], preferred_element_type=F32))

    @pl.when(i >= n_tiles)
    def _():
        y_ref[...] = jnp.zeros_like(y_ref)


def _experts(tile_expert, n_tiles, xs, wg, wu, wd, layer):
    rows = EXPERT_ROWS
    steps = tile_expert.shape[0]
    _, _, d, dff = wg.shape
    assert d == SUBLANES * LANES
    w_idx = lambda i, te, nt: (layer, te[i], 0, 0)
    grid_spec = pltpu.PrefetchScalarGridSpec(
        num_scalar_prefetch=2,
        grid=(steps,),
        in_specs=[pl.BlockSpec((rows * SUBLANES, LANES), lambda i, te, nt: (jnp.minimum(i, nt[0] - 1), 0)),
                  pl.BlockSpec((None, None, d, dff), w_idx),
                  pl.BlockSpec((None, None, d, dff), w_idx),
                  pl.BlockSpec((None, None, dff, d), w_idx)],
        out_specs=pl.BlockSpec((rows * SUBLANES, LANES), lambda i, te, nt: (i, 0)),
        scratch_shapes=[pltpu.VMEM((d, dff), BF16), pltpu.VMEM((d, dff), BF16), pltpu.VMEM((dff, d), BF16)],
    )
    return pl.pallas_call(
        _experts_kernel,
        grid_spec=grid_spec,
        out_shape=jax.ShapeDtypeStruct((steps * rows * SUBLANES, LANES), F32),
        compiler_params=_params("arbitrary"),
        name="experts",
    )(tile_expert, n_tiles, xs, wg, wu, wd)


def _start_slab_gather(src_hbm, dst, sem, row_ref, base, count):
    def body(k, carry):
        _slab_copy(src_hbm, dst, sem, row_ref[base + k], k).start()
        return carry
    lax.fori_loop(0, count, body, 0, unroll=GATHER_UNROLL)


def _combine_kernel(pos_ref, ys_hbm, x1_ref, rt_ref, g_ref, o_ref, buf, sem, *, final_norm):
    i = pl.program_id(0)
    rows = COMBINE_ROWS
    n_steps = pl.num_programs(0)
    slot = i % 2

    def start(step, s):
        for j in range(2):
            _start_slab_gather(ys_hbm, buf.at[s, j], sem.at[s], pos_ref, (step * 2 + j) * rows, rows)

    @pl.when(i == 0)
    def _():
        start(0, 0)

    @pl.when(i + 1 < n_steps)
    def _():
        start(i + 1, 1 - slot)

    _wait_slabs(ys_hbm, buf.at[slot, 0], sem.at[slot])
    _wait_slabs(ys_hbm, buf.at[slot, 1], sem.at[slot])
    rt = rt_ref[...]
    lane = lax.broadcasted_iota(jnp.int32, rt.shape, 1)
    w1 = jnp.sum(jnp.where(lane == RT_W1, rt, 0.0), axis=-1, keepdims=True)
    w2 = jnp.sum(jnp.where(lane == RT_W2, rt, 0.0), axis=-1, keepdims=True)
    y = w1 * _load_slabs(buf.at[slot, 0], rows) + w2 * _load_slabs(buf.at[slot, 1], rows)
    x2 = x1_ref[...] + y
    if final_norm:
        x2 = _rms(x2, g_ref[...])
    o_ref[...] = x2


def _combine(pos, ys, x1, rt, g, final_norm):
    n, d = x1.shape
    rows = COMBINE_ROWS
    grid_spec = pltpu.PrefetchScalarGridSpec(
        num_scalar_prefetch=1,
        grid=(n // rows,),
        in_specs=[pl.BlockSpec(memory_space=pl.ANY),
                  pl.BlockSpec((rows, d), lambda i, pos: (i, 0)),
                  pl.BlockSpec((rows, LANES), lambda i, pos: (i, 0)),
                  pl.BlockSpec(g.shape, lambda i, pos: (0, 0))],
        out_specs=pl.BlockSpec((rows, d), lambda i, pos: (i, 0)),
        scratch_shapes=[pltpu.VMEM((2, 2, rows * SUBLANES, LANES), F32), pltpu.SemaphoreType.DMA((2,))],
    )
    return pl.pallas_call(
        functools.partial(_combine_kernel, final_norm=final_norm),
        grid_spec=grid_spec,
        out_shape=jax.ShapeDtypeStruct((n, d), F32),
        compiler_params=_params("arbitrary"),
        name="combine",
    )(pos, ys, x1, rt, g)


def _dispatch_plan(counts, n):
    rows = EXPERT_ROWS
    steps = (2 * n) // rows + N_EXPERTS
    cnt = counts[0, :N_EXPERTS].astype(jnp.int32)
    tiles = (cnt + rows - 1) // rows
    tile_end = jnp.cumsum(tiles)
    tile_start = tile_end - tiles
    n_tiles = tile_end[-1]
    step_ids = jnp.arange(steps, dtype=jnp.int32)
    tile_expert = jnp.sum(tile_end[None, :] <= jnp.minimum(step_ids, n_tiles - 1)[:, None], axis=1)
    tile_expert = jnp.minimum(tile_expert, N_EXPERTS - 1).astype(jnp.int32)
    pad_tile = jnp.where(cnt % rows != 0, tile_end - 1, -1).astype(jnp.int32)
    offs_b = jnp.broadcast_to((tile_start * rows).astype(F32)[:, None], (N_EXPERTS, DISPATCH_ROWS))
    return tile_expert, n_tiles.reshape(1).astype(jnp.int32), pad_tile, offs_b, steps * rows


def _pad_lanes(a, width):
    return jnp.pad(a, ((0, 0), (0, width - a.shape[1])))


def kernel(x, norm_mix_g, w_in, w_g2_f, b_g_f, w_g2_b, b_g_b, gla_norm_g, rpb, w_out, norm_ffn_g, w_grp, b_grp,
           w_exp, b_exp, w_gate, w_up, w_down, final_norm_g):
    batch, seq, d = x.shape
    n = batch * seq
    rows = seq // GRID_W
    depth = w_in.shape[0]
    xf = x.reshape(n, d)
    c_na = 3 * NA_WIDTH
    c_qk = c_na + 2 * GLA_KEY_WIDTH
    c_v = c_qk + GLA_VAL_WIDTH
    c_r = c_v + GLA_VAL_WIDTH
    for l in range(depth):
        wl = w_in[l]
        q_scale = jnp.concatenate([jnp.full((NA_WIDTH,), NA_HEAD_DIM ** -0.5, F32),
                                   jnp.ones((2 * NA_WIDTH,), F32)])
        w_na = (wl[:, :c_na] * q_scale).astype(BF16)
        qk_scale = jnp.concatenate([jnp.full((GLA_KEY_WIDTH,), GLA_DK ** -0.5, F32),
                                    jnp.ones((GLA_KEY_WIDTH,), F32)])
        w_qk = (wl[:, c_na:c_qk] * qk_scale).astype(BF16)
        w_v = wl[:, c_qk:c_v].astype(BF16)
        w_r = wl[:, c_v:c_r].astype(BF16)
        w_lr = _pad_lanes(wl[:, c_r:], LANES).astype(BF16)
        w_g2 = jnp.zeros((LANES, 2 * GLA_KEY_WIDTH), F32)
        w_g2 = w_g2.at[:GLA_GATE_RANK, :GLA_KEY_WIDTH].set(w_g2_f[l])
        w_g2 = w_g2.at[GLA_GATE_RANK:2 * GLA_GATE_RANK, GLA_KEY_WIDTH:].set(w_g2_b[l])
        b_g = jnp.concatenate([b_g_f[l], b_g_b[l]])[None, :]
        na_qkv, gqk, gv, gr, gates = _in_proj(xf, norm_mix_g[l][None, :], w_na, w_qk, w_v, w_r, w_lr, w_g2, b_g)

        y_na = _na(na_qkv, _na_bias_table(rpb[l], rows), batch, rows)
        o_f, o_b = _gla(gqk, gv, gates, batch, seq)

        w_o = w_out[l].astype(BF16)
        w_rt = _pad_lanes(jnp.concatenate([w_exp[l], w_grp[l]], axis=1), LANES)
        w_rt_hi = w_rt.astype(BF16)
        w_rt_lo = (w_rt - w_rt_hi.astype(F32)).astype(BF16)
        w_rt3 = jnp.concatenate([w_rt_hi, w_rt_hi, w_rt_lo], axis=0)
        b_rt = _pad_lanes(jnp.concatenate([b_exp[l], b_grp[l]])[None, :], LANES)
        x1, h2_slabs, rt, counts = _out_route(y_na, o_f, o_b, gr, xf, w_o[:NA_WIDTH], w_o[NA_WIDTH:],
                                              gla_norm_g[l][None, :], norm_ffn_g[l][None, :], w_rt3, b_rt)

        tile_expert, n_tiles, pad_tile, offs_b, sorted_rows = _dispatch_plan(counts, n)
        pos, xs = _dispatch(pad_tile, n_tiles, rt, offs_b, h2_slabs, sorted_rows)
        ys = _experts(tile_expert, n_tiles, xs, w_gate, w_up, w_down, l)
        pos = pos.reshape(-1)
        last = l == depth - 1
        xf = _combine(pos, ys, x1, rt, final_norm_g[None, :], final_norm=last)
    return xf.reshape(batch, seq, d)
```

```python
import functools

import jax
import jax.numpy as jnp
from jax import lax
from jax.experimental import pallas as pl
from jax.experimental.pallas import tpu as pltpu

F32 = jnp.float32
BF16 = jnp.bfloat16

GRID_W = 64
NA_HEADS = 8
NA_HEAD_DIM = 64
NA_WIDTH = NA_HEADS * NA_HEAD_DIM
WIN_H_MAX = 8
WIN_W = 16
GLA_HEADS = 4
GLA_DK = 64
GLA_DV = 128
GLA_KEY_WIDTH = GLA_HEADS * GLA_DK
GLA_VAL_WIDTH = GLA_HEADS * GLA_DV
GLA_GATE_RANK = 16
GLA_GATE_NORMALIZER = 16.0
GLA_CHUNK = 64
N_GROUPS = 4
EXPERTS_PER_GROUP = 8
N_EXPERTS = N_GROUPS * EXPERTS_PER_GROUP
RMS_EPS = 1e-6

LANES = 128
SUBLANES = 8
VMEM_LIMIT_BYTES = 56 * 1024 * 1024

MASK_VALUE = -1e30

IN_PROJ_ROWS = 512
NA_ROW_BLOCK = 8
NA_ROW_UNROLL = 4
GLA_STEP_CHUNKS = 4
OUT_ROWS = 512
EXPERT_ROWS = 256
DISPATCH_ROWS = 512
COMBINE_ROWS = DISPATCH_ROWS
GATHER_UNROLL = 8


def _params(*sem):
    return pltpu.CompilerParams(dimension_semantics=sem, vmem_limit_bytes=VMEM_LIMIT_BYTES)


def _rms(x, g):
    return x * lax.rsqrt(jnp.mean(x * x, axis=-1, keepdims=True) + RMS_EPS) * g


def _in_proj_kernel(x_ref, g_ref, w_na_ref, w_qk_ref, w_v_ref, w_r_ref, w_lr_ref,
                    w_g2_ref, b_g_ref, na_ref, qk_ref, v_ref, r_ref, gate_ref):
    h = _rms(x_ref[...], g_ref[...]).astype(BF16)
    na_ref[...] = jnp.dot(h, w_na_ref[...], preferred_element_type=F32).astype(BF16)
    qk_ref[...] = jnp.dot(h, w_qk_ref[...], preferred_element_type=F32)
    v_ref[...] = jnp.dot(h, w_v_ref[...], preferred_element_type=F32).astype(BF16)
    r_ref[...] = jnp.dot(h, w_r_ref[...], preferred_element_type=F32)
    lr = jnp.dot(h, w_lr_ref[...], preferred_element_type=F32)
    z = jnp.dot(lr, w_g2_ref[...], preferred_element_type=F32,
                precision=lax.Precision.HIGHEST) + b_g_ref[...]
    log_sig = jnp.minimum(z, 0.0) - jnp.log(1.0 + jnp.exp(-jnp.abs(z)))
    gate_ref[...] = log_sig * (1.0 / GLA_GATE_NORMALIZER)


def _in_proj(x, g, w_na, w_qk, w_v, w_r, w_lr, w_g2, b_g):
    n, d = x.shape
    rows = IN_PROJ_ROWS
    row_spec = lambda width: pl.BlockSpec((rows, width), lambda i: (i, 0))
    full = lambda a: pl.BlockSpec(a.shape, lambda i: (0,) * a.ndim)
    return pl.pallas_call(
        _in_proj_kernel,
        grid=(n // rows,),
        in_specs=[row_spec(d), full(g), full(w_na), full(w_qk), full(w_v), full(w_r), full(w_lr),
                  full(w_g2), full(b_g)],
        out_specs=[row_spec(3 * NA_WIDTH), row_spec(2 * GLA_KEY_WIDTH), row_spec(GLA_VAL_WIDTH),
                   row_spec(GLA_VAL_WIDTH), row_spec(2 * GLA_KEY_WIDTH)],
        out_shape=[jax.ShapeDtypeStruct((n, 3 * NA_WIDTH), BF16),
                   jax.ShapeDtypeStruct((n, 2 * GLA_KEY_WIDTH), F32),
                   jax.ShapeDtypeStruct((n, GLA_VAL_WIDTH), BF16),
                   jax.ShapeDtypeStruct((n, GLA_VAL_WIDTH), F32),
                   jax.ShapeDtypeStruct((n, 2 * GLA_KEY_WIDTH), F32)],
        compiler_params=_params("arbitrary"),
        name="in_proj",
    )(x, g, w_na, w_qk, w_v, w_r, w_lr, w_g2, b_g)


def _na_bias_table(rpb, rows):
    kh = min(WIN_H_MAX, rows)
    cls = jnp.arange(kh)[:, None, None]
    i = jnp.arange(kh)[None, :, None]
    w = jnp.arange(GRID_W)[:, None, None]
    x = jnp.arange(GRID_W)[None, :, None]
    cs = jnp.clip(w - WIN_W // 2, 0, GRID_W - WIN_W)
    valid = (x >= cs) & (x < cs + WIN_W)
    row_sel = (jnp.arange(2 * WIN_H_MAX - 1)[None, None, :] == i - cls + (WIN_H_MAX - 1)).astype(F32)
    col_sel = (valid & (jnp.arange(2 * WIN_W - 1)[None, None, :] == x - w + (WIN_W - 1))).astype(F32)
    hi = lax.Precision.HIGHEST
    by_col = jnp.einsum("hrc,wxc->hrwx", rpb.astype(F32), col_sel, precision=hi)
    bias = jnp.einsum("kir,hrwx->khwix", row_sel, by_col, precision=hi)
    bias = jnp.where(valid[None, None, :, None, :, 0], bias, MASK_VALUE)
    return bias.reshape(kh, NA_HEADS // 2, 2 * GRID_W, kh * GRID_W)


def _na_kernel(q_ref, kp_ref, kc_ref, kn_ref, vp_ref, vc_ref, vn_ref, tbl_ref, o_ref,
               kbuf, vbuf, *, rows, kh):
    rb = NA_ROW_BLOCK
    j = pl.program_id(1)
    blk = rb * GRID_W
    for s, (k_src, v_src) in enumerate(((kp_ref, vp_ref), (kc_ref, vc_ref), (kn_ref, vn_ref))):
        kbuf[s * blk:(s + 1) * blk, :] = k_src[...].reshape(blk, NA_WIDTH)
        vbuf[s * blk:(s + 1) * blk, :] = v_src[...].reshape(blk, NA_WIDTH)
    lane = lax.broadcasted_iota(jnp.int32, (GRID_W, LANES), 1)
    first = lane < NA_HEAD_DIM

    def row_body(lr, carry):
        r = j * rb + lr
        start = jnp.clip(r - kh // 2, 0, rows - kh)
        cls = r - start
        local = pl.multiple_of((start - (j - 1) * rb) * GRID_W, GRID_W)
        q_row = q_ref[lr]
        k_win = kbuf[pl.ds(local, kh * GRID_W), :]
        v_win = vbuf[pl.ds(local, kh * GRID_W), :]
        pairs = [slice(p * LANES, (p + 1) * LANES) for p in range(NA_HEADS // 2)]
        scores = []
        for sl in pairs:
            q_pair = q_row[:, sl]
            zero = jnp.zeros_like(q_pair)
            q_bd = jnp.concatenate([jnp.where(first, q_pair, zero), jnp.where(first, zero, q_pair)], axis=0)
            scores.append(lax.dot_general(q_bd, k_win[:, sl], (((1,), (1,)), ((), ())),
                                          preferred_element_type=F32))
        probs, denoms = [], []
        for p, s in enumerate(scores):
            s = s + tbl_ref[cls, p]
            e = jnp.exp(s - jnp.max(s, axis=-1, keepdims=True))
            denoms.append(jnp.sum(e, axis=-1, keepdims=True))
            probs.append(e.astype(BF16))
        outs = []
        for sl, e, denom in zip(pairs, probs, denoms):
            o = jnp.dot(e, v_win[:, sl], preferred_element_type=F32) / denom
            outs.append(jnp.where(first, o[:GRID_W], o[GRID_W:]))
        o_ref[lr] = jnp.concatenate(outs, axis=-1).astype(o_ref.dtype)
        return carry

    lax.fori_loop(0, rb, row_body, 0, unroll=NA_ROW_UNROLL)


def _na(na_qkv, tbl, batch, rows):
    kh = min(WIN_H_MAX, rows)
    rb = NA_ROW_BLOCK
    nblk = rows // rb
    x4 = na_qkv.reshape(batch, rows, GRID_W, 3 * NA_WIDTH)
    blk = (None, rb, GRID_W, NA_WIDTH)
    prev = lambda j: jnp.maximum(j - 1, 0)
    nxt = lambda j: jnp.minimum(j + 1, nblk - 1)
    specs = [pl.BlockSpec(blk, lambda b, j: (b, j, 0, 0))]
    for col in (1, 2):
        specs += [pl.BlockSpec(blk, lambda b, j, col=col: (b, prev(j), 0, col)),
                  pl.BlockSpec(blk, lambda b, j, col=col: (b, j, 0, col)),
                  pl.BlockSpec(blk, lambda b, j, col=col: (b, nxt(j), 0, col))]
    specs.append(pl.BlockSpec(tbl.shape, lambda b, j: (0, 0, 0, 0)))
    out = pl.pallas_call(
        functools.partial(_na_kernel, rows=rows, kh=kh),
        grid=(batch, nblk),
        in_specs=specs,
        out_specs=pl.BlockSpec(blk, lambda b, j: (b, j, 0, 0)),
        out_shape=jax.ShapeDtypeStruct((batch, rows, GRID_W, NA_WIDTH), BF16),
        scratch_shapes=[pltpu.VMEM((3 * rb * GRID_W, NA_WIDTH), BF16),
                        pltpu.VMEM((3 * rb * GRID_W, NA_WIDTH), BF16)],
        compiler_params=_params("arbitrary", "arbitrary"),
        name="na",
    )(x4, x4, x4, x4, x4, x4, x4, tbl)
    return out.reshape(batch * rows * GRID_W, NA_WIDTH)


def _block_diag_mask(row_block, col_block, nblocks):
    shape = (row_block * nblocks, col_block * nblocks)
    r = lax.shift_right_logical(lax.broadcasted_iota(jnp.int32, shape, 0), row_block.bit_length() - 1)
    c = lax.shift_right_logical(lax.broadcasted_iota(jnp.int32, shape, 1), col_block.bit_length() - 1)
    return r == c


def _split_bf16x3(x):
    hi = x.astype(BF16)
    rest = x - hi.astype(F32)
    mid = rest.astype(BF16)
    lo = (rest - mid.astype(F32)).astype(BF16)
    return hi, mid, lo


def _gla_direction(qk_ref, v_ref, g_ref, *, backward):
    c = GLA_CHUNK
    nc = GLA_STEP_CHUNKS
    kw = GLA_KEY_WIDTH
    order = list(reversed(range(nc))) if backward else list(range(nc))
    chunk = lambda a, n: a[n * c:(n + 1) * c]

    def stage_cumsum():
        step = nc * c
        ti = lax.broadcasted_iota(jnp.int32, (step, step), 0)
        tj = lax.broadcasted_iota(jnp.int32, (step, step), 1)
        same_chunk = lax.shift_right_logical(ti, c.bit_length() - 1) == lax.shift_right_logical(tj, c.bit_length() - 1)
        tri = (tj >= ti) if backward else (tj <= ti)
        cum = jnp.where(same_chunk & tri, 1.0, 0.0).astype(BF16)
        return sum(jnp.dot(cum, piece, preferred_element_type=F32) for piece in _split_bf16x3(g_ref[...]))

    def stage_decays(b):
        ref_row, last_row = (c // 2, 0) if backward else (c // 2 - 1, c - 1)
        rows_of = lambda r: jnp.concatenate(
            [jnp.broadcast_to(b[n * c + r:n * c + r + 1, :], (c, kw)) for n in range(nc)], axis=0)
        b_ref, b_last = rows_of(ref_row), rows_of(last_row)
        q = qk_ref[:, :kw]
        k = qk_ref[:, kw:]
        q_rel = (q * jnp.exp(b - b_ref)).astype(BF16)
        k_rel = (k * jnp.exp(b_ref - b)).astype(BF16)
        k_dec = k * jnp.exp(b_last - b)
        q_dec = (q * jnp.exp(b)).astype(BF16)
        decay = jnp.exp(b_last)
        return q_rel, k_rel, k_dec, q_dec, decay

    def stage_scores(q_rel, k_rel):
        kk_mask = _block_diag_mask(c, GLA_DK, GLA_HEADS)
        out = []
        for n in range(nc):
            k_bd = jnp.where(kk_mask, jnp.concatenate([chunk(k_rel, n)] * GLA_HEADS, axis=0), jnp.zeros((), BF16))
            out.append(lax.dot_general(chunk(q_rel, n), k_bd, (((1,), (1,)), ((), ())),
                                       preferred_element_type=F32))
        return out

    def stage_intra(scores, k_dec, decay):
        si = lax.broadcasted_iota(jnp.int32, (c, c * GLA_HEADS), 0)
        sj = lax.broadcasted_iota(jnp.int32, (c, c * GLA_HEADS), 1) & (c - 1)
        keep = (sj > si) if backward else (sj <= si)
        kv_mask = _block_diag_mask(c, GLA_DV, GLA_HEADS)
        vk_mask = _block_diag_mask(GLA_DK, GLA_DV, 2)
        o_intra, upd, decay_col = [], [], []
        for n in range(nc):
            v = v_ref[n * c:(n + 1) * c, :]
            p = jnp.where(keep, scores[n], 0.0).astype(BF16)
            v_bd = jnp.where(kv_mask, jnp.concatenate([v] * GLA_HEADS, axis=0), jnp.zeros((), BF16))
            o_intra.append(jnp.dot(p, v_bd, preferred_element_type=F32))
            k_dec_t = chunk(k_dec, n).T.astype(BF16)
            pairs = []
            for hp in range(GLA_HEADS // 2):
                kv = jnp.dot(k_dec_t[hp * 2 * GLA_DK:(hp + 1) * 2 * GLA_DK],
                             v[:, hp * 2 * GLA_DV:(hp + 1) * 2 * GLA_DV], preferred_element_type=F32)
                pairs.append(jnp.where(vk_mask, kv, 0.0))
            upd.append(pairs)
            decay_col.append(chunk(decay, n).T[:, :1])
        return o_intra, upd, decay_col

    def scan_step(state, idx, q_dec, o_intra, upd, decay_col, out_ref):
        n = order[idx]
        q_n = chunk(q_dec, n)
        pw = 2 * GLA_DK
        o_inter = jnp.concatenate(
            [jnp.dot(q_n[:, hp * pw:(hp + 1) * pw], s.astype(BF16), preferred_element_type=F32)
             for hp, s in enumerate(state)], axis=1)
        out_ref[n * c:(n + 1) * c, :] = o_intra[n] + o_inter
        return [s * decay_col[n][hp * pw:(hp + 1) * pw] + u for hp, (s, u) in enumerate(zip(state, upd[n]))]

    return stage_cumsum, stage_decays, stage_scores, stage_intra, scan_step


def _gla_kernel(qk_f_ref, v_f_ref, g_f_ref, qk_b_ref, v_b_ref, g_b_ref, o_f_ref, o_b_ref,
                st_f, st_b):
    @pl.when(pl.program_id(1) == 0)
    def _():
        st_f[...] = jnp.zeros_like(st_f)
        st_b[...] = jnp.zeros_like(st_b)

    dirs = (_gla_direction(qk_f_ref, v_f_ref, g_f_ref, backward=False),
            _gla_direction(qk_b_ref, v_b_ref, g_b_ref, backward=True))
    b = [d[0]() for d in dirs]
    dec = [d[1](x) for d, x in zip(dirs, b)]
    scores = [d[2](x[0], x[1]) for d, x in zip(dirs, dec)]
    intra = [d[3](s, x[2], x[4]) for d, s, x in zip(dirs, scores, dec)]
    pairs = range(GLA_HEADS // 2)
    states = [[st[hp] for hp in pairs] for st in (st_f, st_b)]
    for idx in range(GLA_STEP_CHUNKS):
        for j, (d, x, y, out_ref) in enumerate(zip(dirs, dec, intra, (o_f_ref, o_b_ref))):
            states[j] = d[4](states[j], idx, x[3], *y, out_ref)
    for st, state in zip((st_f, st_b), states):
        for hp in pairs:
            st[hp] = state[hp]


def _gla(qk, v, gates, batch, seq):
    step = GLA_STEP_CHUNKS * GLA_CHUNK
    nblk = seq // step
    qk3 = qk.reshape(batch, seq, 2 * GLA_KEY_WIDTH)
    v3 = v.reshape(batch, seq, GLA_VAL_WIDTH)
    g3 = gates.reshape(batch, seq, 2 * GLA_KEY_WIDTH)
    fwd = lambda b, n: (b, n, 0)
    bwd = lambda b, n: (b, nblk - 1 - n, 0)
    bwd_gate = lambda b, n: (b, nblk - 1 - n, 1)
    o_f, o_b = pl.pallas_call(
        _gla_kernel,
        grid=(batch, nblk),
        in_specs=[pl.BlockSpec((None, step, 2 * GLA_KEY_WIDTH), fwd),
                  pl.BlockSpec((None, step, GLA_VAL_WIDTH), fwd),
                  pl.BlockSpec((None, step, GLA_KEY_WIDTH), fwd),
                  pl.BlockSpec((None, step, 2 * GLA_KEY_WIDTH), bwd),
                  pl.BlockSpec((None, step, GLA_VAL_WIDTH), bwd),
                  pl.BlockSpec((None, step, GLA_KEY_WIDTH), bwd_gate)],
        out_specs=[pl.BlockSpec((None, step, GLA_VAL_WIDTH), fwd),
                   pl.BlockSpec((None, step, GLA_VAL_WIDTH), bwd)],
        out_shape=[jax.ShapeDtypeStruct((batch, seq, GLA_VAL_WIDTH), F32)] * 2,
        scratch_shapes=[pltpu.VMEM((GLA_HEADS // 2, 2 * GLA_DK, 2 * GLA_DV), F32),
                        pltpu.VMEM((GLA_HEADS // 2, 2 * GLA_DK, 2 * GLA_DV), F32)],
        compiler_params=_params("arbitrary", "arbitrary"),
        name="gla",
    )(qk3, v3, g3, qk3, v3, g3)
    return o_f.reshape(batch * seq, GLA_VAL_WIDTH), o_b.reshape(batch * seq, GLA_VAL_WIDTH)


def _store_slabs(ref, x):
    rows = x.shape[0]
    for s in range(SUBLANES):
        ref[pl.ds(s, rows, stride=SUBLANES), :] = x[:, s * LANES:(s + 1) * LANES]


def _load_slabs(ref, rows):
    return jnp.concatenate([ref[pl.ds(s, rows, stride=SUBLANES), :] for s in range(SUBLANES)], axis=1)


def _slab_copy(src_hbm, dst, sem, src_row, k):
    dst_row = k * SUBLANES if isinstance(k, int) else pl.multiple_of(k * SUBLANES, SUBLANES)
    return pltpu.make_async_copy(src_hbm.at[pl.ds(pl.multiple_of(src_row, SUBLANES), SUBLANES), :],
                                 dst.at[pl.ds(dst_row, SUBLANES), :], sem)


def _wait_slabs(src_hbm, dst, sem):
    pltpu.make_async_copy(src_hbm.at[pl.ds(0, dst.shape[0]), :], dst, sem).wait()


RT_E1, RT_E2, RT_RANK1, RT_RANK2, RT_W1, RT_W2 = range(6)
GROUP_LANE0 = N_EXPERTS


def _out_route_kernel(na_ref, of_ref, ob_ref, r_ref, x_ref, w_na_ref, w_gla_ref, gn_ref, fn_ref,
                      w_rt_ref, b_rt_ref, x1_ref, h2_ref, rt_ref, cnt_ref, carry):
    @pl.when(pl.program_id(0) == 0)
    def _():
        carry[...] = jnp.zeros_like(carry)

    rows = x_ref.shape[0]
    o = of_ref[...] + ob_ref[...]
    r = r_ref[...]
    parts = []
    for h in range(GLA_HEADS):
        sl = slice(h * GLA_DV, (h + 1) * GLA_DV)
        parts.append(_rms(o[:, sl], gn_ref[...]) * (r[:, sl] * jax.nn.sigmoid(r[:, sl])))
    y_gla = jnp.concatenate(parts, axis=-1).astype(BF16)
    x1 = (x_ref[...] + jnp.dot(na_ref[...], w_na_ref[...], preferred_element_type=F32)
          + jnp.dot(y_gla, w_gla_ref[...], preferred_element_type=F32))
    x1_ref[...] = x1
    h2 = _rms(x1, fn_ref[...])
    _store_slabs(h2_ref, h2)

    h_hi = h2.astype(BF16)
    h_lo = (h2 - h_hi.astype(F32)).astype(BF16)
    logits = jnp.dot(jnp.concatenate([h_hi, h_lo, h_hi], axis=1), w_rt_ref[...],
                     preferred_element_type=F32) + b_rt_ref[...]
    lane_i = lax.broadcasted_iota(jnp.int32, (rows, LANES), 1)
    lane = lane_i.astype(F32)
    lane_grp = lax.shift_right_logical(lane_i, EXPERTS_PER_GROUP.bit_length() - 1).astype(F32)
    neg = jnp.float32(-jnp.inf)
    no_lane = jnp.float32(LANES)
    is_grp = (lane_i >= GROUP_LANE0) & (lane_i < GROUP_LANE0 + N_GROUPS)
    g_logit = jnp.where(is_grp, logits, neg)
    g_max = jnp.max(g_logit, axis=-1, keepdims=True)
    g_sel = jnp.min(jnp.where(is_grp & (g_logit == g_max), lane, no_lane), axis=-1, keepdims=True) - GROUP_LANE0
    grp_w = 1.0 / jnp.sum(jnp.where(is_grp, jnp.exp(g_logit - g_max), 0.0), axis=-1, keepdims=True)
    in_grp = (lane_i < N_EXPERTS) & (lane_grp == g_sel)
    e_logit = jnp.where(in_grp, logits, neg)
    v1 = jnp.max(e_logit, axis=-1, keepdims=True)
    i1 = jnp.min(jnp.where(in_grp & (e_logit == v1), lane, no_lane), axis=-1, keepdims=True)
    rest = in_grp & (lane != i1)
    e_logit2 = jnp.where(rest, logits, neg)
    v2 = jnp.max(e_logit2, axis=-1, keepdims=True)
    i2 = jnp.min(jnp.where(rest & (e_logit2 == v2), lane, no_lane), axis=-1, keepdims=True)
    t = jnp.exp(v2 - v1)
    w1 = grp_w / (1.0 + t)
    w2 = grp_w * t / (1.0 + t)

    sel1 = lane == i1
    sel2 = lane == i2
    onehot = jnp.where(sel1 | sel2, 1.0, 0.0)
    ti = lax.broadcasted_iota(jnp.int32, (rows, rows), 0)
    tj = lax.broadcasted_iota(jnp.int32, (rows, rows), 1)
    before = jnp.where(tj < ti, 1.0, 0.0).astype(BF16)
    ranks = jnp.dot(before, onehot.astype(BF16), preferred_element_type=F32) + carry[0:1, :]
    rank1 = jnp.sum(jnp.where(sel1, ranks, 0.0), axis=-1, keepdims=True)
    rank2 = jnp.sum(jnp.where(sel2, ranks, 0.0), axis=-1, keepdims=True)
    new_carry = carry[0:1, :] + jnp.sum(onehot, axis=0, keepdims=True)
    carry[...] = jnp.broadcast_to(new_carry, carry.shape)
    cnt_ref[...] = jnp.broadcast_to(new_carry, cnt_ref.shape)

    rec = jnp.zeros((rows, LANES), F32)
    for idx, val in ((RT_E1, i1), (RT_E2, i2), (RT_RANK1, rank1), (RT_RANK2, rank2), (RT_W1, w1), (RT_W2, w2)):
        rec = jnp.where(lane_i == idx, val, rec)
    rt_ref[...] = rec


def _out_route(y_na, o_f, o_b, r, x, w_na, w_gla, gn, fn, w_rt, b_rt):
    n, d = x.shape
    rows = OUT_ROWS
    row_spec = lambda width: pl.BlockSpec((rows, width), lambda i: (i, 0))
    full = lambda a: pl.BlockSpec(a.shape, lambda i: (0,) * a.ndim)
    return pl.pallas_call(
        _out_route_kernel,
        grid=(n // rows,),
        in_specs=[row_spec(NA_WIDTH), row_spec(GLA_VAL_WIDTH), row_spec(GLA_VAL_WIDTH),
                  row_spec(GLA_VAL_WIDTH), row_spec(d), full(w_na), full(w_gla), full(gn), full(fn),
                  full(w_rt), full(b_rt)],
        out_specs=[row_spec(d), pl.BlockSpec((rows * SUBLANES, LANES), lambda i: (i, 0)), row_spec(LANES),
                   pl.BlockSpec((SUBLANES, LANES), lambda i: (0, 0))],
        out_shape=[jax.ShapeDtypeStruct((n, d), F32), jax.ShapeDtypeStruct((n * SUBLANES, LANES), F32),
                   jax.ShapeDtypeStruct((n, LANES), F32), jax.ShapeDtypeStruct((SUBLANES, LANES), F32)],
        scratch_shapes=[pltpu.VMEM((8, LANES), F32)],
        compiler_params=_params("arbitrary"),
        name="out_route",
    )(y_na, o_f, o_b, r, x, w_na, w_gla, gn, fn, w_rt, b_rt)


def _dispatch_kernel(pad_tile_ref, n_tiles_ref, rt_ref, offs_ref, h_ref, pos_ref, xs_hbm,
                     zeros, pos_vmem, pos_smem, sem_zero, sem_pos, sem):
    i = pl.program_id(0)
    db = DISPATCH_ROWS
    tile = EXPERT_ROWS * SUBLANES
    max_tiles = xs_hbm.shape[0] // tile

    @pl.when(i == 0)
    def _():
        zeros[...] = jnp.zeros_like(zeros)

        def zero_copy(t):
            return pltpu.make_async_copy(zeros, xs_hbm.at[pl.ds(pl.multiple_of(t * tile, tile), tile), :], sem_zero)

        def for_each_zero_tile(fn):
            def padded(e, carry):
                @pl.when(pad_tile_ref[e] >= 0)
                def _():
                    fn(zero_copy(pad_tile_ref[e]))
                return carry
            lax.fori_loop(0, N_EXPERTS, padded, 0)

            def tail(t, carry):
                fn(zero_copy(t))
                return carry
            lax.fori_loop(n_tiles_ref[0], max_tiles, tail, 0)

        for_each_zero_tile(lambda cp: cp.start())
        for_each_zero_tile(lambda cp: cp.wait())

    rt_t = rt_ref[...].T
    expert = lax.broadcasted_iota(jnp.int32, (N_EXPERTS, db), 0).astype(F32)
    pos = []
    for e_lane, rank_lane in ((RT_E1, RT_RANK1), (RT_E2, RT_RANK2)):
        seg = jnp.sum(jnp.where(expert == rt_t[e_lane:e_lane + 1, :], offs_ref[...], 0.0), axis=0, keepdims=True)
        pos.append((seg + rt_t[rank_lane:rank_lane + 1, :]) * SUBLANES)
    pos = jnp.concatenate(pos, axis=0).astype(jnp.int32)
    pos_ref[...] = pos
    pos_vmem[...] = jnp.concatenate([pos, jnp.zeros((SUBLANES - 2, db), jnp.int32)], axis=0)
    to_smem = pltpu.make_async_copy(pos_vmem, pos_smem, sem_pos)
    to_smem.start()
    to_smem.wait()

    def body(k, carry):
        src = h_ref.at[pl.ds(pl.multiple_of(k * SUBLANES, SUBLANES), SUBLANES), :]
        for slot in range(2):
            dst_row = pl.multiple_of(pos_smem[slot, k], SUBLANES)
            pltpu.make_async_copy(src, xs_hbm.at[pl.ds(dst_row, SUBLANES), :], sem).start(priority=slot)
        return carry
    lax.fori_loop(0, db, body, 0, unroll=GATHER_UNROLL)
    for slot in range(2):
        pltpu.make_async_copy(h_ref, xs_hbm.at[pl.ds(0, db * SUBLANES), :], sem).wait()


def _dispatch(pad_tile, n_tiles, rt, offs_b, h2_slabs, sorted_rows):
    n = rt.shape[0]
    db = DISPATCH_ROWS
    grid_spec = pltpu.PrefetchScalarGridSpec(
        num_scalar_prefetch=2,
        grid=(n // db,),
        in_specs=[pl.BlockSpec((db, LANES), lambda i, pt, nt: (i, 0)),
                  pl.BlockSpec(offs_b.shape, lambda i, pt, nt: (0, 0)),
                  pl.BlockSpec((db * SUBLANES, LANES), lambda i, pt, nt: (i, 0))],
        out_specs=[pl.BlockSpec((None, 2, db), lambda i, pt, nt: (i, 0, 0)),
                   pl.BlockSpec(memory_space=pl.ANY)],
        scratch_shapes=[pltpu.VMEM((EXPERT_ROWS * SUBLANES, LANES), F32),
                        pltpu.VMEM((SUBLANES, db), jnp.int32), pltpu.SMEM((SUBLANES, db), jnp.int32),
                        pltpu.SemaphoreType.DMA(()), pltpu.SemaphoreType.DMA(()), pltpu.SemaphoreType.DMA(())],
    )
    return pl.pallas_call(
        _dispatch_kernel,
        grid_spec=grid_spec,
        out_shape=[jax.ShapeDtypeStruct((n // db, 2, db), jnp.int32),
                   jax.ShapeDtypeStruct((sorted_rows * SUBLANES, LANES), F32)],
        compiler_params=_params("arbitrary"),
        name="dispatch",
    )(pad_tile, n_tiles, rt, offs_b, h2_slabs)


def _experts_kernel(tile_expert_ref, n_tiles_ref, x_ref, wg_ref, wu_ref, wd_ref, y_ref, wg_s, wu_s, wd_s):
    i = pl.program_id(0)
    rows = EXPERT_ROWS
    n_tiles = n_tiles_ref[0]

    @pl.when(i < n_tiles)
    def _():
        @pl.when((i == 0) | (tile_expert_ref[i] != tile_expert_ref[jnp.maximum(i - 1, 0)]))
        def _():
            wg_s[...] = wg_ref[...].astype(BF16)
            wu_s[...] = wu_ref[...].astype(BF16)
            wd_s[...] = wd_ref[...].astype(BF16)

        x = _load_slabs(x_ref, rows).astype(BF16)
        gate = jnp.dot(x, wg_s[...], preferred_element_type=F32)
        up = jnp.dot(x, wu_s[...], preferred_element_type=F32)
        hid = (gate * jax.nn.sigmoid(gate) * up).astype(BF16)
        _store_slabs(y_ref, jnp.dot(hid, wd_s[...], preferred_element_type=F32))

    @pl.when(i >= n_tiles)
    def _():
        y_ref[...] = jnp.zeros_like(y_ref)


def _experts(tile_expert, n_tiles, xs, wg, wu, wd, layer):
    rows = EXPERT_ROWS
    steps = tile_expert.shape[0]
    _, _, d, dff = wg.shape
    assert d == SUBLANES * LANES
    w_idx = lambda i, te, nt: (layer, te[i], 0, 0)
    grid_spec = pltpu.PrefetchScalarGridSpec(
        num_scalar_prefetch=2,
        grid=(steps,),
        in_specs=[pl.BlockSpec((rows * SUBLANES, LANES), lambda i, te, nt: (jnp.minimum(i, nt[0] - 1), 0)),
                  pl.BlockSpec((None, None, d, dff), w_idx),
                  pl.BlockSpec((None, None, d, dff), w_idx),
                  pl.BlockSpec((None, None, dff, d), w_idx)],
        out_specs=pl.BlockSpec((rows * SUBLANES, LANES), lambda i, te, nt: (i, 0)),
        scratch_shapes=[pltpu.VMEM((d, dff), BF16), pltpu.VMEM((d, dff), BF16), pltpu.VMEM((dff, d), BF16)],
    )
    return pl.pallas_call(
        _experts_kernel,
        grid_spec=grid_spec,
        out_shape=jax.ShapeDtypeStruct((steps * rows * SUBLANES, LANES), F32),
        compiler_params=_params("arbitrary"),
        name="experts",
    )(tile_expert, n_tiles, xs, wg, wu, wd)


def _start_slab_gather(src_hbm, dst, sem, row_ref, base, count):
    def body(k, carry):
        _slab_copy(src_hbm, dst, sem, row_ref[base + k], k).start()
        return carry
    lax.fori_loop(0, count, body, 0, unroll=GATHER_UNROLL)


def _combine_kernel(pos_ref, ys_hbm, x1_ref, rt_ref, g_ref, o_ref, buf, sem, *, final_norm):
    i = pl.program_id(0)
    rows = COMBINE_ROWS
    n_steps = pl.num_programs(0)
    slot = i % 2

    def start(step, s):
        for j in range(2):
            _start_slab_gather(ys_hbm, buf.at[s, j], sem.at[s], pos_ref, (step * 2 + j) * rows, rows)

    @pl.when(i == 0)
    def _():
        start(0, 0)

    @pl.when(i + 1 < n_steps)
    def _():
        start(i + 1, 1 - slot)

    _wait_slabs(ys_hbm, buf.at[slot, 0], sem.at[slot])
    _wait_slabs(ys_hbm, buf.at[slot, 1], sem.at[slot])
    rt = rt_ref[...]
    lane = lax.broadcasted_iota(jnp.int32, rt.shape, 1)
    w1 = jnp.sum(jnp.where(lane == RT_W1, rt, 0.0), axis=-1, keepdims=True)
    w2 = jnp.sum(jnp.where(lane == RT_W2, rt, 0.0), axis=-1, keepdims=True)
    y = w1 * _load_slabs(buf.at[slot, 0], rows) + w2 * _load_slabs(buf.at[slot, 1], rows)
    x2 = x1_ref[...] + y
    if final_norm:
        x2 = _rms(x2, g_ref[...])
    o_ref[...] = x2


def _combine(pos, ys, x1, rt, g, final_norm):
    n, d = x1.shape
    rows = COMBINE_ROWS
    grid_spec = pltpu.PrefetchScalarGridSpec(
        num_scalar_prefetch=1,
        grid=(n // rows,),
        in_specs=[pl.BlockSpec(memory_space=pl.ANY),
                  pl.BlockSpec((rows, d), lambda i, pos: (i, 0)),
                  pl.BlockSpec((rows, LANES), lambda i, pos: (i, 0)),
                  pl.BlockSpec(g.shape, lambda i, pos: (0, 0))],
        out_specs=pl.BlockSpec((rows, d), lambda i, pos: (i, 0)),
        scratch_shapes=[pltpu.VMEM((2, 2, rows * SUBLANES, LANES), F32), pltpu.SemaphoreType.DMA((2,))],
    )
    return pl.pallas_call(
        functools.partial(_combine_kernel, final_norm=final_norm),
        grid_spec=grid_spec,
        out_shape=jax.ShapeDtypeStruct((n, d), F32),
        compiler_params=_params("arbitrary"),
        name="combine",
    )(pos, ys, x1, rt, g)


def _dispatch_plan(counts, n):
    rows = EXPERT_ROWS
    steps = (2 * n) // rows + N_EXPERTS
    cnt = counts[0, :N_EXPERTS].astype(jnp.int32)
    tiles = (cnt + rows - 1) // rows
    tile_end = jnp.cumsum(tiles)
    tile_start = tile_end - tiles
    n_tiles = tile_end[-1]
    step_ids = jnp.arange(steps, dtype=jnp.int32)
    tile_expert = jnp.sum(tile_end[None, :] <= jnp.minimum(step_ids, n_tiles - 1)[:, None], axis=1)
    tile_expert = jnp.minimum(tile_expert, N_EXPERTS - 1).astype(jnp.int32)
    pad_tile = jnp.where(cnt % rows != 0, tile_end - 1, -1).astype(jnp.int32)
    offs_b = jnp.broadcast_to((tile_start * rows).astype(F32)[:, None], (N_EXPERTS, DISPATCH_ROWS))
    return tile_expert, n_tiles.reshape(1).astype(jnp.int32), pad_tile, offs_b, steps * rows


def _pad_lanes(a, width):
    return jnp.pad(a, ((0, 0), (0, width - a.shape[1])))


def kernel(x, norm_mix_g, w_in, w_g2_f, b_g_f, w_g2_b, b_g_b, gla_norm_g, rpb, w_out, norm_ffn_g, w_grp, b_grp,
           w_exp, b_exp, w_gate, w_up, w_down, final_norm_g):
    batch, seq, d = x.shape
    n = batch * seq
    rows = seq // GRID_W
    depth = w_in.shape[0]
    xf = x.reshape(n, d)
    c_na = 3 * NA_WIDTH
    c_qk = c_na + 2 * GLA_KEY_WIDTH
    c_v = c_qk + GLA_VAL_WIDTH
    c_r = c_v + GLA_VAL_WIDTH
    for l in range(depth):
        wl = w_in[l]
        q_scale = jnp.concatenate([jnp.full((NA_WIDTH,), NA_HEAD_DIM ** -0.5, F32),
                                   jnp.ones((2 * NA_WIDTH,), F32)])
        w_na = (wl[:, :c_na] * q_scale).astype(BF16)
        qk_scale = jnp.concatenate([jnp.full((GLA_KEY_WIDTH,), GLA_DK ** -0.5, F32),
                                    jnp.ones((GLA_KEY_WIDTH,), F32)])
        w_qk = (wl[:, c_na:c_qk] * qk_scale).astype(BF16)
        w_v = wl[:, c_qk:c_v].astype(BF16)
        w_r = wl[:, c_v:c_r].astype(BF16)
        w_lr = _pad_lanes(wl[:, c_r:], LANES).astype(BF16)
        w_g2 = jnp.zeros((LANES, 2 * GLA_KEY_WIDTH), F32)
        w_g2 = w_g2.at[:GLA_GATE_RANK, :GLA_KEY_WIDTH].set(w_g2_f[l])
        w_g2 = w_g2.at[GLA_GATE_RANK:2 * GLA_GATE_RANK, GLA_KEY_WIDTH:].set(w_g2_b[l])
        b_g = jnp.concatenate([b_g_f[l], b_g_b[l]])[None, :]
        na_qkv, gqk, gv, gr, gates = _in_proj(xf, norm_mix_g[l][None, :], w_na, w_qk, w_v, w_r, w_lr, w_g2, b_g)

        y_na = _na(na_qkv, _na_bias_table(rpb[l], rows), batch, rows)
        o_f, o_b = _gla(gqk, gv, gates, batch, seq)

        w_o = w_out[l].astype(BF16)
        w_rt = _pad_lanes(jnp.concatenate([w_exp[l], w_grp[l]], axis=1), LANES)
        w_rt_hi = w_rt.astype(BF16)
        w_rt_lo = (w_rt - w_rt_hi.astype(F32)).astype(BF16)
        w_rt3 = jnp.concatenate([w_rt_hi, w_rt_hi, w_rt_lo], axis=0)
        b_rt = _pad_lanes(jnp.concatenate([b_exp[l], b_grp[l]])[None, :], LANES)
        x1, h2_slabs, rt, counts = _out_route(y_na, o_f, o_b, gr, xf, w_o[:NA_WIDTH], w_o[NA_WIDTH:],
                                              gla_norm_g[l][None, :], norm_ffn_g[l][None, :], w_rt3, b_rt)

        tile_expert, n_tiles, pad_tile, offs_b, sorted_rows = _dispatch_plan(counts, n)
        pos, xs = _dispatch(pad_tile, n_tiles, rt, offs_b, h2_slabs, sorted_rows)
        ys = _experts(tile_expert, n_tiles, xs, w_gate, w_up, w_down, l)
        pos = pos.reshape(-1)
        last = l == depth - 1
        xf = _combine(pos, ys, x1, rt, final_norm_g[None, :], final_norm=last)
    return xf.reshape(batch, seq, d)
```

```python
import functools

import jax
import jax.numpy as jnp
from jax import lax
from jax.experimental import pallas as pl
from jax.experimental.pallas import tpu as pltpu

F32 = jnp.float32
BF16 = jnp.bfloat16

GRID_W = 64
NA_HEADS = 8
NA_HEAD_DIM = 64
NA_WIDTH = NA_HEADS * NA_HEAD_DIM
WIN_H_MAX = 8
WIN_W = 16
GLA_HEADS = 4
GLA_DK = 64
GLA_DV = 128
GLA_KEY_WIDTH = GLA_HEADS * GLA_DK
GLA_VAL_WIDTH = GLA_HEADS * GLA_DV
GLA_GATE_RANK = 16
GLA_GATE_NORMALIZER = 16.0
GLA_CHUNK = 64
N_GROUPS = 4
EXPERTS_PER_GROUP = 8
N_EXPERTS = N_GROUPS * EXPERTS_PER_GROUP
RMS_EPS = 1e-6

LANES = 128
SUBLANES = 8
MXU_WIDTH = 256
VMEM_LIMIT_BYTES = 56 * 1024 * 1024

MASK_VALUE = -1e30

IN_PROJ_ROWS = 512
NA_ROW_BLOCK = 8
NA_ROW_UNROLL = 4
GLA_STEP_CHUNKS = 4
OUT_ROWS = 512
EXPERT_ROWS = 256
DISPATCH_ROWS = 512
COMBINE_ROWS = DISPATCH_ROWS
GATHER_UNROLL = 8


def _params(*sem):
    return pltpu.CompilerParams(dimension_semantics=sem, vmem_limit_bytes=VMEM_LIMIT_BYTES)


def _rms(x, g):
    return x * lax.rsqrt(jnp.mean(x * x, axis=-1, keepdims=True) + RMS_EPS) * g


def _in_proj_kernel(x_ref, g_ref, w_na_ref, w_qk_ref, w_v_ref, w_r_ref, w_lr_ref,
                    w_g2_ref, b_g_ref, na_ref, qk_ref, v_ref, r_ref, gate_ref, w_gate):
    @pl.when(pl.program_id(0) == 0)
    def _():
        w_gate[...] = jnp.dot(w_lr_ref[...], w_g2_ref[...], preferred_element_type=F32,
                              precision=lax.Precision.HIGHEST).astype(BF16)

    h = _rms(x_ref[...], g_ref[...]).astype(BF16)
    na_ref[...] = jnp.dot(h, w_na_ref[...], preferred_element_type=F32).astype(BF16)
    qk_ref[...] = jnp.dot(h, w_qk_ref[...], preferred_element_type=F32)
    v_ref[...] = jnp.dot(h, w_v_ref[...], preferred_element_type=F32).astype(BF16)
    r_ref[...] = jnp.dot(h, w_r_ref[...], preferred_element_type=F32)
    z = jnp.dot(h, w_gate[...], preferred_element_type=F32) + b_g_ref[...]
    log_sig = jnp.minimum(z, 0.0) - jnp.log(1.0 + jnp.exp(-jnp.abs(z)))
    gate_ref[...] = log_sig * (1.0 / GLA_GATE_NORMALIZER)


def _in_proj(x, g, w_na, w_qk, w_v, w_r, w_lr, w_g2, b_g):
    n, d = x.shape
    rows = IN_PROJ_ROWS
    row_spec = lambda width: pl.BlockSpec((rows, width), lambda i: (i, 0))
    full = lambda a: pl.BlockSpec(a.shape, lambda i: (0,) * a.ndim)
    return pl.pallas_call(
        _in_proj_kernel,
        grid=(n // rows,),
        in_specs=[row_spec(d), full(g), full(w_na), full(w_qk), full(w_v), full(w_r), full(w_lr),
                  full(w_g2), full(b_g)],
        out_specs=[row_spec(3 * NA_WIDTH), row_spec(2 * GLA_KEY_WIDTH), row_spec(GLA_VAL_WIDTH),
                   row_spec(GLA_VAL_WIDTH), row_spec(2 * GLA_KEY_WIDTH)],
        out_shape=[jax.ShapeDtypeStruct((n, 3 * NA_WIDTH), BF16),
                   jax.ShapeDtypeStruct((n, 2 * GLA_KEY_WIDTH), F32),
                   jax.ShapeDtypeStruct((n, GLA_VAL_WIDTH), BF16),
                   jax.ShapeDtypeStruct((n, GLA_VAL_WIDTH), F32),
                   jax.ShapeDtypeStruct((n, 2 * GLA_KEY_WIDTH), F32)],
        scratch_shapes=[pltpu.VMEM((d, 2 * GLA_KEY_WIDTH), BF16)],
        compiler_params=_params("arbitrary"),
        name="in_proj",
    )(x, g, w_na, w_qk, w_v, w_r, w_lr, w_g2, b_g)


def _na_bias_table(rpb, rows):
    kh = min(WIN_H_MAX, rows)
    cls = jnp.arange(kh)[:, None, None]
    i = jnp.arange(kh)[None, :, None]
    w = jnp.arange(GRID_W)[:, None, None]
    x = jnp.arange(GRID_W)[None, :, None]
    cs = jnp.clip(w - WIN_W // 2, 0, GRID_W - WIN_W)
    valid = (x >= cs) & (x < cs + WIN_W)
    row_sel = (jnp.arange(2 * WIN_H_MAX - 1)[None, None, :] == i - cls + (WIN_H_MAX - 1)).astype(F32)
    col_sel = (valid & (jnp.arange(2 * WIN_W - 1)[None, None, :] == x - w + (WIN_W - 1))).astype(F32)
    hi = lax.Precision.HIGHEST
    by_col = jnp.einsum("hrc,wxc->hrwx", rpb.astype(F32), col_sel, precision=hi)
    bias = jnp.einsum("kir,hrwx->khwix", row_sel, by_col, precision=hi)
    bias = jnp.where(valid[None, None, :, None, :, 0], bias, MASK_VALUE)
    return bias.reshape(kh, NA_HEADS // 2, 2 * GRID_W, kh * GRID_W)


def _na_kernel(q_ref, kp_ref, kc_ref, kn_ref, vp_ref, vc_ref, vn_ref, tbl_ref, o_ref,
               kbuf, vbuf, *, rows, kh):
    rb = NA_ROW_BLOCK
    j = pl.program_id(1)
    blk = rb * GRID_W
    for s, (k_src, v_src) in enumerate(((kp_ref, vp_ref), (kc_ref, vc_ref), (kn_ref, vn_ref))):
        kbuf[s * blk:(s + 1) * blk, :] = k_src[...].reshape(blk, NA_WIDTH)
        vbuf[s * blk:(s + 1) * blk, :] = v_src[...].reshape(blk, NA_WIDTH)
    lane = lax.broadcasted_iota(jnp.int32, (GRID_W, LANES), 1)
    first = lane < NA_HEAD_DIM

    def row_body(lr, carry):
        r = j * rb + lr
        start = jnp.clip(r - kh // 2, 0, rows - kh)
        cls = r - start
        local = pl.multiple_of((start - (j - 1) * rb) * GRID_W, GRID_W)
        q_row = q_ref[lr]
        k_win = kbuf[pl.ds(local, kh * GRID_W), :]
        v_win = vbuf[pl.ds(local, kh * GRID_W), :]
        pairs = [slice(p * LANES, (p + 1) * LANES) for p in range(NA_HEADS // 2)]
        scores = []
        for sl in pairs:
            q_pair = q_row[:, sl]
            zero = jnp.zeros_like(q_pair)
            q_bd = jnp.concatenate([jnp.where(first, q_pair, zero), jnp.where(first, zero, q_pair)], axis=0)
            scores.append(lax.dot_general(q_bd, k_win[:, sl], (((1,), (1,)), ((), ())),
                                          preferred_element_type=F32))
        probs, denoms = [], []
        for p, s in enumerate(scores):
            s = s + tbl_ref[cls, p]
            e = jnp.exp(s - jnp.max(s, axis=-1, keepdims=True))
            denoms.append(jnp.sum(e, axis=-1, keepdims=True))
            probs.append(e.astype(BF16))
        outs = []
        for sl, e, denom in zip(pairs, probs, denoms):
            o = jnp.dot(e, v_win[:, sl], preferred_element_type=F32) / denom
            outs.append(jnp.where(first, o[:GRID_W], o[GRID_W:]))
        o_ref[lr] = jnp.concatenate(outs, axis=-1).astype(o_ref.dtype)
        return carry

    lax.fori_loop(0, rb, row_body, 0, unroll=NA_ROW_UNROLL)


def _na(na_qkv, tbl, batch, rows):
    kh = min(WIN_H_MAX, rows)
    rb = NA_ROW_BLOCK
    nblk = rows // rb
    x4 = na_qkv.reshape(batch, rows, GRID_W, 3 * NA_WIDTH)
    blk = (None, rb, GRID_W, NA_WIDTH)
    prev = lambda j: jnp.maximum(j - 1, 0)
    nxt = lambda j: jnp.minimum(j + 1, nblk - 1)
    specs = [pl.BlockSpec(blk, lambda b, j: (b, j, 0, 0))]
    for col in (1, 2):
        specs += [pl.BlockSpec(blk, lambda b, j, col=col: (b, prev(j), 0, col)),
                  pl.BlockSpec(blk, lambda b, j, col=col: (b, j, 0, col)),
                  pl.BlockSpec(blk, lambda b, j, col=col: (b, nxt(j), 0, col))]
    specs.append(pl.BlockSpec(tbl.shape, lambda b, j: (0, 0, 0, 0)))
    out = pl.pallas_call(
        functools.partial(_na_kernel, rows=rows, kh=kh),
        grid=(batch, nblk),
        in_specs=specs,
        out_specs=pl.BlockSpec(blk, lambda b, j: (b, j, 0, 0)),
        out_shape=jax.ShapeDtypeStruct((batch, rows, GRID_W, NA_WIDTH), BF16),
        scratch_shapes=[pltpu.VMEM((3 * rb * GRID_W, NA_WIDTH), BF16),
                        pltpu.VMEM((3 * rb * GRID_W, NA_WIDTH), BF16)],
        compiler_params=_params("arbitrary", "arbitrary"),
        name="na",
    )(x4, x4, x4, x4, x4, x4, x4, tbl)
    return out.reshape(batch * rows * GRID_W, NA_WIDTH)


def _block_diag_mask(row_block, col_block, nblocks):
    shape = (row_block * nblocks, col_block * nblocks)
    r = lax.shift_right_logical(lax.broadcasted_iota(jnp.int32, shape, 0), row_block.bit_length() - 1)
    c = lax.shift_right_logical(lax.broadcasted_iota(jnp.int32, shape, 1), col_block.bit_length() - 1)
    return r == c


def _split_bf16x3(x):
    hi = x.astype(BF16)
    rest = x - hi.astype(F32)
    mid = rest.astype(BF16)
    lo = (rest - mid.astype(F32)).astype(BF16)
    return hi, mid, lo


def _gla_direction(qk_ref, v_ref, g_ref, *, backward):
    c = GLA_CHUNK
    nc = GLA_STEP_CHUNKS
    kw = GLA_KEY_WIDTH
    order = list(reversed(range(nc))) if backward else list(range(nc))
    chunk = lambda a, n: a[n * c:(n + 1) * c]

    def stage_cumsum():
        step = nc * c
        ti = lax.broadcasted_iota(jnp.int32, (step, step), 0)
        tj = lax.broadcasted_iota(jnp.int32, (step, step), 1)
        same_chunk = lax.shift_right_logical(ti, c.bit_length() - 1) == lax.shift_right_logical(tj, c.bit_length() - 1)
        tri = (tj >= ti) if backward else (tj <= ti)
        cum = jnp.where(same_chunk & tri, 1.0, 0.0).astype(BF16)
        return sum(jnp.dot(cum, piece, preferred_element_type=F32) for piece in _split_bf16x3(g_ref[...]))

    def stage_decays(b):
        ref_row, last_row = (c // 2, 0) if backward else (c // 2 - 1, c - 1)
        rows_of = lambda r: jnp.concatenate(
            [jnp.broadcast_to(b[n * c + r:n * c + r + 1, :], (c, kw)) for n in range(nc)], axis=0)
        b_ref, b_last = rows_of(ref_row), rows_of(last_row)
        q = qk_ref[:, :kw]
        k = qk_ref[:, kw:]
        q_rel = (q * jnp.exp(b - b_ref)).astype(BF16)
        k_rel = (k * jnp.exp(b_ref - b)).astype(BF16)
        k_dec = k * jnp.exp(b_last - b)
        q_dec = (q * jnp.exp(b)).astype(BF16)
        decay = jnp.exp(b_last)
        return q_rel, k_rel, k_dec, q_dec, decay

    def stage_scores(q_rel, k_rel):
        kk_mask = _block_diag_mask(c, GLA_DK, GLA_HEADS)
        out = []
        for n in range(nc):
            k_bd = jnp.where(kk_mask, jnp.concatenate([chunk(k_rel, n)] * GLA_HEADS, axis=0), jnp.zeros((), BF16))
            out.append(lax.dot_general(chunk(q_rel, n), k_bd, (((1,), (1,)), ((), ())),
                                       preferred_element_type=F32))
        return out

    def stage_intra(scores, k_dec, decay):
        si = lax.broadcasted_iota(jnp.int32, (c, c * GLA_HEADS), 0)
        sj = lax.broadcasted_iota(jnp.int32, (c, c * GLA_HEADS), 1) & (c - 1)
        keep = (sj > si) if backward else (sj <= si)
        kv_mask = _block_diag_mask(c, GLA_DV, GLA_HEADS)
        vk_mask = _block_diag_mask(GLA_DK, GLA_DV, 2)
        o_intra, upd, decay_col = [], [], []
        for n in range(nc):
            v = v_ref[n * c:(n + 1) * c, :]
            p = jnp.where(keep, scores[n], 0.0).astype(BF16)
            v_bd = jnp.where(kv_mask, jnp.concatenate([v] * GLA_HEADS, axis=0), jnp.zeros((), BF16))
            o_intra.append(jnp.dot(p, v_bd, preferred_element_type=F32))
            k_dec_t = chunk(k_dec, n).T.astype(BF16)
            pairs = []
            for hp in range(GLA_HEADS // 2):
                kv = jnp.dot(k_dec_t[hp * 2 * GLA_DK:(hp + 1) * 2 * GLA_DK],
                             v[:, hp * 2 * GLA_DV:(hp + 1) * 2 * GLA_DV], preferred_element_type=F32)
                pairs.append(jnp.where(vk_mask, kv, 0.0))
            upd.append(pairs)
            decay_col.append(chunk(decay, n).T[:, :1])
        return o_intra, upd, decay_col

    def scan_step(state, idx, q_dec, o_intra, upd, decay_col, out_ref):
        n = order[idx]
        q_n = chunk(q_dec, n)
        pw = 2 * GLA_DK
        o_inter = jnp.concatenate(
            [jnp.dot(q_n[:, hp * pw:(hp + 1) * pw], s.astype(BF16), preferred_element_type=F32)
             for hp, s in enumerate(state)], axis=1)
        out_ref[n * c:(n + 1) * c, :] = o_intra[n] + o_inter
        return [s * decay_col[n][hp * pw:(hp + 1) * pw] + u for hp, (s, u) in enumerate(zip(state, upd[n]))]

    return stage_cumsum, stage_decays, stage_scores, stage_intra, scan_step


def _gla_kernel(qk_f_ref, v_f_ref, g_f_ref, qk_b_ref, v_b_ref, g_b_ref, o_f_ref, o_b_ref,
                st_f, st_b):
    @pl.when(pl.program_id(1) == 0)
    def _():
        st_f[...] = jnp.zeros_like(st_f)
        st_b[...] = jnp.zeros_like(st_b)

    dirs = (_gla_direction(qk_f_ref, v_f_ref, g_f_ref, backward=False),
            _gla_direction(qk_b_ref, v_b_ref, g_b_ref, backward=True))
    b = [d[0]() for d in dirs]
    dec = [d[1](x) for d, x in zip(dirs, b)]
    scores = [d[2](x[0], x[1]) for d, x in zip(dirs, dec)]
    intra = [d[3](s, x[2], x[4]) for d, s, x in zip(dirs, scores, dec)]
    pairs = range(GLA_HEADS // 2)
    states = [[st[hp] for hp in pairs] for st in (st_f, st_b)]
    for idx in range(GLA_STEP_CHUNKS):
        for j, (d, x, y, out_ref) in enumerate(zip(dirs, dec, intra, (o_f_ref, o_b_ref))):
            states[j] = d[4](states[j], idx, x[3], *y, out_ref)
    for st, state in zip((st_f, st_b), states):
        for hp in pairs:
            st[hp] = state[hp]


def _gla(qk, v, gates, batch, seq):
    step = GLA_STEP_CHUNKS * GLA_CHUNK
    nblk = seq // step
    qk3 = qk.reshape(batch, seq, 2 * GLA_KEY_WIDTH)
    v3 = v.reshape(batch, seq, GLA_VAL_WIDTH)
    g3 = gates.reshape(batch, seq, 2 * GLA_KEY_WIDTH)
    fwd = lambda b, n: (b, n, 0)
    bwd = lambda b, n: (b, nblk - 1 - n, 0)
    bwd_gate = lambda b, n: (b, nblk - 1 - n, 1)
    o_f, o_b = pl.pallas_call(
        _gla_kernel,
        grid=(batch, nblk),
        in_specs=[pl.BlockSpec((None, step, 2 * GLA_KEY_WIDTH), fwd),
                  pl.BlockSpec((None, step, GLA_VAL_WIDTH), fwd),
                  pl.BlockSpec((None, step, GLA_KEY_WIDTH), fwd),
                  pl.BlockSpec((None, step, 2 * GLA_KEY_WIDTH), bwd),
                  pl.BlockSpec((None, step, GLA_VAL_WIDTH), bwd),
                  pl.BlockSpec((None, step, GLA_KEY_WIDTH), bwd_gate)],
        out_specs=[pl.BlockSpec((None, step, GLA_VAL_WIDTH), fwd),
                   pl.BlockSpec((None, step, GLA_VAL_WIDTH), bwd)],
        out_shape=[jax.ShapeDtypeStruct((batch, seq, GLA_VAL_WIDTH), F32)] * 2,
        scratch_shapes=[pltpu.VMEM((GLA_HEADS // 2, 2 * GLA_DK, 2 * GLA_DV), F32),
                        pltpu.VMEM((GLA_HEADS // 2, 2 * GLA_DK, 2 * GLA_DV), F32)],
        compiler_params=_params("arbitrary", "arbitrary"),
        name="gla",
    )(qk3, v3, g3, qk3, v3, g3)
    return o_f.reshape(batch * seq, GLA_VAL_WIDTH), o_b.reshape(batch * seq, GLA_VAL_WIDTH)


def _store_slabs(ref, x):
    rows = x.shape[0]
    for s in range(SUBLANES):
        ref[pl.ds(s, rows, stride=SUBLANES), :] = x[:, s * LANES:(s + 1) * LANES]


def _load_slabs(ref, rows):
    return jnp.concatenate([ref[pl.ds(s, rows, stride=SUBLANES), :] for s in range(SUBLANES)], axis=1)


def _slab_copy(src_hbm, dst, sem, src_row, k):
    dst_row = k * SUBLANES if isinstance(k, int) else pl.multiple_of(k * SUBLANES, SUBLANES)
    return pltpu.make_async_copy(src_hbm.at[pl.ds(pl.multiple_of(src_row, SUBLANES), SUBLANES), :],
                                 dst.at[pl.ds(dst_row, SUBLANES), :], sem)


def _wait_slabs(src_hbm, dst, sem):
    pltpu.make_async_copy(src_hbm.at[pl.ds(0, dst.shape[0]), :], dst, sem).wait()


RT_E1, RT_E2, RT_RANK1, RT_RANK2, RT_W1, RT_W2 = range(6)
GROUP_ROW0 = N_EXPERTS
ROUTE_ROWS = 64


def _out_route_kernel(na_ref, of_ref, ob_ref, r_ref, x_ref, w_na_ref, w_gla_ref, gn_ref, fn_ref,
                      w_rt_ref, b_rt_ref, x1_ref, h2_ref, rt_ref, cnt_ref, carry):
    @pl.when(pl.program_id(0) == 0)
    def _():
        carry[...] = jnp.zeros_like(carry)

    rows = x_ref.shape[0]
    o = of_ref[...] + ob_ref[...]
    r = r_ref[...]
    parts = []
    for h in range(GLA_HEADS):
        sl = slice(h * GLA_DV, (h + 1) * GLA_DV)
        parts.append(_rms(o[:, sl], gn_ref[...]) * (r[:, sl] * jax.nn.sigmoid(r[:, sl])))
    y_gla = jnp.concatenate(parts, axis=-1).astype(BF16)
    x1 = (x_ref[...] + jnp.dot(na_ref[...], w_na_ref[...], preferred_element_type=F32)
          + jnp.dot(y_gla, w_gla_ref[...], preferred_element_type=F32))
    x1_ref[...] = x1
    h2 = _rms(x1, fn_ref[...])
    _store_slabs(h2_ref, h2)

    h_hi = h2.astype(BF16)
    h_lo = (h2 - h_hi.astype(F32)).astype(BF16)
    logits = lax.dot_general(w_rt_ref[...], jnp.concatenate([h_hi, h_lo, h_hi], axis=1),
                             (((1,), (1,)), ((), ())), preferred_element_type=F32)
    logits = logits[:ROUTE_ROWS] + b_rt_ref[...]
    row_i = lax.broadcasted_iota(jnp.int32, (ROUTE_ROWS, rows), 0)
    row = row_i.astype(F32)
    row_grp = lax.shift_right_logical(row_i, EXPERTS_PER_GROUP.bit_length() - 1).astype(F32)
    neg = jnp.float32(-jnp.inf)
    no_row = jnp.float32(ROUTE_ROWS)
    is_grp = (row_i >= GROUP_ROW0) & (row_i < GROUP_ROW0 + N_GROUPS)
    g_logit = jnp.where(is_grp, logits, neg)
    g_max = jnp.max(g_logit, axis=0, keepdims=True)
    g_sel = jnp.min(jnp.where(is_grp & (g_logit == g_max), row, no_row), axis=0, keepdims=True) - GROUP_ROW0
    grp_w = 1.0 / jnp.sum(jnp.where(is_grp, jnp.exp(g_logit - g_max), 0.0), axis=0, keepdims=True)
    in_grp = (row_i < N_EXPERTS) & (row_grp == g_sel)
    e_logit = jnp.where(in_grp, logits, neg)
    v1 = jnp.max(e_logit, axis=0, keepdims=True)
    i1 = jnp.min(jnp.where(in_grp & (e_logit == v1), row, no_row), axis=0, keepdims=True)
    rest = in_grp & (row != i1)
    e_logit2 = jnp.where(rest, logits, neg)
    v2 = jnp.max(e_logit2, axis=0, keepdims=True)
    i2 = jnp.min(jnp.where(rest & (e_logit2 == v2), row, no_row), axis=0, keepdims=True)
    t = jnp.exp(v2 - v1)
    w1 = grp_w / (1.0 + t)
    w2 = grp_w * t / (1.0 + t)

    sel1 = row == i1
    sel2 = row == i2
    onehot = jnp.where(sel1 | sel2, 1.0, 0.0)
    ti = lax.broadcasted_iota(jnp.int32, (rows, rows), 0)
    tj = lax.broadcasted_iota(jnp.int32, (rows, rows), 1)
    earlier = jnp.where(ti < tj, 1.0, 0.0).astype(BF16)
    ranks = jnp.dot(onehot.astype(BF16), earlier, preferred_element_type=F32) + carry[:, :1]
    rank1 = jnp.sum(jnp.where(sel1, ranks, 0.0), axis=0, keepdims=True)
    rank2 = jnp.sum(jnp.where(sel2, ranks, 0.0), axis=0, keepdims=True)
    new_carry = carry[...] + jnp.sum(onehot, axis=1, keepdims=True)
    carry[...] = new_carry
    cnt_ref[...] = new_carry

    fields = {RT_E1: i1, RT_E2: i2, RT_RANK1: rank1, RT_RANK2: rank2, RT_W1: w1, RT_W2: w2}
    zero = jnp.zeros_like(w1)
    rt_ref[...] = jnp.concatenate([fields.get(f, zero) for f in range(SUBLANES)], axis=0)


def _out_route(y_na, o_f, o_b, r, x, w_na, w_gla, gn, fn, w_rt, b_rt):
    n, d = x.shape
    rows = OUT_ROWS
    row_spec = lambda width: pl.BlockSpec((rows, width), lambda i: (i, 0))
    full = lambda a: pl.BlockSpec(a.shape, lambda i: (0,) * a.ndim)
    return pl.pallas_call(
        _out_route_kernel,
        grid=(n // rows,),
        in_specs=[row_spec(NA_WIDTH), row_spec(GLA_VAL_WIDTH), row_spec(GLA_VAL_WIDTH),
                  row_spec(GLA_VAL_WIDTH), row_spec(d), full(w_na), full(w_gla), full(gn), full(fn),
                  full(w_rt), full(b_rt)],
        out_specs=[row_spec(d), pl.BlockSpec((rows * SUBLANES, LANES), lambda i: (i, 0)),
                   pl.BlockSpec((SUBLANES, rows), lambda i: (0, i)),
                   pl.BlockSpec((ROUTE_ROWS, LANES), lambda i: (0, 0))],
        out_shape=[jax.ShapeDtypeStruct((n, d), F32), jax.ShapeDtypeStruct((n * SUBLANES, LANES), F32),
                   jax.ShapeDtypeStruct((SUBLANES, n), F32), jax.ShapeDtypeStruct((ROUTE_ROWS, LANES), F32)],
        scratch_shapes=[pltpu.VMEM((ROUTE_ROWS, LANES), F32)],
        compiler_params=_params("arbitrary"),
        name="out_route",
    )(y_na, o_f, o_b, r, x, w_na, w_gla, gn, fn, w_rt, b_rt)


def _dispatch_kernel(pad_tile_ref, n_tiles_ref, rt_ref, offs_ref, h_ref, pos_ref, rec_ref, xs_hbm,
                     zeros, pos_vmem, pos_smem, sem_zero, sem_pos, sem):
    i = pl.program_id(0)
    db = DISPATCH_ROWS
    tile = EXPERT_ROWS * SUBLANES
    max_tiles = xs_hbm.shape[0] // tile

    @pl.when(i == 0)
    def _():
        zeros[...] = jnp.zeros_like(zeros)

        def zero_copy(t):
            return pltpu.make_async_copy(zeros, xs_hbm.at[pl.ds(pl.multiple_of(t * tile, tile), tile), :], sem_zero)

        def for_each_zero_tile(fn):
            def padded(e, carry):
                @pl.when(pad_tile_ref[e] >= 0)
                def _():
                    fn(zero_copy(pad_tile_ref[e]))
                return carry
            lax.fori_loop(0, N_EXPERTS, padded, 0)

            def tail(t, carry):
                fn(zero_copy(t))
                return carry
            lax.fori_loop(n_tiles_ref[0], max_tiles, tail, 0)

        for_each_zero_tile(lambda cp: cp.start())
        for_each_zero_tile(lambda cp: cp.wait())

    rt_t = rt_ref[...]
    rec_ref[...] = jnp.concatenate([rt_t, jnp.zeros((LANES - SUBLANES, db), F32)], axis=0).T
    expert = lax.broadcasted_iota(jnp.int32, (N_EXPERTS, db), 0).astype(F32)
    pos = []
    for e_lane, rank_lane in ((RT_E1, RT_RANK1), (RT_E2, RT_RANK2)):
        seg = jnp.sum(jnp.where(expert == rt_t[e_lane:e_lane + 1, :], offs_ref[...], 0.0), axis=0, keepdims=True)
        pos.append((seg + rt_t[rank_lane:rank_lane + 1, :]) * SUBLANES)
    pos = jnp.concatenate(pos, axis=0).astype(jnp.int32)
    pos_ref[...] = pos
    pos_vmem[...] = jnp.concatenate([pos, jnp.zeros((SUBLANES - 2, db), jnp.int32)], axis=0)
    to_smem = pltpu.make_async_copy(pos_vmem, pos_smem, sem_pos)
    to_smem.start()
    to_smem.wait()

    def body(k, carry):
        src = h_ref.at[pl.ds(pl.multiple_of(k * SUBLANES, SUBLANES), SUBLANES), :]
        for slot in range(2):
            dst_row = pl.multiple_of(pos_smem[slot, k], SUBLANES)
            pltpu.make_async_copy(src, xs_hbm.at[pl.ds(dst_row, SUBLANES), :], sem).start(priority=slot)
        return carry
    lax.fori_loop(0, db, body, 0, unroll=GATHER_UNROLL)
    for slot in range(2):
        pltpu.make_async_copy(h_ref, xs_hbm.at[pl.ds(0, db * SUBLANES), :], sem).wait()


def _dispatch(pad_tile, n_tiles, rt_t, offs_b, h2_slabs, sorted_rows):
    n = rt_t.shape[1]
    db = DISPATCH_ROWS
    grid_spec = pltpu.PrefetchScalarGridSpec(
        num_scalar_prefetch=2,
        grid=(n // db,),
        in_specs=[pl.BlockSpec((SUBLANES, db), lambda i, pt, nt: (0, i)),
                  pl.BlockSpec(offs_b.shape, lambda i, pt, nt: (0, 0)),
                  pl.BlockSpec((db * SUBLANES, LANES), lambda i, pt, nt: (i, 0))],
        out_specs=[pl.BlockSpec((None, 2, db), lambda i, pt, nt: (i, 0, 0)),
                   pl.BlockSpec((db, LANES), lambda i, pt, nt: (i, 0)),
                   pl.BlockSpec(memory_space=pl.ANY)],
        scratch_shapes=[pltpu.VMEM((EXPERT_ROWS * SUBLANES, LANES), F32),
                        pltpu.VMEM((SUBLANES, db), jnp.int32), pltpu.SMEM((SUBLANES, db), jnp.int32),
                        pltpu.SemaphoreType.DMA(()), pltpu.SemaphoreType.DMA(()), pltpu.SemaphoreType.DMA(())],
    )
    return pl.pallas_call(
        _dispatch_kernel,
        grid_spec=grid_spec,
        out_shape=[jax.ShapeDtypeStruct((n // db, 2, db), jnp.int32),
                   jax.ShapeDtypeStruct((n, LANES), F32),
                   jax.ShapeDtypeStruct((sorted_rows * SUBLANES, LANES), F32)],
        compiler_params=_params("arbitrary"),
        name="dispatch",
    )(pad_tile, n_tiles, rt_t, offs_b, h2_slabs)


def _experts_kernel(tile_expert_ref, n_tiles_ref, x_ref, wg_ref, wu_ref, wd_ref, y_ref, wg_s, wu_s, wd_s):
    i = pl.program_id(0)
    rows = EXPERT_ROWS
    n_tiles = n_tiles_ref[0]

    @pl.when(i < n_tiles)
    def _():
        @pl.when((i == 0) | (tile_expert_ref[i] != tile_expert_ref[jnp.maximum(i - 1, 0)]))
        def _():
            wg_s[...] = wg_ref[...].astype(BF16)
            wu_s[...] = wu_ref[...].astype(BF16)
            wd_s[...] = wd_ref[...].astype(BF16)

        x = _load_slabs(x_ref, rows).astype(BF16)
        dff = wg_s.shape[1]
        hid = []
        for c in range(0, dff, MXU_WIDTH):
            gate = jnp.dot(x, wg_s[:, c:c + MXU_WIDTH], preferred_element_type=F32)
            up = jnp.dot(x, wu_s[:, c:c + MXU_WIDTH], preferred_element_type=F32)
            hid.append((gate * jax.nn.sigmoid(gate) * up).astype(BF16))
        hid = jnp.concatenate(hid, axis=1)
        for c in range(0, wd_s.shape[1], MXU_WIDTH):
            y = jnp.dot(hid, wd_s[:, c:c + MXU_WIDTH], preferred_element_type=F32)
            for s in range(c // LANES, (c + MXU_WIDTH) // LANES):
                y_ref[pl.ds(s, rows, stride=SUBLANES), :] = y[:, s * LANES - c:(s + 1) * LANES - c]

    @pl.when(i >= n_tiles)
    def _():
        y_ref[...] = jnp.zeros_like(y_ref)


def _experts(tile_expert, n_tiles, xs, wg, wu, wd, layer):
    rows = EXPERT_ROWS
    steps = tile_expert.shape[0]
    _, _, d, dff = wg.shape
    assert d == SUBLANES * LANES
    w_idx = lambda i, te, nt: (layer, te[i], 0, 0)
    grid_spec = pltpu.PrefetchScalarGridSpec(
        num_scalar_prefetch=2,
        grid=(steps,),
        in_specs=[pl.BlockSpec((rows * SUBLANES, LANES), lambda i, te, nt: (jnp.minimum(i, nt[0] - 1), 0)),
                  pl.BlockSpec((None, None, d, dff), w_idx),
                  pl.BlockSpec((None, None, d, dff), w_idx),
                  pl.BlockSpec((None, None, dff, d), w_idx)],
        out_specs=pl.BlockSpec((rows * SUBLANES, LANES), lambda i, te, nt: (i, 0)),
        scratch_shapes=[pltpu.VMEM((d, dff), BF16), pltpu.VMEM((d, dff), BF16), pltpu.VMEM((dff, d), BF16)],
    )
    return pl.pallas_call(
        _experts_kernel,
        grid_spec=grid_spec,
        out_shape=jax.ShapeDtypeStruct((steps * rows * SUBLANES, LANES), F32),
        compiler_params=_params("arbitrary"),
        name="experts",
    )(tile_expert, n_tiles, xs, wg, wu, wd)


def _start_slab_gather(src_hbm, dst, sem, row_ref, base, count):
    def body(k, carry):
        _slab_copy(src_hbm, dst, sem, row_ref[base + k], k).start()
        return carry
    lax.fori_loop(0, count, body, 0, unroll=GATHER_UNROLL)


def _combine_kernel(pos_ref, ys_hbm, x1_ref, rt_ref, g_ref, o_ref, buf, sem, *, final_norm):
    i = pl.program_id(0)
    rows = COMBINE_ROWS
    n_steps = pl.num_programs(0)
    slot = i % 2

    def start(step, s):
        for j in range(2):
            _start_slab_gather(ys_hbm, buf.at[s, j], sem.at[s], pos_ref, (step * 2 + j) * rows, rows)

    @pl.when(i == 0)
    def _():
        start(0, 0)

    @pl.when(i + 1 < n_steps)
    def _():
        start(i + 1, 1 - slot)

    _wait_slabs(ys_hbm, buf.at[slot, 0], sem.at[slot])
    _wait_slabs(ys_hbm, buf.at[slot, 1], sem.at[slot])
    rt = rt_ref[...]
    lane = lax.broadcasted_iota(jnp.int32, rt.shape, 1)
    w1 = jnp.sum(jnp.where(lane == RT_W1, rt, 0.0), axis=-1, keepdims=True)
    w2 = jnp.sum(jnp.where(lane == RT_W2, rt, 0.0), axis=-1, keepdims=True)
    y = w1 * _load_slabs(buf.at[slot, 0], rows) + w2 * _load_slabs(buf.at[slot, 1], rows)
    x2 = x1_ref[...] + y
    if final_norm:
        x2 = _rms(x2, g_ref[...])
    o_ref[...] = x2


def _combine(pos, ys, x1, rt, g, final_norm):
    n, d = x1.shape
    rows = COMBINE_ROWS
    grid_spec = pltpu.PrefetchScalarGridSpec(
        num_scalar_prefetch=1,
        grid=(n // rows,),
        in_specs=[pl.BlockSpec(memory_space=pl.ANY),
                  pl.BlockSpec((rows, d), lambda i, pos: (i, 0)),
                  pl.BlockSpec((rows, LANES), lambda i, pos: (i, 0)),
                  pl.BlockSpec(g.shape, lambda i, pos: (0, 0))],
        out_specs=pl.BlockSpec((rows, d), lambda i, pos: (i, 0)),
        scratch_shapes=[pltpu.VMEM((2, 2, rows * SUBLANES, LANES), F32), pltpu.SemaphoreType.DMA((2,))],
    )
    return pl.pallas_call(
        functools.partial(_combine_kernel, final_norm=final_norm),
        grid_spec=grid_spec,
        out_shape=jax.ShapeDtypeStruct((n, d), F32),
        compiler_params=_params("arbitrary"),
        name="combine",
    )(pos, ys, x1, rt, g)


def _dispatch_plan(counts, n):
    rows = EXPERT_ROWS
    steps = (2 * n) // rows + N_EXPERTS
    cnt = counts[:N_EXPERTS, 0].astype(jnp.int32)
    tiles = (cnt + rows - 1) // rows
    tile_end = jnp.cumsum(tiles)
    tile_start = tile_end - tiles
    n_tiles = tile_end[-1]
    step_ids = jnp.arange(steps, dtype=jnp.int32)
    tile_expert = jnp.sum(tile_end[None, :] <= jnp.minimum(step_ids, n_tiles - 1)[:, None], axis=1)
    tile_expert = jnp.minimum(tile_expert, N_EXPERTS - 1).astype(jnp.int32)
    pad_tile = jnp.where(cnt % rows != 0, tile_end - 1, -1).astype(jnp.int32)
    offs_b = jnp.broadcast_to((tile_start * rows).astype(F32)[:, None], (N_EXPERTS, DISPATCH_ROWS))
    return tile_expert, n_tiles.reshape(1).astype(jnp.int32), pad_tile, offs_b, steps * rows


def _pad_lanes(a, width):
    return jnp.pad(a, ((0, 0), (0, width - a.shape[1])))


def kernel(x, norm_mix_g, w_in, w_g2_f, b_g_f, w_g2_b, b_g_b, gla_norm_g, rpb, w_out, norm_ffn_g, w_grp, b_grp,
           w_exp, b_exp, w_gate, w_up, w_down, final_norm_g):
    batch, seq, d = x.shape
    n = batch * seq
    rows = seq // GRID_W
    depth = w_in.shape[0]
    xf = x.reshape(n, d)
    c_na = 3 * NA_WIDTH
    c_qk = c_na + 2 * GLA_KEY_WIDTH
    c_v = c_qk + GLA_VAL_WIDTH
    c_r = c_v + GLA_VAL_WIDTH
    for l in range(depth):
        wl = w_in[l]
        q_scale = jnp.concatenate([jnp.full((NA_WIDTH,), NA_HEAD_DIM ** -0.5, F32),
                                   jnp.ones((2 * NA_WIDTH,), F32)])
        w_na = (wl[:, :c_na] * q_scale).astype(BF16)
        qk_scale = jnp.concatenate([jnp.full((GLA_KEY_WIDTH,), GLA_DK ** -0.5, F32),
                                    jnp.ones((GLA_KEY_WIDTH,), F32)])
        w_qk = (wl[:, c_na:c_qk] * qk_scale).astype(BF16)
        w_v = wl[:, c_qk:c_v].astype(BF16)
        w_r = wl[:, c_v:c_r].astype(BF16)
        w_lr = _pad_lanes(wl[:, c_r:], LANES)
        w_g2 = jnp.zeros((LANES, 2 * GLA_KEY_WIDTH), F32)
        w_g2 = w_g2.at[:GLA_GATE_RANK, :GLA_KEY_WIDTH].set(w_g2_f[l])
        w_g2 = w_g2.at[GLA_GATE_RANK:2 * GLA_GATE_RANK, GLA_KEY_WIDTH:].set(w_g2_b[l])
        b_g = jnp.concatenate([b_g_f[l], b_g_b[l]])[None, :]
        na_qkv, gqk, gv, gr, gates = _in_proj(xf, norm_mix_g[l][None, :], w_na, w_qk, w_v, w_r, w_lr, w_g2, b_g)

        y_na = _na(na_qkv, _na_bias_table(rpb[l], rows), batch, rows)
        o_f, o_b = _gla(gqk, gv, gates, batch, seq)

        w_o = w_out[l].astype(BF16)
        w_rt = _pad_lanes(jnp.concatenate([w_exp[l], w_grp[l]], axis=1), LANES)
        w_rt_hi = w_rt.astype(BF16)
        w_rt_lo = (w_rt - w_rt_hi.astype(F32)).astype(BF16)
        w_rt3 = jnp.concatenate([w_rt_hi, w_rt_hi, w_rt_lo], axis=0).T
        b_rt = _pad_lanes(jnp.concatenate([b_exp[l], b_grp[l]])[None, :], ROUTE_ROWS).T
        x1, h2_slabs, rt_t, counts = _out_route(y_na, o_f, o_b, gr, xf, w_o[:NA_WIDTH], w_o[NA_WIDTH:],
                                                gla_norm_g[l][None, :], norm_ffn_g[l][None, :], w_rt3, b_rt)

        tile_expert, n_tiles, pad_tile, offs_b, sorted_rows = _dispatch_plan(counts, n)
        pos, rt, xs = _dispatch(pad_tile, n_tiles, rt_t, offs_b, h2_slabs, sorted_rows)
        ys = _experts(tile_expert, n_tiles, xs, w_gate, w_up, w_down, l)
        pos = pos.reshape(-1)
        last = l == depth - 1
        xf = _combine(pos, ys, x1, rt, final_norm_g[None, :], final_norm=last)
    return xf.reshape(batch, seq, d)
```

```python
import functools

import jax
import jax.numpy as jnp
from jax import lax
from jax.experimental import pallas as pl
from jax.experimental.pallas import tpu as pltpu

F32 = jnp.float32
BF16 = jnp.bfloat16

GRID_W = 64
NA_HEADS = 8
NA_HEAD_DIM = 64
NA_WIDTH = NA_HEADS * NA_HEAD_DIM
WIN_H_MAX = 8
WIN_W = 16
GLA_HEADS = 4
GLA_DK = 64
GLA_DV = 128
GLA_KEY_WIDTH = GLA_HEADS * GLA_DK
GLA_VAL_WIDTH = GLA_HEADS * GLA_DV
GLA_GATE_RANK = 16
GLA_GATE_NORMALIZER = 16.0
GLA_CHUNK = 64
N_GROUPS = 4
EXPERTS_PER_GROUP = 8
N_EXPERTS = N_GROUPS * EXPERTS_PER_GROUP
RMS_EPS = 1e-6

LANES = 128
SUBLANES = 8
MXU_WIDTH = 256
VMEM_LIMIT_BYTES = 56 * 1024 * 1024

MASK_VALUE = -1e30

IN_PROJ_ROWS = 512
NA_ROW_BLOCK = 8
NA_ROW_UNROLL = 4
GLA_STEP_CHUNKS = 4
OUT_ROWS = 512
EXPERT_ROWS = 256
DISPATCH_ROWS = 512
COMBINE_ROWS = DISPATCH_ROWS
GATHER_UNROLL = 64


def _params(*sem):
    return pltpu.CompilerParams(dimension_semantics=sem, vmem_limit_bytes=VMEM_LIMIT_BYTES)


def _rms(x, g):
    return x * lax.rsqrt(jnp.mean(x * x, axis=-1, keepdims=True) + RMS_EPS) * g


def _in_proj_kernel(x_ref, g_ref, w_na_ref, w_qk_ref, w_v_ref, w_r_ref, w_lr_ref,
                    w_g2_ref, b_g_ref, na_ref, qk_ref, v_ref, r_ref, gate_ref, w_gate):
    @pl.when(pl.program_id(0) == 0)
    def _():
        w_gate[...] = jnp.dot(w_lr_ref[...], w_g2_ref[...], preferred_element_type=F32,
                              precision=lax.Precision.HIGHEST).astype(BF16)

    h = _rms(x_ref[...], g_ref[...]).astype(BF16)
    na_ref[...] = jnp.dot(h, w_na_ref[...], preferred_element_type=F32).astype(BF16)
    qk_ref[...] = jnp.dot(h, w_qk_ref[...], preferred_element_type=F32)
    v_ref[...] = jnp.dot(h, w_v_ref[...], preferred_element_type=F32).astype(BF16)
    r_ref[...] = jnp.dot(h, w_r_ref[...], preferred_element_type=F32)
    z = jnp.dot(h, w_gate[...], preferred_element_type=F32) + b_g_ref[...]
    log_sig = jnp.minimum(z, 0.0) - jnp.log(1.0 + jnp.exp(-jnp.abs(z)))
    gate_ref[...] = log_sig * (1.0 / GLA_GATE_NORMALIZER)


def _in_proj(x, g, w_na, w_qk, w_v, w_r, w_lr, w_g2, b_g):
    n, d = x.shape
    rows = IN_PROJ_ROWS
    row_spec = lambda width: pl.BlockSpec((rows, width), lambda i: (i, 0))
    full = lambda a: pl.BlockSpec(a.shape, lambda i: (0,) * a.ndim)
    return pl.pallas_call(
        _in_proj_kernel,
        grid=(n // rows,),
        in_specs=[row_spec(d), full(g), full(w_na), full(w_qk), full(w_v), full(w_r), full(w_lr),
                  full(w_g2), full(b_g)],
        out_specs=[row_spec(3 * NA_WIDTH), row_spec(2 * GLA_KEY_WIDTH), row_spec(GLA_VAL_WIDTH),
                   row_spec(GLA_VAL_WIDTH), row_spec(2 * GLA_KEY_WIDTH)],
        out_shape=[jax.ShapeDtypeStruct((n, 3 * NA_WIDTH), BF16),
                   jax.ShapeDtypeStruct((n, 2 * GLA_KEY_WIDTH), F32),
                   jax.ShapeDtypeStruct((n, GLA_VAL_WIDTH), BF16),
                   jax.ShapeDtypeStruct((n, GLA_VAL_WIDTH), F32),
                   jax.ShapeDtypeStruct((n, 2 * GLA_KEY_WIDTH), F32)],
        scratch_shapes=[pltpu.VMEM((d, 2 * GLA_KEY_WIDTH), BF16)],
        compiler_params=_params("arbitrary"),
        name="in_proj",
    )(x, g, w_na, w_qk, w_v, w_r, w_lr, w_g2, b_g)


def _na_bias_table(rpb, rows):
    kh = min(WIN_H_MAX, rows)
    cls = jnp.arange(kh)[:, None, None]
    i = jnp.arange(kh)[None, :, None]
    w = jnp.arange(GRID_W)[:, None, None]
    x = jnp.arange(GRID_W)[None, :, None]
    cs = jnp.clip(w - WIN_W // 2, 0, GRID_W - WIN_W)
    valid = (x >= cs) & (x < cs + WIN_W)
    row_sel = (jnp.arange(2 * WIN_H_MAX - 1)[None, None, :] == i - cls + (WIN_H_MAX - 1)).astype(F32)
    col_sel = (valid & (jnp.arange(2 * WIN_W - 1)[None, None, :] == x - w + (WIN_W - 1))).astype(F32)
    hi = lax.Precision.HIGHEST
    by_col = jnp.einsum("hrc,wxc->hrwx", rpb.astype(F32), col_sel, precision=hi)
    bias = jnp.einsum("kir,hrwx->khwix", row_sel, by_col, precision=hi)
    bias = jnp.where(valid[None, None, :, None, :, 0], bias, MASK_VALUE)
    return bias.reshape(kh, NA_HEADS // 2, 2 * GRID_W, kh * GRID_W)


def _na_kernel(q_ref, kp_ref, kc_ref, kn_ref, vp_ref, vc_ref, vn_ref, tbl_ref, o_ref,
               kbuf, vbuf, *, rows, kh):
    rb = NA_ROW_BLOCK
    j = pl.program_id(1)
    blk = rb * GRID_W
    for s, (k_src, v_src) in enumerate(((kp_ref, vp_ref), (kc_ref, vc_ref), (kn_ref, vn_ref))):
        kbuf[s * blk:(s + 1) * blk, :] = k_src[...].reshape(blk, NA_WIDTH)
        vbuf[s * blk:(s + 1) * blk, :] = v_src[...].reshape(blk, NA_WIDTH)
    lane = lax.broadcasted_iota(jnp.int32, (GRID_W, LANES), 1)
    first = lane < NA_HEAD_DIM

    def row_body(lr, carry):
        r = j * rb + lr
        start = jnp.clip(r - kh // 2, 0, rows - kh)
        cls = r - start
        local = pl.multiple_of((start - (j - 1) * rb) * GRID_W, GRID_W)
        q_row = q_ref[lr]
        k_win = kbuf[pl.ds(local, kh * GRID_W), :]
        v_win = vbuf[pl.ds(local, kh * GRID_W), :]
        pairs = [slice(p * LANES, (p + 1) * LANES) for p in range(NA_HEADS // 2)]
        scores = []
        for sl in pairs:
            q_pair = q_row[:, sl]
            zero = jnp.zeros_like(q_pair)
            q_bd = jnp.concatenate([jnp.where(first, q_pair, zero), jnp.where(first, zero, q_pair)], axis=0)
            scores.append(lax.dot_general(q_bd, k_win[:, sl], (((1,), (1,)), ((), ())),
                                          preferred_element_type=F32))
        probs, denoms = [], []
        for p, s in enumerate(scores):
            s = s + tbl_ref[cls, p]
            e = jnp.exp(s - jnp.max(s, axis=-1, keepdims=True))
            denoms.append(jnp.sum(e, axis=-1, keepdims=True))
            probs.append(e.astype(BF16))
        outs = []
        for sl, e, denom in zip(pairs, probs, denoms):
            o = jnp.dot(e, v_win[:, sl], preferred_element_type=F32) / denom
            outs.append(jnp.where(first, o[:GRID_W], o[GRID_W:]))
        o_ref[lr] = jnp.concatenate(outs, axis=-1).astype(o_ref.dtype)
        return carry

    lax.fori_loop(0, rb, row_body, 0, unroll=NA_ROW_UNROLL)


def _na(na_qkv, tbl, batch, rows):
    kh = min(WIN_H_MAX, rows)
    rb = NA_ROW_BLOCK
    nblk = rows // rb
    x4 = na_qkv.reshape(batch, rows, GRID_W, 3 * NA_WIDTH)
    blk = (None, rb, GRID_W, NA_WIDTH)
    prev = lambda j: jnp.maximum(j - 1, 0)
    nxt = lambda j: jnp.minimum(j + 1, nblk - 1)
    specs = [pl.BlockSpec(blk, lambda b, j: (b, j, 0, 0))]
    for col in (1, 2):
        specs += [pl.BlockSpec(blk, lambda b, j, col=col: (b, prev(j), 0, col)),
                  pl.BlockSpec(blk, lambda b, j, col=col: (b, j, 0, col)),
                  pl.BlockSpec(blk, lambda b, j, col=col: (b, nxt(j), 0, col))]
    specs.append(pl.BlockSpec(tbl.shape, lambda b, j: (0, 0, 0, 0)))
    out = pl.pallas_call(
        functools.partial(_na_kernel, rows=rows, kh=kh),
        grid=(batch, nblk),
        in_specs=specs,
        out_specs=pl.BlockSpec(blk, lambda b, j: (b, j, 0, 0)),
        out_shape=jax.ShapeDtypeStruct((batch, rows, GRID_W, NA_WIDTH), BF16),
        scratch_shapes=[pltpu.VMEM((3 * rb * GRID_W, NA_WIDTH), BF16),
                        pltpu.VMEM((3 * rb * GRID_W, NA_WIDTH), BF16)],
        compiler_params=_params("arbitrary", "arbitrary"),
        name="na",
    )(x4, x4, x4, x4, x4, x4, x4, tbl)
    return out.reshape(batch * rows * GRID_W, NA_WIDTH)


def _block_diag_mask(row_block, col_block, nblocks):
    shape = (row_block * nblocks, col_block * nblocks)
    r = lax.shift_right_logical(lax.broadcasted_iota(jnp.int32, shape, 0), row_block.bit_length() - 1)
    c = lax.shift_right_logical(lax.broadcasted_iota(jnp.int32, shape, 1), col_block.bit_length() - 1)
    return r == c


def _split_bf16x3(x):
    hi = x.astype(BF16)
    rest = x - hi.astype(F32)
    mid = rest.astype(BF16)
    lo = (rest - mid.astype(F32)).astype(BF16)
    return hi, mid, lo


def _gla_direction(qk_ref, v_ref, g_ref, *, backward):
    c = GLA_CHUNK
    nc = GLA_STEP_CHUNKS
    kw = GLA_KEY_WIDTH
    order = list(reversed(range(nc))) if backward else list(range(nc))
    chunk = lambda a, n: a[n * c:(n + 1) * c]

    def stage_cumsum():
        step = nc * c
        ti = lax.broadcasted_iota(jnp.int32, (step, step), 0)
        tj = lax.broadcasted_iota(jnp.int32, (step, step), 1)
        same_chunk = lax.shift_right_logical(ti, c.bit_length() - 1) == lax.shift_right_logical(tj, c.bit_length() - 1)
        tri = (tj >= ti) if backward else (tj <= ti)
        cum = jnp.where(same_chunk & tri, 1.0, 0.0).astype(BF16)
        return sum(jnp.dot(cum, piece, preferred_element_type=F32) for piece in _split_bf16x3(g_ref[...]))

    def stage_decays(b):
        ref_row, last_row = (c // 2, 0) if backward else (c // 2 - 1, c - 1)
        rows_of = lambda r: jnp.concatenate(
            [jnp.broadcast_to(b[n * c + r:n * c + r + 1, :], (c, kw)) for n in range(nc)], axis=0)
        b_ref, b_last = rows_of(ref_row), rows_of(last_row)
        q = qk_ref[:, :kw]
        k = qk_ref[:, kw:]
        q_rel = (q * jnp.exp(b - b_ref)).astype(BF16)
        k_rel = (k * jnp.exp(b_ref - b)).astype(BF16)
        k_dec = k * jnp.exp(b_last - b)
        q_dec = (q * jnp.exp(b)).astype(BF16)
        decay = jnp.exp(b_last)
        return q_rel, k_rel, k_dec, q_dec, decay

    def stage_scores(q_rel, k_rel):
        kk_mask = _block_diag_mask(c, GLA_DK, GLA_HEADS)
        out = []
        for n in range(nc):
            k_bd = jnp.where(kk_mask, jnp.concatenate([chunk(k_rel, n)] * GLA_HEADS, axis=0), jnp.zeros((), BF16))
            out.append(lax.dot_general(chunk(q_rel, n), k_bd, (((1,), (1,)), ((), ())),
                                       preferred_element_type=F32))
        return out

    def stage_intra(scores, k_dec, decay):
        si = lax.broadcasted_iota(jnp.int32, (c, c * GLA_HEADS), 0)
        sj = lax.broadcasted_iota(jnp.int32, (c, c * GLA_HEADS), 1) & (c - 1)
        keep = (sj > si) if backward else (sj <= si)
        kv_mask = _block_diag_mask(c, GLA_DV, GLA_HEADS)
        vk_mask = _block_diag_mask(GLA_DK, GLA_DV, 2)
        o_intra, upd, decay_col = [], [], []
        for n in range(nc):
            v = v_ref[n * c:(n + 1) * c, :]
            p = jnp.where(keep, scores[n], 0.0).astype(BF16)
            v_bd = jnp.where(kv_mask, jnp.concatenate([v] * GLA_HEADS, axis=0), jnp.zeros((), BF16))
            o_intra.append(jnp.dot(p, v_bd, preferred_element_type=F32))
            k_dec_t = chunk(k_dec, n).T.astype(BF16)
            pairs = []
            for hp in range(GLA_HEADS // 2):
                kv = jnp.dot(k_dec_t[hp * 2 * GLA_DK:(hp + 1) * 2 * GLA_DK],
                             v[:, hp * 2 * GLA_DV:(hp + 1) * 2 * GLA_DV], preferred_element_type=F32)
                pairs.append(jnp.where(vk_mask, kv, 0.0))
            upd.append(pairs)
            decay_col.append(chunk(decay, n).T[:, :1])
        return o_intra, upd, decay_col

    def scan_step(state, idx, q_dec, o_intra, upd, decay_col, out_ref):
        n = order[idx]
        q_n = chunk(q_dec, n)
        pw = 2 * GLA_DK
        o_inter = jnp.concatenate(
            [jnp.dot(q_n[:, hp * pw:(hp + 1) * pw], s.astype(BF16), preferred_element_type=F32)
             for hp, s in enumerate(state)], axis=1)
        out_ref[n * c:(n + 1) * c, :] = o_intra[n] + o_inter
        return [s * decay_col[n][hp * pw:(hp + 1) * pw] + u for hp, (s, u) in enumerate(zip(state, upd[n]))]

    return stage_cumsum, stage_decays, stage_scores, stage_intra, scan_step


def _gla_kernel(qk_f_ref, v_f_ref, g_f_ref, qk_b_ref, v_b_ref, g_b_ref, o_f_ref, o_b_ref,
                st_f, st_b):
    @pl.when(pl.program_id(1) == 0)
    def _():
        st_f[...] = jnp.zeros_like(st_f)
        st_b[...] = jnp.zeros_like(st_b)

    dirs = (_gla_direction(qk_f_ref, v_f_ref, g_f_ref, backward=False),
            _gla_direction(qk_b_ref, v_b_ref, g_b_ref, backward=True))
    b = [d[0]() for d in dirs]
    dec = [d[1](x) for d, x in zip(dirs, b)]
    scores = [d[2](x[0], x[1]) for d, x in zip(dirs, dec)]
    intra = [d[3](s, x[2], x[4]) for d, s, x in zip(dirs, scores, dec)]
    pairs = range(GLA_HEADS // 2)
    states = [[st[hp] for hp in pairs] for st in (st_f, st_b)]
    for idx in range(GLA_STEP_CHUNKS):
        for j, (d, x, y, out_ref) in enumerate(zip(dirs, dec, intra, (o_f_ref, o_b_ref))):
            states[j] = d[4](states[j], idx, x[3], *y, out_ref)
    for st, state in zip((st_f, st_b), states):
        for hp in pairs:
            st[hp] = state[hp]


def _gla(qk, v, gates, batch, seq):
    step = GLA_STEP_CHUNKS * GLA_CHUNK
    nblk = seq // step
    qk3 = qk.reshape(batch, seq, 2 * GLA_KEY_WIDTH)
    v3 = v.reshape(batch, seq, GLA_VAL_WIDTH)
    g3 = gates.reshape(batch, seq, 2 * GLA_KEY_WIDTH)
    fwd = lambda b, n: (b, n, 0)
    bwd = lambda b, n: (b, nblk - 1 - n, 0)
    bwd_gate = lambda b, n: (b, nblk - 1 - n, 1)
    o_f, o_b = pl.pallas_call(
        _gla_kernel,
        grid=(batch, nblk),
        in_specs=[pl.BlockSpec((None, step, 2 * GLA_KEY_WIDTH), fwd),
                  pl.BlockSpec((None, step, GLA_VAL_WIDTH), fwd),
                  pl.BlockSpec((None, step, GLA_KEY_WIDTH), fwd),
                  pl.BlockSpec((None, step, 2 * GLA_KEY_WIDTH), bwd),
                  pl.BlockSpec((None, step, GLA_VAL_WIDTH), bwd),
                  pl.BlockSpec((None, step, GLA_KEY_WIDTH), bwd_gate)],
        out_specs=[pl.BlockSpec((None, step, GLA_VAL_WIDTH), fwd),
                   pl.BlockSpec((None, step, GLA_VAL_WIDTH), bwd)],
        out_shape=[jax.ShapeDtypeStruct((batch, seq, GLA_VAL_WIDTH), F32)] * 2,
        scratch_shapes=[pltpu.VMEM((GLA_HEADS // 2, 2 * GLA_DK, 2 * GLA_DV), F32),
                        pltpu.VMEM((GLA_HEADS // 2, 2 * GLA_DK, 2 * GLA_DV), F32)],
        compiler_params=_params("arbitrary", "arbitrary"),
        name="gla",
    )(qk3, v3, g3, qk3, v3, g3)
    return o_f.reshape(batch * seq, GLA_VAL_WIDTH), o_b.reshape(batch * seq, GLA_VAL_WIDTH)


def _store_slabs(ref, x):
    rows = x.shape[0]
    for s in range(SUBLANES):
        ref[pl.ds(s, rows, stride=SUBLANES), :] = x[:, s * LANES:(s + 1) * LANES]


def _load_slabs(ref, rows):
    return jnp.concatenate([ref[pl.ds(s, rows, stride=SUBLANES), :] for s in range(SUBLANES)], axis=1)


def _slab_copy(src_hbm, dst, sem, src_row, k):
    dst_row = k * SUBLANES if isinstance(k, int) else pl.multiple_of(k * SUBLANES, SUBLANES)
    return pltpu.make_async_copy(src_hbm.at[pl.ds(pl.multiple_of(src_row, SUBLANES), SUBLANES), :],
                                 dst.at[pl.ds(dst_row, SUBLANES), :], sem)


def _wait_slabs(src_hbm, dst, sem):
    pltpu.make_async_copy(src_hbm.at[pl.ds(0, dst.shape[0]), :], dst, sem).wait()


RT_E1, RT_E2, RT_RANK1, RT_RANK2, RT_W1, RT_W2 = range(6)
GROUP_ROW0 = N_EXPERTS
ROUTE_ROWS = 64


def _out_route_kernel(na_ref, of_ref, ob_ref, r_ref, x_ref, w_na_ref, w_gla_ref, gn_ref, fn_ref,
                      w_rt_ref, b_rt_ref, x1_ref, h2_ref, rt_ref, cnt_ref, carry):
    @pl.when(pl.program_id(0) == 0)
    def _():
        carry[...] = jnp.zeros_like(carry)

    rows = x_ref.shape[0]
    o = of_ref[...] + ob_ref[...]
    r = r_ref[...]
    parts = []
    for h in range(GLA_HEADS):
        sl = slice(h * GLA_DV, (h + 1) * GLA_DV)
        parts.append(_rms(o[:, sl], gn_ref[...]) * (r[:, sl] * jax.nn.sigmoid(r[:, sl])))
    y_gla = jnp.concatenate(parts, axis=-1).astype(BF16)
    x1 = (x_ref[...] + jnp.dot(na_ref[...], w_na_ref[...], preferred_element_type=F32)
          + jnp.dot(y_gla, w_gla_ref[...], preferred_element_type=F32))
    x1_ref[...] = x1
    h2 = _rms(x1, fn_ref[...])
    _store_slabs(h2_ref, h2)

    h_hi = h2.astype(BF16)
    h_lo = (h2 - h_hi.astype(F32)).astype(BF16)
    logits = lax.dot_general(w_rt_ref[...], jnp.concatenate([h_hi, h_lo, h_hi], axis=1),
                             (((1,), (1,)), ((), ())), preferred_element_type=F32)
    logits = logits[:ROUTE_ROWS] + b_rt_ref[...]
    row_i = lax.broadcasted_iota(jnp.int32, (ROUTE_ROWS, rows), 0)
    row = row_i.astype(F32)
    row_grp = lax.shift_right_logical(row_i, EXPERTS_PER_GROUP.bit_length() - 1).astype(F32)
    neg = jnp.float32(-jnp.inf)
    no_row = jnp.float32(ROUTE_ROWS)
    is_grp = (row_i >= GROUP_ROW0) & (row_i < GROUP_ROW0 + N_GROUPS)
    g_logit = jnp.where(is_grp, logits, neg)
    g_max = jnp.max(g_logit, axis=0, keepdims=True)
    g_sel = jnp.min(jnp.where(is_grp & (g_logit == g_max), row, no_row), axis=0, keepdims=True) - GROUP_ROW0
    grp_w = 1.0 / jnp.sum(jnp.where(is_grp, jnp.exp(g_logit - g_max), 0.0), axis=0, keepdims=True)
    in_grp = (row_i < N_EXPERTS) & (row_grp == g_sel)
    e_logit = jnp.where(in_grp, logits, neg)
    v1 = jnp.max(e_logit, axis=0, keepdims=True)
    i1 = jnp.min(jnp.where(in_grp & (e_logit == v1), row, no_row), axis=0, keepdims=True)
    rest = in_grp & (row != i1)
    e_logit2 = jnp.where(rest, logits, neg)
    v2 = jnp.max(e_logit2, axis=0, keepdims=True)
    i2 = jnp.min(jnp.where(rest & (e_logit2 == v2), row, no_row), axis=0, keepdims=True)
    t = jnp.exp(v2 - v1)
    w1 = grp_w / (1.0 + t)
    w2 = grp_w * t / (1.0 + t)

    sel1 = row == i1
    sel2 = row == i2
    onehot = jnp.where(sel1 | sel2, 1.0, 0.0)
    ti = lax.broadcasted_iota(jnp.int32, (rows, rows), 0)
    tj = lax.broadcasted_iota(jnp.int32, (rows, rows), 1)
    earlier = jnp.where(ti < tj, 1.0, 0.0).astype(BF16)
    ranks = jnp.dot(onehot.astype(BF16), earlier, preferred_element_type=F32) + carry[:, :1]
    rank1 = jnp.sum(jnp.where(sel1, ranks, 0.0), axis=0, keepdims=True)
    rank2 = jnp.sum(jnp.where(sel2, ranks, 0.0), axis=0, keepdims=True)
    new_carry = carry[...] + jnp.sum(onehot, axis=1, keepdims=True)
    carry[...] = new_carry
    cnt_ref[...] = new_carry

    fields = {RT_E1: i1, RT_E2: i2, RT_RANK1: rank1, RT_RANK2: rank2, RT_W1: w1, RT_W2: w2}
    zero = jnp.zeros_like(w1)
    rt_ref[...] = jnp.concatenate([fields.get(f, zero) for f in range(SUBLANES)], axis=0)


def _out_route(y_na, o_f, o_b, r, x, w_na, w_gla, gn, fn, w_rt, b_rt):
    n, d = x.shape
    rows = OUT_ROWS
    row_spec = lambda width: pl.BlockSpec((rows, width), lambda i: (i, 0))
    full = lambda a: pl.BlockSpec(a.shape, lambda i: (0,) * a.ndim)
    return pl.pallas_call(
        _out_route_kernel,
        grid=(n // rows,),
        in_specs=[row_spec(NA_WIDTH), row_spec(GLA_VAL_WIDTH), row_spec(GLA_VAL_WIDTH),
                  row_spec(GLA_VAL_WIDTH), row_spec(d), full(w_na), full(w_gla), full(gn), full(fn),
                  full(w_rt), full(b_rt)],
        out_specs=[row_spec(d), pl.BlockSpec((rows * SUBLANES, LANES), lambda i: (i, 0)),
                   pl.BlockSpec((SUBLANES, rows), lambda i: (0, i)),
                   pl.BlockSpec((ROUTE_ROWS, LANES), lambda i: (0, 0))],
        out_shape=[jax.ShapeDtypeStruct((n, d), F32), jax.ShapeDtypeStruct((n * SUBLANES, LANES), F32),
                   jax.ShapeDtypeStruct((SUBLANES, n), F32), jax.ShapeDtypeStruct((ROUTE_ROWS, LANES), F32)],
        scratch_shapes=[pltpu.VMEM((ROUTE_ROWS, LANES), F32)],
        compiler_params=_params("arbitrary"),
        name="out_route",
    )(y_na, o_f, o_b, r, x, w_na, w_gla, gn, fn, w_rt, b_rt)


def _dispatch_kernel(pad_tile_ref, n_tiles_ref, rt_ref, offs_ref, h_ref, pos_ref, rec_ref, xs_hbm,
                     zeros, pos_vmem, pos_smem, sem_zero, sem_pos, sem):
    i = pl.program_id(0)
    db = DISPATCH_ROWS
    tile = EXPERT_ROWS * SUBLANES
    max_tiles = xs_hbm.shape[0] // tile

    @pl.when(i == 0)
    def _():
        zeros[...] = jnp.zeros_like(zeros)

        def zero_copy(t):
            return pltpu.make_async_copy(zeros, xs_hbm.at[pl.ds(pl.multiple_of(t * tile, tile), tile), :], sem_zero)

        def for_each_zero_tile(fn):
            def padded(e, carry):
                @pl.when(pad_tile_ref[e] >= 0)
                def _():
                    fn(zero_copy(pad_tile_ref[e]))
                return carry
            lax.fori_loop(0, N_EXPERTS, padded, 0)

            def tail(t, carry):
                fn(zero_copy(t))
                return carry
            lax.fori_loop(n_tiles_ref[0], max_tiles, tail, 0)

        for_each_zero_tile(lambda cp: cp.start())
        for_each_zero_tile(lambda cp: cp.wait())

    rt_t = rt_ref[...]
    rec_ref[...] = jnp.concatenate([rt_t, jnp.zeros((LANES - SUBLANES, db), F32)], axis=0).T
    expert = lax.broadcasted_iota(jnp.int32, (N_EXPERTS, db), 0).astype(F32)
    pos = []
    for e_lane, rank_lane in ((RT_E1, RT_RANK1), (RT_E2, RT_RANK2)):
        seg = jnp.sum(jnp.where(expert == rt_t[e_lane:e_lane + 1, :], offs_ref[...], 0.0), axis=0, keepdims=True)
        pos.append((seg + rt_t[rank_lane:rank_lane + 1, :]) * SUBLANES)
    pos = jnp.concatenate(pos, axis=0).astype(jnp.int32)
    pos_ref[...] = pos
    pos_vmem[...] = jnp.concatenate([pos, jnp.zeros((SUBLANES - 2, db), jnp.int32)], axis=0)
    to_smem = pltpu.make_async_copy(pos_vmem, pos_smem, sem_pos)
    to_smem.start()
    to_smem.wait()

    for k in range(db):
        src = h_ref.at[pl.ds(k * SUBLANES, SUBLANES), :]
        for slot in range(2):
            dst_row = pl.multiple_of(pos_smem[slot, k], SUBLANES)
            pltpu.make_async_copy(src, xs_hbm.at[pl.ds(dst_row, SUBLANES), :], sem).start(priority=slot)
    for slot in range(2):
        pltpu.make_async_copy(h_ref, xs_hbm.at[pl.ds(0, db * SUBLANES), :], sem).wait()


def _dispatch(pad_tile, n_tiles, rt_t, offs_b, h2_slabs, sorted_rows):
    n = rt_t.shape[1]
    db = DISPATCH_ROWS
    grid_spec = pltpu.PrefetchScalarGridSpec(
        num_scalar_prefetch=2,
        grid=(n // db,),
        in_specs=[pl.BlockSpec((SUBLANES, db), lambda i, pt, nt: (0, i)),
                  pl.BlockSpec(offs_b.shape, lambda i, pt, nt: (0, 0)),
                  pl.BlockSpec((db * SUBLANES, LANES), lambda i, pt, nt: (i, 0))],
        out_specs=[pl.BlockSpec((None, 2, db), lambda i, pt, nt: (i, 0, 0)),
                   pl.BlockSpec((db, LANES), lambda i, pt, nt: (i, 0)),
                   pl.BlockSpec(memory_space=pl.ANY)],
        scratch_shapes=[pltpu.VMEM((EXPERT_ROWS * SUBLANES, LANES), F32),
                        pltpu.VMEM((SUBLANES, db), jnp.int32), pltpu.SMEM((SUBLANES, db), jnp.int32),
                        pltpu.SemaphoreType.DMA(()), pltpu.SemaphoreType.DMA(()), pltpu.SemaphoreType.DMA(())],
    )
    return pl.pallas_call(
        _dispatch_kernel,
        grid_spec=grid_spec,
        out_shape=[jax.ShapeDtypeStruct((n // db, 2, db), jnp.int32),
                   jax.ShapeDtypeStruct((n, LANES), F32),
                   jax.ShapeDtypeStruct((sorted_rows * SUBLANES, LANES), F32)],
        compiler_params=_params("arbitrary"),
        name="dispatch",
    )(pad_tile, n_tiles, rt_t, offs_b, h2_slabs)


def _expert_tile(x_ref, y_ref, wg_s, wu_s, wd_s):
    rows = EXPERT_ROWS
    x = _load_slabs(x_ref, rows).astype(BF16)
    hid = []
    for c in range(0, wg_s.shape[1], MXU_WIDTH):
        gate = jnp.dot(x, wg_s[:, c:c + MXU_WIDTH], preferred_element_type=F32)
        up = jnp.dot(x, wu_s[:, c:c + MXU_WIDTH], preferred_element_type=F32)
        hid.append((gate * jax.nn.sigmoid(gate) * up).astype(BF16))
    hid = jnp.concatenate(hid, axis=1)
    for c in range(0, wd_s.shape[1], MXU_WIDTH):
        y = jnp.dot(hid, wd_s[:, c:c + MXU_WIDTH], preferred_element_type=F32)
        for s in range(c // LANES, (c + MXU_WIDTH) // LANES):
            y_ref[pl.ds(s, rows, stride=SUBLANES), :] = y[:, s * LANES - c:(s + 1) * LANES - c]


def _experts_kernel(tile_start_ref, tiles_ref, x_hbm, wg_ref, wu_ref, wd_ref, y_hbm,
                    xbuf, ybuf, wg_s, wu_s, wd_s, sem_x, sem_y):
    e = pl.program_id(0)
    tile = EXPERT_ROWS * SUBLANES
    n = tiles_ref[e]
    base = tile_start_ref[e]
    tile_rows = lambda t: pl.ds(pl.multiple_of((base + t) * tile, tile), tile)

    def x_copy(t, slot):
        return pltpu.make_async_copy(x_hbm.at[tile_rows(t), :], xbuf.at[slot], sem_x.at[slot])

    def y_copy(t, slot):
        return pltpu.make_async_copy(ybuf.at[slot], y_hbm.at[tile_rows(t), :], sem_y.at[slot])

    @pl.when(n > 0)
    def _():
        x_copy(0, 0).start()
        wg_s[...] = wg_ref[...].astype(BF16)
        wu_s[...] = wu_ref[...].astype(BF16)
        wd_s[...] = wd_ref[...].astype(BF16)

        def tile_body(t, carry):
            slot = lax.rem(t, 2)
            x_copy(t, slot).wait()

            @pl.when(t + 1 < n)
            def _():
                x_copy(t + 1, 1 - slot).start()

            @pl.when(t >= 2)
            def _():
                y_copy(t - 2, slot).wait()

            _expert_tile(xbuf.at[slot], ybuf.at[slot], wg_s, wu_s, wd_s)
            y_copy(t, slot).start()
            return carry
        lax.fori_loop(0, n, tile_body, 0)

        @pl.when(n >= 2)
        def _():
            y_copy(n - 2, lax.rem(n, 2)).wait()
        y_copy(n - 1, lax.rem(n - 1, 2)).wait()

    @pl.when(e == pl.num_programs(0) - 1)
    def _():
        first_unused = base + n
        max_tiles = y_hbm.shape[0] // tile
        ybuf[0] = jnp.zeros(ybuf.shape[1:], ybuf.dtype)
        zero_copy = lambda t: pltpu.make_async_copy(
            ybuf.at[0], y_hbm.at[pl.ds(pl.multiple_of(t * tile, tile), tile), :], sem_y.at[0])

        def start(t, carry):
            zero_copy(t).start()
            return carry

        def wait(t, carry):
            zero_copy(t).wait()
            return carry
        lax.fori_loop(first_unused, max_tiles, start, 0)
        lax.fori_loop(first_unused, max_tiles, wait, 0)


def _experts(tile_start, tiles, xs, wg, wu, wd, layer):
    rows = EXPERT_ROWS
    _, n_experts, d, dff = wg.shape
    assert d == SUBLANES * LANES
    w_idx = lambda e, ts, nt: (layer, e, 0, 0)
    grid_spec = pltpu.PrefetchScalarGridSpec(
        num_scalar_prefetch=2,
        grid=(n_experts,),
        in_specs=[pl.BlockSpec(memory_space=pl.ANY),
                  pl.BlockSpec((None, None, d, dff), w_idx),
                  pl.BlockSpec((None, None, d, dff), w_idx),
                  pl.BlockSpec((None, None, dff, d), w_idx)],
        out_specs=pl.BlockSpec(memory_space=pl.ANY),
        scratch_shapes=[pltpu.VMEM((2, rows * SUBLANES, LANES), F32), pltpu.VMEM((2, rows * SUBLANES, LANES), F32),
                        pltpu.VMEM((d, dff), BF16), pltpu.VMEM((d, dff), BF16), pltpu.VMEM((dff, d), BF16),
                        pltpu.SemaphoreType.DMA((2,)), pltpu.SemaphoreType.DMA((2,))],
    )
    return pl.pallas_call(
        _experts_kernel,
        grid_spec=grid_spec,
        out_shape=jax.ShapeDtypeStruct(xs.shape, F32),
        compiler_params=_params("arbitrary"),
        name="experts",
    )(tile_start, tiles, xs, wg, wu, wd)


def _start_slab_gather(src_hbm, dst, sem, row_ref, base, count):
    def body(k, carry):
        _slab_copy(src_hbm, dst, sem, row_ref[base + k], k).start()
        return carry
    lax.fori_loop(0, count, body, 0, unroll=GATHER_UNROLL)


def _combine_kernel(pos_ref, ys_hbm, x1_ref, rt_ref, g_ref, o_ref, buf, sem, *, final_norm):
    i = pl.program_id(0)
    rows = COMBINE_ROWS
    n_steps = pl.num_programs(0)
    slot = i % 2

    def start(step, s):
        for j in range(2):
            _start_slab_gather(ys_hbm, buf.at[s, j], sem.at[s], pos_ref, (step * 2 + j) * rows, rows)

    @pl.when(i == 0)
    def _():
        start(0, 0)

    @pl.when(i + 1 < n_steps)
    def _():
        start(i + 1, 1 - slot)

    _wait_slabs(ys_hbm, buf.at[slot, 0], sem.at[slot])
    _wait_slabs(ys_hbm, buf.at[slot, 1], sem.at[slot])
    rt = rt_ref[...]
    lane = lax.broadcasted_iota(jnp.int32, rt.shape, 1)
    w1 = jnp.sum(jnp.where(lane == RT_W1, rt, 0.0), axis=-1, keepdims=True)
    w2 = jnp.sum(jnp.where(lane == RT_W2, rt, 0.0), axis=-1, keepdims=True)
    y = w1 * _load_slabs(buf.at[slot, 0], rows) + w2 * _load_slabs(buf.at[slot, 1], rows)
    x2 = x1_ref[...] + y
    if final_norm:
        x2 = _rms(x2, g_ref[...])
    o_ref[...] = x2


def _combine(pos, ys, x1, rt, g, final_norm):
    n, d = x1.shape
    rows = COMBINE_ROWS
    grid_spec = pltpu.PrefetchScalarGridSpec(
        num_scalar_prefetch=1,
        grid=(n // rows,),
        in_specs=[pl.BlockSpec(memory_space=pl.ANY),
                  pl.BlockSpec((rows, d), lambda i, pos: (i, 0)),
                  pl.BlockSpec((rows, LANES), lambda i, pos: (i, 0)),
                  pl.BlockSpec(g.shape, lambda i, pos: (0, 0))],
        out_specs=pl.BlockSpec((rows, d), lambda i, pos: (i, 0)),
        scratch_shapes=[pltpu.VMEM((2, 2, rows * SUBLANES, LANES), F32), pltpu.SemaphoreType.DMA((2,))],
    )
    return pl.pallas_call(
        functools.partial(_combine_kernel, final_norm=final_norm),
        grid_spec=grid_spec,
        out_shape=jax.ShapeDtypeStruct((n, d), F32),
        compiler_params=_params("arbitrary"),
        name="combine",
    )(pos, ys, x1, rt, g)


def _dispatch_plan(counts, n):
    rows = EXPERT_ROWS
    max_tiles = (2 * n) // rows + N_EXPERTS
    cnt = counts[:N_EXPERTS, 0].astype(jnp.int32)
    tiles = (cnt + rows - 1) // rows
    tile_end = jnp.cumsum(tiles)
    tile_start = tile_end - tiles
    n_tiles = tile_end[-1:]
    pad_tile = jnp.where(cnt % rows != 0, tile_end - 1, -1).astype(jnp.int32)
    offs_b = jnp.broadcast_to((tile_start * rows).astype(F32)[:, None], (N_EXPERTS, DISPATCH_ROWS))
    return tile_start.astype(jnp.int32), tiles, n_tiles, pad_tile, offs_b, max_tiles * rows


def _pad_lanes(a, width):
    return jnp.pad(a, ((0, 0), (0, width - a.shape[1])))


def kernel(x, norm_mix_g, w_in, w_g2_f, b_g_f, w_g2_b, b_g_b, gla_norm_g, rpb, w_out, norm_ffn_g, w_grp, b_grp,
           w_exp, b_exp, w_gate, w_up, w_down, final_norm_g):
    batch, seq, d = x.shape
    n = batch * seq
    rows = seq // GRID_W
    depth = w_in.shape[0]
    xf = x.reshape(n, d)
    c_na = 3 * NA_WIDTH
    c_qk = c_na + 2 * GLA_KEY_WIDTH
    c_v = c_qk + GLA_VAL_WIDTH
    c_r = c_v + GLA_VAL_WIDTH
    for l in range(depth):
        wl = w_in[l]
        q_scale = jnp.concatenate([jnp.full((NA_WIDTH,), NA_HEAD_DIM ** -0.5, F32),
                                   jnp.ones((2 * NA_WIDTH,), F32)])
        w_na = (wl[:, :c_na] * q_scale).astype(BF16)
        qk_scale = jnp.concatenate([jnp.full((GLA_KEY_WIDTH,), GLA_DK ** -0.5, F32),
                                    jnp.ones((GLA_KEY_WIDTH,), F32)])
        w_qk = (wl[:, c_na:c_qk] * qk_scale).astype(BF16)
        w_v = wl[:, c_qk:c_v].astype(BF16)
        w_r = wl[:, c_v:c_r].astype(BF16)
        w_lr = _pad_lanes(wl[:, c_r:], LANES)
        w_g2 = jnp.zeros((LANES, 2 * GLA_KEY_WIDTH), F32)
        w_g2 = w_g2.at[:GLA_GATE_RANK, :GLA_KEY_WIDTH].set(w_g2_f[l])
        w_g2 = w_g2.at[GLA_GATE_RANK:2 * GLA_GATE_RANK, GLA_KEY_WIDTH:].set(w_g2_b[l])
        b_g = jnp.concatenate([b_g_f[l], b_g_b[l]])[None, :]
        na_qkv, gqk, gv, gr, gates = _in_proj(xf, norm_mix_g[l][None, :], w_na, w_qk, w_v, w_r, w_lr, w_g2, b_g)

        y_na = _na(na_qkv, _na_bias_table(rpb[l], rows), batch, rows)
        o_f, o_b = _gla(gqk, gv, gates, batch, seq)

        w_o = w_out[l].astype(BF16)
        w_rt = _pad_lanes(jnp.concatenate([w_exp[l], w_grp[l]], axis=1), LANES)
        w_rt_hi = w_rt.astype(BF16)
        w_rt_lo = (w_rt - w_rt_hi.astype(F32)).astype(BF16)
        w_rt3 = jnp.concatenate([w_rt_hi, w_rt_hi, w_rt_lo], axis=0).T
        b_rt = _pad_lanes(jnp.concatenate([b_exp[l], b_grp[l]])[None, :], ROUTE_ROWS).T
        x1, h2_slabs, rt_t, counts = _out_route(y_na, o_f, o_b, gr, xf, w_o[:NA_WIDTH], w_o[NA_WIDTH:],
                                                gla_norm_g[l][None, :], norm_ffn_g[l][None, :], w_rt3, b_rt)

        tile_start, tiles, n_tiles, pad_tile, offs_b, sorted_rows = _dispatch_plan(counts, n)
        pos, rt, xs = _dispatch(pad_tile, n_tiles, rt_t, offs_b, h2_slabs, sorted_rows)
        ys = _experts(tile_start, tiles, xs, w_gate, w_up, w_down, l)
        pos = pos.reshape(-1)
        last = l == depth - 1
        xf = _combine(pos, ys, x1, rt, final_norm_g[None, :], final_norm=last)
    return xf.reshape(batch, seq, d)
```

```python
import functools

import jax
import jax.numpy as jnp
from jax import lax
from jax.experimental import pallas as pl
from jax.experimental.pallas import tpu as pltpu

F32 = jnp.float32
BF16 = jnp.bfloat16

GRID_W = 64
NA_HEADS = 8
NA_HEAD_DIM = 64
NA_WIDTH = NA_HEADS * NA_HEAD_DIM
WIN_H_MAX = 8
WIN_W = 16
GLA_HEADS = 4
GLA_DK = 64
GLA_DV = 128
GLA_KEY_WIDTH = GLA_HEADS * GLA_DK
GLA_VAL_WIDTH = GLA_HEADS * GLA_DV
GLA_GATE_RANK = 16
GLA_GATE_NORMALIZER = 16.0
GLA_CHUNK = 64
N_GROUPS = 4
EXPERTS_PER_GROUP = 8
N_EXPERTS = N_GROUPS * EXPERTS_PER_GROUP
RMS_EPS = 1e-6

LANES = 128
SUBLANES = 8
MXU_WIDTH = 256
VMEM_LIMIT_BYTES = 56 * 1024 * 1024

MASK_VALUE = -1e30

IN_PROJ_ROWS = 512
NA_ROW_BLOCK = 8
NA_ROW_UNROLL = 4
GLA_STEP_CHUNKS = 4
OUT_ROWS = 512
EXPERT_ROWS = 256
DISPATCH_ROWS = 512
COMBINE_ROWS = DISPATCH_ROWS
GATHER_UNROLL = 64


def _params(*sem):
    return pltpu.CompilerParams(dimension_semantics=sem, vmem_limit_bytes=VMEM_LIMIT_BYTES)


def _rms(x, g):
    return x * lax.rsqrt(jnp.mean(x * x, axis=-1, keepdims=True) + RMS_EPS) * g


def _in_proj_kernel(x_ref, g_ref, w_na_ref, w_qk_ref, w_v_ref, w_r_ref, w_lr_ref,
                    w_g2_ref, b_g_ref, na_ref, qk_ref, v_ref, r_ref, gate_ref, w_gate):
    @pl.when(pl.program_id(0) == 0)
    def _():
        w_gate[...] = jnp.dot(w_lr_ref[...], w_g2_ref[...], preferred_element_type=F32,
                              precision=lax.Precision.HIGHEST).astype(BF16)

    h = _rms(x_ref[...], g_ref[...]).astype(BF16)
    na_ref[...] = jnp.dot(h, w_na_ref[...], preferred_element_type=F32).astype(BF16)
    qk_ref[...] = jnp.dot(h, w_qk_ref[...], preferred_element_type=F32)
    v_ref[...] = jnp.dot(h, w_v_ref[...], preferred_element_type=F32).astype(BF16)
    r_ref[...] = jnp.dot(h, w_r_ref[...], preferred_element_type=F32)
    z = jnp.dot(h, w_gate[...], preferred_element_type=F32) + b_g_ref[...]
    log_sig = jnp.minimum(z, 0.0) - jnp.log(1.0 + jnp.exp(-jnp.abs(z)))
    gate_ref[...] = log_sig * (1.0 / GLA_GATE_NORMALIZER)


def _in_proj(x, g, w_na, w_qk, w_v, w_r, w_lr, w_g2, b_g):
    n, d = x.shape
    rows = IN_PROJ_ROWS
    row_spec = lambda width: pl.BlockSpec((rows, width), lambda i: (i, 0))
    full = lambda a: pl.BlockSpec(a.shape, lambda i: (0,) * a.ndim)
    return pl.pallas_call(
        _in_proj_kernel,
        grid=(n // rows,),
        in_specs=[row_spec(d), full(g), full(w_na), full(w_qk), full(w_v), full(w_r), full(w_lr),
                  full(w_g2), full(b_g)],
        out_specs=[row_spec(3 * NA_WIDTH), row_spec(2 * GLA_KEY_WIDTH), row_spec(GLA_VAL_WIDTH),
                   row_spec(GLA_VAL_WIDTH), row_spec(2 * GLA_KEY_WIDTH)],
        out_shape=[jax.ShapeDtypeStruct((n, 3 * NA_WIDTH), BF16),
                   jax.ShapeDtypeStruct((n, 2 * GLA_KEY_WIDTH), F32),
                   jax.ShapeDtypeStruct((n, GLA_VAL_WIDTH), BF16),
                   jax.ShapeDtypeStruct((n, GLA_VAL_WIDTH), F32),
                   jax.ShapeDtypeStruct((n, 2 * GLA_KEY_WIDTH), F32)],
        scratch_shapes=[pltpu.VMEM((d, 2 * GLA_KEY_WIDTH), BF16)],
        compiler_params=_params("arbitrary"),
        name="in_proj",
    )(x, g, w_na, w_qk, w_v, w_r, w_lr, w_g2, b_g)


def _na_bias_table(rpb, rows):
    kh = min(WIN_H_MAX, rows)
    cls = jnp.arange(kh)[:, None, None]
    i = jnp.arange(kh)[None, :, None]
    w = jnp.arange(GRID_W)[:, None, None]
    x = jnp.arange(GRID_W)[None, :, None]
    cs = jnp.clip(w - WIN_W // 2, 0, GRID_W - WIN_W)
    valid = (x >= cs) & (x < cs + WIN_W)
    row_sel = (jnp.arange(2 * WIN_H_MAX - 1)[None, None, :] == i - cls + (WIN_H_MAX - 1)).astype(F32)
    col_sel = (valid & (jnp.arange(2 * WIN_W - 1)[None, None, :] == x - w + (WIN_W - 1))).astype(F32)
    hi = lax.Precision.HIGHEST
    by_col = jnp.einsum("hrc,wxc->hrwx", rpb.astype(F32), col_sel, precision=hi)
    bias = jnp.einsum("kir,hrwx->khwix", row_sel, by_col, precision=hi)
    bias = jnp.where(valid[None, None, :, None, :, 0], bias, MASK_VALUE)
    return bias.reshape(kh, NA_HEADS // 2, 2 * GRID_W, kh * GRID_W)


def _na_kernel(q_ref, kp_ref, kc_ref, kn_ref, vp_ref, vc_ref, vn_ref, tbl_ref, o_ref,
               kbuf, vbuf, *, rows, kh):
    rb = NA_ROW_BLOCK
    j = pl.program_id(1)
    blk = rb * GRID_W
    for s, (k_src, v_src) in enumerate(((kp_ref, vp_ref), (kc_ref, vc_ref), (kn_ref, vn_ref))):
        kbuf[s * blk:(s + 1) * blk, :] = k_src[...].reshape(blk, NA_WIDTH)
        vbuf[s * blk:(s + 1) * blk, :] = v_src[...].reshape(blk, NA_WIDTH)
    lane = lax.broadcasted_iota(jnp.int32, (GRID_W, LANES), 1)
    first = lane < NA_HEAD_DIM

    def row_body(lr, carry):
        r = j * rb + lr
        start = jnp.clip(r - kh // 2, 0, rows - kh)
        cls = r - start
        local = pl.multiple_of((start - (j - 1) * rb) * GRID_W, GRID_W)
        q_row = q_ref[lr]
        k_win = kbuf[pl.ds(local, kh * GRID_W), :]
        v_win = vbuf[pl.ds(local, kh * GRID_W), :]
        pairs = [slice(p * LANES, (p + 1) * LANES) for p in range(NA_HEADS // 2)]
        scores = []
        for sl in pairs:
            q_pair = q_row[:, sl]
            zero = jnp.zeros_like(q_pair)
            q_bd = jnp.concatenate([jnp.where(first, q_pair, zero), jnp.where(first, zero, q_pair)], axis=0)
            scores.append(lax.dot_general(q_bd, k_win[:, sl], (((1,), (1,)), ((), ())),
                                          preferred_element_type=F32))
        probs, denoms = [], []
        for p, s in enumerate(scores):
            s = s + tbl_ref[cls, p]
            e = jnp.exp(s - jnp.max(s, axis=-1, keepdims=True))
            denoms.append(jnp.sum(e, axis=-1, keepdims=True))
            probs.append(e.astype(BF16))
        outs = []
        for sl, e, denom in zip(pairs, probs, denoms):
            o = jnp.dot(e, v_win[:, sl], preferred_element_type=F32) / denom
            outs.append(jnp.where(first, o[:GRID_W], o[GRID_W:]))
        o_ref[lr] = jnp.concatenate(outs, axis=-1).astype(o_ref.dtype)
        return carry

    lax.fori_loop(0, rb, row_body, 0, unroll=NA_ROW_UNROLL)


def _na(na_qkv, tbl, batch, rows):
    kh = min(WIN_H_MAX, rows)
    rb = NA_ROW_BLOCK
    nblk = rows // rb
    x4 = na_qkv.reshape(batch, rows, GRID_W, 3 * NA_WIDTH)
    blk = (None, rb, GRID_W, NA_WIDTH)
    prev = lambda j: jnp.maximum(j - 1, 0)
    nxt = lambda j: jnp.minimum(j + 1, nblk - 1)
    specs = [pl.BlockSpec(blk, lambda b, j: (b, j, 0, 0))]
    for col in (1, 2):
        specs += [pl.BlockSpec(blk, lambda b, j, col=col: (b, prev(j), 0, col)),
                  pl.BlockSpec(blk, lambda b, j, col=col: (b, j, 0, col)),
                  pl.BlockSpec(blk, lambda b, j, col=col: (b, nxt(j), 0, col))]
    specs.append(pl.BlockSpec(tbl.shape, lambda b, j: (0, 0, 0, 0)))
    out = pl.pallas_call(
        functools.partial(_na_kernel, rows=rows, kh=kh),
        grid=(batch, nblk),
        in_specs=specs,
        out_specs=pl.BlockSpec(blk, lambda b, j: (b, j, 0, 0)),
        out_shape=jax.ShapeDtypeStruct((batch, rows, GRID_W, NA_WIDTH), BF16),
        scratch_shapes=[pltpu.VMEM((3 * rb * GRID_W, NA_WIDTH), BF16),
                        pltpu.VMEM((3 * rb * GRID_W, NA_WIDTH), BF16)],
        compiler_params=_params("arbitrary", "arbitrary"),
        name="na",
    )(x4, x4, x4, x4, x4, x4, x4, tbl)
    return out.reshape(batch * rows * GRID_W, NA_WIDTH)


def _block_diag_mask(row_block, col_block, nblocks):
    shape = (row_block * nblocks, col_block * nblocks)
    r = lax.shift_right_logical(lax.broadcasted_iota(jnp.int32, shape, 0), row_block.bit_length() - 1)
    c = lax.shift_right_logical(lax.broadcasted_iota(jnp.int32, shape, 1), col_block.bit_length() - 1)
    return r == c


def _split_bf16x3(x):
    hi = x.astype(BF16)
    rest = x - hi.astype(F32)
    mid = rest.astype(BF16)
    lo = (rest - mid.astype(F32)).astype(BF16)
    return hi, mid, lo


def _gla_direction(qk_ref, v_ref, g_ref, *, backward):
    c = GLA_CHUNK
    nc = GLA_STEP_CHUNKS
    kw = GLA_KEY_WIDTH
    order = list(reversed(range(nc))) if backward else list(range(nc))
    chunk = lambda a, n: a[n * c:(n + 1) * c]

    def stage_cumsum():
        step = nc * c
        ti = lax.broadcasted_iota(jnp.int32, (step, step), 0)
        tj = lax.broadcasted_iota(jnp.int32, (step, step), 1)
        same_chunk = lax.shift_right_logical(ti, c.bit_length() - 1) == lax.shift_right_logical(tj, c.bit_length() - 1)
        tri = (tj >= ti) if backward else (tj <= ti)
        cum = jnp.where(same_chunk & tri, 1.0, 0.0).astype(BF16)
        return sum(jnp.dot(cum, piece, preferred_element_type=F32) for piece in _split_bf16x3(g_ref[...]))

    def stage_decays(b):
        ref_row, last_row = (c // 2, 0) if backward else (c // 2 - 1, c - 1)
        rows_of = lambda r: jnp.concatenate(
            [jnp.broadcast_to(b[n * c + r:n * c + r + 1, :], (c, kw)) for n in range(nc)], axis=0)
        b_ref, b_last = rows_of(ref_row), rows_of(last_row)
        q = qk_ref[:, :kw]
        k = qk_ref[:, kw:]
        q_rel = (q * jnp.exp(b - b_ref)).astype(BF16)
        k_rel = (k * jnp.exp(b_ref - b)).astype(BF16)
        k_dec = k * jnp.exp(b_last - b)
        q_dec = (q * jnp.exp(b)).astype(BF16)
        decay = jnp.exp(b_last)
        return q_rel, k_rel, k_dec, q_dec, decay

    def stage_scores(q_rel, k_rel):
        kk_mask = _block_diag_mask(c, GLA_DK, GLA_HEADS)
        out = []
        for n in range(nc):
            k_bd = jnp.where(kk_mask, jnp.concatenate([chunk(k_rel, n)] * GLA_HEADS, axis=0), jnp.zeros((), BF16))
            out.append(lax.dot_general(chunk(q_rel, n), k_bd, (((1,), (1,)), ((), ())),
                                       preferred_element_type=F32))
        return out

    def stage_intra(scores, k_dec, decay):
        si = lax.broadcasted_iota(jnp.int32, (c, c * GLA_HEADS), 0)
        sj = lax.broadcasted_iota(jnp.int32, (c, c * GLA_HEADS), 1) & (c - 1)
        keep = (sj > si) if backward else (sj <= si)
        kv_mask = _block_diag_mask(c, GLA_DV, GLA_HEADS)
        vk_mask = _block_diag_mask(GLA_DK, GLA_DV, 2)
        o_intra, upd, decay_col = [], [], []
        for n in range(nc):
            v = v_ref[n * c:(n + 1) * c, :]
            p = jnp.where(keep, scores[n], 0.0).astype(BF16)
            v_bd = jnp.where(kv_mask, jnp.concatenate([v] * GLA_HEADS, axis=0), jnp.zeros((), BF16))
            o_intra.append(jnp.dot(p, v_bd, preferred_element_type=F32))
            k_dec_t = chunk(k_dec, n).T.astype(BF16)
            pairs = []
            for hp in range(GLA_HEADS // 2):
                kv = jnp.dot(k_dec_t[hp * 2 * GLA_DK:(hp + 1) * 2 * GLA_DK],
                             v[:, hp * 2 * GLA_DV:(hp + 1) * 2 * GLA_DV], preferred_element_type=F32)
                pairs.append(jnp.where(vk_mask, kv, 0.0))
            upd.append(pairs)
            decay_col.append(chunk(decay, n).T[:, :1])
        return o_intra, upd, decay_col

    def scan_step(state, idx, q_dec, o_intra, upd, decay_col, out_ref):
        n = order[idx]
        q_n = chunk(q_dec, n)
        pw = 2 * GLA_DK
        o_inter = jnp.concatenate(
            [jnp.dot(q_n[:, hp * pw:(hp + 1) * pw], s.astype(BF16), preferred_element_type=F32)
             for hp, s in enumerate(state)], axis=1)
        out_ref[n * c:(n + 1) * c, :] = o_intra[n] + o_inter
        return [s * decay_col[n][hp * pw:(hp + 1) * pw] + u for hp, (s, u) in enumerate(zip(state, upd[n]))]

    return stage_cumsum, stage_decays, stage_scores, stage_intra, scan_step


def _gla_kernel(qk_f_ref, v_f_ref, g_f_ref, qk_b_ref, v_b_ref, g_b_ref, o_f_ref, o_b_ref,
                st_f, st_b):
    @pl.when(pl.program_id(1) == 0)
    def _():
        st_f[...] = jnp.zeros_like(st_f)
        st_b[...] = jnp.zeros_like(st_b)

    dirs = (_gla_direction(qk_f_ref, v_f_ref, g_f_ref, backward=False),
            _gla_direction(qk_b_ref, v_b_ref, g_b_ref, backward=True))
    b = [d[0]() for d in dirs]
    dec = [d[1](x) for d, x in zip(dirs, b)]
    scores = [d[2](x[0], x[1]) for d, x in zip(dirs, dec)]
    intra = [d[3](s, x[2], x[4]) for d, s, x in zip(dirs, scores, dec)]
    pairs = range(GLA_HEADS // 2)
    states = [[st[hp] for hp in pairs] for st in (st_f, st_b)]
    for idx in range(GLA_STEP_CHUNKS):
        for j, (d, x, y, out_ref) in enumerate(zip(dirs, dec, intra, (o_f_ref, o_b_ref))):
            states[j] = d[4](states[j], idx, x[3], *y, out_ref)
    for st, state in zip((st_f, st_b), states):
        for hp in pairs:
            st[hp] = state[hp]


def _gla(qk, v, gates, batch, seq):
    step = GLA_STEP_CHUNKS * GLA_CHUNK
    nblk = seq // step
    qk3 = qk.reshape(batch, seq, 2 * GLA_KEY_WIDTH)
    v3 = v.reshape(batch, seq, GLA_VAL_WIDTH)
    g3 = gates.reshape(batch, seq, 2 * GLA_KEY_WIDTH)
    fwd = lambda b, n: (b, n, 0)
    bwd = lambda b, n: (b, nblk - 1 - n, 0)
    bwd_gate = lambda b, n: (b, nblk - 1 - n, 1)
    o_f, o_b = pl.pallas_call(
        _gla_kernel,
        grid=(batch, nblk),
        in_specs=[pl.BlockSpec((None, step, 2 * GLA_KEY_WIDTH), fwd),
                  pl.BlockSpec((None, step, GLA_VAL_WIDTH), fwd),
                  pl.BlockSpec((None, step, GLA_KEY_WIDTH), fwd),
                  pl.BlockSpec((None, step, 2 * GLA_KEY_WIDTH), bwd),
                  pl.BlockSpec((None, step, GLA_VAL_WIDTH), bwd),
                  pl.BlockSpec((None, step, GLA_KEY_WIDTH), bwd_gate)],
        out_specs=[pl.BlockSpec((None, step, GLA_VAL_WIDTH), fwd),
                   pl.BlockSpec((None, step, GLA_VAL_WIDTH), bwd)],
        out_shape=[jax.ShapeDtypeStruct((batch, seq, GLA_VAL_WIDTH), F32)] * 2,
        scratch_shapes=[pltpu.VMEM((GLA_HEADS // 2, 2 * GLA_DK, 2 * GLA_DV), F32),
                        pltpu.VMEM((GLA_HEADS // 2, 2 * GLA_DK, 2 * GLA_DV), F32)],
        compiler_params=_params("arbitrary", "arbitrary"),
        name="gla",
    )(qk3, v3, g3, qk3, v3, g3)
    return o_f.reshape(batch * seq, GLA_VAL_WIDTH), o_b.reshape(batch * seq, GLA_VAL_WIDTH)


def _store_slabs(ref, x):
    rows = x.shape[0]
    for s in range(SUBLANES):
        ref[pl.ds(s, rows, stride=SUBLANES), :] = x[:, s * LANES:(s + 1) * LANES]


def _load_slabs(ref, rows):
    return jnp.concatenate([ref[pl.ds(s, rows, stride=SUBLANES), :] for s in range(SUBLANES)], axis=1)


def _slab_copy(src_hbm, dst, sem, src_row, k):
    dst_row = k * SUBLANES if isinstance(k, int) else pl.multiple_of(k * SUBLANES, SUBLANES)
    return pltpu.make_async_copy(src_hbm.at[pl.ds(pl.multiple_of(src_row, SUBLANES), SUBLANES), :],
                                 dst.at[pl.ds(dst_row, SUBLANES), :], sem)


def _wait_slabs(src_hbm, dst, sem):
    pltpu.make_async_copy(src_hbm.at[pl.ds(0, dst.shape[0]), :], dst, sem).wait()


RT_E1, RT_E2, RT_RANK1, RT_RANK2, RT_W1, RT_W2 = range(6)
GROUP_ROW0 = N_EXPERTS
ROUTE_ROWS = 64


def _out_route_kernel(na_ref, of_ref, ob_ref, r_ref, x_ref, w_na_ref, w_gla_ref, gn_ref, fn_ref,
                      w_rt_ref, b_rt_ref, x1_ref, h2_ref, rt_ref, cnt_ref, carry):
    @pl.when(pl.program_id(0) == 0)
    def _():
        carry[...] = jnp.zeros_like(carry)

    rows = x_ref.shape[0]
    o = of_ref[...] + ob_ref[...]
    r = r_ref[...]
    parts = []
    for h in range(GLA_HEADS):
        sl = slice(h * GLA_DV, (h + 1) * GLA_DV)
        parts.append(_rms(o[:, sl], gn_ref[...]) * (r[:, sl] * jax.nn.sigmoid(r[:, sl])))
    y_gla = jnp.concatenate(parts, axis=-1).astype(BF16)
    x1 = (x_ref[...] + jnp.dot(na_ref[...], w_na_ref[...], preferred_element_type=F32)
          + jnp.dot(y_gla, w_gla_ref[...], preferred_element_type=F32))
    x1_ref[...] = x1
    h2 = _rms(x1, fn_ref[...])
    _store_slabs(h2_ref, h2)

    h_hi = h2.astype(BF16)
    h_lo = (h2 - h_hi.astype(F32)).astype(BF16)
    logits = lax.dot_general(w_rt_ref[...], jnp.concatenate([h_hi, h_lo, h_hi], axis=1),
                             (((1,), (1,)), ((), ())), preferred_element_type=F32)
    logits = logits[:ROUTE_ROWS] + b_rt_ref[...]
    row_i = lax.broadcasted_iota(jnp.int32, (ROUTE_ROWS, rows), 0)
    row = row_i.astype(F32)
    row_grp = lax.shift_right_logical(row_i, EXPERTS_PER_GROUP.bit_length() - 1).astype(F32)
    neg = jnp.float32(-jnp.inf)
    no_row = jnp.float32(ROUTE_ROWS)
    is_grp = (row_i >= GROUP_ROW0) & (row_i < GROUP_ROW0 + N_GROUPS)
    g_logit = jnp.where(is_grp, logits, neg)
    g_max = jnp.max(g_logit, axis=0, keepdims=True)
    g_sel = jnp.min(jnp.where(is_grp & (g_logit == g_max), row, no_row), axis=0, keepdims=True) - GROUP_ROW0
    grp_w = 1.0 / jnp.sum(jnp.where(is_grp, jnp.exp(g_logit - g_max), 0.0), axis=0, keepdims=True)
    in_grp = (row_i < N_EXPERTS) & (row_grp == g_sel)
    e_logit = jnp.where(in_grp, logits, neg)
    v1 = jnp.max(e_logit, axis=0, keepdims=True)
    i1 = jnp.min(jnp.where(in_grp & (e_logit == v1), row, no_row), axis=0, keepdims=True)
    rest = in_grp & (row != i1)
    e_logit2 = jnp.where(rest, logits, neg)
    v2 = jnp.max(e_logit2, axis=0, keepdims=True)
    i2 = jnp.min(jnp.where(rest & (e_logit2 == v2), row, no_row), axis=0, keepdims=True)
    t = jnp.exp(v2 - v1)
    w1 = grp_w / (1.0 + t)
    w2 = grp_w * t / (1.0 + t)

    sel1 = row == i1
    sel2 = row == i2
    onehot = jnp.where(sel1 | sel2, 1.0, 0.0)
    ti = lax.broadcasted_iota(jnp.int32, (rows, rows), 0)
    tj = lax.broadcasted_iota(jnp.int32, (rows, rows), 1)
    earlier = jnp.where(ti < tj, 1.0, 0.0).astype(BF16)
    ranks = jnp.dot(onehot.astype(BF16), earlier, preferred_element_type=F32) + carry[:, :1]
    rank1 = jnp.sum(jnp.where(sel1, ranks, 0.0), axis=0, keepdims=True)
    rank2 = jnp.sum(jnp.where(sel2, ranks, 0.0), axis=0, keepdims=True)
    new_carry = carry[...] + jnp.sum(onehot, axis=1, keepdims=True)
    carry[...] = new_carry
    cnt_ref[...] = new_carry

    fields = {RT_E1: i1, RT_E2: i2, RT_RANK1: rank1, RT_RANK2: rank2, RT_W1: w1, RT_W2: w2}
    zero = jnp.zeros_like(w1)
    rt_ref[...] = jnp.concatenate([fields.get(f, zero) for f in range(SUBLANES)], axis=0)


def _out_route(y_na, o_f, o_b, r, x, w_na, w_gla, gn, fn, w_rt, b_rt):
    n, d = x.shape
    rows = OUT_ROWS
    row_spec = lambda width: pl.BlockSpec((rows, width), lambda i: (i, 0))
    full = lambda a: pl.BlockSpec(a.shape, lambda i: (0,) * a.ndim)
    return pl.pallas_call(
        _out_route_kernel,
        grid=(n // rows,),
        in_specs=[row_spec(NA_WIDTH), row_spec(GLA_VAL_WIDTH), row_spec(GLA_VAL_WIDTH),
                  row_spec(GLA_VAL_WIDTH), row_spec(d), full(w_na), full(w_gla), full(gn), full(fn),
                  full(w_rt), full(b_rt)],
        out_specs=[row_spec(d), pl.BlockSpec((rows * SUBLANES, LANES), lambda i: (i, 0)),
                   pl.BlockSpec((SUBLANES, rows), lambda i: (0, i)),
                   pl.BlockSpec((ROUTE_ROWS, LANES), lambda i: (0, 0))],
        out_shape=[jax.ShapeDtypeStruct((n, d), F32), jax.ShapeDtypeStruct((n * SUBLANES, LANES), F32),
                   jax.ShapeDtypeStruct((SUBLANES, n), F32), jax.ShapeDtypeStruct((ROUTE_ROWS, LANES), F32)],
        scratch_shapes=[pltpu.VMEM((ROUTE_ROWS, LANES), F32)],
        compiler_params=_params("arbitrary"),
        name="out_route",
    )(y_na, o_f, o_b, r, x, w_na, w_gla, gn, fn, w_rt, b_rt)


def _dispatch_kernel(pad_tile_ref, n_tiles_ref, rt_ref, offs_ref, h_ref, pos_ref, rec_ref, xs_hbm,
                     zeros, pos_vmem, pos_smem, sem_zero, sem_pos, sem):
    i = pl.program_id(0)
    db = DISPATCH_ROWS
    tile = EXPERT_ROWS * SUBLANES
    max_tiles = xs_hbm.shape[0] // tile

    @pl.when(i == 0)
    def _():
        zeros[...] = jnp.zeros_like(zeros)

        def zero_copy(t):
            return pltpu.make_async_copy(zeros, xs_hbm.at[pl.ds(pl.multiple_of(t * tile, tile), tile), :], sem_zero)

        def for_each_zero_tile(fn):
            def padded(e, carry):
                @pl.when(pad_tile_ref[e] >= 0)
                def _():
                    fn(zero_copy(pad_tile_ref[e]))
                return carry
            lax.fori_loop(0, N_EXPERTS, padded, 0)

            def tail(t, carry):
                fn(zero_copy(t))
                return carry
            lax.fori_loop(n_tiles_ref[0], max_tiles, tail, 0)

        for_each_zero_tile(lambda cp: cp.start())
        for_each_zero_tile(lambda cp: cp.wait())

    rt_t = rt_ref[...]
    rec_ref[...] = jnp.concatenate([rt_t, jnp.zeros((LANES - SUBLANES, db), F32)], axis=0).T
    expert = lax.broadcasted_iota(jnp.int32, (N_EXPERTS, db), 0).astype(F32)
    pos = []
    for e_lane, rank_lane in ((RT_E1, RT_RANK1), (RT_E2, RT_RANK2)):
        seg = jnp.sum(jnp.where(expert == rt_t[e_lane:e_lane + 1, :], offs_ref[...], 0.0), axis=0, keepdims=True)
        pos.append((seg + rt_t[rank_lane:rank_lane + 1, :]) * SUBLANES)
    pos = jnp.concatenate(pos, axis=0).astype(jnp.int32)
    pos_ref[...] = pos
    pos_vmem[...] = jnp.concatenate([pos, jnp.zeros((SUBLANES - 2, db), jnp.int32)], axis=0)
    to_smem = pltpu.make_async_copy(pos_vmem, pos_smem, sem_pos)
    to_smem.start()
    to_smem.wait()

    for k in range(db):
        src = h_ref.at[pl.ds(k * SUBLANES, SUBLANES), :]
        for slot in range(2):
            dst_row = pl.multiple_of(pos_smem[slot, k], SUBLANES)
            pltpu.make_async_copy(src, xs_hbm.at[pl.ds(dst_row, SUBLANES), :], sem).start(priority=slot)
    for slot in range(2):
        pltpu.make_async_copy(h_ref, xs_hbm.at[pl.ds(0, db * SUBLANES), :], sem).wait()


def _dispatch(pad_tile, n_tiles, rt_t, offs_b, h2_slabs, sorted_rows):
    n = rt_t.shape[1]
    db = DISPATCH_ROWS
    grid_spec = pltpu.PrefetchScalarGridSpec(
        num_scalar_prefetch=2,
        grid=(n // db,),
        in_specs=[pl.BlockSpec((SUBLANES, db), lambda i, pt, nt: (0, i)),
                  pl.BlockSpec(offs_b.shape, lambda i, pt, nt: (0, 0)),
                  pl.BlockSpec((db * SUBLANES, LANES), lambda i, pt, nt: (i, 0))],
        out_specs=[pl.BlockSpec((None, 2, db), lambda i, pt, nt: (i, 0, 0)),
                   pl.BlockSpec((db, LANES), lambda i, pt, nt: (i, 0)),
                   pl.BlockSpec(memory_space=pl.ANY)],
        scratch_shapes=[pltpu.VMEM((EXPERT_ROWS * SUBLANES, LANES), F32),
                        pltpu.VMEM((SUBLANES, db), jnp.int32), pltpu.SMEM((SUBLANES, db), jnp.int32),
                        pltpu.SemaphoreType.DMA(()), pltpu.SemaphoreType.DMA(()), pltpu.SemaphoreType.DMA(())],
    )
    return pl.pallas_call(
        _dispatch_kernel,
        grid_spec=grid_spec,
        out_shape=[jax.ShapeDtypeStruct((n // db, 2, db), jnp.int32),
                   jax.ShapeDtypeStruct((n, LANES), F32),
                   jax.ShapeDtypeStruct((sorted_rows * SUBLANES, LANES), F32)],
        compiler_params=_params("arbitrary"),
        name="dispatch",
    )(pad_tile, n_tiles, rt_t, offs_b, h2_slabs)


TILE_DMA_PRIORITY = 1


def _expert_tile(x_ref, y_ref, wg_s, wu_s, wd_s):
    rows = EXPERT_ROWS
    x = _load_slabs(x_ref, rows).astype(BF16)
    hid = []
    for c in range(0, wg_s.shape[1], MXU_WIDTH):
        gate = jnp.dot(x, wg_s[:, c:c + MXU_WIDTH], preferred_element_type=F32)
        up = jnp.dot(x, wu_s[:, c:c + MXU_WIDTH], preferred_element_type=F32)
        hid.append((gate * jax.nn.sigmoid(gate) * up).astype(BF16))
    hid = jnp.concatenate(hid, axis=1)
    for c in range(0, wd_s.shape[1], MXU_WIDTH):
        y = jnp.dot(hid, wd_s[:, c:c + MXU_WIDTH], preferred_element_type=F32)
        for s in range(c // LANES, (c + MXU_WIDTH) // LANES):
            y_ref[pl.ds(s, rows, stride=SUBLANES), :] = y[:, s * LANES - c:(s + 1) * LANES - c]


def _experts_kernel(tile_start_ref, tiles_ref, x_hbm, wg_ref, wu_ref, wd_ref, y_hbm,
                    xbuf, ybuf, wg_s, wu_s, wd_s, sem_x, sem_y):
    e = pl.program_id(0)
    tile = EXPERT_ROWS * SUBLANES
    n = tiles_ref[e]
    base = tile_start_ref[e]
    tile_rows = lambda t: pl.ds(pl.multiple_of((base + t) * tile, tile), tile)

    def x_copy(t, slot):
        return pltpu.make_async_copy(x_hbm.at[tile_rows(t), :], xbuf.at[slot], sem_x.at[slot])

    def y_copy(t, slot):
        return pltpu.make_async_copy(ybuf.at[slot], y_hbm.at[tile_rows(t), :], sem_y.at[slot])

    @pl.when(n > 0)
    def _():
        x_copy(0, 0).start(priority=TILE_DMA_PRIORITY)
        wg_s[...] = wg_ref[...].astype(BF16)
        wu_s[...] = wu_ref[...].astype(BF16)
        wd_s[...] = wd_ref[...].astype(BF16)

        def tile_body(t, carry):
            slot = lax.rem(t, 2)
            x_copy(t, slot).wait()

            @pl.when(t + 1 < n)
            def _():
                x_copy(t + 1, 1 - slot).start(priority=TILE_DMA_PRIORITY)

            @pl.when(t >= 2)
            def _():
                y_copy(t - 2, slot).wait()

            _expert_tile(xbuf.at[slot], ybuf.at[slot], wg_s, wu_s, wd_s)
            y_copy(t, slot).start(priority=TILE_DMA_PRIORITY)
            return carry
        lax.fori_loop(0, n, tile_body, 0)

        @pl.when(n >= 2)
        def _():
            y_copy(n - 2, lax.rem(n, 2)).wait()
        y_copy(n - 1, lax.rem(n - 1, 2)).wait()

    @pl.when(e == pl.num_programs(0) - 1)
    def _():
        first_unused = base + n
        max_tiles = y_hbm.shape[0] // tile
        ybuf[0] = jnp.zeros(ybuf.shape[1:], ybuf.dtype)
        zero_copy = lambda t: pltpu.make_async_copy(
            ybuf.at[0], y_hbm.at[pl.ds(pl.multiple_of(t * tile, tile), tile), :], sem_y.at[0])

        def start(t, carry):
            zero_copy(t).start()
            return carry

        def wait(t, carry):
            zero_copy(t).wait()
            return carry
        lax.fori_loop(first_unused, max_tiles, start, 0)
        lax.fori_loop(first_unused, max_tiles, wait, 0)


def _experts(tile_start, tiles, xs, wg, wu, wd, layer):
    rows = EXPERT_ROWS
    _, n_experts, d, dff = wg.shape
    assert d == SUBLANES * LANES
    w_idx = lambda e, ts, nt: (layer, e, 0, 0)
    grid_spec = pltpu.PrefetchScalarGridSpec(
        num_scalar_prefetch=2,
        grid=(n_experts,),
        in_specs=[pl.BlockSpec(memory_space=pl.ANY),
                  pl.BlockSpec((None, None, d, dff), w_idx),
                  pl.BlockSpec((None, None, d, dff), w_idx),
                  pl.BlockSpec((None, None, dff, d), w_idx)],
        out_specs=pl.BlockSpec(memory_space=pl.ANY),
        scratch_shapes=[pltpu.VMEM((2, rows * SUBLANES, LANES), F32), pltpu.VMEM((2, rows * SUBLANES, LANES), F32),
                        pltpu.VMEM((d, dff), BF16), pltpu.VMEM((d, dff), BF16), pltpu.VMEM((dff, d), BF16),
                        pltpu.SemaphoreType.DMA((2,)), pltpu.SemaphoreType.DMA((2,))],
    )
    return pl.pallas_call(
        _experts_kernel,
        grid_spec=grid_spec,
        out_shape=jax.ShapeDtypeStruct(xs.shape, F32),
        compiler_params=_params("arbitrary"),
        name="experts",
    )(tile_start, tiles, xs, wg, wu, wd)


def _start_slab_gather(src_hbm, dst, sem, row_ref, base, count, priority):
    def body(k, carry):
        _slab_copy(src_hbm, dst, sem, row_ref[base + k], k).start(priority=priority)
        return carry
    lax.fori_loop(0, count, body, 0, unroll=GATHER_UNROLL)


def _combine_kernel(pos_ref, ys_hbm, x1_ref, rt_ref, g_ref, o_ref, buf, sem, *, final_norm):
    i = pl.program_id(0)
    rows = COMBINE_ROWS
    n_steps = pl.num_programs(0)
    slot = i % 2

    def start(step, s):
        for j in range(2):
            _start_slab_gather(ys_hbm, buf.at[s, j], sem.at[s], pos_ref, (step * 2 + j) * rows, rows, priority=j)

    @pl.when(i == 0)
    def _():
        start(0, 0)

    @pl.when(i + 1 < n_steps)
    def _():
        start(i + 1, 1 - slot)

    _wait_slabs(ys_hbm, buf.at[slot, 0], sem.at[slot])
    _wait_slabs(ys_hbm, buf.at[slot, 1], sem.at[slot])
    rt = rt_ref[...]
    lane = lax.broadcasted_iota(jnp.int32, rt.shape, 1)
    w1 = jnp.sum(jnp.where(lane == RT_W1, rt, 0.0), axis=-1, keepdims=True)
    w2 = jnp.sum(jnp.where(lane == RT_W2, rt, 0.0), axis=-1, keepdims=True)
    y = w1 * _load_slabs(buf.at[slot, 0], rows) + w2 * _load_slabs(buf.at[slot, 1], rows)
    x2 = x1_ref[...] + y
    if final_norm:
        x2 = _rms(x2, g_ref[...])
    o_ref[...] = x2


def _combine(pos, ys, x1, rt, g, final_norm):
    n, d = x1.shape
    rows = COMBINE_ROWS
    grid_spec = pltpu.PrefetchScalarGridSpec(
        num_scalar_prefetch=1,
        grid=(n // rows,),
        in_specs=[pl.BlockSpec(memory_space=pl.ANY),
                  pl.BlockSpec((rows, d), lambda i, pos: (i, 0)),
                  pl.BlockSpec((rows, LANES), lambda i, pos: (i, 0)),
                  pl.BlockSpec(g.shape, lambda i, pos: (0, 0))],
        out_specs=pl.BlockSpec((rows, d), lambda i, pos: (i, 0)),
        scratch_shapes=[pltpu.VMEM((2, 2, rows * SUBLANES, LANES), F32), pltpu.SemaphoreType.DMA((2,))],
    )
    return pl.pallas_call(
        functools.partial(_combine_kernel, final_norm=final_norm),
        grid_spec=grid_spec,
        out_shape=jax.ShapeDtypeStruct((n, d), F32),
        compiler_params=_params("arbitrary"),
        name="combine",
    )(pos, ys, x1, rt, g)


def _dispatch_plan(counts, n):
    rows = EXPERT_ROWS
    max_tiles = (2 * n) // rows + N_EXPERTS
    cnt = counts[:N_EXPERTS, 0].astype(jnp.int32)
    tiles = (cnt + rows - 1) // rows
    tile_end = jnp.cumsum(tiles)
    tile_start = tile_end - tiles
    n_tiles = tile_end[-1:]
    pad_tile = jnp.where(cnt % rows != 0, tile_end - 1, -1).astype(jnp.int32)
    offs_b = jnp.broadcast_to((tile_start * rows).astype(F32)[:, None], (N_EXPERTS, DISPATCH_ROWS))
    return tile_start.astype(jnp.int32), tiles, n_tiles, pad_tile, offs_b, max_tiles * rows


def _pad_lanes(a, width):
    return jnp.pad(a, ((0, 0), (0, width - a.shape[1])))


def kernel(x, norm_mix_g, w_in, w_g2_f, b_g_f, w_g2_b, b_g_b, gla_norm_g, rpb, w_out, norm_ffn_g, w_grp, b_grp,
           w_exp, b_exp, w_gate, w_up, w_down, final_norm_g):
    batch, seq, d = x.shape
    n = batch * seq
    rows = seq // GRID_W
    depth = w_in.shape[0]
    xf = x.reshape(n, d)
    c_na = 3 * NA_WIDTH
    c_qk = c_na + 2 * GLA_KEY_WIDTH
    c_v = c_qk + GLA_VAL_WIDTH
    c_r = c_v + GLA_VAL_WIDTH
    for l in range(depth):
        wl = w_in[l]
        q_scale = jnp.concatenate([jnp.full((NA_WIDTH,), NA_HEAD_DIM ** -0.5, F32),
                                   jnp.ones((2 * NA_WIDTH,), F32)])
        w_na = (wl[:, :c_na] * q_scale).astype(BF16)
        qk_scale = jnp.concatenate([jnp.full((GLA_KEY_WIDTH,), GLA_DK ** -0.5, F32),
                                    jnp.ones((GLA_KEY_WIDTH,), F32)])
        w_qk = (wl[:, c_na:c_qk] * qk_scale).astype(BF16)
        w_v = wl[:, c_qk:c_v].astype(BF16)
        w_r = wl[:, c_v:c_r].astype(BF16)
        w_lr = _pad_lanes(wl[:, c_r:], LANES)
        w_g2 = jnp.zeros((LANES, 2 * GLA_KEY_WIDTH), F32)
        w_g2 = w_g2.at[:GLA_GATE_RANK, :GLA_KEY_WIDTH].set(w_g2_f[l])
        w_g2 = w_g2.at[GLA_GATE_RANK:2 * GLA_GATE_RANK, GLA_KEY_WIDTH:].set(w_g2_b[l])
        b_g = jnp.concatenate([b_g_f[l], b_g_b[l]])[None, :]
        na_qkv, gqk, gv, gr, gates = _in_proj(xf, norm_mix_g[l][None, :], w_na, w_qk, w_v, w_r, w_lr, w_g2, b_g)

        y_na = _na(na_qkv, _na_bias_table(rpb[l], rows), batch, rows)
        o_f, o_b = _gla(gqk, gv, gates, batch, seq)

        w_o = w_out[l].astype(BF16)
        w_rt = _pad_lanes(jnp.concatenate([w_exp[l], w_grp[l]], axis=1), LANES)
        w_rt_hi = w_rt.astype(BF16)
        w_rt_lo = (w_rt - w_rt_hi.astype(F32)).astype(BF16)
        w_rt3 = jnp.concatenate([w_rt_hi, w_rt_hi, w_rt_lo], axis=0).T
        b_rt = _pad_lanes(jnp.concatenate([b_exp[l], b_grp[l]])[None, :], ROUTE_ROWS).T
        x1, h2_slabs, rt_t, counts = _out_route(y_na, o_f, o_b, gr, xf, w_o[:NA_WIDTH], w_o[NA_WIDTH:],
                                                gla_norm_g[l][None, :], norm_ffn_g[l][None, :], w_rt3, b_rt)

        tile_start, tiles, n_tiles, pad_tile, offs_b, sorted_rows = _dispatch_plan(counts, n)
        pos, rt, xs = _dispatch(pad_tile, n_tiles, rt_t, offs_b, h2_slabs, sorted_rows)
        ys = _experts(tile_start, tiles, xs, w_gate, w_up, w_down, l)
        pos = pos.reshape(-1)
        last = l == depth - 1
        xf = _combine(pos, ys, x1, rt, final_norm_g[None, :], final_norm=last)
    return xf.reshape(batch, seq, d)
```

```python
import functools

import jax
import jax.numpy as jnp
from jax import lax
from jax.experimental import pallas as pl
from jax.experimental.pallas import tpu as pltpu

F32 = jnp.float32
BF16 = jnp.bfloat16

GRID_W = 64
NA_HEADS = 8
NA_HEAD_DIM = 64
NA_WIDTH = NA_HEADS * NA_HEAD_DIM
WIN_H_MAX = 8
WIN_W = 16
GLA_HEADS = 4
GLA_DK = 64
GLA_DV = 128
GLA_KEY_WIDTH = GLA_HEADS * GLA_DK
GLA_VAL_WIDTH = GLA_HEADS * GLA_DV
GLA_GATE_RANK = 16
GLA_GATE_NORMALIZER = 16.0
GLA_CHUNK = 64
N_GROUPS = 4
EXPERTS_PER_GROUP = 8
N_EXPERTS = N_GROUPS * EXPERTS_PER_GROUP
RMS_EPS = 1e-6

LANES = 128
SUBLANES = 8
MXU_WIDTH = 256
VMEM_LIMIT_BYTES = 56 * 1024 * 1024

MASK_VALUE = -1e30

IN_PROJ_ROWS = 512
NA_ROW_BLOCK = 8
NA_ROW_UNROLL = 4
GLA_STEP_CHUNKS = 4
OUT_ROWS = 512
EXPERT_ROWS = 256
DISPATCH_ROWS = 512
COMBINE_ROWS = DISPATCH_ROWS
GATHER_UNROLL = 64


def _params(*sem):
    return pltpu.CompilerParams(dimension_semantics=sem, vmem_limit_bytes=VMEM_LIMIT_BYTES)


def _rms(x, g):
    return x * lax.rsqrt(jnp.mean(x * x, axis=-1, keepdims=True) + RMS_EPS) * g


def _in_proj_kernel(x_ref, g_ref, w_na_ref, w_qk_ref, w_v_ref, w_r_ref, w_lr_ref,
                    w_g2_ref, b_g_ref, na_ref, qk_ref, v_ref, r_ref, gate_ref, w_gate):
    @pl.when(pl.program_id(0) == 0)
    def _():
        w_gate[...] = jnp.dot(w_lr_ref[...], w_g2_ref[...], preferred_element_type=F32,
                              precision=lax.Precision.HIGHEST).astype(BF16)

    h = _rms(x_ref[...], g_ref[...]).astype(BF16)
    na_ref[...] = jnp.dot(h, w_na_ref[...], preferred_element_type=F32).astype(BF16)
    qk_ref[...] = jnp.dot(h, w_qk_ref[...], preferred_element_type=F32)
    v_ref[...] = jnp.dot(h, w_v_ref[...], preferred_element_type=F32).astype(BF16)
    r_ref[...] = jnp.dot(h, w_r_ref[...], preferred_element_type=F32)
    z = jnp.dot(h, w_gate[...], preferred_element_type=F32) + b_g_ref[...]
    log_sig = jnp.minimum(z, 0.0) - jnp.log(1.0 + jnp.exp(-jnp.abs(z)))
    gate_ref[...] = log_sig * (1.0 / GLA_GATE_NORMALIZER)


def _in_proj(x, g, w_na, w_qk, w_v, w_r, w_lr, w_g2, b_g):
    n, d = x.shape
    rows = IN_PROJ_ROWS
    row_spec = lambda width: pl.BlockSpec((rows, width), lambda i: (i, 0))
    full = lambda a: pl.BlockSpec(a.shape, lambda i: (0,) * a.ndim)
    return pl.pallas_call(
        _in_proj_kernel,
        grid=(n // rows,),
        in_specs=[row_spec(d), full(g), full(w_na), full(w_qk), full(w_v), full(w_r), full(w_lr),
                  full(w_g2), full(b_g)],
        out_specs=[row_spec(3 * NA_WIDTH), row_spec(2 * GLA_KEY_WIDTH), row_spec(GLA_VAL_WIDTH),
                   row_spec(GLA_VAL_WIDTH), row_spec(2 * GLA_KEY_WIDTH)],
        out_shape=[jax.ShapeDtypeStruct((n, 3 * NA_WIDTH), BF16),
                   jax.ShapeDtypeStruct((n, 2 * GLA_KEY_WIDTH), F32),
                   jax.ShapeDtypeStruct((n, GLA_VAL_WIDTH), BF16),
                   jax.ShapeDtypeStruct((n, GLA_VAL_WIDTH), F32),
                   jax.ShapeDtypeStruct((n, 2 * GLA_KEY_WIDTH), F32)],
        scratch_shapes=[pltpu.VMEM((d, 2 * GLA_KEY_WIDTH), BF16)],
        compiler_params=_params("arbitrary"),
        name="in_proj",
    )(x, g, w_na, w_qk, w_v, w_r, w_lr, w_g2, b_g)


def _na_bias_table(rpb, rows):
    kh = min(WIN_H_MAX, rows)
    cls = jnp.arange(kh)[:, None, None]
    i = jnp.arange(kh)[None, :, None]
    w = jnp.arange(GRID_W)[:, None, None]
    x = jnp.arange(GRID_W)[None, :, None]
    cs = jnp.clip(w - WIN_W // 2, 0, GRID_W - WIN_W)
    valid = (x >= cs) & (x < cs + WIN_W)
    row_sel = (jnp.arange(2 * WIN_H_MAX - 1)[None, None, :] == i - cls + (WIN_H_MAX - 1)).astype(F32)
    col_sel = (valid & (jnp.arange(2 * WIN_W - 1)[None, None, :] == x - w + (WIN_W - 1))).astype(F32)
    hi = lax.Precision.HIGHEST
    by_col = jnp.einsum("hrc,wxc->hrwx", rpb.astype(F32), col_sel, precision=hi)
    bias = jnp.einsum("kir,hrwx->khwix", row_sel, by_col, precision=hi)
    bias = jnp.where(valid[None, None, :, None, :, 0], bias, MASK_VALUE)
    return bias.reshape(kh, NA_HEADS // 2, 2 * GRID_W, kh * GRID_W)


def _na_kernel(q_ref, kp_ref, kc_ref, kn_ref, vp_ref, vc_ref, vn_ref, tbl_ref, o_ref,
               kbuf, vbuf, *, rows, kh):
    rb = NA_ROW_BLOCK
    j = pl.program_id(1)
    blk = rb * GRID_W
    for s, (k_src, v_src) in enumerate(((kp_ref, vp_ref), (kc_ref, vc_ref), (kn_ref, vn_ref))):
        kbuf[s * blk:(s + 1) * blk, :] = k_src[...].reshape(blk, NA_WIDTH)
        vbuf[s * blk:(s + 1) * blk, :] = v_src[...].reshape(blk, NA_WIDTH)
    lane = lax.broadcasted_iota(jnp.int32, (GRID_W, LANES), 1)
    first = lane < NA_HEAD_DIM

    def row_body(lr, carry):
        r = j * rb + lr
        start = jnp.clip(r - kh // 2, 0, rows - kh)
        cls = r - start
        local = pl.multiple_of((start - (j - 1) * rb) * GRID_W, GRID_W)
        q_row = q_ref[lr]
        k_win = kbuf[pl.ds(local, kh * GRID_W), :]
        v_win = vbuf[pl.ds(local, kh * GRID_W), :]
        pairs = [slice(p * LANES, (p + 1) * LANES) for p in range(NA_HEADS // 2)]
        scores = []
        for sl in pairs:
            q_pair = q_row[:, sl]
            zero = jnp.zeros_like(q_pair)
            q_bd = jnp.concatenate([jnp.where(first, q_pair, zero), jnp.where(first, zero, q_pair)], axis=0)
            scores.append(lax.dot_general(q_bd, k_win[:, sl], (((1,), (1,)), ((), ())),
                                          preferred_element_type=F32))
        probs, denoms = [], []
        for p, s in enumerate(scores):
            s = s + tbl_ref[cls, p]
            e = jnp.exp(s - jnp.max(s, axis=-1, keepdims=True))
            denoms.append(jnp.sum(e, axis=-1, keepdims=True))
            probs.append(e.astype(BF16))
        outs = []
        for sl, e, denom in zip(pairs, probs, denoms):
            o = jnp.dot(e, v_win[:, sl], preferred_element_type=F32) / denom
            outs.append(jnp.where(first, o[:GRID_W], o[GRID_W:]))
        o_ref[lr] = jnp.concatenate(outs, axis=-1).astype(o_ref.dtype)
        return carry

    lax.fori_loop(0, rb, row_body, 0, unroll=NA_ROW_UNROLL)


def _na(na_qkv, tbl, batch, rows):
    kh = min(WIN_H_MAX, rows)
    rb = NA_ROW_BLOCK
    nblk = rows // rb
    x4 = na_qkv.reshape(batch, rows, GRID_W, 3 * NA_WIDTH)
    blk = (None, rb, GRID_W, NA_WIDTH)
    prev = lambda j: jnp.maximum(j - 1, 0)
    nxt = lambda j: jnp.minimum(j + 1, nblk - 1)
    specs = [pl.BlockSpec(blk, lambda b, j: (b, j, 0, 0))]
    for col in (1, 2):
        specs += [pl.BlockSpec(blk, lambda b, j, col=col: (b, prev(j), 0, col)),
                  pl.BlockSpec(blk, lambda b, j, col=col: (b, j, 0, col)),
                  pl.BlockSpec(blk, lambda b, j, col=col: (b, nxt(j), 0, col))]
    specs.append(pl.BlockSpec(tbl.shape, lambda b, j: (0, 0, 0, 0)))
    out = pl.pallas_call(
        functools.partial(_na_kernel, rows=rows, kh=kh),
        grid=(batch, nblk),
        in_specs=specs,
        out_specs=pl.BlockSpec(blk, lambda b, j: (b, j, 0, 0)),
        out_shape=jax.ShapeDtypeStruct((batch, rows, GRID_W, NA_WIDTH), BF16),
        scratch_shapes=[pltpu.VMEM((3 * rb * GRID_W, NA_WIDTH), BF16),
                        pltpu.VMEM((3 * rb * GRID_W, NA_WIDTH), BF16)],
        compiler_params=_params("arbitrary", "arbitrary"),
        name="na",
    )(x4, x4, x4, x4, x4, x4, x4, tbl)
    return out.reshape(batch * rows * GRID_W, NA_WIDTH)


def _block_diag_mask(row_block, col_block, nblocks):
    shape = (row_block * nblocks, col_block * nblocks)
    r = lax.shift_right_logical(lax.broadcasted_iota(jnp.int32, shape, 0), row_block.bit_length() - 1)
    c = lax.shift_right_logical(lax.broadcasted_iota(jnp.int32, shape, 1), col_block.bit_length() - 1)
    return r == c


def _split_bf16x3(x):
    hi = x.astype(BF16)
    rest = x - hi.astype(F32)
    mid = rest.astype(BF16)
    lo = (rest - mid.astype(F32)).astype(BF16)
    return hi, mid, lo


def _gla_direction(qk_ref, v_ref, g_ref, *, backward):
    c = GLA_CHUNK
    nc = GLA_STEP_CHUNKS
    kw = GLA_KEY_WIDTH
    order = list(reversed(range(nc))) if backward else list(range(nc))
    chunk = lambda a, n: a[n * c:(n + 1) * c]

    def stage_cumsum():
        step = nc * c
        ti = lax.broadcasted_iota(jnp.int32, (step, step), 0)
        tj = lax.broadcasted_iota(jnp.int32, (step, step), 1)
        same_chunk = lax.shift_right_logical(ti, c.bit_length() - 1) == lax.shift_right_logical(tj, c.bit_length() - 1)
        tri = (tj >= ti) if backward else (tj <= ti)
        cum = jnp.where(same_chunk & tri, 1.0, 0.0).astype(BF16)
        return sum(jnp.dot(cum, piece, preferred_element_type=F32) for piece in _split_bf16x3(g_ref[...]))

    def stage_decays(b):
        ref_row, last_row = (c // 2, 0) if backward else (c // 2 - 1, c - 1)
        rows_of = lambda r: jnp.concatenate(
            [jnp.broadcast_to(b[n * c + r:n * c + r + 1, :], (c, kw)) for n in range(nc)], axis=0)
        b_ref, b_last = rows_of(ref_row), rows_of(last_row)
        q = qk_ref[:, :kw]
        k = qk_ref[:, kw:]
        q_rel = (q * jnp.exp(b - b_ref)).astype(BF16)
        k_rel = (k * jnp.exp(b_ref - b)).astype(BF16)
        k_dec = k * jnp.exp(b_last - b)
        q_dec = (q * jnp.exp(b)).astype(BF16)
        decay = jnp.exp(b_last)
        return q_rel, k_rel, k_dec, q_dec, decay

    def stage_scores(q_rel, k_rel):
        kk_mask = _block_diag_mask(c, GLA_DK, GLA_HEADS)
        out = []
        for n in range(nc):
            k_bd = jnp.where(kk_mask, jnp.concatenate([chunk(k_rel, n)] * GLA_HEADS, axis=0), jnp.zeros((), BF16))
            out.append(lax.dot_general(chunk(q_rel, n), k_bd, (((1,), (1,)), ((), ())),
                                       preferred_element_type=F32))
        return out

    def stage_intra(scores, k_dec, decay):
        si = lax.broadcasted_iota(jnp.int32, (c, c * GLA_HEADS), 0)
        sj = lax.broadcasted_iota(jnp.int32, (c, c * GLA_HEADS), 1) & (c - 1)
        keep = (sj > si) if backward else (sj <= si)
        kv_mask = _block_diag_mask(c, GLA_DV, GLA_HEADS)
        vk_mask = _block_diag_mask(GLA_DK, GLA_DV, 2)
        o_intra, upd, decay_col = [], [], []
        for n in range(nc):
            v = v_ref[n * c:(n + 1) * c, :]
            p = jnp.where(keep, scores[n], 0.0).astype(BF16)
            v_bd = jnp.where(kv_mask, jnp.concatenate([v] * GLA_HEADS, axis=0), jnp.zeros((), BF16))
            o_intra.append(jnp.dot(p, v_bd, preferred_element_type=F32))
            k_dec_t = chunk(k_dec, n).T.astype(BF16)
            pairs = []
            for hp in range(GLA_HEADS // 2):
                kv = jnp.dot(k_dec_t[hp * 2 * GLA_DK:(hp + 1) * 2 * GLA_DK],
                             v[:, hp * 2 * GLA_DV:(hp + 1) * 2 * GLA_DV], preferred_element_type=F32)
                pairs.append(jnp.where(vk_mask, kv, 0.0))
            upd.append(pairs)
            decay_col.append(chunk(decay, n).T[:, :1])
        return o_intra, upd, decay_col

    def scan_step(state, idx, q_dec, o_intra, upd, decay_col, out_ref):
        n = order[idx]
        q_n = chunk(q_dec, n)
        pw = 2 * GLA_DK
        o_inter = jnp.concatenate(
            [jnp.dot(q_n[:, hp * pw:(hp + 1) * pw], s.astype(BF16), preferred_element_type=F32)
             for hp, s in enumerate(state)], axis=1)
        out_ref[n * c:(n + 1) * c, :] = o_intra[n] + o_inter
        return [s * decay_col[n][hp * pw:(hp + 1) * pw] + u for hp, (s, u) in enumerate(zip(state, upd[n]))]

    return stage_cumsum, stage_decays, stage_scores, stage_intra, scan_step


def _gla_kernel(qk_f_ref, v_f_ref, g_f_ref, qk_b_ref, v_b_ref, g_b_ref, o_f_ref, o_b_ref,
                st_f, st_b):
    @pl.when(pl.program_id(1) == 0)
    def _():
        st_f[...] = jnp.zeros_like(st_f)
        st_b[...] = jnp.zeros_like(st_b)

    dirs = (_gla_direction(qk_f_ref, v_f_ref, g_f_ref, backward=False),
            _gla_direction(qk_b_ref, v_b_ref, g_b_ref, backward=True))
    b = [d[0]() for d in dirs]
    dec = [d[1](x) for d, x in zip(dirs, b)]
    scores = [d[2](x[0], x[1]) for d, x in zip(dirs, dec)]
    intra = [d[3](s, x[2], x[4]) for d, s, x in zip(dirs, scores, dec)]
    pairs = range(GLA_HEADS // 2)
    states = [[st[hp] for hp in pairs] for st in (st_f, st_b)]
    for idx in range(GLA_STEP_CHUNKS):
        for j, (d, x, y, out_ref) in enumerate(zip(dirs, dec, intra, (o_f_ref, o_b_ref))):
            states[j] = d[4](states[j], idx, x[3], *y, out_ref)
    for st, state in zip((st_f, st_b), states):
        for hp in pairs:
            st[hp] = state[hp]


def _gla(qk, v, gates, batch, seq):
    step = GLA_STEP_CHUNKS * GLA_CHUNK
    nblk = seq // step
    qk3 = qk.reshape(batch, seq, 2 * GLA_KEY_WIDTH)
    v3 = v.reshape(batch, seq, GLA_VAL_WIDTH)
    g3 = gates.reshape(batch, seq, 2 * GLA_KEY_WIDTH)
    fwd = lambda b, n: (b, n, 0)
    bwd = lambda b, n: (b, nblk - 1 - n, 0)
    bwd_gate = lambda b, n: (b, nblk - 1 - n, 1)
    o_f, o_b = pl.pallas_call(
        _gla_kernel,
        grid=(batch, nblk),
        in_specs=[pl.BlockSpec((None, step, 2 * GLA_KEY_WIDTH), fwd),
                  pl.BlockSpec((None, step, GLA_VAL_WIDTH), fwd),
                  pl.BlockSpec((None, step, GLA_KEY_WIDTH), fwd),
                  pl.BlockSpec((None, step, 2 * GLA_KEY_WIDTH), bwd),
                  pl.BlockSpec((None, step, GLA_VAL_WIDTH), bwd),
                  pl.BlockSpec((None, step, GLA_KEY_WIDTH), bwd_gate)],
        out_specs=[pl.BlockSpec((None, step, GLA_VAL_WIDTH), fwd),
                   pl.BlockSpec((None, step, GLA_VAL_WIDTH), bwd)],
        out_shape=[jax.ShapeDtypeStruct((batch, seq, GLA_VAL_WIDTH), F32)] * 2,
        scratch_shapes=[pltpu.VMEM((GLA_HEADS // 2, 2 * GLA_DK, 2 * GLA_DV), F32),
                        pltpu.VMEM((GLA_HEADS // 2, 2 * GLA_DK, 2 * GLA_DV), F32)],
        compiler_params=_params("arbitrary", "arbitrary"),
        name="gla",
    )(qk3, v3, g3, qk3, v3, g3)
    return o_f.reshape(batch * seq, GLA_VAL_WIDTH), o_b.reshape(batch * seq, GLA_VAL_WIDTH)


def _store_slabs(ref, x):
    rows = x.shape[0]
    for s in range(SUBLANES):
        ref[pl.ds(s, rows, stride=SUBLANES), :] = x[:, s * LANES:(s + 1) * LANES]


def _load_slabs(ref, rows):
    return jnp.concatenate([ref[pl.ds(s, rows, stride=SUBLANES), :] for s in range(SUBLANES)], axis=1)


def _slab_copy(src_hbm, dst, sem, src_row, k):
    dst_row = k * SUBLANES if isinstance(k, int) else pl.multiple_of(k * SUBLANES, SUBLANES)
    return pltpu.make_async_copy(src_hbm.at[pl.ds(pl.multiple_of(src_row, SUBLANES), SUBLANES), :],
                                 dst.at[pl.ds(dst_row, SUBLANES), :], sem)


def _wait_slabs(src_hbm, dst, sem):
    pltpu.make_async_copy(src_hbm.at[pl.ds(0, dst.shape[0]), :], dst, sem).wait()


RT_E1, RT_E2, RT_RANK1, RT_RANK2, RT_W1, RT_W2 = range(6)
GROUP_ROW0 = N_EXPERTS
ROUTE_ROWS = 64


def _out_route_kernel(na_ref, of_ref, ob_ref, r_ref, x_ref, w_na_ref, w_gla_ref, gn_ref, fn_ref,
                      w_rt_ref, b_rt_ref, x1_ref, h2_ref, rt_ref, cnt_ref, carry):
    @pl.when(pl.program_id(0) == 0)
    def _():
        carry[...] = jnp.zeros_like(carry)

    rows = x_ref.shape[0]
    o = of_ref[...] + ob_ref[...]
    r = r_ref[...]
    parts = []
    for h in range(GLA_HEADS):
        sl = slice(h * GLA_DV, (h + 1) * GLA_DV)
        parts.append(_rms(o[:, sl], gn_ref[...]) * (r[:, sl] * jax.nn.sigmoid(r[:, sl])))
    y_gla = jnp.concatenate(parts, axis=-1).astype(BF16)
    x1 = (x_ref[...] + jnp.dot(na_ref[...], w_na_ref[...], preferred_element_type=F32)
          + jnp.dot(y_gla, w_gla_ref[...], preferred_element_type=F32))
    x1_ref[...] = x1
    h2 = _rms(x1, fn_ref[...])
    _store_slabs(h2_ref, h2)

    h_hi = h2.astype(BF16)
    h_lo = (h2 - h_hi.astype(F32)).astype(BF16)
    logits = lax.dot_general(w_rt_ref[...], jnp.concatenate([h_hi, h_lo, h_hi], axis=1),
                             (((1,), (1,)), ((), ())), preferred_element_type=F32)
    logits = logits[:ROUTE_ROWS] + b_rt_ref[...]
    row_i = lax.broadcasted_iota(jnp.int32, (ROUTE_ROWS, rows), 0)
    row = row_i.astype(F32)
    row_grp = lax.shift_right_logical(row_i, EXPERTS_PER_GROUP.bit_length() - 1).astype(F32)
    neg = jnp.float32(-jnp.inf)
    no_row = jnp.float32(ROUTE_ROWS)
    is_grp = (row_i >= GROUP_ROW0) & (row_i < GROUP_ROW0 + N_GROUPS)
    g_logit = jnp.where(is_grp, logits, neg)
    g_max = jnp.max(g_logit, axis=0, keepdims=True)
    g_sel = jnp.min(jnp.where(is_grp & (g_logit == g_max), row, no_row), axis=0, keepdims=True) - GROUP_ROW0
    grp_w = 1.0 / jnp.sum(jnp.where(is_grp, jnp.exp(g_logit - g_max), 0.0), axis=0, keepdims=True)
    in_grp = (row_i < N_EXPERTS) & (row_grp == g_sel)
    e_logit = jnp.where(in_grp, logits, neg)
    v1 = jnp.max(e_logit, axis=0, keepdims=True)
    i1 = jnp.min(jnp.where(in_grp & (e_logit == v1), row, no_row), axis=0, keepdims=True)
    rest = in_grp & (row != i1)
    e_logit2 = jnp.where(rest, logits, neg)
    v2 = jnp.max(e_logit2, axis=0, keepdims=True)
    i2 = jnp.min(jnp.where(rest & (e_logit2 == v2), row, no_row), axis=0, keepdims=True)
    t = jnp.exp(v2 - v1)
    w1 = grp_w / (1.0 + t)
    w2 = grp_w * t / (1.0 + t)

    sel1 = row == i1
    sel2 = row == i2
    onehot = jnp.where(sel1 | sel2, 1.0, 0.0)
    ti = lax.broadcasted_iota(jnp.int32, (rows, rows), 0)
    tj = lax.broadcasted_iota(jnp.int32, (rows, rows), 1)
    earlier = jnp.where(ti < tj, 1.0, 0.0).astype(BF16)
    ranks = jnp.dot(onehot.astype(BF16), earlier, preferred_element_type=F32) + carry[:, :1]
    rank1 = jnp.sum(jnp.where(sel1, ranks, 0.0), axis=0, keepdims=True)
    rank2 = jnp.sum(jnp.where(sel2, ranks, 0.0), axis=0, keepdims=True)
    new_carry = carry[...] + jnp.sum(onehot, axis=1, keepdims=True)
    carry[...] = new_carry
    cnt_ref[...] = new_carry

    fields = {RT_E1: i1, RT_E2: i2, RT_RANK1: rank1, RT_RANK2: rank2, RT_W1: w1, RT_W2: w2}
    zero = jnp.zeros_like(w1)
    rt_ref[...] = jnp.concatenate([fields.get(f, zero) for f in range(SUBLANES)], axis=0)


def _out_route(y_na, o_f, o_b, r, x, w_na, w_gla, gn, fn, w_rt, b_rt):
    n, d = x.shape
    rows = OUT_ROWS
    row_spec = lambda width: pl.BlockSpec((rows, width), lambda i: (i, 0))
    full = lambda a: pl.BlockSpec(a.shape, lambda i: (0,) * a.ndim)
    return pl.pallas_call(
        _out_route_kernel,
        grid=(n // rows,),
        in_specs=[row_spec(NA_WIDTH), row_spec(GLA_VAL_WIDTH), row_spec(GLA_VAL_WIDTH),
                  row_spec(GLA_VAL_WIDTH), row_spec(d), full(w_na), full(w_gla), full(gn), full(fn),
                  full(w_rt), full(b_rt)],
        out_specs=[row_spec(d), pl.BlockSpec((rows * SUBLANES, LANES), lambda i: (i, 0)),
                   pl.BlockSpec((SUBLANES, rows), lambda i: (0, i)),
                   pl.BlockSpec((ROUTE_ROWS, LANES), lambda i: (0, 0))],
        out_shape=[jax.ShapeDtypeStruct((n, d), F32), jax.ShapeDtypeStruct((n * SUBLANES, LANES), F32),
                   jax.ShapeDtypeStruct((SUBLANES, n), F32), jax.ShapeDtypeStruct((ROUTE_ROWS, LANES), F32)],
        scratch_shapes=[pltpu.VMEM((ROUTE_ROWS, LANES), F32)],
        compiler_params=_params("arbitrary"),
        name="out_route",
    )(y_na, o_f, o_b, r, x, w_na, w_gla, gn, fn, w_rt, b_rt)


def _dispatch_kernel(pad_tile_ref, n_tiles_ref, rt_ref, offs_ref, h_ref, pos_ref, rec_ref, xs_hbm,
                     zeros, pos_vmem, pos_smem, sem_zero, sem_pos, sem):
    i = pl.program_id(0)
    db = DISPATCH_ROWS
    tile = EXPERT_ROWS * SUBLANES
    max_tiles = xs_hbm.shape[0] // tile

    @pl.when(i == 0)
    def _():
        zeros[...] = jnp.zeros_like(zeros)

        def zero_copy(t):
            return pltpu.make_async_copy(zeros, xs_hbm.at[pl.ds(pl.multiple_of(t * tile, tile), tile), :], sem_zero)

        def for_each_zero_tile(fn):
            def padded(e, carry):
                @pl.when(pad_tile_ref[e] >= 0)
                def _():
                    fn(zero_copy(pad_tile_ref[e]))
                return carry
            lax.fori_loop(0, N_EXPERTS, padded, 0)

            def tail(t, carry):
                fn(zero_copy(t))
                return carry
            lax.fori_loop(n_tiles_ref[0], max_tiles, tail, 0)

        for_each_zero_tile(lambda cp: cp.start())
        for_each_zero_tile(lambda cp: cp.wait())

    rt_t = rt_ref[...]
    rec_ref[...] = jnp.concatenate([rt_t, jnp.zeros((LANES - SUBLANES, db), F32)], axis=0).T
    expert = lax.broadcasted_iota(jnp.int32, (N_EXPERTS, db), 0).astype(F32)
    pos = []
    for e_lane, rank_lane in ((RT_E1, RT_RANK1), (RT_E2, RT_RANK2)):
        seg = jnp.sum(jnp.where(expert == rt_t[e_lane:e_lane + 1, :], offs_ref[...], 0.0), axis=0, keepdims=True)
        pos.append((seg + rt_t[rank_lane:rank_lane + 1, :]) * SUBLANES)
    pos = jnp.concatenate(pos, axis=0).astype(jnp.int32)
    pos_ref[...] = pos
    pos_vmem[...] = jnp.concatenate([pos, jnp.zeros((SUBLANES - 2, db), jnp.int32)], axis=0)
    to_smem = pltpu.make_async_copy(pos_vmem, pos_smem, sem_pos)
    to_smem.start()
    to_smem.wait()

    for k in range(db):
        src = h_ref.at[pl.ds(k * SUBLANES, SUBLANES), :]
        for slot in range(2):
            dst_row = pl.multiple_of(pos_smem[slot, k], SUBLANES)
            pltpu.make_async_copy(src, xs_hbm.at[pl.ds(dst_row, SUBLANES), :], sem).start(priority=slot)
    for slot in range(2):
        pltpu.make_async_copy(h_ref, xs_hbm.at[pl.ds(0, db * SUBLANES), :], sem).wait()


def _dispatch(pad_tile, n_tiles, rt_t, offs_b, h2_slabs, sorted_rows):
    n = rt_t.shape[1]
    db = DISPATCH_ROWS
    grid_spec = pltpu.PrefetchScalarGridSpec(
        num_scalar_prefetch=2,
        grid=(n // db,),
        in_specs=[pl.BlockSpec((SUBLANES, db), lambda i, pt, nt: (0, i)),
                  pl.BlockSpec(offs_b.shape, lambda i, pt, nt: (0, 0)),
                  pl.BlockSpec((db * SUBLANES, LANES), lambda i, pt, nt: (i, 0))],
        out_specs=[pl.BlockSpec((None, 2, db), lambda i, pt, nt: (i, 0, 0)),
                   pl.BlockSpec((db, LANES), lambda i, pt, nt: (i, 0)),
                   pl.BlockSpec(memory_space=pl.ANY)],
        scratch_shapes=[pltpu.VMEM((EXPERT_ROWS * SUBLANES, LANES), F32),
                        pltpu.VMEM((SUBLANES, db), jnp.int32), pltpu.SMEM((SUBLANES, db), jnp.int32),
                        pltpu.SemaphoreType.DMA(()), pltpu.SemaphoreType.DMA(()), pltpu.SemaphoreType.DMA(())],
    )
    return pl.pallas_call(
        _dispatch_kernel,
        grid_spec=grid_spec,
        out_shape=[jax.ShapeDtypeStruct((n // db, 2, db), jnp.int32),
                   jax.ShapeDtypeStruct((n, LANES), F32),
                   jax.ShapeDtypeStruct((sorted_rows * SUBLANES, LANES), F32)],
        compiler_params=_params("arbitrary"),
        name="dispatch",
    )(pad_tile, n_tiles, rt_t, offs_b, h2_slabs)


TILE_DMA_PRIORITY = 1


def _expert_tile(x_ref, y_ref, wg_s, wu_s, wd_s):
    rows = EXPERT_ROWS
    x = _load_slabs(x_ref, rows).astype(BF16)
    hid = []
    for c in range(0, wg_s.shape[1], MXU_WIDTH):
        gate = jnp.dot(x, wg_s[:, c:c + MXU_WIDTH], preferred_element_type=F32)
        up = jnp.dot(x, wu_s[:, c:c + MXU_WIDTH], preferred_element_type=F32)
        hid.append((gate * jax.nn.sigmoid(gate) * up).astype(BF16))
    hid = jnp.concatenate(hid, axis=1)
    for c in range(0, wd_s.shape[1], MXU_WIDTH):
        y = jnp.dot(hid, wd_s[:, c:c + MXU_WIDTH], preferred_element_type=F32)
        for s in range(c // LANES, (c + MXU_WIDTH) // LANES):
            y_ref[pl.ds(s, rows, stride=SUBLANES), :] = y[:, s * LANES - c:(s + 1) * LANES - c]


def _experts_kernel(tile_start_ref, tiles_ref, x_hbm, wg_ref, wu_ref, wd_ref, y_hbm,
                    xbuf, ybuf, wg_s, wu_s, wd_s, sem_x, sem_y):
    e = pl.program_id(0)
    last_step = e == pl.num_programs(0) - 1
    tile = EXPERT_ROWS * SUBLANES
    n = tiles_ref[e]
    base = tile_start_ref[e]
    n_tiles = tile_start_ref[N_EXPERTS - 1] + tiles_ref[N_EXPERTS - 1]
    tile_rows = lambda g: pl.ds(pl.multiple_of(g * tile, tile), tile)

    def x_copy(g, slot):
        return pltpu.make_async_copy(x_hbm.at[tile_rows(g), :], xbuf.at[slot], sem_x.at[slot])

    def y_copy(g, slot):
        return pltpu.make_async_copy(ybuf.at[slot], y_hbm.at[tile_rows(g), :], sem_y.at[slot])

    @pl.when(e == 0)
    def _():
        x_copy(0, 0).start(priority=TILE_DMA_PRIORITY)

    @pl.when(n > 0)
    def _():
        wg_s[...] = wg_ref[...].astype(BF16)
        wu_s[...] = wu_ref[...].astype(BF16)
        wd_s[...] = wd_ref[...].astype(BF16)

        def tile_body(g, carry):
            slot = lax.rem(g, 2)
            x_copy(g, slot).wait()

            @pl.when(g + 1 < n_tiles)
            def _():
                x_copy(g + 1, 1 - slot).start(priority=TILE_DMA_PRIORITY)

            @pl.when(g >= 2)
            def _():
                y_copy(g - 2, slot).wait()

            _expert_tile(xbuf.at[slot], ybuf.at[slot], wg_s, wu_s, wd_s)
            y_copy(g, slot).start(priority=TILE_DMA_PRIORITY)
            return carry
        lax.fori_loop(base, base + n, tile_body, 0)

    @pl.when(last_step)
    def _():
        @pl.when(n_tiles >= 2)
        def _():
            y_copy(n_tiles - 2, lax.rem(n_tiles, 2)).wait()
        y_copy(n_tiles - 1, lax.rem(n_tiles - 1, 2)).wait()

        first_unused = n_tiles
        max_tiles = y_hbm.shape[0] // tile
        ybuf[0] = jnp.zeros(ybuf.shape[1:], ybuf.dtype)
        zero_copy = lambda t: pltpu.make_async_copy(
            ybuf.at[0], y_hbm.at[pl.ds(pl.multiple_of(t * tile, tile), tile), :], sem_y.at[0])

        def start(t, carry):
            zero_copy(t).start()
            return carry

        def wait(t, carry):
            zero_copy(t).wait()
            return carry
        lax.fori_loop(first_unused, max_tiles, start, 0)
        lax.fori_loop(first_unused, max_tiles, wait, 0)


def _experts(tile_start, tiles, xs, wg, wu, wd, layer):
    rows = EXPERT_ROWS
    _, n_experts, d, dff = wg.shape
    assert d == SUBLANES * LANES
    w_idx = lambda e, ts, nt: (layer, e, 0, 0)
    grid_spec = pltpu.PrefetchScalarGridSpec(
        num_scalar_prefetch=2,
        grid=(n_experts,),
        in_specs=[pl.BlockSpec(memory_space=pl.ANY),
                  pl.BlockSpec((None, None, d, dff), w_idx),
                  pl.BlockSpec((None, None, d, dff), w_idx),
                  pl.BlockSpec((None, None, dff, d), w_idx)],
        out_specs=pl.BlockSpec(memory_space=pl.ANY),
        scratch_shapes=[pltpu.VMEM((2, rows * SUBLANES, LANES), F32), pltpu.VMEM((2, rows * SUBLANES, LANES), F32),
                        pltpu.VMEM((d, dff), BF16), pltpu.VMEM((d, dff), BF16), pltpu.VMEM((dff, d), BF16),
                        pltpu.SemaphoreType.DMA((2,)), pltpu.SemaphoreType.DMA((2,))],
    )
    return pl.pallas_call(
        _experts_kernel,
        grid_spec=grid_spec,
        out_shape=jax.ShapeDtypeStruct(xs.shape, F32),
        compiler_params=_params("arbitrary"),
        name="experts",
    )(tile_start, tiles, xs, wg, wu, wd)


def _start_slab_gather(src_hbm, dst, sem, row_ref, base, count, priority):
    def body(k, carry):
        _slab_copy(src_hbm, dst, sem, row_ref[base + k], k).start(priority=priority)
        return carry
    lax.fori_loop(0, count, body, 0, unroll=GATHER_UNROLL)


def _combine_kernel(pos_ref, ys_hbm, x1_ref, rt_ref, g_ref, o_ref, buf, sem, *, final_norm):
    i = pl.program_id(0)
    rows = COMBINE_ROWS
    n_steps = pl.num_programs(0)
    slot = i % 2

    def start(step, s):
        for j in range(2):
            _start_slab_gather(ys_hbm, buf.at[s, j], sem.at[s], pos_ref, (step * 2 + j) * rows, rows, priority=j)

    @pl.when(i == 0)
    def _():
        start(0, 0)

    @pl.when(i + 1 < n_steps)
    def _():
        start(i + 1, 1 - slot)

    _wait_slabs(ys_hbm, buf.at[slot, 0], sem.at[slot])
    _wait_slabs(ys_hbm, buf.at[slot, 1], sem.at[slot])
    rt = rt_ref[...]
    lane = lax.broadcasted_iota(jnp.int32, rt.shape, 1)
    w1 = jnp.sum(jnp.where(lane == RT_W1, rt, 0.0), axis=-1, keepdims=True)
    w2 = jnp.sum(jnp.where(lane == RT_W2, rt, 0.0), axis=-1, keepdims=True)
    y = w1 * _load_slabs(buf.at[slot, 0], rows) + w2 * _load_slabs(buf.at[slot, 1], rows)
    x2 = x1_ref[...] + y
    if final_norm:
        x2 = _rms(x2, g_ref[...])
    o_ref[...] = x2


def _combine(pos, ys, x1, rt, g, final_norm):
    n, d = x1.shape
    rows = COMBINE_ROWS
    grid_spec = pltpu.PrefetchScalarGridSpec(
        num_scalar_prefetch=1,
        grid=(n // rows,),
        in_specs=[pl.BlockSpec(memory_space=pl.ANY),
                  pl.BlockSpec((rows, d), lambda i, pos: (i, 0)),
                  pl.BlockSpec((rows, LANES), lambda i, pos: (i, 0)),
                  pl.BlockSpec(g.shape, lambda i, pos: (0, 0))],
        out_specs=pl.BlockSpec((rows, d), lambda i, pos: (i, 0)),
        scratch_shapes=[pltpu.VMEM((2, 2, rows * SUBLANES, LANES), F32), pltpu.SemaphoreType.DMA((2,))],
    )
    return pl.pallas_call(
        functools.partial(_combine_kernel, final_norm=final_norm),
        grid_spec=grid_spec,
        out_shape=jax.ShapeDtypeStruct((n, d), F32),
        compiler_params=_params("arbitrary"),
        name="combine",
    )(pos, ys, x1, rt, g)


def _dispatch_plan(counts, n):
    rows = EXPERT_ROWS
    max_tiles = (2 * n) // rows + N_EXPERTS
    cnt = counts[:N_EXPERTS, 0].astype(jnp.int32)
    tiles = (cnt + rows - 1) // rows
    tile_end = jnp.cumsum(tiles)
    tile_start = tile_end - tiles
    n_tiles = tile_end[-1:]
    pad_tile = jnp.where(cnt % rows != 0, tile_end - 1, -1).astype(jnp.int32)
    offs_b = jnp.broadcast_to((tile_start * rows).astype(F32)[:, None], (N_EXPERTS, DISPATCH_ROWS))
    return tile_start.astype(jnp.int32), tiles, n_tiles, pad_tile, offs_b, max_tiles * rows


def _pad_lanes(a, width):
    return jnp.pad(a, ((0, 0), (0, width - a.shape[1])))


def kernel(x, norm_mix_g, w_in, w_g2_f, b_g_f, w_g2_b, b_g_b, gla_norm_g, rpb, w_out, norm_ffn_g, w_grp, b_grp,
           w_exp, b_exp, w_gate, w_up, w_down, final_norm_g):
    batch, seq, d = x.shape
    n = batch * seq
    rows = seq // GRID_W
    depth = w_in.shape[0]
    xf = x.reshape(n, d)
    c_na = 3 * NA_WIDTH
    c_qk = c_na + 2 * GLA_KEY_WIDTH
    c_v = c_qk + GLA_VAL_WIDTH
    c_r = c_v + GLA_VAL_WIDTH
    for l in range(depth):
        wl = w_in[l]
        q_scale = jnp.concatenate([jnp.full((NA_WIDTH,), NA_HEAD_DIM ** -0.5, F32),
                                   jnp.ones((2 * NA_WIDTH,), F32)])
        w_na = (wl[:, :c_na] * q_scale).astype(BF16)
        qk_scale = jnp.concatenate([jnp.full((GLA_KEY_WIDTH,), GLA_DK ** -0.5, F32),
                                    jnp.ones((GLA_KEY_WIDTH,), F32)])
        w_qk = (wl[:, c_na:c_qk] * qk_scale).astype(BF16)
        w_v = wl[:, c_qk:c_v].astype(BF16)
        w_r = wl[:, c_v:c_r].astype(BF16)
        w_lr = _pad_lanes(wl[:, c_r:], LANES)
        w_g2 = jnp.zeros((LANES, 2 * GLA_KEY_WIDTH), F32)
        w_g2 = w_g2.at[:GLA_GATE_RANK, :GLA_KEY_WIDTH].set(w_g2_f[l])
        w_g2 = w_g2.at[GLA_GATE_RANK:2 * GLA_GATE_RANK, GLA_KEY_WIDTH:].set(w_g2_b[l])
        b_g = jnp.concatenate([b_g_f[l], b_g_b[l]])[None, :]
        na_qkv, gqk, gv, gr, gates = _in_proj(xf, norm_mix_g[l][None, :], w_na, w_qk, w_v, w_r, w_lr, w_g2, b_g)

        y_na = _na(na_qkv, _na_bias_table(rpb[l], rows), batch, rows)
        o_f, o_b = _gla(gqk, gv, gates, batch, seq)

        w_o = w_out[l].astype(BF16)
        w_rt = _pad_lanes(jnp.concatenate([w_exp[l], w_grp[l]], axis=1), LANES)
        w_rt_hi = w_rt.astype(BF16)
        w_rt_lo = (w_rt - w_rt_hi.astype(F32)).astype(BF16)
        w_rt3 = jnp.concatenate([w_rt_hi, w_rt_hi, w_rt_lo], axis=0).T
        b_rt = _pad_lanes(jnp.concatenate([b_exp[l], b_grp[l]])[None, :], ROUTE_ROWS).T
        x1, h2_slabs, rt_t, counts = _out_route(y_na, o_f, o_b, gr, xf, w_o[:NA_WIDTH], w_o[NA_WIDTH:],
                                                gla_norm_g[l][None, :], norm_ffn_g[l][None, :], w_rt3, b_rt)

        tile_start, tiles, n_tiles, pad_tile, offs_b, sorted_rows = _dispatch_plan(counts, n)
        pos, rt, xs = _dispatch(pad_tile, n_tiles, rt_t, offs_b, h2_slabs, sorted_rows)
        ys = _experts(tile_start, tiles, xs, w_gate, w_up, w_down, l)
        pos = pos.reshape(-1)
        last = l == depth - 1
        xf = _combine(pos, ys, x1, rt, final_norm_g[None, :], final_norm=last)
    return xf.reshape(batch, seq, d)
```

```python
import functools

import jax
import jax.numpy as jnp
from jax import lax
from jax.experimental import pallas as pl
from jax.experimental.pallas import tpu as pltpu

F32 = jnp.float32
BF16 = jnp.bfloat16

GRID_W = 64
NA_HEADS = 8
NA_HEAD_DIM = 64
NA_WIDTH = NA_HEADS * NA_HEAD_DIM
WIN_H_MAX = 8
WIN_W = 16
GLA_HEADS = 4
GLA_DK = 64
GLA_DV = 128
GLA_KEY_WIDTH = GLA_HEADS * GLA_DK
GLA_VAL_WIDTH = GLA_HEADS * GLA_DV
GLA_GATE_RANK = 16
GLA_GATE_NORMALIZER = 16.0
GLA_CHUNK = 64
N_GROUPS = 4
EXPERTS_PER_GROUP = 8
N_EXPERTS = N_GROUPS * EXPERTS_PER_GROUP
RMS_EPS = 1e-6

LANES = 128
SUBLANES = 8
MXU_WIDTH = 256
VMEM_LIMIT_BYTES = 56 * 1024 * 1024

MASK_VALUE = -1e30

IN_PROJ_ROWS = 512
NA_ROW_BLOCK = 8
NA_ROW_UNROLL = 4
GLA_STEP_CHUNKS = 4
OUT_ROWS = 512
EXPERT_ROWS = 256
DISPATCH_ROWS = 512
COMBINE_ROWS = DISPATCH_ROWS
GATHER_UNROLL = 64


def _params(*sem):
    return pltpu.CompilerParams(dimension_semantics=sem, vmem_limit_bytes=VMEM_LIMIT_BYTES)


def _rms(x, g):
    return x * lax.rsqrt(jnp.mean(x * x, axis=-1, keepdims=True) + RMS_EPS) * g


def _in_proj_kernel(x_ref, g_ref, w_na_ref, w_qk_ref, w_v_ref, w_r_ref, w_lr_ref,
                    w_g2_ref, b_g_ref, na_ref, qk_ref, v_ref, r_ref, gate_ref, w_gate):
    @pl.when(pl.program_id(0) == 0)
    def _():
        w_gate[...] = jnp.dot(w_lr_ref[...], w_g2_ref[...], preferred_element_type=F32,
                              precision=lax.Precision.HIGHEST).astype(BF16)

    h = _rms(x_ref[...], g_ref[...]).astype(BF16)
    na_ref[...] = jnp.dot(h, w_na_ref[...], preferred_element_type=F32).astype(BF16)
    qk_ref[...] = jnp.dot(h, w_qk_ref[...], preferred_element_type=F32)
    v_ref[...] = jnp.dot(h, w_v_ref[...], preferred_element_type=F32).astype(BF16)
    r_ref[...] = jnp.dot(h, w_r_ref[...], preferred_element_type=F32)
    z = jnp.dot(h, w_gate[...], preferred_element_type=F32) + b_g_ref[...]
    log_sig = jnp.minimum(z, 0.0) - jnp.log(1.0 + jnp.exp(-jnp.abs(z)))
    gate_ref[...] = log_sig * (1.0 / GLA_GATE_NORMALIZER)


def _in_proj(x, g, w_na, w_qk, w_v, w_r, w_lr, w_g2, b_g):
    n, d = x.shape
    rows = IN_PROJ_ROWS
    row_spec = lambda width: pl.BlockSpec((rows, width), lambda i: (i, 0))
    full = lambda a: pl.BlockSpec(a.shape, lambda i: (0,) * a.ndim)
    return pl.pallas_call(
        _in_proj_kernel,
        grid=(n // rows,),
        in_specs=[row_spec(d), full(g), full(w_na), full(w_qk), full(w_v), full(w_r), full(w_lr),
                  full(w_g2), full(b_g)],
        out_specs=[row_spec(3 * NA_WIDTH), row_spec(2 * GLA_KEY_WIDTH), row_spec(GLA_VAL_WIDTH),
                   row_spec(GLA_VAL_WIDTH), row_spec(2 * GLA_KEY_WIDTH)],
        out_shape=[jax.ShapeDtypeStruct((n, 3 * NA_WIDTH), BF16),
                   jax.ShapeDtypeStruct((n, 2 * GLA_KEY_WIDTH), F32),
                   jax.ShapeDtypeStruct((n, GLA_VAL_WIDTH), BF16),
                   jax.ShapeDtypeStruct((n, GLA_VAL_WIDTH), F32),
                   jax.ShapeDtypeStruct((n, 2 * GLA_KEY_WIDTH), F32)],
        scratch_shapes=[pltpu.VMEM((d, 2 * GLA_KEY_WIDTH), BF16)],
        compiler_params=_params("arbitrary"),
        name="in_proj",
    )(x, g, w_na, w_qk, w_v, w_r, w_lr, w_g2, b_g)


def _na_bias_table(rpb, rows):
    kh = min(WIN_H_MAX, rows)
    cls = jnp.arange(kh)[:, None, None]
    i = jnp.arange(kh)[None, :, None]
    w = jnp.arange(GRID_W)[:, None, None]
    x = jnp.arange(GRID_W)[None, :, None]
    cs = jnp.clip(w - WIN_W // 2, 0, GRID_W - WIN_W)
    valid = (x >= cs) & (x < cs + WIN_W)
    row_sel = (jnp.arange(2 * WIN_H_MAX - 1)[None, None, :] == i - cls + (WIN_H_MAX - 1)).astype(F32)
    col_sel = (valid & (jnp.arange(2 * WIN_W - 1)[None, None, :] == x - w + (WIN_W - 1))).astype(F32)
    hi = lax.Precision.HIGHEST
    by_col = jnp.einsum("hrc,wxc->hrwx", rpb.astype(F32), col_sel, precision=hi)
    bias = jnp.einsum("kir,hrwx->khwix", row_sel, by_col, precision=hi)
    bias = jnp.where(valid[None, None, :, None, :, 0], bias, MASK_VALUE)
    return bias.reshape(kh, NA_HEADS // 2, 2 * GRID_W, kh * GRID_W)


def _na_kernel(q_ref, kp_ref, kc_ref, kn_ref, vp_ref, vc_ref, vn_ref, tbl_ref, o_ref,
               kbuf, vbuf, *, rows, kh):
    rb = NA_ROW_BLOCK
    j = pl.program_id(1)
    blk = rb * GRID_W
    for s, (k_src, v_src) in enumerate(((kp_ref, vp_ref), (kc_ref, vc_ref), (kn_ref, vn_ref))):
        kbuf[s * blk:(s + 1) * blk, :] = k_src[...].reshape(blk, NA_WIDTH)
        vbuf[s * blk:(s + 1) * blk, :] = v_src[...].reshape(blk, NA_WIDTH)
    lane = lax.broadcasted_iota(jnp.int32, (GRID_W, LANES), 1)
    first = lane < NA_HEAD_DIM

    def row_body(lr, carry):
        r = j * rb + lr
        start = jnp.clip(r - kh // 2, 0, rows - kh)
        cls = r - start
        local = pl.multiple_of((start - (j - 1) * rb) * GRID_W, GRID_W)
        q_row = q_ref[lr]
        k_win = kbuf[pl.ds(local, kh * GRID_W), :]
        v_win = vbuf[pl.ds(local, kh * GRID_W), :]
        pairs = [slice(p * LANES, (p + 1) * LANES) for p in range(NA_HEADS // 2)]
        scores = []
        for sl in pairs:
            q_pair = q_row[:, sl]
            zero = jnp.zeros_like(q_pair)
            q_bd = jnp.concatenate([jnp.where(first, q_pair, zero), jnp.where(first, zero, q_pair)], axis=0)
            scores.append(lax.dot_general(q_bd, k_win[:, sl], (((1,), (1,)), ((), ())),
                                          preferred_element_type=F32))
        probs, denoms = [], []
        for p, s in enumerate(scores):
            s = s + tbl_ref[cls, p]
            e = jnp.exp(s - jnp.max(s, axis=-1, keepdims=True))
            denoms.append(jnp.sum(e, axis=-1, keepdims=True))
            probs.append(e.astype(BF16))
        outs = []
        for sl, e, denom in zip(pairs, probs, denoms):
            o = jnp.dot(e, v_win[:, sl], preferred_element_type=F32) / denom
            outs.append(jnp.where(first, o[:GRID_W], o[GRID_W:]))
        o_ref[lr] = jnp.concatenate(outs, axis=-1).astype(o_ref.dtype)
        return carry

    lax.fori_loop(0, rb, row_body, 0, unroll=NA_ROW_UNROLL)


def _na(na_qkv, tbl, batch, rows):
    kh = min(WIN_H_MAX, rows)
    rb = NA_ROW_BLOCK
    nblk = rows // rb
    x4 = na_qkv.reshape(batch, rows, GRID_W, 3 * NA_WIDTH)
    blk = (None, rb, GRID_W, NA_WIDTH)
    prev = lambda j: jnp.maximum(j - 1, 0)
    nxt = lambda j: jnp.minimum(j + 1, nblk - 1)
    specs = [pl.BlockSpec(blk, lambda b, j: (b, j, 0, 0))]
    for col in (1, 2):
        specs += [pl.BlockSpec(blk, lambda b, j, col=col: (b, prev(j), 0, col)),
                  pl.BlockSpec(blk, lambda b, j, col=col: (b, j, 0, col)),
                  pl.BlockSpec(blk, lambda b, j, col=col: (b, nxt(j), 0, col))]
    specs.append(pl.BlockSpec(tbl.shape, lambda b, j: (0, 0, 0, 0)))
    out = pl.pallas_call(
        functools.partial(_na_kernel, rows=rows, kh=kh),
        grid=(batch, nblk),
        in_specs=specs,
        out_specs=pl.BlockSpec(blk, lambda b, j: (b, j, 0, 0)),
        out_shape=jax.ShapeDtypeStruct((batch, rows, GRID_W, NA_WIDTH), BF16),
        scratch_shapes=[pltpu.VMEM((3 * rb * GRID_W, NA_WIDTH), BF16),
                        pltpu.VMEM((3 * rb * GRID_W, NA_WIDTH), BF16)],
        compiler_params=_params("arbitrary", "arbitrary"),
        name="na",
    )(x4, x4, x4, x4, x4, x4, x4, tbl)
    return out.reshape(batch * rows * GRID_W, NA_WIDTH)


def _block_diag_mask(row_block, col_block, nblocks):
    shape = (row_block * nblocks, col_block * nblocks)
    r = lax.shift_right_logical(lax.broadcasted_iota(jnp.int32, shape, 0), row_block.bit_length() - 1)
    c = lax.shift_right_logical(lax.broadcasted_iota(jnp.int32, shape, 1), col_block.bit_length() - 1)
    return r == c


def _split_bf16x3(x):
    hi = x.astype(BF16)
    rest = x - hi.astype(F32)
    mid = rest.astype(BF16)
    lo = (rest - mid.astype(F32)).astype(BF16)
    return hi, mid, lo


def _gla_direction(qk_ref, v_ref, g_ref, *, backward):
    c = GLA_CHUNK
    nc = GLA_STEP_CHUNKS
    kw = GLA_KEY_WIDTH
    order = list(reversed(range(nc))) if backward else list(range(nc))
    chunk = lambda a, n: a[n * c:(n + 1) * c]

    def stage_cumsum():
        step = nc * c
        ti = lax.broadcasted_iota(jnp.int32, (step, step), 0)
        tj = lax.broadcasted_iota(jnp.int32, (step, step), 1)
        same_chunk = lax.shift_right_logical(ti, c.bit_length() - 1) == lax.shift_right_logical(tj, c.bit_length() - 1)
        tri = (tj >= ti) if backward else (tj <= ti)
        cum = jnp.where(same_chunk & tri, 1.0, 0.0).astype(BF16)
        return sum(jnp.dot(cum, piece, preferred_element_type=F32) for piece in _split_bf16x3(g_ref[...]))

    def stage_decays(b):
        ref_row, last_row = (c // 2, 0) if backward else (c // 2 - 1, c - 1)
        rows_of = lambda r: jnp.concatenate(
            [jnp.broadcast_to(b[n * c + r:n * c + r + 1, :], (c, kw)) for n in range(nc)], axis=0)
        b_ref, b_last = rows_of(ref_row), rows_of(last_row)
        q = qk_ref[:, :kw]
        k = qk_ref[:, kw:]
        q_rel = (q * jnp.exp(b - b_ref)).astype(BF16)
        k_rel = (k * jnp.exp(b_ref - b)).astype(BF16)
        k_dec = k * jnp.exp(b_last - b)
        q_dec = (q * jnp.exp(b)).astype(BF16)
        decay = jnp.exp(b_last)
        return q_rel, k_rel, k_dec, q_dec, decay

    def stage_scores(q_rel, k_rel):
        kk_mask = _block_diag_mask(c, GLA_DK, GLA_HEADS)
        out = []
        for n in range(nc):
            k_bd = jnp.where(kk_mask, jnp.concatenate([chunk(k_rel, n)] * GLA_HEADS, axis=0), jnp.zeros((), BF16))
            out.append(lax.dot_general(chunk(q_rel, n), k_bd, (((1,), (1,)), ((), ())),
                                       preferred_element_type=F32))
        return out

    def stage_intra(scores, k_dec, decay):
        si = lax.broadcasted_iota(jnp.int32, (c, c * GLA_HEADS), 0)
        sj = lax.broadcasted_iota(jnp.int32, (c, c * GLA_HEADS), 1) & (c - 1)
        keep = (sj > si) if backward else (sj <= si)
        kv_mask = _block_diag_mask(c, GLA_DV, GLA_HEADS)
        vk_mask = _block_diag_mask(GLA_DK, GLA_DV, 2)
        o_intra, upd, decay_col = [], [], []
        for n in range(nc):
            v = v_ref[n * c:(n + 1) * c, :]
            p = jnp.where(keep, scores[n], 0.0).astype(BF16)
            v_bd = jnp.where(kv_mask, jnp.concatenate([v] * GLA_HEADS, axis=0), jnp.zeros((), BF16))
            o_intra.append(jnp.dot(p, v_bd, preferred_element_type=F32))
            k_dec_t = chunk(k_dec, n).T.astype(BF16)
            pairs = []
            for hp in range(GLA_HEADS // 2):
                kv = jnp.dot(k_dec_t[hp * 2 * GLA_DK:(hp + 1) * 2 * GLA_DK],
                             v[:, hp * 2 * GLA_DV:(hp + 1) * 2 * GLA_DV], preferred_element_type=F32)
                pairs.append(jnp.where(vk_mask, kv, 0.0))
            upd.append(pairs)
            decay_col.append(chunk(decay, n).T[:, :1])
        return o_intra, upd, decay_col

    def scan_step(state, idx, q_dec, o_intra, upd, decay_col, out_ref):
        n = order[idx]
        q_n = chunk(q_dec, n)
        pw = 2 * GLA_DK
        o_inter = jnp.concatenate(
            [jnp.dot(q_n[:, hp * pw:(hp + 1) * pw], s.astype(BF16), preferred_element_type=F32)
             for hp, s in enumerate(state)], axis=1)
        out_ref[n * c:(n + 1) * c, :] = o_intra[n] + o_inter
        return [s * decay_col[n][hp * pw:(hp + 1) * pw] + u for hp, (s, u) in enumerate(zip(state, upd[n]))]

    return stage_cumsum, stage_decays, stage_scores, stage_intra, scan_step


def _gla_kernel(qk_f_ref, v_f_ref, g_f_ref, qk_b_ref, v_b_ref, g_b_ref, o_f_ref, o_b_ref,
                st_f, st_b):
    @pl.when(pl.program_id(1) == 0)
    def _():
        st_f[...] = jnp.zeros_like(st_f)
        st_b[...] = jnp.zeros_like(st_b)

    dirs = (_gla_direction(qk_f_ref, v_f_ref, g_f_ref, backward=False),
            _gla_direction(qk_b_ref, v_b_ref, g_b_ref, backward=True))
    b = [d[0]() for d in dirs]
    dec = [d[1](x) for d, x in zip(dirs, b)]
    scores = [d[2](x[0], x[1]) for d, x in zip(dirs, dec)]
    intra = [d[3](s, x[2], x[4]) for d, s, x in zip(dirs, scores, dec)]
    pairs = range(GLA_HEADS // 2)
    states = [[st[hp] for hp in pairs] for st in (st_f, st_b)]
    for idx in range(GLA_STEP_CHUNKS):
        for j, (d, x, y, out_ref) in enumerate(zip(dirs, dec, intra, (o_f_ref, o_b_ref))):
            states[j] = d[4](states[j], idx, x[3], *y, out_ref)
    for st, state in zip((st_f, st_b), states):
        for hp in pairs:
            st[hp] = state[hp]


def _gla(qk, v, gates, batch, seq):
    step = GLA_STEP_CHUNKS * GLA_CHUNK
    nblk = seq // step
    qk3 = qk.reshape(batch, seq, 2 * GLA_KEY_WIDTH)
    v3 = v.reshape(batch, seq, GLA_VAL_WIDTH)
    g3 = gates.reshape(batch, seq, 2 * GLA_KEY_WIDTH)
    fwd = lambda b, n: (b, n, 0)
    bwd = lambda b, n: (b, nblk - 1 - n, 0)
    bwd_gate = lambda b, n: (b, nblk - 1 - n, 1)
    o_f, o_b = pl.pallas_call(
        _gla_kernel,
        grid=(batch, nblk),
        in_specs=[pl.BlockSpec((None, step, 2 * GLA_KEY_WIDTH), fwd),
                  pl.BlockSpec((None, step, GLA_VAL_WIDTH), fwd),
                  pl.BlockSpec((None, step, GLA_KEY_WIDTH), fwd),
                  pl.BlockSpec((None, step, 2 * GLA_KEY_WIDTH), bwd),
                  pl.BlockSpec((None, step, GLA_VAL_WIDTH), bwd),
                  pl.BlockSpec((None, step, GLA_KEY_WIDTH), bwd_gate)],
        out_specs=[pl.BlockSpec((None, step, GLA_VAL_WIDTH), fwd),
                   pl.BlockSpec((None, step, GLA_VAL_WIDTH), bwd)],
        out_shape=[jax.ShapeDtypeStruct((batch, seq, GLA_VAL_WIDTH), F32)] * 2,
        scratch_shapes=[pltpu.VMEM((GLA_HEADS // 2, 2 * GLA_DK, 2 * GLA_DV), F32),
                        pltpu.VMEM((GLA_HEADS // 2, 2 * GLA_DK, 2 * GLA_DV), F32)],
        compiler_params=_params("arbitrary", "arbitrary"),
        name="gla",
    )(qk3, v3, g3, qk3, v3, g3)
    return o_f.reshape(batch * seq, GLA_VAL_WIDTH), o_b.reshape(batch * seq, GLA_VAL_WIDTH)


def _store_slabs(ref, x):
    rows = x.shape[0]
    for s in range(SUBLANES):
        ref[pl.ds(s, rows, stride=SUBLANES), :] = x[:, s * LANES:(s + 1) * LANES]


def _load_slabs(ref, rows):
    return jnp.concatenate([ref[pl.ds(s, rows, stride=SUBLANES), :] for s in range(SUBLANES)], axis=1)


def _slab_copy(src_hbm, dst, sem, src_row, k):
    dst_row = k * SUBLANES if isinstance(k, int) else pl.multiple_of(k * SUBLANES, SUBLANES)
    return pltpu.make_async_copy(src_hbm.at[pl.ds(pl.multiple_of(src_row, SUBLANES), SUBLANES), :],
                                 dst.at[pl.ds(dst_row, SUBLANES), :], sem)


def _wait_slabs(src_hbm, dst, sem):
    pltpu.make_async_copy(src_hbm.at[pl.ds(0, dst.shape[0]), :], dst, sem).wait()


RT_E1, RT_E2, RT_RANK1, RT_RANK2, RT_W1, RT_W2 = range(6)
GROUP_ROW0 = N_EXPERTS
ROUTE_ROWS = 64


def _out_route_kernel(na_ref, of_ref, ob_ref, r_ref, x_ref, w_na_ref, w_gla_ref, gn_ref, fn_ref,
                      w_rt_ref, b_rt_ref, x1_ref, h2_ref, rt_ref, cnt_ref, carry):
    @pl.when(pl.program_id(0) == 0)
    def _():
        carry[...] = jnp.zeros_like(carry)

    rows = x_ref.shape[0]
    o = of_ref[...] + ob_ref[...]
    r = r_ref[...]
    parts = []
    for h in range(GLA_HEADS):
        sl = slice(h * GLA_DV, (h + 1) * GLA_DV)
        parts.append(_rms(o[:, sl], gn_ref[...]) * (r[:, sl] * jax.nn.sigmoid(r[:, sl])))
    y_gla = jnp.concatenate(parts, axis=-1).astype(BF16)
    x1 = (x_ref[...] + jnp.dot(na_ref[...], w_na_ref[...], preferred_element_type=F32)
          + jnp.dot(y_gla, w_gla_ref[...], preferred_element_type=F32))
    x1_ref[...] = x1
    h2 = _rms(x1, fn_ref[...])
    _store_slabs(h2_ref, h2)

    h_hi = h2.astype(BF16)
    h_lo = (h2 - h_hi.astype(F32)).astype(BF16)
    logits = lax.dot_general(w_rt_ref[...], jnp.concatenate([h_hi, h_lo, h_hi], axis=1),
                             (((1,), (1,)), ((), ())), preferred_element_type=F32)
    logits = logits[:ROUTE_ROWS] + b_rt_ref[...]
    row_i = lax.broadcasted_iota(jnp.int32, (ROUTE_ROWS, rows), 0)
    row = row_i.astype(F32)
    row_grp = lax.shift_right_logical(row_i, EXPERTS_PER_GROUP.bit_length() - 1).astype(F32)
    neg = jnp.float32(-jnp.inf)
    no_row = jnp.float32(ROUTE_ROWS)
    is_grp = (row_i >= GROUP_ROW0) & (row_i < GROUP_ROW0 + N_GROUPS)
    g_logit = jnp.where(is_grp, logits, neg)
    g_max = jnp.max(g_logit, axis=0, keepdims=True)
    g_sel = jnp.min(jnp.where(is_grp & (g_logit == g_max), row, no_row), axis=0, keepdims=True) - GROUP_ROW0
    grp_w = 1.0 / jnp.sum(jnp.where(is_grp, jnp.exp(g_logit - g_max), 0.0), axis=0, keepdims=True)
    in_grp = (row_i < N_EXPERTS) & (row_grp == g_sel)
    e_logit = jnp.where(in_grp, logits, neg)
    v1 = jnp.max(e_logit, axis=0, keepdims=True)
    i1 = jnp.min(jnp.where(in_grp & (e_logit == v1), row, no_row), axis=0, keepdims=True)
    rest = in_grp & (row != i1)
    e_logit2 = jnp.where(rest, logits, neg)
    v2 = jnp.max(e_logit2, axis=0, keepdims=True)
    i2 = jnp.min(jnp.where(rest & (e_logit2 == v2), row, no_row), axis=0, keepdims=True)
    t = jnp.exp(v2 - v1)
    w1 = grp_w / (1.0 + t)
    w2 = grp_w * t / (1.0 + t)

    sel1 = row == i1
    sel2 = row == i2
    onehot = jnp.where(sel1 | sel2, 1.0, 0.0)
    ti = lax.broadcasted_iota(jnp.int32, (rows, rows), 0)
    tj = lax.broadcasted_iota(jnp.int32, (rows, rows), 1)
    earlier = jnp.where(ti < tj, 1.0, 0.0).astype(BF16)
    ranks = jnp.dot(onehot.astype(BF16), earlier, preferred_element_type=F32) + carry[:, :1]
    rank1 = jnp.sum(jnp.where(sel1, ranks, 0.0), axis=0, keepdims=True)
    rank2 = jnp.sum(jnp.where(sel2, ranks, 0.0), axis=0, keepdims=True)
    new_carry = carry[...] + jnp.sum(onehot, axis=1, keepdims=True)
    carry[...] = new_carry
    cnt_ref[...] = new_carry

    fields = {RT_E1: i1, RT_E2: i2, RT_RANK1: rank1, RT_RANK2: rank2, RT_W1: w1, RT_W2: w2}
    zero = jnp.zeros_like(w1)
    rt_ref[...] = jnp.concatenate([fields.get(f, zero) for f in range(SUBLANES)], axis=0)


def _out_route(y_na, o_f, o_b, r, x, w_na, w_gla, gn, fn, w_rt, b_rt):
    n, d = x.shape
    rows = OUT_ROWS
    row_spec = lambda width: pl.BlockSpec((rows, width), lambda i: (i, 0))
    full = lambda a: pl.BlockSpec(a.shape, lambda i: (0,) * a.ndim)
    return pl.pallas_call(
        _out_route_kernel,
        grid=(n // rows,),
        in_specs=[row_spec(NA_WIDTH), row_spec(GLA_VAL_WIDTH), row_spec(GLA_VAL_WIDTH),
                  row_spec(GLA_VAL_WIDTH), row_spec(d), full(w_na), full(w_gla), full(gn), full(fn),
                  full(w_rt), full(b_rt)],
        out_specs=[row_spec(d), pl.BlockSpec((rows * SUBLANES, LANES), lambda i: (i, 0)),
                   pl.BlockSpec((SUBLANES, rows), lambda i: (0, i)),
                   pl.BlockSpec((ROUTE_ROWS, LANES), lambda i: (0, 0))],
        out_shape=[jax.ShapeDtypeStruct((n, d), F32), jax.ShapeDtypeStruct((n * SUBLANES, LANES), F32),
                   jax.ShapeDtypeStruct((SUBLANES, n), F32), jax.ShapeDtypeStruct((ROUTE_ROWS, LANES), F32)],
        scratch_shapes=[pltpu.VMEM((ROUTE_ROWS, LANES), F32)],
        compiler_params=_params("arbitrary"),
        name="out_route",
    )(y_na, o_f, o_b, r, x, w_na, w_gla, gn, fn, w_rt, b_rt)


def _dispatch_kernel(pad_tile_ref, n_tiles_ref, rt_ref, offs_ref, h_ref, pos_ref, rec_ref, xs_hbm,
                     zeros, pos_vmem, pos_smem, sem_zero, sem_pos, sem):
    i = pl.program_id(0)
    db = DISPATCH_ROWS
    tile = EXPERT_ROWS * SUBLANES
    max_tiles = xs_hbm.shape[0] // tile

    @pl.when(i == 0)
    def _():
        zeros[...] = jnp.zeros_like(zeros)

        def zero_copy(t):
            return pltpu.make_async_copy(zeros, xs_hbm.at[pl.ds(pl.multiple_of(t * tile, tile), tile), :], sem_zero)

        def for_each_zero_tile(fn):
            def padded(e, carry):
                @pl.when(pad_tile_ref[e] >= 0)
                def _():
                    fn(zero_copy(pad_tile_ref[e]))
                return carry
            lax.fori_loop(0, N_EXPERTS, padded, 0)

            def tail(t, carry):
                fn(zero_copy(t))
                return carry
            lax.fori_loop(n_tiles_ref[0], max_tiles, tail, 0)

        for_each_zero_tile(lambda cp: cp.start())
        for_each_zero_tile(lambda cp: cp.wait())

    rt_t = rt_ref[...]
    rec_ref[...] = jnp.concatenate([rt_t, jnp.zeros((LANES - SUBLANES, db), F32)], axis=0).T
    expert = lax.broadcasted_iota(jnp.int32, (N_EXPERTS, db), 0).astype(F32)
    pos = []
    for e_lane, rank_lane in ((RT_E1, RT_RANK1), (RT_E2, RT_RANK2)):
        seg = jnp.sum(jnp.where(expert == rt_t[e_lane:e_lane + 1, :], offs_ref[...], 0.0), axis=0, keepdims=True)
        pos.append((seg + rt_t[rank_lane:rank_lane + 1, :]) * SUBLANES)
    pos = jnp.concatenate(pos, axis=0).astype(jnp.int32)
    pos_ref[...] = pos
    pos_vmem[...] = jnp.concatenate([pos, jnp.zeros((SUBLANES - 2, db), jnp.int32)], axis=0)
    to_smem = pltpu.make_async_copy(pos_vmem, pos_smem, sem_pos)
    to_smem.start()
    to_smem.wait()

    for k in range(db):
        src = h_ref.at[pl.ds(k * SUBLANES, SUBLANES), :]
        for slot in range(2):
            dst_row = pl.multiple_of(pos_smem[slot, k], SUBLANES)
            pltpu.make_async_copy(src, xs_hbm.at[pl.ds(dst_row, SUBLANES), :], sem).start(priority=slot)
    for slot in range(2):
        pltpu.make_async_copy(h_ref, xs_hbm.at[pl.ds(0, db * SUBLANES), :], sem).wait()


def _dispatch(pad_tile, n_tiles, rt_t, offs_b, h2_slabs, sorted_rows):
    n = rt_t.shape[1]
    db = DISPATCH_ROWS
    grid_spec = pltpu.PrefetchScalarGridSpec(
        num_scalar_prefetch=2,
        grid=(n // db,),
        in_specs=[pl.BlockSpec((SUBLANES, db), lambda i, pt, nt: (0, i)),
                  pl.BlockSpec(offs_b.shape, lambda i, pt, nt: (0, 0)),
                  pl.BlockSpec((db * SUBLANES, LANES), lambda i, pt, nt: (i, 0))],
        out_specs=[pl.BlockSpec((None, 2, db), lambda i, pt, nt: (i, 0, 0)),
                   pl.BlockSpec((db, LANES), lambda i, pt, nt: (i, 0)),
                   pl.BlockSpec(memory_space=pl.ANY)],
        scratch_shapes=[pltpu.VMEM((EXPERT_ROWS * SUBLANES, LANES), F32),
                        pltpu.VMEM((SUBLANES, db), jnp.int32), pltpu.SMEM((SUBLANES, db), jnp.int32),
                        pltpu.SemaphoreType.DMA(()), pltpu.SemaphoreType.DMA(()), pltpu.SemaphoreType.DMA(())],
    )
    return pl.pallas_call(
        _dispatch_kernel,
        grid_spec=grid_spec,
        out_shape=[jax.ShapeDtypeStruct((n // db, 2, db), jnp.int32),
                   jax.ShapeDtypeStruct((n, LANES), F32),
                   jax.ShapeDtypeStruct((sorted_rows * SUBLANES, LANES), F32)],
        compiler_params=_params("arbitrary"),
        name="dispatch",
    )(pad_tile, n_tiles, rt_t, offs_b, h2_slabs)


TILE_DMA_PRIORITY = 1


def _expert_tile(x_ref, y_ref, wg_s, wu_s, wd_s):
    rows = EXPERT_ROWS
    x = _load_slabs(x_ref, rows).astype(BF16)
    hid = []
    for c in range(0, wg_s.shape[1], MXU_WIDTH):
        gate = jnp.dot(x, wg_s[:, c:c + MXU_WIDTH], preferred_element_type=F32)
        up = jnp.dot(x, wu_s[:, c:c + MXU_WIDTH], preferred_element_type=F32)
        hid.append((gate * jax.nn.sigmoid(gate) * up).astype(BF16))
    hid = jnp.concatenate(hid, axis=1)
    for c in range(0, wd_s.shape[1], MXU_WIDTH):
        y = jnp.dot(hid, wd_s[:, c:c + MXU_WIDTH], preferred_element_type=F32)
        for s in range(c // LANES, (c + MXU_WIDTH) // LANES):
            y_ref[pl.ds(s, rows, stride=SUBLANES), :] = y[:, s * LANES - c:(s + 1) * LANES - c]


def _experts_kernel(tile_start_ref, tiles_ref, x_hbm, wg_hbm, wu_hbm, wd_hbm, y_hbm,
                    xbuf, ybuf, wg_f, wu_f, wd_f, wg_s, wu_s, wd_s, sem_x, sem_y, sem_w, *, layer):
    e = pl.program_id(0)
    last_step = e == pl.num_programs(0) - 1
    tile = EXPERT_ROWS * SUBLANES
    n = tiles_ref[e]
    base = tile_start_ref[e]
    n_tiles = tile_start_ref[N_EXPERTS - 1] + tiles_ref[N_EXPERTS - 1]
    tile_rows = lambda g: pl.ds(pl.multiple_of(g * tile, tile), tile)

    def x_copy(g, slot):
        return pltpu.make_async_copy(x_hbm.at[tile_rows(g), :], xbuf.at[slot], sem_x.at[slot])

    def y_copy(g, slot):
        return pltpu.make_async_copy(ybuf.at[slot], y_hbm.at[tile_rows(g), :], sem_y.at[slot])

    def w_copies(expert, slot):
        return [pltpu.make_async_copy(w_hbm.at[layer, expert], stage.at[slot], sem_w.at[slot])
                for w_hbm, stage in ((wg_hbm, wg_f), (wu_hbm, wu_f), (wd_hbm, wd_f))]

    w_slot = lax.rem(e, 2)

    @pl.when(e == 0)
    def _():
        x_copy(0, 0).start(priority=TILE_DMA_PRIORITY)
        for cp in w_copies(0, 0):
            cp.start()

    for cp in w_copies(e, w_slot):
        cp.wait()

    @pl.when(jnp.logical_not(last_step))
    def _():
        for cp in w_copies(e + 1, 1 - w_slot):
            cp.start()

    @pl.when(n > 0)
    def _():
        wg_s[...] = wg_f[w_slot].astype(BF16)
        wu_s[...] = wu_f[w_slot].astype(BF16)
        wd_s[...] = wd_f[w_slot].astype(BF16)

        def tile_body(g, carry):
            slot = lax.rem(g, 2)
            x_copy(g, slot).wait()

            @pl.when(g + 1 < n_tiles)
            def _():
                x_copy(g + 1, 1 - slot).start(priority=TILE_DMA_PRIORITY)

            @pl.when(g >= 2)
            def _():
                y_copy(g - 2, slot).wait()

            _expert_tile(xbuf.at[slot], ybuf.at[slot], wg_s, wu_s, wd_s)
            y_copy(g, slot).start(priority=TILE_DMA_PRIORITY)
            return carry
        lax.fori_loop(base, base + n, tile_body, 0)

    @pl.when(last_step)
    def _():
        @pl.when(n_tiles >= 2)
        def _():
            y_copy(n_tiles - 2, lax.rem(n_tiles, 2)).wait()
        y_copy(n_tiles - 1, lax.rem(n_tiles - 1, 2)).wait()

        first_unused = n_tiles
        max_tiles = y_hbm.shape[0] // tile
        ybuf[0] = jnp.zeros(ybuf.shape[1:], ybuf.dtype)
        zero_copy = lambda t: pltpu.make_async_copy(
            ybuf.at[0], y_hbm.at[pl.ds(pl.multiple_of(t * tile, tile), tile), :], sem_y.at[0])

        def start(t, carry):
            zero_copy(t).start()
            return carry

        def wait(t, carry):
            zero_copy(t).wait()
            return carry
        lax.fori_loop(first_unused, max_tiles, start, 0)
        lax.fori_loop(first_unused, max_tiles, wait, 0)


def _experts(tile_start, tiles, xs, wg, wu, wd, layer):
    rows = EXPERT_ROWS
    _, n_experts, d, dff = wg.shape
    assert d == SUBLANES * LANES and n_experts == N_EXPERTS
    any_spec = pl.BlockSpec(memory_space=pl.ANY)
    grid_spec = pltpu.PrefetchScalarGridSpec(
        num_scalar_prefetch=2,
        grid=(n_experts,),
        in_specs=[any_spec, any_spec, any_spec, any_spec],
        out_specs=any_spec,
        scratch_shapes=[pltpu.VMEM((2, rows * SUBLANES, LANES), F32), pltpu.VMEM((2, rows * SUBLANES, LANES), F32),
                        pltpu.VMEM((2, d, dff), F32), pltpu.VMEM((2, d, dff), F32), pltpu.VMEM((2, dff, d), F32),
                        pltpu.VMEM((d, dff), BF16), pltpu.VMEM((d, dff), BF16), pltpu.VMEM((dff, d), BF16),
                        pltpu.SemaphoreType.DMA((2,)), pltpu.SemaphoreType.DMA((2,)),
                        pltpu.SemaphoreType.DMA((2,))],
    )
    return pl.pallas_call(
        functools.partial(_experts_kernel, layer=layer),
        grid_spec=grid_spec,
        out_shape=jax.ShapeDtypeStruct(xs.shape, F32),
        compiler_params=_params("arbitrary"),
        name="experts",
    )(tile_start, tiles, xs, wg, wu, wd)


def _start_slab_gather(src_hbm, dst, sem, row_ref, base, count, priority):
    def body(k, carry):
        _slab_copy(src_hbm, dst, sem, row_ref[base + k], k).start(priority=priority)
        return carry
    lax.fori_loop(0, count, body, 0, unroll=GATHER_UNROLL)


def _combine_kernel(pos_ref, ys_hbm, x1_ref, rt_ref, g_ref, o_ref, buf, sem, *, final_norm):
    i = pl.program_id(0)
    rows = COMBINE_ROWS
    n_steps = pl.num_programs(0)
    slot = i % 2

    def start(step, s):
        for j in range(2):
            _start_slab_gather(ys_hbm, buf.at[s, j], sem.at[s], pos_ref, (step * 2 + j) * rows, rows, priority=j)

    @pl.when(i == 0)
    def _():
        start(0, 0)

    @pl.when(i + 1 < n_steps)
    def _():
        start(i + 1, 1 - slot)

    _wait_slabs(ys_hbm, buf.at[slot, 0], sem.at[slot])
    _wait_slabs(ys_hbm, buf.at[slot, 1], sem.at[slot])
    rt = rt_ref[...]
    lane = lax.broadcasted_iota(jnp.int32, rt.shape, 1)
    w1 = jnp.sum(jnp.where(lane == RT_W1, rt, 0.0), axis=-1, keepdims=True)
    w2 = jnp.sum(jnp.where(lane == RT_W2, rt, 0.0), axis=-1, keepdims=True)
    y = w1 * _load_slabs(buf.at[slot, 0], rows) + w2 * _load_slabs(buf.at[slot, 1], rows)
    x2 = x1_ref[...] + y
    if final_norm:
        x2 = _rms(x2, g_ref[...])
    o_ref[...] = x2


def _combine(pos, ys, x1, rt, g, final_norm):
    n, d = x1.shape
    rows = COMBINE_ROWS
    grid_spec = pltpu.PrefetchScalarGridSpec(
        num_scalar_prefetch=1,
        grid=(n // rows,),
        in_specs=[pl.BlockSpec(memory_space=pl.ANY),
                  pl.BlockSpec((rows, d), lambda i, pos: (i, 0)),
                  pl.BlockSpec((rows, LANES), lambda i, pos: (i, 0)),
                  pl.BlockSpec(g.shape, lambda i, pos: (0, 0))],
        out_specs=pl.BlockSpec((rows, d), lambda i, pos: (i, 0)),
        scratch_shapes=[pltpu.VMEM((2, 2, rows * SUBLANES, LANES), F32), pltpu.SemaphoreType.DMA((2,))],
    )
    return pl.pallas_call(
        functools.partial(_combine_kernel, final_norm=final_norm),
        grid_spec=grid_spec,
        out_shape=jax.ShapeDtypeStruct((n, d), F32),
        compiler_params=_params("arbitrary"),
        name="combine",
    )(pos, ys, x1, rt, g)


def _dispatch_plan(counts, n):
    rows = EXPERT_ROWS
    max_tiles = (2 * n) // rows + N_EXPERTS
    cnt = counts[:N_EXPERTS, 0].astype(jnp.int32)
    tiles = (cnt + rows - 1) // rows
    tile_end = jnp.cumsum(tiles)
    tile_start = tile_end - tiles
    n_tiles = tile_end[-1:]
    pad_tile = jnp.where(cnt % rows != 0, tile_end - 1, -1).astype(jnp.int32)
    offs_b = jnp.broadcast_to((tile_start * rows).astype(F32)[:, None], (N_EXPERTS, DISPATCH_ROWS))
    return tile_start.astype(jnp.int32), tiles, n_tiles, pad_tile, offs_b, max_tiles * rows


def _pad_lanes(a, width):
    return jnp.pad(a, ((0, 0), (0, width - a.shape[1])))


def kernel(x, norm_mix_g, w_in, w_g2_f, b_g_f, w_g2_b, b_g_b, gla_norm_g, rpb, w_out, norm_ffn_g, w_grp, b_grp,
           w_exp, b_exp, w_gate, w_up, w_down, final_norm_g):
    batch, seq, d = x.shape
    n = batch * seq
    rows = seq // GRID_W
    depth = w_in.shape[0]
    xf = x.reshape(n, d)
    c_na = 3 * NA_WIDTH
    c_qk = c_na + 2 * GLA_KEY_WIDTH
    c_v = c_qk + GLA_VAL_WIDTH
    c_r = c_v + GLA_VAL_WIDTH
    for l in range(depth):
        wl = w_in[l]
        q_scale = jnp.concatenate([jnp.full((NA_WIDTH,), NA_HEAD_DIM ** -0.5, F32),
                                   jnp.ones((2 * NA_WIDTH,), F32)])
        w_na = (wl[:, :c_na] * q_scale).astype(BF16)
        qk_scale = jnp.concatenate([jnp.full((GLA_KEY_WIDTH,), GLA_DK ** -0.5, F32),
                                    jnp.ones((GLA_KEY_WIDTH,), F32)])
        w_qk = (wl[:, c_na:c_qk] * qk_scale).astype(BF16)
        w_v = wl[:, c_qk:c_v].astype(BF16)
        w_r = wl[:, c_v:c_r].astype(BF16)
        w_lr = _pad_lanes(wl[:, c_r:], LANES)
        w_g2 = jnp.zeros((LANES, 2 * GLA_KEY_WIDTH), F32)
        w_g2 = w_g2.at[:GLA_GATE_RANK, :GLA_KEY_WIDTH].set(w_g2_f[l])
        w_g2 = w_g2.at[GLA_GATE_RANK:2 * GLA_GATE_RANK, GLA_KEY_WIDTH:].set(w_g2_b[l])
        b_g = jnp.concatenate([b_g_f[l], b_g_b[l]])[None, :]
        na_qkv, gqk, gv, gr, gates = _in_proj(xf, norm_mix_g[l][None, :], w_na, w_qk, w_v, w_r, w_lr, w_g2, b_g)

        y_na = _na(na_qkv, _na_bias_table(rpb[l], rows), batch, rows)
        o_f, o_b = _gla(gqk, gv, gates, batch, seq)

        w_o = w_out[l].astype(BF16)
        w_rt = _pad_lanes(jnp.concatenate([w_exp[l], w_grp[l]], axis=1), LANES)
        w_rt_hi = w_rt.astype(BF16)
        w_rt_lo = (w_rt - w_rt_hi.astype(F32)).astype(BF16)
        w_rt3 = jnp.concatenate([w_rt_hi, w_rt_hi, w_rt_lo], axis=0).T
        b_rt = _pad_lanes(jnp.concatenate([b_exp[l], b_grp[l]])[None, :], ROUTE_ROWS).T
        x1, h2_slabs, rt_t, counts = _out_route(y_na, o_f, o_b, gr, xf, w_o[:NA_WIDTH], w_o[NA_WIDTH:],
                                                gla_norm_g[l][None, :], norm_ffn_g[l][None, :], w_rt3, b_rt)

        tile_start, tiles, n_tiles, pad_tile, offs_b, sorted_rows = _dispatch_plan(counts, n)
        pos, rt, xs = _dispatch(pad_tile, n_tiles, rt_t, offs_b, h2_slabs, sorted_rows)
        ys = _experts(tile_start, tiles, xs, w_gate, w_up, w_down, l)
        pos = pos.reshape(-1)
        last = l == depth - 1
        xf = _combine(pos, ys, x1, rt, final_norm_g[None, :], final_norm=last)
    return xf.reshape(batch, seq, d)
```

```python
import functools

import jax
import jax.numpy as jnp
from jax import lax
from jax.experimental import pallas as pl
from jax.experimental.pallas import tpu as pltpu

F32 = jnp.float32
BF16 = jnp.bfloat16

GRID_W = 64
NA_HEADS = 8
NA_HEAD_DIM = 64
NA_WIDTH = NA_HEADS * NA_HEAD_DIM
WIN_H_MAX = 8
WIN_W = 16
GLA_HEADS = 4
GLA_DK = 64
GLA_DV = 128
GLA_KEY_WIDTH = GLA_HEADS * GLA_DK
GLA_VAL_WIDTH = GLA_HEADS * GLA_DV
GLA_GATE_RANK = 16
GLA_GATE_NORMALIZER = 16.0
GLA_CHUNK = 64
N_GROUPS = 4
EXPERTS_PER_GROUP = 8
N_EXPERTS = N_GROUPS * EXPERTS_PER_GROUP
RMS_EPS = 1e-6

LANES = 128
SUBLANES = 8
MXU_WIDTH = 256
VMEM_LIMIT_BYTES = 56 * 1024 * 1024

MASK_VALUE = -1e30

IN_PROJ_ROWS = 512
NA_ROW_BLOCK = 8
NA_ROW_UNROLL = 4
GLA_STEP_CHUNKS = 4
OUT_ROWS = 512
EXPERT_ROWS = 512
DISPATCH_ROWS = 512
COMBINE_ROWS = DISPATCH_ROWS
GATHER_UNROLL = 64


def _params(*sem):
    return pltpu.CompilerParams(dimension_semantics=sem, vmem_limit_bytes=VMEM_LIMIT_BYTES)


def _rms(x, g):
    return x * lax.rsqrt(jnp.mean(x * x, axis=-1, keepdims=True) + RMS_EPS) * g


def _in_proj_kernel(x_ref, g_ref, w_na_ref, w_qk_ref, w_v_ref, w_r_ref, w_lr_ref,
                    w_g2_ref, b_g_ref, na_ref, qk_ref, v_ref, r_ref, gate_ref, w_gate):
    @pl.when(pl.program_id(0) == 0)
    def _():
        w_gate[...] = jnp.dot(w_lr_ref[...], w_g2_ref[...], preferred_element_type=F32,
                              precision=lax.Precision.HIGHEST).astype(BF16)

    h = _rms(x_ref[...], g_ref[...]).astype(BF16)
    na_ref[...] = jnp.dot(h, w_na_ref[...], preferred_element_type=F32).astype(BF16)
    qk_ref[...] = jnp.dot(h, w_qk_ref[...], preferred_element_type=F32)
    v_ref[...] = jnp.dot(h, w_v_ref[...], preferred_element_type=F32).astype(BF16)
    r_ref[...] = jnp.dot(h, w_r_ref[...], preferred_element_type=F32)
    z = jnp.dot(h, w_gate[...], preferred_element_type=F32) + b_g_ref[...]
    log_sig = jnp.minimum(z, 0.0) - jnp.log(1.0 + jnp.exp(-jnp.abs(z)))
    gate_ref[...] = log_sig * (1.0 / GLA_GATE_NORMALIZER)


def _in_proj(x, g, w_na, w_qk, w_v, w_r, w_lr, w_g2, b_g):
    n, d = x.shape
    rows = IN_PROJ_ROWS
    row_spec = lambda width: pl.BlockSpec((rows, width), lambda i: (i, 0))
    full = lambda a: pl.BlockSpec(a.shape, lambda i: (0,) * a.ndim)
    return pl.pallas_call(
        _in_proj_kernel,
        grid=(n // rows,),
        in_specs=[row_spec(d), full(g), full(w_na), full(w_qk), full(w_v), full(w_r), full(w_lr),
                  full(w_g2), full(b_g)],
        out_specs=[row_spec(3 * NA_WIDTH), row_spec(2 * GLA_KEY_WIDTH), row_spec(GLA_VAL_WIDTH),
                   row_spec(GLA_VAL_WIDTH), row_spec(2 * GLA_KEY_WIDTH)],
        out_shape=[jax.ShapeDtypeStruct((n, 3 * NA_WIDTH), BF16),
                   jax.ShapeDtypeStruct((n, 2 * GLA_KEY_WIDTH), F32),
                   jax.ShapeDtypeStruct((n, GLA_VAL_WIDTH), BF16),
                   jax.ShapeDtypeStruct((n, GLA_VAL_WIDTH), F32),
                   jax.ShapeDtypeStruct((n, 2 * GLA_KEY_WIDTH), F32)],
        scratch_shapes=[pltpu.VMEM((d, 2 * GLA_KEY_WIDTH), BF16)],
        compiler_params=_params("arbitrary"),
        name="in_proj",
    )(x, g, w_na, w_qk, w_v, w_r, w_lr, w_g2, b_g)


def _na_bias_table(rpb, rows):
    kh = min(WIN_H_MAX, rows)
    w = jnp.arange(GRID_W)[:, None, None]
    x = jnp.arange(GRID_W)[None, :, None]
    cs = jnp.clip(w - WIN_W // 2, 0, GRID_W - WIN_W)
    valid = (x >= cs) & (x < cs + WIN_W)
    col_sel = (valid & (jnp.arange(2 * WIN_W - 1)[None, None, :] == x - w + (WIN_W - 1))).astype(F32)
    by_col = jnp.einsum("hrc,wxc->hwrx", rpb.astype(F32), col_sel, precision=lax.Precision.HIGHEST)
    by_col = jnp.where(valid[None, :, None, :, 0], by_col, MASK_VALUE)
    by_col = by_col.reshape(NA_HEADS * GRID_W, (2 * WIN_H_MAX - 1) * GRID_W)
    bias = jnp.stack([by_col[:, (WIN_H_MAX - 1 - c) * GRID_W:(WIN_H_MAX - 1 - c + kh) * GRID_W]
                      for c in range(kh)])
    return bias.reshape(kh, NA_HEADS // 2, 2 * GRID_W, kh * GRID_W)


def _na_kernel(q_ref, kp_ref, kc_ref, kn_ref, vp_ref, vc_ref, vn_ref, tbl_ref, o_ref,
               kbuf, vbuf, *, rows, kh):
    rb = NA_ROW_BLOCK
    j = pl.program_id(1)
    blk = rb * GRID_W
    for s, (k_src, v_src) in enumerate(((kp_ref, vp_ref), (kc_ref, vc_ref), (kn_ref, vn_ref))):
        kbuf[s * blk:(s + 1) * blk, :] = k_src[...].reshape(blk, NA_WIDTH)
        vbuf[s * blk:(s + 1) * blk, :] = v_src[...].reshape(blk, NA_WIDTH)
    lane = lax.broadcasted_iota(jnp.int32, (GRID_W, LANES), 1)
    first = lane < NA_HEAD_DIM

    def row_body(lr, carry):
        r = j * rb + lr
        start = jnp.clip(r - kh // 2, 0, rows - kh)
        cls = r - start
        local = pl.multiple_of((start - (j - 1) * rb) * GRID_W, GRID_W)
        q_row = q_ref[lr]
        k_win = kbuf[pl.ds(local, kh * GRID_W), :]
        v_win = vbuf[pl.ds(local, kh * GRID_W), :]
        pairs = [slice(p * LANES, (p + 1) * LANES) for p in range(NA_HEADS // 2)]
        scores = []
        for sl in pairs:
            q_pair = q_row[:, sl]
            zero = jnp.zeros_like(q_pair)
            q_bd = jnp.concatenate([jnp.where(first, q_pair, zero), jnp.where(first, zero, q_pair)], axis=0)
            scores.append(lax.dot_general(q_bd, k_win[:, sl], (((1,), (1,)), ((), ())),
                                          preferred_element_type=F32))
        probs, denoms = [], []
        for p, s in enumerate(scores):
            s = s + tbl_ref[cls, p]
            e = jnp.exp(s - jnp.max(s, axis=-1, keepdims=True))
            denoms.append(jnp.sum(e, axis=-1, keepdims=True))
            probs.append(e.astype(BF16))
        outs = []
        for sl, e, denom in zip(pairs, probs, denoms):
            o = jnp.dot(e, v_win[:, sl], preferred_element_type=F32) / denom
            outs.append(jnp.where(first, o[:GRID_W], o[GRID_W:]))
        o_ref[lr] = jnp.concatenate(outs, axis=-1).astype(o_ref.dtype)
        return carry

    lax.fori_loop(0, rb, row_body, 0, unroll=NA_ROW_UNROLL)


def _na(na_qkv, tbl, batch, rows):
    kh = min(WIN_H_MAX, rows)
    rb = NA_ROW_BLOCK
    nblk = rows // rb
    x4 = na_qkv.reshape(batch, rows, GRID_W, 3 * NA_WIDTH)
    blk = (None, rb, GRID_W, NA_WIDTH)
    prev = lambda j: jnp.maximum(j - 1, 0)
    nxt = lambda j: jnp.minimum(j + 1, nblk - 1)
    specs = [pl.BlockSpec(blk, lambda b, j: (b, j, 0, 0))]
    for col in (1, 2):
        specs += [pl.BlockSpec(blk, lambda b, j, col=col: (b, prev(j), 0, col)),
                  pl.BlockSpec(blk, lambda b, j, col=col: (b, j, 0, col)),
                  pl.BlockSpec(blk, lambda b, j, col=col: (b, nxt(j), 0, col))]
    specs.append(pl.BlockSpec(tbl.shape, lambda b, j: (0, 0, 0, 0)))
    out = pl.pallas_call(
        functools.partial(_na_kernel, rows=rows, kh=kh),
        grid=(batch, nblk),
        in_specs=specs,
        out_specs=pl.BlockSpec(blk, lambda b, j: (b, j, 0, 0)),
        out_shape=jax.ShapeDtypeStruct((batch, rows, GRID_W, NA_WIDTH), BF16),
        scratch_shapes=[pltpu.VMEM((3 * rb * GRID_W, NA_WIDTH), BF16),
                        pltpu.VMEM((3 * rb * GRID_W, NA_WIDTH), BF16)],
        compiler_params=_params("arbitrary", "arbitrary"),
        name="na",
    )(x4, x4, x4, x4, x4, x4, x4, tbl)
    return out.reshape(batch * rows * GRID_W, NA_WIDTH)


def _block_diag_mask(row_block, col_block, nblocks):
    shape = (row_block * nblocks, col_block * nblocks)
    r = lax.shift_right_logical(lax.broadcasted_iota(jnp.int32, shape, 0), row_block.bit_length() - 1)
    c = lax.shift_right_logical(lax.broadcasted_iota(jnp.int32, shape, 1), col_block.bit_length() - 1)
    return r == c


def _split_bf16x3(x):
    hi = x.astype(BF16)
    rest = x - hi.astype(F32)
    mid = rest.astype(BF16)
    lo = (rest - mid.astype(F32)).astype(BF16)
    return hi, mid, lo


def _gla_direction(qk_ref, v_ref, g_ref, *, backward):
    c = GLA_CHUNK
    nc = GLA_STEP_CHUNKS
    kw = GLA_KEY_WIDTH
    order = list(reversed(range(nc))) if backward else list(range(nc))
    chunk = lambda a, n: a[n * c:(n + 1) * c]

    def stage_cumsum():
        step = nc * c
        ti = lax.broadcasted_iota(jnp.int32, (step, step), 0)
        tj = lax.broadcasted_iota(jnp.int32, (step, step), 1)
        same_chunk = lax.shift_right_logical(ti, c.bit_length() - 1) == lax.shift_right_logical(tj, c.bit_length() - 1)
        tri = (tj >= ti) if backward else (tj <= ti)
        cum = jnp.where(same_chunk & tri, 1.0, 0.0).astype(BF16)
        return sum(jnp.dot(cum, piece, preferred_element_type=F32) for piece in _split_bf16x3(g_ref[...]))

    def stage_decays(b):
        ref_row, last_row = (c // 2, 0) if backward else (c // 2 - 1, c - 1)
        rows_of = lambda r: jnp.concatenate(
            [jnp.broadcast_to(b[n * c + r:n * c + r + 1, :], (c, kw)) for n in range(nc)], axis=0)
        b_ref, b_last = rows_of(ref_row), rows_of(last_row)
        q = qk_ref[:, :kw]
        k = qk_ref[:, kw:]
        q_rel = (q * jnp.exp(b - b_ref)).astype(BF16)
        k_rel = (k * jnp.exp(b_ref - b)).astype(BF16)
        k_dec = k * jnp.exp(b_last - b)
        q_dec = (q * jnp.exp(b)).astype(BF16)
        decay = jnp.exp(b_last)
        return q_rel, k_rel, k_dec, q_dec, decay

    def stage_scores(q_rel, k_rel):
        kk_mask = _block_diag_mask(c, GLA_DK, GLA_HEADS)
        out = []
        for n in range(nc):
            k_bd = jnp.where(kk_mask, jnp.concatenate([chunk(k_rel, n)] * GLA_HEADS, axis=0), jnp.zeros((), BF16))
            out.append(lax.dot_general(chunk(q_rel, n), k_bd, (((1,), (1,)), ((), ())),
                                       preferred_element_type=F32))
        return out

    def stage_intra(scores, k_dec, decay):
        si = lax.broadcasted_iota(jnp.int32, (c, c * GLA_HEADS), 0)
        sj = lax.broadcasted_iota(jnp.int32, (c, c * GLA_HEADS), 1) & (c - 1)
        keep = (sj > si) if backward else (sj <= si)
        kv_mask = _block_diag_mask(c, GLA_DV, GLA_HEADS)
        vk_mask = _block_diag_mask(GLA_DK, GLA_DV, 2)
        o_intra, upd, decay_col = [], [], []
        for n in range(nc):
            v = v_ref[n * c:(n + 1) * c, :]
            p = jnp.where(keep, scores[n], 0.0).astype(BF16)
            v_bd = jnp.where(kv_mask, jnp.concatenate([v] * GLA_HEADS, axis=0), jnp.zeros((), BF16))
            o_intra.append(jnp.dot(p, v_bd, preferred_element_type=F32))
            k_dec_t = chunk(k_dec, n).T.astype(BF16)
            pairs = []
            for hp in range(GLA_HEADS // 2):
                kv = jnp.dot(k_dec_t[hp * 2 * GLA_DK:(hp + 1) * 2 * GLA_DK],
                             v[:, hp * 2 * GLA_DV:(hp + 1) * 2 * GLA_DV], preferred_element_type=F32)
                pairs.append(jnp.where(vk_mask, kv, 0.0))
            upd.append(pairs)
            decay_col.append(chunk(decay, n).T[:, :1])
        return o_intra, upd, decay_col

    def scan_step(state, idx, q_dec, o_intra, upd, decay_col, out_ref):
        n = order[idx]
        q_n = chunk(q_dec, n)
        pw = 2 * GLA_DK
        o_inter = jnp.concatenate(
            [jnp.dot(q_n[:, hp * pw:(hp + 1) * pw], s.astype(BF16), preferred_element_type=F32)
             for hp, s in enumerate(state)], axis=1)
        out_ref[n * c:(n + 1) * c, :] = o_intra[n] + o_inter
        return [s * decay_col[n][hp * pw:(hp + 1) * pw] + u for hp, (s, u) in enumerate(zip(state, upd[n]))]

    return stage_cumsum, stage_decays, stage_scores, stage_intra, scan_step


def _gla_kernel(qk_f_ref, v_f_ref, g_f_ref, qk_b_ref, v_b_ref, g_b_ref, o_f_ref, o_b_ref,
                st_f, st_b):
    @pl.when(pl.program_id(1) == 0)
    def _():
        st_f[...] = jnp.zeros_like(st_f)
        st_b[...] = jnp.zeros_like(st_b)

    dirs = (_gla_direction(qk_f_ref, v_f_ref, g_f_ref, backward=False),
            _gla_direction(qk_b_ref, v_b_ref, g_b_ref, backward=True))
    b = [d[0]() for d in dirs]
    dec = [d[1](x) for d, x in zip(dirs, b)]
    scores = [d[2](x[0], x[1]) for d, x in zip(dirs, dec)]
    intra = [d[3](s, x[2], x[4]) for d, s, x in zip(dirs, scores, dec)]
    pairs = range(GLA_HEADS // 2)
    states = [[st[hp] for hp in pairs] for st in (st_f, st_b)]
    for idx in range(GLA_STEP_CHUNKS):
        for j, (d, x, y, out_ref) in enumerate(zip(dirs, dec, intra, (o_f_ref, o_b_ref))):
            states[j] = d[4](states[j], idx, x[3], *y, out_ref)
    for st, state in zip((st_f, st_b), states):
        for hp in pairs:
            st[hp] = state[hp]


def _gla(qk, v, gates, batch, seq):
    step = GLA_STEP_CHUNKS * GLA_CHUNK
    nblk = seq // step
    qk3 = qk.reshape(batch, seq, 2 * GLA_KEY_WIDTH)
    v3 = v.reshape(batch, seq, GLA_VAL_WIDTH)
    g3 = gates.reshape(batch, seq, 2 * GLA_KEY_WIDTH)
    fwd = lambda b, n: (b, n, 0)
    bwd = lambda b, n: (b, nblk - 1 - n, 0)
    bwd_gate = lambda b, n: (b, nblk - 1 - n, 1)
    o_f, o_b = pl.pallas_call(
        _gla_kernel,
        grid=(batch, nblk),
        in_specs=[pl.BlockSpec((None, step, 2 * GLA_KEY_WIDTH), fwd),
                  pl.BlockSpec((None, step, GLA_VAL_WIDTH), fwd),
                  pl.BlockSpec((None, step, GLA_KEY_WIDTH), fwd),
                  pl.BlockSpec((None, step, 2 * GLA_KEY_WIDTH), bwd),
                  pl.BlockSpec((None, step, GLA_VAL_WIDTH), bwd),
                  pl.BlockSpec((None, step, GLA_KEY_WIDTH), bwd_gate)],
        out_specs=[pl.BlockSpec((None, step, GLA_VAL_WIDTH), fwd),
                   pl.BlockSpec((None, step, GLA_VAL_WIDTH), bwd)],
        out_shape=[jax.ShapeDtypeStruct((batch, seq, GLA_VAL_WIDTH), F32)] * 2,
        scratch_shapes=[pltpu.VMEM((GLA_HEADS // 2, 2 * GLA_DK, 2 * GLA_DV), F32),
                        pltpu.VMEM((GLA_HEADS // 2, 2 * GLA_DK, 2 * GLA_DV), F32)],
        compiler_params=_params("arbitrary", "arbitrary"),
        name="gla",
    )(qk3, v3, g3, qk3, v3, g3)
    return o_f.reshape(batch * seq, GLA_VAL_WIDTH), o_b.reshape(batch * seq, GLA_VAL_WIDTH)


def _store_slabs(ref, x):
    rows = x.shape[0]
    for s in range(SUBLANES):
        ref[pl.ds(s, rows, stride=SUBLANES), :] = x[:, s * LANES:(s + 1) * LANES]


def _load_slabs(ref, rows):
    return jnp.concatenate([ref[pl.ds(s, rows, stride=SUBLANES), :] for s in range(SUBLANES)], axis=1)


def _slab_copy(src_hbm, dst, sem, src_row, k):
    dst_row = k * SUBLANES if isinstance(k, int) else pl.multiple_of(k * SUBLANES, SUBLANES)
    return pltpu.make_async_copy(src_hbm.at[pl.ds(pl.multiple_of(src_row, SUBLANES), SUBLANES), :],
                                 dst.at[pl.ds(dst_row, SUBLANES), :], sem)


def _wait_slabs(src_hbm, dst, sem):
    pltpu.make_async_copy(src_hbm.at[pl.ds(0, dst.shape[0]), :], dst, sem).wait()


RT_E1, RT_E2, RT_RANK1, RT_RANK2, RT_W1, RT_W2 = range(6)
GROUP_ROW0 = N_EXPERTS
ROUTE_ROWS = 64


def _out_route_kernel(na_ref, of_ref, ob_ref, r_ref, x_ref, w_na_ref, w_gla_ref, gn_ref, fn_ref,
                      w_rt_ref, b_rt_ref, x1_ref, h2_ref, rt_ref, cnt_ref, carry):
    @pl.when(pl.program_id(0) == 0)
    def _():
        carry[...] = jnp.zeros_like(carry)

    rows = x_ref.shape[0]
    o = of_ref[...] + ob_ref[...]
    r = r_ref[...]
    parts = []
    for h in range(GLA_HEADS):
        sl = slice(h * GLA_DV, (h + 1) * GLA_DV)
        parts.append(_rms(o[:, sl], gn_ref[...]) * (r[:, sl] * jax.nn.sigmoid(r[:, sl])))
    y_gla = jnp.concatenate(parts, axis=-1).astype(BF16)
    x1 = (x_ref[...] + jnp.dot(na_ref[...], w_na_ref[...], preferred_element_type=F32)
          + jnp.dot(y_gla, w_gla_ref[...], preferred_element_type=F32))
    x1_ref[...] = x1
    h2 = _rms(x1, fn_ref[...])
    _store_slabs(h2_ref, h2)

    h_hi = h2.astype(BF16)
    h_lo = (h2 - h_hi.astype(F32)).astype(BF16)
    logits = lax.dot_general(w_rt_ref[...], jnp.concatenate([h_hi, h_lo, h_hi], axis=1),
                             (((1,), (1,)), ((), ())), preferred_element_type=F32)
    logits = logits[:ROUTE_ROWS] + b_rt_ref[...]
    row_i = lax.broadcasted_iota(jnp.int32, (ROUTE_ROWS, rows), 0)
    row = row_i.astype(F32)
    row_grp = lax.shift_right_logical(row_i, EXPERTS_PER_GROUP.bit_length() - 1).astype(F32)
    neg = jnp.float32(-jnp.inf)
    no_row = jnp.float32(ROUTE_ROWS)
    is_grp = (row_i >= GROUP_ROW0) & (row_i < GROUP_ROW0 + N_GROUPS)
    g_logit = jnp.where(is_grp, logits, neg)
    g_max = jnp.max(g_logit, axis=0, keepdims=True)
    g_sel = jnp.min(jnp.where(is_grp & (g_logit == g_max), row, no_row), axis=0, keepdims=True) - GROUP_ROW0
    grp_w = 1.0 / jnp.sum(jnp.where(is_grp, jnp.exp(g_logit - g_max), 0.0), axis=0, keepdims=True)
    in_grp = (row_i < N_EXPERTS) & (row_grp == g_sel)
    e_logit = jnp.where(in_grp, logits, neg)
    v1 = jnp.max(e_logit, axis=0, keepdims=True)
    i1 = jnp.min(jnp.where(in_grp & (e_logit == v1), row, no_row), axis=0, keepdims=True)
    rest = in_grp & (row != i1)
    e_logit2 = jnp.where(rest, logits, neg)
    v2 = jnp.max(e_logit2, axis=0, keepdims=True)
    i2 = jnp.min(jnp.where(rest & (e_logit2 == v2), row, no_row), axis=0, keepdims=True)
    t = jnp.exp(v2 - v1)
    w1 = grp_w / (1.0 + t)
    w2 = grp_w * t / (1.0 + t)

    sel1 = row == i1
    sel2 = row == i2
    onehot = jnp.where(sel1 | sel2, 1.0, 0.0)
    ti = lax.broadcasted_iota(jnp.int32, (rows, rows), 0)
    tj = lax.broadcasted_iota(jnp.int32, (rows, rows), 1)
    earlier = jnp.where(ti < tj, 1.0, 0.0).astype(BF16)
    ranks = jnp.dot(onehot.astype(BF16), earlier, preferred_element_type=F32) + carry[:, :1]
    rank1 = jnp.sum(jnp.where(sel1, ranks, 0.0), axis=0, keepdims=True)
    rank2 = jnp.sum(jnp.where(sel2, ranks, 0.0), axis=0, keepdims=True)
    new_carry = carry[...] + jnp.sum(onehot, axis=1, keepdims=True)
    carry[...] = new_carry
    cnt_ref[...] = new_carry

    fields = {RT_E1: i1, RT_E2: i2, RT_RANK1: rank1, RT_RANK2: rank2, RT_W1: w1, RT_W2: w2}
    zero = jnp.zeros_like(w1)
    rt_ref[...] = jnp.concatenate([fields.get(f, zero) for f in range(SUBLANES)], axis=0)


def _out_route(y_na, o_f, o_b, r, x, w_na, w_gla, gn, fn, w_rt, b_rt):
    n, d = x.shape
    rows = OUT_ROWS
    row_spec = lambda width: pl.BlockSpec((rows, width), lambda i: (i, 0))
    full = lambda a: pl.BlockSpec(a.shape, lambda i: (0,) * a.ndim)
    return pl.pallas_call(
        _out_route_kernel,
        grid=(n // rows,),
        in_specs=[row_spec(NA_WIDTH), row_spec(GLA_VAL_WIDTH), row_spec(GLA_VAL_WIDTH),
                  row_spec(GLA_VAL_WIDTH), row_spec(d), full(w_na), full(w_gla), full(gn), full(fn),
                  full(w_rt), full(b_rt)],
        out_specs=[row_spec(d), pl.BlockSpec((rows * SUBLANES, LANES), lambda i: (i, 0)),
                   pl.BlockSpec((SUBLANES, rows), lambda i: (0, i)),
                   pl.BlockSpec((ROUTE_ROWS, LANES), lambda i: (0, 0))],
        out_shape=[jax.ShapeDtypeStruct((n, d), F32), jax.ShapeDtypeStruct((n * SUBLANES, LANES), F32),
                   jax.ShapeDtypeStruct((SUBLANES, n), F32), jax.ShapeDtypeStruct((ROUTE_ROWS, LANES), F32)],
        scratch_shapes=[pltpu.VMEM((ROUTE_ROWS, LANES), F32)],
        compiler_params=_params("arbitrary"),
        name="out_route",
    )(y_na, o_f, o_b, r, x, w_na, w_gla, gn, fn, w_rt, b_rt)


def _dispatch_kernel(pad_tile_ref, n_tiles_ref, rt_ref, offs_ref, h_ref, pos_ref, rec_ref, xs_hbm,
                     zeros, pos_vmem, pos_smem, sem_zero, sem_pos, sem):
    i = pl.program_id(0)
    db = DISPATCH_ROWS
    tile = EXPERT_ROWS * SUBLANES
    max_tiles = xs_hbm.shape[0] // tile

    @pl.when(i == 0)
    def _():
        zeros[...] = jnp.zeros_like(zeros)

        def zero_copy(t):
            return pltpu.make_async_copy(zeros, xs_hbm.at[pl.ds(pl.multiple_of(t * tile, tile), tile), :], sem_zero)

        def for_each_zero_tile(fn):
            def padded(e, carry):
                @pl.when(pad_tile_ref[e] >= 0)
                def _():
                    fn(zero_copy(pad_tile_ref[e]))
                return carry
            lax.fori_loop(0, N_EXPERTS, padded, 0)

            def tail(t, carry):
                fn(zero_copy(t))
                return carry
            lax.fori_loop(n_tiles_ref[0], max_tiles, tail, 0)

        for_each_zero_tile(lambda cp: cp.start())
        for_each_zero_tile(lambda cp: cp.wait())

    rt_t = rt_ref[...]
    rec_ref[...] = jnp.concatenate([rt_t, jnp.zeros((LANES - SUBLANES, db), F32)], axis=0).T
    expert = lax.broadcasted_iota(jnp.int32, (N_EXPERTS, db), 0).astype(F32)
    pos = []
    for e_lane, rank_lane in ((RT_E1, RT_RANK1), (RT_E2, RT_RANK2)):
        seg = jnp.sum(jnp.where(expert == rt_t[e_lane:e_lane + 1, :], offs_ref[...], 0.0), axis=0, keepdims=True)
        pos.append((seg + rt_t[rank_lane:rank_lane + 1, :]) * SUBLANES)
    pos = jnp.concatenate(pos, axis=0).astype(jnp.int32)
    pos_ref[...] = pos
    pos_vmem[...] = jnp.concatenate([pos, jnp.zeros((SUBLANES - 2, db), jnp.int32)], axis=0)
    to_smem = pltpu.make_async_copy(pos_vmem, pos_smem, sem_pos)
    to_smem.start()
    to_smem.wait()

    for k in range(db):
        src = h_ref.at[pl.ds(k * SUBLANES, SUBLANES), :]
        for slot in range(2):
            dst_row = pl.multiple_of(pos_smem[slot, k], SUBLANES)
            pltpu.make_async_copy(src, xs_hbm.at[pl.ds(dst_row, SUBLANES), :], sem).start(priority=slot)
    for slot in range(2):
        pltpu.make_async_copy(h_ref, xs_hbm.at[pl.ds(0, db * SUBLANES), :], sem).wait()


def _dispatch(pad_tile, n_tiles, rt_t, offs_b, h2_slabs, sorted_rows):
    n = rt_t.shape[1]
    db = DISPATCH_ROWS
    grid_spec = pltpu.PrefetchScalarGridSpec(
        num_scalar_prefetch=2,
        grid=(n // db,),
        in_specs=[pl.BlockSpec((SUBLANES, db), lambda i, pt, nt: (0, i)),
                  pl.BlockSpec(offs_b.shape, lambda i, pt, nt: (0, 0)),
                  pl.BlockSpec((db * SUBLANES, LANES), lambda i, pt, nt: (i, 0))],
        out_specs=[pl.BlockSpec((None, 2, db), lambda i, pt, nt: (i, 0, 0)),
                   pl.BlockSpec((db, LANES), lambda i, pt, nt: (i, 0)),
                   pl.BlockSpec(memory_space=pl.ANY)],
        scratch_shapes=[pltpu.VMEM((EXPERT_ROWS * SUBLANES, LANES), F32),
                        pltpu.VMEM((SUBLANES, db), jnp.int32), pltpu.SMEM((SUBLANES, db), jnp.int32),
                        pltpu.SemaphoreType.DMA(()), pltpu.SemaphoreType.DMA(()), pltpu.SemaphoreType.DMA(())],
    )
    return pl.pallas_call(
        _dispatch_kernel,
        grid_spec=grid_spec,
        out_shape=[jax.ShapeDtypeStruct((n // db, 2, db), jnp.int32),
                   jax.ShapeDtypeStruct((n, LANES), F32),
                   jax.ShapeDtypeStruct((sorted_rows * SUBLANES, LANES), F32)],
        compiler_params=_params("arbitrary"),
        name="dispatch",
    )(pad_tile, n_tiles, rt_t, offs_b, h2_slabs)


TILE_DMA_PRIORITY = 1


def _expert_tile(x_ref, y_ref, wg_s, wu_s, wd_s):
    rows = EXPERT_ROWS
    x = _load_slabs(x_ref, rows).astype(BF16)
    hid = []
    for c in range(0, wg_s.shape[1], MXU_WIDTH):
        gate = jnp.dot(x, wg_s[:, c:c + MXU_WIDTH], preferred_element_type=F32)
        up = jnp.dot(x, wu_s[:, c:c + MXU_WIDTH], preferred_element_type=F32)
        hid.append((gate * jax.nn.sigmoid(gate) * up).astype(BF16))
    hid = jnp.concatenate(hid, axis=1)
    for c in range(0, wd_s.shape[1], MXU_WIDTH):
        y = jnp.dot(hid, wd_s[:, c:c + MXU_WIDTH], preferred_element_type=F32)
        for s in range(c // LANES, (c + MXU_WIDTH) // LANES):
            y_ref[pl.ds(s, rows, stride=SUBLANES), :] = y[:, s * LANES - c:(s + 1) * LANES - c]


def _experts_kernel(tile_start_ref, tiles_ref, x_hbm, wg_hbm, wu_hbm, wd_hbm, y_hbm,
                    xbuf, ybuf, wg_f, wu_f, wd_f, wg_s, wu_s, wd_s, sem_x, sem_y, sem_w, *, layer):
    e = pl.program_id(0)
    last_step = e == pl.num_programs(0) - 1
    tile = EXPERT_ROWS * SUBLANES
    n = tiles_ref[e]
    base = tile_start_ref[e]
    n_tiles = tile_start_ref[N_EXPERTS - 1] + tiles_ref[N_EXPERTS - 1]
    tile_rows = lambda g: pl.ds(pl.multiple_of(g * tile, tile), tile)

    def x_copy(g, slot):
        return pltpu.make_async_copy(x_hbm.at[tile_rows(g), :], xbuf.at[slot], sem_x.at[slot])

    def y_copy(g, slot):
        return pltpu.make_async_copy(ybuf.at[slot], y_hbm.at[tile_rows(g), :], sem_y.at[slot])

    def w_copies(expert, slot):
        return [pltpu.make_async_copy(w_hbm.at[layer, expert], stage.at[slot], sem_w.at[slot])
                for w_hbm, stage in ((wg_hbm, wg_f), (wu_hbm, wu_f), (wd_hbm, wd_f))]

    w_slot = lax.rem(e, 2)

    @pl.when(e == 0)
    def _():
        x_copy(0, 0).start(priority=TILE_DMA_PRIORITY)
        for cp in w_copies(0, 0):
            cp.start()

    for cp in w_copies(e, w_slot):
        cp.wait()

    @pl.when(jnp.logical_not(last_step))
    def _():
        for cp in w_copies(e + 1, 1 - w_slot):
            cp.start()

    @pl.when(n > 0)
    def _():
        wg_s[...] = wg_f[w_slot].astype(BF16)
        wu_s[...] = wu_f[w_slot].astype(BF16)
        wd_s[...] = wd_f[w_slot].astype(BF16)

        def tile_body(g, carry):
            slot = lax.rem(g, 2)
            x_copy(g, slot).wait()

            @pl.when(g + 1 < n_tiles)
            def _():
                x_copy(g + 1, 1 - slot).start(priority=TILE_DMA_PRIORITY)

            @pl.when(g >= 2)
            def _():
                y_copy(g - 2, slot).wait()

            _expert_tile(xbuf.at[slot], ybuf.at[slot], wg_s, wu_s, wd_s)
            y_copy(g, slot).start(priority=TILE_DMA_PRIORITY)
            return carry
        lax.fori_loop(base, base + n, tile_body, 0)

    @pl.when(last_step)
    def _():
        @pl.when(n_tiles >= 2)
        def _():
            y_copy(n_tiles - 2, lax.rem(n_tiles, 2)).wait()
        y_copy(n_tiles - 1, lax.rem(n_tiles - 1, 2)).wait()

        first_unused = n_tiles
        max_tiles = y_hbm.shape[0] // tile
        ybuf[0] = jnp.zeros(ybuf.shape[1:], ybuf.dtype)
        zero_copy = lambda t: pltpu.make_async_copy(
            ybuf.at[0], y_hbm.at[pl.ds(pl.multiple_of(t * tile, tile), tile), :], sem_y.at[0])

        def start(t, carry):
            zero_copy(t).start()
            return carry

        def wait(t, carry):
            zero_copy(t).wait()
            return carry
        lax.fori_loop(first_unused, max_tiles, start, 0)
        lax.fori_loop(first_unused, max_tiles, wait, 0)


def _experts(tile_start, tiles, xs, wg, wu, wd, layer):
    rows = EXPERT_ROWS
    _, n_experts, d, dff = wg.shape
    assert d == SUBLANES * LANES and n_experts == N_EXPERTS
    any_spec = pl.BlockSpec(memory_space=pl.ANY)
    grid_spec = pltpu.PrefetchScalarGridSpec(
        num_scalar_prefetch=2,
        grid=(n_experts,),
        in_specs=[any_spec, any_spec, any_spec, any_spec],
        out_specs=any_spec,
        scratch_shapes=[pltpu.VMEM((2, rows * SUBLANES, LANES), F32), pltpu.VMEM((2, rows * SUBLANES, LANES), F32),
                        pltpu.VMEM((2, d, dff), F32), pltpu.VMEM((2, d, dff), F32), pltpu.VMEM((2, dff, d), F32),
                        pltpu.VMEM((d, dff), BF16), pltpu.VMEM((d, dff), BF16), pltpu.VMEM((dff, d), BF16),
                        pltpu.SemaphoreType.DMA((2,)), pltpu.SemaphoreType.DMA((2,)),
                        pltpu.SemaphoreType.DMA((2,))],
    )
    return pl.pallas_call(
        functools.partial(_experts_kernel, layer=layer),
        grid_spec=grid_spec,
        out_shape=jax.ShapeDtypeStruct(xs.shape, F32),
        compiler_params=_params("arbitrary"),
        name="experts",
    )(tile_start, tiles, xs, wg, wu, wd)


def _start_slab_gather(src_hbm, dst, sem, row_ref, base, count, priority):
    def body(k, carry):
        _slab_copy(src_hbm, dst, sem, row_ref[base + k], k).start(priority=priority)
        return carry
    lax.fori_loop(0, count, body, 0, unroll=GATHER_UNROLL)


def _combine_kernel(pos_ref, ys_hbm, x1_ref, rt_ref, g_ref, o_ref, buf, sem, *, final_norm):
    i = pl.program_id(0)
    rows = COMBINE_ROWS
    n_steps = pl.num_programs(0)
    slot = i % 2

    def start(step, s):
        for j in range(2):
            _start_slab_gather(ys_hbm, buf.at[s, j], sem.at[s], pos_ref, (step * 2 + j) * rows, rows, priority=j)

    @pl.when(i == 0)
    def _():
        start(0, 0)

    @pl.when(i + 1 < n_steps)
    def _():
        start(i + 1, 1 - slot)

    _wait_slabs(ys_hbm, buf.at[slot, 0], sem.at[slot])
    _wait_slabs(ys_hbm, buf.at[slot, 1], sem.at[slot])
    rt = rt_ref[...]
    lane = lax.broadcasted_iota(jnp.int32, rt.shape, 1)
    w1 = jnp.sum(jnp.where(lane == RT_W1, rt, 0.0), axis=-1, keepdims=True)
    w2 = jnp.sum(jnp.where(lane == RT_W2, rt, 0.0), axis=-1, keepdims=True)
    y = w1 * _load_slabs(buf.at[slot, 0], rows) + w2 * _load_slabs(buf.at[slot, 1], rows)
    x2 = x1_ref[...] + y
    if final_norm:
        x2 = _rms(x2, g_ref[...])
    o_ref[...] = x2


def _combine(pos, ys, x1, rt, g, final_norm):
    n, d = x1.shape
    rows = COMBINE_ROWS
    grid_spec = pltpu.PrefetchScalarGridSpec(
        num_scalar_prefetch=1,
        grid=(n // rows,),
        in_specs=[pl.BlockSpec(memory_space=pl.ANY),
                  pl.BlockSpec((rows, d), lambda i, pos: (i, 0)),
                  pl.BlockSpec((rows, LANES), lambda i, pos: (i, 0)),
                  pl.BlockSpec(g.shape, lambda i, pos: (0, 0))],
        out_specs=pl.BlockSpec((rows, d), lambda i, pos: (i, 0)),
        scratch_shapes=[pltpu.VMEM((2, 2, rows * SUBLANES, LANES), F32), pltpu.SemaphoreType.DMA((2,))],
    )
    return pl.pallas_call(
        functools.partial(_combine_kernel, final_norm=final_norm),
        grid_spec=grid_spec,
        out_shape=jax.ShapeDtypeStruct((n, d), F32),
        compiler_params=_params("arbitrary"),
        name="combine",
    )(pos, ys, x1, rt, g)


def _dispatch_plan(counts, n):
    rows = EXPERT_ROWS
    max_tiles = (2 * n) // rows + N_EXPERTS
    cnt = counts[:N_EXPERTS, 0].astype(jnp.int32)
    tiles = (cnt + rows - 1) // rows
    tile_end = jnp.cumsum(tiles)
    tile_start = tile_end - tiles
    n_tiles = tile_end[-1:]
    pad_tile = jnp.where(cnt % rows != 0, tile_end - 1, -1).astype(jnp.int32)
    offs_b = jnp.broadcast_to((tile_start * rows).astype(F32)[:, None], (N_EXPERTS, DISPATCH_ROWS))
    return tile_start.astype(jnp.int32), tiles, n_tiles, pad_tile, offs_b, max_tiles * rows


def _pad_lanes(a, width):
    return jnp.pad(a, ((0, 0), (0, width - a.shape[1])))


def kernel(x, norm_mix_g, w_in, w_g2_f, b_g_f, w_g2_b, b_g_b, gla_norm_g, rpb, w_out, norm_ffn_g, w_grp, b_grp,
           w_exp, b_exp, w_gate, w_up, w_down, final_norm_g):
    batch, seq, d = x.shape
    n = batch * seq
    rows = seq // GRID_W
    depth = w_in.shape[0]
    xf = x.reshape(n, d)
    c_na = 3 * NA_WIDTH
    c_qk = c_na + 2 * GLA_KEY_WIDTH
    c_v = c_qk + GLA_VAL_WIDTH
    c_r = c_v + GLA_VAL_WIDTH
    for l in range(depth):
        wl = w_in[l]
        q_scale = jnp.concatenate([jnp.full((NA_WIDTH,), NA_HEAD_DIM ** -0.5, F32),
                                   jnp.ones((2 * NA_WIDTH,), F32)])
        w_na = (wl[:, :c_na] * q_scale).astype(BF16)
        qk_scale = jnp.concatenate([jnp.full((GLA_KEY_WIDTH,), GLA_DK ** -0.5, F32),
                                    jnp.ones((GLA_KEY_WIDTH,), F32)])
        w_qk = (wl[:, c_na:c_qk] * qk_scale).astype(BF16)
        w_v = wl[:, c_qk:c_v].astype(BF16)
        w_r = wl[:, c_v:c_r].astype(BF16)
        w_lr = _pad_lanes(wl[:, c_r:], LANES)
        w_g2 = jnp.zeros((LANES, 2 * GLA_KEY_WIDTH), F32)
        w_g2 = w_g2.at[:GLA_GATE_RANK, :GLA_KEY_WIDTH].set(w_g2_f[l])
        w_g2 = w_g2.at[GLA_GATE_RANK:2 * GLA_GATE_RANK, GLA_KEY_WIDTH:].set(w_g2_b[l])
        b_g = jnp.concatenate([b_g_f[l], b_g_b[l]])[None, :]
        na_qkv, gqk, gv, gr, gates = _in_proj(xf, norm_mix_g[l][None, :], w_na, w_qk, w_v, w_r, w_lr, w_g2, b_g)

        y_na = _na(na_qkv, _na_bias_table(rpb[l], rows), batch, rows)
        o_f, o_b = _gla(gqk, gv, gates, batch, seq)

        w_o = w_out[l].astype(BF16)
        w_rt = _pad_lanes(jnp.concatenate([w_exp[l], w_grp[l]], axis=1), LANES)
        w_rt_hi = w_rt.astype(BF16)
        w_rt_lo = (w_rt - w_rt_hi.astype(F32)).astype(BF16)
        w_rt3 = jnp.concatenate([w_rt_hi, w_rt_hi, w_rt_lo], axis=0).T
        b_rt = _pad_lanes(jnp.concatenate([b_exp[l], b_grp[l]])[None, :], ROUTE_ROWS).T
        x1, h2_slabs, rt_t, counts = _out_route(y_na, o_f, o_b, gr, xf, w_o[:NA_WIDTH], w_o[NA_WIDTH:],
                                                gla_norm_g[l][None, :], norm_ffn_g[l][None, :], w_rt3, b_rt)

        tile_start, tiles, n_tiles, pad_tile, offs_b, sorted_rows = _dispatch_plan(counts, n)
        pos, rt, xs = _dispatch(pad_tile, n_tiles, rt_t, offs_b, h2_slabs, sorted_rows)
        ys = _experts(tile_start, tiles, xs, w_gate, w_up, w_down, l)
        pos = pos.reshape(-1)
        last = l == depth - 1
        xf = _combine(pos, ys, x1, rt, final_norm_g[None, :], final_norm=last)
    return xf.reshape(batch, seq, d)
```

```python
import functools

import jax
import jax.numpy as jnp
from jax import lax
from jax.experimental import pallas as pl
from jax.experimental.pallas import tpu as pltpu

F32 = jnp.float32
BF16 = jnp.bfloat16

GRID_W = 64
NA_HEADS = 8
NA_HEAD_DIM = 64
NA_WIDTH = NA_HEADS * NA_HEAD_DIM
WIN_H_MAX = 8
WIN_W = 16
GLA_HEADS = 4
GLA_DK = 64
GLA_DV = 128
GLA_KEY_WIDTH = GLA_HEADS * GLA_DK
GLA_VAL_WIDTH = GLA_HEADS * GLA_DV
GLA_GATE_RANK = 16
GLA_GATE_NORMALIZER = 16.0
GLA_CHUNK = 64
N_GROUPS = 4
EXPERTS_PER_GROUP = 8
N_EXPERTS = N_GROUPS * EXPERTS_PER_GROUP
RMS_EPS = 1e-6

LANES = 128
SUBLANES = 8
MXU_WIDTH = 256
VMEM_LIMIT_BYTES = 56 * 1024 * 1024

MASK_VALUE = -1e30

IN_PROJ_ROWS = 512
NA_ROW_BLOCK = 8
NA_ROW_UNROLL = 4
GLA_STEP_CHUNKS = 4
OUT_ROWS = 512
EXPERT_ROWS = 512
EXPERT_TILE_BUFFERS = 3
DISPATCH_ROWS = 512
COMBINE_ROWS = DISPATCH_ROWS
GATHER_UNROLL = 64


def _params(*sem):
    return pltpu.CompilerParams(dimension_semantics=sem, vmem_limit_bytes=VMEM_LIMIT_BYTES)


def _rms(x, g):
    return x * lax.rsqrt(jnp.mean(x * x, axis=-1, keepdims=True) + RMS_EPS) * g


def _in_proj_kernel(x_ref, g_ref, w_na_ref, w_qk_ref, w_v_ref, w_r_ref, w_lr_ref,
                    w_g2_ref, b_g_ref, na_ref, qk_ref, v_ref, r_ref, gate_ref, w_gate):
    @pl.when(pl.program_id(0) == 0)
    def _():
        w_gate[...] = jnp.dot(w_lr_ref[...], w_g2_ref[...], preferred_element_type=F32,
                              precision=lax.Precision.HIGHEST).astype(BF16)

    h = _rms(x_ref[...], g_ref[...]).astype(BF16)
    na_ref[...] = jnp.dot(h, w_na_ref[...], preferred_element_type=F32).astype(BF16)
    qk_ref[...] = jnp.dot(h, w_qk_ref[...], preferred_element_type=F32)
    v_ref[...] = jnp.dot(h, w_v_ref[...], preferred_element_type=F32).astype(BF16)
    r_ref[...] = jnp.dot(h, w_r_ref[...], preferred_element_type=F32)
    z = jnp.dot(h, w_gate[...], preferred_element_type=F32) + b_g_ref[...]
    log_sig = jnp.minimum(z, 0.0) - jnp.log(1.0 + jnp.exp(-jnp.abs(z)))
    gate_ref[...] = log_sig * (1.0 / GLA_GATE_NORMALIZER)


def _in_proj(x, g, w_na, w_qk, w_v, w_r, w_lr, w_g2, b_g):
    n, d = x.shape
    rows = IN_PROJ_ROWS
    row_spec = lambda width: pl.BlockSpec((rows, width), lambda i: (i, 0))
    full = lambda a: pl.BlockSpec(a.shape, lambda i: (0,) * a.ndim)
    return pl.pallas_call(
        _in_proj_kernel,
        grid=(n // rows,),
        in_specs=[row_spec(d), full(g), full(w_na), full(w_qk), full(w_v), full(w_r), full(w_lr),
                  full(w_g2), full(b_g)],
        out_specs=[row_spec(3 * NA_WIDTH), row_spec(2 * GLA_KEY_WIDTH), row_spec(GLA_VAL_WIDTH),
                   row_spec(GLA_VAL_WIDTH), row_spec(2 * GLA_KEY_WIDTH)],
        out_shape=[jax.ShapeDtypeStruct((n, 3 * NA_WIDTH), BF16),
                   jax.ShapeDtypeStruct((n, 2 * GLA_KEY_WIDTH), F32),
                   jax.ShapeDtypeStruct((n, GLA_VAL_WIDTH), BF16),
                   jax.ShapeDtypeStruct((n, GLA_VAL_WIDTH), F32),
                   jax.ShapeDtypeStruct((n, 2 * GLA_KEY_WIDTH), F32)],
        scratch_shapes=[pltpu.VMEM((d, 2 * GLA_KEY_WIDTH), BF16)],
        compiler_params=_params("arbitrary"),
        name="in_proj",
    )(x, g, w_na, w_qk, w_v, w_r, w_lr, w_g2, b_g)


def _na_bias_table(rpb, rows):
    kh = min(WIN_H_MAX, rows)
    w = jnp.arange(GRID_W)[:, None, None]
    x = jnp.arange(GRID_W)[None, :, None]
    cs = jnp.clip(w - WIN_W // 2, 0, GRID_W - WIN_W)
    valid = (x >= cs) & (x < cs + WIN_W)
    col_sel = (valid & (jnp.arange(2 * WIN_W - 1)[None, None, :] == x - w + (WIN_W - 1))).astype(F32)
    by_col = jnp.einsum("hrc,wxc->hwrx", rpb.astype(F32), col_sel, precision=lax.Precision.HIGHEST)
    by_col = jnp.where(valid[None, :, None, :, 0], by_col, MASK_VALUE)
    by_col = by_col.reshape(NA_HEADS * GRID_W, (2 * WIN_H_MAX - 1) * GRID_W)
    bias = jnp.stack([by_col[:, (WIN_H_MAX - 1 - c) * GRID_W:(WIN_H_MAX - 1 - c + kh) * GRID_W]
                      for c in range(kh)])
    return bias.reshape(kh, NA_HEADS // 2, 2 * GRID_W, kh * GRID_W)


def _na_kernel(q_ref, kp_ref, kc_ref, kn_ref, vp_ref, vc_ref, vn_ref, tbl_ref, o_ref,
               kbuf, vbuf, *, rows, kh):
    rb = NA_ROW_BLOCK
    j = pl.program_id(1)
    blk = rb * GRID_W
    for s, (k_src, v_src) in enumerate(((kp_ref, vp_ref), (kc_ref, vc_ref), (kn_ref, vn_ref))):
        kbuf[s * blk:(s + 1) * blk, :] = k_src[...].reshape(blk, NA_WIDTH)
        vbuf[s * blk:(s + 1) * blk, :] = v_src[...].reshape(blk, NA_WIDTH)
    lane = lax.broadcasted_iota(jnp.int32, (GRID_W, LANES), 1)
    first = lane < NA_HEAD_DIM

    def row_body(lr, carry):
        r = j * rb + lr
        start = jnp.clip(r - kh // 2, 0, rows - kh)
        cls = r - start
        local = pl.multiple_of((start - (j - 1) * rb) * GRID_W, GRID_W)
        q_row = q_ref[lr]
        k_win = kbuf[pl.ds(local, kh * GRID_W), :]
        v_win = vbuf[pl.ds(local, kh * GRID_W), :]
        pairs = [slice(p * LANES, (p + 1) * LANES) for p in range(NA_HEADS // 2)]
        scores = []
        for sl in pairs:
            q_pair = q_row[:, sl]
            zero = jnp.zeros_like(q_pair)
            q_bd = jnp.concatenate([jnp.where(first, q_pair, zero), jnp.where(first, zero, q_pair)], axis=0)
            scores.append(lax.dot_general(q_bd, k_win[:, sl], (((1,), (1,)), ((), ())),
                                          preferred_element_type=F32))
        probs, denoms = [], []
        for p, s in enumerate(scores):
            s = s + tbl_ref[cls, p]
            e = jnp.exp(s - jnp.max(s, axis=-1, keepdims=True))
            denoms.append(jnp.sum(e, axis=-1, keepdims=True))
            probs.append(e.astype(BF16))
        outs = []
        for sl, e, denom in zip(pairs, probs, denoms):
            o = jnp.dot(e, v_win[:, sl], preferred_element_type=F32) / denom
            outs.append(jnp.where(first, o[:GRID_W], o[GRID_W:]))
        o_ref[lr] = jnp.concatenate(outs, axis=-1).astype(o_ref.dtype)
        return carry

    lax.fori_loop(0, rb, row_body, 0, unroll=NA_ROW_UNROLL)


def _na(na_qkv, tbl, batch, rows):
    kh = min(WIN_H_MAX, rows)
    rb = NA_ROW_BLOCK
    nblk = rows // rb
    x4 = na_qkv.reshape(batch, rows, GRID_W, 3 * NA_WIDTH)
    blk = (None, rb, GRID_W, NA_WIDTH)
    prev = lambda j: jnp.maximum(j - 1, 0)
    nxt = lambda j: jnp.minimum(j + 1, nblk - 1)
    specs = [pl.BlockSpec(blk, lambda b, j: (b, j, 0, 0))]
    for col in (1, 2):
        specs += [pl.BlockSpec(blk, lambda b, j, col=col: (b, prev(j), 0, col)),
                  pl.BlockSpec(blk, lambda b, j, col=col: (b, j, 0, col)),
                  pl.BlockSpec(blk, lambda b, j, col=col: (b, nxt(j), 0, col))]
    specs.append(pl.BlockSpec(tbl.shape, lambda b, j: (0, 0, 0, 0)))
    out = pl.pallas_call(
        functools.partial(_na_kernel, rows=rows, kh=kh),
        grid=(batch, nblk),
        in_specs=specs,
        out_specs=pl.BlockSpec(blk, lambda b, j: (b, j, 0, 0)),
        out_shape=jax.ShapeDtypeStruct((batch, rows, GRID_W, NA_WIDTH), BF16),
        scratch_shapes=[pltpu.VMEM((3 * rb * GRID_W, NA_WIDTH), BF16),
                        pltpu.VMEM((3 * rb * GRID_W, NA_WIDTH), BF16)],
        compiler_params=_params("arbitrary", "arbitrary"),
        name="na",
    )(x4, x4, x4, x4, x4, x4, x4, tbl)
    return out.reshape(batch * rows * GRID_W, NA_WIDTH)


def _block_diag_mask(row_block, col_block, nblocks):
    shape = (row_block * nblocks, col_block * nblocks)
    r = lax.shift_right_logical(lax.broadcasted_iota(jnp.int32, shape, 0), row_block.bit_length() - 1)
    c = lax.shift_right_logical(lax.broadcasted_iota(jnp.int32, shape, 1), col_block.bit_length() - 1)
    return r == c


def _split_bf16x3(x):
    hi = x.astype(BF16)
    rest = x - hi.astype(F32)
    mid = rest.astype(BF16)
    lo = (rest - mid.astype(F32)).astype(BF16)
    return hi, mid, lo


def _gla_direction(qk_ref, v_ref, g_ref, *, backward):
    c = GLA_CHUNK
    nc = GLA_STEP_CHUNKS
    kw = GLA_KEY_WIDTH
    order = list(reversed(range(nc))) if backward else list(range(nc))
    chunk = lambda a, n: a[n * c:(n + 1) * c]

    def stage_cumsum():
        step = nc * c
        ti = lax.broadcasted_iota(jnp.int32, (step, step), 0)
        tj = lax.broadcasted_iota(jnp.int32, (step, step), 1)
        same_chunk = lax.shift_right_logical(ti, c.bit_length() - 1) == lax.shift_right_logical(tj, c.bit_length() - 1)
        tri = (tj >= ti) if backward else (tj <= ti)
        cum = jnp.where(same_chunk & tri, 1.0, 0.0).astype(BF16)
        return sum(jnp.dot(cum, piece, preferred_element_type=F32) for piece in _split_bf16x3(g_ref[...]))

    def stage_decays(b):
        ref_row, last_row = (c // 2, 0) if backward else (c // 2 - 1, c - 1)
        rows_of = lambda r: jnp.concatenate(
            [jnp.broadcast_to(b[n * c + r:n * c + r + 1, :], (c, kw)) for n in range(nc)], axis=0)
        b_ref, b_last = rows_of(ref_row), rows_of(last_row)
        q = qk_ref[:, :kw]
        k = qk_ref[:, kw:]
        q_rel = (q * jnp.exp(b - b_ref)).astype(BF16)
        k_rel = (k * jnp.exp(b_ref - b)).astype(BF16)
        k_dec = k * jnp.exp(b_last - b)
        q_dec = (q * jnp.exp(b)).astype(BF16)
        decay = jnp.exp(b_last)
        return q_rel, k_rel, k_dec, q_dec, decay

    def stage_scores(q_rel, k_rel):
        kk_mask = _block_diag_mask(c, GLA_DK, GLA_HEADS)
        out = []
        for n in range(nc):
            k_bd = jnp.where(kk_mask, jnp.concatenate([chunk(k_rel, n)] * GLA_HEADS, axis=0), jnp.zeros((), BF16))
            out.append(lax.dot_general(chunk(q_rel, n), k_bd, (((1,), (1,)), ((), ())),
                                       preferred_element_type=F32))
        return out

    def stage_intra(scores, k_dec, decay):
        si = lax.broadcasted_iota(jnp.int32, (c, c * GLA_HEADS), 0)
        sj = lax.broadcasted_iota(jnp.int32, (c, c * GLA_HEADS), 1) & (c - 1)
        keep = (sj > si) if backward else (sj <= si)
        kv_mask = _block_diag_mask(c, GLA_DV, GLA_HEADS)
        vk_mask = _block_diag_mask(GLA_DK, GLA_DV, 2)
        o_intra, upd, decay_col = [], [], []
        for n in range(nc):
            v = v_ref[n * c:(n + 1) * c, :]
            p = jnp.where(keep, scores[n], 0.0).astype(BF16)
            v_bd = jnp.where(kv_mask, jnp.concatenate([v] * GLA_HEADS, axis=0), jnp.zeros((), BF16))
            o_intra.append(jnp.dot(p, v_bd, preferred_element_type=F32))
            k_dec_t = chunk(k_dec, n).T.astype(BF16)
            pairs = []
            for hp in range(GLA_HEADS // 2):
                kv = jnp.dot(k_dec_t[hp * 2 * GLA_DK:(hp + 1) * 2 * GLA_DK],
                             v[:, hp * 2 * GLA_DV:(hp + 1) * 2 * GLA_DV], preferred_element_type=F32)
                pairs.append(jnp.where(vk_mask, kv, 0.0))
            upd.append(pairs)
            decay_col.append(chunk(decay, n).T[:, :1])
        return o_intra, upd, decay_col

    def scan_step(state, idx, q_dec, o_intra, upd, decay_col, out_ref):
        n = order[idx]
        q_n = chunk(q_dec, n)
        pw = 2 * GLA_DK
        o_inter = jnp.concatenate(
            [jnp.dot(q_n[:, hp * pw:(hp + 1) * pw], s.astype(BF16), preferred_element_type=F32)
             for hp, s in enumerate(state)], axis=1)
        out_ref[n * c:(n + 1) * c, :] = o_intra[n] + o_inter
        return [s * decay_col[n][hp * pw:(hp + 1) * pw] + u for hp, (s, u) in enumerate(zip(state, upd[n]))]

    return stage_cumsum, stage_decays, stage_scores, stage_intra, scan_step


def _gla_kernel(qk_f_ref, v_f_ref, g_f_ref, qk_b_ref, v_b_ref, g_b_ref, o_f_ref, o_b_ref,
                st_f, st_b):
    @pl.when(pl.program_id(1) == 0)
    def _():
        st_f[...] = jnp.zeros_like(st_f)
        st_b[...] = jnp.zeros_like(st_b)

    dirs = (_gla_direction(qk_f_ref, v_f_ref, g_f_ref, backward=False),
            _gla_direction(qk_b_ref, v_b_ref, g_b_ref, backward=True))
    b = [d[0]() for d in dirs]
    dec = [d[1](x) for d, x in zip(dirs, b)]
    scores = [d[2](x[0], x[1]) for d, x in zip(dirs, dec)]
    intra = [d[3](s, x[2], x[4]) for d, s, x in zip(dirs, scores, dec)]
    pairs = range(GLA_HEADS // 2)
    states = [[st[hp] for hp in pairs] for st in (st_f, st_b)]
    for idx in range(GLA_STEP_CHUNKS):
        for j, (d, x, y, out_ref) in enumerate(zip(dirs, dec, intra, (o_f_ref, o_b_ref))):
            states[j] = d[4](states[j], idx, x[3], *y, out_ref)
    for st, state in zip((st_f, st_b), states):
        for hp in pairs:
            st[hp] = state[hp]


def _gla(qk, v, gates, batch, seq):
    step = GLA_STEP_CHUNKS * GLA_CHUNK
    nblk = seq // step
    qk3 = qk.reshape(batch, seq, 2 * GLA_KEY_WIDTH)
    v3 = v.reshape(batch, seq, GLA_VAL_WIDTH)
    g3 = gates.reshape(batch, seq, 2 * GLA_KEY_WIDTH)
    fwd = lambda b, n: (b, n, 0)
    bwd = lambda b, n: (b, nblk - 1 - n, 0)
    bwd_gate = lambda b, n: (b, nblk - 1 - n, 1)
    o_f, o_b = pl.pallas_call(
        _gla_kernel,
        grid=(batch, nblk),
        in_specs=[pl.BlockSpec((None, step, 2 * GLA_KEY_WIDTH), fwd),
                  pl.BlockSpec((None, step, GLA_VAL_WIDTH), fwd),
                  pl.BlockSpec((None, step, GLA_KEY_WIDTH), fwd),
                  pl.BlockSpec((None, step, 2 * GLA_KEY_WIDTH), bwd),
                  pl.BlockSpec((None, step, GLA_VAL_WIDTH), bwd),
                  pl.BlockSpec((None, step, GLA_KEY_WIDTH), bwd_gate)],
        out_specs=[pl.BlockSpec((None, step, GLA_VAL_WIDTH), fwd),
                   pl.BlockSpec((None, step, GLA_VAL_WIDTH), bwd)],
        out_shape=[jax.ShapeDtypeStruct((batch, seq, GLA_VAL_WIDTH), F32)] * 2,
        scratch_shapes=[pltpu.VMEM((GLA_HEADS // 2, 2 * GLA_DK, 2 * GLA_DV), F32),
                        pltpu.VMEM((GLA_HEADS // 2, 2 * GLA_DK, 2 * GLA_DV), F32)],
        compiler_params=_params("arbitrary", "arbitrary"),
        name="gla",
    )(qk3, v3, g3, qk3, v3, g3)
    return o_f.reshape(batch * seq, GLA_VAL_WIDTH), o_b.reshape(batch * seq, GLA_VAL_WIDTH)


def _store_slabs(ref, x):
    rows = x.shape[0]
    for s in range(SUBLANES):
        ref[pl.ds(s, rows, stride=SUBLANES), :] = x[:, s * LANES:(s + 1) * LANES]


def _load_slabs(ref, rows):
    return jnp.concatenate([ref[pl.ds(s, rows, stride=SUBLANES), :] for s in range(SUBLANES)], axis=1)


def _slab_copy(src_hbm, dst, sem, src_row, k):
    dst_row = k * SUBLANES if isinstance(k, int) else pl.multiple_of(k * SUBLANES, SUBLANES)
    return pltpu.make_async_copy(src_hbm.at[pl.ds(pl.multiple_of(src_row, SUBLANES), SUBLANES), :],
                                 dst.at[pl.ds(dst_row, SUBLANES), :], sem)


def _wait_slabs(src_hbm, dst, sem):
    pltpu.make_async_copy(src_hbm.at[pl.ds(0, dst.shape[0]), :], dst, sem).wait()


RT_E1, RT_E2, RT_RANK1, RT_RANK2, RT_W1, RT_W2 = range(6)
GROUP_ROW0 = N_EXPERTS
ROUTE_ROWS = 64


def _out_route_kernel(na_ref, of_ref, ob_ref, r_ref, x_ref, w_na_ref, w_gla_ref, gn_ref, fn_ref,
                      w_rt_ref, b_rt_ref, x1_ref, h2_ref, rt_ref, cnt_ref, carry):
    @pl.when(pl.program_id(0) == 0)
    def _():
        carry[...] = jnp.zeros_like(carry)

    rows = x_ref.shape[0]
    o = of_ref[...] + ob_ref[...]
    r = r_ref[...]
    parts = []
    for h in range(GLA_HEADS):
        sl = slice(h * GLA_DV, (h + 1) * GLA_DV)
        parts.append(_rms(o[:, sl], gn_ref[...]) * (r[:, sl] * jax.nn.sigmoid(r[:, sl])))
    y_gla = jnp.concatenate(parts, axis=-1).astype(BF16)
    x1 = (x_ref[...] + jnp.dot(na_ref[...], w_na_ref[...], preferred_element_type=F32)
          + jnp.dot(y_gla, w_gla_ref[...], preferred_element_type=F32))
    x1_ref[...] = x1
    h2 = _rms(x1, fn_ref[...])
    _store_slabs(h2_ref, h2)

    h_hi = h2.astype(BF16)
    h_lo = (h2 - h_hi.astype(F32)).astype(BF16)
    logits = lax.dot_general(w_rt_ref[...], jnp.concatenate([h_hi, h_lo, h_hi], axis=1),
                             (((1,), (1,)), ((), ())), preferred_element_type=F32)
    logits = logits[:ROUTE_ROWS] + b_rt_ref[...]
    row_i = lax.broadcasted_iota(jnp.int32, (ROUTE_ROWS, rows), 0)
    row = row_i.astype(F32)
    row_grp = lax.shift_right_logical(row_i, EXPERTS_PER_GROUP.bit_length() - 1).astype(F32)
    neg = jnp.float32(-jnp.inf)
    no_row = jnp.float32(ROUTE_ROWS)
    is_grp = (row_i >= GROUP_ROW0) & (row_i < GROUP_ROW0 + N_GROUPS)
    g_logit = jnp.where(is_grp, logits, neg)
    g_max = jnp.max(g_logit, axis=0, keepdims=True)
    g_sel = jnp.min(jnp.where(is_grp & (g_logit == g_max), row, no_row), axis=0, keepdims=True) - GROUP_ROW0
    grp_w = 1.0 / jnp.sum(jnp.where(is_grp, jnp.exp(g_logit - g_max), 0.0), axis=0, keepdims=True)
    in_grp = (row_i < N_EXPERTS) & (row_grp == g_sel)
    e_logit = jnp.where(in_grp, logits, neg)
    v1 = jnp.max(e_logit, axis=0, keepdims=True)
    i1 = jnp.min(jnp.where(in_grp & (e_logit == v1), row, no_row), axis=0, keepdims=True)
    rest = in_grp & (row != i1)
    e_logit2 = jnp.where(rest, logits, neg)
    v2 = jnp.max(e_logit2, axis=0, keepdims=True)
    i2 = jnp.min(jnp.where(rest & (e_logit2 == v2), row, no_row), axis=0, keepdims=True)
    t = jnp.exp(v2 - v1)
    w1 = grp_w / (1.0 + t)
    w2 = grp_w * t / (1.0 + t)

    sel1 = row == i1
    sel2 = row == i2
    onehot = jnp.where(sel1 | sel2, 1.0, 0.0)
    ti = lax.broadcasted_iota(jnp.int32, (rows, rows), 0)
    tj = lax.broadcasted_iota(jnp.int32, (rows, rows), 1)
    earlier = jnp.where(ti < tj, 1.0, 0.0).astype(BF16)
    ranks = jnp.dot(onehot.astype(BF16), earlier, preferred_element_type=F32) + carry[:, :1]
    rank1 = jnp.sum(jnp.where(sel1, ranks, 0.0), axis=0, keepdims=True)
    rank2 = jnp.sum(jnp.where(sel2, ranks, 0.0), axis=0, keepdims=True)
    new_carry = carry[...] + jnp.sum(onehot, axis=1, keepdims=True)
    carry[...] = new_carry
    cnt_ref[...] = new_carry

    fields = {RT_E1: i1, RT_E2: i2, RT_RANK1: rank1, RT_RANK2: rank2, RT_W1: w1, RT_W2: w2}
    zero = jnp.zeros_like(w1)
    rt_ref[...] = jnp.concatenate([fields.get(f, zero) for f in range(SUBLANES)], axis=0)


def _out_route(y_na, o_f, o_b, r, x, w_na, w_gla, gn, fn, w_rt, b_rt):
    n, d = x.shape
    rows = OUT_ROWS
    row_spec = lambda width: pl.BlockSpec((rows, width), lambda i: (i, 0))
    full = lambda a: pl.BlockSpec(a.shape, lambda i: (0,) * a.ndim)
    return pl.pallas_call(
        _out_route_kernel,
        grid=(n // rows,),
        in_specs=[row_spec(NA_WIDTH), row_spec(GLA_VAL_WIDTH), row_spec(GLA_VAL_WIDTH),
                  row_spec(GLA_VAL_WIDTH), row_spec(d), full(w_na), full(w_gla), full(gn), full(fn),
                  full(w_rt), full(b_rt)],
        out_specs=[row_spec(d), pl.BlockSpec((rows * SUBLANES, LANES), lambda i: (i, 0)),
                   pl.BlockSpec((SUBLANES, rows), lambda i: (0, i)),
                   pl.BlockSpec((ROUTE_ROWS, LANES), lambda i: (0, 0))],
        out_shape=[jax.ShapeDtypeStruct((n, d), F32), jax.ShapeDtypeStruct((n * SUBLANES, LANES), F32),
                   jax.ShapeDtypeStruct((SUBLANES, n), F32), jax.ShapeDtypeStruct((ROUTE_ROWS, LANES), F32)],
        scratch_shapes=[pltpu.VMEM((ROUTE_ROWS, LANES), F32)],
        compiler_params=_params("arbitrary"),
        name="out_route",
    )(y_na, o_f, o_b, r, x, w_na, w_gla, gn, fn, w_rt, b_rt)


def _dispatch_kernel(pad_tile_ref, n_tiles_ref, rt_ref, offs_ref, h_ref, pos_ref, rec_ref, xs_hbm,
                     zeros, pos_vmem, pos_smem, sem_zero, sem_pos, sem):
    i = pl.program_id(0)
    db = DISPATCH_ROWS
    tile = EXPERT_ROWS * SUBLANES
    max_tiles = xs_hbm.shape[0] // tile

    @pl.when(i == 0)
    def _():
        zeros[...] = jnp.zeros_like(zeros)

        def zero_copy(t):
            return pltpu.make_async_copy(zeros, xs_hbm.at[pl.ds(pl.multiple_of(t * tile, tile), tile), :], sem_zero)

        def for_each_zero_tile(fn):
            def padded(e, carry):
                @pl.when(pad_tile_ref[e] >= 0)
                def _():
                    fn(zero_copy(pad_tile_ref[e]))
                return carry
            lax.fori_loop(0, N_EXPERTS, padded, 0)

            def tail(t, carry):
                fn(zero_copy(t))
                return carry
            lax.fori_loop(n_tiles_ref[0], max_tiles, tail, 0)

        for_each_zero_tile(lambda cp: cp.start())
        for_each_zero_tile(lambda cp: cp.wait())

    rt_t = rt_ref[...]
    rec_ref[...] = jnp.concatenate([rt_t, jnp.zeros((LANES - SUBLANES, db), F32)], axis=0).T
    expert = lax.broadcasted_iota(jnp.int32, (N_EXPERTS, db), 0).astype(F32)
    pos = []
    for e_lane, rank_lane in ((RT_E1, RT_RANK1), (RT_E2, RT_RANK2)):
        seg = jnp.sum(jnp.where(expert == rt_t[e_lane:e_lane + 1, :], offs_ref[...], 0.0), axis=0, keepdims=True)
        pos.append((seg + rt_t[rank_lane:rank_lane + 1, :]) * SUBLANES)
    pos = jnp.concatenate(pos, axis=0).astype(jnp.int32)
    pos_ref[...] = pos
    pos_vmem[...] = jnp.concatenate([pos, jnp.zeros((SUBLANES - 2, db), jnp.int32)], axis=0)
    to_smem = pltpu.make_async_copy(pos_vmem, pos_smem, sem_pos)
    to_smem.start()
    to_smem.wait()

    for k in range(db):
        src = h_ref.at[pl.ds(k * SUBLANES, SUBLANES), :]
        for slot in range(2):
            dst_row = pl.multiple_of(pos_smem[slot, k], SUBLANES)
            pltpu.make_async_copy(src, xs_hbm.at[pl.ds(dst_row, SUBLANES), :], sem).start(priority=slot)
    for slot in range(2):
        pltpu.make_async_copy(h_ref, xs_hbm.at[pl.ds(0, db * SUBLANES), :], sem).wait()


def _dispatch(pad_tile, n_tiles, rt_t, offs_b, h2_slabs, sorted_rows):
    n = rt_t.shape[1]
    db = DISPATCH_ROWS
    grid_spec = pltpu.PrefetchScalarGridSpec(
        num_scalar_prefetch=2,
        grid=(n // db,),
        in_specs=[pl.BlockSpec((SUBLANES, db), lambda i, pt, nt: (0, i)),
                  pl.BlockSpec(offs_b.shape, lambda i, pt, nt: (0, 0)),
                  pl.BlockSpec((db * SUBLANES, LANES), lambda i, pt, nt: (i, 0))],
        out_specs=[pl.BlockSpec((None, 2, db), lambda i, pt, nt: (i, 0, 0)),
                   pl.BlockSpec((db, LANES), lambda i, pt, nt: (i, 0)),
                   pl.BlockSpec(memory_space=pl.ANY)],
        scratch_shapes=[pltpu.VMEM((EXPERT_ROWS * SUBLANES, LANES), F32),
                        pltpu.VMEM((SUBLANES, db), jnp.int32), pltpu.SMEM((SUBLANES, db), jnp.int32),
                        pltpu.SemaphoreType.DMA(()), pltpu.SemaphoreType.DMA(()), pltpu.SemaphoreType.DMA(())],
    )
    return pl.pallas_call(
        _dispatch_kernel,
        grid_spec=grid_spec,
        out_shape=[jax.ShapeDtypeStruct((n // db, 2, db), jnp.int32),
                   jax.ShapeDtypeStruct((n, LANES), F32),
                   jax.ShapeDtypeStruct((sorted_rows * SUBLANES, LANES), F32)],
        compiler_params=_params("arbitrary"),
        name="dispatch",
    )(pad_tile, n_tiles, rt_t, offs_b, h2_slabs)


TILE_DMA_PRIORITY = 1


def _expert_tile(x_ref, y_ref, wg_s, wu_s, wd_s):
    rows = EXPERT_ROWS
    x = _load_slabs(x_ref, rows).astype(BF16)
    hid = []
    for c in range(0, wg_s.shape[1], MXU_WIDTH):
        gate = jnp.dot(x, wg_s[:, c:c + MXU_WIDTH], preferred_element_type=F32)
        up = jnp.dot(x, wu_s[:, c:c + MXU_WIDTH], preferred_element_type=F32)
        hid.append((gate * jax.nn.sigmoid(gate) * up).astype(BF16))
    hid = jnp.concatenate(hid, axis=1)
    for c in range(0, wd_s.shape[1], MXU_WIDTH):
        y = jnp.dot(hid, wd_s[:, c:c + MXU_WIDTH], preferred_element_type=F32)
        for s in range(c // LANES, (c + MXU_WIDTH) // LANES):
            y_ref[pl.ds(s, rows, stride=SUBLANES), :] = y[:, s * LANES - c:(s + 1) * LANES - c]


def _experts_kernel(tile_start_ref, tiles_ref, x_hbm, wg_hbm, wu_hbm, wd_hbm, y_hbm,
                    xbuf, ybuf, wg_f, wu_f, wd_f, wg_s, wu_s, wd_s, sem_x, sem_y, sem_w, *, layer):
    nbuf = EXPERT_TILE_BUFFERS
    e = pl.program_id(0)
    last_step = e == pl.num_programs(0) - 1
    tile = EXPERT_ROWS * SUBLANES
    n = tiles_ref[e]
    base = tile_start_ref[e]
    n_tiles = tile_start_ref[N_EXPERTS - 1] + tiles_ref[N_EXPERTS - 1]
    tile_rows = lambda g: pl.ds(pl.multiple_of(g * tile, tile), tile)

    def x_copy(g, slot):
        return pltpu.make_async_copy(x_hbm.at[tile_rows(g), :], xbuf.at[slot], sem_x.at[slot])

    def y_copy(g, slot):
        return pltpu.make_async_copy(ybuf.at[slot], y_hbm.at[tile_rows(g), :], sem_y.at[slot])

    def w_copies(expert, slot):
        return [pltpu.make_async_copy(w_hbm.at[layer, expert], stage.at[slot], sem_w.at[slot])
                for w_hbm, stage in ((wg_hbm, wg_f), (wu_hbm, wu_f), (wd_hbm, wd_f))]

    w_slot = lax.rem(e, 2)

    @pl.when(e == 0)
    def _():
        for g in range(nbuf - 1):
            @pl.when(g < n_tiles)
            def _():
                x_copy(g, g).start(priority=TILE_DMA_PRIORITY)
        for cp in w_copies(0, 0):
            cp.start()

    for cp in w_copies(e, w_slot):
        cp.wait()

    @pl.when(jnp.logical_not(last_step))
    def _():
        for cp in w_copies(e + 1, 1 - w_slot):
            cp.start()

    @pl.when(n > 0)
    def _():
        wg_s[...] = wg_f[w_slot].astype(BF16)
        wu_s[...] = wu_f[w_slot].astype(BF16)
        wd_s[...] = wd_f[w_slot].astype(BF16)

        def tile_body(g, carry):
            slot = lax.rem(g, nbuf)
            x_copy(g, slot).wait()

            @pl.when(g + nbuf - 1 < n_tiles)
            def _():
                x_copy(g + nbuf - 1, lax.rem(g + nbuf - 1, nbuf)).start(priority=TILE_DMA_PRIORITY)

            @pl.when(g >= nbuf)
            def _():
                y_copy(g - nbuf, slot).wait()

            _expert_tile(xbuf.at[slot], ybuf.at[slot], wg_s, wu_s, wd_s)
            y_copy(g, slot).start(priority=TILE_DMA_PRIORITY)
            return carry
        lax.fori_loop(base, base + n, tile_body, 0)

    @pl.when(last_step)
    def _():
        for back in range(nbuf, 0, -1):
            @pl.when(n_tiles >= back)
            def _():
                y_copy(n_tiles - back, lax.rem(n_tiles - back, nbuf)).wait()

        first_unused = n_tiles
        max_tiles = y_hbm.shape[0] // tile
        ybuf[0] = jnp.zeros(ybuf.shape[1:], ybuf.dtype)
        zero_copy = lambda t: pltpu.make_async_copy(
            ybuf.at[0], y_hbm.at[pl.ds(pl.multiple_of(t * tile, tile), tile), :], sem_y.at[0])

        def start(t, carry):
            zero_copy(t).start()
            return carry

        def wait(t, carry):
            zero_copy(t).wait()
            return carry
        lax.fori_loop(first_unused, max_tiles, start, 0)
        lax.fori_loop(first_unused, max_tiles, wait, 0)


def _experts(tile_start, tiles, xs, wg, wu, wd, layer):
    rows = EXPERT_ROWS
    _, n_experts, d, dff = wg.shape
    assert d == SUBLANES * LANES and n_experts == N_EXPERTS
    nbuf = EXPERT_TILE_BUFFERS
    any_spec = pl.BlockSpec(memory_space=pl.ANY)
    grid_spec = pltpu.PrefetchScalarGridSpec(
        num_scalar_prefetch=2,
        grid=(n_experts,),
        in_specs=[any_spec, any_spec, any_spec, any_spec],
        out_specs=any_spec,
        scratch_shapes=[pltpu.VMEM((nbuf, rows * SUBLANES, LANES), F32),
                        pltpu.VMEM((nbuf, rows * SUBLANES, LANES), F32),
                        pltpu.VMEM((2, d, dff), F32), pltpu.VMEM((2, d, dff), F32), pltpu.VMEM((2, dff, d), F32),
                        pltpu.VMEM((d, dff), BF16), pltpu.VMEM((d, dff), BF16), pltpu.VMEM((dff, d), BF16),
                        pltpu.SemaphoreType.DMA((nbuf,)), pltpu.SemaphoreType.DMA((nbuf,)),
                        pltpu.SemaphoreType.DMA((2,))],
    )
    return pl.pallas_call(
        functools.partial(_experts_kernel, layer=layer),
        grid_spec=grid_spec,
        out_shape=jax.ShapeDtypeStruct(xs.shape, F32),
        compiler_params=_params("arbitrary"),
        name="experts",
    )(tile_start, tiles, xs, wg, wu, wd)


def _start_slab_gather(src_hbm, dst, sem, row_ref, base, count, priority):
    def body(k, carry):
        _slab_copy(src_hbm, dst, sem, row_ref[base + k], k).start(priority=priority)
        return carry
    lax.fori_loop(0, count, body, 0, unroll=GATHER_UNROLL)


def _combine_kernel(pos_ref, ys_hbm, x1_ref, rt_ref, g_ref, o_ref, buf, sem, *, final_norm):
    i = pl.program_id(0)
    rows = COMBINE_ROWS
    n_steps = pl.num_programs(0)
    slot = i % 2

    def start(step, s):
        for j in range(2):
            _start_slab_gather(ys_hbm, buf.at[s, j], sem.at[s], pos_ref, (step * 2 + j) * rows, rows, priority=j)

    @pl.when(i == 0)
    def _():
        start(0, 0)

    @pl.when(i + 1 < n_steps)
    def _():
        start(i + 1, 1 - slot)

    _wait_slabs(ys_hbm, buf.at[slot, 0], sem.at[slot])
    _wait_slabs(ys_hbm, buf.at[slot, 1], sem.at[slot])
    rt = rt_ref[...]
    lane = lax.broadcasted_iota(jnp.int32, rt.shape, 1)
    w1 = jnp.sum(jnp.where(lane == RT_W1, rt, 0.0), axis=-1, keepdims=True)
    w2 = jnp.sum(jnp.where(lane == RT_W2, rt, 0.0), axis=-1, keepdims=True)
    y = w1 * _load_slabs(buf.at[slot, 0], rows) + w2 * _load_slabs(buf.at[slot, 1], rows)
    x2 = x1_ref[...] + y
    if final_norm:
        x2 = _rms(x2, g_ref[...])
    o_ref[...] = x2


def _combine(pos, ys, x1, rt, g, final_norm):
    n, d = x1.shape
    rows = COMBINE_ROWS
    grid_spec = pltpu.PrefetchScalarGridSpec(
        num_scalar_prefetch=1,
        grid=(n // rows,),
        in_specs=[pl.BlockSpec(memory_space=pl.ANY),
                  pl.BlockSpec((rows, d), lambda i, pos: (i, 0)),
                  pl.BlockSpec((rows, LANES), lambda i, pos: (i, 0)),
                  pl.BlockSpec(g.shape, lambda i, pos: (0, 0))],
        out_specs=pl.BlockSpec((rows, d), lambda i, pos: (i, 0)),
        scratch_shapes=[pltpu.VMEM((2, 2, rows * SUBLANES, LANES), F32), pltpu.SemaphoreType.DMA((2,))],
    )
    return pl.pallas_call(
        functools.partial(_combine_kernel, final_norm=final_norm),
        grid_spec=grid_spec,
        out_shape=jax.ShapeDtypeStruct((n, d), F32),
        compiler_params=_params("arbitrary"),
        name="combine",
    )(pos, ys, x1, rt, g)


def _dispatch_plan(counts, n):
    rows = EXPERT_ROWS
    max_tiles = (2 * n) // rows + N_EXPERTS
    cnt = counts[:N_EXPERTS, 0].astype(jnp.int32)
    tiles = (cnt + rows - 1) // rows
    tile_end = jnp.cumsum(tiles)
    tile_start = tile_end - tiles
    n_tiles = tile_end[-1:]
    pad_tile = jnp.where(cnt % rows != 0, tile_end - 1, -1).astype(jnp.int32)
    offs_b = jnp.broadcast_to((tile_start * rows).astype(F32)[:, None], (N_EXPERTS, DISPATCH_ROWS))
    return tile_start.astype(jnp.int32), tiles, n_tiles, pad_tile, offs_b, max_tiles * rows


def _pad_lanes(a, width):
    return jnp.pad(a, ((0, 0), (0, width - a.shape[1])))


def kernel(x, norm_mix_g, w_in, w_g2_f, b_g_f, w_g2_b, b_g_b, gla_norm_g, rpb, w_out, norm_ffn_g, w_grp, b_grp,
           w_exp, b_exp, w_gate, w_up, w_down, final_norm_g):
    batch, seq, d = x.shape
    n = batch * seq
    rows = seq // GRID_W
    depth = w_in.shape[0]
    xf = x.reshape(n, d)
    c_na = 3 * NA_WIDTH
    c_qk = c_na + 2 * GLA_KEY_WIDTH
    c_v = c_qk + GLA_VAL_WIDTH
    c_r = c_v + GLA_VAL_WIDTH
    for l in range(depth):
        wl = w_in[l]
        q_scale = jnp.concatenate([jnp.full((NA_WIDTH,), NA_HEAD_DIM ** -0.5, F32),
                                   jnp.ones((2 * NA_WIDTH,), F32)])
        w_na = (wl[:, :c_na] * q_scale).astype(BF16)
        qk_scale = jnp.concatenate([jnp.full((GLA_KEY_WIDTH,), GLA_DK ** -0.5, F32),
                                    jnp.ones((GLA_KEY_WIDTH,), F32)])
        w_qk = (wl[:, c_na:c_qk] * qk_scale).astype(BF16)
        w_v = wl[:, c_qk:c_v].astype(BF16)
        w_r = wl[:, c_v:c_r].astype(BF16)
        w_lr = _pad_lanes(wl[:, c_r:], LANES)
        w_g2 = jnp.zeros((LANES, 2 * GLA_KEY_WIDTH), F32)
        w_g2 = w_g2.at[:GLA_GATE_RANK, :GLA_KEY_WIDTH].set(w_g2_f[l])
        w_g2 = w_g2.at[GLA_GATE_RANK:2 * GLA_GATE_RANK, GLA_KEY_WIDTH:].set(w_g2_b[l])
        b_g = jnp.concatenate([b_g_f[l], b_g_b[l]])[None, :]
        na_qkv, gqk, gv, gr, gates = _in_proj(xf, norm_mix_g[l][None, :], w_na, w_qk, w_v, w_r, w_lr, w_g2, b_g)

        y_na = _na(na_qkv, _na_bias_table(rpb[l], rows), batch, rows)
        o_f, o_b = _gla(gqk, gv, gates, batch, seq)

        w_o = w_out[l].astype(BF16)
        w_rt = _pad_lanes(jnp.concatenate([w_exp[l], w_grp[l]], axis=1), LANES)
        w_rt_hi = w_rt.astype(BF16)
        w_rt_lo = (w_rt - w_rt_hi.astype(F32)).astype(BF16)
        w_rt3 = jnp.concatenate([w_rt_hi, w_rt_hi, w_rt_lo], axis=0).T
        b_rt = _pad_lanes(jnp.concatenate([b_exp[l], b_grp[l]])[None, :], ROUTE_ROWS).T
        x1, h2_slabs, rt_t, counts = _out_route(y_na, o_f, o_b, gr, xf, w_o[:NA_WIDTH], w_o[NA_WIDTH:],
                                                gla_norm_g[l][None, :], norm_ffn_g[l][None, :], w_rt3, b_rt)

        tile_start, tiles, n_tiles, pad_tile, offs_b, sorted_rows = _dispatch_plan(counts, n)
        pos, rt, xs = _dispatch(pad_tile, n_tiles, rt_t, offs_b, h2_slabs, sorted_rows)
        ys = _experts(tile_start, tiles, xs, w_gate, w_up, w_down, l)
        pos = pos.reshape(-1)
        last = l == depth - 1
        xf = _combine(pos, ys, x1, rt, final_norm_g[None, :], final_norm=last)
    return xf.reshape(batch, seq, d)
```

```python
import functools

import jax
import jax.numpy as jnp
from jax import lax
from jax.experimental import pallas as pl
from jax.experimental.pallas import tpu as pltpu

F32 = jnp.float32
BF16 = jnp.bfloat16

GRID_W = 64
NA_HEADS = 8
NA_HEAD_DIM = 64
NA_WIDTH = NA_HEADS * NA_HEAD_DIM
WIN_H_MAX = 8
WIN_W = 16
GLA_HEADS = 4
GLA_DK = 64
GLA_DV = 128
GLA_KEY_WIDTH = GLA_HEADS * GLA_DK
GLA_VAL_WIDTH = GLA_HEADS * GLA_DV
GLA_GATE_RANK = 16
GLA_GATE_NORMALIZER = 16.0
GLA_CHUNK = 64
N_GROUPS = 4
EXPERTS_PER_GROUP = 8
N_EXPERTS = N_GROUPS * EXPERTS_PER_GROUP
RMS_EPS = 1e-6

LANES = 128
SUBLANES = 8
MXU_WIDTH = 256
VMEM_LIMIT_BYTES = 56 * 1024 * 1024

MASK_VALUE = -1e30

IN_PROJ_ROWS = 512
NA_ROW_BLOCK = 8
NA_ROW_UNROLL = 4
GLA_STEP_CHUNKS = 4
OUT_ROWS = 512
EXPERT_ROWS = 256
EXPERT_TILE_BUFFERS = 4
DISPATCH_ROWS = 512
COMBINE_ROWS = DISPATCH_ROWS
GATHER_UNROLL = 64


def _params(*sem):
    return pltpu.CompilerParams(dimension_semantics=sem, vmem_limit_bytes=VMEM_LIMIT_BYTES)


def _rms(x, g):
    return x * lax.rsqrt(jnp.mean(x * x, axis=-1, keepdims=True) + RMS_EPS) * g


def _in_proj_kernel(x_ref, g_ref, w_na_ref, w_qk_ref, w_v_ref, w_r_ref, w_lr_ref,
                    w_g2_ref, b_g_ref, na_ref, qk_ref, v_ref, r_ref, gate_ref, w_gate):
    @pl.when(pl.program_id(0) == 0)
    def _():
        w_gate[...] = jnp.dot(w_lr_ref[...], w_g2_ref[...], preferred_element_type=F32,
                              precision=lax.Precision.HIGHEST).astype(BF16)

    h = _rms(x_ref[...], g_ref[...]).astype(BF16)
    na_ref[...] = jnp.dot(h, w_na_ref[...], preferred_element_type=F32).astype(BF16)
    qk_ref[...] = jnp.dot(h, w_qk_ref[...], preferred_element_type=F32)
    v_ref[...] = jnp.dot(h, w_v_ref[...], preferred_element_type=F32).astype(BF16)
    r_ref[...] = jnp.dot(h, w_r_ref[...], preferred_element_type=F32)
    z = jnp.dot(h, w_gate[...], preferred_element_type=F32) + b_g_ref[...]
    log_sig = jnp.minimum(z, 0.0) - jnp.log(1.0 + jnp.exp(-jnp.abs(z)))
    gate_ref[...] = log_sig * (1.0 / GLA_GATE_NORMALIZER)


def _in_proj(x, g, w_na, w_qk, w_v, w_r, w_lr, w_g2, b_g):
    n, d = x.shape
    rows = IN_PROJ_ROWS
    row_spec = lambda width: pl.BlockSpec((rows, width), lambda i: (i, 0))
    full = lambda a: pl.BlockSpec(a.shape, lambda i: (0,) * a.ndim)
    return pl.pallas_call(
        _in_proj_kernel,
        grid=(n // rows,),
        in_specs=[row_spec(d), full(g), full(w_na), full(w_qk), full(w_v), full(w_r), full(w_lr),
                  full(w_g2), full(b_g)],
        out_specs=[row_spec(3 * NA_WIDTH), row_spec(2 * GLA_KEY_WIDTH), row_spec(GLA_VAL_WIDTH),
                   row_spec(GLA_VAL_WIDTH), row_spec(2 * GLA_KEY_WIDTH)],
        out_shape=[jax.ShapeDtypeStruct((n, 3 * NA_WIDTH), BF16),
                   jax.ShapeDtypeStruct((n, 2 * GLA_KEY_WIDTH), F32),
                   jax.ShapeDtypeStruct((n, GLA_VAL_WIDTH), BF16),
                   jax.ShapeDtypeStruct((n, GLA_VAL_WIDTH), F32),
                   jax.ShapeDtypeStruct((n, 2 * GLA_KEY_WIDTH), F32)],
        scratch_shapes=[pltpu.VMEM((d, 2 * GLA_KEY_WIDTH), BF16)],
        compiler_params=_params("arbitrary"),
        name="in_proj",
    )(x, g, w_na, w_qk, w_v, w_r, w_lr, w_g2, b_g)


def _na_bias_table(rpb, rows):
    kh = min(WIN_H_MAX, rows)
    w = jnp.arange(GRID_W)[:, None, None]
    x = jnp.arange(GRID_W)[None, :, None]
    cs = jnp.clip(w - WIN_W // 2, 0, GRID_W - WIN_W)
    valid = (x >= cs) & (x < cs + WIN_W)
    col_sel = (valid & (jnp.arange(2 * WIN_W - 1)[None, None, :] == x - w + (WIN_W - 1))).astype(F32)
    by_col = jnp.einsum("hrc,wxc->hwrx", rpb.astype(F32), col_sel, precision=lax.Precision.HIGHEST)
    by_col = jnp.where(valid[None, :, None, :, 0], by_col, MASK_VALUE)
    by_col = by_col.reshape(NA_HEADS * GRID_W, (2 * WIN_H_MAX - 1) * GRID_W)
    bias = jnp.stack([by_col[:, (WIN_H_MAX - 1 - c) * GRID_W:(WIN_H_MAX - 1 - c + kh) * GRID_W]
                      for c in range(kh)])
    return bias.reshape(kh, NA_HEADS // 2, 2 * GRID_W, kh * GRID_W)


def _na_kernel(q_ref, kp_ref, kc_ref, kn_ref, vp_ref, vc_ref, vn_ref, tbl_ref, o_ref,
               kbuf, vbuf, *, rows, kh):
    rb = NA_ROW_BLOCK
    j = pl.program_id(1)
    blk = rb * GRID_W
    for s, (k_src, v_src) in enumerate(((kp_ref, vp_ref), (kc_ref, vc_ref), (kn_ref, vn_ref))):
        kbuf[s * blk:(s + 1) * blk, :] = k_src[...].reshape(blk, NA_WIDTH)
        vbuf[s * blk:(s + 1) * blk, :] = v_src[...].reshape(blk, NA_WIDTH)
    lane = lax.broadcasted_iota(jnp.int32, (GRID_W, LANES), 1)
    first = lane < NA_HEAD_DIM

    def row_body(lr, carry):
        r = j * rb + lr
        start = jnp.clip(r - kh // 2, 0, rows - kh)
        cls = r - start
        local = pl.multiple_of((start - (j - 1) * rb) * GRID_W, GRID_W)
        q_row = q_ref[lr]
        k_win = kbuf[pl.ds(local, kh * GRID_W), :]
        v_win = vbuf[pl.ds(local, kh * GRID_W), :]
        pairs = [slice(p * LANES, (p + 1) * LANES) for p in range(NA_HEADS // 2)]
        scores = []
        for sl in pairs:
            q_pair = q_row[:, sl]
            zero = jnp.zeros_like(q_pair)
            q_bd = jnp.concatenate([jnp.where(first, q_pair, zero), jnp.where(first, zero, q_pair)], axis=0)
            scores.append(lax.dot_general(q_bd, k_win[:, sl], (((1,), (1,)), ((), ())),
                                          preferred_element_type=F32))
        probs, denoms = [], []
        for p, s in enumerate(scores):
            s = s + tbl_ref[cls, p]
            e = jnp.exp(s - jnp.max(s, axis=-1, keepdims=True))
            denoms.append(jnp.sum(e, axis=-1, keepdims=True))
            probs.append(e.astype(BF16))
        outs = []
        for sl, e, denom in zip(pairs, probs, denoms):
            o = jnp.dot(e, v_win[:, sl], preferred_element_type=F32) / denom
            outs.append(jnp.where(first, o[:GRID_W], o[GRID_W:]))
        o_ref[lr] = jnp.concatenate(outs, axis=-1).astype(o_ref.dtype)
        return carry

    lax.fori_loop(0, rb, row_body, 0, unroll=NA_ROW_UNROLL)


def _na(na_qkv, tbl, batch, rows):
    kh = min(WIN_H_MAX, rows)
    rb = NA_ROW_BLOCK
    nblk = rows // rb
    x4 = na_qkv.reshape(batch, rows, GRID_W, 3 * NA_WIDTH)
    blk = (None, rb, GRID_W, NA_WIDTH)
    prev = lambda j: jnp.maximum(j - 1, 0)
    nxt = lambda j: jnp.minimum(j + 1, nblk - 1)
    specs = [pl.BlockSpec(blk, lambda b, j: (b, j, 0, 0))]
    for col in (1, 2):
        specs += [pl.BlockSpec(blk, lambda b, j, col=col: (b, prev(j), 0, col)),
                  pl.BlockSpec(blk, lambda b, j, col=col: (b, j, 0, col)),
                  pl.BlockSpec(blk, lambda b, j, col=col: (b, nxt(j), 0, col))]
    specs.append(pl.BlockSpec(tbl.shape, lambda b, j: (0, 0, 0, 0)))
    out = pl.pallas_call(
        functools.partial(_na_kernel, rows=rows, kh=kh),
        grid=(batch, nblk),
        in_specs=specs,
        out_specs=pl.BlockSpec(blk, lambda b, j: (b, j, 0, 0)),
        out_shape=jax.ShapeDtypeStruct((batch, rows, GRID_W, NA_WIDTH), BF16),
        scratch_shapes=[pltpu.VMEM((3 * rb * GRID_W, NA_WIDTH), BF16),
                        pltpu.VMEM((3 * rb * GRID_W, NA_WIDTH), BF16)],
        compiler_params=_params("arbitrary", "arbitrary"),
        name="na",
    )(x4, x4, x4, x4, x4, x4, x4, tbl)
    return out.reshape(batch * rows * GRID_W, NA_WIDTH)


def _block_diag_mask(row_block, col_block, nblocks):
    shape = (row_block * nblocks, col_block * nblocks)
    r = lax.shift_right_logical(lax.broadcasted_iota(jnp.int32, shape, 0), row_block.bit_length() - 1)
    c = lax.shift_right_logical(lax.broadcasted_iota(jnp.int32, shape, 1), col_block.bit_length() - 1)
    return r == c


def _split_bf16x3(x):
    hi = x.astype(BF16)
    rest = x - hi.astype(F32)
    mid = rest.astype(BF16)
    lo = (rest - mid.astype(F32)).astype(BF16)
    return hi, mid, lo


def _gla_direction(qk_ref, v_ref, g_ref, *, backward):
    c = GLA_CHUNK
    nc = GLA_STEP_CHUNKS
    kw = GLA_KEY_WIDTH
    order = list(reversed(range(nc))) if backward else list(range(nc))
    chunk = lambda a, n: a[n * c:(n + 1) * c]

    def stage_cumsum():
        step = nc * c
        ti = lax.broadcasted_iota(jnp.int32, (step, step), 0)
        tj = lax.broadcasted_iota(jnp.int32, (step, step), 1)
        same_chunk = lax.shift_right_logical(ti, c.bit_length() - 1) == lax.shift_right_logical(tj, c.bit_length() - 1)
        tri = (tj >= ti) if backward else (tj <= ti)
        cum = jnp.where(same_chunk & tri, 1.0, 0.0).astype(BF16)
        return sum(jnp.dot(cum, piece, preferred_element_type=F32) for piece in _split_bf16x3(g_ref[...]))

    def stage_decays(b):
        ref_row, last_row = (c // 2, 0) if backward else (c // 2 - 1, c - 1)
        rows_of = lambda r: jnp.concatenate(
            [jnp.broadcast_to(b[n * c + r:n * c + r + 1, :], (c, kw)) for n in range(nc)], axis=0)
        b_ref, b_last = rows_of(ref_row), rows_of(last_row)
        q = qk_ref[:, :kw]
        k = qk_ref[:, kw:]
        q_rel = (q * jnp.exp(b - b_ref)).astype(BF16)
        k_rel = (k * jnp.exp(b_ref - b)).astype(BF16)
        k_dec = k * jnp.exp(b_last - b)
        q_dec = (q * jnp.exp(b)).astype(BF16)
        decay = jnp.exp(b_last)
        return q_rel, k_rel, k_dec, q_dec, decay

    def stage_scores(q_rel, k_rel):
        kk_mask = _block_diag_mask(c, GLA_DK, GLA_HEADS)
        out = []
        for n in range(nc):
            k_bd = jnp.where(kk_mask, jnp.concatenate([chunk(k_rel, n)] * GLA_HEADS, axis=0), jnp.zeros((), BF16))
            out.append(lax.dot_general(chunk(q_rel, n), k_bd, (((1,), (1,)), ((), ())),
                                       preferred_element_type=F32))
        return out

    def stage_intra(scores, k_dec, decay):
        si = lax.broadcasted_iota(jnp.int32, (c, c * GLA_HEADS), 0)
        sj = lax.broadcasted_iota(jnp.int32, (c, c * GLA_HEADS), 1) & (c - 1)
        keep = (sj > si) if backward else (sj <= si)
        kv_mask = _block_diag_mask(c, GLA_DV, GLA_HEADS)
        vk_mask = _block_diag_mask(GLA_DK, GLA_DV, 2)
        o_intra, upd, decay_col = [], [], []
        for n in range(nc):
            v = v_ref[n * c:(n + 1) * c, :]
            p = jnp.where(keep, scores[n], 0.0).astype(BF16)
            v_bd = jnp.where(kv_mask, jnp.concatenate([v] * GLA_HEADS, axis=0), jnp.zeros((), BF16))
            o_intra.append(jnp.dot(p, v_bd, preferred_element_type=F32))
            k_dec_t = chunk(k_dec, n).T.astype(BF16)
            pairs = []
            for hp in range(GLA_HEADS // 2):
                kv = jnp.dot(k_dec_t[hp * 2 * GLA_DK:(hp + 1) * 2 * GLA_DK],
                             v[:, hp * 2 * GLA_DV:(hp + 1) * 2 * GLA_DV], preferred_element_type=F32)
                pairs.append(jnp.where(vk_mask, kv, 0.0))
            upd.append(pairs)
            decay_col.append(chunk(decay, n).T[:, :1])
        return o_intra, upd, decay_col

    def scan_step(state, idx, q_dec, o_intra, upd, decay_col, out_ref):
        n = order[idx]
        q_n = chunk(q_dec, n)
        pw = 2 * GLA_DK
        o_inter = jnp.concatenate(
            [jnp.dot(q_n[:, hp * pw:(hp + 1) * pw], s.astype(BF16), preferred_element_type=F32)
             for hp, s in enumerate(state)], axis=1)
        out_ref[n * c:(n + 1) * c, :] = o_intra[n] + o_inter
        return [s * decay_col[n][hp * pw:(hp + 1) * pw] + u for hp, (s, u) in enumerate(zip(state, upd[n]))]

    return stage_cumsum, stage_decays, stage_scores, stage_intra, scan_step


def _gla_kernel(qk_f_ref, v_f_ref, g_f_ref, qk_b_ref, v_b_ref, g_b_ref, o_f_ref, o_b_ref,
                st_f, st_b):
    @pl.when(pl.program_id(1) == 0)
    def _():
        st_f[...] = jnp.zeros_like(st_f)
        st_b[...] = jnp.zeros_like(st_b)

    dirs = (_gla_direction(qk_f_ref, v_f_ref, g_f_ref, backward=False),
            _gla_direction(qk_b_ref, v_b_ref, g_b_ref, backward=True))
    b = [d[0]() for d in dirs]
    dec = [d[1](x) for d, x in zip(dirs, b)]
    scores = [d[2](x[0], x[1]) for d, x in zip(dirs, dec)]
    intra = [d[3](s, x[2], x[4]) for d, s, x in zip(dirs, scores, dec)]
    pairs = range(GLA_HEADS // 2)
    states = [[st[hp] for hp in pairs] for st in (st_f, st_b)]
    for idx in range(GLA_STEP_CHUNKS):
        for j, (d, x, y, out_ref) in enumerate(zip(dirs, dec, intra, (o_f_ref, o_b_ref))):
            states[j] = d[4](states[j], idx, x[3], *y, out_ref)
    for st, state in zip((st_f, st_b), states):
        for hp in pairs:
            st[hp] = state[hp]


def _gla(qk, v, gates, batch, seq):
    step = GLA_STEP_CHUNKS * GLA_CHUNK
    nblk = seq // step
    qk3 = qk.reshape(batch, seq, 2 * GLA_KEY_WIDTH)
    v3 = v.reshape(batch, seq, GLA_VAL_WIDTH)
    g3 = gates.reshape(batch, seq, 2 * GLA_KEY_WIDTH)
    fwd = lambda b, n: (b, n, 0)
    bwd = lambda b, n: (b, nblk - 1 - n, 0)
    bwd_gate = lambda b, n: (b, nblk - 1 - n, 1)
    o_f, o_b = pl.pallas_call(
        _gla_kernel,
        grid=(batch, nblk),
        in_specs=[pl.BlockSpec((None, step, 2 * GLA_KEY_WIDTH), fwd),
                  pl.BlockSpec((None, step, GLA_VAL_WIDTH), fwd),
                  pl.BlockSpec((None, step, GLA_KEY_WIDTH), fwd),
                  pl.BlockSpec((None, step, 2 * GLA_KEY_WIDTH), bwd),
                  pl.BlockSpec((None, step, GLA_VAL_WIDTH), bwd),
                  pl.BlockSpec((None, step, GLA_KEY_WIDTH), bwd_gate)],
        out_specs=[pl.BlockSpec((None, step, GLA_VAL_WIDTH), fwd),
                   pl.BlockSpec((None, step, GLA_VAL_WIDTH), bwd)],
        out_shape=[jax.ShapeDtypeStruct((batch, seq, GLA_VAL_WIDTH), F32)] * 2,
        scratch_shapes=[pltpu.VMEM((GLA_HEADS // 2, 2 * GLA_DK, 2 * GLA_DV), F32),
                        pltpu.VMEM((GLA_HEADS // 2, 2 * GLA_DK, 2 * GLA_DV), F32)],
        compiler_params=_params("arbitrary", "arbitrary"),
        name="gla",
    )(qk3, v3, g3, qk3, v3, g3)
    return o_f.reshape(batch * seq, GLA_VAL_WIDTH), o_b.reshape(batch * seq, GLA_VAL_WIDTH)


def _store_slabs(ref, x):
    rows = x.shape[0]
    for s in range(SUBLANES):
        ref[pl.ds(s, rows, stride=SUBLANES), :] = x[:, s * LANES:(s + 1) * LANES]


def _load_slabs(ref, rows):
    return jnp.concatenate([ref[pl.ds(s, rows, stride=SUBLANES), :] for s in range(SUBLANES)], axis=1)


def _slab_copy(src_hbm, dst, sem, src_row, k):
    dst_row = k * SUBLANES if isinstance(k, int) else pl.multiple_of(k * SUBLANES, SUBLANES)
    return pltpu.make_async_copy(src_hbm.at[pl.ds(pl.multiple_of(src_row, SUBLANES), SUBLANES), :],
                                 dst.at[pl.ds(dst_row, SUBLANES), :], sem)


def _wait_slabs(src_hbm, dst, sem):
    pltpu.make_async_copy(src_hbm.at[pl.ds(0, dst.shape[0]), :], dst, sem).wait()


RT_E1, RT_E2, RT_RANK1, RT_RANK2, RT_W1, RT_W2 = range(6)
GROUP_ROW0 = N_EXPERTS
ROUTE_ROWS = 64


def _out_route_kernel(na_ref, of_ref, ob_ref, r_ref, x_ref, w_na_ref, w_gla_ref, gn_ref, fn_ref,
                      w_rt_ref, b_rt_ref, x1_ref, h2_ref, rt_ref, cnt_ref, carry):
    @pl.when(pl.program_id(0) == 0)
    def _():
        carry[...] = jnp.zeros_like(carry)

    rows = x_ref.shape[0]
    o = of_ref[...] + ob_ref[...]
    r = r_ref[...]
    parts = []
    for h in range(GLA_HEADS):
        sl = slice(h * GLA_DV, (h + 1) * GLA_DV)
        parts.append(_rms(o[:, sl], gn_ref[...]) * (r[:, sl] * jax.nn.sigmoid(r[:, sl])))
    y_gla = jnp.concatenate(parts, axis=-1).astype(BF16)
    x1 = (x_ref[...] + jnp.dot(na_ref[...], w_na_ref[...], preferred_element_type=F32)
          + jnp.dot(y_gla, w_gla_ref[...], preferred_element_type=F32))
    x1_ref[...] = x1
    h2 = _rms(x1, fn_ref[...])
    _store_slabs(h2_ref, h2)

    h_hi = h2.astype(BF16)
    h_lo = (h2 - h_hi.astype(F32)).astype(BF16)
    logits = lax.dot_general(w_rt_ref[...], jnp.concatenate([h_hi, h_lo, h_hi], axis=1),
                             (((1,), (1,)), ((), ())), preferred_element_type=F32)
    logits = logits[:ROUTE_ROWS] + b_rt_ref[...]
    row_i = lax.broadcasted_iota(jnp.int32, (ROUTE_ROWS, rows), 0)
    row = row_i.astype(F32)
    row_grp = lax.shift_right_logical(row_i, EXPERTS_PER_GROUP.bit_length() - 1).astype(F32)
    neg = jnp.float32(-jnp.inf)
    no_row = jnp.float32(ROUTE_ROWS)
    is_grp = (row_i >= GROUP_ROW0) & (row_i < GROUP_ROW0 + N_GROUPS)
    g_logit = jnp.where(is_grp, logits, neg)
    g_max = jnp.max(g_logit, axis=0, keepdims=True)
    g_sel = jnp.min(jnp.where(is_grp & (g_logit == g_max), row, no_row), axis=0, keepdims=True) - GROUP_ROW0
    grp_w = 1.0 / jnp.sum(jnp.where(is_grp, jnp.exp(g_logit - g_max), 0.0), axis=0, keepdims=True)
    in_grp = (row_i < N_EXPERTS) & (row_grp == g_sel)
    e_logit = jnp.where(in_grp, logits, neg)
    v1 = jnp.max(e_logit, axis=0, keepdims=True)
    i1 = jnp.min(jnp.where(in_grp & (e_logit == v1), row, no_row), axis=0, keepdims=True)
    rest = in_grp & (row != i1)
    e_logit2 = jnp.where(rest, logits, neg)
    v2 = jnp.max(e_logit2, axis=0, keepdims=True)
    i2 = jnp.min(jnp.where(rest & (e_logit2 == v2), row, no_row), axis=0, keepdims=True)
    t = jnp.exp(v2 - v1)
    w1 = grp_w / (1.0 + t)
    w2 = grp_w * t / (1.0 + t)

    sel1 = row == i1
    sel2 = row == i2
    onehot = jnp.where(sel1 | sel2, 1.0, 0.0)
    ti = lax.broadcasted_iota(jnp.int32, (rows, rows), 0)
    tj = lax.broadcasted_iota(jnp.int32, (rows, rows), 1)
    earlier = jnp.where(ti < tj, 1.0, 0.0).astype(BF16)
    ranks = jnp.dot(onehot.astype(BF16), earlier, preferred_element_type=F32) + carry[:, :1]
    rank1 = jnp.sum(jnp.where(sel1, ranks, 0.0), axis=0, keepdims=True)
    rank2 = jnp.sum(jnp.where(sel2, ranks, 0.0), axis=0, keepdims=True)
    new_carry = carry[...] + jnp.sum(onehot, axis=1, keepdims=True)
    carry[...] = new_carry
    cnt_ref[...] = new_carry

    fields = {RT_E1: i1, RT_E2: i2, RT_RANK1: rank1, RT_RANK2: rank2, RT_W1: w1, RT_W2: w2}
    zero = jnp.zeros_like(w1)
    rt_ref[...] = jnp.concatenate([fields.get(f, zero) for f in range(SUBLANES)], axis=0)


def _out_route(y_na, o_f, o_b, r, x, w_na, w_gla, gn, fn, w_rt, b_rt):
    n, d = x.shape
    rows = OUT_ROWS
    row_spec = lambda width: pl.BlockSpec((rows, width), lambda i: (i, 0))
    full = lambda a: pl.BlockSpec(a.shape, lambda i: (0,) * a.ndim)
    return pl.pallas_call(
        _out_route_kernel,
        grid=(n // rows,),
        in_specs=[row_spec(NA_WIDTH), row_spec(GLA_VAL_WIDTH), row_spec(GLA_VAL_WIDTH),
                  row_spec(GLA_VAL_WIDTH), row_spec(d), full(w_na), full(w_gla), full(gn), full(fn),
                  full(w_rt), full(b_rt)],
        out_specs=[row_spec(d), pl.BlockSpec((rows * SUBLANES, LANES), lambda i: (i, 0)),
                   pl.BlockSpec((SUBLANES, rows), lambda i: (0, i)),
                   pl.BlockSpec((ROUTE_ROWS, LANES), lambda i: (0, 0))],
        out_shape=[jax.ShapeDtypeStruct((n, d), F32), jax.ShapeDtypeStruct((n * SUBLANES, LANES), F32),
                   jax.ShapeDtypeStruct((SUBLANES, n), F32), jax.ShapeDtypeStruct((ROUTE_ROWS, LANES), F32)],
        scratch_shapes=[pltpu.VMEM((ROUTE_ROWS, LANES), F32)],
        compiler_params=_params("arbitrary"),
        name="out_route",
    )(y_na, o_f, o_b, r, x, w_na, w_gla, gn, fn, w_rt, b_rt)


def _dispatch_kernel(pad_tile_ref, n_tiles_ref, rt_ref, offs_ref, h_ref, pos_ref, rec_ref, xs_hbm,
                     zeros, pos_vmem, pos_smem, sem_zero, sem_pos, sem):
    i = pl.program_id(0)
    db = DISPATCH_ROWS
    tile = EXPERT_ROWS * SUBLANES
    max_tiles = xs_hbm.shape[0] // tile

    @pl.when(i == 0)
    def _():
        zeros[...] = jnp.zeros_like(zeros)

        def zero_copy(t):
            return pltpu.make_async_copy(zeros, xs_hbm.at[pl.ds(pl.multiple_of(t * tile, tile), tile), :], sem_zero)

        def for_each_zero_tile(fn):
            def padded(e, carry):
                @pl.when(pad_tile_ref[e] >= 0)
                def _():
                    fn(zero_copy(pad_tile_ref[e]))
                return carry
            lax.fori_loop(0, N_EXPERTS, padded, 0)

            def tail(t, carry):
                fn(zero_copy(t))
                return carry
            lax.fori_loop(n_tiles_ref[0], max_tiles, tail, 0)

        for_each_zero_tile(lambda cp: cp.start())
        for_each_zero_tile(lambda cp: cp.wait())

    rt_t = rt_ref[...]
    rec_ref[...] = jnp.concatenate([rt_t, jnp.zeros((LANES - SUBLANES, db), F32)], axis=0).T
    expert = lax.broadcasted_iota(jnp.int32, (N_EXPERTS, db), 0).astype(F32)
    pos = []
    for e_lane, rank_lane in ((RT_E1, RT_RANK1), (RT_E2, RT_RANK2)):
        seg = jnp.sum(jnp.where(expert == rt_t[e_lane:e_lane + 1, :], offs_ref[...], 0.0), axis=0, keepdims=True)
        pos.append((seg + rt_t[rank_lane:rank_lane + 1, :]) * SUBLANES)
    pos = jnp.concatenate(pos, axis=0).astype(jnp.int32)
    pos_ref[...] = pos
    pos_vmem[...] = jnp.concatenate([pos, jnp.zeros((SUBLANES - 2, db), jnp.int32)], axis=0)
    to_smem = pltpu.make_async_copy(pos_vmem, pos_smem, sem_pos)
    to_smem.start()
    to_smem.wait()

    for k in range(db):
        src = h_ref.at[pl.ds(k * SUBLANES, SUBLANES), :]
        for slot in range(2):
            dst_row = pl.multiple_of(pos_smem[slot, k], SUBLANES)
            pltpu.make_async_copy(src, xs_hbm.at[pl.ds(dst_row, SUBLANES), :], sem).start(priority=slot)
    for slot in range(2):
        pltpu.make_async_copy(h_ref, xs_hbm.at[pl.ds(0, db * SUBLANES), :], sem).wait()


def _dispatch(pad_tile, n_tiles, rt_t, offs_b, h2_slabs, sorted_rows):
    n = rt_t.shape[1]
    db = DISPATCH_ROWS
    grid_spec = pltpu.PrefetchScalarGridSpec(
        num_scalar_prefetch=2,
        grid=(n // db,),
        in_specs=[pl.BlockSpec((SUBLANES, db), lambda i, pt, nt: (0, i)),
                  pl.BlockSpec(offs_b.shape, lambda i, pt, nt: (0, 0)),
                  pl.BlockSpec((db * SUBLANES, LANES), lambda i, pt, nt: (i, 0))],
        out_specs=[pl.BlockSpec((None, 2, db), lambda i, pt, nt: (i, 0, 0)),
                   pl.BlockSpec((db, LANES), lambda i, pt, nt: (i, 0)),
                   pl.BlockSpec(memory_space=pl.ANY)],
        scratch_shapes=[pltpu.VMEM((EXPERT_ROWS * SUBLANES, LANES), F32),
                        pltpu.VMEM((SUBLANES, db), jnp.int32), pltpu.SMEM((SUBLANES, db), jnp.int32),
                        pltpu.SemaphoreType.DMA(()), pltpu.SemaphoreType.DMA(()), pltpu.SemaphoreType.DMA(())],
    )
    return pl.pallas_call(
        _dispatch_kernel,
        grid_spec=grid_spec,
        out_shape=[jax.ShapeDtypeStruct((n // db, 2, db), jnp.int32),
                   jax.ShapeDtypeStruct((n, LANES), F32),
                   jax.ShapeDtypeStruct((sorted_rows * SUBLANES, LANES), F32)],
        compiler_params=_params("arbitrary"),
        name="dispatch",
    )(pad_tile, n_tiles, rt_t, offs_b, h2_slabs)


TILE_DMA_PRIORITY = 1


def _expert_tile(x_ref, y_ref, wg_s, wu_s, wd_s):
    rows = EXPERT_ROWS
    x = _load_slabs(x_ref, rows).astype(BF16)
    hid = []
    for c in range(0, wg_s.shape[1], MXU_WIDTH):
        gate = jnp.dot(x, wg_s[:, c:c + MXU_WIDTH], preferred_element_type=F32)
        up = jnp.dot(x, wu_s[:, c:c + MXU_WIDTH], preferred_element_type=F32)
        hid.append((gate * jax.nn.sigmoid(gate) * up).astype(BF16))
    hid = jnp.concatenate(hid, axis=1)
    for c in range(0, wd_s.shape[1], MXU_WIDTH):
        y = jnp.dot(hid, wd_s[:, c:c + MXU_WIDTH], preferred_element_type=F32)
        for s in range(c // LANES, (c + MXU_WIDTH) // LANES):
            y_ref[pl.ds(s, rows, stride=SUBLANES), :] = y[:, s * LANES - c:(s + 1) * LANES - c]


def _experts_kernel(tile_start_ref, tiles_ref, x_hbm, wg_hbm, wu_hbm, wd_hbm, y_hbm,
                    xbuf, ybuf, wg_f, wu_f, wd_f, wg_s, wu_s, wd_s, sem_x, sem_y, sem_w, *, layer):
    nbuf = EXPERT_TILE_BUFFERS
    e = pl.program_id(0)
    last_step = e == pl.num_programs(0) - 1
    tile = EXPERT_ROWS * SUBLANES
    n = tiles_ref[e]
    base = tile_start_ref[e]
    n_tiles = tile_start_ref[N_EXPERTS - 1] + tiles_ref[N_EXPERTS - 1]
    tile_rows = lambda g: pl.ds(pl.multiple_of(g * tile, tile), tile)

    def x_copy(g, slot):
        return pltpu.make_async_copy(x_hbm.at[tile_rows(g), :], xbuf.at[slot], sem_x.at[slot])

    def y_copy(g, slot):
        return pltpu.make_async_copy(ybuf.at[slot], y_hbm.at[tile_rows(g), :], sem_y.at[slot])

    def w_copies(expert, slot):
        return [pltpu.make_async_copy(w_hbm.at[layer, expert], stage.at[slot], sem_w.at[slot])
                for w_hbm, stage in ((wg_hbm, wg_f), (wu_hbm, wu_f), (wd_hbm, wd_f))]

    w_slot = lax.rem(e, 2)

    @pl.when(e == 0)
    def _():
        for g in range(nbuf - 1):
            @pl.when(g < n_tiles)
            def _():
                x_copy(g, g).start(priority=TILE_DMA_PRIORITY)
        for cp in w_copies(0, 0):
            cp.start()

    for cp in w_copies(e, w_slot):
        cp.wait()

    @pl.when(jnp.logical_not(last_step))
    def _():
        for cp in w_copies(e + 1, 1 - w_slot):
            cp.start()

    @pl.when(n > 0)
    def _():
        wg_s[...] = wg_f[w_slot].astype(BF16)
        wu_s[...] = wu_f[w_slot].astype(BF16)
        wd_s[...] = wd_f[w_slot].astype(BF16)

        def tile_body(g, carry):
            slot = lax.rem(g, nbuf)
            x_copy(g, slot).wait()

            @pl.when(g + nbuf - 1 < n_tiles)
            def _():
                x_copy(g + nbuf - 1, lax.rem(g + nbuf - 1, nbuf)).start(priority=TILE_DMA_PRIORITY)

            @pl.when(g >= nbuf)
            def _():
                y_copy(g - nbuf, slot).wait()

            _expert_tile(xbuf.at[slot], ybuf.at[slot], wg_s, wu_s, wd_s)
            y_copy(g, slot).start(priority=TILE_DMA_PRIORITY)
            return carry
        lax.fori_loop(base, base + n, tile_body, 0)

    @pl.when(last_step)
    def _():
        for back in range(nbuf, 0, -1):
            @pl.when(n_tiles >= back)
            def _():
                y_copy(n_tiles - back, lax.rem(n_tiles - back, nbuf)).wait()

        first_unused = n_tiles
        max_tiles = y_hbm.shape[0] // tile
        ybuf[0] = jnp.zeros(ybuf.shape[1:], ybuf.dtype)
        zero_copy = lambda t: pltpu.make_async_copy(
            ybuf.at[0], y_hbm.at[pl.ds(pl.multiple_of(t * tile, tile), tile), :], sem_y.at[0])

        def start(t, carry):
            zero_copy(t).start()
            return carry

        def wait(t, carry):
            zero_copy(t).wait()
            return carry
        lax.fori_loop(first_unused, max_tiles, start, 0)
        lax.fori_loop(first_unused, max_tiles, wait, 0)


def _experts(tile_start, tiles, xs, wg, wu, wd, layer):
    rows = EXPERT_ROWS
    _, n_experts, d, dff = wg.shape
    assert d == SUBLANES * LANES and n_experts == N_EXPERTS
    nbuf = EXPERT_TILE_BUFFERS
    any_spec = pl.BlockSpec(memory_space=pl.ANY)
    grid_spec = pltpu.PrefetchScalarGridSpec(
        num_scalar_prefetch=2,
        grid=(n_experts,),
        in_specs=[any_spec, any_spec, any_spec, any_spec],
        out_specs=any_spec,
        scratch_shapes=[pltpu.VMEM((nbuf, rows * SUBLANES, LANES), F32),
                        pltpu.VMEM((nbuf, rows * SUBLANES, LANES), F32),
                        pltpu.VMEM((2, d, dff), F32), pltpu.VMEM((2, d, dff), F32), pltpu.VMEM((2, dff, d), F32),
                        pltpu.VMEM((d, dff), BF16), pltpu.VMEM((d, dff), BF16), pltpu.VMEM((dff, d), BF16),
                        pltpu.SemaphoreType.DMA((nbuf,)), pltpu.SemaphoreType.DMA((nbuf,)),
                        pltpu.SemaphoreType.DMA((2,))],
    )
    return pl.pallas_call(
        functools.partial(_experts_kernel, layer=layer),
        grid_spec=grid_spec,
        out_shape=jax.ShapeDtypeStruct(xs.shape, F32),
        compiler_params=_params("arbitrary"),
        name="experts",
    )(tile_start, tiles, xs, wg, wu, wd)


def _start_slab_gather(src_hbm, dst, sem, row_ref, base, count, priority):
    def body(k, carry):
        _slab_copy(src_hbm, dst, sem, row_ref[base + k], k).start(priority=priority)
        return carry
    lax.fori_loop(0, count, body, 0, unroll=GATHER_UNROLL)


def _combine_kernel(pos_ref, ys_hbm, x1_ref, rt_ref, g_ref, o_ref, buf, sem, *, final_norm):
    i = pl.program_id(0)
    rows = COMBINE_ROWS
    n_steps = pl.num_programs(0)
    slot = i % 2

    def start(step, s):
        for j in range(2):
            _start_slab_gather(ys_hbm, buf.at[s, j], sem.at[s], pos_ref, (step * 2 + j) * rows, rows, priority=j)

    @pl.when(i == 0)
    def _():
        start(0, 0)

    @pl.when(i + 1 < n_steps)
    def _():
        start(i + 1, 1 - slot)

    _wait_slabs(ys_hbm, buf.at[slot, 0], sem.at[slot])
    _wait_slabs(ys_hbm, buf.at[slot, 1], sem.at[slot])
    rt = rt_ref[...]
    lane = lax.broadcasted_iota(jnp.int32, rt.shape, 1)
    w1 = jnp.sum(jnp.where(lane == RT_W1, rt, 0.0), axis=-1, keepdims=True)
    w2 = jnp.sum(jnp.where(lane == RT_W2, rt, 0.0), axis=-1, keepdims=True)
    y = w1 * _load_slabs(buf.at[slot, 0], rows) + w2 * _load_slabs(buf.at[slot, 1], rows)
    x2 = x1_ref[...] + y
    if final_norm:
        x2 = _rms(x2, g_ref[...])
    o_ref[...] = x2


def _combine(pos, ys, x1, rt, g, final_norm):
    n, d = x1.shape
    rows = COMBINE_ROWS
    grid_spec = pltpu.PrefetchScalarGridSpec(
        num_scalar_prefetch=1,
        grid=(n // rows,),
        in_specs=[pl.BlockSpec(memory_space=pl.ANY),
                  pl.BlockSpec((rows, d), lambda i, pos: (i, 0)),
                  pl.BlockSpec((rows, LANES), lambda i, pos: (i, 0)),
                  pl.BlockSpec(g.shape, lambda i, pos: (0, 0))],
        out_specs=pl.BlockSpec((rows, d), lambda i, pos: (i, 0)),
        scratch_shapes=[pltpu.VMEM((2, 2, rows * SUBLANES, LANES), F32), pltpu.SemaphoreType.DMA((2,))],
    )
    return pl.pallas_call(
        functools.partial(_combine_kernel, final_norm=final_norm),
        grid_spec=grid_spec,
        out_shape=jax.ShapeDtypeStruct((n, d), F32),
        compiler_params=_params("arbitrary"),
        name="combine",
    )(pos, ys, x1, rt, g)


def _dispatch_plan(counts, n):
    rows = EXPERT_ROWS
    max_tiles = (2 * n) // rows + N_EXPERTS
    cnt = counts[:N_EXPERTS, 0].astype(jnp.int32)
    tiles = (cnt + rows - 1) // rows
    tile_end = jnp.cumsum(tiles)
    tile_start = tile_end - tiles
    n_tiles = tile_end[-1:]
    pad_tile = jnp.where(cnt % rows != 0, tile_end - 1, -1).astype(jnp.int32)
    offs_b = jnp.broadcast_to((tile_start * rows).astype(F32)[:, None], (N_EXPERTS, DISPATCH_ROWS))
    return tile_start.astype(jnp.int32), tiles, n_tiles, pad_tile, offs_b, max_tiles * rows


def _pad_lanes(a, width):
    return jnp.pad(a, ((0, 0), (0, width - a.shape[1])))


def kernel(x, norm_mix_g, w_in, w_g2_f, b_g_f, w_g2_b, b_g_b, gla_norm_g, rpb, w_out, norm_ffn_g, w_grp, b_grp,
           w_exp, b_exp, w_gate, w_up, w_down, final_norm_g):
    batch, seq, d = x.shape
    n = batch * seq
    rows = seq // GRID_W
    depth = w_in.shape[0]
    xf = x.reshape(n, d)
    c_na = 3 * NA_WIDTH
    c_qk = c_na + 2 * GLA_KEY_WIDTH
    c_v = c_qk + GLA_VAL_WIDTH
    c_r = c_v + GLA_VAL_WIDTH
    for l in range(depth):
        wl = w_in[l]
        q_scale = jnp.concatenate([jnp.full((NA_WIDTH,), NA_HEAD_DIM ** -0.5, F32),
                                   jnp.ones((2 * NA_WIDTH,), F32)])
        w_na = (wl[:, :c_na] * q_scale).astype(BF16)
        qk_scale = jnp.concatenate([jnp.full((GLA_KEY_WIDTH,), GLA_DK ** -0.5, F32),
                                    jnp.ones((GLA_KEY_WIDTH,), F32)])
        w_qk = (wl[:, c_na:c_qk] * qk_scale).astype(BF16)
        w_v = wl[:, c_qk:c_v].astype(BF16)
        w_r = wl[:, c_v:c_r].astype(BF16)
        w_lr = _pad_lanes(wl[:, c_r:], LANES)
        w_g2 = jnp.zeros((LANES, 2 * GLA_KEY_WIDTH), F32)
        w_g2 = w_g2.at[:GLA_GATE_RANK, :GLA_KEY_WIDTH].set(w_g2_f[l])
        w_g2 = w_g2.at[GLA_GATE_RANK:2 * GLA_GATE_RANK, GLA_KEY_WIDTH:].set(w_g2_b[l])
        b_g = jnp.concatenate([b_g_f[l], b_g_b[l]])[None, :]
        na_qkv, gqk, gv, gr, gates = _in_proj(xf, norm_mix_g[l][None, :], w_na, w_qk, w_v, w_r, w_lr, w_g2, b_g)

        y_na = _na(na_qkv, _na_bias_table(rpb[l], rows), batch, rows)
        o_f, o_b = _gla(gqk, gv, gates, batch, seq)

        w_o = w_out[l].astype(BF16)
        w_rt = _pad_lanes(jnp.concatenate([w_exp[l], w_grp[l]], axis=1), LANES)
        w_rt_hi = w_rt.astype(BF16)
        w_rt_lo = (w_rt - w_rt_hi.astype(F32)).astype(BF16)
        w_rt3 = jnp.concatenate([w_rt_hi, w_rt_hi, w_rt_lo], axis=0).T
        b_rt = _pad_lanes(jnp.concatenate([b_exp[l], b_grp[l]])[None, :], ROUTE_ROWS).T
        x1, h2_slabs, rt_t, counts = _out_route(y_na, o_f, o_b, gr, xf, w_o[:NA_WIDTH], w_o[NA_WIDTH:],
                                                gla_norm_g[l][None, :], norm_ffn_g[l][None, :], w_rt3, b_rt)

        tile_start, tiles, n_tiles, pad_tile, offs_b, sorted_rows = _dispatch_plan(counts, n)
        pos, rt, xs = _dispatch(pad_tile, n_tiles, rt_t, offs_b, h2_slabs, sorted_rows)
        ys = _experts(tile_start, tiles, xs, w_gate, w_up, w_down, l)
        pos = pos.reshape(-1)
        last = l == depth - 1
        xf = _combine(pos, ys, x1, rt, final_norm_g[None, :], final_norm=last)
    return xf.reshape(batch, seq, d)
```

```python
import functools

import jax
import jax.numpy as jnp
from jax import lax
from jax.experimental import pallas as pl
from jax.experimental.pallas import tpu as pltpu

F32 = jnp.float32
BF16 = jnp.bfloat16

GRID_W = 64
NA_HEADS = 8
NA_HEAD_DIM = 64
NA_WIDTH = NA_HEADS * NA_HEAD_DIM
WIN_H_MAX = 8
WIN_W = 16
GLA_HEADS = 4
GLA_DK = 64
GLA_DV = 128
GLA_KEY_WIDTH = GLA_HEADS * GLA_DK
GLA_VAL_WIDTH = GLA_HEADS * GLA_DV
GLA_GATE_RANK = 16
GLA_GATE_NORMALIZER = 16.0
GLA_CHUNK = 64
N_GROUPS = 4
EXPERTS_PER_GROUP = 8
N_EXPERTS = N_GROUPS * EXPERTS_PER_GROUP
RMS_EPS = 1e-6

LANES = 128
SUBLANES = 8
MXU_WIDTH = 256
VMEM_LIMIT_BYTES = 56 * 1024 * 1024

MASK_VALUE = -1e30

IN_PROJ_ROWS = 512
NA_ROW_BLOCK = 8
NA_ROW_UNROLL = 4
GLA_STEP_CHUNKS = 4
OUT_ROWS = 512
EXPERT_ROWS = 256
EXPERT_TILE_BUFFERS = 4
DISPATCH_ROWS = 512
COMBINE_ROWS = DISPATCH_ROWS
GATHER_UNROLL = 64


def _params(*sem):
    return pltpu.CompilerParams(dimension_semantics=sem, vmem_limit_bytes=VMEM_LIMIT_BYTES)


def _rms(x, g):
    return x * lax.rsqrt(jnp.mean(x * x, axis=-1, keepdims=True) + RMS_EPS) * g


def _in_proj_kernel(x_ref, g_ref, w_na_ref, w_qk_ref, w_v_ref, w_r_ref, w_lr_ref,
                    w_g2_ref, b_g_ref, na_ref, qk_ref, v_ref, r_ref, gate_ref, w_gate):
    @pl.when(pl.program_id(0) == 0)
    def _():
        w_gate[...] = jnp.dot(w_lr_ref[...], w_g2_ref[...], preferred_element_type=F32,
                              precision=lax.Precision.HIGHEST).astype(BF16)

    h = _rms(x_ref[...], g_ref[...]).astype(BF16)
    na_ref[...] = jnp.dot(h, w_na_ref[...], preferred_element_type=F32).astype(BF16)
    qk_ref[...] = jnp.dot(h, w_qk_ref[...], preferred_element_type=F32)
    v_ref[...] = jnp.dot(h, w_v_ref[...], preferred_element_type=F32).astype(BF16)
    r_ref[...] = jnp.dot(h, w_r_ref[...], preferred_element_type=F32)
    z = jnp.dot(h, w_gate[...], preferred_element_type=F32) + b_g_ref[...]
    log_sig = jnp.minimum(z, 0.0) - jnp.log(1.0 + jnp.exp(-jnp.abs(z)))
    gate_ref[...] = log_sig * (1.0 / GLA_GATE_NORMALIZER)


def _in_proj(x, g, w_na, w_qk, w_v, w_r, w_lr, w_g2, b_g):
    n, d = x.shape
    rows = IN_PROJ_ROWS
    row_spec = lambda width: pl.BlockSpec((rows, width), lambda i: (i, 0))
    full = lambda a: pl.BlockSpec(a.shape, lambda i: (0,) * a.ndim)
    return pl.pallas_call(
        _in_proj_kernel,
        grid=(n // rows,),
        in_specs=[row_spec(d), full(g), full(w_na), full(w_qk), full(w_v), full(w_r), full(w_lr),
                  full(w_g2), full(b_g)],
        out_specs=[row_spec(3 * NA_WIDTH), row_spec(2 * GLA_KEY_WIDTH), row_spec(GLA_VAL_WIDTH),
                   row_spec(GLA_VAL_WIDTH), row_spec(2 * GLA_KEY_WIDTH)],
        out_shape=[jax.ShapeDtypeStruct((n, 3 * NA_WIDTH), BF16),
                   jax.ShapeDtypeStruct((n, 2 * GLA_KEY_WIDTH), F32),
                   jax.ShapeDtypeStruct((n, GLA_VAL_WIDTH), BF16),
                   jax.ShapeDtypeStruct((n, GLA_VAL_WIDTH), F32),
                   jax.ShapeDtypeStruct((n, 2 * GLA_KEY_WIDTH), F32)],
        scratch_shapes=[pltpu.VMEM((d, 2 * GLA_KEY_WIDTH), BF16)],
        compiler_params=_params("arbitrary"),
        name="in_proj",
    )(x, g, w_na, w_qk, w_v, w_r, w_lr, w_g2, b_g)


def _na_bias_table(rpb, rows):
    kh = min(WIN_H_MAX, rows)
    w = jnp.arange(GRID_W)[:, None, None]
    x = jnp.arange(GRID_W)[None, :, None]
    cs = jnp.clip(w - WIN_W // 2, 0, GRID_W - WIN_W)
    valid = (x >= cs) & (x < cs + WIN_W)
    col_sel = (valid & (jnp.arange(2 * WIN_W - 1)[None, None, :] == x - w + (WIN_W - 1))).astype(F32)
    by_col = jnp.einsum("hrc,wxc->hwrx", rpb.astype(F32), col_sel, precision=lax.Precision.HIGHEST)
    by_col = jnp.where(valid[None, :, None, :, 0], by_col, MASK_VALUE)
    by_col = by_col.reshape(NA_HEADS * GRID_W, (2 * WIN_H_MAX - 1) * GRID_W)
    bias = jnp.stack([by_col[:, (WIN_H_MAX - 1 - c) * GRID_W:(WIN_H_MAX - 1 - c + kh) * GRID_W]
                      for c in range(kh)])
    return bias.reshape(kh, NA_HEADS // 2, 2 * GRID_W, kh * GRID_W)


def _na_kernel(q_ref, kp_ref, kc_ref, kn_ref, vp_ref, vc_ref, vn_ref, tbl_ref, o_ref,
               kbuf, vbuf, *, rows, kh):
    rb = NA_ROW_BLOCK
    j = pl.program_id(1)
    blk = rb * GRID_W
    for s, (k_src, v_src) in enumerate(((kp_ref, vp_ref), (kc_ref, vc_ref), (kn_ref, vn_ref))):
        kbuf[s * blk:(s + 1) * blk, :] = k_src[...].reshape(blk, NA_WIDTH)
        vbuf[s * blk:(s + 1) * blk, :] = v_src[...].reshape(blk, NA_WIDTH)
    lane = lax.broadcasted_iota(jnp.int32, (GRID_W, LANES), 1)
    first = lane < NA_HEAD_DIM

    def row_body(lr, carry):
        r = j * rb + lr
        start = jnp.clip(r - kh // 2, 0, rows - kh)
        cls = r - start
        local = pl.multiple_of((start - (j - 1) * rb) * GRID_W, GRID_W)
        q_row = q_ref[lr]
        k_win = kbuf[pl.ds(local, kh * GRID_W), :]
        v_win = vbuf[pl.ds(local, kh * GRID_W), :]
        pairs = [slice(p * LANES, (p + 1) * LANES) for p in range(NA_HEADS // 2)]
        scores = []
        for sl in pairs:
            q_pair = q_row[:, sl]
            zero = jnp.zeros_like(q_pair)
            q_bd = jnp.concatenate([jnp.where(first, q_pair, zero), jnp.where(first, zero, q_pair)], axis=0)
            scores.append(lax.dot_general(q_bd, k_win[:, sl], (((1,), (1,)), ((), ())),
                                          preferred_element_type=F32))
        probs, denoms = [], []
        for p, s in enumerate(scores):
            s = s + tbl_ref[cls, p]
            e = jnp.exp(s - jnp.max(s, axis=-1, keepdims=True))
            denoms.append(jnp.sum(e, axis=-1, keepdims=True))
            probs.append(e.astype(BF16))
        outs = []
        for sl, e, denom in zip(pairs, probs, denoms):
            o = jnp.dot(e, v_win[:, sl], preferred_element_type=F32) / denom
            outs.append(jnp.where(first, o[:GRID_W], o[GRID_W:]))
        o_ref[lr] = jnp.concatenate(outs, axis=-1).astype(o_ref.dtype)
        return carry

    lax.fori_loop(0, rb, row_body, 0, unroll=NA_ROW_UNROLL)


def _na(na_qkv, tbl, batch, rows):
    kh = min(WIN_H_MAX, rows)
    rb = NA_ROW_BLOCK
    nblk = rows // rb
    x4 = na_qkv.reshape(batch, rows, GRID_W, 3 * NA_WIDTH)
    blk = (None, rb, GRID_W, NA_WIDTH)
    prev = lambda j: jnp.maximum(j - 1, 0)
    nxt = lambda j: jnp.minimum(j + 1, nblk - 1)
    specs = [pl.BlockSpec(blk, lambda b, j: (b, j, 0, 0))]
    for col in (1, 2):
        specs += [pl.BlockSpec(blk, lambda b, j, col=col: (b, prev(j), 0, col)),
                  pl.BlockSpec(blk, lambda b, j, col=col: (b, j, 0, col)),
                  pl.BlockSpec(blk, lambda b, j, col=col: (b, nxt(j), 0, col))]
    specs.append(pl.BlockSpec(tbl.shape, lambda b, j: (0, 0, 0, 0)))
    out = pl.pallas_call(
        functools.partial(_na_kernel, rows=rows, kh=kh),
        grid=(batch, nblk),
        in_specs=specs,
        out_specs=pl.BlockSpec(blk, lambda b, j: (b, j, 0, 0)),
        out_shape=jax.ShapeDtypeStruct((batch, rows, GRID_W, NA_WIDTH), BF16),
        scratch_shapes=[pltpu.VMEM((3 * rb * GRID_W, NA_WIDTH), BF16),
                        pltpu.VMEM((3 * rb * GRID_W, NA_WIDTH), BF16)],
        compiler_params=_params("arbitrary", "arbitrary"),
        name="na",
    )(x4, x4, x4, x4, x4, x4, x4, tbl)
    return out.reshape(batch * rows * GRID_W, NA_WIDTH)


def _block_diag_mask(row_block, col_block, nblocks):
    shape = (row_block * nblocks, col_block * nblocks)
    r = lax.shift_right_logical(lax.broadcasted_iota(jnp.int32, shape, 0), row_block.bit_length() - 1)
    c = lax.shift_right_logical(lax.broadcasted_iota(jnp.int32, shape, 1), col_block.bit_length() - 1)
    return r == c


def _split_bf16x3(x):
    hi = x.astype(BF16)
    rest = x - hi.astype(F32)
    mid = rest.astype(BF16)
    lo = (rest - mid.astype(F32)).astype(BF16)
    return hi, mid, lo


def _gla_direction(qk_ref, v_ref, g_ref, *, backward):
    c = GLA_CHUNK
    nc = GLA_STEP_CHUNKS
    kw = GLA_KEY_WIDTH
    order = list(reversed(range(nc))) if backward else list(range(nc))
    chunk = lambda a, n: a[n * c:(n + 1) * c]

    def stage_cumsum():
        step = nc * c
        ti = lax.broadcasted_iota(jnp.int32, (step, step), 0)
        tj = lax.broadcasted_iota(jnp.int32, (step, step), 1)
        same_chunk = lax.shift_right_logical(ti, c.bit_length() - 1) == lax.shift_right_logical(tj, c.bit_length() - 1)
        tri = (tj >= ti) if backward else (tj <= ti)
        cum = jnp.where(same_chunk & tri, 1.0, 0.0).astype(BF16)
        return sum(jnp.dot(cum, piece, preferred_element_type=F32) for piece in _split_bf16x3(g_ref[...]))

    def stage_decays(b):
        ref_row, last_row = (c // 2, 0) if backward else (c // 2 - 1, c - 1)
        rows_of = lambda r: jnp.concatenate(
            [jnp.broadcast_to(b[n * c + r:n * c + r + 1, :], (c, kw)) for n in range(nc)], axis=0)
        b_ref, b_last = rows_of(ref_row), rows_of(last_row)
        q = qk_ref[:, :kw]
        k = qk_ref[:, kw:]
        q_rel = (q * jnp.exp(b - b_ref)).astype(BF16)
        k_rel = (k * jnp.exp(b_ref - b)).astype(BF16)
        k_dec = k * jnp.exp(b_last - b)
        q_dec = (q * jnp.exp(b)).astype(BF16)
        decay = jnp.exp(b_last)
        return q_rel, k_rel, k_dec, q_dec, decay

    def stage_scores(q_rel, k_rel):
        kk_mask = _block_diag_mask(c, GLA_DK, GLA_HEADS)
        out = []
        for n in range(nc):
            k_bd = jnp.where(kk_mask, jnp.concatenate([chunk(k_rel, n)] * GLA_HEADS, axis=0), jnp.zeros((), BF16))
            out.append(lax.dot_general(chunk(q_rel, n), k_bd, (((1,), (1,)), ((), ())),
                                       preferred_element_type=F32))
        return out

    def stage_intra(scores, k_dec, decay):
        si = lax.broadcasted_iota(jnp.int32, (c, c * GLA_HEADS), 0)
        sj = lax.broadcasted_iota(jnp.int32, (c, c * GLA_HEADS), 1) & (c - 1)
        keep = (sj > si) if backward else (sj <= si)
        kv_mask = _block_diag_mask(c, GLA_DV, GLA_HEADS)
        vk_mask = _block_diag_mask(GLA_DK, GLA_DV, 2)
        o_intra, upd, decay_col = [], [], []
        for n in range(nc):
            v = v_ref[n * c:(n + 1) * c, :]
            p = jnp.where(keep, scores[n], 0.0).astype(BF16)
            v_bd = jnp.where(kv_mask, jnp.concatenate([v] * GLA_HEADS, axis=0), jnp.zeros((), BF16))
            o_intra.append(jnp.dot(p, v_bd, preferred_element_type=F32))
            k_dec_t = chunk(k_dec, n).T.astype(BF16)
            pairs = []
            for hp in range(GLA_HEADS // 2):
                kv = jnp.dot(k_dec_t[hp * 2 * GLA_DK:(hp + 1) * 2 * GLA_DK],
                             v[:, hp * 2 * GLA_DV:(hp + 1) * 2 * GLA_DV], preferred_element_type=F32)
                pairs.append(jnp.where(vk_mask, kv, 0.0))
            upd.append(pairs)
            decay_col.append(chunk(decay, n).T[:, :1])
        return o_intra, upd, decay_col

    def scan_step(state, idx, q_dec, o_intra, upd, decay_col, out_ref):
        n = order[idx]
        q_n = chunk(q_dec, n)
        pw = 2 * GLA_DK
        o_inter = jnp.concatenate(
            [jnp.dot(q_n[:, hp * pw:(hp + 1) * pw], s.astype(BF16), preferred_element_type=F32)
             for hp, s in enumerate(state)], axis=1)
        out_ref[n * c:(n + 1) * c, :] = o_intra[n] + o_inter
        return [s * decay_col[n][hp * pw:(hp + 1) * pw] + u for hp, (s, u) in enumerate(zip(state, upd[n]))]

    return stage_cumsum, stage_decays, stage_scores, stage_intra, scan_step


def _gla_kernel(qk_f_ref, v_f_ref, g_f_ref, qk_b_ref, v_b_ref, g_b_ref, o_f_ref, o_b_ref,
                st_f, st_b):
    @pl.when(pl.program_id(1) == 0)
    def _():
        st_f[...] = jnp.zeros_like(st_f)
        st_b[...] = jnp.zeros_like(st_b)

    dirs = (_gla_direction(qk_f_ref, v_f_ref, g_f_ref, backward=False),
            _gla_direction(qk_b_ref, v_b_ref, g_b_ref, backward=True))
    b = [d[0]() for d in dirs]
    dec = [d[1](x) for d, x in zip(dirs, b)]
    scores = [d[2](x[0], x[1]) for d, x in zip(dirs, dec)]
    intra = [d[3](s, x[2], x[4]) for d, s, x in zip(dirs, scores, dec)]
    pairs = range(GLA_HEADS // 2)
    states = [[st[hp] for hp in pairs] for st in (st_f, st_b)]
    for idx in range(GLA_STEP_CHUNKS):
        for j, (d, x, y, out_ref) in enumerate(zip(dirs, dec, intra, (o_f_ref, o_b_ref))):
            states[j] = d[4](states[j], idx, x[3], *y, out_ref)
    for st, state in zip((st_f, st_b), states):
        for hp in pairs:
            st[hp] = state[hp]


def _gla(qk, v, gates, batch, seq):
    step = GLA_STEP_CHUNKS * GLA_CHUNK
    nblk = seq // step
    qk3 = qk.reshape(batch, seq, 2 * GLA_KEY_WIDTH)
    v3 = v.reshape(batch, seq, GLA_VAL_WIDTH)
    g3 = gates.reshape(batch, seq, 2 * GLA_KEY_WIDTH)
    fwd = lambda b, n: (b, n, 0)
    bwd = lambda b, n: (b, nblk - 1 - n, 0)
    bwd_gate = lambda b, n: (b, nblk - 1 - n, 1)
    o_f, o_b = pl.pallas_call(
        _gla_kernel,
        grid=(batch, nblk),
        in_specs=[pl.BlockSpec((None, step, 2 * GLA_KEY_WIDTH), fwd),
                  pl.BlockSpec((None, step, GLA_VAL_WIDTH), fwd),
                  pl.BlockSpec((None, step, GLA_KEY_WIDTH), fwd),
                  pl.BlockSpec((None, step, 2 * GLA_KEY_WIDTH), bwd),
                  pl.BlockSpec((None, step, GLA_VAL_WIDTH), bwd),
                  pl.BlockSpec((None, step, GLA_KEY_WIDTH), bwd_gate)],
        out_specs=[pl.BlockSpec((None, step, GLA_VAL_WIDTH), fwd),
                   pl.BlockSpec((None, step, GLA_VAL_WIDTH), bwd)],
        out_shape=[jax.ShapeDtypeStruct((batch, seq, GLA_VAL_WIDTH), F32)] * 2,
        scratch_shapes=[pltpu.VMEM((GLA_HEADS // 2, 2 * GLA_DK, 2 * GLA_DV), F32),
                        pltpu.VMEM((GLA_HEADS // 2, 2 * GLA_DK, 2 * GLA_DV), F32)],
        compiler_params=_params("arbitrary", "arbitrary"),
        name="gla",
    )(qk3, v3, g3, qk3, v3, g3)
    return o_f.reshape(batch * seq, GLA_VAL_WIDTH), o_b.reshape(batch * seq, GLA_VAL_WIDTH)


def _store_slabs(ref, x):
    rows = x.shape[0]
    for s in range(SUBLANES):
        ref[pl.ds(s, rows, stride=SUBLANES), :] = x[:, s * LANES:(s + 1) * LANES]


def _load_slabs(ref, rows):
    return jnp.concatenate([ref[pl.ds(s, rows, stride=SUBLANES), :] for s in range(SUBLANES)], axis=1)


def _slab_copy(src_hbm, dst, sem, src_row, k):
    dst_row = k * SUBLANES if isinstance(k, int) else pl.multiple_of(k * SUBLANES, SUBLANES)
    return pltpu.make_async_copy(src_hbm.at[pl.ds(pl.multiple_of(src_row, SUBLANES), SUBLANES), :],
                                 dst.at[pl.ds(dst_row, SUBLANES), :], sem)


def _wait_slabs(src_hbm, dst, sem):
    pltpu.make_async_copy(src_hbm.at[pl.ds(0, dst.shape[0]), :], dst, sem).wait()


RT_E1, RT_E2, RT_RANK1, RT_RANK2, RT_W1, RT_W2 = range(6)
GROUP_ROW0 = N_EXPERTS
ROUTE_ROWS = 64


def _out_route_kernel(na_ref, of_ref, ob_ref, r_ref, x_ref, w_na_ref, w_gla_ref, gn_ref, fn_ref,
                      w_rt_ref, b_rt_ref, x1_ref, h2_ref, rt_ref, cnt_ref, carry):
    @pl.when(pl.program_id(0) == 0)
    def _():
        carry[...] = jnp.zeros_like(carry)

    rows = x_ref.shape[0]
    o = of_ref[...] + ob_ref[...]
    r = r_ref[...]
    parts = []
    for h in range(GLA_HEADS):
        sl = slice(h * GLA_DV, (h + 1) * GLA_DV)
        parts.append(_rms(o[:, sl], gn_ref[...]) * (r[:, sl] * jax.nn.sigmoid(r[:, sl])))
    y_gla = jnp.concatenate(parts, axis=-1).astype(BF16)
    x1 = (x_ref[...] + jnp.dot(na_ref[...], w_na_ref[...], preferred_element_type=F32)
          + jnp.dot(y_gla, w_gla_ref[...], preferred_element_type=F32))
    x1_ref[...] = x1
    h2 = _rms(x1, fn_ref[...])
    _store_slabs(h2_ref, h2)

    h_hi = h2.astype(BF16)
    h_lo = (h2 - h_hi.astype(F32)).astype(BF16)
    logits = lax.dot_general(w_rt_ref[...], jnp.concatenate([h_hi, h_lo, h_hi], axis=1),
                             (((1,), (1,)), ((), ())), preferred_element_type=F32)
    logits = logits[:ROUTE_ROWS] + b_rt_ref[...]
    row_i = lax.broadcasted_iota(jnp.int32, (ROUTE_ROWS, rows), 0)
    row = row_i.astype(F32)
    row_grp = lax.shift_right_logical(row_i, EXPERTS_PER_GROUP.bit_length() - 1).astype(F32)
    neg = jnp.float32(-jnp.inf)
    no_row = jnp.float32(ROUTE_ROWS)
    is_grp = (row_i >= GROUP_ROW0) & (row_i < GROUP_ROW0 + N_GROUPS)
    g_logit = jnp.where(is_grp, logits, neg)
    g_max = jnp.max(g_logit, axis=0, keepdims=True)
    g_sel = jnp.min(jnp.where(is_grp & (g_logit == g_max), row, no_row), axis=0, keepdims=True) - GROUP_ROW0
    grp_w = 1.0 / jnp.sum(jnp.where(is_grp, jnp.exp(g_logit - g_max), 0.0), axis=0, keepdims=True)
    in_grp = (row_i < N_EXPERTS) & (row_grp == g_sel)
    e_logit = jnp.where(in_grp, logits, neg)
    v1 = jnp.max(e_logit, axis=0, keepdims=True)
    i1 = jnp.min(jnp.where(in_grp & (e_logit == v1), row, no_row), axis=0, keepdims=True)
    rest = in_grp & (row != i1)
    e_logit2 = jnp.where(rest, logits, neg)
    v2 = jnp.max(e_logit2, axis=0, keepdims=True)
    i2 = jnp.min(jnp.where(rest & (e_logit2 == v2), row, no_row), axis=0, keepdims=True)
    t = jnp.exp(v2 - v1)
    w1 = grp_w / (1.0 + t)
    w2 = grp_w * t / (1.0 + t)

    sel1 = row == i1
    sel2 = row == i2
    onehot = jnp.where(sel1 | sel2, 1.0, 0.0)
    ti = lax.broadcasted_iota(jnp.int32, (rows, rows), 0)
    tj = lax.broadcasted_iota(jnp.int32, (rows, rows), 1)
    earlier = jnp.where(ti < tj, 1.0, 0.0).astype(BF16)
    ranks = jnp.dot(onehot.astype(BF16), earlier, preferred_element_type=F32) + carry[:, :1]
    rank1 = jnp.sum(jnp.where(sel1, ranks, 0.0), axis=0, keepdims=True)
    rank2 = jnp.sum(jnp.where(sel2, ranks, 0.0), axis=0, keepdims=True)
    new_carry = carry[...] + jnp.sum(onehot, axis=1, keepdims=True)
    carry[...] = new_carry
    cnt_ref[...] = new_carry

    fields = {RT_E1: i1, RT_E2: i2, RT_RANK1: rank1, RT_RANK2: rank2, RT_W1: w1, RT_W2: w2}
    zero = jnp.zeros_like(w1)
    rt_ref[...] = jnp.concatenate([fields.get(f, zero) for f in range(SUBLANES)], axis=0)


def _out_route(y_na, o_f, o_b, r, x, w_na, w_gla, gn, fn, w_rt, b_rt):
    n, d = x.shape
    rows = OUT_ROWS
    row_spec = lambda width: pl.BlockSpec((rows, width), lambda i: (i, 0))
    full = lambda a: pl.BlockSpec(a.shape, lambda i: (0,) * a.ndim)
    return pl.pallas_call(
        _out_route_kernel,
        grid=(n // rows,),
        in_specs=[row_spec(NA_WIDTH), row_spec(GLA_VAL_WIDTH), row_spec(GLA_VAL_WIDTH),
                  row_spec(GLA_VAL_WIDTH), row_spec(d), full(w_na), full(w_gla), full(gn), full(fn),
                  full(w_rt), full(b_rt)],
        out_specs=[row_spec(d), pl.BlockSpec((rows * SUBLANES, LANES), lambda i: (i, 0)),
                   pl.BlockSpec((SUBLANES, rows), lambda i: (0, i)),
                   pl.BlockSpec((ROUTE_ROWS, LANES), lambda i: (0, 0))],
        out_shape=[jax.ShapeDtypeStruct((n, d), F32), jax.ShapeDtypeStruct((n * SUBLANES, LANES), F32),
                   jax.ShapeDtypeStruct((SUBLANES, n), F32), jax.ShapeDtypeStruct((ROUTE_ROWS, LANES), F32)],
        scratch_shapes=[pltpu.VMEM((ROUTE_ROWS, LANES), F32)],
        compiler_params=_params("arbitrary"),
        name="out_route",
    )(y_na, o_f, o_b, r, x, w_na, w_gla, gn, fn, w_rt, b_rt)


def _dispatch_kernel(pad_tile_ref, n_tiles_ref, rt_ref, offs_ref, h_hbm, pos_ref, rec_ref, xs_hbm,
                     zeros, hbuf, pos_vmem, pos_smem, sem_zero, sem_pos, sem_in, sem):
    i = pl.program_id(0)
    n_steps = pl.num_programs(0)
    db = DISPATCH_ROWS
    tile = EXPERT_ROWS * SUBLANES
    max_tiles = xs_hbm.shape[0] // tile
    buf = lax.rem(i, 2)

    def block_copy(step, b):
        rows = pl.ds(pl.multiple_of(step * db * SUBLANES, db * SUBLANES), db * SUBLANES)
        return pltpu.make_async_copy(h_hbm.at[rows, :], hbuf.at[b], sem_in.at[b])

    def wait_row_copies(b):
        for _ in range(2):
            pltpu.make_async_copy(hbuf.at[b], xs_hbm.at[pl.ds(0, db * SUBLANES), :], sem).wait()

    def zero_copy(t):
        return pltpu.make_async_copy(zeros, xs_hbm.at[pl.ds(pl.multiple_of(t * tile, tile), tile), :], sem_zero)

    def for_each_zero_tile(fn):
        def padded(e, carry):
            @pl.when(pad_tile_ref[e] >= 0)
            def _():
                fn(zero_copy(pad_tile_ref[e]))
            return carry
        lax.fori_loop(0, N_EXPERTS, padded, 0)

        def tail(t, carry):
            fn(zero_copy(t))
            return carry
        lax.fori_loop(n_tiles_ref[0], max_tiles, tail, 0)

    @pl.when(i == 0)
    def _():
        block_copy(0, 0).start()
        zeros[...] = jnp.zeros_like(zeros)
        for_each_zero_tile(lambda cp: cp.start())

    rt_t = rt_ref[...]
    rec_ref[...] = jnp.concatenate([rt_t, jnp.zeros((LANES - SUBLANES, db), F32)], axis=0).T
    expert = lax.broadcasted_iota(jnp.int32, (N_EXPERTS, db), 0).astype(F32)
    pos = []
    for e_lane, rank_lane in ((RT_E1, RT_RANK1), (RT_E2, RT_RANK2)):
        seg = jnp.sum(jnp.where(expert == rt_t[e_lane:e_lane + 1, :], offs_ref[...], 0.0), axis=0, keepdims=True)
        pos.append((seg + rt_t[rank_lane:rank_lane + 1, :]) * SUBLANES)
    pos = jnp.concatenate(pos, axis=0).astype(jnp.int32)
    pos_ref[...] = pos
    pos_vmem[...] = jnp.concatenate([pos, jnp.zeros((SUBLANES - 2, db), jnp.int32)], axis=0)
    to_smem = pltpu.make_async_copy(pos_vmem, pos_smem, sem_pos)
    to_smem.start()

    @pl.when(i == 0)
    def _():
        for_each_zero_tile(lambda cp: cp.wait())

    @pl.when(i > 0)
    def _():
        wait_row_copies(1 - buf)

    to_smem.wait()
    block_copy(i, buf).wait()

    @pl.when(i + 1 < n_steps)
    def _():
        block_copy(i + 1, 1 - buf).start()

    h_blk = hbuf.at[buf]
    for k in range(db):
        src = h_blk.at[pl.ds(k * SUBLANES, SUBLANES), :]
        for slot in range(2):
            dst_row = pl.multiple_of(pos_smem[slot, k], SUBLANES)
            pltpu.make_async_copy(src, xs_hbm.at[pl.ds(dst_row, SUBLANES), :], sem).start(priority=slot)

    @pl.when(i == n_steps - 1)
    def _():
        wait_row_copies(buf)


def _dispatch(pad_tile, n_tiles, rt_t, offs_b, h2_slabs, sorted_rows):
    n = rt_t.shape[1]
    db = DISPATCH_ROWS
    grid_spec = pltpu.PrefetchScalarGridSpec(
        num_scalar_prefetch=2,
        grid=(n // db,),
        in_specs=[pl.BlockSpec((SUBLANES, db), lambda i, pt, nt: (0, i)),
                  pl.BlockSpec(offs_b.shape, lambda i, pt, nt: (0, 0)),
                  pl.BlockSpec(memory_space=pl.ANY)],
        out_specs=[pl.BlockSpec((None, 2, db), lambda i, pt, nt: (i, 0, 0)),
                   pl.BlockSpec((db, LANES), lambda i, pt, nt: (i, 0)),
                   pl.BlockSpec(memory_space=pl.ANY)],
        scratch_shapes=[pltpu.VMEM((EXPERT_ROWS * SUBLANES, LANES), F32),
                        pltpu.VMEM((2, db * SUBLANES, LANES), F32),
                        pltpu.VMEM((SUBLANES, db), jnp.int32), pltpu.SMEM((SUBLANES, db), jnp.int32),
                        pltpu.SemaphoreType.DMA(()), pltpu.SemaphoreType.DMA(()),
                        pltpu.SemaphoreType.DMA((2,)), pltpu.SemaphoreType.DMA(())],
    )
    return pl.pallas_call(
        _dispatch_kernel,
        grid_spec=grid_spec,
        out_shape=[jax.ShapeDtypeStruct((n // db, 2, db), jnp.int32),
                   jax.ShapeDtypeStruct((n, LANES), F32),
                   jax.ShapeDtypeStruct((sorted_rows * SUBLANES, LANES), F32)],
        compiler_params=_params("arbitrary"),
        name="dispatch",
    )(pad_tile, n_tiles, rt_t, offs_b, h2_slabs)


TILE_DMA_PRIORITY = 1


def _expert_tile(x_ref, y_ref, wg_s, wu_s, wd_s):
    rows = EXPERT_ROWS
    x = _load_slabs(x_ref, rows).astype(BF16)
    hid = []
    for c in range(0, wg_s.shape[1], MXU_WIDTH):
        gate = jnp.dot(x, wg_s[:, c:c + MXU_WIDTH], preferred_element_type=F32)
        up = jnp.dot(x, wu_s[:, c:c + MXU_WIDTH], preferred_element_type=F32)
        hid.append((gate * jax.nn.sigmoid(gate) * up).astype(BF16))
    hid = jnp.concatenate(hid, axis=1)
    for c in range(0, wd_s.shape[1], MXU_WIDTH):
        y = jnp.dot(hid, wd_s[:, c:c + MXU_WIDTH], preferred_element_type=F32)
        for s in range(c // LANES, (c + MXU_WIDTH) // LANES):
            y_ref[pl.ds(s, rows, stride=SUBLANES), :] = y[:, s * LANES - c:(s + 1) * LANES - c]


def _experts_kernel(tile_start_ref, tiles_ref, x_hbm, wg_hbm, wu_hbm, wd_hbm, y_hbm,
                    xbuf, ybuf, wg_f, wu_f, wd_f, wg_s, wu_s, wd_s, sem_x, sem_y, sem_w, *, layer):
    nbuf = EXPERT_TILE_BUFFERS
    e = pl.program_id(0)
    last_step = e == pl.num_programs(0) - 1
    tile = EXPERT_ROWS * SUBLANES
    n = tiles_ref[e]
    base = tile_start_ref[e]
    n_tiles = tile_start_ref[N_EXPERTS - 1] + tiles_ref[N_EXPERTS - 1]
    tile_rows = lambda g: pl.ds(pl.multiple_of(g * tile, tile), tile)

    def x_copy(g, slot):
        return pltpu.make_async_copy(x_hbm.at[tile_rows(g), :], xbuf.at[slot], sem_x.at[slot])

    def y_copy(g, slot):
        return pltpu.make_async_copy(ybuf.at[slot], y_hbm.at[tile_rows(g), :], sem_y.at[slot])

    def w_copies(expert, slot):
        return [pltpu.make_async_copy(w_hbm.at[layer, expert], stage.at[slot], sem_w.at[slot])
                for w_hbm, stage in ((wg_hbm, wg_f), (wu_hbm, wu_f), (wd_hbm, wd_f))]

    w_slot = lax.rem(e, 2)

    @pl.when(e == 0)
    def _():
        for g in range(nbuf - 1):
            @pl.when(g < n_tiles)
            def _():
                x_copy(g, g).start(priority=TILE_DMA_PRIORITY)
        for cp in w_copies(0, 0):
            cp.start()

    for cp in w_copies(e, w_slot):
        cp.wait()

    @pl.when(jnp.logical_not(last_step))
    def _():
        for cp in w_copies(e + 1, 1 - w_slot):
            cp.start()

    @pl.when(n > 0)
    def _():
        wg_s[...] = wg_f[w_slot].astype(BF16)
        wu_s[...] = wu_f[w_slot].astype(BF16)
        wd_s[...] = wd_f[w_slot].astype(BF16)

        def tile_body(g, carry):
            slot = lax.rem(g, nbuf)
            x_copy(g, slot).wait()

            @pl.when(g + nbuf - 1 < n_tiles)
            def _():
                x_copy(g + nbuf - 1, lax.rem(g + nbuf - 1, nbuf)).start(priority=TILE_DMA_PRIORITY)

            @pl.when(g >= nbuf)
            def _():
                y_copy(g - nbuf, slot).wait()

            _expert_tile(xbuf.at[slot], ybuf.at[slot], wg_s, wu_s, wd_s)
            y_copy(g, slot).start(priority=TILE_DMA_PRIORITY)
            return carry
        lax.fori_loop(base, base + n, tile_body, 0)

    @pl.when(last_step)
    def _():
        for back in range(nbuf, 0, -1):
            @pl.when(n_tiles >= back)
            def _():
                y_copy(n_tiles - back, lax.rem(n_tiles - back, nbuf)).wait()

        first_unused = n_tiles
        max_tiles = y_hbm.shape[0] // tile
        ybuf[0] = jnp.zeros(ybuf.shape[1:], ybuf.dtype)
        zero_copy = lambda t: pltpu.make_async_copy(
            ybuf.at[0], y_hbm.at[pl.ds(pl.multiple_of(t * tile, tile), tile), :], sem_y.at[0])

        def start(t, carry):
            zero_copy(t).start()
            return carry

        def wait(t, carry):
            zero_copy(t).wait()
            return carry
        lax.fori_loop(first_unused, max_tiles, start, 0)
        lax.fori_loop(first_unused, max_tiles, wait, 0)


def _experts(tile_start, tiles, xs, wg, wu, wd, layer):
    rows = EXPERT_ROWS
    _, n_experts, d, dff = wg.shape
    assert d == SUBLANES * LANES and n_experts == N_EXPERTS
    nbuf = EXPERT_TILE_BUFFERS
    any_spec = pl.BlockSpec(memory_space=pl.ANY)
    grid_spec = pltpu.PrefetchScalarGridSpec(
        num_scalar_prefetch=2,
        grid=(n_experts,),
        in_specs=[any_spec, any_spec, any_spec, any_spec],
        out_specs=any_spec,
        scratch_shapes=[pltpu.VMEM((nbuf, rows * SUBLANES, LANES), F32),
                        pltpu.VMEM((nbuf, rows * SUBLANES, LANES), F32),
                        pltpu.VMEM((2, d, dff), F32), pltpu.VMEM((2, d, dff), F32), pltpu.VMEM((2, dff, d), F32),
                        pltpu.VMEM((d, dff), BF16), pltpu.VMEM((d, dff), BF16), pltpu.VMEM((dff, d), BF16),
                        pltpu.SemaphoreType.DMA((nbuf,)), pltpu.SemaphoreType.DMA((nbuf,)),
                        pltpu.SemaphoreType.DMA((2,))],
    )
    return pl.pallas_call(
        functools.partial(_experts_kernel, layer=layer),
        grid_spec=grid_spec,
        out_shape=jax.ShapeDtypeStruct(xs.shape, F32),
        compiler_params=_params("arbitrary"),
        name="experts",
    )(tile_start, tiles, xs, wg, wu, wd)


def _start_slab_gather(src_hbm, dst, sem, row_ref, base, count, priority):
    def body(k, carry):
        _slab_copy(src_hbm, dst, sem, row_ref[base + k], k).start(priority=priority)
        return carry
    lax.fori_loop(0, count, body, 0, unroll=GATHER_UNROLL)


def _combine_kernel(pos_ref, ys_hbm, x1_ref, rt_ref, g_ref, o_ref, buf, sem, *, final_norm):
    i = pl.program_id(0)
    rows = COMBINE_ROWS
    n_steps = pl.num_programs(0)
    slot = i % 2

    def start(step, s):
        for j in range(2):
            _start_slab_gather(ys_hbm, buf.at[s, j], sem.at[s], pos_ref, (step * 2 + j) * rows, rows, priority=j)

    @pl.when(i == 0)
    def _():
        start(0, 0)

    @pl.when(i + 1 < n_steps)
    def _():
        start(i + 1, 1 - slot)

    _wait_slabs(ys_hbm, buf.at[slot, 0], sem.at[slot])
    _wait_slabs(ys_hbm, buf.at[slot, 1], sem.at[slot])
    rt = rt_ref[...]
    lane = lax.broadcasted_iota(jnp.int32, rt.shape, 1)
    w1 = jnp.sum(jnp.where(lane == RT_W1, rt, 0.0), axis=-1, keepdims=True)
    w2 = jnp.sum(jnp.where(lane == RT_W2, rt, 0.0), axis=-1, keepdims=True)
    y = w1 * _load_slabs(buf.at[slot, 0], rows) + w2 * _load_slabs(buf.at[slot, 1], rows)
    x2 = x1_ref[...] + y
    if final_norm:
        x2 = _rms(x2, g_ref[...])
    o_ref[...] = x2


def _combine(pos, ys, x1, rt, g, final_norm):
    n, d = x1.shape
    rows = COMBINE_ROWS
    grid_spec = pltpu.PrefetchScalarGridSpec(
        num_scalar_prefetch=1,
        grid=(n // rows,),
        in_specs=[pl.BlockSpec(memory_space=pl.ANY),
                  pl.BlockSpec((rows, d), lambda i, pos: (i, 0)),
                  pl.BlockSpec((rows, LANES), lambda i, pos: (i, 0)),
                  pl.BlockSpec(g.shape, lambda i, pos: (0, 0))],
        out_specs=pl.BlockSpec((rows, d), lambda i, pos: (i, 0)),
        scratch_shapes=[pltpu.VMEM((2, 2, rows * SUBLANES, LANES), F32), pltpu.SemaphoreType.DMA((2,))],
    )
    return pl.pallas_call(
        functools.partial(_combine_kernel, final_norm=final_norm),
        grid_spec=grid_spec,
        out_shape=jax.ShapeDtypeStruct((n, d), F32),
        compiler_params=_params("arbitrary"),
        name="combine",
    )(pos, ys, x1, rt, g)


def _dispatch_plan(counts, n):
    rows = EXPERT_ROWS
    max_tiles = (2 * n) // rows + N_EXPERTS
    cnt = counts[:N_EXPERTS, 0].astype(jnp.int32)
    tiles = (cnt + rows - 1) // rows
    tile_end = jnp.cumsum(tiles)
    tile_start = tile_end - tiles
    n_tiles = tile_end[-1:]
    pad_tile = jnp.where(cnt % rows != 0, tile_end - 1, -1).astype(jnp.int32)
    offs_b = jnp.broadcast_to((tile_start * rows).astype(F32)[:, None], (N_EXPERTS, DISPATCH_ROWS))
    return tile_start.astype(jnp.int32), tiles, n_tiles, pad_tile, offs_b, max_tiles * rows


def _pad_lanes(a, width):
    return jnp.pad(a, ((0, 0), (0, width - a.shape[1])))


def kernel(x, norm_mix_g, w_in, w_g2_f, b_g_f, w_g2_b, b_g_b, gla_norm_g, rpb, w_out, norm_ffn_g, w_grp, b_grp,
           w_exp, b_exp, w_gate, w_up, w_down, final_norm_g):
    batch, seq, d = x.shape
    n = batch * seq
    rows = seq // GRID_W
    depth = w_in.shape[0]
    xf = x.reshape(n, d)
    c_na = 3 * NA_WIDTH
    c_qk = c_na + 2 * GLA_KEY_WIDTH
    c_v = c_qk + GLA_VAL_WIDTH
    c_r = c_v + GLA_VAL_WIDTH
    for l in range(depth):
        wl = w_in[l]
        q_scale = jnp.concatenate([jnp.full((NA_WIDTH,), NA_HEAD_DIM ** -0.5, F32),
                                   jnp.ones((2 * NA_WIDTH,), F32)])
        w_na = (wl[:, :c_na] * q_scale).astype(BF16)
        qk_scale = jnp.concatenate([jnp.full((GLA_KEY_WIDTH,), GLA_DK ** -0.5, F32),
                                    jnp.ones((GLA_KEY_WIDTH,), F32)])
        w_qk = (wl[:, c_na:c_qk] * qk_scale).astype(BF16)
        w_v = wl[:, c_qk:c_v].astype(BF16)
        w_r = wl[:, c_v:c_r].astype(BF16)
        w_lr = _pad_lanes(wl[:, c_r:], LANES)
        w_g2 = jnp.zeros((LANES, 2 * GLA_KEY_WIDTH), F32)
        w_g2 = w_g2.at[:GLA_GATE_RANK, :GLA_KEY_WIDTH].set(w_g2_f[l])
        w_g2 = w_g2.at[GLA_GATE_RANK:2 * GLA_GATE_RANK, GLA_KEY_WIDTH:].set(w_g2_b[l])
        b_g = jnp.concatenate([b_g_f[l], b_g_b[l]])[None, :]
        na_qkv, gqk, gv, gr, gates = _in_proj(xf, norm_mix_g[l][None, :], w_na, w_qk, w_v, w_r, w_lr, w_g2, b_g)

        y_na = _na(na_qkv, _na_bias_table(rpb[l], rows), batch, rows)
        o_f, o_b = _gla(gqk, gv, gates, batch, seq)

        w_o = w_out[l].astype(BF16)
        w_rt = _pad_lanes(jnp.concatenate([w_exp[l], w_grp[l]], axis=1), LANES)
        w_rt_hi = w_rt.astype(BF16)
        w_rt_lo = (w_rt - w_rt_hi.astype(F32)).astype(BF16)
        w_rt3 = jnp.concatenate([w_rt_hi, w_rt_hi, w_rt_lo], axis=0).T
        b_rt = _pad_lanes(jnp.concatenate([b_exp[l], b_grp[l]])[None, :], ROUTE_ROWS).T
        x1, h2_slabs, rt_t, counts = _out_route(y_na, o_f, o_b, gr, xf, w_o[:NA_WIDTH], w_o[NA_WIDTH:],
                                                gla_norm_g[l][None, :], norm_ffn_g[l][None, :], w_rt3, b_rt)

        tile_start, tiles, n_tiles, pad_tile, offs_b, sorted_rows = _dispatch_plan(counts, n)
        pos, rt, xs = _dispatch(pad_tile, n_tiles, rt_t, offs_b, h2_slabs, sorted_rows)
        ys = _experts(tile_start, tiles, xs, w_gate, w_up, w_down, l)
        pos = pos.reshape(-1)
        last = l == depth - 1
        xf = _combine(pos, ys, x1, rt, final_norm_g[None, :], final_norm=last)
    return xf.reshape(batch, seq, d)
```

```python
import functools

import jax
import jax.numpy as jnp
from jax import lax
from jax.experimental import pallas as pl
from jax.experimental.pallas import tpu as pltpu

F32 = jnp.float32
BF16 = jnp.bfloat16

GRID_W = 64
NA_HEADS = 8
NA_HEAD_DIM = 64
NA_WIDTH = NA_HEADS * NA_HEAD_DIM
WIN_H_MAX = 8
WIN_W = 16
GLA_HEADS = 4
GLA_DK = 64
GLA_DV = 128
GLA_KEY_WIDTH = GLA_HEADS * GLA_DK
GLA_VAL_WIDTH = GLA_HEADS * GLA_DV
GLA_GATE_RANK = 16
GLA_GATE_NORMALIZER = 16.0
GLA_CHUNK = 64
N_GROUPS = 4
EXPERTS_PER_GROUP = 8
N_EXPERTS = N_GROUPS * EXPERTS_PER_GROUP
RMS_EPS = 1e-6

LANES = 128
SUBLANES = 8
MXU_WIDTH = 256
VMEM_LIMIT_BYTES = 56 * 1024 * 1024

MASK_VALUE = -1e30

IN_PROJ_ROWS = 512
NA_ROW_BLOCK = 8
NA_ROW_UNROLL = 8
GLA_STEP_CHUNKS = 4
OUT_ROWS = 512
EXPERT_ROWS = 256
EXPERT_TILE_BUFFERS = 4
DISPATCH_ROWS = 512
COMBINE_ROWS = DISPATCH_ROWS
GATHER_UNROLL = 64


def _params(*sem):
    return pltpu.CompilerParams(dimension_semantics=sem, vmem_limit_bytes=VMEM_LIMIT_BYTES)


def _rms(x, g):
    return x * lax.rsqrt(jnp.mean(x * x, axis=-1, keepdims=True) + RMS_EPS) * g


def _in_proj_kernel(x_ref, g_ref, w_na_ref, w_qk_ref, w_v_ref, w_r_ref, w_lr_ref,
                    w_g2_ref, b_g_ref, na_ref, qk_ref, v_ref, r_ref, gate_ref, w_gate):
    @pl.when(pl.program_id(0) == 0)
    def _():
        w_gate[...] = jnp.dot(w_lr_ref[...], w_g2_ref[...], preferred_element_type=F32,
                              precision=lax.Precision.HIGHEST).astype(BF16)

    h = _rms(x_ref[...], g_ref[...]).astype(BF16)
    na_ref[...] = jnp.dot(h, w_na_ref[...], preferred_element_type=F32).astype(BF16)
    qk_ref[...] = jnp.dot(h, w_qk_ref[...], preferred_element_type=F32)
    v_ref[...] = jnp.dot(h, w_v_ref[...], preferred_element_type=F32).astype(BF16)
    r_ref[...] = jnp.dot(h, w_r_ref[...], preferred_element_type=F32)
    z = jnp.dot(h, w_gate[...], preferred_element_type=F32) + b_g_ref[...]
    log_sig = jnp.minimum(z, 0.0) - jnp.log(1.0 + jnp.exp(-jnp.abs(z)))
    gate_ref[...] = log_sig * (1.0 / GLA_GATE_NORMALIZER)


def _layer_spec(a, layer):
    return pl.BlockSpec((None,) + a.shape[1:], lambda *_: (layer,) + (0,) * (a.ndim - 1))


def _in_proj(x, layer, g, w_na, w_qk, w_v, w_r, w_lr, w_g2, b_g):
    n, d = x.shape
    rows = IN_PROJ_ROWS
    row_spec = lambda width: pl.BlockSpec((rows, width), lambda i: (i, 0))
    full = lambda a: _layer_spec(a, layer)
    return pl.pallas_call(
        _in_proj_kernel,
        grid=(n // rows,),
        in_specs=[row_spec(d), full(g), full(w_na), full(w_qk), full(w_v), full(w_r), full(w_lr),
                  full(w_g2), full(b_g)],
        out_specs=[row_spec(3 * NA_WIDTH), row_spec(2 * GLA_KEY_WIDTH), row_spec(GLA_VAL_WIDTH),
                   row_spec(GLA_VAL_WIDTH), row_spec(2 * GLA_KEY_WIDTH)],
        out_shape=[jax.ShapeDtypeStruct((n, 3 * NA_WIDTH), BF16),
                   jax.ShapeDtypeStruct((n, 2 * GLA_KEY_WIDTH), F32),
                   jax.ShapeDtypeStruct((n, GLA_VAL_WIDTH), BF16),
                   jax.ShapeDtypeStruct((n, GLA_VAL_WIDTH), F32),
                   jax.ShapeDtypeStruct((n, 2 * GLA_KEY_WIDTH), F32)],
        scratch_shapes=[pltpu.VMEM((d, 2 * GLA_KEY_WIDTH), BF16)],
        compiler_params=_params("arbitrary"),
        name="in_proj",
    )(x, g, w_na, w_qk, w_v, w_r, w_lr, w_g2, b_g)


def _na_bias_table(rpb, rows):
    kh = min(WIN_H_MAX, rows)
    w = jnp.arange(GRID_W)[:, None, None]
    x = jnp.arange(GRID_W)[None, :, None]
    cs = jnp.clip(w - WIN_W // 2, 0, GRID_W - WIN_W)
    valid = (x >= cs) & (x < cs + WIN_W)
    col_sel = (valid & (jnp.arange(2 * WIN_W - 1)[None, None, :] == x - w + (WIN_W - 1))).astype(F32)
    by_col = jnp.einsum("hrc,wxc->hwrx", rpb.astype(F32), col_sel, precision=lax.Precision.HIGHEST)
    by_col = jnp.where(valid[None, :, None, :, 0], by_col, MASK_VALUE)
    by_col = by_col.reshape(NA_HEADS * GRID_W, (2 * WIN_H_MAX - 1) * GRID_W)
    bias = jnp.stack([by_col[:, (WIN_H_MAX - 1 - c) * GRID_W:(WIN_H_MAX - 1 - c + kh) * GRID_W]
                      for c in range(kh)])
    return bias.reshape(kh, NA_HEADS // 2, 2 * GRID_W, kh * GRID_W)


def _na_kernel(q_ref, kp_ref, kc_ref, kn_ref, vp_ref, vc_ref, vn_ref, tbl_ref, o_ref,
               kbuf, vbuf, *, rows, kh):
    rb = NA_ROW_BLOCK
    j = pl.program_id(1)
    blk = rb * GRID_W
    for s, (k_src, v_src) in enumerate(((kp_ref, vp_ref), (kc_ref, vc_ref), (kn_ref, vn_ref))):
        kbuf[s * blk:(s + 1) * blk, :] = k_src[...].reshape(blk, NA_WIDTH)
        vbuf[s * blk:(s + 1) * blk, :] = v_src[...].reshape(blk, NA_WIDTH)
    lane = lax.broadcasted_iota(jnp.int32, (GRID_W, LANES), 1)
    first = lane < NA_HEAD_DIM

    def row_body(lr, carry):
        r = j * rb + lr
        start = jnp.clip(r - kh // 2, 0, rows - kh)
        cls = r - start
        local = pl.multiple_of((start - (j - 1) * rb) * GRID_W, GRID_W)
        q_row = q_ref[lr]
        k_win = kbuf[pl.ds(local, kh * GRID_W), :]
        v_win = vbuf[pl.ds(local, kh * GRID_W), :]
        pairs = [slice(p * LANES, (p + 1) * LANES) for p in range(NA_HEADS // 2)]
        scores = []
        for sl in pairs:
            q_pair = q_row[:, sl]
            zero = jnp.zeros_like(q_pair)
            q_bd = jnp.concatenate([jnp.where(first, q_pair, zero), jnp.where(first, zero, q_pair)], axis=0)
            scores.append(lax.dot_general(q_bd, k_win[:, sl], (((1,), (1,)), ((), ())),
                                          preferred_element_type=F32))
        probs, denoms = [], []
        for p, s in enumerate(scores):
            s = s + tbl_ref[cls, p]
            e = jnp.exp(s - jnp.max(s, axis=-1, keepdims=True))
            denoms.append(jnp.sum(e, axis=-1, keepdims=True))
            probs.append(e.astype(BF16))
        outs = []
        for sl, e, denom in zip(pairs, probs, denoms):
            o = jnp.dot(e, v_win[:, sl], preferred_element_type=F32) / denom
            outs.append(jnp.where(first, o[:GRID_W], o[GRID_W:]))
        o_ref[lr] = jnp.concatenate(outs, axis=-1).astype(o_ref.dtype)
        return carry

    lax.fori_loop(0, rb, row_body, 0, unroll=NA_ROW_UNROLL)


def _na(na_qkv, tbl, layer, batch, rows):
    kh = min(WIN_H_MAX, rows)
    rb = NA_ROW_BLOCK
    nblk = rows // rb
    x4 = na_qkv.reshape(batch, rows, GRID_W, 3 * NA_WIDTH)
    blk = (None, rb, GRID_W, NA_WIDTH)
    prev = lambda j: jnp.maximum(j - 1, 0)
    nxt = lambda j: jnp.minimum(j + 1, nblk - 1)
    specs = [pl.BlockSpec(blk, lambda b, j: (b, j, 0, 0))]
    for col in (1, 2):
        specs += [pl.BlockSpec(blk, lambda b, j, col=col: (b, prev(j), 0, col)),
                  pl.BlockSpec(blk, lambda b, j, col=col: (b, j, 0, col)),
                  pl.BlockSpec(blk, lambda b, j, col=col: (b, nxt(j), 0, col))]
    specs.append(_layer_spec(tbl, layer))
    out = pl.pallas_call(
        functools.partial(_na_kernel, rows=rows, kh=kh),
        grid=(batch, nblk),
        in_specs=specs,
        out_specs=pl.BlockSpec(blk, lambda b, j: (b, j, 0, 0)),
        out_shape=jax.ShapeDtypeStruct((batch, rows, GRID_W, NA_WIDTH), BF16),
        scratch_shapes=[pltpu.VMEM((3 * rb * GRID_W, NA_WIDTH), BF16),
                        pltpu.VMEM((3 * rb * GRID_W, NA_WIDTH), BF16)],
        compiler_params=_params("arbitrary", "arbitrary"),
        name="na",
    )(x4, x4, x4, x4, x4, x4, x4, tbl)
    return out.reshape(batch * rows * GRID_W, NA_WIDTH)


def _block_diag_mask(row_block, col_block, nblocks):
    shape = (row_block * nblocks, col_block * nblocks)
    r = lax.shift_right_logical(lax.broadcasted_iota(jnp.int32, shape, 0), row_block.bit_length() - 1)
    c = lax.shift_right_logical(lax.broadcasted_iota(jnp.int32, shape, 1), col_block.bit_length() - 1)
    return r == c


def _split_bf16x3(x):
    hi = x.astype(BF16)
    rest = x - hi.astype(F32)
    mid = rest.astype(BF16)
    lo = (rest - mid.astype(F32)).astype(BF16)
    return hi, mid, lo


def _gla_direction(qk_ref, v_ref, g_ref, *, backward):
    c = GLA_CHUNK
    nc = GLA_STEP_CHUNKS
    kw = GLA_KEY_WIDTH
    order = list(reversed(range(nc))) if backward else list(range(nc))
    chunk = lambda a, n: a[n * c:(n + 1) * c]

    def stage_cumsum():
        step = nc * c
        ti = lax.broadcasted_iota(jnp.int32, (step, step), 0)
        tj = lax.broadcasted_iota(jnp.int32, (step, step), 1)
        same_chunk = lax.shift_right_logical(ti, c.bit_length() - 1) == lax.shift_right_logical(tj, c.bit_length() - 1)
        tri = (tj >= ti) if backward else (tj <= ti)
        cum = jnp.where(same_chunk & tri, 1.0, 0.0).astype(BF16)
        return sum(jnp.dot(cum, piece, preferred_element_type=F32) for piece in _split_bf16x3(g_ref[...]))

    def stage_decays(b):
        ref_row, last_row = (c // 2, 0) if backward else (c // 2 - 1, c - 1)
        rows_of = lambda r: jnp.concatenate(
            [jnp.broadcast_to(b[n * c + r:n * c + r + 1, :], (c, kw)) for n in range(nc)], axis=0)
        b_ref, b_last = rows_of(ref_row), rows_of(last_row)
        q = qk_ref[:, :kw]
        k = qk_ref[:, kw:]
        q_rel = (q * jnp.exp(b - b_ref)).astype(BF16)
        k_rel = (k * jnp.exp(b_ref - b)).astype(BF16)
        k_dec = k * jnp.exp(b_last - b)
        q_dec = (q * jnp.exp(b)).astype(BF16)
        decay = jnp.exp(b_last)
        return q_rel, k_rel, k_dec, q_dec, decay

    def stage_scores(q_rel, k_rel):
        kk_mask = _block_diag_mask(c, GLA_DK, GLA_HEADS)
        out = []
        for n in range(nc):
            k_bd = jnp.where(kk_mask, jnp.concatenate([chunk(k_rel, n)] * GLA_HEADS, axis=0), jnp.zeros((), BF16))
            out.append(lax.dot_general(chunk(q_rel, n), k_bd, (((1,), (1,)), ((), ())),
                                       preferred_element_type=F32))
        return out

    def stage_intra(scores, k_dec, decay):
        si = lax.broadcasted_iota(jnp.int32, (c, c * GLA_HEADS), 0)
        sj = lax.broadcasted_iota(jnp.int32, (c, c * GLA_HEADS), 1) & (c - 1)
        keep = (sj > si) if backward else (sj <= si)
        kv_mask = _block_diag_mask(c, GLA_DV, GLA_HEADS)
        vk_mask = _block_diag_mask(GLA_DK, GLA_DV, 2)
        o_intra, upd, decay_col = [], [], []
        for n in range(nc):
            v = v_ref[n * c:(n + 1) * c, :]
            p = jnp.where(keep, scores[n], 0.0).astype(BF16)
            v_bd = jnp.where(kv_mask, jnp.concatenate([v] * GLA_HEADS, axis=0), jnp.zeros((), BF16))
            o_intra.append(jnp.dot(p, v_bd, preferred_element_type=F32))
            k_dec_t = chunk(k_dec, n).T.astype(BF16)
            pairs = []
            for hp in range(GLA_HEADS // 2):
                kv = jnp.dot(k_dec_t[hp * 2 * GLA_DK:(hp + 1) * 2 * GLA_DK],
                             v[:, hp * 2 * GLA_DV:(hp + 1) * 2 * GLA_DV], preferred_element_type=F32)
                pairs.append(jnp.where(vk_mask, kv, 0.0))
            upd.append(pairs)
            decay_col.append(chunk(decay, n).T[:, :1])
        return o_intra, upd, decay_col

    def scan_step(state, idx, q_dec, o_intra, upd, decay_col, out_ref):
        n = order[idx]
        q_n = chunk(q_dec, n)
        pw = 2 * GLA_DK
        o_inter = jnp.concatenate(
            [jnp.dot(q_n[:, hp * pw:(hp + 1) * pw], s.astype(BF16), preferred_element_type=F32)
             for hp, s in enumerate(state)], axis=1)
        out_ref[n * c:(n + 1) * c, :] = o_intra[n] + o_inter
        return [s * decay_col[n][hp * pw:(hp + 1) * pw] + u for hp, (s, u) in enumerate(zip(state, upd[n]))]

    return stage_cumsum, stage_decays, stage_scores, stage_intra, scan_step


def _gla_kernel(qk_f_ref, v_f_ref, g_f_ref, qk_b_ref, v_b_ref, g_b_ref, o_f_ref, o_b_ref,
                st_f, st_b):
    @pl.when(pl.program_id(1) == 0)
    def _():
        st_f[...] = jnp.zeros_like(st_f)
        st_b[...] = jnp.zeros_like(st_b)

    dirs = (_gla_direction(qk_f_ref, v_f_ref, g_f_ref, backward=False),
            _gla_direction(qk_b_ref, v_b_ref, g_b_ref, backward=True))
    b = [d[0]() for d in dirs]
    dec = [d[1](x) for d, x in zip(dirs, b)]
    scores = [d[2](x[0], x[1]) for d, x in zip(dirs, dec)]
    intra = [d[3](s, x[2], x[4]) for d, s, x in zip(dirs, scores, dec)]
    pairs = range(GLA_HEADS // 2)
    states = [[st[hp] for hp in pairs] for st in (st_f, st_b)]
    for idx in range(GLA_STEP_CHUNKS):
        for j, (d, x, y, out_ref) in enumerate(zip(dirs, dec, intra, (o_f_ref, o_b_ref))):
            states[j] = d[4](states[j], idx, x[3], *y, out_ref)
    for st, state in zip((st_f, st_b), states):
        for hp in pairs:
            st[hp] = state[hp]


def _gla(qk, v, gates, batch, seq):
    step = GLA_STEP_CHUNKS * GLA_CHUNK
    nblk = seq // step
    qk3 = qk.reshape(batch, seq, 2 * GLA_KEY_WIDTH)
    v3 = v.reshape(batch, seq, GLA_VAL_WIDTH)
    g3 = gates.reshape(batch, seq, 2 * GLA_KEY_WIDTH)
    fwd = lambda b, n: (b, n, 0)
    bwd = lambda b, n: (b, nblk - 1 - n, 0)
    bwd_gate = lambda b, n: (b, nblk - 1 - n, 1)
    o_f, o_b = pl.pallas_call(
        _gla_kernel,
        grid=(batch, nblk),
        in_specs=[pl.BlockSpec((None, step, 2 * GLA_KEY_WIDTH), fwd),
                  pl.BlockSpec((None, step, GLA_VAL_WIDTH), fwd),
                  pl.BlockSpec((None, step, GLA_KEY_WIDTH), fwd),
                  pl.BlockSpec((None, step, 2 * GLA_KEY_WIDTH), bwd),
                  pl.BlockSpec((None, step, GLA_VAL_WIDTH), bwd),
                  pl.BlockSpec((None, step, GLA_KEY_WIDTH), bwd_gate)],
        out_specs=[pl.BlockSpec((None, step, GLA_VAL_WIDTH), fwd),
                   pl.BlockSpec((None, step, GLA_VAL_WIDTH), bwd)],
        out_shape=[jax.ShapeDtypeStruct((batch, seq, GLA_VAL_WIDTH), F32)] * 2,
        scratch_shapes=[pltpu.VMEM((GLA_HEADS // 2, 2 * GLA_DK, 2 * GLA_DV), F32),
                        pltpu.VMEM((GLA_HEADS // 2, 2 * GLA_DK, 2 * GLA_DV), F32)],
        compiler_params=_params("arbitrary", "arbitrary"),
        name="gla",
    )(qk3, v3, g3, qk3, v3, g3)
    return o_f.reshape(batch * seq, GLA_VAL_WIDTH), o_b.reshape(batch * seq, GLA_VAL_WIDTH)


def _store_slabs(ref, x):
    rows = x.shape[0]
    for s in range(SUBLANES):
        ref[pl.ds(s, rows, stride=SUBLANES), :] = x[:, s * LANES:(s + 1) * LANES]


def _load_slabs(ref, rows):
    return jnp.concatenate([ref[pl.ds(s, rows, stride=SUBLANES), :] for s in range(SUBLANES)], axis=1)


def _slab_copy(src_hbm, dst, sem, src_row, k):
    dst_row = k * SUBLANES if isinstance(k, int) else pl.multiple_of(k * SUBLANES, SUBLANES)
    return pltpu.make_async_copy(src_hbm.at[pl.ds(pl.multiple_of(src_row, SUBLANES), SUBLANES), :],
                                 dst.at[pl.ds(dst_row, SUBLANES), :], sem)


def _wait_slabs(src_hbm, dst, sem):
    pltpu.make_async_copy(src_hbm.at[pl.ds(0, dst.shape[0]), :], dst, sem).wait()


RT_E1, RT_E2, RT_RANK1, RT_RANK2, RT_W1, RT_W2 = range(6)
GROUP_ROW0 = N_EXPERTS
ROUTE_ROWS = 64


def _out_route_kernel(na_ref, of_ref, ob_ref, r_ref, x_ref, w_na_ref, w_gla_ref, gn_ref, fn_ref,
                      w_rt_ref, b_rt_ref, x1_ref, h2_ref, rt_ref, cnt_ref, carry):
    @pl.when(pl.program_id(0) == 0)
    def _():
        carry[...] = jnp.zeros_like(carry)

    rows = x_ref.shape[0]
    o = of_ref[...] + ob_ref[...]
    r = r_ref[...]
    parts = []
    for h in range(GLA_HEADS):
        sl = slice(h * GLA_DV, (h + 1) * GLA_DV)
        parts.append(_rms(o[:, sl], gn_ref[...]) * (r[:, sl] * jax.nn.sigmoid(r[:, sl])))
    y_gla = jnp.concatenate(parts, axis=-1).astype(BF16)
    x1 = (x_ref[...] + jnp.dot(na_ref[...], w_na_ref[...], preferred_element_type=F32)
          + jnp.dot(y_gla, w_gla_ref[...], preferred_element_type=F32))
    x1_ref[...] = x1
    h2 = _rms(x1, fn_ref[...])
    _store_slabs(h2_ref, h2)

    h_hi = h2.astype(BF16)
    h_lo = (h2 - h_hi.astype(F32)).astype(BF16)
    logits = lax.dot_general(w_rt_ref[...], jnp.concatenate([h_hi, h_lo, h_hi], axis=1),
                             (((1,), (1,)), ((), ())), preferred_element_type=F32)
    logits = logits[:ROUTE_ROWS] + b_rt_ref[...]
    row_i = lax.broadcasted_iota(jnp.int32, (ROUTE_ROWS, rows), 0)
    row = row_i.astype(F32)
    row_grp = lax.shift_right_logical(row_i, EXPERTS_PER_GROUP.bit_length() - 1).astype(F32)
    neg = jnp.float32(-jnp.inf)
    no_row = jnp.float32(ROUTE_ROWS)
    is_grp = (row_i >= GROUP_ROW0) & (row_i < GROUP_ROW0 + N_GROUPS)
    g_logit = jnp.where(is_grp, logits, neg)
    g_max = jnp.max(g_logit, axis=0, keepdims=True)
    g_sel = jnp.min(jnp.where(is_grp & (g_logit == g_max), row, no_row), axis=0, keepdims=True) - GROUP_ROW0
    grp_w = 1.0 / jnp.sum(jnp.where(is_grp, jnp.exp(g_logit - g_max), 0.0), axis=0, keepdims=True)
    in_grp = (row_i < N_EXPERTS) & (row_grp == g_sel)
    e_logit = jnp.where(in_grp, logits, neg)
    v1 = jnp.max(e_logit, axis=0, keepdims=True)
    i1 = jnp.min(jnp.where(in_grp & (e_logit == v1), row, no_row), axis=0, keepdims=True)
    rest = in_grp & (row != i1)
    e_logit2 = jnp.where(rest, logits, neg)
    v2 = jnp.max(e_logit2, axis=0, keepdims=True)
    i2 = jnp.min(jnp.where(rest & (e_logit2 == v2), row, no_row), axis=0, keepdims=True)
    t = jnp.exp(v2 - v1)
    w1 = grp_w / (1.0 + t)
    w2 = grp_w * t / (1.0 + t)

    sel1 = row == i1
    sel2 = row == i2
    onehot = jnp.where(sel1 | sel2, 1.0, 0.0)
    ti = lax.broadcasted_iota(jnp.int32, (rows, rows), 0)
    tj = lax.broadcasted_iota(jnp.int32, (rows, rows), 1)
    earlier = jnp.where(ti < tj, 1.0, 0.0).astype(BF16)
    ranks = jnp.dot(onehot.astype(BF16), earlier, preferred_element_type=F32) + carry[:, :1]
    rank1 = jnp.sum(jnp.where(sel1, ranks, 0.0), axis=0, keepdims=True)
    rank2 = jnp.sum(jnp.where(sel2, ranks, 0.0), axis=0, keepdims=True)
    new_carry = carry[...] + jnp.sum(onehot, axis=1, keepdims=True)
    carry[...] = new_carry
    cnt_ref[...] = new_carry

    fields = {RT_E1: i1, RT_E2: i2, RT_RANK1: rank1, RT_RANK2: rank2, RT_W1: w1, RT_W2: w2}
    zero = jnp.zeros_like(w1)
    rt_ref[...] = jnp.concatenate([fields.get(f, zero) for f in range(SUBLANES)], axis=0)


def _out_route(y_na, o_f, o_b, r, x, layer, w_na, w_gla, gn, fn, w_rt, b_rt):
    n, d = x.shape
    rows = OUT_ROWS
    row_spec = lambda width: pl.BlockSpec((rows, width), lambda i: (i, 0))
    full = lambda a: _layer_spec(a, layer)
    return pl.pallas_call(
        _out_route_kernel,
        grid=(n // rows,),
        in_specs=[row_spec(NA_WIDTH), row_spec(GLA_VAL_WIDTH), row_spec(GLA_VAL_WIDTH),
                  row_spec(GLA_VAL_WIDTH), row_spec(d), full(w_na), full(w_gla), full(gn), full(fn),
                  full(w_rt), full(b_rt)],
        out_specs=[row_spec(d), pl.BlockSpec((rows * SUBLANES, LANES), lambda i: (i, 0)),
                   pl.BlockSpec((SUBLANES, rows), lambda i: (0, i)),
                   pl.BlockSpec((ROUTE_ROWS, LANES), lambda i: (0, 0))],
        out_shape=[jax.ShapeDtypeStruct((n, d), F32), jax.ShapeDtypeStruct((n * SUBLANES, LANES), F32),
                   jax.ShapeDtypeStruct((SUBLANES, n), F32), jax.ShapeDtypeStruct((ROUTE_ROWS, LANES), F32)],
        scratch_shapes=[pltpu.VMEM((ROUTE_ROWS, LANES), F32)],
        compiler_params=_params("arbitrary"),
        name="out_route",
    )(y_na, o_f, o_b, r, x, w_na, w_gla, gn, fn, w_rt, b_rt)


def _dispatch_kernel(pad_tile_ref, n_tiles_ref, rt_ref, offs_ref, h_hbm, pos_ref, rec_ref, xs_hbm,
                     zeros, hbuf, pos_vmem, pos_smem, sem_zero, sem_pos, sem_in, sem):
    i = pl.program_id(0)
    n_steps = pl.num_programs(0)
    db = DISPATCH_ROWS
    tile = EXPERT_ROWS * SUBLANES
    max_tiles = xs_hbm.shape[0] // tile
    buf = lax.rem(i, 2)

    def block_copy(step, b):
        rows = pl.ds(pl.multiple_of(step * db * SUBLANES, db * SUBLANES), db * SUBLANES)
        return pltpu.make_async_copy(h_hbm.at[rows, :], hbuf.at[b], sem_in.at[b])

    def wait_row_copies(b):
        for _ in range(2):
            pltpu.make_async_copy(hbuf.at[b], xs_hbm.at[pl.ds(0, db * SUBLANES), :], sem).wait()

    def zero_copy(t):
        return pltpu.make_async_copy(zeros, xs_hbm.at[pl.ds(pl.multiple_of(t * tile, tile), tile), :], sem_zero)

    def for_each_zero_tile(fn):
        def padded(e, carry):
            @pl.when(pad_tile_ref[e] >= 0)
            def _():
                fn(zero_copy(pad_tile_ref[e]))
            return carry
        lax.fori_loop(0, N_EXPERTS, padded, 0)

        def tail(t, carry):
            fn(zero_copy(t))
            return carry
        lax.fori_loop(n_tiles_ref[0], max_tiles, tail, 0)

    @pl.when(i == 0)
    def _():
        block_copy(0, 0).start()
        zeros[...] = jnp.zeros_like(zeros)
        for_each_zero_tile(lambda cp: cp.start())

    rt_t = rt_ref[...]
    rec_ref[...] = jnp.concatenate([rt_t, jnp.zeros((LANES - SUBLANES, db), F32)], axis=0).T
    expert = lax.broadcasted_iota(jnp.int32, (N_EXPERTS, db), 0).astype(F32)
    pos = []
    for e_lane, rank_lane in ((RT_E1, RT_RANK1), (RT_E2, RT_RANK2)):
        seg = jnp.sum(jnp.where(expert == rt_t[e_lane:e_lane + 1, :], offs_ref[...], 0.0), axis=0, keepdims=True)
        pos.append((seg + rt_t[rank_lane:rank_lane + 1, :]) * SUBLANES)
    pos = jnp.concatenate(pos, axis=0).astype(jnp.int32)
    pos_ref[...] = pos
    pos_vmem[...] = jnp.concatenate([pos, jnp.zeros((SUBLANES - 2, db), jnp.int32)], axis=0)
    to_smem = pltpu.make_async_copy(pos_vmem, pos_smem, sem_pos)
    to_smem.start()

    @pl.when(i == 0)
    def _():
        for_each_zero_tile(lambda cp: cp.wait())

    @pl.when(i > 0)
    def _():
        wait_row_copies(1 - buf)

    to_smem.wait()
    block_copy(i, buf).wait()

    @pl.when(i + 1 < n_steps)
    def _():
        block_copy(i + 1, 1 - buf).start()

    h_blk = hbuf.at[buf]
    for k in range(db):
        src = h_blk.at[pl.ds(k * SUBLANES, SUBLANES), :]
        for slot in range(2):
            dst_row = pl.multiple_of(pos_smem[slot, k], SUBLANES)
            pltpu.make_async_copy(src, xs_hbm.at[pl.ds(dst_row, SUBLANES), :], sem).start(priority=slot)

    @pl.when(i == n_steps - 1)
    def _():
        wait_row_copies(buf)


def _dispatch(pad_tile, n_tiles, rt_t, offs_b, h2_slabs, sorted_rows):
    n = rt_t.shape[1]
    db = DISPATCH_ROWS
    grid_spec = pltpu.PrefetchScalarGridSpec(
        num_scalar_prefetch=2,
        grid=(n // db,),
        in_specs=[pl.BlockSpec((SUBLANES, db), lambda i, pt, nt: (0, i)),
                  pl.BlockSpec(offs_b.shape, lambda i, pt, nt: (0, 0)),
                  pl.BlockSpec(memory_space=pl.ANY)],
        out_specs=[pl.BlockSpec((None, 2, db), lambda i, pt, nt: (i, 0, 0)),
                   pl.BlockSpec((db, LANES), lambda i, pt, nt: (i, 0)),
                   pl.BlockSpec(memory_space=pl.ANY)],
        scratch_shapes=[pltpu.VMEM((EXPERT_ROWS * SUBLANES, LANES), F32),
                        pltpu.VMEM((2, db * SUBLANES, LANES), F32),
                        pltpu.VMEM((SUBLANES, db), jnp.int32), pltpu.SMEM((SUBLANES, db), jnp.int32),
                        pltpu.SemaphoreType.DMA(()), pltpu.SemaphoreType.DMA(()),
                        pltpu.SemaphoreType.DMA((2,)), pltpu.SemaphoreType.DMA(())],
    )
    return pl.pallas_call(
        _dispatch_kernel,
        grid_spec=grid_spec,
        out_shape=[jax.ShapeDtypeStruct((n // db, 2, db), jnp.int32),
                   jax.ShapeDtypeStruct((n, LANES), F32),
                   jax.ShapeDtypeStruct((sorted_rows * SUBLANES, LANES), F32)],
        compiler_params=_params("arbitrary"),
        name="dispatch",
    )(pad_tile, n_tiles, rt_t, offs_b, h2_slabs)


TILE_DMA_PRIORITY = 1


def _expert_tile(x_ref, y_ref, wg_s, wu_s, wd_s):
    rows = EXPERT_ROWS
    x = _load_slabs(x_ref, rows).astype(BF16)
    hid = []
    for c in range(0, wg_s.shape[1], MXU_WIDTH):
        gate = jnp.dot(x, wg_s[:, c:c + MXU_WIDTH], preferred_element_type=F32)
        up = jnp.dot(x, wu_s[:, c:c + MXU_WIDTH], preferred_element_type=F32)
        hid.append((gate * jax.nn.sigmoid(gate) * up).astype(BF16))
    hid = jnp.concatenate(hid, axis=1)
    for c in range(0, wd_s.shape[1], MXU_WIDTH):
        y = jnp.dot(hid, wd_s[:, c:c + MXU_WIDTH], preferred_element_type=F32)
        for s in range(c // LANES, (c + MXU_WIDTH) // LANES):
            y_ref[pl.ds(s, rows, stride=SUBLANES), :] = y[:, s * LANES - c:(s + 1) * LANES - c]


def _experts_kernel(tile_start_ref, tiles_ref, x_hbm, wg_hbm, wu_hbm, wd_hbm, y_hbm,
                    xbuf, ybuf, wg_f, wu_f, wd_f, wg_s, wu_s, wd_s, sem_x, sem_y, sem_w, *, layer):
    nbuf = EXPERT_TILE_BUFFERS
    e = pl.program_id(0)
    last_step = e == pl.num_programs(0) - 1
    tile = EXPERT_ROWS * SUBLANES
    n = tiles_ref[e]
    base = tile_start_ref[e]
    n_tiles = tile_start_ref[N_EXPERTS - 1] + tiles_ref[N_EXPERTS - 1]
    tile_rows = lambda g: pl.ds(pl.multiple_of(g * tile, tile), tile)

    def x_copy(g, slot):
        return pltpu.make_async_copy(x_hbm.at[tile_rows(g), :], xbuf.at[slot], sem_x.at[slot])

    def y_copy(g, slot):
        return pltpu.make_async_copy(ybuf.at[slot], y_hbm.at[tile_rows(g), :], sem_y.at[slot])

    def w_copies(expert, slot):
        return [pltpu.make_async_copy(w_hbm.at[layer, expert], stage.at[slot], sem_w.at[slot])
                for w_hbm, stage in ((wg_hbm, wg_f), (wu_hbm, wu_f), (wd_hbm, wd_f))]

    w_slot = lax.rem(e, 2)

    @pl.when(e == 0)
    def _():
        for g in range(nbuf - 1):
            @pl.when(g < n_tiles)
            def _():
                x_copy(g, g).start(priority=TILE_DMA_PRIORITY)
        for cp in w_copies(0, 0):
            cp.start()

    for cp in w_copies(e, w_slot):
        cp.wait()

    @pl.when(jnp.logical_not(last_step))
    def _():
        for cp in w_copies(e + 1, 1 - w_slot):
            cp.start()

    @pl.when(n > 0)
    def _():
        wg_s[...] = wg_f[w_slot].astype(BF16)
        wu_s[...] = wu_f[w_slot].astype(BF16)
        wd_s[...] = wd_f[w_slot].astype(BF16)

        def tile_body(g, carry):
            slot = lax.rem(g, nbuf)
            x_copy(g, slot).wait()

            @pl.when(g + nbuf - 1 < n_tiles)
            def _():
                x_copy(g + nbuf - 1, lax.rem(g + nbuf - 1, nbuf)).start(priority=TILE_DMA_PRIORITY)

            @pl.when(g >= nbuf)
            def _():
                y_copy(g - nbuf, slot).wait()

            _expert_tile(xbuf.at[slot], ybuf.at[slot], wg_s, wu_s, wd_s)
            y_copy(g, slot).start(priority=TILE_DMA_PRIORITY)
            return carry
        lax.fori_loop(base, base + n, tile_body, 0)

    @pl.when(last_step)
    def _():
        for back in range(nbuf, 0, -1):
            @pl.when(n_tiles >= back)
            def _():
                y_copy(n_tiles - back, lax.rem(n_tiles - back, nbuf)).wait()

        first_unused = n_tiles
        max_tiles = y_hbm.shape[0] // tile
        ybuf[0] = jnp.zeros(ybuf.shape[1:], ybuf.dtype)
        zero_copy = lambda t: pltpu.make_async_copy(
            ybuf.at[0], y_hbm.at[pl.ds(pl.multiple_of(t * tile, tile), tile), :], sem_y.at[0])

        def start(t, carry):
            zero_copy(t).start()
            return carry

        def wait(t, carry):
            zero_copy(t).wait()
            return carry
        lax.fori_loop(first_unused, max_tiles, start, 0)
        lax.fori_loop(first_unused, max_tiles, wait, 0)


def _experts(tile_start, tiles, xs, wg, wu, wd, layer):
    rows = EXPERT_ROWS
    _, n_experts, d, dff = wg.shape
    assert d == SUBLANES * LANES and n_experts == N_EXPERTS
    nbuf = EXPERT_TILE_BUFFERS
    any_spec = pl.BlockSpec(memory_space=pl.ANY)
    grid_spec = pltpu.PrefetchScalarGridSpec(
        num_scalar_prefetch=2,
        grid=(n_experts,),
        in_specs=[any_spec, any_spec, any_spec, any_spec],
        out_specs=any_spec,
        scratch_shapes=[pltpu.VMEM((nbuf, rows * SUBLANES, LANES), F32),
                        pltpu.VMEM((nbuf, rows * SUBLANES, LANES), F32),
                        pltpu.VMEM((2, d, dff), F32), pltpu.VMEM((2, d, dff), F32), pltpu.VMEM((2, dff, d), F32),
                        pltpu.VMEM((d, dff), BF16), pltpu.VMEM((d, dff), BF16), pltpu.VMEM((dff, d), BF16),
                        pltpu.SemaphoreType.DMA((nbuf,)), pltpu.SemaphoreType.DMA((nbuf,)),
                        pltpu.SemaphoreType.DMA((2,))],
    )
    return pl.pallas_call(
        functools.partial(_experts_kernel, layer=layer),
        grid_spec=grid_spec,
        out_shape=jax.ShapeDtypeStruct(xs.shape, F32),
        compiler_params=_params("arbitrary"),
        name="experts",
    )(tile_start, tiles, xs, wg, wu, wd)


def _start_slab_gather(src_hbm, dst, sem, row_ref, base, count, priority):
    def body(k, carry):
        _slab_copy(src_hbm, dst, sem, row_ref[base + k], k).start(priority=priority)
        return carry
    lax.fori_loop(0, count, body, 0, unroll=GATHER_UNROLL)


def _combine_kernel(pos_ref, ys_hbm, x1_ref, rt_ref, g_ref, o_ref, buf, sem, *, final_norm):
    i = pl.program_id(0)
    rows = COMBINE_ROWS
    n_steps = pl.num_programs(0)
    slot = i % 2

    def start(step, s):
        for j in range(2):
            _start_slab_gather(ys_hbm, buf.at[s, j], sem.at[s], pos_ref, (step * 2 + j) * rows, rows, priority=j)

    @pl.when(i == 0)
    def _():
        start(0, 0)

    @pl.when(i + 1 < n_steps)
    def _():
        start(i + 1, 1 - slot)

    _wait_slabs(ys_hbm, buf.at[slot, 0], sem.at[slot])
    _wait_slabs(ys_hbm, buf.at[slot, 1], sem.at[slot])
    rt = rt_ref[...]
    lane = lax.broadcasted_iota(jnp.int32, rt.shape, 1)
    w1 = jnp.sum(jnp.where(lane == RT_W1, rt, 0.0), axis=-1, keepdims=True)
    w2 = jnp.sum(jnp.where(lane == RT_W2, rt, 0.0), axis=-1, keepdims=True)
    y = w1 * _load_slabs(buf.at[slot, 0], rows) + w2 * _load_slabs(buf.at[slot, 1], rows)
    x2 = x1_ref[...] + y
    if final_norm:
        x2 = _rms(x2, g_ref[...])
    o_ref[...] = x2


def _combine(pos, ys, x1, rt, g, final_norm):
    n, d = x1.shape
    rows = COMBINE_ROWS
    grid_spec = pltpu.PrefetchScalarGridSpec(
        num_scalar_prefetch=1,
        grid=(n // rows,),
        in_specs=[pl.BlockSpec(memory_space=pl.ANY),
                  pl.BlockSpec((rows, d), lambda i, pos: (i, 0)),
                  pl.BlockSpec((rows, LANES), lambda i, pos: (i, 0)),
                  pl.BlockSpec(g.shape, lambda i, pos: (0, 0))],
        out_specs=pl.BlockSpec((rows, d), lambda i, pos: (i, 0)),
        scratch_shapes=[pltpu.VMEM((2, 2, rows * SUBLANES, LANES), F32), pltpu.SemaphoreType.DMA((2,))],
    )
    return pl.pallas_call(
        functools.partial(_combine_kernel, final_norm=final_norm),
        grid_spec=grid_spec,
        out_shape=jax.ShapeDtypeStruct((n, d), F32),
        compiler_params=_params("arbitrary"),
        name="combine",
    )(pos, ys, x1, rt, g)


def _dispatch_plan(counts, n):
    rows = EXPERT_ROWS
    max_tiles = (2 * n) // rows + N_EXPERTS
    cnt = counts[:N_EXPERTS, 0].astype(jnp.int32)
    tiles = (cnt + rows - 1) // rows
    tile_end = jnp.cumsum(tiles)
    tile_start = tile_end - tiles
    n_tiles = tile_end[-1:]
    pad_tile = jnp.where(cnt % rows != 0, tile_end - 1, -1).astype(jnp.int32)
    offs_b = jnp.broadcast_to((tile_start * rows).astype(F32)[:, None], (N_EXPERTS, DISPATCH_ROWS))
    return tile_start.astype(jnp.int32), tiles, n_tiles, pad_tile, offs_b, max_tiles * rows


def kernel(x, norm_mix_g, w_in, w_g2_f, b_g_f, w_g2_b, b_g_b, gla_norm_g, rpb, w_out, norm_ffn_g, w_grp, b_grp,
           w_exp, b_exp, w_gate, w_up, w_down, final_norm_g):
    batch, seq, d = x.shape
    n = batch * seq
    rows = seq // GRID_W
    depth = w_in.shape[0]
    xf = x.reshape(n, d)
    c_na = 3 * NA_WIDTH
    c_qk = c_na + 2 * GLA_KEY_WIDTH
    c_v = c_qk + GLA_VAL_WIDTH
    c_r = c_v + GLA_VAL_WIDTH

    q_scale = jnp.concatenate([jnp.full((NA_WIDTH,), NA_HEAD_DIM ** -0.5, F32), jnp.ones((2 * NA_WIDTH,), F32)])
    w_na = (w_in[:, :, :c_na] * q_scale).astype(BF16)
    qk_scale = jnp.concatenate([jnp.full((GLA_KEY_WIDTH,), GLA_DK ** -0.5, F32), jnp.ones((GLA_KEY_WIDTH,), F32)])
    w_qk = (w_in[:, :, c_na:c_qk] * qk_scale).astype(BF16)
    w_v = w_in[:, :, c_qk:c_v].astype(BF16)
    w_r = w_in[:, :, c_v:c_r].astype(BF16)
    w_lr = jnp.pad(w_in[:, :, c_r:], ((0, 0), (0, 0), (0, LANES - 2 * GLA_GATE_RANK)))
    zero_g2 = jnp.zeros_like(w_g2_f)
    w_g2 = jnp.concatenate([jnp.concatenate([w_g2_f, zero_g2], axis=2),
                            jnp.concatenate([zero_g2, w_g2_b], axis=2)], axis=1)
    w_g2 = jnp.pad(w_g2, ((0, 0), (0, LANES - 2 * GLA_GATE_RANK), (0, 0)))
    b_g = jnp.concatenate([b_g_f, b_g_b], axis=1)[:, None, :]
    bias_tbl = jax.vmap(functools.partial(_na_bias_table, rows=rows))(rpb)
    w_o = w_out.astype(BF16)
    w_o_na, w_o_gla = w_o[:, :NA_WIDTH], w_o[:, NA_WIDTH:]
    w_rt = jnp.pad(jnp.concatenate([w_exp, w_grp], axis=2), ((0, 0), (0, 0), (0, LANES - N_EXPERTS - N_GROUPS)))
    w_rt_hi = w_rt.astype(BF16)
    w_rt_lo = (w_rt - w_rt_hi.astype(F32)).astype(BF16)
    w_rt3 = jnp.concatenate([w_rt_hi, w_rt_hi, w_rt_lo], axis=1).transpose(0, 2, 1)
    b_rt = jnp.pad(jnp.concatenate([b_exp, b_grp], axis=1),
                   ((0, 0), (0, ROUTE_ROWS - N_EXPERTS - N_GROUPS)))[:, :, None]
    per_layer_row = lambda a: a[:, None, :]

    for l in range(depth):
        na_qkv, gqk, gv, gr, gates = _in_proj(xf, l, per_layer_row(norm_mix_g), w_na, w_qk, w_v, w_r, w_lr,
                                              w_g2, b_g)
        y_na = _na(na_qkv, bias_tbl, l, batch, rows)
        o_f, o_b = _gla(gqk, gv, gates, batch, seq)
        x1, h2_slabs, rt_t, counts = _out_route(y_na, o_f, o_b, gr, xf, l, w_o_na, w_o_gla,
                                                per_layer_row(gla_norm_g), per_layer_row(norm_ffn_g), w_rt3, b_rt)

        tile_start, tiles, n_tiles, pad_tile, offs_b, sorted_rows = _dispatch_plan(counts, n)
        pos, rt, xs = _dispatch(pad_tile, n_tiles, rt_t, offs_b, h2_slabs, sorted_rows)
        ys = _experts(tile_start, tiles, xs, w_gate, w_up, w_down, l)
        pos = pos.reshape(-1)
        last = l == depth - 1
        xf = _combine(pos, ys, x1, rt, final_norm_g[None, :], final_norm=last)
    return xf.reshape(batch, seq, d)
```

```python
import functools

import jax
import jax.numpy as jnp
from jax import lax
from jax.experimental import pallas as pl
from jax.experimental.pallas import tpu as pltpu

F32 = jnp.float32
BF16 = jnp.bfloat16

GRID_W = 64
NA_HEADS = 8
NA_HEAD_DIM = 64
NA_WIDTH = NA_HEADS * NA_HEAD_DIM
WIN_H_MAX = 8
WIN_W = 16
GLA_HEADS = 4
GLA_DK = 64
GLA_DV = 128
GLA_KEY_WIDTH = GLA_HEADS * GLA_DK
GLA_VAL_WIDTH = GLA_HEADS * GLA_DV
GLA_GATE_RANK = 16
GLA_GATE_NORMALIZER = 16.0
GLA_CHUNK = 64
N_GROUPS = 4
EXPERTS_PER_GROUP = 8
N_EXPERTS = N_GROUPS * EXPERTS_PER_GROUP
RMS_EPS = 1e-6

LANES = 128
SUBLANES = 8
MXU_WIDTH = 256
VMEM_LIMIT_BYTES = 56 * 1024 * 1024

MASK_VALUE = -1e30

IN_PROJ_ROWS = 512
NA_ROW_BLOCK = 8
NA_ROW_UNROLL = 8
GLA_STEP_CHUNKS = 4
OUT_ROWS = 512
EXPERT_ROWS = 256
EXPERT_TILE_BUFFERS = 4
DISPATCH_ROWS = 512
COMBINE_ROWS = DISPATCH_ROWS
FUSED_ROWS = 512


def _params(*sem):
    return pltpu.CompilerParams(dimension_semantics=sem, vmem_limit_bytes=VMEM_LIMIT_BYTES)


def _rms(x, g):
    return x * lax.rsqrt(jnp.mean(x * x, axis=-1, keepdims=True) + RMS_EPS) * g


def _in_proj_rows(x, g_ref, w_na_ref, w_qk_ref, w_v_ref, w_r_ref, w_lr_ref,
                  w_g2_ref, b_g_ref, na_ref, qk_ref, v_ref, r_ref, gate_ref, w_gate):
    @pl.when(pl.program_id(0) == 0)
    def _():
        w_gate[...] = jnp.dot(w_lr_ref[...], w_g2_ref[...], preferred_element_type=F32,
                              precision=lax.Precision.HIGHEST).astype(BF16)

    h = _rms(x, g_ref[...]).astype(BF16)
    na_ref[...] = jnp.dot(h, w_na_ref[...], preferred_element_type=F32).astype(BF16)
    qk_ref[...] = jnp.dot(h, w_qk_ref[...], preferred_element_type=F32)
    v_ref[...] = jnp.dot(h, w_v_ref[...], preferred_element_type=F32).astype(BF16)
    r_ref[...] = jnp.dot(h, w_r_ref[...], preferred_element_type=F32)
    z = jnp.dot(h, w_gate[...], preferred_element_type=F32) + b_g_ref[...]
    log_sig = jnp.minimum(z, 0.0) - jnp.log(1.0 + jnp.exp(-jnp.abs(z)))
    gate_ref[...] = log_sig * (1.0 / GLA_GATE_NORMALIZER)


def _in_proj_kernel(x_ref, *refs):
    _in_proj_rows(x_ref[...], *refs)


def _in_proj_outputs(n, rows):
    row_spec = lambda width: pl.BlockSpec((rows, width), lambda i, *_: (i, 0))
    widths = (3 * NA_WIDTH, 2 * GLA_KEY_WIDTH, GLA_VAL_WIDTH, GLA_VAL_WIDTH, 2 * GLA_KEY_WIDTH)
    dtypes = (BF16, F32, BF16, F32, F32)
    return ([row_spec(w) for w in widths],
            [jax.ShapeDtypeStruct((n, w), t) for w, t in zip(widths, dtypes)])


def _layer_spec(a, layer):
    return pl.BlockSpec((None,) + a.shape[1:], lambda *_: (layer,) + (0,) * (a.ndim - 1))


def _in_proj(x, layer, g, w_na, w_qk, w_v, w_r, w_lr, w_g2, b_g):
    n, d = x.shape
    rows = IN_PROJ_ROWS
    params = (g, w_na, w_qk, w_v, w_r, w_lr, w_g2, b_g)
    out_specs, out_shape = _in_proj_outputs(n, rows)
    return pl.pallas_call(
        _in_proj_kernel,
        grid=(n // rows,),
        in_specs=[pl.BlockSpec((rows, d), lambda i: (i, 0))] + [_layer_spec(a, layer) for a in params],
        out_specs=out_specs,
        out_shape=out_shape,
        scratch_shapes=[pltpu.VMEM((d, 2 * GLA_KEY_WIDTH), BF16)],
        compiler_params=_params("arbitrary"),
        name="in_proj",
    )(x, *params)


def _na_bias_table(rpb, rows):
    kh = min(WIN_H_MAX, rows)
    w = jnp.arange(GRID_W)[:, None, None]
    x = jnp.arange(GRID_W)[None, :, None]
    cs = jnp.clip(w - WIN_W // 2, 0, GRID_W - WIN_W)
    valid = (x >= cs) & (x < cs + WIN_W)
    col_sel = (valid & (jnp.arange(2 * WIN_W - 1)[None, None, :] == x - w + (WIN_W - 1))).astype(F32)
    by_col = jnp.einsum("hrc,wxc->hwrx", rpb.astype(F32), col_sel, precision=lax.Precision.HIGHEST)
    by_col = jnp.where(valid[None, :, None, :, 0], by_col, MASK_VALUE)
    by_col = by_col.reshape(NA_HEADS * GRID_W, (2 * WIN_H_MAX - 1) * GRID_W)
    bias = jnp.stack([by_col[:, (WIN_H_MAX - 1 - c) * GRID_W:(WIN_H_MAX - 1 - c + kh) * GRID_W]
                      for c in range(kh)])
    return bias.reshape(kh, NA_HEADS // 2, 2 * GRID_W, kh * GRID_W)


def _na_kernel(q_ref, kp_ref, kc_ref, kn_ref, vp_ref, vc_ref, vn_ref, tbl_ref, o_ref,
               kbuf, vbuf, *, rows, kh):
    rb = NA_ROW_BLOCK
    j = pl.program_id(1)
    blk = rb * GRID_W
    for s, (k_src, v_src) in enumerate(((kp_ref, vp_ref), (kc_ref, vc_ref), (kn_ref, vn_ref))):
        kbuf[s * blk:(s + 1) * blk, :] = k_src[...].reshape(blk, NA_WIDTH)
        vbuf[s * blk:(s + 1) * blk, :] = v_src[...].reshape(blk, NA_WIDTH)
    lane = lax.broadcasted_iota(jnp.int32, (GRID_W, LANES), 1)
    first = lane < NA_HEAD_DIM

    def row_body(lr, carry):
        r = j * rb + lr
        start = jnp.clip(r - kh // 2, 0, rows - kh)
        cls = r - start
        local = pl.multiple_of((start - (j - 1) * rb) * GRID_W, GRID_W)
        q_row = q_ref[lr]
        k_win = kbuf[pl.ds(local, kh * GRID_W), :]
        v_win = vbuf[pl.ds(local, kh * GRID_W), :]
        pairs = [slice(p * LANES, (p + 1) * LANES) for p in range(NA_HEADS // 2)]
        scores = []
        for sl in pairs:
            q_pair = q_row[:, sl]
            zero = jnp.zeros_like(q_pair)
            q_bd = jnp.concatenate([jnp.where(first, q_pair, zero), jnp.where(first, zero, q_pair)], axis=0)
            scores.append(lax.dot_general(q_bd, k_win[:, sl], (((1,), (1,)), ((), ())),
                                          preferred_element_type=F32))
        probs, denoms = [], []
        for p, s in enumerate(scores):
            s = s + tbl_ref[cls, p]
            e = jnp.exp(s - jnp.max(s, axis=-1, keepdims=True))
            denoms.append(jnp.sum(e, axis=-1, keepdims=True))
            probs.append(e.astype(BF16))
        outs = []
        for sl, e, denom in zip(pairs, probs, denoms):
            o = jnp.dot(e, v_win[:, sl], preferred_element_type=F32) / denom
            outs.append(jnp.where(first, o[:GRID_W], o[GRID_W:]))
        o_ref[lr] = jnp.concatenate(outs, axis=-1).astype(o_ref.dtype)
        return carry

    lax.fori_loop(0, rb, row_body, 0, unroll=NA_ROW_UNROLL)


def _na(na_qkv, tbl, layer, batch, rows):
    kh = min(WIN_H_MAX, rows)
    rb = NA_ROW_BLOCK
    nblk = rows // rb
    x4 = na_qkv.reshape(batch, rows, GRID_W, 3 * NA_WIDTH)
    blk = (None, rb, GRID_W, NA_WIDTH)
    prev = lambda j: jnp.maximum(j - 1, 0)
    nxt = lambda j: jnp.minimum(j + 1, nblk - 1)
    specs = [pl.BlockSpec(blk, lambda b, j: (b, j, 0, 0))]
    for col in (1, 2):
        specs += [pl.BlockSpec(blk, lambda b, j, col=col: (b, prev(j), 0, col)),
                  pl.BlockSpec(blk, lambda b, j, col=col: (b, j, 0, col)),
                  pl.BlockSpec(blk, lambda b, j, col=col: (b, nxt(j), 0, col))]
    specs.append(_layer_spec(tbl, layer))
    out = pl.pallas_call(
        functools.partial(_na_kernel, rows=rows, kh=kh),
        grid=(batch, nblk),
        in_specs=specs,
        out_specs=pl.BlockSpec(blk, lambda b, j: (b, j, 0, 0)),
        out_shape=jax.ShapeDtypeStruct((batch, rows, GRID_W, NA_WIDTH), BF16),
        scratch_shapes=[pltpu.VMEM((3 * rb * GRID_W, NA_WIDTH), BF16),
                        pltpu.VMEM((3 * rb * GRID_W, NA_WIDTH), BF16)],
        compiler_params=_params("arbitrary", "arbitrary"),
        name="na",
    )(x4, x4, x4, x4, x4, x4, x4, tbl)
    return out.reshape(batch * rows * GRID_W, NA_WIDTH)


def _block_diag_mask(row_block, col_block, nblocks):
    shape = (row_block * nblocks, col_block * nblocks)
    r = lax.shift_right_logical(lax.broadcasted_iota(jnp.int32, shape, 0), row_block.bit_length() - 1)
    c = lax.shift_right_logical(lax.broadcasted_iota(jnp.int32, shape, 1), col_block.bit_length() - 1)
    return r == c


def _split_bf16x3(x):
    hi = x.astype(BF16)
    rest = x - hi.astype(F32)
    mid = rest.astype(BF16)
    lo = (rest - mid.astype(F32)).astype(BF16)
    return hi, mid, lo


def _gla_direction(qk_ref, v_ref, g_ref, *, backward):
    c = GLA_CHUNK
    nc = GLA_STEP_CHUNKS
    kw = GLA_KEY_WIDTH
    order = list(reversed(range(nc))) if backward else list(range(nc))
    chunk = lambda a, n: a[n * c:(n + 1) * c]

    def stage_cumsum():
        step = nc * c
        ti = lax.broadcasted_iota(jnp.int32, (step, step), 0)
        tj = lax.broadcasted_iota(jnp.int32, (step, step), 1)
        same_chunk = lax.shift_right_logical(ti, c.bit_length() - 1) == lax.shift_right_logical(tj, c.bit_length() - 1)
        tri = (tj >= ti) if backward else (tj <= ti)
        cum = jnp.where(same_chunk & tri, 1.0, 0.0).astype(BF16)
        return sum(jnp.dot(cum, piece, preferred_element_type=F32) for piece in _split_bf16x3(g_ref[...]))

    def stage_decays(b):
        ref_row, last_row = (c // 2, 0) if backward else (c // 2 - 1, c - 1)
        rows_of = lambda r: jnp.concatenate(
            [jnp.broadcast_to(b[n * c + r:n * c + r + 1, :], (c, kw)) for n in range(nc)], axis=0)
        b_ref, b_last = rows_of(ref_row), rows_of(last_row)
        q = qk_ref[:, :kw]
        k = qk_ref[:, kw:]
        q_rel = (q * jnp.exp(b - b_ref)).astype(BF16)
        k_rel = (k * jnp.exp(b_ref - b)).astype(BF16)
        k_dec = k * jnp.exp(b_last - b)
        q_dec = (q * jnp.exp(b)).astype(BF16)
        decay = jnp.exp(b_last)
        return q_rel, k_rel, k_dec, q_dec, decay

    def stage_scores(q_rel, k_rel):
        kk_mask = _block_diag_mask(c, GLA_DK, GLA_HEADS)
        out = []
        for n in range(nc):
            k_bd = jnp.where(kk_mask, jnp.concatenate([chunk(k_rel, n)] * GLA_HEADS, axis=0), jnp.zeros((), BF16))
            out.append(lax.dot_general(chunk(q_rel, n), k_bd, (((1,), (1,)), ((), ())),
                                       preferred_element_type=F32))
        return out

    def stage_intra(scores, k_dec, decay):
        si = lax.broadcasted_iota(jnp.int32, (c, c * GLA_HEADS), 0)
        sj = lax.broadcasted_iota(jnp.int32, (c, c * GLA_HEADS), 1) & (c - 1)
        keep = (sj > si) if backward else (sj <= si)
        kv_mask = _block_diag_mask(c, GLA_DV, GLA_HEADS)
        vk_mask = _block_diag_mask(GLA_DK, GLA_DV, 2)
        o_intra, upd, decay_col = [], [], []
        for n in range(nc):
            v = v_ref[n * c:(n + 1) * c, :]
            p = jnp.where(keep, scores[n], 0.0).astype(BF16)
            v_bd = jnp.where(kv_mask, jnp.concatenate([v] * GLA_HEADS, axis=0), jnp.zeros((), BF16))
            o_intra.append(jnp.dot(p, v_bd, preferred_element_type=F32))
            k_dec_t = chunk(k_dec, n).T.astype(BF16)
            pairs = []
            for hp in range(GLA_HEADS // 2):
                kv = jnp.dot(k_dec_t[hp * 2 * GLA_DK:(hp + 1) * 2 * GLA_DK],
                             v[:, hp * 2 * GLA_DV:(hp + 1) * 2 * GLA_DV], preferred_element_type=F32)
                pairs.append(jnp.where(vk_mask, kv, 0.0))
            upd.append(pairs)
            decay_col.append(chunk(decay, n).T[:, :1])
        return o_intra, upd, decay_col

    def scan_step(state, idx, q_dec, o_intra, upd, decay_col, out_ref):
        n = order[idx]
        q_n = chunk(q_dec, n)
        pw = 2 * GLA_DK
        o_inter = jnp.concatenate(
            [jnp.dot(q_n[:, hp * pw:(hp + 1) * pw], s.astype(BF16), preferred_element_type=F32)
             for hp, s in enumerate(state)], axis=1)
        out_ref[n * c:(n + 1) * c, :] = o_intra[n] + o_inter
        return [s * decay_col[n][hp * pw:(hp + 1) * pw] + u for hp, (s, u) in enumerate(zip(state, upd[n]))]

    return stage_cumsum, stage_decays, stage_scores, stage_intra, scan_step


def _gla_kernel(qk_f_ref, v_f_ref, g_f_ref, qk_b_ref, v_b_ref, g_b_ref, o_f_ref, o_b_ref,
                st_f, st_b):
    @pl.when(pl.program_id(1) == 0)
    def _():
        st_f[...] = jnp.zeros_like(st_f)
        st_b[...] = jnp.zeros_like(st_b)

    dirs = (_gla_direction(qk_f_ref, v_f_ref, g_f_ref, backward=False),
            _gla_direction(qk_b_ref, v_b_ref, g_b_ref, backward=True))
    b = [d[0]() for d in dirs]
    dec = [d[1](x) for d, x in zip(dirs, b)]
    scores = [d[2](x[0], x[1]) for d, x in zip(dirs, dec)]
    intra = [d[3](s, x[2], x[4]) for d, s, x in zip(dirs, scores, dec)]
    pairs = range(GLA_HEADS // 2)
    states = [[st[hp] for hp in pairs] for st in (st_f, st_b)]
    for idx in range(GLA_STEP_CHUNKS):
        for j, (d, x, y, out_ref) in enumerate(zip(dirs, dec, intra, (o_f_ref, o_b_ref))):
            states[j] = d[4](states[j], idx, x[3], *y, out_ref)
    for st, state in zip((st_f, st_b), states):
        for hp in pairs:
            st[hp] = state[hp]


def _gla(qk, v, gates, batch, seq):
    step = GLA_STEP_CHUNKS * GLA_CHUNK
    nblk = seq // step
    qk3 = qk.reshape(batch, seq, 2 * GLA_KEY_WIDTH)
    v3 = v.reshape(batch, seq, GLA_VAL_WIDTH)
    g3 = gates.reshape(batch, seq, 2 * GLA_KEY_WIDTH)
    fwd = lambda b, n: (b, n, 0)
    bwd = lambda b, n: (b, nblk - 1 - n, 0)
    bwd_gate = lambda b, n: (b, nblk - 1 - n, 1)
    o_f, o_b = pl.pallas_call(
        _gla_kernel,
        grid=(batch, nblk),
        in_specs=[pl.BlockSpec((None, step, 2 * GLA_KEY_WIDTH), fwd),
                  pl.BlockSpec((None, step, GLA_VAL_WIDTH), fwd),
                  pl.BlockSpec((None, step, GLA_KEY_WIDTH), fwd),
                  pl.BlockSpec((None, step, 2 * GLA_KEY_WIDTH), bwd),
                  pl.BlockSpec((None, step, GLA_VAL_WIDTH), bwd),
                  pl.BlockSpec((None, step, GLA_KEY_WIDTH), bwd_gate)],
        out_specs=[pl.BlockSpec((None, step, GLA_VAL_WIDTH), fwd),
                   pl.BlockSpec((None, step, GLA_VAL_WIDTH), bwd)],
        out_shape=[jax.ShapeDtypeStruct((batch, seq, GLA_VAL_WIDTH), F32)] * 2,
        scratch_shapes=[pltpu.VMEM((GLA_HEADS // 2, 2 * GLA_DK, 2 * GLA_DV), F32),
                        pltpu.VMEM((GLA_HEADS // 2, 2 * GLA_DK, 2 * GLA_DV), F32)],
        compiler_params=_params("arbitrary", "arbitrary"),
        name="gla",
    )(qk3, v3, g3, qk3, v3, g3)
    return o_f.reshape(batch * seq, GLA_VAL_WIDTH), o_b.reshape(batch * seq, GLA_VAL_WIDTH)


def _store_slabs(ref, x):
    rows = x.shape[0]
    for s in range(SUBLANES):
        ref[pl.ds(s, rows, stride=SUBLANES), :] = x[:, s * LANES:(s + 1) * LANES]


def _load_slabs(ref, rows):
    return jnp.concatenate([ref[pl.ds(s, rows, stride=SUBLANES), :] for s in range(SUBLANES)], axis=1)


def _slab_copy(src_hbm, dst, sem, src_row, k):
    dst_row = k * SUBLANES if isinstance(k, int) else pl.multiple_of(k * SUBLANES, SUBLANES)
    return pltpu.make_async_copy(src_hbm.at[pl.ds(pl.multiple_of(src_row, SUBLANES), SUBLANES), :],
                                 dst.at[pl.ds(dst_row, SUBLANES), :], sem)


def _wait_slabs(src_hbm, dst, sem):
    pltpu.make_async_copy(src_hbm.at[pl.ds(0, dst.shape[0]), :], dst, sem).wait()


RT_E1, RT_E2, RT_RANK1, RT_RANK2, RT_W1, RT_W2 = range(6)
GROUP_ROW0 = N_EXPERTS
ROUTE_ROWS = 64


def _out_route_kernel(na_ref, of_ref, ob_ref, r_ref, x_ref, w_na_ref, w_gla_ref, gn_ref, fn_ref,
                      w_rt_ref, b_rt_ref, x1_ref, h2_ref, rt_ref, cnt_ref, carry):
    @pl.when(pl.program_id(0) == 0)
    def _():
        carry[...] = jnp.zeros_like(carry)

    rows = x_ref.shape[0]
    o = of_ref[...] + ob_ref[...]
    r = r_ref[...]
    parts = []
    for h in range(GLA_HEADS):
        sl = slice(h * GLA_DV, (h + 1) * GLA_DV)
        parts.append(_rms(o[:, sl], gn_ref[...]) * (r[:, sl] * jax.nn.sigmoid(r[:, sl])))
    y_gla = jnp.concatenate(parts, axis=-1).astype(BF16)
    x1 = (x_ref[...] + jnp.dot(na_ref[...], w_na_ref[...], preferred_element_type=F32)
          + jnp.dot(y_gla, w_gla_ref[...], preferred_element_type=F32))
    x1_ref[...] = x1
    h2 = _rms(x1, fn_ref[...])
    _store_slabs(h2_ref, h2)

    h_hi = h2.astype(BF16)
    h_lo = (h2 - h_hi.astype(F32)).astype(BF16)
    logits = lax.dot_general(w_rt_ref[...], jnp.concatenate([h_hi, h_lo, h_hi], axis=1),
                             (((1,), (1,)), ((), ())), preferred_element_type=F32)
    logits = logits[:ROUTE_ROWS] + b_rt_ref[...]
    row_i = lax.broadcasted_iota(jnp.int32, (ROUTE_ROWS, rows), 0)
    row = row_i.astype(F32)
    row_grp = lax.shift_right_logical(row_i, EXPERTS_PER_GROUP.bit_length() - 1).astype(F32)
    neg = jnp.float32(-jnp.inf)
    no_row = jnp.float32(ROUTE_ROWS)
    is_grp = (row_i >= GROUP_ROW0) & (row_i < GROUP_ROW0 + N_GROUPS)
    g_logit = jnp.where(is_grp, logits, neg)
    g_max = jnp.max(g_logit, axis=0, keepdims=True)
    g_sel = jnp.min(jnp.where(is_grp & (g_logit == g_max), row, no_row), axis=0, keepdims=True) - GROUP_ROW0
    grp_w = 1.0 / jnp.sum(jnp.where(is_grp, jnp.exp(g_logit - g_max), 0.0), axis=0, keepdims=True)
    in_grp = (row_i < N_EXPERTS) & (row_grp == g_sel)
    e_logit = jnp.where(in_grp, logits, neg)
    v1 = jnp.max(e_logit, axis=0, keepdims=True)
    i1 = jnp.min(jnp.where(in_grp & (e_logit == v1), row, no_row), axis=0, keepdims=True)
    rest = in_grp & (row != i1)
    e_logit2 = jnp.where(rest, logits, neg)
    v2 = jnp.max(e_logit2, axis=0, keepdims=True)
    i2 = jnp.min(jnp.where(rest & (e_logit2 == v2), row, no_row), axis=0, keepdims=True)
    t = jnp.exp(v2 - v1)
    w1 = grp_w / (1.0 + t)
    w2 = grp_w * t / (1.0 + t)

    sel1 = row == i1
    sel2 = row == i2
    onehot = jnp.where(sel1 | sel2, 1.0, 0.0)
    ti = lax.broadcasted_iota(jnp.int32, (rows, rows), 0)
    tj = lax.broadcasted_iota(jnp.int32, (rows, rows), 1)
    earlier = jnp.where(ti < tj, 1.0, 0.0).astype(BF16)
    ranks = jnp.dot(onehot.astype(BF16), earlier, preferred_element_type=F32) + carry[:, :1]
    rank1 = jnp.sum(jnp.where(sel1, ranks, 0.0), axis=0, keepdims=True)
    rank2 = jnp.sum(jnp.where(sel2, ranks, 0.0), axis=0, keepdims=True)
    new_carry = carry[...] + jnp.sum(onehot, axis=1, keepdims=True)
    carry[...] = new_carry
    cnt_ref[...] = new_carry

    fields = {RT_E1: i1, RT_E2: i2, RT_RANK1: rank1, RT_RANK2: rank2, RT_W1: w1, RT_W2: w2}
    zero = jnp.zeros_like(w1)
    rt_ref[...] = jnp.concatenate([fields.get(f, zero) for f in range(SUBLANES)], axis=0)


def _out_route(y_na, o_f, o_b, r, x, layer, w_na, w_gla, gn, fn, w_rt, b_rt):
    n, d = x.shape
    rows = OUT_ROWS
    row_spec = lambda width: pl.BlockSpec((rows, width), lambda i: (i, 0))
    full = lambda a: _layer_spec(a, layer)
    return pl.pallas_call(
        _out_route_kernel,
        grid=(n // rows,),
        in_specs=[row_spec(NA_WIDTH), row_spec(GLA_VAL_WIDTH), row_spec(GLA_VAL_WIDTH),
                  row_spec(GLA_VAL_WIDTH), row_spec(d), full(w_na), full(w_gla), full(gn), full(fn),
                  full(w_rt), full(b_rt)],
        out_specs=[row_spec(d), pl.BlockSpec((rows * SUBLANES, LANES), lambda i: (i, 0)),
                   pl.BlockSpec((SUBLANES, rows), lambda i: (0, i)),
                   pl.BlockSpec((ROUTE_ROWS, LANES), lambda i: (0, 0))],
        out_shape=[jax.ShapeDtypeStruct((n, d), F32), jax.ShapeDtypeStruct((n * SUBLANES, LANES), F32),
                   jax.ShapeDtypeStruct((SUBLANES, n), F32), jax.ShapeDtypeStruct((ROUTE_ROWS, LANES), F32)],
        scratch_shapes=[pltpu.VMEM((ROUTE_ROWS, LANES), F32)],
        compiler_params=_params("arbitrary"),
        name="out_route",
    )(y_na, o_f, o_b, r, x, w_na, w_gla, gn, fn, w_rt, b_rt)


def _dispatch_kernel(pad_tile_ref, n_tiles_ref, rt_ref, offs_ref, h_hbm, pos_ref, rec_ref, xs_hbm,
                     zeros, hbuf, pos_vmem, pos_smem, sem_zero, sem_pos, sem_in, sem):
    i = pl.program_id(0)
    n_steps = pl.num_programs(0)
    db = DISPATCH_ROWS
    tile = EXPERT_ROWS * SUBLANES
    max_tiles = xs_hbm.shape[0] // tile
    buf = lax.rem(i, 2)

    def block_copy(step, b):
        rows = pl.ds(pl.multiple_of(step * db * SUBLANES, db * SUBLANES), db * SUBLANES)
        return pltpu.make_async_copy(h_hbm.at[rows, :], hbuf.at[b], sem_in.at[b])

    def wait_row_copies(b):
        for _ in range(2):
            pltpu.make_async_copy(hbuf.at[b], xs_hbm.at[pl.ds(0, db * SUBLANES), :], sem).wait()

    def zero_copy(t):
        return pltpu.make_async_copy(zeros, xs_hbm.at[pl.ds(pl.multiple_of(t * tile, tile), tile), :], sem_zero)

    def for_each_zero_tile(fn):
        def padded(e, carry):
            @pl.when(pad_tile_ref[e] >= 0)
            def _():
                fn(zero_copy(pad_tile_ref[e]))
            return carry
        lax.fori_loop(0, N_EXPERTS, padded, 0)

        def tail(t, carry):
            fn(zero_copy(t))
            return carry
        lax.fori_loop(n_tiles_ref[0], max_tiles, tail, 0)

    @pl.when(i == 0)
    def _():
        block_copy(0, 0).start()
        zeros[...] = jnp.zeros_like(zeros)
        for_each_zero_tile(lambda cp: cp.start())

    rt_t = rt_ref[...]
    rec_ref[...] = jnp.concatenate([rt_t, jnp.zeros((LANES - SUBLANES, db), F32)], axis=0).T
    expert = lax.broadcasted_iota(jnp.int32, (N_EXPERTS, db), 0).astype(F32)
    pos = []
    for e_lane, rank_lane in ((RT_E1, RT_RANK1), (RT_E2, RT_RANK2)):
        seg = jnp.sum(jnp.where(expert == rt_t[e_lane:e_lane + 1, :], offs_ref[...], 0.0), axis=0, keepdims=True)
        pos.append((seg + rt_t[rank_lane:rank_lane + 1, :]) * SUBLANES)
    pos = jnp.concatenate(pos, axis=0).astype(jnp.int32)
    pos_ref[...] = pos
    pos_vmem[...] = jnp.concatenate([pos, jnp.zeros((SUBLANES - 2, db), jnp.int32)], axis=0)
    to_smem = pltpu.make_async_copy(pos_vmem, pos_smem, sem_pos)
    to_smem.start()

    @pl.when(i == 0)
    def _():
        for_each_zero_tile(lambda cp: cp.wait())

    @pl.when(i > 0)
    def _():
        wait_row_copies(1 - buf)

    to_smem.wait()
    block_copy(i, buf).wait()

    @pl.when(i + 1 < n_steps)
    def _():
        block_copy(i + 1, 1 - buf).start()

    h_blk = hbuf.at[buf]
    for k in range(db):
        src = h_blk.at[pl.ds(k * SUBLANES, SUBLANES), :]
        for slot in range(2):
            dst_row = pl.multiple_of(pos_smem[slot, k], SUBLANES)
            pltpu.make_async_copy(src, xs_hbm.at[pl.ds(dst_row, SUBLANES), :], sem).start(priority=slot)

    @pl.when(i == n_steps - 1)
    def _():
        wait_row_copies(buf)


def _dispatch(pad_tile, n_tiles, rt_t, offs_b, h2_slabs, sorted_rows):
    n = rt_t.shape[1]
    db = DISPATCH_ROWS
    grid_spec = pltpu.PrefetchScalarGridSpec(
        num_scalar_prefetch=2,
        grid=(n // db,),
        in_specs=[pl.BlockSpec((SUBLANES, db), lambda i, pt, nt: (0, i)),
                  pl.BlockSpec(offs_b.shape, lambda i, pt, nt: (0, 0)),
                  pl.BlockSpec(memory_space=pl.ANY)],
        out_specs=[pl.BlockSpec((None, 2, db), lambda i, pt, nt: (i, 0, 0)),
                   pl.BlockSpec((db, LANES), lambda i, pt, nt: (i, 0)),
                   pl.BlockSpec(memory_space=pl.ANY)],
        scratch_shapes=[pltpu.VMEM((EXPERT_ROWS * SUBLANES, LANES), F32),
                        pltpu.VMEM((2, db * SUBLANES, LANES), F32),
                        pltpu.VMEM((SUBLANES, db), jnp.int32), pltpu.SMEM((SUBLANES, db), jnp.int32),
                        pltpu.SemaphoreType.DMA(()), pltpu.SemaphoreType.DMA(()),
                        pltpu.SemaphoreType.DMA((2,)), pltpu.SemaphoreType.DMA(())],
    )
    return pl.pallas_call(
        _dispatch_kernel,
        grid_spec=grid_spec,
        out_shape=[jax.ShapeDtypeStruct((n // db, 2, db), jnp.int32),
                   jax.ShapeDtypeStruct((n, LANES), F32),
                   jax.ShapeDtypeStruct((sorted_rows * SUBLANES, LANES), F32)],
        compiler_params=_params("arbitrary"),
        name="dispatch",
    )(pad_tile, n_tiles, rt_t, offs_b, h2_slabs)


TILE_DMA_PRIORITY = 1


def _expert_tile(x_ref, y_ref, wg_s, wu_s, wd_s):
    rows = EXPERT_ROWS
    x = _load_slabs(x_ref, rows).astype(BF16)
    hid = []
    for c in range(0, wg_s.shape[1], MXU_WIDTH):
        gate = jnp.dot(x, wg_s[:, c:c + MXU_WIDTH], preferred_element_type=F32)
        up = jnp.dot(x, wu_s[:, c:c + MXU_WIDTH], preferred_element_type=F32)
        hid.append((gate * jax.nn.sigmoid(gate) * up).astype(BF16))
    hid = jnp.concatenate(hid, axis=1)
    for c in range(0, wd_s.shape[1], MXU_WIDTH):
        y = jnp.dot(hid, wd_s[:, c:c + MXU_WIDTH], preferred_element_type=F32)
        for s in range(c // LANES, (c + MXU_WIDTH) // LANES):
            y_ref[pl.ds(s, rows, stride=SUBLANES), :] = y[:, s * LANES - c:(s + 1) * LANES - c]


def _experts_kernel(tile_start_ref, tiles_ref, x_hbm, wg_hbm, wu_hbm, wd_hbm, y_hbm,
                    xbuf, ybuf, wg_f, wu_f, wd_f, wg_s, wu_s, wd_s, sem_x, sem_y, sem_w, *, layer):
    nbuf = EXPERT_TILE_BUFFERS
    e = pl.program_id(0)
    last_step = e == pl.num_programs(0) - 1
    tile = EXPERT_ROWS * SUBLANES
    n = tiles_ref[e]
    base = tile_start_ref[e]
    n_tiles = tile_start_ref[N_EXPERTS - 1] + tiles_ref[N_EXPERTS - 1]
    tile_rows = lambda g: pl.ds(pl.multiple_of(g * tile, tile), tile)

    def x_copy(g, slot):
        return pltpu.make_async_copy(x_hbm.at[tile_rows(g), :], xbuf.at[slot], sem_x.at[slot])

    def y_copy(g, slot):
        return pltpu.make_async_copy(ybuf.at[slot], y_hbm.at[tile_rows(g), :], sem_y.at[slot])

    def w_copies(expert, slot):
        return [pltpu.make_async_copy(w_hbm.at[layer, expert], stage.at[slot], sem_w.at[slot])
                for w_hbm, stage in ((wg_hbm, wg_f), (wu_hbm, wu_f), (wd_hbm, wd_f))]

    w_slot = lax.rem(e, 2)

    @pl.when(e == 0)
    def _():
        for g in range(nbuf - 1):
            @pl.when(g < n_tiles)
            def _():
                x_copy(g, g).start(priority=TILE_DMA_PRIORITY)
        for cp in w_copies(0, 0):
            cp.start()

    for cp in w_copies(e, w_slot):
        cp.wait()

    @pl.when(jnp.logical_not(last_step))
    def _():
        for cp in w_copies(e + 1, 1 - w_slot):
            cp.start()

    @pl.when(n > 0)
    def _():
        wg_s[...] = wg_f[w_slot].astype(BF16)
        wu_s[...] = wu_f[w_slot].astype(BF16)
        wd_s[...] = wd_f[w_slot].astype(BF16)

        def tile_body(g, carry):
            slot = lax.rem(g, nbuf)
            x_copy(g, slot).wait()

            @pl.when(g + nbuf - 1 < n_tiles)
            def _():
                x_copy(g + nbuf - 1, lax.rem(g + nbuf - 1, nbuf)).start(priority=TILE_DMA_PRIORITY)

            @pl.when(g >= nbuf)
            def _():
                y_copy(g - nbuf, slot).wait()

            _expert_tile(xbuf.at[slot], ybuf.at[slot], wg_s, wu_s, wd_s)
            y_copy(g, slot).start(priority=TILE_DMA_PRIORITY)
            return carry
        lax.fori_loop(base, base + n, tile_body, 0)

    @pl.when(last_step)
    def _():
        for back in range(nbuf, 0, -1):
            @pl.when(n_tiles >= back)
            def _():
                y_copy(n_tiles - back, lax.rem(n_tiles - back, nbuf)).wait()

        first_unused = n_tiles
        max_tiles = y_hbm.shape[0] // tile
        ybuf[0] = jnp.zeros(ybuf.shape[1:], ybuf.dtype)
        zero_copy = lambda t: pltpu.make_async_copy(
            ybuf.at[0], y_hbm.at[pl.ds(pl.multiple_of(t * tile, tile), tile), :], sem_y.at[0])

        def start(t, carry):
            zero_copy(t).start()
            return carry

        def wait(t, carry):
            zero_copy(t).wait()
            return carry
        lax.fori_loop(first_unused, max_tiles, start, 0)
        lax.fori_loop(first_unused, max_tiles, wait, 0)


def _experts(tile_start, tiles, xs, wg, wu, wd, layer):
    rows = EXPERT_ROWS
    _, n_experts, d, dff = wg.shape
    assert d == SUBLANES * LANES and n_experts == N_EXPERTS
    nbuf = EXPERT_TILE_BUFFERS
    any_spec = pl.BlockSpec(memory_space=pl.ANY)
    grid_spec = pltpu.PrefetchScalarGridSpec(
        num_scalar_prefetch=2,
        grid=(n_experts,),
        in_specs=[any_spec, any_spec, any_spec, any_spec],
        out_specs=any_spec,
        scratch_shapes=[pltpu.VMEM((nbuf, rows * SUBLANES, LANES), F32),
                        pltpu.VMEM((nbuf, rows * SUBLANES, LANES), F32),
                        pltpu.VMEM((2, d, dff), F32), pltpu.VMEM((2, d, dff), F32), pltpu.VMEM((2, dff, d), F32),
                        pltpu.VMEM((d, dff), BF16), pltpu.VMEM((d, dff), BF16), pltpu.VMEM((dff, d), BF16),
                        pltpu.SemaphoreType.DMA((nbuf,)), pltpu.SemaphoreType.DMA((nbuf,)),
                        pltpu.SemaphoreType.DMA((2,))],
    )
    return pl.pallas_call(
        functools.partial(_experts_kernel, layer=layer),
        grid_spec=grid_spec,
        out_shape=jax.ShapeDtypeStruct(xs.shape, F32),
        compiler_params=_params("arbitrary"),
        name="experts",
    )(tile_start, tiles, xs, wg, wu, wd)


def _start_slab_gather(src_hbm, dst, sem, row_ref, base, count, priority):
    for k in range(count):
        _slab_copy(src_hbm, dst, sem, row_ref[base + k], k).start(priority=priority)


def _combined_rows(pos_ref, ys_hbm, x1_ref, rt_ref, buf, sem):
    i = pl.program_id(0)
    n_steps = pl.num_programs(0)
    rows = x1_ref.shape[0]
    steps_per_block = COMBINE_ROWS // rows
    slot = i % 2

    def start(step, s):
        first = lax.div(step, steps_per_block) * (2 * COMBINE_ROWS) + lax.rem(step, steps_per_block) * rows
        for j in range(2):
            _start_slab_gather(ys_hbm, buf.at[s, j], sem.at[s], pos_ref, first + j * COMBINE_ROWS, rows, priority=j)

    @pl.when(i == 0)
    def _():
        start(0, 0)

    @pl.when(i + 1 < n_steps)
    def _():
        start(i + 1, 1 - slot)

    _wait_slabs(ys_hbm, buf.at[slot, 0], sem.at[slot])
    _wait_slabs(ys_hbm, buf.at[slot, 1], sem.at[slot])
    rt = rt_ref[...]
    lane = lax.broadcasted_iota(jnp.int32, rt.shape, 1)
    w1 = jnp.sum(jnp.where(lane == RT_W1, rt, 0.0), axis=-1, keepdims=True)
    w2 = jnp.sum(jnp.where(lane == RT_W2, rt, 0.0), axis=-1, keepdims=True)
    y = w1 * _load_slabs(buf.at[slot, 0], rows) + w2 * _load_slabs(buf.at[slot, 1], rows)
    return x1_ref[...] + y


def _combine_scratch(rows):
    return [pltpu.VMEM((2, 2, rows * SUBLANES, LANES), F32), pltpu.SemaphoreType.DMA((2,))]


def _combine_final_kernel(pos_ref, ys_hbm, x1_ref, rt_ref, g_ref, o_ref, buf, sem):
    o_ref[...] = _rms(_combined_rows(pos_ref, ys_hbm, x1_ref, rt_ref, buf, sem), g_ref[...])


def _combine_final(pos, ys, x1, rt, g):
    n, d = x1.shape
    rows = COMBINE_ROWS
    grid_spec = pltpu.PrefetchScalarGridSpec(
        num_scalar_prefetch=1,
        grid=(n // rows,),
        in_specs=[pl.BlockSpec(memory_space=pl.ANY),
                  pl.BlockSpec((rows, d), lambda i, pos: (i, 0)),
                  pl.BlockSpec((rows, LANES), lambda i, pos: (i, 0)),
                  pl.BlockSpec(g.shape, lambda i, pos: (0, 0))],
        out_specs=pl.BlockSpec((rows, d), lambda i, pos: (i, 0)),
        scratch_shapes=_combine_scratch(rows),
    )
    return pl.pallas_call(
        _combine_final_kernel,
        grid_spec=grid_spec,
        out_shape=jax.ShapeDtypeStruct((n, d), F32),
        compiler_params=_params("arbitrary"),
        name="combine",
    )(pos, ys, x1, rt, g)


def _combine_in_proj_kernel(pos_ref, ys_hbm, x1_ref, rt_ref, *refs):
    param_refs, (x_ref, *proj_refs), (w_gate, buf, sem) = refs[:8], refs[8:14], refs[14:]
    x = _combined_rows(pos_ref, ys_hbm, x1_ref, rt_ref, buf, sem)
    x_ref[...] = x
    _in_proj_rows(x, *param_refs, *proj_refs, w_gate)


def _combine_in_proj(pos, ys, x1, rt, layer, g, w_na, w_qk, w_v, w_r, w_lr, w_g2, b_g):
    n, d = x1.shape
    rows = FUSED_ROWS
    params = (g, w_na, w_qk, w_v, w_r, w_lr, w_g2, b_g)
    proj_specs, proj_shapes = _in_proj_outputs(n, rows)
    grid_spec = pltpu.PrefetchScalarGridSpec(
        num_scalar_prefetch=1,
        grid=(n // rows,),
        in_specs=[pl.BlockSpec(memory_space=pl.ANY),
                  pl.BlockSpec((rows, d), lambda i, pos: (i, 0)),
                  pl.BlockSpec((rows, LANES), lambda i, pos: (i, 0))] + [_layer_spec(a, layer) for a in params],
        out_specs=[pl.BlockSpec((rows, d), lambda i, pos: (i, 0))] + proj_specs,
        scratch_shapes=[pltpu.VMEM((d, 2 * GLA_KEY_WIDTH), BF16)] + _combine_scratch(rows),
    )
    x, *proj = pl.pallas_call(
        _combine_in_proj_kernel,
        grid_spec=grid_spec,
        out_shape=[jax.ShapeDtypeStruct((n, d), F32)] + proj_shapes,
        compiler_params=_params("arbitrary"),
        name="combine_in_proj",
    )(pos, ys, x1, rt, *params)
    return x, proj


def _dispatch_plan(counts, n):
    rows = EXPERT_ROWS
    max_tiles = (2 * n) // rows + N_EXPERTS
    cnt = counts[:N_EXPERTS, 0].astype(jnp.int32)
    tiles = (cnt + rows - 1) // rows
    tile_end = jnp.cumsum(tiles)
    tile_start = tile_end - tiles
    n_tiles = tile_end[-1:]
    pad_tile = jnp.where(cnt % rows != 0, tile_end - 1, -1).astype(jnp.int32)
    offs_b = jnp.broadcast_to((tile_start * rows).astype(F32)[:, None], (N_EXPERTS, DISPATCH_ROWS))
    return tile_start.astype(jnp.int32), tiles, n_tiles, pad_tile, offs_b, max_tiles * rows


def kernel(x, norm_mix_g, w_in, w_g2_f, b_g_f, w_g2_b, b_g_b, gla_norm_g, rpb, w_out, norm_ffn_g, w_grp, b_grp,
           w_exp, b_exp, w_gate, w_up, w_down, final_norm_g):
    batch, seq, d = x.shape
    n = batch * seq
    rows = seq // GRID_W
    depth = w_in.shape[0]
    xf = x.reshape(n, d)
    c_na = 3 * NA_WIDTH
    c_qk = c_na + 2 * GLA_KEY_WIDTH
    c_v = c_qk + GLA_VAL_WIDTH
    c_r = c_v + GLA_VAL_WIDTH

    q_scale = jnp.concatenate([jnp.full((NA_WIDTH,), NA_HEAD_DIM ** -0.5, F32), jnp.ones((2 * NA_WIDTH,), F32)])
    w_na = (w_in[:, :, :c_na] * q_scale).astype(BF16)
    qk_scale = jnp.concatenate([jnp.full((GLA_KEY_WIDTH,), GLA_DK ** -0.5, F32), jnp.ones((GLA_KEY_WIDTH,), F32)])
    w_qk = (w_in[:, :, c_na:c_qk] * qk_scale).astype(BF16)
    w_v = w_in[:, :, c_qk:c_v].astype(BF16)
    w_r = w_in[:, :, c_v:c_r].astype(BF16)
    w_lr = jnp.pad(w_in[:, :, c_r:], ((0, 0), (0, 0), (0, LANES - 2 * GLA_GATE_RANK)))
    zero_g2 = jnp.zeros_like(w_g2_f)
    w_g2 = jnp.concatenate([jnp.concatenate([w_g2_f, zero_g2], axis=2),
                            jnp.concatenate([zero_g2, w_g2_b], axis=2)], axis=1)
    w_g2 = jnp.pad(w_g2, ((0, 0), (0, LANES - 2 * GLA_GATE_RANK), (0, 0)))
    b_g = jnp.concatenate([b_g_f, b_g_b], axis=1)[:, None, :]
    bias_tbl = jax.vmap(functools.partial(_na_bias_table, rows=rows))(rpb)
    w_o = w_out.astype(BF16)
    w_o_na, w_o_gla = w_o[:, :NA_WIDTH], w_o[:, NA_WIDTH:]
    w_rt = jnp.pad(jnp.concatenate([w_exp, w_grp], axis=2), ((0, 0), (0, 0), (0, LANES - N_EXPERTS - N_GROUPS)))
    w_rt_hi = w_rt.astype(BF16)
    w_rt_lo = (w_rt - w_rt_hi.astype(F32)).astype(BF16)
    w_rt3 = jnp.concatenate([w_rt_hi, w_rt_hi, w_rt_lo], axis=1).transpose(0, 2, 1)
    b_rt = jnp.pad(jnp.concatenate([b_exp, b_grp], axis=1),
                   ((0, 0), (0, ROUTE_ROWS - N_EXPERTS - N_GROUPS)))[:, :, None]
    per_layer_row = lambda a: a[:, None, :]

    in_proj_params = (per_layer_row(norm_mix_g), w_na, w_qk, w_v, w_r, w_lr, w_g2, b_g)
    proj = _in_proj(xf, 0, *in_proj_params)
    for l in range(depth):
        na_qkv, gqk, gv, gr, gates = proj
        y_na = _na(na_qkv, bias_tbl, l, batch, rows)
        o_f, o_b = _gla(gqk, gv, gates, batch, seq)
        x1, h2_slabs, rt_t, counts = _out_route(y_na, o_f, o_b, gr, xf, l, w_o_na, w_o_gla,
                                                per_layer_row(gla_norm_g), per_layer_row(norm_ffn_g), w_rt3, b_rt)

        tile_start, tiles, n_tiles, pad_tile, offs_b, sorted_rows = _dispatch_plan(counts, n)
        pos, rt, xs = _dispatch(pad_tile, n_tiles, rt_t, offs_b, h2_slabs, sorted_rows)
        ys = _experts(tile_start, tiles, xs, w_gate, w_up, w_down, l)
        pos = pos.reshape(-1)
        if l + 1 < depth:
            xf, proj = _combine_in_proj(pos, ys, x1, rt, l + 1, *in_proj_params)
    return _combine_final(pos, ys, x1, rt, final_norm_g[None, :]).reshape(batch, seq, d)
```

```python
import functools

import jax
import jax.numpy as jnp
from jax import lax
from jax.experimental import pallas as pl
from jax.experimental.pallas import tpu as pltpu

F32 = jnp.float32
BF16 = jnp.bfloat16

GRID_W = 64
NA_HEADS = 8
NA_HEAD_DIM = 64
NA_WIDTH = NA_HEADS * NA_HEAD_DIM
WIN_H_MAX = 8
WIN_W = 16
GLA_HEADS = 4
GLA_DK = 64
GLA_DV = 128
GLA_KEY_WIDTH = GLA_HEADS * GLA_DK
GLA_VAL_WIDTH = GLA_HEADS * GLA_DV
GLA_GATE_RANK = 16
GLA_GATE_NORMALIZER = 16.0
GLA_CHUNK = 64
N_GROUPS = 4
EXPERTS_PER_GROUP = 8
N_EXPERTS = N_GROUPS * EXPERTS_PER_GROUP
RMS_EPS = 1e-6

LANES = 128
SUBLANES = 8
MXU_WIDTH = 256
VMEM_LIMIT_BYTES = 56 * 1024 * 1024

MASK_VALUE = -1e30

IN_PROJ_ROWS = 512
NA_ROW_BLOCK = 8
NA_ROW_UNROLL = 8
GLA_STEP_CHUNKS = 4
OUT_ROWS = 512
EXPERT_ROWS = 256
EXPERT_TILE_BUFFERS = 4
DISPATCH_ROWS = 512
COMBINE_ROWS = DISPATCH_ROWS
FUSED_ROWS = 512


def _params(*sem):
    return pltpu.CompilerParams(dimension_semantics=sem, vmem_limit_bytes=VMEM_LIMIT_BYTES)


def _rms(x, g):
    return x * lax.rsqrt(jnp.mean(x * x, axis=-1, keepdims=True) + RMS_EPS) * g


def _in_proj_rows(x, g_ref, w_na_ref, w_qk_ref, w_v_ref, w_r_ref, w_lr_ref,
                  w_g2_ref, b_g_ref, na_ref, qk_ref, v_ref, r_ref, gate_ref, w_gate):
    @pl.when(pl.program_id(0) == 0)
    def _():
        w_gate[...] = jnp.dot(w_lr_ref[...], w_g2_ref[...], preferred_element_type=F32,
                              precision=lax.Precision.HIGHEST).astype(BF16)

    h = _rms(x, g_ref[...]).astype(BF16)
    na_ref[...] = jnp.dot(h, w_na_ref[...], preferred_element_type=F32).astype(BF16)
    qk_ref[...] = jnp.dot(h, w_qk_ref[...], preferred_element_type=F32)
    v_ref[...] = jnp.dot(h, w_v_ref[...], preferred_element_type=F32).astype(BF16)
    r_ref[...] = jnp.dot(h, w_r_ref[...], preferred_element_type=F32)
    z = jnp.dot(h, w_gate[...], preferred_element_type=F32) + b_g_ref[...]
    log_sig = jnp.minimum(z, 0.0) - jnp.log(1.0 + jnp.exp(-jnp.abs(z)))
    gate_ref[...] = log_sig * (1.0 / GLA_GATE_NORMALIZER)


def _in_proj_kernel(x_ref, *refs):
    _in_proj_rows(x_ref[...], *refs)


def _in_proj_outputs(n, rows):
    row_spec = lambda width: pl.BlockSpec((rows, width), lambda i, *_: (i, 0))
    widths = (3 * NA_WIDTH, 2 * GLA_KEY_WIDTH, GLA_VAL_WIDTH, GLA_VAL_WIDTH, 2 * GLA_KEY_WIDTH)
    dtypes = (BF16, F32, BF16, F32, F32)
    return ([row_spec(w) for w in widths],
            [jax.ShapeDtypeStruct((n, w), t) for w, t in zip(widths, dtypes)])


def _layer_spec(a, layer):
    return pl.BlockSpec((None,) + a.shape[1:], lambda *_: (layer,) + (0,) * (a.ndim - 1))


def _in_proj(x, layer, g, w_na, w_qk, w_v, w_r, w_lr, w_g2, b_g):
    n, d = x.shape
    rows = IN_PROJ_ROWS
    params = (g, w_na, w_qk, w_v, w_r, w_lr, w_g2, b_g)
    out_specs, out_shape = _in_proj_outputs(n, rows)
    return pl.pallas_call(
        _in_proj_kernel,
        grid=(n // rows,),
        in_specs=[pl.BlockSpec((rows, d), lambda i: (i, 0))] + [_layer_spec(a, layer) for a in params],
        out_specs=out_specs,
        out_shape=out_shape,
        scratch_shapes=[pltpu.VMEM((d, 2 * GLA_KEY_WIDTH), BF16)],
        compiler_params=_params("arbitrary"),
        name="in_proj",
    )(x, *params)


def _na_bias_table(rpb, rows):
    kh = min(WIN_H_MAX, rows)
    w = jnp.arange(GRID_W)[:, None, None]
    x = jnp.arange(GRID_W)[None, :, None]
    cs = jnp.clip(w - WIN_W // 2, 0, GRID_W - WIN_W)
    valid = (x >= cs) & (x < cs + WIN_W)
    col_sel = (valid & (jnp.arange(2 * WIN_W - 1)[None, None, :] == x - w + (WIN_W - 1))).astype(F32)
    by_col = jnp.einsum("hrc,wxc->hwrx", rpb.astype(F32), col_sel, precision=lax.Precision.HIGHEST)
    by_col = jnp.where(valid[None, :, None, :, 0], by_col, MASK_VALUE)
    by_col = by_col.reshape(NA_HEADS * GRID_W, (2 * WIN_H_MAX - 1) * GRID_W)
    bias = jnp.stack([by_col[:, (WIN_H_MAX - 1 - c) * GRID_W:(WIN_H_MAX - 1 - c + kh) * GRID_W]
                      for c in range(kh)])
    return bias.reshape(kh, NA_HEADS // 2, 2 * GRID_W, kh * GRID_W)


def _na_window_start(j, rows, kh):
    return jnp.clip(j * NA_ROW_BLOCK - kh // 2, 0, rows - (NA_ROW_BLOCK + kh))


def _na_kernel(q_ref, k_ref, v_ref, tbl_ref, o_ref, *, rows, kh):
    rb = NA_ROW_BLOCK
    j = pl.program_id(1)
    fetched = _na_window_start(j, rows, kh)
    lane = lax.broadcasted_iota(jnp.int32, (GRID_W, LANES), 1)
    first = lane < NA_HEAD_DIM

    def row_body(lr, carry):
        r = j * rb + lr
        start = jnp.clip(r - kh // 2, 0, rows - kh)
        cls = r - start
        q_row = q_ref[lr]
        k_win = k_ref[0, pl.ds(start - fetched, kh)].reshape(kh * GRID_W, NA_WIDTH)
        v_win = v_ref[0, pl.ds(start - fetched, kh)].reshape(kh * GRID_W, NA_WIDTH)
        pairs = [slice(p * LANES, (p + 1) * LANES) for p in range(NA_HEADS // 2)]
        scores = []
        for sl in pairs:
            q_pair = q_row[:, sl]
            zero = jnp.zeros_like(q_pair)
            q_bd = jnp.concatenate([jnp.where(first, q_pair, zero), jnp.where(first, zero, q_pair)], axis=0)
            scores.append(lax.dot_general(q_bd, k_win[:, sl], (((1,), (1,)), ((), ())),
                                          preferred_element_type=F32))
        probs, denoms = [], []
        for p, s in enumerate(scores):
            s = s + tbl_ref[cls, p]
            e = jnp.exp(s - jnp.max(s, axis=-1, keepdims=True))
            denoms.append(jnp.sum(e, axis=-1, keepdims=True))
            probs.append(e.astype(BF16))
        outs = []
        for sl, e, denom in zip(pairs, probs, denoms):
            o = jnp.dot(e, v_win[:, sl], preferred_element_type=F32) / denom
            outs.append(jnp.where(first, o[:GRID_W], o[GRID_W:]))
        o_ref[lr] = jnp.concatenate(outs, axis=-1).astype(o_ref.dtype)
        return carry

    lax.fori_loop(0, rb, row_body, 0, unroll=NA_ROW_UNROLL)


def _na(na_qkv, tbl, layer, batch, rows):
    kh = min(WIN_H_MAX, rows)
    rb = NA_ROW_BLOCK
    nblk = rows // rb
    x4 = na_qkv.reshape(batch, rows, GRID_W, 3 * NA_WIDTH)
    assert rows >= rb + kh
    blk = (None, rb, GRID_W, NA_WIDTH)
    window = (pl.Element(1), pl.Element(rb + kh), pl.Element(GRID_W), pl.Element(NA_WIDTH))
    specs = [pl.BlockSpec(blk, lambda b, j: (b, j, 0, 0))]
    for col in (1, 2):
        specs.append(pl.BlockSpec(window, lambda b, j, col=col: (b, _na_window_start(j, rows, kh), 0,
                                                                  col * NA_WIDTH)))
    specs.append(_layer_spec(tbl, layer))
    out = pl.pallas_call(
        functools.partial(_na_kernel, rows=rows, kh=kh),
        grid=(batch, nblk),
        in_specs=specs,
        out_specs=pl.BlockSpec(blk, lambda b, j: (b, j, 0, 0)),
        out_shape=jax.ShapeDtypeStruct((batch, rows, GRID_W, NA_WIDTH), BF16),
        compiler_params=_params("arbitrary", "arbitrary"),
        name="na",
    )(x4, x4, x4, tbl)
    return out.reshape(batch * rows * GRID_W, NA_WIDTH)


def _block_diag_mask(row_block, col_block, nblocks):
    shape = (row_block * nblocks, col_block * nblocks)
    r = lax.shift_right_logical(lax.broadcasted_iota(jnp.int32, shape, 0), row_block.bit_length() - 1)
    c = lax.shift_right_logical(lax.broadcasted_iota(jnp.int32, shape, 1), col_block.bit_length() - 1)
    return r == c


def _split_bf16x3(x):
    hi = x.astype(BF16)
    rest = x - hi.astype(F32)
    mid = rest.astype(BF16)
    lo = (rest - mid.astype(F32)).astype(BF16)
    return hi, mid, lo


def _gla_direction(qk_ref, v_ref, g_ref, *, backward):
    c = GLA_CHUNK
    nc = GLA_STEP_CHUNKS
    kw = GLA_KEY_WIDTH
    order = list(reversed(range(nc))) if backward else list(range(nc))
    chunk = lambda a, n: a[n * c:(n + 1) * c]

    def stage_cumsum():
        step = nc * c
        ti = lax.broadcasted_iota(jnp.int32, (step, step), 0)
        tj = lax.broadcasted_iota(jnp.int32, (step, step), 1)
        same_chunk = lax.shift_right_logical(ti, c.bit_length() - 1) == lax.shift_right_logical(tj, c.bit_length() - 1)
        tri = (tj >= ti) if backward else (tj <= ti)
        cum = jnp.where(same_chunk & tri, 1.0, 0.0).astype(BF16)
        return sum(jnp.dot(cum, piece, preferred_element_type=F32) for piece in _split_bf16x3(g_ref[...]))

    def stage_decays(b):
        ref_row, last_row = (c // 2, 0) if backward else (c // 2 - 1, c - 1)
        rows_of = lambda r: jnp.concatenate(
            [jnp.broadcast_to(b[n * c + r:n * c + r + 1, :], (c, kw)) for n in range(nc)], axis=0)
        b_ref, b_last = rows_of(ref_row), rows_of(last_row)
        q = qk_ref[:, :kw]
        k = qk_ref[:, kw:]
        q_rel = (q * jnp.exp(b - b_ref)).astype(BF16)
        k_rel = (k * jnp.exp(b_ref - b)).astype(BF16)
        k_dec = k * jnp.exp(b_last - b)
        q_dec = (q * jnp.exp(b)).astype(BF16)
        decay = jnp.exp(b_last)
        return q_rel, k_rel, k_dec, q_dec, decay

    def stage_scores(q_rel, k_rel):
        kk_mask = _block_diag_mask(c, GLA_DK, GLA_HEADS)
        out = []
        for n in range(nc):
            k_bd = jnp.where(kk_mask, jnp.concatenate([chunk(k_rel, n)] * GLA_HEADS, axis=0), jnp.zeros((), BF16))
            out.append(lax.dot_general(chunk(q_rel, n), k_bd, (((1,), (1,)), ((), ())),
                                       preferred_element_type=F32))
        return out

    def stage_intra(scores, k_dec, decay):
        si = lax.broadcasted_iota(jnp.int32, (c, c * GLA_HEADS), 0)
        sj = lax.broadcasted_iota(jnp.int32, (c, c * GLA_HEADS), 1) & (c - 1)
        keep = (sj > si) if backward else (sj <= si)
        kv_mask = _block_diag_mask(c, GLA_DV, GLA_HEADS)
        vk_mask = _block_diag_mask(GLA_DK, GLA_DV, 2)
        o_intra, upd, decay_col = [], [], []
        for n in range(nc):
            v = v_ref[n * c:(n + 1) * c, :]
            p = jnp.where(keep, scores[n], 0.0).astype(BF16)
            v_bd = jnp.where(kv_mask, jnp.concatenate([v] * GLA_HEADS, axis=0), jnp.zeros((), BF16))
            o_intra.append(jnp.dot(p, v_bd, preferred_element_type=F32))
            k_dec_t = chunk(k_dec, n).T.astype(BF16)
            pairs = []
            for hp in range(GLA_HEADS // 2):
                kv = jnp.dot(k_dec_t[hp * 2 * GLA_DK:(hp + 1) * 2 * GLA_DK],
                             v[:, hp * 2 * GLA_DV:(hp + 1) * 2 * GLA_DV], preferred_element_type=F32)
                pairs.append(jnp.where(vk_mask, kv, 0.0))
            upd.append(pairs)
            decay_col.append(chunk(decay, n).T[:, :1])
        return o_intra, upd, decay_col

    def scan_step(state, idx, q_dec, o_intra, upd, decay_col, out_ref):
        n = order[idx]
        q_n = chunk(q_dec, n)
        pw = 2 * GLA_DK
        o_inter = jnp.concatenate(
            [jnp.dot(q_n[:, hp * pw:(hp + 1) * pw], s.astype(BF16), preferred_element_type=F32)
             for hp, s in enumerate(state)], axis=1)
        out_ref[n * c:(n + 1) * c, :] = o_intra[n] + o_inter
        return [s * decay_col[n][hp * pw:(hp + 1) * pw] + u for hp, (s, u) in enumerate(zip(state, upd[n]))]

    return stage_cumsum, stage_decays, stage_scores, stage_intra, scan_step


def _gla_kernel(qk_f_ref, v_f_ref, g_f_ref, qk_b_ref, v_b_ref, g_b_ref, o_f_ref, o_b_ref,
                st_f, st_b):
    @pl.when(pl.program_id(1) == 0)
    def _():
        st_f[...] = jnp.zeros_like(st_f)
        st_b[...] = jnp.zeros_like(st_b)

    dirs = (_gla_direction(qk_f_ref, v_f_ref, g_f_ref, backward=False),
            _gla_direction(qk_b_ref, v_b_ref, g_b_ref, backward=True))
    b = [d[0]() for d in dirs]
    dec = [d[1](x) for d, x in zip(dirs, b)]
    scores = [d[2](x[0], x[1]) for d, x in zip(dirs, dec)]
    intra = [d[3](s, x[2], x[4]) for d, s, x in zip(dirs, scores, dec)]
    pairs = range(GLA_HEADS // 2)
    states = [[st[hp] for hp in pairs] for st in (st_f, st_b)]
    for idx in range(GLA_STEP_CHUNKS):
        for j, (d, x, y, out_ref) in enumerate(zip(dirs, dec, intra, (o_f_ref, o_b_ref))):
            states[j] = d[4](states[j], idx, x[3], *y, out_ref)
    for st, state in zip((st_f, st_b), states):
        for hp in pairs:
            st[hp] = state[hp]


def _gla(qk, v, gates, batch, seq):
    step = GLA_STEP_CHUNKS * GLA_CHUNK
    nblk = seq // step
    qk3 = qk.reshape(batch, seq, 2 * GLA_KEY_WIDTH)
    v3 = v.reshape(batch, seq, GLA_VAL_WIDTH)
    g3 = gates.reshape(batch, seq, 2 * GLA_KEY_WIDTH)
    fwd = lambda b, n: (b, n, 0)
    bwd = lambda b, n: (b, nblk - 1 - n, 0)
    bwd_gate = lambda b, n: (b, nblk - 1 - n, 1)
    o_f, o_b = pl.pallas_call(
        _gla_kernel,
        grid=(batch, nblk),
        in_specs=[pl.BlockSpec((None, step, 2 * GLA_KEY_WIDTH), fwd),
                  pl.BlockSpec((None, step, GLA_VAL_WIDTH), fwd),
                  pl.BlockSpec((None, step, GLA_KEY_WIDTH), fwd),
                  pl.BlockSpec((None, step, 2 * GLA_KEY_WIDTH), bwd),
                  pl.BlockSpec((None, step, GLA_VAL_WIDTH), bwd),
                  pl.BlockSpec((None, step, GLA_KEY_WIDTH), bwd_gate)],
        out_specs=[pl.BlockSpec((None, step, GLA_VAL_WIDTH), fwd),
                   pl.BlockSpec((None, step, GLA_VAL_WIDTH), bwd)],
        out_shape=[jax.ShapeDtypeStruct((batch, seq, GLA_VAL_WIDTH), F32)] * 2,
        scratch_shapes=[pltpu.VMEM((GLA_HEADS // 2, 2 * GLA_DK, 2 * GLA_DV), F32),
                        pltpu.VMEM((GLA_HEADS // 2, 2 * GLA_DK, 2 * GLA_DV), F32)],
        compiler_params=_params("arbitrary", "arbitrary"),
        name="gla",
    )(qk3, v3, g3, qk3, v3, g3)
    return o_f.reshape(batch * seq, GLA_VAL_WIDTH), o_b.reshape(batch * seq, GLA_VAL_WIDTH)


def _store_slabs(ref, x):
    rows = x.shape[0]
    for s in range(SUBLANES):
        ref[pl.ds(s, rows, stride=SUBLANES), :] = x[:, s * LANES:(s + 1) * LANES]


def _load_slabs(ref, rows):
    return jnp.concatenate([ref[pl.ds(s, rows, stride=SUBLANES), :] for s in range(SUBLANES)], axis=1)


def _slab_copy(src_hbm, dst, sem, src_row, k):
    dst_row = k * SUBLANES if isinstance(k, int) else pl.multiple_of(k * SUBLANES, SUBLANES)
    return pltpu.make_async_copy(src_hbm.at[pl.ds(pl.multiple_of(src_row, SUBLANES), SUBLANES), :],
                                 dst.at[pl.ds(dst_row, SUBLANES), :], sem)


def _wait_slabs(src_hbm, dst, sem):
    pltpu.make_async_copy(src_hbm.at[pl.ds(0, dst.shape[0]), :], dst, sem).wait()


RT_E1, RT_E2, RT_RANK1, RT_RANK2, RT_W1, RT_W2 = range(6)
GROUP_ROW0 = N_EXPERTS
ROUTE_ROWS = 64


def _out_route_kernel(na_ref, of_ref, ob_ref, r_ref, x_ref, w_na_ref, w_gla_ref, gn_ref, fn_ref,
                      w_rt_ref, b_rt_ref, x1_ref, h2_ref, rt_ref, cnt_ref, carry):
    @pl.when(pl.program_id(0) == 0)
    def _():
        carry[...] = jnp.zeros_like(carry)

    rows = x_ref.shape[0]
    o = of_ref[...] + ob_ref[...]
    r = r_ref[...]
    parts = []
    for h in range(GLA_HEADS):
        sl = slice(h * GLA_DV, (h + 1) * GLA_DV)
        parts.append(_rms(o[:, sl], gn_ref[...]) * (r[:, sl] * jax.nn.sigmoid(r[:, sl])))
    y_gla = jnp.concatenate(parts, axis=-1).astype(BF16)
    x1 = (x_ref[...] + jnp.dot(na_ref[...], w_na_ref[...], preferred_element_type=F32)
          + jnp.dot(y_gla, w_gla_ref[...], preferred_element_type=F32))
    x1_ref[...] = x1
    h2 = _rms(x1, fn_ref[...])
    _store_slabs(h2_ref, h2)

    h_hi = h2.astype(BF16)
    h_lo = (h2 - h_hi.astype(F32)).astype(BF16)
    logits = lax.dot_general(w_rt_ref[...], jnp.concatenate([h_hi, h_lo, h_hi], axis=1),
                             (((1,), (1,)), ((), ())), preferred_element_type=F32)
    logits = logits[:ROUTE_ROWS] + b_rt_ref[...]
    row_i = lax.broadcasted_iota(jnp.int32, (ROUTE_ROWS, rows), 0)
    row = row_i.astype(F32)
    row_grp = lax.shift_right_logical(row_i, EXPERTS_PER_GROUP.bit_length() - 1).astype(F32)
    neg = jnp.float32(-jnp.inf)
    no_row = jnp.float32(ROUTE_ROWS)
    is_grp = (row_i >= GROUP_ROW0) & (row_i < GROUP_ROW0 + N_GROUPS)
    g_logit = jnp.where(is_grp, logits, neg)
    g_max = jnp.max(g_logit, axis=0, keepdims=True)
    g_sel = jnp.min(jnp.where(is_grp & (g_logit == g_max), row, no_row), axis=0, keepdims=True) - GROUP_ROW0
    grp_w = 1.0 / jnp.sum(jnp.where(is_grp, jnp.exp(g_logit - g_max), 0.0), axis=0, keepdims=True)
    in_grp = (row_i < N_EXPERTS) & (row_grp == g_sel)
    e_logit = jnp.where(in_grp, logits, neg)
    v1 = jnp.max(e_logit, axis=0, keepdims=True)
    i1 = jnp.min(jnp.where(in_grp & (e_logit == v1), row, no_row), axis=0, keepdims=True)
    rest = in_grp & (row != i1)
    e_logit2 = jnp.where(rest, logits, neg)
    v2 = jnp.max(e_logit2, axis=0, keepdims=True)
    i2 = jnp.min(jnp.where(rest & (e_logit2 == v2), row, no_row), axis=0, keepdims=True)
    t = jnp.exp(v2 - v1)
    w1 = grp_w / (1.0 + t)
    w2 = grp_w * t / (1.0 + t)

    sel1 = row == i1
    sel2 = row == i2
    onehot = jnp.where(sel1 | sel2, 1.0, 0.0)
    ti = lax.broadcasted_iota(jnp.int32, (rows, rows), 0)
    tj = lax.broadcasted_iota(jnp.int32, (rows, rows), 1)
    earlier = jnp.where(ti < tj, 1.0, 0.0).astype(BF16)
    ranks = jnp.dot(onehot.astype(BF16), earlier, preferred_element_type=F32) + carry[:, :1]
    rank1 = jnp.sum(jnp.where(sel1, ranks, 0.0), axis=0, keepdims=True)
    rank2 = jnp.sum(jnp.where(sel2, ranks, 0.0), axis=0, keepdims=True)
    new_carry = carry[...] + jnp.sum(onehot, axis=1, keepdims=True)
    carry[...] = new_carry
    cnt_ref[...] = new_carry

    fields = {RT_E1: i1, RT_E2: i2, RT_RANK1: rank1, RT_RANK2: rank2, RT_W1: w1, RT_W2: w2}
    zero = jnp.zeros_like(w1)
    rt_ref[...] = jnp.concatenate([fields.get(f, zero) for f in range(SUBLANES)], axis=0)


def _out_route(y_na, o_f, o_b, r, x, layer, w_na, w_gla, gn, fn, w_rt, b_rt):
    n, d = x.shape
    rows = OUT_ROWS
    row_spec = lambda width: pl.BlockSpec((rows, width), lambda i: (i, 0))
    full = lambda a: _layer_spec(a, layer)
    return pl.pallas_call(
        _out_route_kernel,
        grid=(n // rows,),
        in_specs=[row_spec(NA_WIDTH), row_spec(GLA_VAL_WIDTH), row_spec(GLA_VAL_WIDTH),
                  row_spec(GLA_VAL_WIDTH), row_spec(d), full(w_na), full(w_gla), full(gn), full(fn),
                  full(w_rt), full(b_rt)],
        out_specs=[row_spec(d), pl.BlockSpec((rows * SUBLANES, LANES), lambda i: (i, 0)),
                   pl.BlockSpec((SUBLANES, rows), lambda i: (0, i)),
                   pl.BlockSpec((ROUTE_ROWS, LANES), lambda i: (0, 0))],
        out_shape=[jax.ShapeDtypeStruct((n, d), F32), jax.ShapeDtypeStruct((n * SUBLANES, LANES), F32),
                   jax.ShapeDtypeStruct((SUBLANES, n), F32), jax.ShapeDtypeStruct((ROUTE_ROWS, LANES), F32)],
        scratch_shapes=[pltpu.VMEM((ROUTE_ROWS, LANES), F32)],
        compiler_params=_params("arbitrary"),
        name="out_route",
    )(y_na, o_f, o_b, r, x, w_na, w_gla, gn, fn, w_rt, b_rt)


def _dispatch_kernel(pad_tile_ref, n_tiles_ref, rt_ref, offs_ref, h_hbm, pos_ref, rec_ref, xs_hbm,
                     zeros, hbuf, pos_vmem, pos_smem, sem_zero, sem_pos, sem_in, sem):
    i = pl.program_id(0)
    n_steps = pl.num_programs(0)
    db = DISPATCH_ROWS
    tile = EXPERT_ROWS * SUBLANES
    max_tiles = xs_hbm.shape[0] // tile
    buf = lax.rem(i, 2)

    def block_copy(step, b):
        rows = pl.ds(pl.multiple_of(step * db * SUBLANES, db * SUBLANES), db * SUBLANES)
        return pltpu.make_async_copy(h_hbm.at[rows, :], hbuf.at[b], sem_in.at[b])

    def wait_row_copies(b):
        for _ in range(2):
            pltpu.make_async_copy(hbuf.at[b], xs_hbm.at[pl.ds(0, db * SUBLANES), :], sem).wait()

    def zero_copy(t):
        return pltpu.make_async_copy(zeros, xs_hbm.at[pl.ds(pl.multiple_of(t * tile, tile), tile), :], sem_zero)

    def for_each_zero_tile(fn):
        def padded(e, carry):
            @pl.when(pad_tile_ref[e] >= 0)
            def _():
                fn(zero_copy(pad_tile_ref[e]))
            return carry
        lax.fori_loop(0, N_EXPERTS, padded, 0)

        def tail(t, carry):
            fn(zero_copy(t))
            return carry
        lax.fori_loop(n_tiles_ref[0], max_tiles, tail, 0)

    @pl.when(i == 0)
    def _():
        block_copy(0, 0).start()
        zeros[...] = jnp.zeros_like(zeros)
        for_each_zero_tile(lambda cp: cp.start())

    rt_t = rt_ref[...]
    rec_ref[...] = jnp.concatenate([rt_t, jnp.zeros((LANES - SUBLANES, db), F32)], axis=0).T
    expert = lax.broadcasted_iota(jnp.int32, (N_EXPERTS, db), 0).astype(F32)
    pos = []
    for e_lane, rank_lane in ((RT_E1, RT_RANK1), (RT_E2, RT_RANK2)):
        seg = jnp.sum(jnp.where(expert == rt_t[e_lane:e_lane + 1, :], offs_ref[...], 0.0), axis=0, keepdims=True)
        pos.append((seg + rt_t[rank_lane:rank_lane + 1, :]) * SUBLANES)
    pos = jnp.concatenate(pos, axis=0).astype(jnp.int32)
    pos_ref[...] = pos
    pos_vmem[...] = jnp.concatenate([pos, jnp.zeros((SUBLANES - 2, db), jnp.int32)], axis=0)
    to_smem = pltpu.make_async_copy(pos_vmem, pos_smem, sem_pos)
    to_smem.start()

    @pl.when(i == 0)
    def _():
        for_each_zero_tile(lambda cp: cp.wait())

    @pl.when(i > 0)
    def _():
        wait_row_copies(1 - buf)

    to_smem.wait()
    block_copy(i, buf).wait()

    @pl.when(i + 1 < n_steps)
    def _():
        block_copy(i + 1, 1 - buf).start()

    h_blk = hbuf.at[buf]
    for k in range(db):
        src = h_blk.at[pl.ds(k * SUBLANES, SUBLANES), :]
        for slot in range(2):
            dst_row = pl.multiple_of(pos_smem[slot, k], SUBLANES)
            pltpu.make_async_copy(src, xs_hbm.at[pl.ds(dst_row, SUBLANES), :], sem).start(priority=slot)

    @pl.when(i == n_steps - 1)
    def _():
        wait_row_copies(buf)


def _dispatch(pad_tile, n_tiles, rt_t, offs_b, h2_slabs, sorted_rows):
    n = rt_t.shape[1]
    db = DISPATCH_ROWS
    grid_spec = pltpu.PrefetchScalarGridSpec(
        num_scalar_prefetch=2,
        grid=(n // db,),
        in_specs=[pl.BlockSpec((SUBLANES, db), lambda i, pt, nt: (0, i)),
                  pl.BlockSpec(offs_b.shape, lambda i, pt, nt: (0, 0)),
                  pl.BlockSpec(memory_space=pl.ANY)],
        out_specs=[pl.BlockSpec((None, 2, db), lambda i, pt, nt: (i, 0, 0)),
                   pl.BlockSpec((db, LANES), lambda i, pt, nt: (i, 0)),
                   pl.BlockSpec(memory_space=pl.ANY)],
        scratch_shapes=[pltpu.VMEM((EXPERT_ROWS * SUBLANES, LANES), F32),
                        pltpu.VMEM((2, db * SUBLANES, LANES), F32),
                        pltpu.VMEM((SUBLANES, db), jnp.int32), pltpu.SMEM((SUBLANES, db), jnp.int32),
                        pltpu.SemaphoreType.DMA(()), pltpu.SemaphoreType.DMA(()),
                        pltpu.SemaphoreType.DMA((2,)), pltpu.SemaphoreType.DMA(())],
    )
    return pl.pallas_call(
        _dispatch_kernel,
        grid_spec=grid_spec,
        out_shape=[jax.ShapeDtypeStruct((n // db, 2, db), jnp.int32),
                   jax.ShapeDtypeStruct((n, LANES), F32),
                   jax.ShapeDtypeStruct((sorted_rows * SUBLANES, LANES), F32)],
        compiler_params=_params("arbitrary"),
        name="dispatch",
    )(pad_tile, n_tiles, rt_t, offs_b, h2_slabs)


TILE_DMA_PRIORITY = 1


def _expert_tile(x_ref, y_ref, wg_s, wu_s, wd_s):
    rows = EXPERT_ROWS
    x = _load_slabs(x_ref, rows).astype(BF16)
    hid = []
    for c in range(0, wg_s.shape[1], MXU_WIDTH):
        gate = jnp.dot(x, wg_s[:, c:c + MXU_WIDTH], preferred_element_type=F32)
        up = jnp.dot(x, wu_s[:, c:c + MXU_WIDTH], preferred_element_type=F32)
        hid.append((gate * jax.nn.sigmoid(gate) * up).astype(BF16))
    hid = jnp.concatenate(hid, axis=1)
    for c in range(0, wd_s.shape[1], MXU_WIDTH):
        y = jnp.dot(hid, wd_s[:, c:c + MXU_WIDTH], preferred_element_type=F32)
        for s in range(c // LANES, (c + MXU_WIDTH) // LANES):
            y_ref[pl.ds(s, rows, stride=SUBLANES), :] = y[:, s * LANES - c:(s + 1) * LANES - c]


def _experts_kernel(tile_start_ref, tiles_ref, x_hbm, wg_hbm, wu_hbm, wd_hbm, y_hbm,
                    xbuf, ybuf, wg_f, wu_f, wd_f, wg_s, wu_s, wd_s, sem_x, sem_y, sem_w, *, layer):
    nbuf = EXPERT_TILE_BUFFERS
    e = pl.program_id(0)
    last_step = e == pl.num_programs(0) - 1
    tile = EXPERT_ROWS * SUBLANES
    n = tiles_ref[e]
    base = tile_start_ref[e]
    n_tiles = tile_start_ref[N_EXPERTS - 1] + tiles_ref[N_EXPERTS - 1]
    tile_rows = lambda g: pl.ds(pl.multiple_of(g * tile, tile), tile)

    def x_copy(g, slot):
        return pltpu.make_async_copy(x_hbm.at[tile_rows(g), :], xbuf.at[slot], sem_x.at[slot])

    def y_copy(g, slot):
        return pltpu.make_async_copy(ybuf.at[slot], y_hbm.at[tile_rows(g), :], sem_y.at[slot])

    def w_copies(expert, slot):
        return [pltpu.make_async_copy(w_hbm.at[layer, expert], stage.at[slot], sem_w.at[slot])
                for w_hbm, stage in ((wg_hbm, wg_f), (wu_hbm, wu_f), (wd_hbm, wd_f))]

    w_slot = lax.rem(e, 2)

    @pl.when(e == 0)
    def _():
        for g in range(nbuf - 1):
            @pl.when(g < n_tiles)
            def _():
                x_copy(g, g).start(priority=TILE_DMA_PRIORITY)
        for cp in w_copies(0, 0):
            cp.start()

    for cp in w_copies(e, w_slot):
        cp.wait()

    @pl.when(jnp.logical_not(last_step))
    def _():
        for cp in w_copies(e + 1, 1 - w_slot):
            cp.start()

    @pl.when(n > 0)
    def _():
        wg_s[...] = wg_f[w_slot].astype(BF16)
        wu_s[...] = wu_f[w_slot].astype(BF16)
        wd_s[...] = wd_f[w_slot].astype(BF16)

        def tile_body(g, carry):
            slot = lax.rem(g, nbuf)
            x_copy(g, slot).wait()

            @pl.when(g + nbuf - 1 < n_tiles)
            def _():
                x_copy(g + nbuf - 1, lax.rem(g + nbuf - 1, nbuf)).start(priority=TILE_DMA_PRIORITY)

            @pl.when(g >= nbuf)
            def _():
                y_copy(g - nbuf, slot).wait()

            _expert_tile(xbuf.at[slot], ybuf.at[slot], wg_s, wu_s, wd_s)
            y_copy(g, slot).start(priority=TILE_DMA_PRIORITY)
            return carry
        lax.fori_loop(base, base + n, tile_body, 0)

    @pl.when(last_step)
    def _():
        for back in range(nbuf, 0, -1):
            @pl.when(n_tiles >= back)
            def _():
                y_copy(n_tiles - back, lax.rem(n_tiles - back, nbuf)).wait()

        first_unused = n_tiles
        max_tiles = y_hbm.shape[0] // tile
        ybuf[0] = jnp.zeros(ybuf.shape[1:], ybuf.dtype)
        zero_copy = lambda t: pltpu.make_async_copy(
            ybuf.at[0], y_hbm.at[pl.ds(pl.multiple_of(t * tile, tile), tile), :], sem_y.at[0])

        def start(t, carry):
            zero_copy(t).start()
            return carry

        def wait(t, carry):
            zero_copy(t).wait()
            return carry
        lax.fori_loop(first_unused, max_tiles, start, 0)
        lax.fori_loop(first_unused, max_tiles, wait, 0)


def _experts(tile_start, tiles, xs, wg, wu, wd, layer):
    rows = EXPERT_ROWS
    _, n_experts, d, dff = wg.shape
    assert d == SUBLANES * LANES and n_experts == N_EXPERTS
    nbuf = EXPERT_TILE_BUFFERS
    any_spec = pl.BlockSpec(memory_space=pl.ANY)
    grid_spec = pltpu.PrefetchScalarGridSpec(
        num_scalar_prefetch=2,
        grid=(n_experts,),
        in_specs=[any_spec, any_spec, any_spec, any_spec],
        out_specs=any_spec,
        scratch_shapes=[pltpu.VMEM((nbuf, rows * SUBLANES, LANES), F32),
                        pltpu.VMEM((nbuf, rows * SUBLANES, LANES), F32),
                        pltpu.VMEM((2, d, dff), F32), pltpu.VMEM((2, d, dff), F32), pltpu.VMEM((2, dff, d), F32),
                        pltpu.VMEM((d, dff), BF16), pltpu.VMEM((d, dff), BF16), pltpu.VMEM((dff, d), BF16),
                        pltpu.SemaphoreType.DMA((nbuf,)), pltpu.SemaphoreType.DMA((nbuf,)),
                        pltpu.SemaphoreType.DMA((2,))],
    )
    return pl.pallas_call(
        functools.partial(_experts_kernel, layer=layer),
        grid_spec=grid_spec,
        out_shape=jax.ShapeDtypeStruct(xs.shape, F32),
        compiler_params=_params("arbitrary"),
        name="experts",
    )(tile_start, tiles, xs, wg, wu, wd)


def _start_slab_gather(src_hbm, dst, sem, row_ref, base, count, priority):
    for k in range(count):
        _slab_copy(src_hbm, dst, sem, row_ref[base + k], k).start(priority=priority)


def _combined_rows(pos_ref, ys_hbm, x1_ref, rt_ref, buf, sem):
    i = pl.program_id(0)
    n_steps = pl.num_programs(0)
    rows = x1_ref.shape[0]
    steps_per_block = COMBINE_ROWS // rows
    slot = i % 2

    def start(step, s):
        first = lax.div(step, steps_per_block) * (2 * COMBINE_ROWS) + lax.rem(step, steps_per_block) * rows
        for j in range(2):
            _start_slab_gather(ys_hbm, buf.at[s, j], sem.at[s], pos_ref, first + j * COMBINE_ROWS, rows, priority=j)

    @pl.when(i == 0)
    def _():
        start(0, 0)

    @pl.when(i + 1 < n_steps)
    def _():
        start(i + 1, 1 - slot)

    _wait_slabs(ys_hbm, buf.at[slot, 0], sem.at[slot])
    _wait_slabs(ys_hbm, buf.at[slot, 1], sem.at[slot])
    rt = rt_ref[...]
    lane = lax.broadcasted_iota(jnp.int32, rt.shape, 1)
    w1 = jnp.sum(jnp.where(lane == RT_W1, rt, 0.0), axis=-1, keepdims=True)
    w2 = jnp.sum(jnp.where(lane == RT_W2, rt, 0.0), axis=-1, keepdims=True)
    y = w1 * _load_slabs(buf.at[slot, 0], rows) + w2 * _load_slabs(buf.at[slot, 1], rows)
    return x1_ref[...] + y


def _combine_scratch(rows):
    return [pltpu.VMEM((2, 2, rows * SUBLANES, LANES), F32), pltpu.SemaphoreType.DMA((2,))]


def _combine_final_kernel(pos_ref, ys_hbm, x1_ref, rt_ref, g_ref, o_ref, buf, sem):
    o_ref[...] = _rms(_combined_rows(pos_ref, ys_hbm, x1_ref, rt_ref, buf, sem), g_ref[...])


def _combine_final(pos, ys, x1, rt, g):
    n, d = x1.shape
    rows = COMBINE_ROWS
    grid_spec = pltpu.PrefetchScalarGridSpec(
        num_scalar_prefetch=1,
        grid=(n // rows,),
        in_specs=[pl.BlockSpec(memory_space=pl.ANY),
                  pl.BlockSpec((rows, d), lambda i, pos: (i, 0)),
                  pl.BlockSpec((rows, LANES), lambda i, pos: (i, 0)),
                  pl.BlockSpec(g.shape, lambda i, pos: (0, 0))],
        out_specs=pl.BlockSpec((rows, d), lambda i, pos: (i, 0)),
        scratch_shapes=_combine_scratch(rows),
    )
    return pl.pallas_call(
        _combine_final_kernel,
        grid_spec=grid_spec,
        out_shape=jax.ShapeDtypeStruct((n, d), F32),
        compiler_params=_params("arbitrary"),
        name="combine",
    )(pos, ys, x1, rt, g)


def _combine_in_proj_kernel(pos_ref, ys_hbm, x1_ref, rt_ref, *refs):
    param_refs, (x_ref, *proj_refs), (w_gate, buf, sem) = refs[:8], refs[8:14], refs[14:]
    x = _combined_rows(pos_ref, ys_hbm, x1_ref, rt_ref, buf, sem)
    x_ref[...] = x
    _in_proj_rows(x, *param_refs, *proj_refs, w_gate)


def _combine_in_proj(pos, ys, x1, rt, layer, g, w_na, w_qk, w_v, w_r, w_lr, w_g2, b_g):
    n, d = x1.shape
    rows = FUSED_ROWS
    params = (g, w_na, w_qk, w_v, w_r, w_lr, w_g2, b_g)
    proj_specs, proj_shapes = _in_proj_outputs(n, rows)
    grid_spec = pltpu.PrefetchScalarGridSpec(
        num_scalar_prefetch=1,
        grid=(n // rows,),
        in_specs=[pl.BlockSpec(memory_space=pl.ANY),
                  pl.BlockSpec((rows, d), lambda i, pos: (i, 0)),
                  pl.BlockSpec((rows, LANES), lambda i, pos: (i, 0))] + [_layer_spec(a, layer) for a in params],
        out_specs=[pl.BlockSpec((rows, d), lambda i, pos: (i, 0))] + proj_specs,
        scratch_shapes=[pltpu.VMEM((d, 2 * GLA_KEY_WIDTH), BF16)] + _combine_scratch(rows),
    )
    x, *proj = pl.pallas_call(
        _combine_in_proj_kernel,
        grid_spec=grid_spec,
        out_shape=[jax.ShapeDtypeStruct((n, d), F32)] + proj_shapes,
        compiler_params=_params("arbitrary"),
        name="combine_in_proj",
    )(pos, ys, x1, rt, *params)
    return x, proj


def _dispatch_plan(counts, n):
    rows = EXPERT_ROWS
    max_tiles = (2 * n) // rows + N_EXPERTS
    cnt = counts[:N_EXPERTS, 0].astype(jnp.int32)
    tiles = (cnt + rows - 1) // rows
    tile_end = jnp.cumsum(tiles)
    tile_start = tile_end - tiles
    n_tiles = tile_end[-1:]
    pad_tile = jnp.where(cnt % rows != 0, tile_end - 1, -1).astype(jnp.int32)
    offs_b = jnp.broadcast_to((tile_start * rows).astype(F32)[:, None], (N_EXPERTS, DISPATCH_ROWS))
    return tile_start.astype(jnp.int32), tiles, n_tiles, pad_tile, offs_b, max_tiles * rows


def kernel(x, norm_mix_g, w_in, w_g2_f, b_g_f, w_g2_b, b_g_b, gla_norm_g, rpb, w_out, norm_ffn_g, w_grp, b_grp,
           w_exp, b_exp, w_gate, w_up, w_down, final_norm_g):
    batch, seq, d = x.shape
    n = batch * seq
    rows = seq // GRID_W
    depth = w_in.shape[0]
    xf = x.reshape(n, d)
    c_na = 3 * NA_WIDTH
    c_qk = c_na + 2 * GLA_KEY_WIDTH
    c_v = c_qk + GLA_VAL_WIDTH
    c_r = c_v + GLA_VAL_WIDTH

    q_scale = jnp.concatenate([jnp.full((NA_WIDTH,), NA_HEAD_DIM ** -0.5, F32), jnp.ones((2 * NA_WIDTH,), F32)])
    w_na = (w_in[:, :, :c_na] * q_scale).astype(BF16)
    qk_scale = jnp.concatenate([jnp.full((GLA_KEY_WIDTH,), GLA_DK ** -0.5, F32), jnp.ones((GLA_KEY_WIDTH,), F32)])
    w_qk = (w_in[:, :, c_na:c_qk] * qk_scale).astype(BF16)
    w_v = w_in[:, :, c_qk:c_v].astype(BF16)
    w_r = w_in[:, :, c_v:c_r].astype(BF16)
    w_lr = jnp.pad(w_in[:, :, c_r:], ((0, 0), (0, 0), (0, LANES - 2 * GLA_GATE_RANK)))
    zero_g2 = jnp.zeros_like(w_g2_f)
    w_g2 = jnp.concatenate([jnp.concatenate([w_g2_f, zero_g2], axis=2),
                            jnp.concatenate([zero_g2, w_g2_b], axis=2)], axis=1)
    w_g2 = jnp.pad(w_g2, ((0, 0), (0, LANES - 2 * GLA_GATE_RANK), (0, 0)))
    b_g = jnp.concatenate([b_g_f, b_g_b], axis=1)[:, None, :]
    bias_tbl = jax.vmap(functools.partial(_na_bias_table, rows=rows))(rpb)
    w_o = w_out.astype(BF16)
    w_o_na, w_o_gla = w_o[:, :NA_WIDTH], w_o[:, NA_WIDTH:]
    w_rt = jnp.pad(jnp.concatenate([w_exp, w_grp], axis=2), ((0, 0), (0, 0), (0, LANES - N_EXPERTS - N_GROUPS)))
    w_rt_hi = w_rt.astype(BF16)
    w_rt_lo = (w_rt - w_rt_hi.astype(F32)).astype(BF16)
    w_rt3 = jnp.concatenate([w_rt_hi, w_rt_hi, w_rt_lo], axis=1).transpose(0, 2, 1)
    b_rt = jnp.pad(jnp.concatenate([b_exp, b_grp], axis=1),
                   ((0, 0), (0, ROUTE_ROWS - N_EXPERTS - N_GROUPS)))[:, :, None]
    per_layer_row = lambda a: a[:, None, :]

    in_proj_params = (per_layer_row(norm_mix_g), w_na, w_qk, w_v, w_r, w_lr, w_g2, b_g)
    proj = _in_proj(xf, 0, *in_proj_params)
    for l in range(depth):
        na_qkv, gqk, gv, gr, gates = proj
        y_na = _na(na_qkv, bias_tbl, l, batch, rows)
        o_f, o_b = _gla(gqk, gv, gates, batch, seq)
        x1, h2_slabs, rt_t, counts = _out_route(y_na, o_f, o_b, gr, xf, l, w_o_na, w_o_gla,
                                                per_layer_row(gla_norm_g), per_layer_row(norm_ffn_g), w_rt3, b_rt)

        tile_start, tiles, n_tiles, pad_tile, offs_b, sorted_rows = _dispatch_plan(counts, n)
        pos, rt, xs = _dispatch(pad_tile, n_tiles, rt_t, offs_b, h2_slabs, sorted_rows)
        ys = _experts(tile_start, tiles, xs, w_gate, w_up, w_down, l)
        pos = pos.reshape(-1)
        if l + 1 < depth:
            xf, proj = _combine_in_proj(pos, ys, x1, rt, l + 1, *in_proj_params)
    return _combine_final(pos, ys, x1, rt, final_norm_g[None, :]).reshape(batch, seq, d)
```

```python
import functools

import jax
import jax.numpy as jnp
from jax import lax
from jax.experimental import pallas as pl
from jax.experimental.pallas import tpu as pltpu

F32 = jnp.float32
BF16 = jnp.bfloat16

GRID_W = 64
NA_HEADS = 8
NA_HEAD_DIM = 64
NA_WIDTH = NA_HEADS * NA_HEAD_DIM
WIN_H_MAX = 8
WIN_W = 16
GLA_HEADS = 4
GLA_DK = 64
GLA_DV = 128
GLA_KEY_WIDTH = GLA_HEADS * GLA_DK
GLA_VAL_WIDTH = GLA_HEADS * GLA_DV
GLA_GATE_RANK = 16
GLA_GATE_NORMALIZER = 16.0
GLA_CHUNK = 64
N_GROUPS = 4
EXPERTS_PER_GROUP = 8
N_EXPERTS = N_GROUPS * EXPERTS_PER_GROUP
RMS_EPS = 1e-6

LANES = 128
SUBLANES = 8
MXU_WIDTH = 256
VMEM_LIMIT_BYTES = 56 * 1024 * 1024

MASK_VALUE = -1e30

IN_PROJ_ROWS = 512
NA_ROW_BLOCK = 8
NA_ROW_UNROLL = 8
GLA_STEP_CHUNKS = 4
OUT_ROWS = 512
EXPERT_ROWS = 256
EXPERT_TILE_BUFFERS = 4
DISPATCH_ROWS = 1024
COMBINE_ROWS = DISPATCH_ROWS
FUSED_ROWS = 512


def _params(*sem):
    return pltpu.CompilerParams(dimension_semantics=sem, vmem_limit_bytes=VMEM_LIMIT_BYTES)


def _rms(x, g):
    return x * lax.rsqrt(jnp.mean(x * x, axis=-1, keepdims=True) + RMS_EPS) * g


def _in_proj_rows(x, g_ref, w_na_ref, w_qk_ref, w_v_ref, w_r_ref, w_lr_ref,
                  w_g2_ref, b_g_ref, na_ref, qk_ref, v_ref, r_ref, gate_ref, w_gate):
    @pl.when(pl.program_id(0) == 0)
    def _():
        w_gate[...] = jnp.dot(w_lr_ref[...], w_g2_ref[...], preferred_element_type=F32,
                              precision=lax.Precision.HIGHEST).astype(BF16)

    h = _rms(x, g_ref[...]).astype(BF16)
    na_ref[...] = jnp.dot(h, w_na_ref[...], preferred_element_type=F32).astype(BF16)
    qk_ref[...] = jnp.dot(h, w_qk_ref[...], preferred_element_type=F32)
    v_ref[...] = jnp.dot(h, w_v_ref[...], preferred_element_type=F32).astype(BF16)
    r_ref[...] = jnp.dot(h, w_r_ref[...], preferred_element_type=F32)
    z = jnp.dot(h, w_gate[...], preferred_element_type=F32) + b_g_ref[...]
    log_sig = jnp.minimum(z, 0.0) - jnp.log(1.0 + jnp.exp(-jnp.abs(z)))
    gate_ref[...] = log_sig * (1.0 / GLA_GATE_NORMALIZER)


def _in_proj_kernel(x_ref, *refs):
    _in_proj_rows(x_ref[...], *refs)


def _in_proj_outputs(n, rows):
    row_spec = lambda width: pl.BlockSpec((rows, width), lambda i, *_: (i, 0))
    widths = (3 * NA_WIDTH, 2 * GLA_KEY_WIDTH, GLA_VAL_WIDTH, GLA_VAL_WIDTH, 2 * GLA_KEY_WIDTH)
    dtypes = (BF16, F32, BF16, F32, F32)
    return ([row_spec(w) for w in widths],
            [jax.ShapeDtypeStruct((n, w), t) for w, t in zip(widths, dtypes)])


def _layer_spec(a, layer):
    return pl.BlockSpec((None,) + a.shape[1:], lambda *_: (layer,) + (0,) * (a.ndim - 1))


def _in_proj(x, layer, g, w_na, w_qk, w_v, w_r, w_lr, w_g2, b_g):
    n, d = x.shape
    rows = IN_PROJ_ROWS
    params = (g, w_na, w_qk, w_v, w_r, w_lr, w_g2, b_g)
    out_specs, out_shape = _in_proj_outputs(n, rows)
    return pl.pallas_call(
        _in_proj_kernel,
        grid=(n // rows,),
        in_specs=[pl.BlockSpec((rows, d), lambda i: (i, 0))] + [_layer_spec(a, layer) for a in params],
        out_specs=out_specs,
        out_shape=out_shape,
        scratch_shapes=[pltpu.VMEM((d, 2 * GLA_KEY_WIDTH), BF16)],
        compiler_params=_params("arbitrary"),
        name="in_proj",
    )(x, *params)


def _na_bias_table(rpb, rows):
    kh = min(WIN_H_MAX, rows)
    w = jnp.arange(GRID_W)[:, None, None]
    x = jnp.arange(GRID_W)[None, :, None]
    cs = jnp.clip(w - WIN_W // 2, 0, GRID_W - WIN_W)
    valid = (x >= cs) & (x < cs + WIN_W)
    col_sel = (valid & (jnp.arange(2 * WIN_W - 1)[None, None, :] == x - w + (WIN_W - 1))).astype(F32)
    by_col = jnp.einsum("hrc,wxc->hwrx", rpb.astype(F32), col_sel, precision=lax.Precision.HIGHEST)
    by_col = jnp.where(valid[None, :, None, :, 0], by_col, MASK_VALUE)
    by_col = by_col.reshape(NA_HEADS * GRID_W, (2 * WIN_H_MAX - 1) * GRID_W)
    bias = jnp.stack([by_col[:, (WIN_H_MAX - 1 - c) * GRID_W:(WIN_H_MAX - 1 - c + kh) * GRID_W]
                      for c in range(kh)])
    return bias.reshape(kh, NA_HEADS // 2, 2 * GRID_W, kh * GRID_W)


def _na_window_start(j, rows, kh):
    return jnp.clip(j * NA_ROW_BLOCK - kh // 2, 0, rows - (NA_ROW_BLOCK + kh))


def _na_kernel(q_ref, k_ref, v_ref, tbl_ref, o_ref, *, rows, kh):
    rb = NA_ROW_BLOCK
    j = pl.program_id(1)
    fetched = _na_window_start(j, rows, kh)
    lane = lax.broadcasted_iota(jnp.int32, (GRID_W, LANES), 1)
    first = lane < NA_HEAD_DIM

    def row_body(lr, carry):
        r = j * rb + lr
        start = jnp.clip(r - kh // 2, 0, rows - kh)
        cls = r - start
        q_row = q_ref[lr]
        k_win = k_ref[0, pl.ds(start - fetched, kh)].reshape(kh * GRID_W, NA_WIDTH)
        v_win = v_ref[0, pl.ds(start - fetched, kh)].reshape(kh * GRID_W, NA_WIDTH)
        pairs = [slice(p * LANES, (p + 1) * LANES) for p in range(NA_HEADS // 2)]
        scores = []
        for sl in pairs:
            q_pair = q_row[:, sl]
            zero = jnp.zeros_like(q_pair)
            q_bd = jnp.concatenate([jnp.where(first, q_pair, zero), jnp.where(first, zero, q_pair)], axis=0)
            scores.append(lax.dot_general(q_bd, k_win[:, sl], (((1,), (1,)), ((), ())),
                                          preferred_element_type=F32))
        probs, denoms = [], []
        for p, s in enumerate(scores):
            s = s + tbl_ref[cls, p]
            e = jnp.exp(s - jnp.max(s, axis=-1, keepdims=True))
            denoms.append(jnp.sum(e, axis=-1, keepdims=True))
            probs.append(e.astype(BF16))
        outs = []
        for sl, e, denom in zip(pairs, probs, denoms):
            o = jnp.dot(e, v_win[:, sl], preferred_element_type=F32) / denom
            outs.append(jnp.where(first, o[:GRID_W], o[GRID_W:]))
        o_ref[lr] = jnp.concatenate(outs, axis=-1).astype(o_ref.dtype)
        return carry

    lax.fori_loop(0, rb, row_body, 0, unroll=NA_ROW_UNROLL)


def _na(na_qkv, tbl, layer, batch, rows):
    kh = min(WIN_H_MAX, rows)
    rb = NA_ROW_BLOCK
    nblk = rows // rb
    x4 = na_qkv.reshape(batch, rows, GRID_W, 3 * NA_WIDTH)
    assert rows >= rb + kh
    blk = (None, rb, GRID_W, NA_WIDTH)
    window = (pl.Element(1), pl.Element(rb + kh), pl.Element(GRID_W), pl.Element(NA_WIDTH))
    specs = [pl.BlockSpec(blk, lambda b, j: (b, j, 0, 0))]
    for col in (1, 2):
        specs.append(pl.BlockSpec(window, lambda b, j, col=col: (b, _na_window_start(j, rows, kh), 0,
                                                                  col * NA_WIDTH)))
    specs.append(_layer_spec(tbl, layer))
    out = pl.pallas_call(
        functools.partial(_na_kernel, rows=rows, kh=kh),
        grid=(batch, nblk),
        in_specs=specs,
        out_specs=pl.BlockSpec(blk, lambda b, j: (b, j, 0, 0)),
        out_shape=jax.ShapeDtypeStruct((batch, rows, GRID_W, NA_WIDTH), BF16),
        compiler_params=_params("arbitrary", "arbitrary"),
        name="na",
    )(x4, x4, x4, tbl)
    return out.reshape(batch * rows * GRID_W, NA_WIDTH)


def _block_diag_mask(row_block, col_block, nblocks):
    shape = (row_block * nblocks, col_block * nblocks)
    r = lax.shift_right_logical(lax.broadcasted_iota(jnp.int32, shape, 0), row_block.bit_length() - 1)
    c = lax.shift_right_logical(lax.broadcasted_iota(jnp.int32, shape, 1), col_block.bit_length() - 1)
    return r == c


def _split_bf16x3(x):
    hi = x.astype(BF16)
    rest = x - hi.astype(F32)
    mid = rest.astype(BF16)
    lo = (rest - mid.astype(F32)).astype(BF16)
    return hi, mid, lo


def _gla_direction(qk_ref, v_ref, g_ref, *, backward):
    c = GLA_CHUNK
    nc = GLA_STEP_CHUNKS
    kw = GLA_KEY_WIDTH
    order = list(reversed(range(nc))) if backward else list(range(nc))
    chunk = lambda a, n: a[n * c:(n + 1) * c]

    def stage_cumsum():
        step = nc * c
        ti = lax.broadcasted_iota(jnp.int32, (step, step), 0)
        tj = lax.broadcasted_iota(jnp.int32, (step, step), 1)
        same_chunk = lax.shift_right_logical(ti, c.bit_length() - 1) == lax.shift_right_logical(tj, c.bit_length() - 1)
        tri = (tj >= ti) if backward else (tj <= ti)
        cum = jnp.where(same_chunk & tri, 1.0, 0.0).astype(BF16)
        return sum(jnp.dot(cum, piece, preferred_element_type=F32) for piece in _split_bf16x3(g_ref[...]))

    def stage_decays(b):
        ref_row, last_row = (c // 2, 0) if backward else (c // 2 - 1, c - 1)
        rows_of = lambda r: jnp.concatenate(
            [jnp.broadcast_to(b[n * c + r:n * c + r + 1, :], (c, kw)) for n in range(nc)], axis=0)
        b_ref, b_last = rows_of(ref_row), rows_of(last_row)
        q = qk_ref[:, :kw]
        k = qk_ref[:, kw:]
        q_rel = (q * jnp.exp(b - b_ref)).astype(BF16)
        k_rel = (k * jnp.exp(b_ref - b)).astype(BF16)
        k_dec = k * jnp.exp(b_last - b)
        q_dec = (q * jnp.exp(b)).astype(BF16)
        decay = jnp.exp(b_last)
        return q_rel, k_rel, k_dec, q_dec, decay

    def stage_scores(q_rel, k_rel):
        kk_mask = _block_diag_mask(c, GLA_DK, GLA_HEADS)
        out = []
        for n in range(nc):
            k_bd = jnp.where(kk_mask, jnp.concatenate([chunk(k_rel, n)] * GLA_HEADS, axis=0), jnp.zeros((), BF16))
            out.append(lax.dot_general(chunk(q_rel, n), k_bd, (((1,), (1,)), ((), ())),
                                       preferred_element_type=F32))
        return out

    def stage_intra(scores, k_dec, decay):
        si = lax.broadcasted_iota(jnp.int32, (c, c * GLA_HEADS), 0)
        sj = lax.broadcasted_iota(jnp.int32, (c, c * GLA_HEADS), 1) & (c - 1)
        keep = (sj > si) if backward else (sj <= si)
        kv_mask = _block_diag_mask(c, GLA_DV, GLA_HEADS)
        vk_mask = _block_diag_mask(GLA_DK, GLA_DV, 2)
        o_intra, upd, decay_col = [], [], []
        for n in range(nc):
            v = v_ref[n * c:(n + 1) * c, :]
            p = jnp.where(keep, scores[n], 0.0).astype(BF16)
            v_bd = jnp.where(kv_mask, jnp.concatenate([v] * GLA_HEADS, axis=0), jnp.zeros((), BF16))
            o_intra.append(jnp.dot(p, v_bd, preferred_element_type=F32))
            k_dec_t = chunk(k_dec, n).T.astype(BF16)
            pairs = []
            for hp in range(GLA_HEADS // 2):
                kv = jnp.dot(k_dec_t[hp * 2 * GLA_DK:(hp + 1) * 2 * GLA_DK],
                             v[:, hp * 2 * GLA_DV:(hp + 1) * 2 * GLA_DV], preferred_element_type=F32)
                pairs.append(jnp.where(vk_mask, kv, 0.0))
            upd.append(pairs)
            decay_col.append(chunk(decay, n).T[:, :1])
        return o_intra, upd, decay_col

    def scan_step(state, idx, q_dec, o_intra, upd, decay_col, out_ref):
        n = order[idx]
        q_n = chunk(q_dec, n)
        pw = 2 * GLA_DK
        o_inter = jnp.concatenate(
            [jnp.dot(q_n[:, hp * pw:(hp + 1) * pw], s.astype(BF16), preferred_element_type=F32)
             for hp, s in enumerate(state)], axis=1)
        out_ref[n * c:(n + 1) * c, :] = o_intra[n] + o_inter
        return [s * decay_col[n][hp * pw:(hp + 1) * pw] + u for hp, (s, u) in enumerate(zip(state, upd[n]))]

    return stage_cumsum, stage_decays, stage_scores, stage_intra, scan_step


def _gla_kernel(qk_f_ref, v_f_ref, g_f_ref, qk_b_ref, v_b_ref, g_b_ref, o_f_ref, o_b_ref,
                st_f, st_b):
    @pl.when(pl.program_id(1) == 0)
    def _():
        st_f[...] = jnp.zeros_like(st_f)
        st_b[...] = jnp.zeros_like(st_b)

    dirs = (_gla_direction(qk_f_ref, v_f_ref, g_f_ref, backward=False),
            _gla_direction(qk_b_ref, v_b_ref, g_b_ref, backward=True))
    b = [d[0]() for d in dirs]
    dec = [d[1](x) for d, x in zip(dirs, b)]
    scores = [d[2](x[0], x[1]) for d, x in zip(dirs, dec)]
    intra = [d[3](s, x[2], x[4]) for d, s, x in zip(dirs, scores, dec)]
    pairs = range(GLA_HEADS // 2)
    states = [[st[hp] for hp in pairs] for st in (st_f, st_b)]
    for idx in range(GLA_STEP_CHUNKS):
        for j, (d, x, y, out_ref) in enumerate(zip(dirs, dec, intra, (o_f_ref, o_b_ref))):
            states[j] = d[4](states[j], idx, x[3], *y, out_ref)
    for st, state in zip((st_f, st_b), states):
        for hp in pairs:
            st[hp] = state[hp]


def _gla(qk, v, gates, batch, seq):
    step = GLA_STEP_CHUNKS * GLA_CHUNK
    nblk = seq // step
    qk3 = qk.reshape(batch, seq, 2 * GLA_KEY_WIDTH)
    v3 = v.reshape(batch, seq, GLA_VAL_WIDTH)
    g3 = gates.reshape(batch, seq, 2 * GLA_KEY_WIDTH)
    fwd = lambda b, n: (b, n, 0)
    bwd = lambda b, n: (b, nblk - 1 - n, 0)
    bwd_gate = lambda b, n: (b, nblk - 1 - n, 1)
    o_f, o_b = pl.pallas_call(
        _gla_kernel,
        grid=(batch, nblk),
        in_specs=[pl.BlockSpec((None, step, 2 * GLA_KEY_WIDTH), fwd),
                  pl.BlockSpec((None, step, GLA_VAL_WIDTH), fwd),
                  pl.BlockSpec((None, step, GLA_KEY_WIDTH), fwd),
                  pl.BlockSpec((None, step, 2 * GLA_KEY_WIDTH), bwd),
                  pl.BlockSpec((None, step, GLA_VAL_WIDTH), bwd),
                  pl.BlockSpec((None, step, GLA_KEY_WIDTH), bwd_gate)],
        out_specs=[pl.BlockSpec((None, step, GLA_VAL_WIDTH), fwd),
                   pl.BlockSpec((None, step, GLA_VAL_WIDTH), bwd)],
        out_shape=[jax.ShapeDtypeStruct((batch, seq, GLA_VAL_WIDTH), F32)] * 2,
        scratch_shapes=[pltpu.VMEM((GLA_HEADS // 2, 2 * GLA_DK, 2 * GLA_DV), F32),
                        pltpu.VMEM((GLA_HEADS // 2, 2 * GLA_DK, 2 * GLA_DV), F32)],
        compiler_params=_params("arbitrary", "arbitrary"),
        name="gla",
    )(qk3, v3, g3, qk3, v3, g3)
    return o_f.reshape(batch * seq, GLA_VAL_WIDTH), o_b.reshape(batch * seq, GLA_VAL_WIDTH)


def _store_slabs(ref, x):
    rows = x.shape[0]
    for s in range(SUBLANES):
        ref[pl.ds(s, rows, stride=SUBLANES), :] = x[:, s * LANES:(s + 1) * LANES]


def _load_slabs(ref, rows):
    return jnp.concatenate([ref[pl.ds(s, rows, stride=SUBLANES), :] for s in range(SUBLANES)], axis=1)


def _slab_copy(src_hbm, dst, sem, src_row, k):
    dst_row = k * SUBLANES if isinstance(k, int) else pl.multiple_of(k * SUBLANES, SUBLANES)
    return pltpu.make_async_copy(src_hbm.at[pl.ds(pl.multiple_of(src_row, SUBLANES), SUBLANES), :],
                                 dst.at[pl.ds(dst_row, SUBLANES), :], sem)


def _wait_slabs(src_hbm, dst, sem):
    pltpu.make_async_copy(src_hbm.at[pl.ds(0, dst.shape[0]), :], dst, sem).wait()


RT_E1, RT_E2, RT_RANK1, RT_RANK2, RT_W1, RT_W2 = range(6)
GROUP_ROW0 = N_EXPERTS
ROUTE_ROWS = 64


def _out_route_kernel(na_ref, of_ref, ob_ref, r_ref, x_ref, w_na_ref, w_gla_ref, gn_ref, fn_ref,
                      w_rt_ref, b_rt_ref, x1_ref, h2_ref, rt_ref, cnt_ref, carry):
    @pl.when(pl.program_id(0) == 0)
    def _():
        carry[...] = jnp.zeros_like(carry)

    rows = x_ref.shape[0]
    o = of_ref[...] + ob_ref[...]
    r = r_ref[...]
    parts = []
    for h in range(GLA_HEADS):
        sl = slice(h * GLA_DV, (h + 1) * GLA_DV)
        parts.append(_rms(o[:, sl], gn_ref[...]) * (r[:, sl] * jax.nn.sigmoid(r[:, sl])))
    y_gla = jnp.concatenate(parts, axis=-1).astype(BF16)
    x1 = (x_ref[...] + jnp.dot(na_ref[...], w_na_ref[...], preferred_element_type=F32)
          + jnp.dot(y_gla, w_gla_ref[...], preferred_element_type=F32))
    x1_ref[...] = x1
    h2 = _rms(x1, fn_ref[...])
    _store_slabs(h2_ref, h2)

    h_hi = h2.astype(BF16)
    h_lo = (h2 - h_hi.astype(F32)).astype(BF16)
    logits = lax.dot_general(w_rt_ref[...], jnp.concatenate([h_hi, h_lo, h_hi], axis=1),
                             (((1,), (1,)), ((), ())), preferred_element_type=F32)
    logits = logits[:ROUTE_ROWS] + b_rt_ref[...]
    row_i = lax.broadcasted_iota(jnp.int32, (ROUTE_ROWS, rows), 0)
    row = row_i.astype(F32)
    row_grp = lax.shift_right_logical(row_i, EXPERTS_PER_GROUP.bit_length() - 1).astype(F32)
    neg = jnp.float32(-jnp.inf)
    no_row = jnp.float32(ROUTE_ROWS)
    is_grp = (row_i >= GROUP_ROW0) & (row_i < GROUP_ROW0 + N_GROUPS)
    g_logit = jnp.where(is_grp, logits, neg)
    g_max = jnp.max(g_logit, axis=0, keepdims=True)
    g_sel = jnp.min(jnp.where(is_grp & (g_logit == g_max), row, no_row), axis=0, keepdims=True) - GROUP_ROW0
    grp_w = 1.0 / jnp.sum(jnp.where(is_grp, jnp.exp(g_logit - g_max), 0.0), axis=0, keepdims=True)
    in_grp = (row_i < N_EXPERTS) & (row_grp == g_sel)
    e_logit = jnp.where(in_grp, logits, neg)
    v1 = jnp.max(e_logit, axis=0, keepdims=True)
    i1 = jnp.min(jnp.where(in_grp & (e_logit == v1), row, no_row), axis=0, keepdims=True)
    rest = in_grp & (row != i1)
    e_logit2 = jnp.where(rest, logits, neg)
    v2 = jnp.max(e_logit2, axis=0, keepdims=True)
    i2 = jnp.min(jnp.where(rest & (e_logit2 == v2), row, no_row), axis=0, keepdims=True)
    t = jnp.exp(v2 - v1)
    w1 = grp_w / (1.0 + t)
    w2 = grp_w * t / (1.0 + t)

    sel1 = row == i1
    sel2 = row == i2
    onehot = jnp.where(sel1 | sel2, 1.0, 0.0)
    ti = lax.broadcasted_iota(jnp.int32, (rows, rows), 0)
    tj = lax.broadcasted_iota(jnp.int32, (rows, rows), 1)
    earlier = jnp.where(ti < tj, 1.0, 0.0).astype(BF16)
    ranks = jnp.dot(onehot.astype(BF16), earlier, preferred_element_type=F32) + carry[:, :1]
    rank1 = jnp.sum(jnp.where(sel1, ranks, 0.0), axis=0, keepdims=True)
    rank2 = jnp.sum(jnp.where(sel2, ranks, 0.0), axis=0, keepdims=True)
    new_carry = carry[...] + jnp.sum(onehot, axis=1, keepdims=True)
    carry[...] = new_carry
    cnt_ref[...] = new_carry

    fields = {RT_E1: i1, RT_E2: i2, RT_RANK1: rank1, RT_RANK2: rank2, RT_W1: w1, RT_W2: w2}
    zero = jnp.zeros_like(w1)
    rt_ref[...] = jnp.concatenate([fields.get(f, zero) for f in range(SUBLANES)], axis=0)


def _out_route(y_na, o_f, o_b, r, x, layer, w_na, w_gla, gn, fn, w_rt, b_rt):
    n, d = x.shape
    rows = OUT_ROWS
    row_spec = lambda width: pl.BlockSpec((rows, width), lambda i: (i, 0))
    full = lambda a: _layer_spec(a, layer)
    return pl.pallas_call(
        _out_route_kernel,
        grid=(n // rows,),
        in_specs=[row_spec(NA_WIDTH), row_spec(GLA_VAL_WIDTH), row_spec(GLA_VAL_WIDTH),
                  row_spec(GLA_VAL_WIDTH), row_spec(d), full(w_na), full(w_gla), full(gn), full(fn),
                  full(w_rt), full(b_rt)],
        out_specs=[row_spec(d), pl.BlockSpec((rows * SUBLANES, LANES), lambda i: (i, 0)),
                   pl.BlockSpec((SUBLANES, rows), lambda i: (0, i)),
                   pl.BlockSpec((ROUTE_ROWS, LANES), lambda i: (0, 0))],
        out_shape=[jax.ShapeDtypeStruct((n, d), F32), jax.ShapeDtypeStruct((n * SUBLANES, LANES), F32),
                   jax.ShapeDtypeStruct((SUBLANES, n), F32), jax.ShapeDtypeStruct((ROUTE_ROWS, LANES), F32)],
        scratch_shapes=[pltpu.VMEM((ROUTE_ROWS, LANES), F32)],
        compiler_params=_params("arbitrary"),
        name="out_route",
    )(y_na, o_f, o_b, r, x, w_na, w_gla, gn, fn, w_rt, b_rt)


def _dispatch_kernel(pad_tile_ref, n_tiles_ref, rt_ref, offs_ref, h_hbm, pos_ref, rec_ref, xs_hbm,
                     zeros, hbuf, pos_vmem, pos_smem, sem_zero, sem_pos, sem_in, sem):
    i = pl.program_id(0)
    n_steps = pl.num_programs(0)
    db = DISPATCH_ROWS
    tile = EXPERT_ROWS * SUBLANES
    max_tiles = xs_hbm.shape[0] // tile
    buf = lax.rem(i, 2)

    def block_copy(step, b):
        rows = pl.ds(pl.multiple_of(step * db * SUBLANES, db * SUBLANES), db * SUBLANES)
        return pltpu.make_async_copy(h_hbm.at[rows, :], hbuf.at[b], sem_in.at[b])

    def wait_row_copies(b):
        for _ in range(2):
            pltpu.make_async_copy(hbuf.at[b], xs_hbm.at[pl.ds(0, db * SUBLANES), :], sem).wait()

    def zero_copy(t):
        return pltpu.make_async_copy(zeros, xs_hbm.at[pl.ds(pl.multiple_of(t * tile, tile), tile), :], sem_zero)

    def for_each_zero_tile(fn):
        def padded(e, carry):
            @pl.when(pad_tile_ref[e] >= 0)
            def _():
                fn(zero_copy(pad_tile_ref[e]))
            return carry
        lax.fori_loop(0, N_EXPERTS, padded, 0)

        def tail(t, carry):
            fn(zero_copy(t))
            return carry
        lax.fori_loop(n_tiles_ref[0], max_tiles, tail, 0)

    @pl.when(i == 0)
    def _():
        block_copy(0, 0).start()
        zeros[...] = jnp.zeros_like(zeros)
        for_each_zero_tile(lambda cp: cp.start())

    rt_t = rt_ref[...]
    rec_ref[...] = jnp.concatenate([rt_t, jnp.zeros((LANES - SUBLANES, db), F32)], axis=0).T
    expert = lax.broadcasted_iota(jnp.int32, (N_EXPERTS, db), 0).astype(F32)
    pos = []
    for e_lane, rank_lane in ((RT_E1, RT_RANK1), (RT_E2, RT_RANK2)):
        seg = jnp.sum(jnp.where(expert == rt_t[e_lane:e_lane + 1, :], offs_ref[...], 0.0), axis=0, keepdims=True)
        pos.append((seg + rt_t[rank_lane:rank_lane + 1, :]) * SUBLANES)
    pos = jnp.concatenate(pos, axis=0).astype(jnp.int32)
    pos_ref[...] = pos
    pos_vmem[...] = jnp.concatenate([pos, jnp.zeros((SUBLANES - 2, db), jnp.int32)], axis=0)
    to_smem = pltpu.make_async_copy(pos_vmem, pos_smem, sem_pos)
    to_smem.start()

    @pl.when(i == 0)
    def _():
        for_each_zero_tile(lambda cp: cp.wait())

    @pl.when(i > 0)
    def _():
        wait_row_copies(1 - buf)

    to_smem.wait()
    block_copy(i, buf).wait()

    @pl.when(i + 1 < n_steps)
    def _():
        block_copy(i + 1, 1 - buf).start()

    h_blk = hbuf.at[buf]
    for k in range(db):
        src = h_blk.at[pl.ds(k * SUBLANES, SUBLANES), :]
        for slot in range(2):
            dst_row = pl.multiple_of(pos_smem[slot, k], SUBLANES)
            pltpu.make_async_copy(src, xs_hbm.at[pl.ds(dst_row, SUBLANES), :], sem).start(priority=slot)

    @pl.when(i == n_steps - 1)
    def _():
        wait_row_copies(buf)


def _dispatch(pad_tile, n_tiles, rt_t, offs_b, h2_slabs, sorted_rows):
    n = rt_t.shape[1]
    db = DISPATCH_ROWS
    grid_spec = pltpu.PrefetchScalarGridSpec(
        num_scalar_prefetch=2,
        grid=(n // db,),
        in_specs=[pl.BlockSpec((SUBLANES, db), lambda i, pt, nt: (0, i)),
                  pl.BlockSpec(offs_b.shape, lambda i, pt, nt: (0, 0)),
                  pl.BlockSpec(memory_space=pl.ANY)],
        out_specs=[pl.BlockSpec((None, 2, db), lambda i, pt, nt: (i, 0, 0)),
                   pl.BlockSpec((db, LANES), lambda i, pt, nt: (i, 0)),
                   pl.BlockSpec(memory_space=pl.ANY)],
        scratch_shapes=[pltpu.VMEM((EXPERT_ROWS * SUBLANES, LANES), F32),
                        pltpu.VMEM((2, db * SUBLANES, LANES), F32),
                        pltpu.VMEM((SUBLANES, db), jnp.int32), pltpu.SMEM((SUBLANES, db), jnp.int32),
                        pltpu.SemaphoreType.DMA(()), pltpu.SemaphoreType.DMA(()),
                        pltpu.SemaphoreType.DMA((2,)), pltpu.SemaphoreType.DMA(())],
    )
    return pl.pallas_call(
        _dispatch_kernel,
        grid_spec=grid_spec,
        out_shape=[jax.ShapeDtypeStruct((n // db, 2, db), jnp.int32),
                   jax.ShapeDtypeStruct((n, LANES), F32),
                   jax.ShapeDtypeStruct((sorted_rows * SUBLANES, LANES), F32)],
        compiler_params=_params("arbitrary"),
        name="dispatch",
    )(pad_tile, n_tiles, rt_t, offs_b, h2_slabs)


TILE_DMA_PRIORITY = 1


def _expert_tile(x_ref, y_ref, wg_s, wu_s, wd_s):
    rows = EXPERT_ROWS
    x = _load_slabs(x_ref, rows).astype(BF16)
    hid = []
    for c in range(0, wg_s.shape[1], MXU_WIDTH):
        gate = jnp.dot(x, wg_s[:, c:c + MXU_WIDTH], preferred_element_type=F32)
        up = jnp.dot(x, wu_s[:, c:c + MXU_WIDTH], preferred_element_type=F32)
        hid.append((gate * jax.nn.sigmoid(gate) * up).astype(BF16))
    hid = jnp.concatenate(hid, axis=1)
    for c in range(0, wd_s.shape[1], MXU_WIDTH):
        y = jnp.dot(hid, wd_s[:, c:c + MXU_WIDTH], preferred_element_type=F32)
        for s in range(c // LANES, (c + MXU_WIDTH) // LANES):
            y_ref[pl.ds(s, rows, stride=SUBLANES), :] = y[:, s * LANES - c:(s + 1) * LANES - c]


def _experts_kernel(tile_start_ref, tiles_ref, x_hbm, wg_hbm, wu_hbm, wd_hbm, y_hbm,
                    xbuf, ybuf, wg_f, wu_f, wd_f, wg_s, wu_s, wd_s, sem_x, sem_y, sem_w, *, layer):
    nbuf = EXPERT_TILE_BUFFERS
    e = pl.program_id(0)
    last_step = e == pl.num_programs(0) - 1
    tile = EXPERT_ROWS * SUBLANES
    n = tiles_ref[e]
    base = tile_start_ref[e]
    n_tiles = tile_start_ref[N_EXPERTS - 1] + tiles_ref[N_EXPERTS - 1]
    tile_rows = lambda g: pl.ds(pl.multiple_of(g * tile, tile), tile)

    def x_copy(g, slot):
        return pltpu.make_async_copy(x_hbm.at[tile_rows(g), :], xbuf.at[slot], sem_x.at[slot])

    def y_copy(g, slot):
        return pltpu.make_async_copy(ybuf.at[slot], y_hbm.at[tile_rows(g), :], sem_y.at[slot])

    def w_copies(expert, slot):
        return [pltpu.make_async_copy(w_hbm.at[layer, expert], stage.at[slot], sem_w.at[slot])
                for w_hbm, stage in ((wg_hbm, wg_f), (wu_hbm, wu_f), (wd_hbm, wd_f))]

    w_slot = lax.rem(e, 2)

    @pl.when(e == 0)
    def _():
        for g in range(nbuf - 1):
            @pl.when(g < n_tiles)
            def _():
                x_copy(g, g).start(priority=TILE_DMA_PRIORITY)
        for cp in w_copies(0, 0):
            cp.start()

    for cp in w_copies(e, w_slot):
        cp.wait()

    @pl.when(jnp.logical_not(last_step))
    def _():
        for cp in w_copies(e + 1, 1 - w_slot):
            cp.start()

    @pl.when(n > 0)
    def _():
        wg_s[...] = wg_f[w_slot].astype(BF16)
        wu_s[...] = wu_f[w_slot].astype(BF16)
        wd_s[...] = wd_f[w_slot].astype(BF16)

        def tile_body(g, carry):
            slot = lax.rem(g, nbuf)
            x_copy(g, slot).wait()

            @pl.when(g + nbuf - 1 < n_tiles)
            def _():
                x_copy(g + nbuf - 1, lax.rem(g + nbuf - 1, nbuf)).start(priority=TILE_DMA_PRIORITY)

            @pl.when(g >= nbuf)
            def _():
                y_copy(g - nbuf, slot).wait()

            _expert_tile(xbuf.at[slot], ybuf.at[slot], wg_s, wu_s, wd_s)
            y_copy(g, slot).start(priority=TILE_DMA_PRIORITY)
            return carry
        lax.fori_loop(base, base + n, tile_body, 0)

    @pl.when(last_step)
    def _():
        for back in range(nbuf, 0, -1):
            @pl.when(n_tiles >= back)
            def _():
                y_copy(n_tiles - back, lax.rem(n_tiles - back, nbuf)).wait()

        first_unused = n_tiles
        max_tiles = y_hbm.shape[0] // tile
        ybuf[0] = jnp.zeros(ybuf.shape[1:], ybuf.dtype)
        zero_copy = lambda t: pltpu.make_async_copy(
            ybuf.at[0], y_hbm.at[pl.ds(pl.multiple_of(t * tile, tile), tile), :], sem_y.at[0])

        def start(t, carry):
            zero_copy(t).start()
            return carry

        def wait(t, carry):
            zero_copy(t).wait()
            return carry
        lax.fori_loop(first_unused, max_tiles, start, 0)
        lax.fori_loop(first_unused, max_tiles, wait, 0)


def _experts(tile_start, tiles, xs, wg, wu, wd, layer):
    rows = EXPERT_ROWS
    _, n_experts, d, dff = wg.shape
    assert d == SUBLANES * LANES and n_experts == N_EXPERTS
    nbuf = EXPERT_TILE_BUFFERS
    any_spec = pl.BlockSpec(memory_space=pl.ANY)
    grid_spec = pltpu.PrefetchScalarGridSpec(
        num_scalar_prefetch=2,
        grid=(n_experts,),
        in_specs=[any_spec, any_spec, any_spec, any_spec],
        out_specs=any_spec,
        scratch_shapes=[pltpu.VMEM((nbuf, rows * SUBLANES, LANES), F32),
                        pltpu.VMEM((nbuf, rows * SUBLANES, LANES), F32),
                        pltpu.VMEM((2, d, dff), F32), pltpu.VMEM((2, d, dff), F32), pltpu.VMEM((2, dff, d), F32),
                        pltpu.VMEM((d, dff), BF16), pltpu.VMEM((d, dff), BF16), pltpu.VMEM((dff, d), BF16),
                        pltpu.SemaphoreType.DMA((nbuf,)), pltpu.SemaphoreType.DMA((nbuf,)),
                        pltpu.SemaphoreType.DMA((2,))],
    )
    return pl.pallas_call(
        functools.partial(_experts_kernel, layer=layer),
        grid_spec=grid_spec,
        out_shape=jax.ShapeDtypeStruct(xs.shape, F32),
        compiler_params=_params("arbitrary"),
        name="experts",
    )(tile_start, tiles, xs, wg, wu, wd)


def _start_slab_gather(src_hbm, dst, sem, row_ref, base, count, priority):
    for k in range(count):
        _slab_copy(src_hbm, dst, sem, row_ref[base + k], k).start(priority=priority)


def _combined_rows(pos_ref, ys_hbm, x1_ref, rt_ref, buf, sem):
    i = pl.program_id(0)
    n_steps = pl.num_programs(0)
    rows = x1_ref.shape[0]
    steps_per_block = COMBINE_ROWS // rows
    slot = i % 2

    def start(step, s):
        first = lax.div(step, steps_per_block) * (2 * COMBINE_ROWS) + lax.rem(step, steps_per_block) * rows
        for j in range(2):
            _start_slab_gather(ys_hbm, buf.at[s, j], sem.at[s], pos_ref, first + j * COMBINE_ROWS, rows, priority=j)

    @pl.when(i == 0)
    def _():
        start(0, 0)

    @pl.when(i + 1 < n_steps)
    def _():
        start(i + 1, 1 - slot)

    _wait_slabs(ys_hbm, buf.at[slot, 0], sem.at[slot])
    _wait_slabs(ys_hbm, buf.at[slot, 1], sem.at[slot])
    rt = rt_ref[...]
    lane = lax.broadcasted_iota(jnp.int32, rt.shape, 1)
    w1 = jnp.sum(jnp.where(lane == RT_W1, rt, 0.0), axis=-1, keepdims=True)
    w2 = jnp.sum(jnp.where(lane == RT_W2, rt, 0.0), axis=-1, keepdims=True)
    y = w1 * _load_slabs(buf.at[slot, 0], rows) + w2 * _load_slabs(buf.at[slot, 1], rows)
    return x1_ref[...] + y


def _combine_scratch(rows):
    return [pltpu.VMEM((2, 2, rows * SUBLANES, LANES), F32), pltpu.SemaphoreType.DMA((2,))]


def _combine_final_kernel(pos_ref, ys_hbm, x1_ref, rt_ref, g_ref, o_ref, buf, sem):
    o_ref[...] = _rms(_combined_rows(pos_ref, ys_hbm, x1_ref, rt_ref, buf, sem), g_ref[...])


def _combine_final(pos, ys, x1, rt, g):
    n, d = x1.shape
    rows = COMBINE_ROWS
    grid_spec = pltpu.PrefetchScalarGridSpec(
        num_scalar_prefetch=1,
        grid=(n // rows,),
        in_specs=[pl.BlockSpec(memory_space=pl.ANY),
                  pl.BlockSpec((rows, d), lambda i, pos: (i, 0)),
                  pl.BlockSpec((rows, LANES), lambda i, pos: (i, 0)),
                  pl.BlockSpec(g.shape, lambda i, pos: (0, 0))],
        out_specs=pl.BlockSpec((rows, d), lambda i, pos: (i, 0)),
        scratch_shapes=_combine_scratch(rows),
    )
    return pl.pallas_call(
        _combine_final_kernel,
        grid_spec=grid_spec,
        out_shape=jax.ShapeDtypeStruct((n, d), F32),
        compiler_params=_params("arbitrary"),
        name="combine",
    )(pos, ys, x1, rt, g)


def _combine_in_proj_kernel(pos_ref, ys_hbm, x1_ref, rt_ref, *refs):
    param_refs, (x_ref, *proj_refs), (w_gate, buf, sem) = refs[:8], refs[8:14], refs[14:]
    x = _combined_rows(pos_ref, ys_hbm, x1_ref, rt_ref, buf, sem)
    x_ref[...] = x
    _in_proj_rows(x, *param_refs, *proj_refs, w_gate)


def _combine_in_proj(pos, ys, x1, rt, layer, g, w_na, w_qk, w_v, w_r, w_lr, w_g2, b_g):
    n, d = x1.shape
    rows = FUSED_ROWS
    params = (g, w_na, w_qk, w_v, w_r, w_lr, w_g2, b_g)
    proj_specs, proj_shapes = _in_proj_outputs(n, rows)
    grid_spec = pltpu.PrefetchScalarGridSpec(
        num_scalar_prefetch=1,
        grid=(n // rows,),
        in_specs=[pl.BlockSpec(memory_space=pl.ANY),
                  pl.BlockSpec((rows, d), lambda i, pos: (i, 0)),
                  pl.BlockSpec((rows, LANES), lambda i, pos: (i, 0))] + [_layer_spec(a, layer) for a in params],
        out_specs=[pl.BlockSpec((rows, d), lambda i, pos: (i, 0))] + proj_specs,
        scratch_shapes=[pltpu.VMEM((d, 2 * GLA_KEY_WIDTH), BF16)] + _combine_scratch(rows),
    )
    x, *proj = pl.pallas_call(
        _combine_in_proj_kernel,
        grid_spec=grid_spec,
        out_shape=[jax.ShapeDtypeStruct((n, d), F32)] + proj_shapes,
        compiler_params=_params("arbitrary"),
        name="combine_in_proj",
    )(pos, ys, x1, rt, *params)
    return x, proj


def _dispatch_plan(counts, n):
    rows = EXPERT_ROWS
    max_tiles = (2 * n) // rows + N_EXPERTS
    cnt = counts[:N_EXPERTS, 0].astype(jnp.int32)
    tiles = (cnt + rows - 1) // rows
    tile_end = jnp.cumsum(tiles)
    tile_start = tile_end - tiles
    n_tiles = tile_end[-1:]
    pad_tile = jnp.where(cnt % rows != 0, tile_end - 1, -1).astype(jnp.int32)
    offs_b = jnp.broadcast_to((tile_start * rows).astype(F32)[:, None], (N_EXPERTS, DISPATCH_ROWS))
    return tile_start.astype(jnp.int32), tiles, n_tiles, pad_tile, offs_b, max_tiles * rows


def kernel(x, norm_mix_g, w_in, w_g2_f, b_g_f, w_g2_b, b_g_b, gla_norm_g, rpb, w_out, norm_ffn_g, w_grp, b_grp,
           w_exp, b_exp, w_gate, w_up, w_down, final_norm_g):
    batch, seq, d = x.shape
    n = batch * seq
    rows = seq // GRID_W
    depth = w_in.shape[0]
    xf = x.reshape(n, d)
    c_na = 3 * NA_WIDTH
    c_qk = c_na + 2 * GLA_KEY_WIDTH
    c_v = c_qk + GLA_VAL_WIDTH
    c_r = c_v + GLA_VAL_WIDTH

    q_scale = jnp.concatenate([jnp.full((NA_WIDTH,), NA_HEAD_DIM ** -0.5, F32), jnp.ones((2 * NA_WIDTH,), F32)])
    w_na = (w_in[:, :, :c_na] * q_scale).astype(BF16)
    qk_scale = jnp.concatenate([jnp.full((GLA_KEY_WIDTH,), GLA_DK ** -0.5, F32), jnp.ones((GLA_KEY_WIDTH,), F32)])
    w_qk = (w_in[:, :, c_na:c_qk] * qk_scale).astype(BF16)
    w_v = w_in[:, :, c_qk:c_v].astype(BF16)
    w_r = w_in[:, :, c_v:c_r].astype(BF16)
    w_lr = jnp.pad(w_in[:, :, c_r:], ((0, 0), (0, 0), (0, LANES - 2 * GLA_GATE_RANK)))
    zero_g2 = jnp.zeros_like(w_g2_f)
    w_g2 = jnp.concatenate([jnp.concatenate([w_g2_f, zero_g2], axis=2),
                            jnp.concatenate([zero_g2, w_g2_b], axis=2)], axis=1)
    w_g2 = jnp.pad(w_g2, ((0, 0), (0, LANES - 2 * GLA_GATE_RANK), (0, 0)))
    b_g = jnp.concatenate([b_g_f, b_g_b], axis=1)[:, None, :]
    bias_tbl = jax.vmap(functools.partial(_na_bias_table, rows=rows))(rpb)
    w_o = w_out.astype(BF16)
    w_o_na, w_o_gla = w_o[:, :NA_WIDTH], w_o[:, NA_WIDTH:]
    w_rt = jnp.pad(jnp.concatenate([w_exp, w_grp], axis=2), ((0, 0), (0, 0), (0, LANES - N_EXPERTS - N_GROUPS)))
    w_rt_hi = w_rt.astype(BF16)
    w_rt_lo = (w_rt - w_rt_hi.astype(F32)).astype(BF16)
    w_rt3 = jnp.concatenate([w_rt_hi, w_rt_hi, w_rt_lo], axis=1).transpose(0, 2, 1)
    b_rt = jnp.pad(jnp.concatenate([b_exp, b_grp], axis=1),
                   ((0, 0), (0, ROUTE_ROWS - N_EXPERTS - N_GROUPS)))[:, :, None]
    per_layer_row = lambda a: a[:, None, :]

    in_proj_params = (per_layer_row(norm_mix_g), w_na, w_qk, w_v, w_r, w_lr, w_g2, b_g)
    proj = _in_proj(xf, 0, *in_proj_params)
    for l in range(depth):
        na_qkv, gqk, gv, gr, gates = proj
        y_na = _na(na_qkv, bias_tbl, l, batch, rows)
        o_f, o_b = _gla(gqk, gv, gates, batch, seq)
        x1, h2_slabs, rt_t, counts = _out_route(y_na, o_f, o_b, gr, xf, l, w_o_na, w_o_gla,
                                                per_layer_row(gla_norm_g), per_layer_row(norm_ffn_g), w_rt3, b_rt)

        tile_start, tiles, n_tiles, pad_tile, offs_b, sorted_rows = _dispatch_plan(counts, n)
        pos, rt, xs = _dispatch(pad_tile, n_tiles, rt_t, offs_b, h2_slabs, sorted_rows)
        ys = _experts(tile_start, tiles, xs, w_gate, w_up, w_down, l)
        pos = pos.reshape(-1)
        if l + 1 < depth:
            xf, proj = _combine_in_proj(pos, ys, x1, rt, l + 1, *in_proj_params)
    return _combine_final(pos, ys, x1, rt, final_norm_g[None, :]).reshape(batch, seq, d)
```

```python
import functools

import jax
import jax.numpy as jnp
from jax import lax
from jax.experimental import pallas as pl
from jax.experimental.pallas import tpu as pltpu

F32 = jnp.float32
BF16 = jnp.bfloat16

GRID_W = 64
NA_HEADS = 8
NA_HEAD_DIM = 64
NA_WIDTH = NA_HEADS * NA_HEAD_DIM
WIN_H_MAX = 8
WIN_W = 16
GLA_HEADS = 4
GLA_DK = 64
GLA_DV = 128
GLA_KEY_WIDTH = GLA_HEADS * GLA_DK
GLA_VAL_WIDTH = GLA_HEADS * GLA_DV
GLA_GATE_RANK = 16
GLA_GATE_NORMALIZER = 16.0
GLA_CHUNK = 64
N_GROUPS = 4
EXPERTS_PER_GROUP = 8
N_EXPERTS = N_GROUPS * EXPERTS_PER_GROUP
RMS_EPS = 1e-6

LANES = 128
SUBLANES = 8
MXU_WIDTH = 256
VMEM_LIMIT_BYTES = 56 * 1024 * 1024

MASK_VALUE = -1e30

IN_PROJ_ROWS = 512
NA_ROW_BLOCK = 8
NA_ROW_UNROLL = 8
GLA_STEP_CHUNKS = 4
OUT_ROWS = 512
EXPERT_ROWS = 256
EXPERT_TILE_BUFFERS = 4
DISPATCH_ROWS = 1024
COMBINE_ROWS = DISPATCH_ROWS
COMBINE_STEP_ROWS = 512


def _params(*sem):
    return pltpu.CompilerParams(dimension_semantics=sem, vmem_limit_bytes=VMEM_LIMIT_BYTES)


def _rms(x, g):
    return x * lax.rsqrt(jnp.mean(x * x, axis=-1, keepdims=True) + RMS_EPS) * g


def _in_proj_rows(x, g_ref, w_na_ref, w_qk_ref, w_v_ref, w_r_ref, w_lr_ref,
                  w_g2_ref, b_g_ref, na_ref, qk_ref, v_ref, r_ref, gate_ref, w_gate):
    @pl.when(pl.program_id(0) == 0)
    def _():
        w_gate[...] = jnp.dot(w_lr_ref[...], w_g2_ref[...], preferred_element_type=F32,
                              precision=lax.Precision.HIGHEST).astype(BF16)

    h = _rms(x, g_ref[...]).astype(BF16)
    na_ref[...] = jnp.dot(h, w_na_ref[...], preferred_element_type=F32).astype(BF16)
    qk_ref[...] = jnp.dot(h, w_qk_ref[...], preferred_element_type=F32)
    v_ref[...] = jnp.dot(h, w_v_ref[...], preferred_element_type=F32).astype(BF16)
    r_ref[...] = jnp.dot(h, w_r_ref[...], preferred_element_type=F32)
    z = jnp.dot(h, w_gate[...], preferred_element_type=F32) + b_g_ref[...]
    log_sig = jnp.minimum(z, 0.0) - jnp.log(1.0 + jnp.exp(-jnp.abs(z)))
    gate_ref[...] = log_sig * (1.0 / GLA_GATE_NORMALIZER)


def _in_proj_kernel(x_ref, *refs):
    _in_proj_rows(x_ref[...], *refs)


def _in_proj_outputs(n, rows):
    row_spec = lambda width: pl.BlockSpec((rows, width), lambda i, *_: (i, 0))
    widths = (3 * NA_WIDTH, 2 * GLA_KEY_WIDTH, GLA_VAL_WIDTH, GLA_VAL_WIDTH, 2 * GLA_KEY_WIDTH)
    dtypes = (BF16, F32, BF16, F32, F32)
    return ([row_spec(w) for w in widths],
            [jax.ShapeDtypeStruct((n, w), t) for w, t in zip(widths, dtypes)])


def _layer_spec(a, layer):
    return pl.BlockSpec((None,) + a.shape[1:], lambda *_: (layer,) + (0,) * (a.ndim - 1))


def _in_proj(x, layer, g, w_na, w_qk, w_v, w_r, w_lr, w_g2, b_g):
    n, d = x.shape
    rows = IN_PROJ_ROWS
    params = (g, w_na, w_qk, w_v, w_r, w_lr, w_g2, b_g)
    out_specs, out_shape = _in_proj_outputs(n, rows)
    return pl.pallas_call(
        _in_proj_kernel,
        grid=(n // rows,),
        in_specs=[pl.BlockSpec((rows, d), lambda i: (i, 0))] + [_layer_spec(a, layer) for a in params],
        out_specs=out_specs,
        out_shape=out_shape,
        scratch_shapes=[pltpu.VMEM((d, 2 * GLA_KEY_WIDTH), BF16)],
        compiler_params=_params("arbitrary"),
        name="in_proj",
    )(x, *params)


def _na_bias_table(rpb, rows):
    kh = min(WIN_H_MAX, rows)
    w = jnp.arange(GRID_W)[:, None, None]
    x = jnp.arange(GRID_W)[None, :, None]
    cs = jnp.clip(w - WIN_W // 2, 0, GRID_W - WIN_W)
    valid = (x >= cs) & (x < cs + WIN_W)
    col_sel = (valid & (jnp.arange(2 * WIN_W - 1)[None, None, :] == x - w + (WIN_W - 1))).astype(F32)
    by_col = jnp.einsum("hrc,wxc->hwrx", rpb.astype(F32), col_sel, precision=lax.Precision.HIGHEST)
    by_col = jnp.where(valid[None, :, None, :, 0], by_col, MASK_VALUE)
    by_col = by_col.reshape(NA_HEADS * GRID_W, (2 * WIN_H_MAX - 1) * GRID_W)
    bias = jnp.stack([by_col[:, (WIN_H_MAX - 1 - c) * GRID_W:(WIN_H_MAX - 1 - c + kh) * GRID_W]
                      for c in range(kh)])
    return bias.reshape(kh, NA_HEADS // 2, 2 * GRID_W, kh * GRID_W)


def _na_window_start(j, rows, kh):
    return jnp.clip(j * NA_ROW_BLOCK - kh // 2, 0, rows - (NA_ROW_BLOCK + kh))


def _na_kernel(q_ref, k_ref, v_ref, tbl_ref, o_ref, *, rows, kh):
    rb = NA_ROW_BLOCK
    j = pl.program_id(1)
    fetched = _na_window_start(j, rows, kh)
    lane = lax.broadcasted_iota(jnp.int32, (GRID_W, LANES), 1)
    first = lane < NA_HEAD_DIM

    def row_body(lr, carry):
        r = j * rb + lr
        start = jnp.clip(r - kh // 2, 0, rows - kh)
        cls = r - start
        q_row = q_ref[lr]
        k_win = k_ref[0, pl.ds(start - fetched, kh)].reshape(kh * GRID_W, NA_WIDTH)
        v_win = v_ref[0, pl.ds(start - fetched, kh)].reshape(kh * GRID_W, NA_WIDTH)
        pairs = [slice(p * LANES, (p + 1) * LANES) for p in range(NA_HEADS // 2)]
        scores = []
        for sl in pairs:
            q_pair = q_row[:, sl]
            zero = jnp.zeros_like(q_pair)
            q_bd = jnp.concatenate([jnp.where(first, q_pair, zero), jnp.where(first, zero, q_pair)], axis=0)
            scores.append(lax.dot_general(q_bd, k_win[:, sl], (((1,), (1,)), ((), ())),
                                          preferred_element_type=F32))
        probs, denoms = [], []
        for p, s in enumerate(scores):
            s = s + tbl_ref[cls, p]
            e = jnp.exp(s - jnp.max(s, axis=-1, keepdims=True))
            denoms.append(jnp.sum(e, axis=-1, keepdims=True))
            probs.append(e.astype(BF16))
        outs = []
        for sl, e, denom in zip(pairs, probs, denoms):
            o = jnp.dot(e, v_win[:, sl], preferred_element_type=F32) / denom
            outs.append(jnp.where(first, o[:GRID_W], o[GRID_W:]))
        o_ref[lr] = jnp.concatenate(outs, axis=-1).astype(o_ref.dtype)
        return carry

    lax.fori_loop(0, rb, row_body, 0, unroll=NA_ROW_UNROLL)


def _na(na_qkv, tbl, layer, batch, rows):
    kh = min(WIN_H_MAX, rows)
    rb = NA_ROW_BLOCK
    nblk = rows // rb
    x4 = na_qkv.reshape(batch, rows, GRID_W, 3 * NA_WIDTH)
    assert rows >= rb + kh
    blk = (None, rb, GRID_W, NA_WIDTH)
    window = (pl.Element(1), pl.Element(rb + kh), pl.Element(GRID_W), pl.Element(NA_WIDTH))
    specs = [pl.BlockSpec(blk, lambda b, j: (b, j, 0, 0))]
    for col in (1, 2):
        specs.append(pl.BlockSpec(window, lambda b, j, col=col: (b, _na_window_start(j, rows, kh), 0,
                                                                  col * NA_WIDTH)))
    specs.append(_layer_spec(tbl, layer))
    out = pl.pallas_call(
        functools.partial(_na_kernel, rows=rows, kh=kh),
        grid=(batch, nblk),
        in_specs=specs,
        out_specs=pl.BlockSpec(blk, lambda b, j: (b, j, 0, 0)),
        out_shape=jax.ShapeDtypeStruct((batch, rows, GRID_W, NA_WIDTH), BF16),
        compiler_params=_params("arbitrary", "arbitrary"),
        name="na",
    )(x4, x4, x4, tbl)
    return out.reshape(batch * rows * GRID_W, NA_WIDTH)


def _block_diag_mask(row_block, col_block, nblocks):
    shape = (row_block * nblocks, col_block * nblocks)
    r = lax.shift_right_logical(lax.broadcasted_iota(jnp.int32, shape, 0), row_block.bit_length() - 1)
    c = lax.shift_right_logical(lax.broadcasted_iota(jnp.int32, shape, 1), col_block.bit_length() - 1)
    return r == c


def _split_bf16x3(x):
    hi = x.astype(BF16)
    rest = x - hi.astype(F32)
    mid = rest.astype(BF16)
    lo = (rest - mid.astype(F32)).astype(BF16)
    return hi, mid, lo


def _gla_direction(qk_ref, v_ref, g_ref, *, backward):
    c = GLA_CHUNK
    nc = GLA_STEP_CHUNKS
    kw = GLA_KEY_WIDTH
    order = list(reversed(range(nc))) if backward else list(range(nc))
    chunk = lambda a, n: a[n * c:(n + 1) * c]

    def stage_cumsum():
        step = nc * c
        ti = lax.broadcasted_iota(jnp.int32, (step, step), 0)
        tj = lax.broadcasted_iota(jnp.int32, (step, step), 1)
        same_chunk = lax.shift_right_logical(ti, c.bit_length() - 1) == lax.shift_right_logical(tj, c.bit_length() - 1)
        tri = (tj >= ti) if backward else (tj <= ti)
        cum = jnp.where(same_chunk & tri, 1.0, 0.0).astype(BF16)
        return sum(jnp.dot(cum, piece, preferred_element_type=F32) for piece in _split_bf16x3(g_ref[...]))

    def stage_decays(b):
        ref_row, last_row = (c // 2, 0) if backward else (c // 2 - 1, c - 1)
        rows_of = lambda r: jnp.concatenate(
            [jnp.broadcast_to(b[n * c + r:n * c + r + 1, :], (c, kw)) for n in range(nc)], axis=0)
        b_ref, b_last = rows_of(ref_row), rows_of(last_row)
        q = qk_ref[:, :kw]
        k = qk_ref[:, kw:]
        q_rel = (q * jnp.exp(b - b_ref)).astype(BF16)
        k_rel = (k * jnp.exp(b_ref - b)).astype(BF16)
        k_dec = k * jnp.exp(b_last - b)
        q_dec = (q * jnp.exp(b)).astype(BF16)
        decay = jnp.exp(b_last)
        return q_rel, k_rel, k_dec, q_dec, decay

    def stage_scores(q_rel, k_rel):
        kk_mask = _block_diag_mask(c, GLA_DK, GLA_HEADS)
        out = []
        for n in range(nc):
            k_bd = jnp.where(kk_mask, jnp.concatenate([chunk(k_rel, n)] * GLA_HEADS, axis=0), jnp.zeros((), BF16))
            out.append(lax.dot_general(chunk(q_rel, n), k_bd, (((1,), (1,)), ((), ())),
                                       preferred_element_type=F32))
        return out

    def stage_intra(scores, k_dec, decay):
        si = lax.broadcasted_iota(jnp.int32, (c, c * GLA_HEADS), 0)
        sj = lax.broadcasted_iota(jnp.int32, (c, c * GLA_HEADS), 1) & (c - 1)
        keep = (sj > si) if backward else (sj <= si)
        kv_mask = _block_diag_mask(c, GLA_DV, GLA_HEADS)
        vk_mask = _block_diag_mask(GLA_DK, GLA_DV, 2)
        o_intra, upd, decay_col = [], [], []
        for n in range(nc):
            v = v_ref[n * c:(n + 1) * c, :]
            p = jnp.where(keep, scores[n], 0.0).astype(BF16)
            v_bd = jnp.where(kv_mask, jnp.concatenate([v] * GLA_HEADS, axis=0), jnp.zeros((), BF16))
            o_intra.append(jnp.dot(p, v_bd, preferred_element_type=F32))
            k_dec_t = chunk(k_dec, n).T.astype(BF16)
            pairs = []
            for hp in range(GLA_HEADS // 2):
                kv = jnp.dot(k_dec_t[hp * 2 * GLA_DK:(hp + 1) * 2 * GLA_DK],
                             v[:, hp * 2 * GLA_DV:(hp + 1) * 2 * GLA_DV], preferred_element_type=F32)
                pairs.append(jnp.where(vk_mask, kv, 0.0))
            upd.append(pairs)
            decay_col.append(chunk(decay, n).T[:, :1])
        return o_intra, upd, decay_col

    def scan_step(state, idx, q_dec, o_intra, upd, decay_col, out_ref):
        n = order[idx]
        q_n = chunk(q_dec, n)
        pw = 2 * GLA_DK
        o_inter = jnp.concatenate(
            [jnp.dot(q_n[:, hp * pw:(hp + 1) * pw], s.astype(BF16), preferred_element_type=F32)
             for hp, s in enumerate(state)], axis=1)
        out_ref[n * c:(n + 1) * c, :] = o_intra[n] + o_inter
        return [s * decay_col[n][hp * pw:(hp + 1) * pw] + u for hp, (s, u) in enumerate(zip(state, upd[n]))]

    return stage_cumsum, stage_decays, stage_scores, stage_intra, scan_step


def _gla_kernel(qk_f_ref, v_f_ref, g_f_ref, qk_b_ref, v_b_ref, g_b_ref, o_f_ref, o_b_ref,
                st_f, st_b):
    @pl.when(pl.program_id(1) == 0)
    def _():
        st_f[...] = jnp.zeros_like(st_f)
        st_b[...] = jnp.zeros_like(st_b)

    dirs = (_gla_direction(qk_f_ref, v_f_ref, g_f_ref, backward=False),
            _gla_direction(qk_b_ref, v_b_ref, g_b_ref, backward=True))
    b = [d[0]() for d in dirs]
    dec = [d[1](x) for d, x in zip(dirs, b)]
    scores = [d[2](x[0], x[1]) for d, x in zip(dirs, dec)]
    intra = [d[3](s, x[2], x[4]) for d, s, x in zip(dirs, scores, dec)]
    pairs = range(GLA_HEADS // 2)
    states = [[st[hp] for hp in pairs] for st in (st_f, st_b)]
    for idx in range(GLA_STEP_CHUNKS):
        for j, (d, x, y, out_ref) in enumerate(zip(dirs, dec, intra, (o_f_ref, o_b_ref))):
            states[j] = d[4](states[j], idx, x[3], *y, out_ref)
    for st, state in zip((st_f, st_b), states):
        for hp in pairs:
            st[hp] = state[hp]


def _gla(qk, v, gates, batch, seq):
    step = GLA_STEP_CHUNKS * GLA_CHUNK
    nblk = seq // step
    qk3 = qk.reshape(batch, seq, 2 * GLA_KEY_WIDTH)
    v3 = v.reshape(batch, seq, GLA_VAL_WIDTH)
    g3 = gates.reshape(batch, seq, 2 * GLA_KEY_WIDTH)
    fwd = lambda b, n: (b, n, 0)
    bwd = lambda b, n: (b, nblk - 1 - n, 0)
    bwd_gate = lambda b, n: (b, nblk - 1 - n, 1)
    o_f, o_b = pl.pallas_call(
        _gla_kernel,
        grid=(batch, nblk),
        in_specs=[pl.BlockSpec((None, step, 2 * GLA_KEY_WIDTH), fwd),
                  pl.BlockSpec((None, step, GLA_VAL_WIDTH), fwd),
                  pl.BlockSpec((None, step, GLA_KEY_WIDTH), fwd),
                  pl.BlockSpec((None, step, 2 * GLA_KEY_WIDTH), bwd),
                  pl.BlockSpec((None, step, GLA_VAL_WIDTH), bwd),
                  pl.BlockSpec((None, step, GLA_KEY_WIDTH), bwd_gate)],
        out_specs=[pl.BlockSpec((None, step, GLA_VAL_WIDTH), fwd),
                   pl.BlockSpec((None, step, GLA_VAL_WIDTH), bwd)],
        out_shape=[jax.ShapeDtypeStruct((batch, seq, GLA_VAL_WIDTH), F32)] * 2,
        scratch_shapes=[pltpu.VMEM((GLA_HEADS // 2, 2 * GLA_DK, 2 * GLA_DV), F32),
                        pltpu.VMEM((GLA_HEADS // 2, 2 * GLA_DK, 2 * GLA_DV), F32)],
        compiler_params=_params("arbitrary", "arbitrary"),
        name="gla",
    )(qk3, v3, g3, qk3, v3, g3)
    return o_f.reshape(batch * seq, GLA_VAL_WIDTH), o_b.reshape(batch * seq, GLA_VAL_WIDTH)


def _store_slabs(ref, x):
    rows = x.shape[0]
    for s in range(SUBLANES):
        ref[pl.ds(s, rows, stride=SUBLANES), :] = x[:, s * LANES:(s + 1) * LANES]


def _load_slabs(ref, rows):
    return jnp.concatenate([ref[pl.ds(s, rows, stride=SUBLANES), :] for s in range(SUBLANES)], axis=1)


def _slab_copy(src_hbm, dst, sem, src_row, k):
    dst_row = k * SUBLANES if isinstance(k, int) else pl.multiple_of(k * SUBLANES, SUBLANES)
    return pltpu.make_async_copy(src_hbm.at[pl.ds(pl.multiple_of(src_row, SUBLANES), SUBLANES), :],
                                 dst.at[pl.ds(dst_row, SUBLANES), :], sem)


def _wait_slabs(src_hbm, dst, sem):
    pltpu.make_async_copy(src_hbm.at[pl.ds(0, dst.shape[0]), :], dst, sem).wait()


RT_E1, RT_E2, RT_RANK1, RT_RANK2, RT_W1, RT_W2 = range(6)
GROUP_ROW0 = N_EXPERTS
ROUTE_ROWS = 64


def _out_route_kernel(na_ref, of_ref, ob_ref, r_ref, x_ref, w_na_ref, w_gla_ref, gn_ref, fn_ref,
                      w_rt_ref, b_rt_ref, x1_ref, h2_ref, rt_ref, cnt_ref, carry):
    @pl.when(pl.program_id(0) == 0)
    def _():
        carry[...] = jnp.zeros_like(carry)

    rows = x_ref.shape[0]
    o = of_ref[...] + ob_ref[...]
    r = r_ref[...]
    parts = []
    for h in range(GLA_HEADS):
        sl = slice(h * GLA_DV, (h + 1) * GLA_DV)
        parts.append(_rms(o[:, sl], gn_ref[...]) * (r[:, sl] * jax.nn.sigmoid(r[:, sl])))
    y_gla = jnp.concatenate(parts, axis=-1).astype(BF16)
    x1 = (x_ref[...] + jnp.dot(na_ref[...], w_na_ref[...], preferred_element_type=F32)
          + jnp.dot(y_gla, w_gla_ref[...], preferred_element_type=F32))
    x1_ref[...] = x1
    h2 = _rms(x1, fn_ref[...])
    _store_slabs(h2_ref, h2)

    h_hi = h2.astype(BF16)
    h_lo = (h2 - h_hi.astype(F32)).astype(BF16)
    logits = lax.dot_general(w_rt_ref[...], jnp.concatenate([h_hi, h_lo, h_hi], axis=1),
                             (((1,), (1,)), ((), ())), preferred_element_type=F32)
    logits = logits[:ROUTE_ROWS] + b_rt_ref[...]
    row_i = lax.broadcasted_iota(jnp.int32, (ROUTE_ROWS, rows), 0)
    row = row_i.astype(F32)
    row_grp = lax.shift_right_logical(row_i, EXPERTS_PER_GROUP.bit_length() - 1).astype(F32)
    neg = jnp.float32(-jnp.inf)
    no_row = jnp.float32(ROUTE_ROWS)
    is_grp = (row_i >= GROUP_ROW0) & (row_i < GROUP_ROW0 + N_GROUPS)
    g_logit = jnp.where(is_grp, logits, neg)
    g_max = jnp.max(g_logit, axis=0, keepdims=True)
    g_sel = jnp.min(jnp.where(is_grp & (g_logit == g_max), row, no_row), axis=0, keepdims=True) - GROUP_ROW0
    grp_w = 1.0 / jnp.sum(jnp.where(is_grp, jnp.exp(g_logit - g_max), 0.0), axis=0, keepdims=True)
    in_grp = (row_i < N_EXPERTS) & (row_grp == g_sel)
    e_logit = jnp.where(in_grp, logits, neg)
    v1 = jnp.max(e_logit, axis=0, keepdims=True)
    i1 = jnp.min(jnp.where(in_grp & (e_logit == v1), row, no_row), axis=0, keepdims=True)
    rest = in_grp & (row != i1)
    e_logit2 = jnp.where(rest, logits, neg)
    v2 = jnp.max(e_logit2, axis=0, keepdims=True)
    i2 = jnp.min(jnp.where(rest & (e_logit2 == v2), row, no_row), axis=0, keepdims=True)
    t = jnp.exp(v2 - v1)
    w1 = grp_w / (1.0 + t)
    w2 = grp_w * t / (1.0 + t)

    sel1 = row == i1
    sel2 = row == i2
    onehot = jnp.where(sel1 | sel2, 1.0, 0.0)
    ti = lax.broadcasted_iota(jnp.int32, (rows, rows), 0)
    tj = lax.broadcasted_iota(jnp.int32, (rows, rows), 1)
    earlier = jnp.where(ti < tj, 1.0, 0.0).astype(BF16)
    ranks = jnp.dot(onehot.astype(BF16), earlier, preferred_element_type=F32) + carry[:, :1]
    rank1 = jnp.sum(jnp.where(sel1, ranks, 0.0), axis=0, keepdims=True)
    rank2 = jnp.sum(jnp.where(sel2, ranks, 0.0), axis=0, keepdims=True)
    new_carry = carry[...] + jnp.sum(onehot, axis=1, keepdims=True)
    carry[...] = new_carry
    cnt_ref[...] = new_carry

    fields = {RT_E1: i1, RT_E2: i2, RT_RANK1: rank1, RT_RANK2: rank2, RT_W1: w1, RT_W2: w2}
    zero = jnp.zeros_like(w1)
    rt_ref[...] = jnp.concatenate([fields.get(f, zero) for f in range(SUBLANES)], axis=0)


def _out_route(y_na, o_f, o_b, r, x, layer, w_na, w_gla, gn, fn, w_rt, b_rt):
    n, d = x.shape
    rows = OUT_ROWS
    row_spec = lambda width: pl.BlockSpec((rows, width), lambda i: (i, 0))
    full = lambda a: _layer_spec(a, layer)
    return pl.pallas_call(
        _out_route_kernel,
        grid=(n // rows,),
        in_specs=[row_spec(NA_WIDTH), row_spec(GLA_VAL_WIDTH), row_spec(GLA_VAL_WIDTH),
                  row_spec(GLA_VAL_WIDTH), row_spec(d), full(w_na), full(w_gla), full(gn), full(fn),
                  full(w_rt), full(b_rt)],
        out_specs=[row_spec(d), pl.BlockSpec((rows * SUBLANES, LANES), lambda i: (i, 0)),
                   pl.BlockSpec((SUBLANES, rows), lambda i: (0, i)),
                   pl.BlockSpec((ROUTE_ROWS, LANES), lambda i: (0, 0))],
        out_shape=[jax.ShapeDtypeStruct((n, d), F32), jax.ShapeDtypeStruct((n * SUBLANES, LANES), F32),
                   jax.ShapeDtypeStruct((SUBLANES, n), F32), jax.ShapeDtypeStruct((ROUTE_ROWS, LANES), F32)],
        scratch_shapes=[pltpu.VMEM((ROUTE_ROWS, LANES), F32)],
        compiler_params=_params("arbitrary"),
        name="out_route",
    )(y_na, o_f, o_b, r, x, w_na, w_gla, gn, fn, w_rt, b_rt)


def _dispatch_kernel(pad_tile_ref, n_tiles_ref, rt_ref, offs_ref, h_hbm, pos_ref, rec_ref, xs_hbm,
                     zeros, hbuf, pos_vmem, pos_smem, sem_zero, sem_pos, sem_in, sem):
    i = pl.program_id(0)
    n_steps = pl.num_programs(0)
    db = DISPATCH_ROWS
    tile = EXPERT_ROWS * SUBLANES
    max_tiles = xs_hbm.shape[0] // tile
    buf = lax.rem(i, 2)

    def block_copy(step, b):
        rows = pl.ds(pl.multiple_of(step * db * SUBLANES, db * SUBLANES), db * SUBLANES)
        return pltpu.make_async_copy(h_hbm.at[rows, :], hbuf.at[b], sem_in.at[b])

    def wait_row_copies(b):
        for _ in range(2):
            pltpu.make_async_copy(hbuf.at[b], xs_hbm.at[pl.ds(0, db * SUBLANES), :], sem).wait()

    def zero_copy(t):
        return pltpu.make_async_copy(zeros, xs_hbm.at[pl.ds(pl.multiple_of(t * tile, tile), tile), :], sem_zero)

    def for_each_zero_tile(fn):
        def padded(e, carry):
            @pl.when(pad_tile_ref[e] >= 0)
            def _():
                fn(zero_copy(pad_tile_ref[e]))
            return carry
        lax.fori_loop(0, N_EXPERTS, padded, 0)

        def tail(t, carry):
            fn(zero_copy(t))
            return carry
        lax.fori_loop(n_tiles_ref[0], max_tiles, tail, 0)

    @pl.when(i == 0)
    def _():
        block_copy(0, 0).start()
        zeros[...] = jnp.zeros_like(zeros)
        for_each_zero_tile(lambda cp: cp.start())

    rt_t = rt_ref[...]
    rec_ref[...] = jnp.concatenate([rt_t, jnp.zeros((LANES - SUBLANES, db), F32)], axis=0).T
    expert = lax.broadcasted_iota(jnp.int32, (N_EXPERTS, db), 0).astype(F32)
    pos = []
    for e_lane, rank_lane in ((RT_E1, RT_RANK1), (RT_E2, RT_RANK2)):
        seg = jnp.sum(jnp.where(expert == rt_t[e_lane:e_lane + 1, :], offs_ref[...], 0.0), axis=0, keepdims=True)
        pos.append((seg + rt_t[rank_lane:rank_lane + 1, :]) * SUBLANES)
    pos = jnp.concatenate(pos, axis=0).astype(jnp.int32)
    pos_ref[...] = pos
    pos_vmem[...] = jnp.concatenate([pos, jnp.zeros((SUBLANES - 2, db), jnp.int32)], axis=0)
    to_smem = pltpu.make_async_copy(pos_vmem, pos_smem, sem_pos)
    to_smem.start()

    @pl.when(i == 0)
    def _():
        for_each_zero_tile(lambda cp: cp.wait())

    @pl.when(i > 0)
    def _():
        wait_row_copies(1 - buf)

    to_smem.wait()
    block_copy(i, buf).wait()

    @pl.when(i + 1 < n_steps)
    def _():
        block_copy(i + 1, 1 - buf).start()

    h_blk = hbuf.at[buf]
    for k in range(db):
        src = h_blk.at[pl.ds(k * SUBLANES, SUBLANES), :]
        for slot in range(2):
            dst_row = pl.multiple_of(pos_smem[slot, k], SUBLANES)
            pltpu.make_async_copy(src, xs_hbm.at[pl.ds(dst_row, SUBLANES), :], sem).start(priority=slot)

    @pl.when(i == n_steps - 1)
    def _():
        wait_row_copies(buf)


def _dispatch(pad_tile, n_tiles, rt_t, offs_b, h2_slabs, sorted_rows):
    n = rt_t.shape[1]
    db = DISPATCH_ROWS
    grid_spec = pltpu.PrefetchScalarGridSpec(
        num_scalar_prefetch=2,
        grid=(n // db,),
        in_specs=[pl.BlockSpec((SUBLANES, db), lambda i, pt, nt: (0, i)),
                  pl.BlockSpec(offs_b.shape, lambda i, pt, nt: (0, 0)),
                  pl.BlockSpec(memory_space=pl.ANY)],
        out_specs=[pl.BlockSpec((None, 2, db), lambda i, pt, nt: (i, 0, 0)),
                   pl.BlockSpec((db, LANES), lambda i, pt, nt: (i, 0)),
                   pl.BlockSpec(memory_space=pl.ANY)],
        scratch_shapes=[pltpu.VMEM((EXPERT_ROWS * SUBLANES, LANES), F32),
                        pltpu.VMEM((2, db * SUBLANES, LANES), F32),
                        pltpu.VMEM((SUBLANES, db), jnp.int32), pltpu.SMEM((SUBLANES, db), jnp.int32),
                        pltpu.SemaphoreType.DMA(()), pltpu.SemaphoreType.DMA(()),
                        pltpu.SemaphoreType.DMA((2,)), pltpu.SemaphoreType.DMA(())],
    )
    return pl.pallas_call(
        _dispatch_kernel,
        grid_spec=grid_spec,
        out_shape=[jax.ShapeDtypeStruct((n // db, 2, db), jnp.int32),
                   jax.ShapeDtypeStruct((n, LANES), F32),
                   jax.ShapeDtypeStruct((sorted_rows * SUBLANES, LANES), F32)],
        compiler_params=_params("arbitrary"),
        name="dispatch",
    )(pad_tile, n_tiles, rt_t, offs_b, h2_slabs)


TILE_DMA_PRIORITY = 1


def _expert_tile(x_ref, y_ref, wg_s, wu_s, wd_s):
    rows = EXPERT_ROWS
    x = _load_slabs(x_ref, rows).astype(BF16)
    hid = []
    for c in range(0, wg_s.shape[1], MXU_WIDTH):
        gate = jnp.dot(x, wg_s[:, c:c + MXU_WIDTH], preferred_element_type=F32)
        up = jnp.dot(x, wu_s[:, c:c + MXU_WIDTH], preferred_element_type=F32)
        hid.append((gate * jax.nn.sigmoid(gate) * up).astype(BF16))
    hid = jnp.concatenate(hid, axis=1)
    for c in range(0, wd_s.shape[1], MXU_WIDTH):
        y = jnp.dot(hid, wd_s[:, c:c + MXU_WIDTH], preferred_element_type=F32)
        for s in range(c // LANES, (c + MXU_WIDTH) // LANES):
            y_ref[pl.ds(s, rows, stride=SUBLANES), :] = y[:, s * LANES - c:(s + 1) * LANES - c]


def _experts_kernel(tile_start_ref, tiles_ref, x_hbm, wg_hbm, wu_hbm, wd_hbm, y_hbm,
                    xbuf, ybuf, wg_f, wu_f, wd_f, wg_s, wu_s, wd_s, sem_x, sem_y, sem_w, *, layer):
    nbuf = EXPERT_TILE_BUFFERS
    e = pl.program_id(0)
    last_step = e == pl.num_programs(0) - 1
    tile = EXPERT_ROWS * SUBLANES
    n = tiles_ref[e]
    base = tile_start_ref[e]
    n_tiles = tile_start_ref[N_EXPERTS - 1] + tiles_ref[N_EXPERTS - 1]
    tile_rows = lambda g: pl.ds(pl.multiple_of(g * tile, tile), tile)

    def x_copy(g, slot):
        return pltpu.make_async_copy(x_hbm.at[tile_rows(g), :], xbuf.at[slot], sem_x.at[slot])

    def y_copy(g, slot):
        return pltpu.make_async_copy(ybuf.at[slot], y_hbm.at[tile_rows(g), :], sem_y.at[slot])

    def w_copies(expert, slot):
        return [pltpu.make_async_copy(w_hbm.at[layer, expert], stage.at[slot], sem_w.at[slot])
                for w_hbm, stage in ((wg_hbm, wg_f), (wu_hbm, wu_f), (wd_hbm, wd_f))]

    w_slot = lax.rem(e, 2)

    @pl.when(e == 0)
    def _():
        for g in range(nbuf - 1):
            @pl.when(g < n_tiles)
            def _():
                x_copy(g, g).start(priority=TILE_DMA_PRIORITY)
        for cp in w_copies(0, 0):
            cp.start()

    for cp in w_copies(e, w_slot):
        cp.wait()

    @pl.when(jnp.logical_not(last_step))
    def _():
        for cp in w_copies(e + 1, 1 - w_slot):
            cp.start()

    @pl.when(n > 0)
    def _():
        wg_s[...] = wg_f[w_slot].astype(BF16)
        wu_s[...] = wu_f[w_slot].astype(BF16)
        wd_s[...] = wd_f[w_slot].astype(BF16)

        def tile_body(g, carry):
            slot = lax.rem(g, nbuf)
            x_copy(g, slot).wait()

            @pl.when(g + nbuf - 1 < n_tiles)
            def _():
                x_copy(g + nbuf - 1, lax.rem(g + nbuf - 1, nbuf)).start(priority=TILE_DMA_PRIORITY)

            @pl.when(g >= nbuf)
            def _():
                y_copy(g - nbuf, slot).wait()

            _expert_tile(xbuf.at[slot], ybuf.at[slot], wg_s, wu_s, wd_s)
            y_copy(g, slot).start(priority=TILE_DMA_PRIORITY)
            return carry
        lax.fori_loop(base, base + n, tile_body, 0)

    @pl.when(last_step)
    def _():
        for back in range(nbuf, 0, -1):
            @pl.when(n_tiles >= back)
            def _():
                y_copy(n_tiles - back, lax.rem(n_tiles - back, nbuf)).wait()

        first_unused = n_tiles
        max_tiles = y_hbm.shape[0] // tile
        ybuf[0] = jnp.zeros(ybuf.shape[1:], ybuf.dtype)
        zero_copy = lambda t: pltpu.make_async_copy(
            ybuf.at[0], y_hbm.at[pl.ds(pl.multiple_of(t * tile, tile), tile), :], sem_y.at[0])

        def start(t, carry):
            zero_copy(t).start()
            return carry

        def wait(t, carry):
            zero_copy(t).wait()
            return carry
        lax.fori_loop(first_unused, max_tiles, start, 0)
        lax.fori_loop(first_unused, max_tiles, wait, 0)


def _experts(tile_start, tiles, xs, wg, wu, wd, layer):
    rows = EXPERT_ROWS
    _, n_experts, d, dff = wg.shape
    assert d == SUBLANES * LANES and n_experts == N_EXPERTS
    nbuf = EXPERT_TILE_BUFFERS
    any_spec = pl.BlockSpec(memory_space=pl.ANY)
    grid_spec = pltpu.PrefetchScalarGridSpec(
        num_scalar_prefetch=2,
        grid=(n_experts,),
        in_specs=[any_spec, any_spec, any_spec, any_spec],
        out_specs=any_spec,
        scratch_shapes=[pltpu.VMEM((nbuf, rows * SUBLANES, LANES), F32),
                        pltpu.VMEM((nbuf, rows * SUBLANES, LANES), F32),
                        pltpu.VMEM((2, d, dff), F32), pltpu.VMEM((2, d, dff), F32), pltpu.VMEM((2, dff, d), F32),
                        pltpu.VMEM((d, dff), BF16), pltpu.VMEM((d, dff), BF16), pltpu.VMEM((dff, d), BF16),
                        pltpu.SemaphoreType.DMA((nbuf,)), pltpu.SemaphoreType.DMA((nbuf,)),
                        pltpu.SemaphoreType.DMA((2,))],
    )
    return pl.pallas_call(
        functools.partial(_experts_kernel, layer=layer),
        grid_spec=grid_spec,
        out_shape=jax.ShapeDtypeStruct(xs.shape, F32),
        compiler_params=_params("arbitrary"),
        name="experts",
    )(tile_start, tiles, xs, wg, wu, wd)


def _start_slab_gather(src_hbm, dst, sem, row_ref, base, count, priority):
    for k in range(count):
        _slab_copy(src_hbm, dst, sem, row_ref[base + k], k).start(priority=priority)


def _combined_rows(pos_ref, ys_hbm, x1_ref, rt_ref, buf, sem):
    i = pl.program_id(0)
    n_steps = pl.num_programs(0)
    rows = x1_ref.shape[0]
    steps_per_block = COMBINE_ROWS // rows
    slot = i % 2

    def start(step, s):
        first = lax.div(step, steps_per_block) * (2 * COMBINE_ROWS) + lax.rem(step, steps_per_block) * rows
        for j in range(2):
            _start_slab_gather(ys_hbm, buf.at[s, j], sem.at[s], pos_ref, first + j * COMBINE_ROWS, rows, priority=j)

    @pl.when(i == 0)
    def _():
        start(0, 0)

    @pl.when(i + 1 < n_steps)
    def _():
        start(i + 1, 1 - slot)

    _wait_slabs(ys_hbm, buf.at[slot, 0], sem.at[slot])
    _wait_slabs(ys_hbm, buf.at[slot, 1], sem.at[slot])
    rt = rt_ref[...]
    lane = lax.broadcasted_iota(jnp.int32, rt.shape, 1)
    w1 = jnp.sum(jnp.where(lane == RT_W1, rt, 0.0), axis=-1, keepdims=True)
    w2 = jnp.sum(jnp.where(lane == RT_W2, rt, 0.0), axis=-1, keepdims=True)
    y = w1 * _load_slabs(buf.at[slot, 0], rows) + w2 * _load_slabs(buf.at[slot, 1], rows)
    return x1_ref[...] + y


def _combine_scratch(rows):
    return [pltpu.VMEM((2, 2, rows * SUBLANES, LANES), F32), pltpu.SemaphoreType.DMA((2,))]


def _combine_final_kernel(pos_ref, ys_hbm, x1_ref, rt_ref, g_ref, o_ref, buf, sem):
    o_ref[...] = _rms(_combined_rows(pos_ref, ys_hbm, x1_ref, rt_ref, buf, sem), g_ref[...])


def _combine_final(pos, ys, x1, rt, g):
    n, d = x1.shape
    rows = COMBINE_STEP_ROWS
    grid_spec = pltpu.PrefetchScalarGridSpec(
        num_scalar_prefetch=1,
        grid=(n // rows,),
        in_specs=[pl.BlockSpec(memory_space=pl.ANY),
                  pl.BlockSpec((rows, d), lambda i, pos: (i, 0)),
                  pl.BlockSpec((rows, LANES), lambda i, pos: (i, 0)),
                  pl.BlockSpec(g.shape, lambda i, pos: (0, 0))],
        out_specs=pl.BlockSpec((rows, d), lambda i, pos: (i, 0)),
        scratch_shapes=_combine_scratch(rows),
    )
    return pl.pallas_call(
        _combine_final_kernel,
        grid_spec=grid_spec,
        out_shape=jax.ShapeDtypeStruct((n, d), F32),
        compiler_params=_params("arbitrary"),
        name="combine",
    )(pos, ys, x1, rt, g)


def _combine_in_proj_kernel(pos_ref, ys_hbm, x1_ref, rt_ref, *refs):
    param_refs, (x_ref, *proj_refs), (w_gate, buf, sem) = refs[:8], refs[8:14], refs[14:]
    x = _combined_rows(pos_ref, ys_hbm, x1_ref, rt_ref, buf, sem)
    x_ref[...] = x
    _in_proj_rows(x, *param_refs, *proj_refs, w_gate)


def _combine_in_proj(pos, ys, x1, rt, layer, g, w_na, w_qk, w_v, w_r, w_lr, w_g2, b_g):
    n, d = x1.shape
    rows = COMBINE_STEP_ROWS
    params = (g, w_na, w_qk, w_v, w_r, w_lr, w_g2, b_g)
    proj_specs, proj_shapes = _in_proj_outputs(n, rows)
    grid_spec = pltpu.PrefetchScalarGridSpec(
        num_scalar_prefetch=1,
        grid=(n // rows,),
        in_specs=[pl.BlockSpec(memory_space=pl.ANY),
                  pl.BlockSpec((rows, d), lambda i, pos: (i, 0)),
                  pl.BlockSpec((rows, LANES), lambda i, pos: (i, 0))] + [_layer_spec(a, layer) for a in params],
        out_specs=[pl.BlockSpec((rows, d), lambda i, pos: (i, 0))] + proj_specs,
        scratch_shapes=[pltpu.VMEM((d, 2 * GLA_KEY_WIDTH), BF16)] + _combine_scratch(rows),
    )
    x, *proj = pl.pallas_call(
        _combine_in_proj_kernel,
        grid_spec=grid_spec,
        out_shape=[jax.ShapeDtypeStruct((n, d), F32)] + proj_shapes,
        compiler_params=_params("arbitrary"),
        name="combine_in_proj",
    )(pos, ys, x1, rt, *params)
    return x, proj


def _dispatch_plan(counts, n):
    rows = EXPERT_ROWS
    max_tiles = (2 * n) // rows + N_EXPERTS
    cnt = counts[:N_EXPERTS, 0].astype(jnp.int32)
    tiles = (cnt + rows - 1) // rows
    tile_end = jnp.cumsum(tiles)
    tile_start = tile_end - tiles
    n_tiles = tile_end[-1:]
    pad_tile = jnp.where(cnt % rows != 0, tile_end - 1, -1).astype(jnp.int32)
    offs_b = jnp.broadcast_to((tile_start * rows).astype(F32)[:, None], (N_EXPERTS, DISPATCH_ROWS))
    return tile_start.astype(jnp.int32), tiles, n_tiles, pad_tile, offs_b, max_tiles * rows


def kernel(x, norm_mix_g, w_in, w_g2_f, b_g_f, w_g2_b, b_g_b, gla_norm_g, rpb, w_out, norm_ffn_g, w_grp, b_grp,
           w_exp, b_exp, w_gate, w_up, w_down, final_norm_g):
    batch, seq, d = x.shape
    n = batch * seq
    rows = seq // GRID_W
    depth = w_in.shape[0]
    xf = x.reshape(n, d)
    c_na = 3 * NA_WIDTH
    c_qk = c_na + 2 * GLA_KEY_WIDTH
    c_v = c_qk + GLA_VAL_WIDTH
    c_r = c_v + GLA_VAL_WIDTH

    q_scale = jnp.concatenate([jnp.full((NA_WIDTH,), NA_HEAD_DIM ** -0.5, F32), jnp.ones((2 * NA_WIDTH,), F32)])
    w_na = (w_in[:, :, :c_na] * q_scale).astype(BF16)
    qk_scale = jnp.concatenate([jnp.full((GLA_KEY_WIDTH,), GLA_DK ** -0.5, F32), jnp.ones((GLA_KEY_WIDTH,), F32)])
    w_qk = (w_in[:, :, c_na:c_qk] * qk_scale).astype(BF16)
    w_v = w_in[:, :, c_qk:c_v].astype(BF16)
    w_r = w_in[:, :, c_v:c_r].astype(BF16)
    w_lr = jnp.pad(w_in[:, :, c_r:], ((0, 0), (0, 0), (0, LANES - 2 * GLA_GATE_RANK)))
    zero_g2 = jnp.zeros_like(w_g2_f)
    w_g2 = jnp.concatenate([jnp.concatenate([w_g2_f, zero_g2], axis=2),
                            jnp.concatenate([zero_g2, w_g2_b], axis=2)], axis=1)
    w_g2 = jnp.pad(w_g2, ((0, 0), (0, LANES - 2 * GLA_GATE_RANK), (0, 0)))
    b_g = jnp.concatenate([b_g_f, b_g_b], axis=1)[:, None, :]
    bias_tbl = jax.vmap(functools.partial(_na_bias_table, rows=rows))(rpb)
    w_o = w_out.astype(BF16)
    w_o_na, w_o_gla = w_o[:, :NA_WIDTH], w_o[:, NA_WIDTH:]
    w_rt = jnp.pad(jnp.concatenate([w_exp, w_grp], axis=2), ((0, 0), (0, 0), (0, LANES - N_EXPERTS - N_GROUPS)))
    w_rt_hi = w_rt.astype(BF16)
    w_rt_lo = (w_rt - w_rt_hi.astype(F32)).astype(BF16)
    w_rt3 = jnp.concatenate([w_rt_hi, w_rt_hi, w_rt_lo], axis=1).transpose(0, 2, 1)
    b_rt = jnp.pad(jnp.concatenate([b_exp, b_grp], axis=1),
                   ((0, 0), (0, ROUTE_ROWS - N_EXPERTS - N_GROUPS)))[:, :, None]
    per_layer_row = lambda a: a[:, None, :]

    in_proj_params = (per_layer_row(norm_mix_g), w_na, w_qk, w_v, w_r, w_lr, w_g2, b_g)
    proj = _in_proj(xf, 0, *in_proj_params)
    for l in range(depth):
        na_qkv, gqk, gv, gr, gates = proj
        y_na = _na(na_qkv, bias_tbl, l, batch, rows)
        o_f, o_b = _gla(gqk, gv, gates, batch, seq)
        x1, h2_slabs, rt_t, counts = _out_route(y_na, o_f, o_b, gr, xf, l, w_o_na, w_o_gla,
                                                per_layer_row(gla_norm_g), per_layer_row(norm_ffn_g), w_rt3, b_rt)

        tile_start, tiles, n_tiles, pad_tile, offs_b, sorted_rows = _dispatch_plan(counts, n)
        pos, rt, xs = _dispatch(pad_tile, n_tiles, rt_t, offs_b, h2_slabs, sorted_rows)
        ys = _experts(tile_start, tiles, xs, w_gate, w_up, w_down, l)
        pos = pos.reshape(-1)
        if l + 1 < depth:
            xf, proj = _combine_in_proj(pos, ys, x1, rt, l + 1, *in_proj_params)
    return _combine_final(pos, ys, x1, rt, final_norm_g[None, :]).reshape(batch, seq, d)
```

```python
import functools

import jax
import jax.numpy as jnp
from jax import lax
from jax.experimental import pallas as pl
from jax.experimental.pallas import tpu as pltpu

F32 = jnp.float32
BF16 = jnp.bfloat16

GRID_W = 64
NA_HEADS = 8
NA_HEAD_DIM = 64
NA_WIDTH = NA_HEADS * NA_HEAD_DIM
WIN_H_MAX = 8
WIN_W = 16
GLA_HEADS = 4
GLA_DK = 64
GLA_DV = 128
GLA_KEY_WIDTH = GLA_HEADS * GLA_DK
GLA_VAL_WIDTH = GLA_HEADS * GLA_DV
GLA_GATE_RANK = 16
GLA_GATE_NORMALIZER = 16.0
GLA_CHUNK = 64
N_GROUPS = 4
EXPERTS_PER_GROUP = 8
N_EXPERTS = N_GROUPS * EXPERTS_PER_GROUP
RMS_EPS = 1e-6

LANES = 128
SUBLANES = 8
MXU_WIDTH = 256
VMEM_LIMIT_BYTES = 56 * 1024 * 1024

MASK_VALUE = -1e30

IN_PROJ_ROWS = 512
NA_ROW_BLOCK = 8
NA_ROW_UNROLL = 8
GLA_STEP_CHUNKS = 4
OUT_ROWS = 512
EXPERT_ROWS = 256
EXPERT_TILE_BUFFERS = 4
DISPATCH_ROWS = 1024
COMBINE_ROWS = DISPATCH_ROWS
COMBINE_STEP_ROWS = 512


def _params(*sem):
    return pltpu.CompilerParams(dimension_semantics=sem, vmem_limit_bytes=VMEM_LIMIT_BYTES)


def _rms(x, g):
    return x * lax.rsqrt(jnp.mean(x * x, axis=-1, keepdims=True) + RMS_EPS) * g


def _in_proj_rows(x, g_ref, w_na_ref, w_qk_ref, w_v_ref, w_r_ref, w_lr_ref,
                  w_g2_ref, b_g_ref, na_ref, qk_ref, v_ref, r_ref, gate_ref, w_gate):
    @pl.when(pl.program_id(0) == 0)
    def _():
        w_gate[...] = jnp.dot(w_lr_ref[...], w_g2_ref[...], preferred_element_type=F32,
                              precision=lax.Precision.HIGHEST).astype(BF16)

    h = _rms(x, g_ref[...]).astype(BF16)
    na_ref[...] = jnp.dot(h, w_na_ref[...], preferred_element_type=F32).astype(BF16)
    qk_ref[...] = jnp.dot(h, w_qk_ref[...], preferred_element_type=F32)
    v_ref[...] = jnp.dot(h, w_v_ref[...], preferred_element_type=F32).astype(BF16)
    r_ref[...] = jnp.dot(h, w_r_ref[...], preferred_element_type=F32)
    z = jnp.dot(h, w_gate[...], preferred_element_type=F32) + b_g_ref[...]
    log_sig = jnp.minimum(z, 0.0) - jnp.log(1.0 + jnp.exp(-jnp.abs(z)))
    gate_ref[...] = log_sig * (1.0 / GLA_GATE_NORMALIZER)


def _in_proj_kernel(x_ref, *refs):
    _in_proj_rows(x_ref[...], *refs)


def _in_proj_outputs(n, rows):
    row_spec = lambda width: pl.BlockSpec((rows, width), lambda i, *_: (i, 0))
    widths = (3 * NA_WIDTH, 2 * GLA_KEY_WIDTH, GLA_VAL_WIDTH, GLA_VAL_WIDTH, 2 * GLA_KEY_WIDTH)
    dtypes = (BF16, F32, BF16, F32, F32)
    return ([row_spec(w) for w in widths],
            [jax.ShapeDtypeStruct((n, w), t) for w, t in zip(widths, dtypes)])


def _layer_spec(a, layer):
    return pl.BlockSpec((None,) + a.shape[1:], lambda *_: (layer,) + (0,) * (a.ndim - 1))


def _in_proj(x, layer, g, w_na, w_qk, w_v, w_r, w_lr, w_g2, b_g):
    n, d = x.shape
    rows = IN_PROJ_ROWS
    params = (g, w_na, w_qk, w_v, w_r, w_lr, w_g2, b_g)
    out_specs, out_shape = _in_proj_outputs(n, rows)
    return pl.pallas_call(
        _in_proj_kernel,
        grid=(n // rows,),
        in_specs=[pl.BlockSpec((rows, d), lambda i: (i, 0))] + [_layer_spec(a, layer) for a in params],
        out_specs=out_specs,
        out_shape=out_shape,
        scratch_shapes=[pltpu.VMEM((d, 2 * GLA_KEY_WIDTH), BF16)],
        compiler_params=_params("arbitrary"),
        name="in_proj",
    )(x, *params)


def _na_bias_table(rpb, rows):
    kh = min(WIN_H_MAX, rows)
    w = jnp.arange(GRID_W)[:, None, None]
    x = jnp.arange(GRID_W)[None, :, None]
    cs = jnp.clip(w - WIN_W // 2, 0, GRID_W - WIN_W)
    valid = (x >= cs) & (x < cs + WIN_W)
    col_sel = (valid & (jnp.arange(2 * WIN_W - 1)[None, None, :] == x - w + (WIN_W - 1))).astype(F32)
    by_col = jnp.einsum("hrc,wxc->hwrx", rpb.astype(F32), col_sel, precision=lax.Precision.HIGHEST)
    by_col = jnp.where(valid[None, :, None, :, 0], by_col, MASK_VALUE)
    by_col = by_col.reshape(NA_HEADS * GRID_W, (2 * WIN_H_MAX - 1) * GRID_W)
    bias = jnp.stack([by_col[:, (WIN_H_MAX - 1 - c) * GRID_W:(WIN_H_MAX - 1 - c + kh) * GRID_W]
                      for c in range(kh)])
    return bias.reshape(kh, NA_HEADS // 2, 2 * GRID_W, kh * GRID_W)


def _na_window_start(j, rows, kh):
    return jnp.clip(j * NA_ROW_BLOCK - kh // 2, 0, rows - (NA_ROW_BLOCK + kh))


def _na_kernel(q_ref, k_ref, v_ref, tbl_ref, o_ref, *, rows, kh):
    rb = NA_ROW_BLOCK
    j = pl.program_id(1)
    fetched = _na_window_start(j, rows, kh)
    lane = lax.broadcasted_iota(jnp.int32, (GRID_W, LANES), 1)
    first = lane < NA_HEAD_DIM

    def row_body(lr, carry):
        r = j * rb + lr
        start = jnp.clip(r - kh // 2, 0, rows - kh)
        cls = r - start
        q_row = q_ref[lr]
        k_win = k_ref[0, pl.ds(start - fetched, kh)].reshape(kh * GRID_W, NA_WIDTH)
        v_win = v_ref[0, pl.ds(start - fetched, kh)].reshape(kh * GRID_W, NA_WIDTH)
        pairs = [slice(p * LANES, (p + 1) * LANES) for p in range(NA_HEADS // 2)]
        scores = []
        for sl in pairs:
            q_pair = q_row[:, sl]
            zero = jnp.zeros_like(q_pair)
            q_bd = jnp.concatenate([jnp.where(first, q_pair, zero), jnp.where(first, zero, q_pair)], axis=0)
            scores.append(lax.dot_general(q_bd, k_win[:, sl], (((1,), (1,)), ((), ())),
                                          preferred_element_type=F32))
        probs, denoms = [], []
        for p, s in enumerate(scores):
            s = s + tbl_ref[cls, p]
            e = jnp.exp(s - jnp.max(s, axis=-1, keepdims=True))
            denoms.append(jnp.sum(e, axis=-1, keepdims=True))
            probs.append(e.astype(BF16))
        outs = []
        for sl, e, denom in zip(pairs, probs, denoms):
            o = jnp.dot(e, v_win[:, sl], preferred_element_type=F32) / denom
            outs.append(jnp.where(first, o[:GRID_W], o[GRID_W:]))
        o_ref[lr] = jnp.concatenate(outs, axis=-1).astype(o_ref.dtype)
        return carry

    lax.fori_loop(0, rb, row_body, 0, unroll=NA_ROW_UNROLL)


def _na(na_qkv, tbl, layer, batch, rows):
    kh = min(WIN_H_MAX, rows)
    rb = NA_ROW_BLOCK
    nblk = rows // rb
    x4 = na_qkv.reshape(batch, rows, GRID_W, 3 * NA_WIDTH)
    assert rows >= rb + kh
    blk = (None, rb, GRID_W, NA_WIDTH)
    window = (pl.Element(1), pl.Element(rb + kh), pl.Element(GRID_W), pl.Element(NA_WIDTH))
    specs = [pl.BlockSpec(blk, lambda b, j: (b, j, 0, 0))]
    for col in (1, 2):
        specs.append(pl.BlockSpec(window, lambda b, j, col=col: (b, _na_window_start(j, rows, kh), 0,
                                                                  col * NA_WIDTH)))
    specs.append(_layer_spec(tbl, layer))
    out = pl.pallas_call(
        functools.partial(_na_kernel, rows=rows, kh=kh),
        grid=(batch, nblk),
        in_specs=specs,
        out_specs=pl.BlockSpec(blk, lambda b, j: (b, j, 0, 0)),
        out_shape=jax.ShapeDtypeStruct((batch, rows, GRID_W, NA_WIDTH), BF16),
        compiler_params=_params("arbitrary", "arbitrary"),
        name="na",
    )(x4, x4, x4, tbl)
    return out.reshape(batch * rows * GRID_W, NA_WIDTH)


def _block_diag_mask(row_block, col_block, nblocks):
    shape = (row_block * nblocks, col_block * nblocks)
    r = lax.shift_right_logical(lax.broadcasted_iota(jnp.int32, shape, 0), row_block.bit_length() - 1)
    c = lax.shift_right_logical(lax.broadcasted_iota(jnp.int32, shape, 1), col_block.bit_length() - 1)
    return r == c


def _split_bf16x3(x):
    hi = x.astype(BF16)
    rest = x - hi.astype(F32)
    mid = rest.astype(BF16)
    lo = (rest - mid.astype(F32)).astype(BF16)
    return hi, mid, lo


def _gla_direction(qk_ref, v_ref, g_ref, *, backward):
    c = GLA_CHUNK
    nc = GLA_STEP_CHUNKS
    kw = GLA_KEY_WIDTH
    order = list(reversed(range(nc))) if backward else list(range(nc))
    chunk = lambda a, n: a[n * c:(n + 1) * c]

    def stage_cumsum():
        step = nc * c
        ti = lax.broadcasted_iota(jnp.int32, (step, step), 0)
        tj = lax.broadcasted_iota(jnp.int32, (step, step), 1)
        same_chunk = lax.shift_right_logical(ti, c.bit_length() - 1) == lax.shift_right_logical(tj, c.bit_length() - 1)
        tri = (tj >= ti) if backward else (tj <= ti)
        cum = jnp.where(same_chunk & tri, 1.0, 0.0).astype(BF16)
        return sum(jnp.dot(cum, piece, preferred_element_type=F32) for piece in _split_bf16x3(g_ref[...]))

    def stage_decays(b, idx):
        n = order[idx]
        ref_row, last_row = (c // 2, 0) if backward else (c // 2 - 1, c - 1)
        b_n = chunk(b, n)
        b_ref = jnp.broadcast_to(b_n[ref_row:ref_row + 1, :], (c, kw))
        b_last = jnp.broadcast_to(b_n[last_row:last_row + 1, :], (c, kw))
        q = qk_ref[n * c:(n + 1) * c, :kw]
        k = qk_ref[n * c:(n + 1) * c, kw:]
        q_rel = (q * jnp.exp(b_n - b_ref)).astype(BF16)
        k_rel = (k * jnp.exp(b_ref - b_n)).astype(BF16)
        k_dec = k * jnp.exp(b_last - b_n)
        q_dec = (q * jnp.exp(b_n)).astype(BF16)
        decay = jnp.exp(b_last)
        return q_rel, k_rel, k_dec, q_dec, decay

    def stage_scores(q_rel, k_rel):
        kk_mask = _block_diag_mask(c, GLA_DK, GLA_HEADS)
        k_bd = jnp.where(kk_mask, jnp.concatenate([k_rel] * GLA_HEADS, axis=0), jnp.zeros((), BF16))
        return lax.dot_general(q_rel, k_bd, (((1,), (1,)), ((), ())), preferred_element_type=F32)

    def stage_intra(scores, k_dec, decay, idx):
        n = order[idx]
        si = lax.broadcasted_iota(jnp.int32, (c, c * GLA_HEADS), 0)
        sj = lax.broadcasted_iota(jnp.int32, (c, c * GLA_HEADS), 1) & (c - 1)
        keep = (sj > si) if backward else (sj <= si)
        kv_mask = _block_diag_mask(c, GLA_DV, GLA_HEADS)
        vk_mask = _block_diag_mask(GLA_DK, GLA_DV, 2)
        v = v_ref[n * c:(n + 1) * c, :]
        p = jnp.where(keep, scores, 0.0).astype(BF16)
        v_bd = jnp.where(kv_mask, jnp.concatenate([v] * GLA_HEADS, axis=0), jnp.zeros((), BF16))
        o_intra = jnp.dot(p, v_bd, preferred_element_type=F32)
        k_dec_t = k_dec.T.astype(BF16)
        upd = []
        for hp in range(GLA_HEADS // 2):
            kv = jnp.dot(k_dec_t[hp * 2 * GLA_DK:(hp + 1) * 2 * GLA_DK],
                         v[:, hp * 2 * GLA_DV:(hp + 1) * 2 * GLA_DV], preferred_element_type=F32)
            upd.append(jnp.where(vk_mask, kv, 0.0))
        return o_intra, upd, decay.T[:, :1]

    def scan_step(state, idx, q_dec, o_intra, upd, decay_col, out_ref):
        n = order[idx]
        pw = 2 * GLA_DK
        o_inter = jnp.concatenate(
            [jnp.dot(q_dec[:, hp * pw:(hp + 1) * pw], s.astype(BF16), preferred_element_type=F32)
             for hp, s in enumerate(state)], axis=1)
        out_ref[n * c:(n + 1) * c, :] = o_intra + o_inter
        return [s * decay_col[hp * pw:(hp + 1) * pw] + u for hp, (s, u) in enumerate(zip(state, upd))]

    return stage_cumsum, stage_decays, stage_scores, stage_intra, scan_step


def _gla_kernel(qk_f_ref, v_f_ref, g_f_ref, qk_b_ref, v_b_ref, g_b_ref, o_f_ref, o_b_ref, st):
    @pl.when(pl.program_id(0) == 0)
    def _():
        st[...] = jnp.zeros_like(st)

    batch = qk_f_ref.shape[0]
    streams, outs = [], []
    for bi in range(batch):
        streams.append(_gla_direction(qk_f_ref.at[bi], v_f_ref.at[bi], g_f_ref.at[bi], backward=False))
        streams.append(_gla_direction(qk_b_ref.at[bi], v_b_ref.at[bi], g_b_ref.at[bi], backward=True))
        outs += [o_f_ref.at[bi], o_b_ref.at[bi]]
    b = [s[0]() for s in streams]
    pairs = range(GLA_HEADS // 2)
    states = [[st[si, hp] for hp in pairs] for si in range(len(streams))]
    for idx in range(GLA_STEP_CHUNKS):
        dec = [s[1](x, idx) for s, x in zip(streams, b)]
        scores = [s[2](x[0], x[1]) for s, x in zip(streams, dec)]
        intra = [s[3](sc, x[2], x[4], idx) for s, sc, x in zip(streams, scores, dec)]
        states = [s[4](state, idx, x[3], *y, out_ref)
                  for s, state, x, y, out_ref in zip(streams, states, dec, intra, outs)]
    for si, state in enumerate(states):
        for hp in pairs:
            st[si, hp] = state[hp]


def _gla(qk, v, gates, batch, seq):
    step = GLA_STEP_CHUNKS * GLA_CHUNK
    nblk = seq // step
    qk3 = qk.reshape(batch, seq, 2 * GLA_KEY_WIDTH)
    v3 = v.reshape(batch, seq, GLA_VAL_WIDTH)
    g3 = gates.reshape(batch, seq, 2 * GLA_KEY_WIDTH)
    fwd = lambda n: (0, n, 0)
    bwd = lambda n: (0, nblk - 1 - n, 0)
    bwd_gate = lambda n: (0, nblk - 1 - n, 1)
    o_f, o_b = pl.pallas_call(
        _gla_kernel,
        grid=(nblk,),
        in_specs=[pl.BlockSpec((batch, step, 2 * GLA_KEY_WIDTH), fwd),
                  pl.BlockSpec((batch, step, GLA_VAL_WIDTH), fwd),
                  pl.BlockSpec((batch, step, GLA_KEY_WIDTH), fwd),
                  pl.BlockSpec((batch, step, 2 * GLA_KEY_WIDTH), bwd),
                  pl.BlockSpec((batch, step, GLA_VAL_WIDTH), bwd),
                  pl.BlockSpec((batch, step, GLA_KEY_WIDTH), bwd_gate)],
        out_specs=[pl.BlockSpec((batch, step, GLA_VAL_WIDTH), fwd),
                   pl.BlockSpec((batch, step, GLA_VAL_WIDTH), bwd)],
        out_shape=[jax.ShapeDtypeStruct((batch, seq, GLA_VAL_WIDTH), F32)] * 2,
        scratch_shapes=[pltpu.VMEM((2 * batch, GLA_HEADS // 2, 2 * GLA_DK, 2 * GLA_DV), F32)],
        compiler_params=_params("arbitrary"),
        name="gla",
    )(qk3, v3, g3, qk3, v3, g3)
    return o_f.reshape(batch * seq, GLA_VAL_WIDTH), o_b.reshape(batch * seq, GLA_VAL_WIDTH)


def _store_slabs(ref, x):
    rows = x.shape[0]
    for s in range(SUBLANES):
        ref[pl.ds(s, rows, stride=SUBLANES), :] = x[:, s * LANES:(s + 1) * LANES]


def _load_slabs(ref, rows):
    return jnp.concatenate([ref[pl.ds(s, rows, stride=SUBLANES), :] for s in range(SUBLANES)], axis=1)


def _slab_copy(src_hbm, dst, sem, src_row, k):
    dst_row = k * SUBLANES if isinstance(k, int) else pl.multiple_of(k * SUBLANES, SUBLANES)
    return pltpu.make_async_copy(src_hbm.at[pl.ds(pl.multiple_of(src_row, SUBLANES), SUBLANES), :],
                                 dst.at[pl.ds(dst_row, SUBLANES), :], sem)


def _wait_slabs(src_hbm, dst, sem):
    pltpu.make_async_copy(src_hbm.at[pl.ds(0, dst.shape[0]), :], dst, sem).wait()


RT_E1, RT_E2, RT_RANK1, RT_RANK2, RT_W1, RT_W2 = range(6)
GROUP_ROW0 = N_EXPERTS
ROUTE_ROWS = 64


def _out_route_kernel(na_ref, of_ref, ob_ref, r_ref, x_ref, w_na_ref, w_gla_ref, gn_ref, fn_ref,
                      w_rt_ref, b_rt_ref, x1_ref, h2_ref, rt_ref, cnt_ref, carry):
    @pl.when(pl.program_id(0) == 0)
    def _():
        carry[...] = jnp.zeros_like(carry)

    rows = x_ref.shape[0]
    o = of_ref[...] + ob_ref[...]
    r = r_ref[...]
    parts = []
    for h in range(GLA_HEADS):
        sl = slice(h * GLA_DV, (h + 1) * GLA_DV)
        parts.append(_rms(o[:, sl], gn_ref[...]) * (r[:, sl] * jax.nn.sigmoid(r[:, sl])))
    y_gla = jnp.concatenate(parts, axis=-1).astype(BF16)
    x1 = (x_ref[...] + jnp.dot(na_ref[...], w_na_ref[...], preferred_element_type=F32)
          + jnp.dot(y_gla, w_gla_ref[...], preferred_element_type=F32))
    x1_ref[...] = x1
    h2 = _rms(x1, fn_ref[...])
    _store_slabs(h2_ref, h2)

    h_hi = h2.astype(BF16)
    h_lo = (h2 - h_hi.astype(F32)).astype(BF16)
    logits = lax.dot_general(w_rt_ref[...], jnp.concatenate([h_hi, h_lo, h_hi], axis=1),
                             (((1,), (1,)), ((), ())), preferred_element_type=F32)
    logits = logits[:ROUTE_ROWS] + b_rt_ref[...]
    row_i = lax.broadcasted_iota(jnp.int32, (ROUTE_ROWS, rows), 0)
    row = row_i.astype(F32)
    row_grp = lax.shift_right_logical(row_i, EXPERTS_PER_GROUP.bit_length() - 1).astype(F32)
    neg = jnp.float32(-jnp.inf)
    no_row = jnp.float32(ROUTE_ROWS)
    is_grp = (row_i >= GROUP_ROW0) & (row_i < GROUP_ROW0 + N_GROUPS)
    g_logit = jnp.where(is_grp, logits, neg)
    g_max = jnp.max(g_logit, axis=0, keepdims=True)
    g_sel = jnp.min(jnp.where(is_grp & (g_logit == g_max), row, no_row), axis=0, keepdims=True) - GROUP_ROW0
    grp_w = 1.0 / jnp.sum(jnp.where(is_grp, jnp.exp(g_logit - g_max), 0.0), axis=0, keepdims=True)
    in_grp = (row_i < N_EXPERTS) & (row_grp == g_sel)
    e_logit = jnp.where(in_grp, logits, neg)
    v1 = jnp.max(e_logit, axis=0, keepdims=True)
    i1 = jnp.min(jnp.where(in_grp & (e_logit == v1), row, no_row), axis=0, keepdims=True)
    rest = in_grp & (row != i1)
    e_logit2 = jnp.where(rest, logits, neg)
    v2 = jnp.max(e_logit2, axis=0, keepdims=True)
    i2 = jnp.min(jnp.where(rest & (e_logit2 == v2), row, no_row), axis=0, keepdims=True)
    t = jnp.exp(v2 - v1)
    w1 = grp_w / (1.0 + t)
    w2 = grp_w * t / (1.0 + t)

    sel1 = row == i1
    sel2 = row == i2
    onehot = jnp.where(sel1 | sel2, 1.0, 0.0)
    ti = lax.broadcasted_iota(jnp.int32, (rows, rows), 0)
    tj = lax.broadcasted_iota(jnp.int32, (rows, rows), 1)
    earlier = jnp.where(ti < tj, 1.0, 0.0).astype(BF16)
    ranks = jnp.dot(onehot.astype(BF16), earlier, preferred_element_type=F32) + carry[:, :1]
    rank1 = jnp.sum(jnp.where(sel1, ranks, 0.0), axis=0, keepdims=True)
    rank2 = jnp.sum(jnp.where(sel2, ranks, 0.0), axis=0, keepdims=True)
    new_carry = carry[...] + jnp.sum(onehot, axis=1, keepdims=True)
    carry[...] = new_carry
    cnt_ref[...] = new_carry

    fields = {RT_E1: i1, RT_E2: i2, RT_RANK1: rank1, RT_RANK2: rank2, RT_W1: w1, RT_W2: w2}
    zero = jnp.zeros_like(w1)
    rt_ref[...] = jnp.concatenate([fields.get(f, zero) for f in range(SUBLANES)], axis=0)


def _out_route(y_na, o_f, o_b, r, x, layer, w_na, w_gla, gn, fn, w_rt, b_rt):
    n, d = x.shape
    rows = OUT_ROWS
    row_spec = lambda width: pl.BlockSpec((rows, width), lambda i: (i, 0))
    full = lambda a: _layer_spec(a, layer)
    return pl.pallas_call(
        _out_route_kernel,
        grid=(n // rows,),
        in_specs=[row_spec(NA_WIDTH), row_spec(GLA_VAL_WIDTH), row_spec(GLA_VAL_WIDTH),
                  row_spec(GLA_VAL_WIDTH), row_spec(d), full(w_na), full(w_gla), full(gn), full(fn),
                  full(w_rt), full(b_rt)],
        out_specs=[row_spec(d), pl.BlockSpec((rows * SUBLANES, LANES), lambda i: (i, 0)),
                   pl.BlockSpec((SUBLANES, rows), lambda i: (0, i)),
                   pl.BlockSpec((ROUTE_ROWS, LANES), lambda i: (0, 0))],
        out_shape=[jax.ShapeDtypeStruct((n, d), F32), jax.ShapeDtypeStruct((n * SUBLANES, LANES), F32),
                   jax.ShapeDtypeStruct((SUBLANES, n), F32), jax.ShapeDtypeStruct((ROUTE_ROWS, LANES), F32)],
        scratch_shapes=[pltpu.VMEM((ROUTE_ROWS, LANES), F32)],
        compiler_params=_params("arbitrary"),
        name="out_route",
    )(y_na, o_f, o_b, r, x, w_na, w_gla, gn, fn, w_rt, b_rt)


def _dispatch_kernel(pad_tile_ref, n_tiles_ref, rt_ref, offs_ref, h_hbm, pos_ref, rec_ref, xs_hbm,
                     zeros, hbuf, pos_vmem, pos_smem, sem_zero, sem_pos, sem_in, sem):
    i = pl.program_id(0)
    n_steps = pl.num_programs(0)
    db = DISPATCH_ROWS
    tile = EXPERT_ROWS * SUBLANES
    max_tiles = xs_hbm.shape[0] // tile
    buf = lax.rem(i, 2)

    def block_copy(step, b):
        rows = pl.ds(pl.multiple_of(step * db * SUBLANES, db * SUBLANES), db * SUBLANES)
        return pltpu.make_async_copy(h_hbm.at[rows, :], hbuf.at[b], sem_in.at[b])

    def wait_row_copies(b):
        for _ in range(2):
            pltpu.make_async_copy(hbuf.at[b], xs_hbm.at[pl.ds(0, db * SUBLANES), :], sem).wait()

    def zero_copy(t):
        return pltpu.make_async_copy(zeros, xs_hbm.at[pl.ds(pl.multiple_of(t * tile, tile), tile), :], sem_zero)

    def for_each_zero_tile(fn):
        def padded(e, carry):
            @pl.when(pad_tile_ref[e] >= 0)
            def _():
                fn(zero_copy(pad_tile_ref[e]))
            return carry
        lax.fori_loop(0, N_EXPERTS, padded, 0)

        def tail(t, carry):
            fn(zero_copy(t))
            return carry
        lax.fori_loop(n_tiles_ref[0], max_tiles, tail, 0)

    @pl.when(i == 0)
    def _():
        block_copy(0, 0).start()
        zeros[...] = jnp.zeros_like(zeros)
        for_each_zero_tile(lambda cp: cp.start())

    rt_t = rt_ref[...]
    rec_ref[...] = jnp.concatenate([rt_t, jnp.zeros((LANES - SUBLANES, db), F32)], axis=0).T
    expert = lax.broadcasted_iota(jnp.int32, (N_EXPERTS, db), 0).astype(F32)
    pos = []
    for e_lane, rank_lane in ((RT_E1, RT_RANK1), (RT_E2, RT_RANK2)):
        seg = jnp.sum(jnp.where(expert == rt_t[e_lane:e_lane + 1, :], offs_ref[...], 0.0), axis=0, keepdims=True)
        pos.append((seg + rt_t[rank_lane:rank_lane + 1, :]) * SUBLANES)
    pos = jnp.concatenate(pos, axis=0).astype(jnp.int32)
    pos_ref[...] = pos
    pos_vmem[...] = jnp.concatenate([pos, jnp.zeros((SUBLANES - 2, db), jnp.int32)], axis=0)
    to_smem = pltpu.make_async_copy(pos_vmem, pos_smem, sem_pos)
    to_smem.start()

    @pl.when(i == 0)
    def _():
        for_each_zero_tile(lambda cp: cp.wait())

    @pl.when(i > 0)
    def _():
        wait_row_copies(1 - buf)

    to_smem.wait()
    block_copy(i, buf).wait()

    @pl.when(i + 1 < n_steps)
    def _():
        block_copy(i + 1, 1 - buf).start()

    h_blk = hbuf.at[buf]
    for k in range(db):
        src = h_blk.at[pl.ds(k * SUBLANES, SUBLANES), :]
        for slot in range(2):
            dst_row = pl.multiple_of(pos_smem[slot, k], SUBLANES)
            pltpu.make_async_copy(src, xs_hbm.at[pl.ds(dst_row, SUBLANES), :], sem).start(priority=slot)

    @pl.when(i == n_steps - 1)
    def _():
        wait_row_copies(buf)


def _dispatch(pad_tile, n_tiles, rt_t, offs_b, h2_slabs, sorted_rows):
    n = rt_t.shape[1]
    db = DISPATCH_ROWS
    grid_spec = pltpu.PrefetchScalarGridSpec(
        num_scalar_prefetch=2,
        grid=(n // db,),
        in_specs=[pl.BlockSpec((SUBLANES, db), lambda i, pt, nt: (0, i)),
                  pl.BlockSpec(offs_b.shape, lambda i, pt, nt: (0, 0)),
                  pl.BlockSpec(memory_space=pl.ANY)],
        out_specs=[pl.BlockSpec((None, 2, db), lambda i, pt, nt: (i, 0, 0)),
                   pl.BlockSpec((db, LANES), lambda i, pt, nt: (i, 0)),
                   pl.BlockSpec(memory_space=pl.ANY)],
        scratch_shapes=[pltpu.VMEM((EXPERT_ROWS * SUBLANES, LANES), F32),
                        pltpu.VMEM((2, db * SUBLANES, LANES), F32),
                        pltpu.VMEM((SUBLANES, db), jnp.int32), pltpu.SMEM((SUBLANES, db), jnp.int32),
                        pltpu.SemaphoreType.DMA(()), pltpu.SemaphoreType.DMA(()),
                        pltpu.SemaphoreType.DMA((2,)), pltpu.SemaphoreType.DMA(())],
    )
    return pl.pallas_call(
        _dispatch_kernel,
        grid_spec=grid_spec,
        out_shape=[jax.ShapeDtypeStruct((n // db, 2, db), jnp.int32),
                   jax.ShapeDtypeStruct((n, LANES), F32),
                   jax.ShapeDtypeStruct((sorted_rows * SUBLANES, LANES), F32)],
        compiler_params=_params("arbitrary"),
        name="dispatch",
    )(pad_tile, n_tiles, rt_t, offs_b, h2_slabs)


TILE_DMA_PRIORITY = 1


def _expert_tile(x_ref, y_ref, wg_s, wu_s, wd_s):
    rows = EXPERT_ROWS
    x = _load_slabs(x_ref, rows).astype(BF16)
    hid = []
    for c in range(0, wg_s.shape[1], MXU_WIDTH):
        gate = jnp.dot(x, wg_s[:, c:c + MXU_WIDTH], preferred_element_type=F32)
        up = jnp.dot(x, wu_s[:, c:c + MXU_WIDTH], preferred_element_type=F32)
        hid.append((gate * jax.nn.sigmoid(gate) * up).astype(BF16))
    hid = jnp.concatenate(hid, axis=1)
    for c in range(0, wd_s.shape[1], MXU_WIDTH):
        y = jnp.dot(hid, wd_s[:, c:c + MXU_WIDTH], preferred_element_type=F32)
        for s in range(c // LANES, (c + MXU_WIDTH) // LANES):
            y_ref[pl.ds(s, rows, stride=SUBLANES), :] = y[:, s * LANES - c:(s + 1) * LANES - c]


def _experts_kernel(tile_start_ref, tiles_ref, x_hbm, wg_hbm, wu_hbm, wd_hbm, y_hbm,
                    xbuf, ybuf, wg_f, wu_f, wd_f, wg_s, wu_s, wd_s, sem_x, sem_y, sem_w, *, layer):
    nbuf = EXPERT_TILE_BUFFERS
    e = pl.program_id(0)
    last_step = e == pl.num_programs(0) - 1
    tile = EXPERT_ROWS * SUBLANES
    n = tiles_ref[e]
    base = tile_start_ref[e]
    n_tiles = tile_start_ref[N_EXPERTS - 1] + tiles_ref[N_EXPERTS - 1]
    tile_rows = lambda g: pl.ds(pl.multiple_of(g * tile, tile), tile)

    def x_copy(g, slot):
        return pltpu.make_async_copy(x_hbm.at[tile_rows(g), :], xbuf.at[slot], sem_x.at[slot])

    def y_copy(g, slot):
        return pltpu.make_async_copy(ybuf.at[slot], y_hbm.at[tile_rows(g), :], sem_y.at[slot])

    def w_copies(expert, slot):
        return [pltpu.make_async_copy(w_hbm.at[layer, expert], stage.at[slot], sem_w.at[slot])
                for w_hbm, stage in ((wg_hbm, wg_f), (wu_hbm, wu_f), (wd_hbm, wd_f))]

    w_slot = lax.rem(e, 2)

    @pl.when(e == 0)
    def _():
        for g in range(nbuf - 1):
            @pl.when(g < n_tiles)
            def _():
                x_copy(g, g).start(priority=TILE_DMA_PRIORITY)
        for cp in w_copies(0, 0):
            cp.start()

    for cp in w_copies(e, w_slot):
        cp.wait()

    @pl.when(jnp.logical_not(last_step))
    def _():
        for cp in w_copies(e + 1, 1 - w_slot):
            cp.start()

    @pl.when(n > 0)
    def _():
        wg_s[...] = wg_f[w_slot].astype(BF16)
        wu_s[...] = wu_f[w_slot].astype(BF16)
        wd_s[...] = wd_f[w_slot].astype(BF16)

        def tile_body(g, carry):
            slot = lax.rem(g, nbuf)
            x_copy(g, slot).wait()

            @pl.when(g + nbuf - 1 < n_tiles)
            def _():
                x_copy(g + nbuf - 1, lax.rem(g + nbuf - 1, nbuf)).start(priority=TILE_DMA_PRIORITY)

            @pl.when(g >= nbuf)
            def _():
                y_copy(g - nbuf, slot).wait()

            _expert_tile(xbuf.at[slot], ybuf.at[slot], wg_s, wu_s, wd_s)
            y_copy(g, slot).start(priority=TILE_DMA_PRIORITY)
            return carry
        lax.fori_loop(base, base + n, tile_body, 0)

    @pl.when(last_step)
    def _():
        for back in range(nbuf, 0, -1):
            @pl.when(n_tiles >= back)
            def _():
                y_copy(n_tiles - back, lax.rem(n_tiles - back, nbuf)).wait()

        first_unused = n_tiles
        max_tiles = y_hbm.shape[0] // tile
        ybuf[0] = jnp.zeros(ybuf.shape[1:], ybuf.dtype)
        zero_copy = lambda t: pltpu.make_async_copy(
            ybuf.at[0], y_hbm.at[pl.ds(pl.multiple_of(t * tile, tile), tile), :], sem_y.at[0])

        def start(t, carry):
            zero_copy(t).start()
            return carry

        def wait(t, carry):
            zero_copy(t).wait()
            return carry
        lax.fori_loop(first_unused, max_tiles, start, 0)
        lax.fori_loop(first_unused, max_tiles, wait, 0)


def _experts(tile_start, tiles, xs, wg, wu, wd, layer):
    rows = EXPERT_ROWS
    _, n_experts, d, dff = wg.shape
    assert d == SUBLANES * LANES and n_experts == N_EXPERTS
    nbuf = EXPERT_TILE_BUFFERS
    any_spec = pl.BlockSpec(memory_space=pl.ANY)
    grid_spec = pltpu.PrefetchScalarGridSpec(
        num_scalar_prefetch=2,
        grid=(n_experts,),
        in_specs=[any_spec, any_spec, any_spec, any_spec],
        out_specs=any_spec,
        scratch_shapes=[pltpu.VMEM((nbuf, rows * SUBLANES, LANES), F32),
                        pltpu.VMEM((nbuf, rows * SUBLANES, LANES), F32),
                        pltpu.VMEM((2, d, dff), F32), pltpu.VMEM((2, d, dff), F32), pltpu.VMEM((2, dff, d), F32),
                        pltpu.VMEM((d, dff), BF16), pltpu.VMEM((d, dff), BF16), pltpu.VMEM((dff, d), BF16),
                        pltpu.SemaphoreType.DMA((nbuf,)), pltpu.SemaphoreType.DMA((nbuf,)),
                        pltpu.SemaphoreType.DMA((2,))],
    )
    return pl.pallas_call(
        functools.partial(_experts_kernel, layer=layer),
        grid_spec=grid_spec,
        out_shape=jax.ShapeDtypeStruct(xs.shape, F32),
        compiler_params=_params("arbitrary"),
        name="experts",
    )(tile_start, tiles, xs, wg, wu, wd)


def _start_slab_gather(src_hbm, dst, sem, row_ref, base, count, priority):
    for k in range(count):
        _slab_copy(src_hbm, dst, sem, row_ref[base + k], k).start(priority=priority)


def _combined_rows(pos_ref, ys_hbm, x1_ref, rt_ref, buf, sem):
    i = pl.program_id(0)
    n_steps = pl.num_programs(0)
    rows = x1_ref.shape[0]
    steps_per_block = COMBINE_ROWS // rows
    slot = i % 2

    def start(step, s):
        first = lax.div(step, steps_per_block) * (2 * COMBINE_ROWS) + lax.rem(step, steps_per_block) * rows
        for j in range(2):
            _start_slab_gather(ys_hbm, buf.at[s, j], sem.at[s], pos_ref, first + j * COMBINE_ROWS, rows, priority=j)

    @pl.when(i == 0)
    def _():
        start(0, 0)

    @pl.when(i + 1 < n_steps)
    def _():
        start(i + 1, 1 - slot)

    _wait_slabs(ys_hbm, buf.at[slot, 0], sem.at[slot])
    _wait_slabs(ys_hbm, buf.at[slot, 1], sem.at[slot])
    rt = rt_ref[...]
    lane = lax.broadcasted_iota(jnp.int32, rt.shape, 1)
    w1 = jnp.sum(jnp.where(lane == RT_W1, rt, 0.0), axis=-1, keepdims=True)
    w2 = jnp.sum(jnp.where(lane == RT_W2, rt, 0.0), axis=-1, keepdims=True)
    y = w1 * _load_slabs(buf.at[slot, 0], rows) + w2 * _load_slabs(buf.at[slot, 1], rows)
    return x1_ref[...] + y


def _combine_scratch(rows):
    return [pltpu.VMEM((2, 2, rows * SUBLANES, LANES), F32), pltpu.SemaphoreType.DMA((2,))]


def _combine_final_kernel(pos_ref, ys_hbm, x1_ref, rt_ref, g_ref, o_ref, buf, sem):
    o_ref[...] = _rms(_combined_rows(pos_ref, ys_hbm, x1_ref, rt_ref, buf, sem), g_ref[...])


def _combine_final(pos, ys, x1, rt, g):
    n, d = x1.shape
    rows = COMBINE_STEP_ROWS
    grid_spec = pltpu.PrefetchScalarGridSpec(
        num_scalar_prefetch=1,
        grid=(n // rows,),
        in_specs=[pl.BlockSpec(memory_space=pl.ANY),
                  pl.BlockSpec((rows, d), lambda i, pos: (i, 0)),
                  pl.BlockSpec((rows, LANES), lambda i, pos: (i, 0)),
                  pl.BlockSpec(g.shape, lambda i, pos: (0, 0))],
        out_specs=pl.BlockSpec((rows, d), lambda i, pos: (i, 0)),
        scratch_shapes=_combine_scratch(rows),
    )
    return pl.pallas_call(
        _combine_final_kernel,
        grid_spec=grid_spec,
        out_shape=jax.ShapeDtypeStruct((n, d), F32),
        compiler_params=_params("arbitrary"),
        name="combine",
    )(pos, ys, x1, rt, g)


def _combine_in_proj_kernel(pos_ref, ys_hbm, x1_ref, rt_ref, *refs):
    param_refs, (x_ref, *proj_refs), (w_gate, buf, sem) = refs[:8], refs[8:14], refs[14:]
    x = _combined_rows(pos_ref, ys_hbm, x1_ref, rt_ref, buf, sem)
    x_ref[...] = x
    _in_proj_rows(x, *param_refs, *proj_refs, w_gate)


def _combine_in_proj(pos, ys, x1, rt, layer, g, w_na, w_qk, w_v, w_r, w_lr, w_g2, b_g):
    n, d = x1.shape
    rows = COMBINE_STEP_ROWS
    params = (g, w_na, w_qk, w_v, w_r, w_lr, w_g2, b_g)
    proj_specs, proj_shapes = _in_proj_outputs(n, rows)
    grid_spec = pltpu.PrefetchScalarGridSpec(
        num_scalar_prefetch=1,
        grid=(n // rows,),
        in_specs=[pl.BlockSpec(memory_space=pl.ANY),
                  pl.BlockSpec((rows, d), lambda i, pos: (i, 0)),
                  pl.BlockSpec((rows, LANES), lambda i, pos: (i, 0))] + [_layer_spec(a, layer) for a in params],
        out_specs=[pl.BlockSpec((rows, d), lambda i, pos: (i, 0))] + proj_specs,
        scratch_shapes=[pltpu.VMEM((d, 2 * GLA_KEY_WIDTH), BF16)] + _combine_scratch(rows),
    )
    x, *proj = pl.pallas_call(
        _combine_in_proj_kernel,
        grid_spec=grid_spec,
        out_shape=[jax.ShapeDtypeStruct((n, d), F32)] + proj_shapes,
        compiler_params=_params("arbitrary"),
        name="combine_in_proj",
    )(pos, ys, x1, rt, *params)
    return x, proj


def _dispatch_plan(counts, n):
    rows = EXPERT_ROWS
    max_tiles = (2 * n) // rows + N_EXPERTS
    cnt = counts[:N_EXPERTS, 0].astype(jnp.int32)
    tiles = (cnt + rows - 1) // rows
    tile_end = jnp.cumsum(tiles)
    tile_start = tile_end - tiles
    n_tiles = tile_end[-1:]
    pad_tile = jnp.where(cnt % rows != 0, tile_end - 1, -1).astype(jnp.int32)
    offs_b = jnp.broadcast_to((tile_start * rows).astype(F32)[:, None], (N_EXPERTS, DISPATCH_ROWS))
    return tile_start.astype(jnp.int32), tiles, n_tiles, pad_tile, offs_b, max_tiles * rows


def kernel(x, norm_mix_g, w_in, w_g2_f, b_g_f, w_g2_b, b_g_b, gla_norm_g, rpb, w_out, norm_ffn_g, w_grp, b_grp,
           w_exp, b_exp, w_gate, w_up, w_down, final_norm_g):
    batch, seq, d = x.shape
    n = batch * seq
    rows = seq // GRID_W
    depth = w_in.shape[0]
    xf = x.reshape(n, d)
    c_na = 3 * NA_WIDTH
    c_qk = c_na + 2 * GLA_KEY_WIDTH
    c_v = c_qk + GLA_VAL_WIDTH
    c_r = c_v + GLA_VAL_WIDTH

    q_scale = jnp.concatenate([jnp.full((NA_WIDTH,), NA_HEAD_DIM ** -0.5, F32), jnp.ones((2 * NA_WIDTH,), F32)])
    w_na = (w_in[:, :, :c_na] * q_scale).astype(BF16)
    qk_scale = jnp.concatenate([jnp.full((GLA_KEY_WIDTH,), GLA_DK ** -0.5, F32), jnp.ones((GLA_KEY_WIDTH,), F32)])
    w_qk = (w_in[:, :, c_na:c_qk] * qk_scale).astype(BF16)
    w_v = w_in[:, :, c_qk:c_v].astype(BF16)
    w_r = w_in[:, :, c_v:c_r].astype(BF16)
    w_lr = jnp.pad(w_in[:, :, c_r:], ((0, 0), (0, 0), (0, LANES - 2 * GLA_GATE_RANK)))
    zero_g2 = jnp.zeros_like(w_g2_f)
    w_g2 = jnp.concatenate([jnp.concatenate([w_g2_f, zero_g2], axis=2),
                            jnp.concatenate([zero_g2, w_g2_b], axis=2)], axis=1)
    w_g2 = jnp.pad(w_g2, ((0, 0), (0, LANES - 2 * GLA_GATE_RANK), (0, 0)))
    b_g = jnp.concatenate([b_g_f, b_g_b], axis=1)[:, None, :]
    bias_tbl = jax.vmap(functools.partial(_na_bias_table, rows=rows))(rpb)
    w_o = w_out.astype(BF16)
    w_o_na, w_o_gla = w_o[:, :NA_WIDTH], w_o[:, NA_WIDTH:]
    w_rt = jnp.pad(jnp.concatenate([w_exp, w_grp], axis=2), ((0, 0), (0, 0), (0, LANES - N_EXPERTS - N_GROUPS)))
    w_rt_hi = w_rt.astype(BF16)
    w_rt_lo = (w_rt - w_rt_hi.astype(F32)).astype(BF16)
    w_rt3 = jnp.concatenate([w_rt_hi, w_rt_hi, w_rt_lo], axis=1).transpose(0, 2, 1)
    b_rt = jnp.pad(jnp.concatenate([b_exp, b_grp], axis=1),
                   ((0, 0), (0, ROUTE_ROWS - N_EXPERTS - N_GROUPS)))[:, :, None]
    per_layer_row = lambda a: a[:, None, :]

    in_proj_params = (per_layer_row(norm_mix_g), w_na, w_qk, w_v, w_r, w_lr, w_g2, b_g)
    proj = _in_proj(xf, 0, *in_proj_params)
    for l in range(depth):
        na_qkv, gqk, gv, gr, gates = proj
        y_na = _na(na_qkv, bias_tbl, l, batch, rows)
        o_f, o_b = _gla(gqk, gv, gates, batch, seq)
        x1, h2_slabs, rt_t, counts = _out_route(y_na, o_f, o_b, gr, xf, l, w_o_na, w_o_gla,
                                                per_layer_row(gla_norm_g), per_layer_row(norm_ffn_g), w_rt3, b_rt)

        tile_start, tiles, n_tiles, pad_tile, offs_b, sorted_rows = _dispatch_plan(counts, n)
        pos, rt, xs = _dispatch(pad_tile, n_tiles, rt_t, offs_b, h2_slabs, sorted_rows)
        ys = _experts(tile_start, tiles, xs, w_gate, w_up, w_down, l)
        pos = pos.reshape(-1)
        if l + 1 < depth:
            xf, proj = _combine_in_proj(pos, ys, x1, rt, l + 1, *in_proj_params)
    return _combine_final(pos, ys, x1, rt, final_norm_g[None, :]).reshape(batch, seq, d)
```

```python
import functools

import jax
import jax.numpy as jnp
from jax import lax
from jax.experimental import pallas as pl
from jax.experimental.pallas import tpu as pltpu

F32 = jnp.float32
BF16 = jnp.bfloat16

GRID_W = 64
NA_HEADS = 8
NA_HEAD_DIM = 64
NA_WIDTH = NA_HEADS * NA_HEAD_DIM
WIN_H_MAX = 8
WIN_W = 16
GLA_HEADS = 4
GLA_DK = 64
GLA_DV = 128
GLA_KEY_WIDTH = GLA_HEADS * GLA_DK
GLA_VAL_WIDTH = GLA_HEADS * GLA_DV
GLA_GATE_RANK = 16
GLA_GATE_NORMALIZER = 16.0
GLA_CHUNK = 64
N_GROUPS = 4
EXPERTS_PER_GROUP = 8
N_EXPERTS = N_GROUPS * EXPERTS_PER_GROUP
RMS_EPS = 1e-6

LANES = 128
SUBLANES = 8
MXU_WIDTH = 256
VMEM_LIMIT_BYTES = 56 * 1024 * 1024

MASK_VALUE = -1e30

IN_PROJ_ROWS = 512
NA_ROW_BLOCK = 16
NA_ROW_UNROLL = 8
GLA_STEP_CHUNKS = 4
OUT_ROWS = 512
EXPERT_ROWS = 256
EXPERT_TILE_BUFFERS = 4
DISPATCH_ROWS = 1024
COMBINE_ROWS = DISPATCH_ROWS
COMBINE_STEP_ROWS = 512


def _params(*sem):
    return pltpu.CompilerParams(dimension_semantics=sem, vmem_limit_bytes=VMEM_LIMIT_BYTES)


def _rms(x, g):
    return x * lax.rsqrt(jnp.mean(x * x, axis=-1, keepdims=True) + RMS_EPS) * g


def _in_proj_rows(x, g_ref, w_na_ref, w_qk_ref, w_v_ref, w_r_ref, w_lr_ref,
                  w_g2_ref, b_g_ref, na_ref, qk_ref, v_ref, r_ref, gate_ref, w_gate):
    @pl.when(pl.program_id(0) == 0)
    def _():
        w_gate[...] = jnp.dot(w_lr_ref[...], w_g2_ref[...], preferred_element_type=F32,
                              precision=lax.Precision.HIGHEST).astype(BF16)

    h = _rms(x, g_ref[...]).astype(BF16)
    na_ref[...] = jnp.dot(h, w_na_ref[...], preferred_element_type=F32).astype(BF16)
    qk_ref[...] = jnp.dot(h, w_qk_ref[...], preferred_element_type=F32)
    v_ref[...] = jnp.dot(h, w_v_ref[...], preferred_element_type=F32).astype(BF16)
    r_ref[...] = jnp.dot(h, w_r_ref[...], preferred_element_type=F32)
    z = jnp.dot(h, w_gate[...], preferred_element_type=F32) + b_g_ref[...]
    log_sig = jnp.minimum(z, 0.0) - jnp.log(1.0 + jnp.exp(-jnp.abs(z)))
    gate_ref[...] = log_sig * (1.0 / GLA_GATE_NORMALIZER)


def _in_proj_kernel(x_ref, *refs):
    _in_proj_rows(x_ref[...], *refs)


def _in_proj_outputs(n, rows):
    row_spec = lambda width: pl.BlockSpec((rows, width), lambda i, *_: (i, 0))
    widths = (3 * NA_WIDTH, 2 * GLA_KEY_WIDTH, GLA_VAL_WIDTH, GLA_VAL_WIDTH, 2 * GLA_KEY_WIDTH)
    dtypes = (BF16, F32, BF16, F32, F32)
    return ([row_spec(w) for w in widths],
            [jax.ShapeDtypeStruct((n, w), t) for w, t in zip(widths, dtypes)])


def _layer_spec(a, layer):
    return pl.BlockSpec((None,) + a.shape[1:], lambda *_: (layer,) + (0,) * (a.ndim - 1))


def _in_proj(x, layer, g, w_na, w_qk, w_v, w_r, w_lr, w_g2, b_g):
    n, d = x.shape
    rows = IN_PROJ_ROWS
    params = (g, w_na, w_qk, w_v, w_r, w_lr, w_g2, b_g)
    out_specs, out_shape = _in_proj_outputs(n, rows)
    return pl.pallas_call(
        _in_proj_kernel,
        grid=(n // rows,),
        in_specs=[pl.BlockSpec((rows, d), lambda i: (i, 0))] + [_layer_spec(a, layer) for a in params],
        out_specs=out_specs,
        out_shape=out_shape,
        scratch_shapes=[pltpu.VMEM((d, 2 * GLA_KEY_WIDTH), BF16)],
        compiler_params=_params("arbitrary"),
        name="in_proj",
    )(x, *params)


def _na_bias_table(rpb, rows):
    kh = min(WIN_H_MAX, rows)
    w = jnp.arange(GRID_W)[:, None, None]
    x = jnp.arange(GRID_W)[None, :, None]
    cs = jnp.clip(w - WIN_W // 2, 0, GRID_W - WIN_W)
    valid = (x >= cs) & (x < cs + WIN_W)
    col_sel = (valid & (jnp.arange(2 * WIN_W - 1)[None, None, :] == x - w + (WIN_W - 1))).astype(F32)
    by_col = jnp.einsum("hrc,wxc->hwrx", rpb.astype(F32), col_sel, precision=lax.Precision.HIGHEST)
    by_col = jnp.where(valid[None, :, None, :, 0], by_col, MASK_VALUE)
    by_col = by_col.reshape(NA_HEADS * GRID_W, (2 * WIN_H_MAX - 1) * GRID_W)
    bias = jnp.stack([by_col[:, (WIN_H_MAX - 1 - c) * GRID_W:(WIN_H_MAX - 1 - c + kh) * GRID_W]
                      for c in range(kh)])
    return bias.reshape(kh, NA_HEADS // 2, 2 * GRID_W, kh * GRID_W)


def _na_window_start(j, rows, kh):
    return jnp.clip(j * NA_ROW_BLOCK - kh // 2, 0, rows - (NA_ROW_BLOCK + kh))


def _na_kernel(q_ref, k_ref, v_ref, tbl_ref, o_ref, *, rows, kh):
    rb = NA_ROW_BLOCK
    j = pl.program_id(1)
    fetched = _na_window_start(j, rows, kh)
    lane = lax.broadcasted_iota(jnp.int32, (GRID_W, LANES), 1)
    first = lane < NA_HEAD_DIM

    def row_body(lr, carry):
        r = j * rb + lr
        start = jnp.clip(r - kh // 2, 0, rows - kh)
        cls = r - start
        q_row = q_ref[lr]
        k_win = k_ref[0, pl.ds(start - fetched, kh)].reshape(kh * GRID_W, NA_WIDTH)
        v_win = v_ref[0, pl.ds(start - fetched, kh)].reshape(kh * GRID_W, NA_WIDTH)
        pairs = [slice(p * LANES, (p + 1) * LANES) for p in range(NA_HEADS // 2)]
        scores = []
        for sl in pairs:
            q_pair = q_row[:, sl]
            zero = jnp.zeros_like(q_pair)
            q_bd = jnp.concatenate([jnp.where(first, q_pair, zero), jnp.where(first, zero, q_pair)], axis=0)
            scores.append(lax.dot_general(q_bd, k_win[:, sl], (((1,), (1,)), ((), ())),
                                          preferred_element_type=F32))
        probs, denoms = [], []
        for p, s in enumerate(scores):
            s = s + tbl_ref[cls, p]
            e = jnp.exp(s - jnp.max(s, axis=-1, keepdims=True))
            denoms.append(jnp.sum(e, axis=-1, keepdims=True))
            probs.append(e.astype(BF16))
        outs = []
        for sl, e, denom in zip(pairs, probs, denoms):
            o = jnp.dot(e, v_win[:, sl], preferred_element_type=F32) / denom
            outs.append(jnp.where(first, o[:GRID_W], o[GRID_W:]))
        o_ref[lr] = jnp.concatenate(outs, axis=-1).astype(o_ref.dtype)
        return carry

    lax.fori_loop(0, rb, row_body, 0, unroll=NA_ROW_UNROLL)


def _na(na_qkv, tbl, layer, batch, rows):
    kh = min(WIN_H_MAX, rows)
    rb = NA_ROW_BLOCK
    nblk = rows // rb
    x4 = na_qkv.reshape(batch, rows, GRID_W, 3 * NA_WIDTH)
    assert rows >= rb + kh
    blk = (None, rb, GRID_W, NA_WIDTH)
    window = (pl.Element(1), pl.Element(rb + kh), pl.Element(GRID_W), pl.Element(NA_WIDTH))
    specs = [pl.BlockSpec(blk, lambda b, j: (b, j, 0, 0))]
    for col in (1, 2):
        specs.append(pl.BlockSpec(window, lambda b, j, col=col: (b, _na_window_start(j, rows, kh), 0,
                                                                  col * NA_WIDTH)))
    specs.append(_layer_spec(tbl, layer))
    out = pl.pallas_call(
        functools.partial(_na_kernel, rows=rows, kh=kh),
        grid=(batch, nblk),
        in_specs=specs,
        out_specs=pl.BlockSpec(blk, lambda b, j: (b, j, 0, 0)),
        out_shape=jax.ShapeDtypeStruct((batch, rows, GRID_W, NA_WIDTH), BF16),
        compiler_params=_params("arbitrary", "arbitrary"),
        name="na",
    )(x4, x4, x4, tbl)
    return out.reshape(batch * rows * GRID_W, NA_WIDTH)


def _block_diag_mask(row_block, col_block, nblocks):
    shape = (row_block * nblocks, col_block * nblocks)
    r = lax.shift_right_logical(lax.broadcasted_iota(jnp.int32, shape, 0), row_block.bit_length() - 1)
    c = lax.shift_right_logical(lax.broadcasted_iota(jnp.int32, shape, 1), col_block.bit_length() - 1)
    return r == c


def _split_bf16x3(x):
    hi = x.astype(BF16)
    rest = x - hi.astype(F32)
    mid = rest.astype(BF16)
    lo = (rest - mid.astype(F32)).astype(BF16)
    return hi, mid, lo


def _gla_direction(qk_ref, v_ref, g_ref, *, backward):
    c = GLA_CHUNK
    nc = GLA_STEP_CHUNKS
    kw = GLA_KEY_WIDTH
    order = list(reversed(range(nc))) if backward else list(range(nc))
    chunk = lambda a, n: a[n * c:(n + 1) * c]

    def stage_cumsum():
        step = nc * c
        ti = lax.broadcasted_iota(jnp.int32, (step, step), 0)
        tj = lax.broadcasted_iota(jnp.int32, (step, step), 1)
        same_chunk = lax.shift_right_logical(ti, c.bit_length() - 1) == lax.shift_right_logical(tj, c.bit_length() - 1)
        tri = (tj >= ti) if backward else (tj <= ti)
        cum = jnp.where(same_chunk & tri, 1.0, 0.0).astype(BF16)
        return sum(jnp.dot(cum, piece, preferred_element_type=F32) for piece in _split_bf16x3(g_ref[...]))

    def stage_decays(b, idx):
        n = order[idx]
        ref_row, last_row = (c // 2, 0) if backward else (c // 2 - 1, c - 1)
        b_n = chunk(b, n)
        b_ref = jnp.broadcast_to(b_n[ref_row:ref_row + 1, :], (c, kw))
        b_last = jnp.broadcast_to(b_n[last_row:last_row + 1, :], (c, kw))
        q = qk_ref[n * c:(n + 1) * c, :kw]
        k = qk_ref[n * c:(n + 1) * c, kw:]
        q_rel = (q * jnp.exp(b_n - b_ref)).astype(BF16)
        k_rel = (k * jnp.exp(b_ref - b_n)).astype(BF16)
        k_dec = k * jnp.exp(b_last - b_n)
        q_dec = (q * jnp.exp(b_n)).astype(BF16)
        decay = jnp.exp(b_last)
        return q_rel, k_rel, k_dec, q_dec, decay

    def stage_scores(q_rel, k_rel):
        kk_mask = _block_diag_mask(c, GLA_DK, GLA_HEADS)
        k_bd = jnp.where(kk_mask, jnp.concatenate([k_rel] * GLA_HEADS, axis=0), jnp.zeros((), BF16))
        return lax.dot_general(q_rel, k_bd, (((1,), (1,)), ((), ())), preferred_element_type=F32)

    def stage_intra(scores, k_dec, decay, idx):
        n = order[idx]
        si = lax.broadcasted_iota(jnp.int32, (c, c * GLA_HEADS), 0)
        sj = lax.broadcasted_iota(jnp.int32, (c, c * GLA_HEADS), 1) & (c - 1)
        keep = (sj > si) if backward else (sj <= si)
        kv_mask = _block_diag_mask(c, GLA_DV, GLA_HEADS)
        vk_mask = _block_diag_mask(GLA_DK, GLA_DV, 2)
        v = v_ref[n * c:(n + 1) * c, :]
        p = jnp.where(keep, scores, 0.0).astype(BF16)
        v_bd = jnp.where(kv_mask, jnp.concatenate([v] * GLA_HEADS, axis=0), jnp.zeros((), BF16))
        o_intra = jnp.dot(p, v_bd, preferred_element_type=F32)
        k_dec_t = k_dec.T.astype(BF16)
        upd = []
        for hp in range(GLA_HEADS // 2):
            kv = jnp.dot(k_dec_t[hp * 2 * GLA_DK:(hp + 1) * 2 * GLA_DK],
                         v[:, hp * 2 * GLA_DV:(hp + 1) * 2 * GLA_DV], preferred_element_type=F32)
            upd.append(jnp.where(vk_mask, kv, 0.0))
        return o_intra, upd, decay.T[:, :1]

    def scan_step(state, idx, q_dec, o_intra, upd, decay_col, out_ref):
        n = order[idx]
        pw = 2 * GLA_DK
        o_inter = jnp.concatenate(
            [jnp.dot(q_dec[:, hp * pw:(hp + 1) * pw], s.astype(BF16), preferred_element_type=F32)
             for hp, s in enumerate(state)], axis=1)
        out_ref[n * c:(n + 1) * c, :] = o_intra + o_inter
        return [s * decay_col[hp * pw:(hp + 1) * pw] + u for hp, (s, u) in enumerate(zip(state, upd))]

    return stage_cumsum, stage_decays, stage_scores, stage_intra, scan_step


def _gla_kernel(qk_f_ref, v_f_ref, g_f_ref, qk_b_ref, v_b_ref, g_b_ref, o_f_ref, o_b_ref, st):
    @pl.when(pl.program_id(0) == 0)
    def _():
        st[...] = jnp.zeros_like(st)

    batch = qk_f_ref.shape[0]
    streams, outs = [], []
    for bi in range(batch):
        streams.append(_gla_direction(qk_f_ref.at[bi], v_f_ref.at[bi], g_f_ref.at[bi], backward=False))
        streams.append(_gla_direction(qk_b_ref.at[bi], v_b_ref.at[bi], g_b_ref.at[bi], backward=True))
        outs += [o_f_ref.at[bi], o_b_ref.at[bi]]
    b = [s[0]() for s in streams]
    pairs = range(GLA_HEADS // 2)
    states = [[st[si, hp] for hp in pairs] for si in range(len(streams))]
    for idx in range(GLA_STEP_CHUNKS):
        dec = [s[1](x, idx) for s, x in zip(streams, b)]
        scores = [s[2](x[0], x[1]) for s, x in zip(streams, dec)]
        intra = [s[3](sc, x[2], x[4], idx) for s, sc, x in zip(streams, scores, dec)]
        states = [s[4](state, idx, x[3], *y, out_ref)
                  for s, state, x, y, out_ref in zip(streams, states, dec, intra, outs)]
    for si, state in enumerate(states):
        for hp in pairs:
            st[si, hp] = state[hp]


def _gla(qk, v, gates, batch, seq):
    step = GLA_STEP_CHUNKS * GLA_CHUNK
    nblk = seq // step
    qk3 = qk.reshape(batch, seq, 2 * GLA_KEY_WIDTH)
    v3 = v.reshape(batch, seq, GLA_VAL_WIDTH)
    g3 = gates.reshape(batch, seq, 2 * GLA_KEY_WIDTH)
    fwd = lambda n: (0, n, 0)
    bwd = lambda n: (0, nblk - 1 - n, 0)
    bwd_gate = lambda n: (0, nblk - 1 - n, 1)
    o_f, o_b = pl.pallas_call(
        _gla_kernel,
        grid=(nblk,),
        in_specs=[pl.BlockSpec((batch, step, 2 * GLA_KEY_WIDTH), fwd),
                  pl.BlockSpec((batch, step, GLA_VAL_WIDTH), fwd),
                  pl.BlockSpec((batch, step, GLA_KEY_WIDTH), fwd),
                  pl.BlockSpec((batch, step, 2 * GLA_KEY_WIDTH), bwd),
                  pl.BlockSpec((batch, step, GLA_VAL_WIDTH), bwd),
                  pl.BlockSpec((batch, step, GLA_KEY_WIDTH), bwd_gate)],
        out_specs=[pl.BlockSpec((batch, step, GLA_VAL_WIDTH), fwd),
                   pl.BlockSpec((batch, step, GLA_VAL_WIDTH), bwd)],
        out_shape=[jax.ShapeDtypeStruct((batch, seq, GLA_VAL_WIDTH), F32)] * 2,
        scratch_shapes=[pltpu.VMEM((2 * batch, GLA_HEADS // 2, 2 * GLA_DK, 2 * GLA_DV), F32)],
        compiler_params=_params("arbitrary"),
        name="gla",
    )(qk3, v3, g3, qk3, v3, g3)
    return o_f.reshape(batch * seq, GLA_VAL_WIDTH), o_b.reshape(batch * seq, GLA_VAL_WIDTH)


def _store_slabs(ref, x):
    rows = x.shape[0]
    for s in range(SUBLANES):
        ref[pl.ds(s, rows, stride=SUBLANES), :] = x[:, s * LANES:(s + 1) * LANES]


def _load_slabs(ref, rows):
    return jnp.concatenate([ref[pl.ds(s, rows, stride=SUBLANES), :] for s in range(SUBLANES)], axis=1)


def _slab_copy(src_hbm, dst, sem, src_row, k):
    dst_row = k * SUBLANES if isinstance(k, int) else pl.multiple_of(k * SUBLANES, SUBLANES)
    return pltpu.make_async_copy(src_hbm.at[pl.ds(pl.multiple_of(src_row, SUBLANES), SUBLANES), :],
                                 dst.at[pl.ds(dst_row, SUBLANES), :], sem)


def _wait_slabs(src_hbm, dst, sem):
    pltpu.make_async_copy(src_hbm.at[pl.ds(0, dst.shape[0]), :], dst, sem).wait()


RT_E1, RT_E2, RT_RANK1, RT_RANK2, RT_W1, RT_W2 = range(6)
GROUP_ROW0 = N_EXPERTS
ROUTE_ROWS = 64


def _out_route_kernel(na_ref, of_ref, ob_ref, r_ref, x_ref, w_na_ref, w_gla_ref, gn_ref, fn_ref,
                      w_rt_ref, b_rt_ref, x1_ref, h2_ref, rt_ref, cnt_ref, carry):
    @pl.when(pl.program_id(0) == 0)
    def _():
        carry[...] = jnp.zeros_like(carry)

    rows = x_ref.shape[0]
    o = of_ref[...] + ob_ref[...]
    r = r_ref[...]
    parts = []
    for h in range(GLA_HEADS):
        sl = slice(h * GLA_DV, (h + 1) * GLA_DV)
        parts.append(_rms(o[:, sl], gn_ref[...]) * (r[:, sl] * jax.nn.sigmoid(r[:, sl])))
    y_gla = jnp.concatenate(parts, axis=-1).astype(BF16)
    x1 = (x_ref[...] + jnp.dot(na_ref[...], w_na_ref[...], preferred_element_type=F32)
          + jnp.dot(y_gla, w_gla_ref[...], preferred_element_type=F32))
    x1_ref[...] = x1
    h2 = _rms(x1, fn_ref[...])
    _store_slabs(h2_ref, h2)

    h_hi = h2.astype(BF16)
    h_lo = (h2 - h_hi.astype(F32)).astype(BF16)
    logits = lax.dot_general(w_rt_ref[...], jnp.concatenate([h_hi, h_lo, h_hi], axis=1),
                             (((1,), (1,)), ((), ())), preferred_element_type=F32)
    logits = logits[:ROUTE_ROWS] + b_rt_ref[...]
    row_i = lax.broadcasted_iota(jnp.int32, (ROUTE_ROWS, rows), 0)
    row = row_i.astype(F32)
    row_grp = lax.shift_right_logical(row_i, EXPERTS_PER_GROUP.bit_length() - 1).astype(F32)
    neg = jnp.float32(-jnp.inf)
    no_row = jnp.float32(ROUTE_ROWS)
    is_grp = (row_i >= GROUP_ROW0) & (row_i < GROUP_ROW0 + N_GROUPS)
    g_logit = jnp.where(is_grp, logits, neg)
    g_max = jnp.max(g_logit, axis=0, keepdims=True)
    g_sel = jnp.min(jnp.where(is_grp & (g_logit == g_max), row, no_row), axis=0, keepdims=True) - GROUP_ROW0
    grp_w = 1.0 / jnp.sum(jnp.where(is_grp, jnp.exp(g_logit - g_max), 0.0), axis=0, keepdims=True)
    in_grp = (row_i < N_EXPERTS) & (row_grp == g_sel)
    e_logit = jnp.where(in_grp, logits, neg)
    v1 = jnp.max(e_logit, axis=0, keepdims=True)
    i1 = jnp.min(jnp.where(in_grp & (e_logit == v1), row, no_row), axis=0, keepdims=True)
    rest = in_grp & (row != i1)
    e_logit2 = jnp.where(rest, logits, neg)
    v2 = jnp.max(e_logit2, axis=0, keepdims=True)
    i2 = jnp.min(jnp.where(rest & (e_logit2 == v2), row, no_row), axis=0, keepdims=True)
    t = jnp.exp(v2 - v1)
    w1 = grp_w / (1.0 + t)
    w2 = grp_w * t / (1.0 + t)

    sel1 = row == i1
    sel2 = row == i2
    onehot = jnp.where(sel1 | sel2, 1.0, 0.0)
    ti = lax.broadcasted_iota(jnp.int32, (rows, rows), 0)
    tj = lax.broadcasted_iota(jnp.int32, (rows, rows), 1)
    earlier = jnp.where(ti < tj, 1.0, 0.0).astype(BF16)
    ranks = jnp.dot(onehot.astype(BF16), earlier, preferred_element_type=F32) + carry[:, :1]
    rank1 = jnp.sum(jnp.where(sel1, ranks, 0.0), axis=0, keepdims=True)
    rank2 = jnp.sum(jnp.where(sel2, ranks, 0.0), axis=0, keepdims=True)
    new_carry = carry[...] + jnp.sum(onehot, axis=1, keepdims=True)
    carry[...] = new_carry
    cnt_ref[...] = new_carry

    fields = {RT_E1: i1, RT_E2: i2, RT_RANK1: rank1, RT_RANK2: rank2, RT_W1: w1, RT_W2: w2}
    zero = jnp.zeros_like(w1)
    rt_ref[...] = jnp.concatenate([fields.get(f, zero) for f in range(SUBLANES)], axis=0)


def _out_route(y_na, o_f, o_b, r, x, layer, w_na, w_gla, gn, fn, w_rt, b_rt):
    n, d = x.shape
    rows = OUT_ROWS
    row_spec = lambda width: pl.BlockSpec((rows, width), lambda i: (i, 0))
    full = lambda a: _layer_spec(a, layer)
    return pl.pallas_call(
        _out_route_kernel,
        grid=(n // rows,),
        in_specs=[row_spec(NA_WIDTH), row_spec(GLA_VAL_WIDTH), row_spec(GLA_VAL_WIDTH),
                  row_spec(GLA_VAL_WIDTH), row_spec(d), full(w_na), full(w_gla), full(gn), full(fn),
                  full(w_rt), full(b_rt)],
        out_specs=[row_spec(d), pl.BlockSpec((rows * SUBLANES, LANES), lambda i: (i, 0)),
                   pl.BlockSpec((SUBLANES, rows), lambda i: (0, i)),
                   pl.BlockSpec((ROUTE_ROWS, LANES), lambda i: (0, 0))],
        out_shape=[jax.ShapeDtypeStruct((n, d), F32), jax.ShapeDtypeStruct((n * SUBLANES, LANES), F32),
                   jax.ShapeDtypeStruct((SUBLANES, n), F32), jax.ShapeDtypeStruct((ROUTE_ROWS, LANES), F32)],
        scratch_shapes=[pltpu.VMEM((ROUTE_ROWS, LANES), F32)],
        compiler_params=_params("arbitrary"),
        name="out_route",
    )(y_na, o_f, o_b, r, x, w_na, w_gla, gn, fn, w_rt, b_rt)


def _dispatch_kernel(pad_tile_ref, n_tiles_ref, rt_ref, offs_ref, h_hbm, pos_ref, rec_ref, xs_hbm,
                     zeros, hbuf, pos_vmem, pos_smem, sem_zero, sem_pos, sem_in, sem):
    i = pl.program_id(0)
    n_steps = pl.num_programs(0)
    db = DISPATCH_ROWS
    tile = EXPERT_ROWS * SUBLANES
    max_tiles = xs_hbm.shape[0] // tile
    buf = lax.rem(i, 2)

    def block_copy(step, b):
        rows = pl.ds(pl.multiple_of(step * db * SUBLANES, db * SUBLANES), db * SUBLANES)
        return pltpu.make_async_copy(h_hbm.at[rows, :], hbuf.at[b], sem_in.at[b])

    def wait_row_copies(b):
        for _ in range(2):
            pltpu.make_async_copy(hbuf.at[b], xs_hbm.at[pl.ds(0, db * SUBLANES), :], sem).wait()

    def zero_copy(t):
        return pltpu.make_async_copy(zeros, xs_hbm.at[pl.ds(pl.multiple_of(t * tile, tile), tile), :], sem_zero)

    def for_each_zero_tile(fn):
        def padded(e, carry):
            @pl.when(pad_tile_ref[e] >= 0)
            def _():
                fn(zero_copy(pad_tile_ref[e]))
            return carry
        lax.fori_loop(0, N_EXPERTS, padded, 0)

        def tail(t, carry):
            fn(zero_copy(t))
            return carry
        lax.fori_loop(n_tiles_ref[0], max_tiles, tail, 0)

    @pl.when(i == 0)
    def _():
        block_copy(0, 0).start()
        zeros[...] = jnp.zeros_like(zeros)
        for_each_zero_tile(lambda cp: cp.start())

    rt_t = rt_ref[...]
    rec_ref[...] = jnp.concatenate([rt_t, jnp.zeros((LANES - SUBLANES, db), F32)], axis=0).T
    expert = lax.broadcasted_iota(jnp.int32, (N_EXPERTS, db), 0).astype(F32)
    pos = []
    for e_lane, rank_lane in ((RT_E1, RT_RANK1), (RT_E2, RT_RANK2)):
        seg = jnp.sum(jnp.where(expert == rt_t[e_lane:e_lane + 1, :], offs_ref[...], 0.0), axis=0, keepdims=True)
        pos.append((seg + rt_t[rank_lane:rank_lane + 1, :]) * SUBLANES)
    pos = jnp.concatenate(pos, axis=0).astype(jnp.int32)
    pos_ref[...] = pos
    pos_vmem[...] = jnp.concatenate([pos, jnp.zeros((SUBLANES - 2, db), jnp.int32)], axis=0)
    to_smem = pltpu.make_async_copy(pos_vmem, pos_smem, sem_pos)
    to_smem.start()

    @pl.when(i == 0)
    def _():
        for_each_zero_tile(lambda cp: cp.wait())

    @pl.when(i > 0)
    def _():
        wait_row_copies(1 - buf)

    to_smem.wait()
    block_copy(i, buf).wait()

    @pl.when(i + 1 < n_steps)
    def _():
        block_copy(i + 1, 1 - buf).start()

    h_blk = hbuf.at[buf]
    for k in range(db):
        src = h_blk.at[pl.ds(k * SUBLANES, SUBLANES), :]
        for slot in range(2):
            dst_row = pl.multiple_of(pos_smem[slot, k], SUBLANES)
            pltpu.make_async_copy(src, xs_hbm.at[pl.ds(dst_row, SUBLANES), :], sem).start(priority=slot)

    @pl.when(i == n_steps - 1)
    def _():
        wait_row_copies(buf)


def _dispatch(pad_tile, n_tiles, rt_t, offs_b, h2_slabs, sorted_rows):
    n = rt_t.shape[1]
    db = DISPATCH_ROWS
    grid_spec = pltpu.PrefetchScalarGridSpec(
        num_scalar_prefetch=2,
        grid=(n // db,),
        in_specs=[pl.BlockSpec((SUBLANES, db), lambda i, pt, nt: (0, i)),
                  pl.BlockSpec(offs_b.shape, lambda i, pt, nt: (0, 0)),
                  pl.BlockSpec(memory_space=pl.ANY)],
        out_specs=[pl.BlockSpec((None, 2, db), lambda i, pt, nt: (i, 0, 0)),
                   pl.BlockSpec((db, LANES), lambda i, pt, nt: (i, 0)),
                   pl.BlockSpec(memory_space=pl.ANY)],
        scratch_shapes=[pltpu.VMEM((EXPERT_ROWS * SUBLANES, LANES), F32),
                        pltpu.VMEM((2, db * SUBLANES, LANES), F32),
                        pltpu.VMEM((SUBLANES, db), jnp.int32), pltpu.SMEM((SUBLANES, db), jnp.int32),
                        pltpu.SemaphoreType.DMA(()), pltpu.SemaphoreType.DMA(()),
                        pltpu.SemaphoreType.DMA((2,)), pltpu.SemaphoreType.DMA(())],
    )
    return pl.pallas_call(
        _dispatch_kernel,
        grid_spec=grid_spec,
        out_shape=[jax.ShapeDtypeStruct((n // db, 2, db), jnp.int32),
                   jax.ShapeDtypeStruct((n, LANES), F32),
                   jax.ShapeDtypeStruct((sorted_rows * SUBLANES, LANES), F32)],
        compiler_params=_params("arbitrary"),
        name="dispatch",
    )(pad_tile, n_tiles, rt_t, offs_b, h2_slabs)


TILE_DMA_PRIORITY = 1


def _expert_tile(x_ref, y_ref, wg_s, wu_s, wd_s):
    rows = EXPERT_ROWS
    x = _load_slabs(x_ref, rows).astype(BF16)
    hid = []
    for c in range(0, wg_s.shape[1], MXU_WIDTH):
        gate = jnp.dot(x, wg_s[:, c:c + MXU_WIDTH], preferred_element_type=F32)
        up = jnp.dot(x, wu_s[:, c:c + MXU_WIDTH], preferred_element_type=F32)
        hid.append((gate * jax.nn.sigmoid(gate) * up).astype(BF16))
    hid = jnp.concatenate(hid, axis=1)
    for c in range(0, wd_s.shape[1], MXU_WIDTH):
        y = jnp.dot(hid, wd_s[:, c:c + MXU_WIDTH], preferred_element_type=F32)
        for s in range(c // LANES, (c + MXU_WIDTH) // LANES):
            y_ref[pl.ds(s, rows, stride=SUBLANES), :] = y[:, s * LANES - c:(s + 1) * LANES - c]


def _experts_kernel(tile_start_ref, tiles_ref, x_hbm, wg_hbm, wu_hbm, wd_hbm, y_hbm,
                    xbuf, ybuf, wg_f, wu_f, wd_f, wg_s, wu_s, wd_s, sem_x, sem_y, sem_w, *, layer):
    nbuf = EXPERT_TILE_BUFFERS
    e = pl.program_id(0)
    last_step = e == pl.num_programs(0) - 1
    tile = EXPERT_ROWS * SUBLANES
    n = tiles_ref[e]
    base = tile_start_ref[e]
    n_tiles = tile_start_ref[N_EXPERTS - 1] + tiles_ref[N_EXPERTS - 1]
    tile_rows = lambda g: pl.ds(pl.multiple_of(g * tile, tile), tile)

    def x_copy(g, slot):
        return pltpu.make_async_copy(x_hbm.at[tile_rows(g), :], xbuf.at[slot], sem_x.at[slot])

    def y_copy(g, slot):
        return pltpu.make_async_copy(ybuf.at[slot], y_hbm.at[tile_rows(g), :], sem_y.at[slot])

    def w_copies(expert, slot):
        return [pltpu.make_async_copy(w_hbm.at[layer, expert], stage.at[slot], sem_w.at[slot])
                for w_hbm, stage in ((wg_hbm, wg_f), (wu_hbm, wu_f), (wd_hbm, wd_f))]

    w_slot = lax.rem(e, 2)

    @pl.when(e == 0)
    def _():
        for g in range(nbuf - 1):
            @pl.when(g < n_tiles)
            def _():
                x_copy(g, g).start(priority=TILE_DMA_PRIORITY)
        for cp in w_copies(0, 0):
            cp.start()

    for cp in w_copies(e, w_slot):
        cp.wait()

    @pl.when(jnp.logical_not(last_step))
    def _():
        for cp in w_copies(e + 1, 1 - w_slot):
            cp.start()

    @pl.when(n > 0)
    def _():
        wg_s[...] = wg_f[w_slot].astype(BF16)
        wu_s[...] = wu_f[w_slot].astype(BF16)
        wd_s[...] = wd_f[w_slot].astype(BF16)

        def tile_body(g, carry):
            slot = lax.rem(g, nbuf)
            x_copy(g, slot).wait()

            @pl.when(g + nbuf - 1 < n_tiles)
            def _():
                x_copy(g + nbuf - 1, lax.rem(g + nbuf - 1, nbuf)).start(priority=TILE_DMA_PRIORITY)

            @pl.when(g >= nbuf)
            def _():
                y_copy(g - nbuf, slot).wait()

            _expert_tile(xbuf.at[slot], ybuf.at[slot], wg_s, wu_s, wd_s)
            y_copy(g, slot).start(priority=TILE_DMA_PRIORITY)
            return carry
        lax.fori_loop(base, base + n, tile_body, 0)

    @pl.when(last_step)
    def _():
        for back in range(nbuf, 0, -1):
            @pl.when(n_tiles >= back)
            def _():
                y_copy(n_tiles - back, lax.rem(n_tiles - back, nbuf)).wait()

        first_unused = n_tiles
        max_tiles = y_hbm.shape[0] // tile
        ybuf[0] = jnp.zeros(ybuf.shape[1:], ybuf.dtype)
        zero_copy = lambda t: pltpu.make_async_copy(
            ybuf.at[0], y_hbm.at[pl.ds(pl.multiple_of(t * tile, tile), tile), :], sem_y.at[0])

        def start(t, carry):
            zero_copy(t).start()
            return carry

        def wait(t, carry):
            zero_copy(t).wait()
            return carry
        lax.fori_loop(first_unused, max_tiles, start, 0)
        lax.fori_loop(first_unused, max_tiles, wait, 0)


def _experts(tile_start, tiles, xs, wg, wu, wd, layer):
    rows = EXPERT_ROWS
    _, n_experts, d, dff = wg.shape
    assert d == SUBLANES * LANES and n_experts == N_EXPERTS
    nbuf = EXPERT_TILE_BUFFERS
    any_spec = pl.BlockSpec(memory_space=pl.ANY)
    grid_spec = pltpu.PrefetchScalarGridSpec(
        num_scalar_prefetch=2,
        grid=(n_experts,),
        in_specs=[any_spec, any_spec, any_spec, any_spec],
        out_specs=any_spec,
        scratch_shapes=[pltpu.VMEM((nbuf, rows * SUBLANES, LANES), F32),
                        pltpu.VMEM((nbuf, rows * SUBLANES, LANES), F32),
                        pltpu.VMEM((2, d, dff), F32), pltpu.VMEM((2, d, dff), F32), pltpu.VMEM((2, dff, d), F32),
                        pltpu.VMEM((d, dff), BF16), pltpu.VMEM((d, dff), BF16), pltpu.VMEM((dff, d), BF16),
                        pltpu.SemaphoreType.DMA((nbuf,)), pltpu.SemaphoreType.DMA((nbuf,)),
                        pltpu.SemaphoreType.DMA((2,))],
    )
    return pl.pallas_call(
        functools.partial(_experts_kernel, layer=layer),
        grid_spec=grid_spec,
        out_shape=jax.ShapeDtypeStruct(xs.shape, F32),
        compiler_params=_params("arbitrary"),
        name="experts",
    )(tile_start, tiles, xs, wg, wu, wd)


def _start_slab_gather(src_hbm, dst, sem, row_ref, base, count, priority):
    for k in range(count):
        _slab_copy(src_hbm, dst, sem, row_ref[base + k], k).start(priority=priority)


def _combined_rows(pos_ref, ys_hbm, x1_ref, rt_ref, buf, sem):
    i = pl.program_id(0)
    n_steps = pl.num_programs(0)
    rows = x1_ref.shape[0]
    steps_per_block = COMBINE_ROWS // rows
    slot = i % 2

    def start(step, s):
        first = lax.div(step, steps_per_block) * (2 * COMBINE_ROWS) + lax.rem(step, steps_per_block) * rows
        for j in range(2):
            _start_slab_gather(ys_hbm, buf.at[s, j], sem.at[s], pos_ref, first + j * COMBINE_ROWS, rows, priority=j)

    @pl.when(i == 0)
    def _():
        start(0, 0)

    @pl.when(i + 1 < n_steps)
    def _():
        start(i + 1, 1 - slot)

    _wait_slabs(ys_hbm, buf.at[slot, 0], sem.at[slot])
    _wait_slabs(ys_hbm, buf.at[slot, 1], sem.at[slot])
    rt = rt_ref[...]
    lane = lax.broadcasted_iota(jnp.int32, rt.shape, 1)
    w1 = jnp.sum(jnp.where(lane == RT_W1, rt, 0.0), axis=-1, keepdims=True)
    w2 = jnp.sum(jnp.where(lane == RT_W2, rt, 0.0), axis=-1, keepdims=True)
    y = w1 * _load_slabs(buf.at[slot, 0], rows) + w2 * _load_slabs(buf.at[slot, 1], rows)
    return x1_ref[...] + y


def _combine_scratch(rows):
    return [pltpu.VMEM((2, 2, rows * SUBLANES, LANES), F32), pltpu.SemaphoreType.DMA((2,))]


def _combine_final_kernel(pos_ref, ys_hbm, x1_ref, rt_ref, g_ref, o_ref, buf, sem):
    o_ref[...] = _rms(_combined_rows(pos_ref, ys_hbm, x1_ref, rt_ref, buf, sem), g_ref[...])


def _combine_final(pos, ys, x1, rt, g):
    n, d = x1.shape
    rows = COMBINE_STEP_ROWS
    grid_spec = pltpu.PrefetchScalarGridSpec(
        num_scalar_prefetch=1,
        grid=(n // rows,),
        in_specs=[pl.BlockSpec(memory_space=pl.ANY),
                  pl.BlockSpec((rows, d), lambda i, pos: (i, 0)),
                  pl.BlockSpec((rows, LANES), lambda i, pos: (i, 0)),
                  pl.BlockSpec(g.shape, lambda i, pos: (0, 0))],
        out_specs=pl.BlockSpec((rows, d), lambda i, pos: (i, 0)),
        scratch_shapes=_combine_scratch(rows),
    )
    return pl.pallas_call(
        _combine_final_kernel,
        grid_spec=grid_spec,
        out_shape=jax.ShapeDtypeStruct((n, d), F32),
        compiler_params=_params("arbitrary"),
        name="combine",
    )(pos, ys, x1, rt, g)


def _combine_in_proj_kernel(pos_ref, ys_hbm, x1_ref, rt_ref, *refs):
    param_refs, (x_ref, *proj_refs), (w_gate, buf, sem) = refs[:8], refs[8:14], refs[14:]
    x = _combined_rows(pos_ref, ys_hbm, x1_ref, rt_ref, buf, sem)
    x_ref[...] = x
    _in_proj_rows(x, *param_refs, *proj_refs, w_gate)


def _combine_in_proj(pos, ys, x1, rt, layer, g, w_na, w_qk, w_v, w_r, w_lr, w_g2, b_g):
    n, d = x1.shape
    rows = COMBINE_STEP_ROWS
    params = (g, w_na, w_qk, w_v, w_r, w_lr, w_g2, b_g)
    proj_specs, proj_shapes = _in_proj_outputs(n, rows)
    grid_spec = pltpu.PrefetchScalarGridSpec(
        num_scalar_prefetch=1,
        grid=(n // rows,),
        in_specs=[pl.BlockSpec(memory_space=pl.ANY),
                  pl.BlockSpec((rows, d), lambda i, pos: (i, 0)),
                  pl.BlockSpec((rows, LANES), lambda i, pos: (i, 0))] + [_layer_spec(a, layer) for a in params],
        out_specs=[pl.BlockSpec((rows, d), lambda i, pos: (i, 0))] + proj_specs,
        scratch_shapes=[pltpu.VMEM((d, 2 * GLA_KEY_WIDTH), BF16)] + _combine_scratch(rows),
    )
    x, *proj = pl.pallas_call(
        _combine_in_proj_kernel,
        grid_spec=grid_spec,
        out_shape=[jax.ShapeDtypeStruct((n, d), F32)] + proj_shapes,
        compiler_params=_params("arbitrary"),
        name="combine_in_proj",
    )(pos, ys, x1, rt, *params)
    return x, proj


def _dispatch_plan(counts, n):
    rows = EXPERT_ROWS
    max_tiles = (2 * n) // rows + N_EXPERTS
    cnt = counts[:N_EXPERTS, 0].astype(jnp.int32)
    tiles = (cnt + rows - 1) // rows
    tile_end = jnp.cumsum(tiles)
    tile_start = tile_end - tiles
    n_tiles = tile_end[-1:]
    pad_tile = jnp.where(cnt % rows != 0, tile_end - 1, -1).astype(jnp.int32)
    offs_b = jnp.broadcast_to((tile_start * rows).astype(F32)[:, None], (N_EXPERTS, DISPATCH_ROWS))
    return tile_start.astype(jnp.int32), tiles, n_tiles, pad_tile, offs_b, max_tiles * rows


def kernel(x, norm_mix_g, w_in, w_g2_f, b_g_f, w_g2_b, b_g_b, gla_norm_g, rpb, w_out, norm_ffn_g, w_grp, b_grp,
           w_exp, b_exp, w_gate, w_up, w_down, final_norm_g):
    batch, seq, d = x.shape
    n = batch * seq
    rows = seq // GRID_W
    depth = w_in.shape[0]
    xf = x.reshape(n, d)
    c_na = 3 * NA_WIDTH
    c_qk = c_na + 2 * GLA_KEY_WIDTH
    c_v = c_qk + GLA_VAL_WIDTH
    c_r = c_v + GLA_VAL_WIDTH

    q_scale = jnp.concatenate([jnp.full((NA_WIDTH,), NA_HEAD_DIM ** -0.5, F32), jnp.ones((2 * NA_WIDTH,), F32)])
    w_na = (w_in[:, :, :c_na] * q_scale).astype(BF16)
    qk_scale = jnp.concatenate([jnp.full((GLA_KEY_WIDTH,), GLA_DK ** -0.5, F32), jnp.ones((GLA_KEY_WIDTH,), F32)])
    w_qk = (w_in[:, :, c_na:c_qk] * qk_scale).astype(BF16)
    w_v = w_in[:, :, c_qk:c_v].astype(BF16)
    w_r = w_in[:, :, c_v:c_r].astype(BF16)
    w_lr = jnp.pad(w_in[:, :, c_r:], ((0, 0), (0, 0), (0, LANES - 2 * GLA_GATE_RANK)))
    zero_g2 = jnp.zeros_like(w_g2_f)
    w_g2 = jnp.concatenate([jnp.concatenate([w_g2_f, zero_g2], axis=2),
                            jnp.concatenate([zero_g2, w_g2_b], axis=2)], axis=1)
    w_g2 = jnp.pad(w_g2, ((0, 0), (0, LANES - 2 * GLA_GATE_RANK), (0, 0)))
    b_g = jnp.concatenate([b_g_f, b_g_b], axis=1)[:, None, :]
    bias_tbl = jax.vmap(functools.partial(_na_bias_table, rows=rows))(rpb)
    w_o = w_out.astype(BF16)
    w_o_na, w_o_gla = w_o[:, :NA_WIDTH], w_o[:, NA_WIDTH:]
    w_rt = jnp.pad(jnp.concatenate([w_exp, w_grp], axis=2), ((0, 0), (0, 0), (0, LANES - N_EXPERTS - N_GROUPS)))
    w_rt_hi = w_rt.astype(BF16)
    w_rt_lo = (w_rt - w_rt_hi.astype(F32)).astype(BF16)
    w_rt3 = jnp.concatenate([w_rt_hi, w_rt_hi, w_rt_lo], axis=1).transpose(0, 2, 1)
    b_rt = jnp.pad(jnp.concatenate([b_exp, b_grp], axis=1),
                   ((0, 0), (0, ROUTE_ROWS - N_EXPERTS - N_GROUPS)))[:, :, None]
    per_layer_row = lambda a: a[:, None, :]

    in_proj_params = (per_layer_row(norm_mix_g), w_na, w_qk, w_v, w_r, w_lr, w_g2, b_g)
    proj = _in_proj(xf, 0, *in_proj_params)
    for l in range(depth):
        na_qkv, gqk, gv, gr, gates = proj
        y_na = _na(na_qkv, bias_tbl, l, batch, rows)
        o_f, o_b = _gla(gqk, gv, gates, batch, seq)
        x1, h2_slabs, rt_t, counts = _out_route(y_na, o_f, o_b, gr, xf, l, w_o_na, w_o_gla,
                                                per_layer_row(gla_norm_g), per_layer_row(norm_ffn_g), w_rt3, b_rt)

        tile_start, tiles, n_tiles, pad_tile, offs_b, sorted_rows = _dispatch_plan(counts, n)
        pos, rt, xs = _dispatch(pad_tile, n_tiles, rt_t, offs_b, h2_slabs, sorted_rows)
        ys = _experts(tile_start, tiles, xs, w_gate, w_up, w_down, l)
        pos = pos.reshape(-1)
        if l + 1 < depth:
            xf, proj = _combine_in_proj(pos, ys, x1, rt, l + 1, *in_proj_params)
    return _combine_final(pos, ys, x1, rt, final_norm_g[None, :]).reshape(batch, seq, d)
```

```python
import functools

import jax
import jax.numpy as jnp
from jax import lax
from jax.experimental import pallas as pl
from jax.experimental.pallas import tpu as pltpu

F32 = jnp.float32
BF16 = jnp.bfloat16

GRID_W = 64
NA_HEADS = 8
NA_HEAD_DIM = 64
NA_WIDTH = NA_HEADS * NA_HEAD_DIM
WIN_H_MAX = 8
WIN_W = 16
GLA_HEADS = 4
GLA_DK = 64
GLA_DV = 128
GLA_KEY_WIDTH = GLA_HEADS * GLA_DK
GLA_VAL_WIDTH = GLA_HEADS * GLA_DV
GLA_GATE_RANK = 16
GLA_GATE_NORMALIZER = 16.0
GLA_CHUNK = 64
N_GROUPS = 4
EXPERTS_PER_GROUP = 8
N_EXPERTS = N_GROUPS * EXPERTS_PER_GROUP
RMS_EPS = 1e-6

LANES = 128
SUBLANES = 8
MXU_WIDTH = 256
VMEM_LIMIT_BYTES = 56 * 1024 * 1024

MASK_VALUE = -1e30

IN_PROJ_ROWS = 512
NA_ROW_BLOCK = 16
NA_ROW_UNROLL = 8
GLA_STEP_CHUNKS = 4
OUT_ROWS = 512
EXPERT_ROWS = 256
EXPERT_TILE_BUFFERS = 6
DISPATCH_ROWS = 1024
COMBINE_ROWS = DISPATCH_ROWS
COMBINE_STEP_ROWS = 512


def _params(*sem):
    return pltpu.CompilerParams(dimension_semantics=sem, vmem_limit_bytes=VMEM_LIMIT_BYTES)


def _rms(x, g):
    return x * lax.rsqrt(jnp.mean(x * x, axis=-1, keepdims=True) + RMS_EPS) * g


def _in_proj_rows(x, g_ref, w_na_ref, w_qk_ref, w_v_ref, w_r_ref, w_lr_ref,
                  w_g2_ref, b_g_ref, na_ref, qk_ref, v_ref, r_ref, gate_ref, w_gate):
    @pl.when(pl.program_id(0) == 0)
    def _():
        w_gate[...] = jnp.dot(w_lr_ref[...], w_g2_ref[...], preferred_element_type=F32,
                              precision=lax.Precision.HIGHEST).astype(BF16)

    h = _rms(x, g_ref[...]).astype(BF16)
    na_ref[...] = jnp.dot(h, w_na_ref[...], preferred_element_type=F32).astype(BF16)
    qk_ref[...] = jnp.dot(h, w_qk_ref[...], preferred_element_type=F32)
    v_ref[...] = jnp.dot(h, w_v_ref[...], preferred_element_type=F32).astype(BF16)
    r_ref[...] = jnp.dot(h, w_r_ref[...], preferred_element_type=F32)
    z = jnp.dot(h, w_gate[...], preferred_element_type=F32) + b_g_ref[...]
    log_sig = jnp.minimum(z, 0.0) - jnp.log(1.0 + jnp.exp(-jnp.abs(z)))
    gate_ref[...] = log_sig * (1.0 / GLA_GATE_NORMALIZER)


def _in_proj_kernel(x_ref, *refs):
    _in_proj_rows(x_ref[...], *refs)


def _in_proj_outputs(n, rows):
    row_spec = lambda width: pl.BlockSpec((rows, width), lambda i, *_: (i, 0))
    widths = (3 * NA_WIDTH, 2 * GLA_KEY_WIDTH, GLA_VAL_WIDTH, GLA_VAL_WIDTH, 2 * GLA_KEY_WIDTH)
    dtypes = (BF16, F32, BF16, F32, F32)
    return ([row_spec(w) for w in widths],
            [jax.ShapeDtypeStruct((n, w), t) for w, t in zip(widths, dtypes)])


def _layer_spec(a, layer):
    return pl.BlockSpec((None,) + a.shape[1:], lambda *_: (layer,) + (0,) * (a.ndim - 1))


def _in_proj(x, layer, g, w_na, w_qk, w_v, w_r, w_lr, w_g2, b_g):
    n, d = x.shape
    rows = IN_PROJ_ROWS
    params = (g, w_na, w_qk, w_v, w_r, w_lr, w_g2, b_g)
    out_specs, out_shape = _in_proj_outputs(n, rows)
    return pl.pallas_call(
        _in_proj_kernel,
        grid=(n // rows,),
        in_specs=[pl.BlockSpec((rows, d), lambda i: (i, 0))] + [_layer_spec(a, layer) for a in params],
        out_specs=out_specs,
        out_shape=out_shape,
        scratch_shapes=[pltpu.VMEM((d, 2 * GLA_KEY_WIDTH), BF16)],
        compiler_params=_params("arbitrary"),
        name="in_proj",
    )(x, *params)


def _na_bias_table(rpb, rows):
    kh = min(WIN_H_MAX, rows)
    w = jnp.arange(GRID_W)[:, None, None]
    x = jnp.arange(GRID_W)[None, :, None]
    cs = jnp.clip(w - WIN_W // 2, 0, GRID_W - WIN_W)
    valid = (x >= cs) & (x < cs + WIN_W)
    col_sel = (valid & (jnp.arange(2 * WIN_W - 1)[None, None, :] == x - w + (WIN_W - 1))).astype(F32)
    by_col = jnp.einsum("hrc,wxc->hwrx", rpb.astype(F32), col_sel, precision=lax.Precision.HIGHEST)
    by_col = jnp.where(valid[None, :, None, :, 0], by_col, MASK_VALUE)
    by_col = by_col.reshape(NA_HEADS * GRID_W, (2 * WIN_H_MAX - 1) * GRID_W)
    bias = jnp.stack([by_col[:, (WIN_H_MAX - 1 - c) * GRID_W:(WIN_H_MAX - 1 - c + kh) * GRID_W]
                      for c in range(kh)])
    return bias.reshape(kh, NA_HEADS // 2, 2 * GRID_W, kh * GRID_W)


def _na_window_start(j, rows, kh):
    return jnp.clip(j * NA_ROW_BLOCK - kh // 2, 0, rows - (NA_ROW_BLOCK + kh))


def _na_kernel(q_ref, k_ref, v_ref, tbl_ref, o_ref, *, rows, kh):
    rb = NA_ROW_BLOCK
    j = pl.program_id(1)
    fetched = _na_window_start(j, rows, kh)
    lane = lax.broadcasted_iota(jnp.int32, (GRID_W, LANES), 1)
    first = lane < NA_HEAD_DIM

    def row_body(lr, carry):
        r = j * rb + lr
        start = jnp.clip(r - kh // 2, 0, rows - kh)
        cls = r - start
        q_row = q_ref[lr]
        k_win = k_ref[0, pl.ds(start - fetched, kh)].reshape(kh * GRID_W, NA_WIDTH)
        v_win = v_ref[0, pl.ds(start - fetched, kh)].reshape(kh * GRID_W, NA_WIDTH)
        pairs = [slice(p * LANES, (p + 1) * LANES) for p in range(NA_HEADS // 2)]
        scores = []
        for sl in pairs:
            q_pair = q_row[:, sl]
            zero = jnp.zeros_like(q_pair)
            q_bd = jnp.concatenate([jnp.where(first, q_pair, zero), jnp.where(first, zero, q_pair)], axis=0)
            scores.append(lax.dot_general(q_bd, k_win[:, sl], (((1,), (1,)), ((), ())),
                                          preferred_element_type=F32))
        probs, denoms = [], []
        for p, s in enumerate(scores):
            s = s + tbl_ref[cls, p]
            e = jnp.exp(s - jnp.max(s, axis=-1, keepdims=True))
            denoms.append(jnp.sum(e, axis=-1, keepdims=True))
            probs.append(e.astype(BF16))
        outs = []
        for sl, e, denom in zip(pairs, probs, denoms):
            o = jnp.dot(e, v_win[:, sl], preferred_element_type=F32) / denom
            outs.append(jnp.where(first, o[:GRID_W], o[GRID_W:]))
        o_ref[lr] = jnp.concatenate(outs, axis=-1).astype(o_ref.dtype)
        return carry

    lax.fori_loop(0, rb, row_body, 0, unroll=NA_ROW_UNROLL)


def _na(na_qkv, tbl, layer, batch, rows):
    kh = min(WIN_H_MAX, rows)
    rb = NA_ROW_BLOCK
    nblk = rows // rb
    x4 = na_qkv.reshape(batch, rows, GRID_W, 3 * NA_WIDTH)
    assert rows >= rb + kh
    blk = (None, rb, GRID_W, NA_WIDTH)
    window = (pl.Element(1), pl.Element(rb + kh), pl.Element(GRID_W), pl.Element(NA_WIDTH))
    specs = [pl.BlockSpec(blk, lambda b, j: (b, j, 0, 0))]
    for col in (1, 2):
        specs.append(pl.BlockSpec(window, lambda b, j, col=col: (b, _na_window_start(j, rows, kh), 0,
                                                                  col * NA_WIDTH)))
    specs.append(_layer_spec(tbl, layer))
    out = pl.pallas_call(
        functools.partial(_na_kernel, rows=rows, kh=kh),
        grid=(batch, nblk),
        in_specs=specs,
        out_specs=pl.BlockSpec(blk, lambda b, j: (b, j, 0, 0)),
        out_shape=jax.ShapeDtypeStruct((batch, rows, GRID_W, NA_WIDTH), BF16),
        compiler_params=_params("arbitrary", "arbitrary"),
        name="na",
    )(x4, x4, x4, tbl)
    return out.reshape(batch * rows * GRID_W, NA_WIDTH)


def _block_diag_mask(row_block, col_block, nblocks):
    shape = (row_block * nblocks, col_block * nblocks)
    r = lax.shift_right_logical(lax.broadcasted_iota(jnp.int32, shape, 0), row_block.bit_length() - 1)
    c = lax.shift_right_logical(lax.broadcasted_iota(jnp.int32, shape, 1), col_block.bit_length() - 1)
    return r == c


def _split_bf16x3(x):
    hi = x.astype(BF16)
    rest = x - hi.astype(F32)
    mid = rest.astype(BF16)
    lo = (rest - mid.astype(F32)).astype(BF16)
    return hi, mid, lo


def _gla_direction(qk_ref, v_ref, g_ref, *, backward):
    c = GLA_CHUNK
    nc = GLA_STEP_CHUNKS
    kw = GLA_KEY_WIDTH
    order = list(reversed(range(nc))) if backward else list(range(nc))
    chunk = lambda a, n: a[n * c:(n + 1) * c]

    def stage_cumsum():
        step = nc * c
        ti = lax.broadcasted_iota(jnp.int32, (step, step), 0)
        tj = lax.broadcasted_iota(jnp.int32, (step, step), 1)
        same_chunk = lax.shift_right_logical(ti, c.bit_length() - 1) == lax.shift_right_logical(tj, c.bit_length() - 1)
        tri = (tj >= ti) if backward else (tj <= ti)
        cum = jnp.where(same_chunk & tri, 1.0, 0.0).astype(BF16)
        return sum(jnp.dot(cum, piece, preferred_element_type=F32) for piece in _split_bf16x3(g_ref[...]))

    def stage_decays(b, idx):
        n = order[idx]
        ref_row, last_row = (c // 2, 0) if backward else (c // 2 - 1, c - 1)
        b_n = chunk(b, n)
        b_ref = jnp.broadcast_to(b_n[ref_row:ref_row + 1, :], (c, kw))
        b_last = jnp.broadcast_to(b_n[last_row:last_row + 1, :], (c, kw))
        q = qk_ref[n * c:(n + 1) * c, :kw]
        k = qk_ref[n * c:(n + 1) * c, kw:]
        q_rel = (q * jnp.exp(b_n - b_ref)).astype(BF16)
        k_rel = (k * jnp.exp(b_ref - b_n)).astype(BF16)
        k_dec = k * jnp.exp(b_last - b_n)
        q_dec = (q * jnp.exp(b_n)).astype(BF16)
        decay = jnp.exp(b_last)
        return q_rel, k_rel, k_dec, q_dec, decay

    def stage_scores(q_rel, k_rel):
        kk_mask = _block_diag_mask(c, GLA_DK, GLA_HEADS)
        k_bd = jnp.where(kk_mask, jnp.concatenate([k_rel] * GLA_HEADS, axis=0), jnp.zeros((), BF16))
        return lax.dot_general(q_rel, k_bd, (((1,), (1,)), ((), ())), preferred_element_type=F32)

    def stage_intra(scores, k_dec, decay, idx):
        n = order[idx]
        si = lax.broadcasted_iota(jnp.int32, (c, c * GLA_HEADS), 0)
        sj = lax.broadcasted_iota(jnp.int32, (c, c * GLA_HEADS), 1) & (c - 1)
        keep = (sj > si) if backward else (sj <= si)
        kv_mask = _block_diag_mask(c, GLA_DV, GLA_HEADS)
        vk_mask = _block_diag_mask(GLA_DK, GLA_DV, 2)
        v = v_ref[n * c:(n + 1) * c, :]
        p = jnp.where(keep, scores, 0.0).astype(BF16)
        v_bd = jnp.where(kv_mask, jnp.concatenate([v] * GLA_HEADS, axis=0), jnp.zeros((), BF16))
        o_intra = jnp.dot(p, v_bd, preferred_element_type=F32)
        k_dec_t = k_dec.T.astype(BF16)
        upd = []
        for hp in range(GLA_HEADS // 2):
            kv = jnp.dot(k_dec_t[hp * 2 * GLA_DK:(hp + 1) * 2 * GLA_DK],
                         v[:, hp * 2 * GLA_DV:(hp + 1) * 2 * GLA_DV], preferred_element_type=F32)
            upd.append(jnp.where(vk_mask, kv, 0.0))
        return o_intra, upd, decay.T[:, :1]

    def scan_step(state, idx, q_dec, o_intra, upd, decay_col, out_ref):
        n = order[idx]
        pw = 2 * GLA_DK
        o_inter = jnp.concatenate(
            [jnp.dot(q_dec[:, hp * pw:(hp + 1) * pw], s.astype(BF16), preferred_element_type=F32)
             for hp, s in enumerate(state)], axis=1)
        out_ref[n * c:(n + 1) * c, :] = o_intra + o_inter
        return [s * decay_col[hp * pw:(hp + 1) * pw] + u for hp, (s, u) in enumerate(zip(state, upd))]

    return stage_cumsum, stage_decays, stage_scores, stage_intra, scan_step


def _gla_kernel(qk_f_ref, v_f_ref, g_f_ref, qk_b_ref, v_b_ref, g_b_ref, o_f_ref, o_b_ref, st):
    @pl.when(pl.program_id(0) == 0)
    def _():
        st[...] = jnp.zeros_like(st)

    batch = qk_f_ref.shape[0]
    streams, outs = [], []
    for bi in range(batch):
        streams.append(_gla_direction(qk_f_ref.at[bi], v_f_ref.at[bi], g_f_ref.at[bi], backward=False))
        streams.append(_gla_direction(qk_b_ref.at[bi], v_b_ref.at[bi], g_b_ref.at[bi], backward=True))
        outs += [o_f_ref.at[bi], o_b_ref.at[bi]]
    b = [s[0]() for s in streams]
    pairs = range(GLA_HEADS // 2)
    states = [[st[si, hp] for hp in pairs] for si in range(len(streams))]
    for idx in range(GLA_STEP_CHUNKS):
        dec = [s[1](x, idx) for s, x in zip(streams, b)]
        scores = [s[2](x[0], x[1]) for s, x in zip(streams, dec)]
        intra = [s[3](sc, x[2], x[4], idx) for s, sc, x in zip(streams, scores, dec)]
        states = [s[4](state, idx, x[3], *y, out_ref)
                  for s, state, x, y, out_ref in zip(streams, states, dec, intra, outs)]
    for si, state in enumerate(states):
        for hp in pairs:
            st[si, hp] = state[hp]


def _gla(qk, v, gates, batch, seq):
    step = GLA_STEP_CHUNKS * GLA_CHUNK
    nblk = seq // step
    qk3 = qk.reshape(batch, seq, 2 * GLA_KEY_WIDTH)
    v3 = v.reshape(batch, seq, GLA_VAL_WIDTH)
    g3 = gates.reshape(batch, seq, 2 * GLA_KEY_WIDTH)
    fwd = lambda n: (0, n, 0)
    bwd = lambda n: (0, nblk - 1 - n, 0)
    bwd_gate = lambda n: (0, nblk - 1 - n, 1)
    o_f, o_b = pl.pallas_call(
        _gla_kernel,
        grid=(nblk,),
        in_specs=[pl.BlockSpec((batch, step, 2 * GLA_KEY_WIDTH), fwd),
                  pl.BlockSpec((batch, step, GLA_VAL_WIDTH), fwd),
                  pl.BlockSpec((batch, step, GLA_KEY_WIDTH), fwd),
                  pl.BlockSpec((batch, step, 2 * GLA_KEY_WIDTH), bwd),
                  pl.BlockSpec((batch, step, GLA_VAL_WIDTH), bwd),
                  pl.BlockSpec((batch, step, GLA_KEY_WIDTH), bwd_gate)],
        out_specs=[pl.BlockSpec((batch, step, GLA_VAL_WIDTH), fwd),
                   pl.BlockSpec((batch, step, GLA_VAL_WIDTH), bwd)],
        out_shape=[jax.ShapeDtypeStruct((batch, seq, GLA_VAL_WIDTH), F32)] * 2,
        scratch_shapes=[pltpu.VMEM((2 * batch, GLA_HEADS // 2, 2 * GLA_DK, 2 * GLA_DV), F32)],
        compiler_params=_params("arbitrary"),
        name="gla",
    )(qk3, v3, g3, qk3, v3, g3)
    return o_f.reshape(batch * seq, GLA_VAL_WIDTH), o_b.reshape(batch * seq, GLA_VAL_WIDTH)


def _store_slabs(ref, x):
    rows = x.shape[0]
    for s in range(SUBLANES):
        ref[pl.ds(s, rows, stride=SUBLANES), :] = x[:, s * LANES:(s + 1) * LANES]


def _load_slabs(ref, rows):
    return jnp.concatenate([ref[pl.ds(s, rows, stride=SUBLANES), :] for s in range(SUBLANES)], axis=1)


def _slab_copy(src_hbm, dst, sem, src_row, k):
    dst_row = k * SUBLANES if isinstance(k, int) else pl.multiple_of(k * SUBLANES, SUBLANES)
    return pltpu.make_async_copy(src_hbm.at[pl.ds(pl.multiple_of(src_row, SUBLANES), SUBLANES), :],
                                 dst.at[pl.ds(dst_row, SUBLANES), :], sem)


def _wait_slabs(src_hbm, dst, sem):
    pltpu.make_async_copy(src_hbm.at[pl.ds(0, dst.shape[0]), :], dst, sem).wait()


RT_E1, RT_E2, RT_RANK1, RT_RANK2, RT_W1, RT_W2 = range(6)
GROUP_ROW0 = N_EXPERTS
ROUTE_ROWS = 64


def _out_route_kernel(na_ref, of_ref, ob_ref, r_ref, x_ref, w_na_ref, w_gla_ref, gn_ref, fn_ref,
                      w_rt_ref, b_rt_ref, x1_ref, h2_ref, rt_ref, cnt_ref, carry):
    @pl.when(pl.program_id(0) == 0)
    def _():
        carry[...] = jnp.zeros_like(carry)

    rows = x_ref.shape[0]
    o = of_ref[...] + ob_ref[...]
    r = r_ref[...]
    parts = []
    for h in range(GLA_HEADS):
        sl = slice(h * GLA_DV, (h + 1) * GLA_DV)
        parts.append(_rms(o[:, sl], gn_ref[...]) * (r[:, sl] * jax.nn.sigmoid(r[:, sl])))
    y_gla = jnp.concatenate(parts, axis=-1).astype(BF16)
    x1 = (x_ref[...] + jnp.dot(na_ref[...], w_na_ref[...], preferred_element_type=F32)
          + jnp.dot(y_gla, w_gla_ref[...], preferred_element_type=F32))
    x1_ref[...] = x1
    h2 = _rms(x1, fn_ref[...])
    _store_slabs(h2_ref, h2)

    h_hi = h2.astype(BF16)
    h_lo = (h2 - h_hi.astype(F32)).astype(BF16)
    logits = lax.dot_general(w_rt_ref[...], jnp.concatenate([h_hi, h_lo, h_hi], axis=1),
                             (((1,), (1,)), ((), ())), preferred_element_type=F32)
    logits = logits[:ROUTE_ROWS] + b_rt_ref[...]
    row_i = lax.broadcasted_iota(jnp.int32, (ROUTE_ROWS, rows), 0)
    row = row_i.astype(F32)
    row_grp = lax.shift_right_logical(row_i, EXPERTS_PER_GROUP.bit_length() - 1).astype(F32)
    neg = jnp.float32(-jnp.inf)
    no_row = jnp.float32(ROUTE_ROWS)
    is_grp = (row_i >= GROUP_ROW0) & (row_i < GROUP_ROW0 + N_GROUPS)
    g_logit = jnp.where(is_grp, logits, neg)
    g_max = jnp.max(g_logit, axis=0, keepdims=True)
    g_sel = jnp.min(jnp.where(is_grp & (g_logit == g_max), row, no_row), axis=0, keepdims=True) - GROUP_ROW0
    grp_w = 1.0 / jnp.sum(jnp.where(is_grp, jnp.exp(g_logit - g_max), 0.0), axis=0, keepdims=True)
    in_grp = (row_i < N_EXPERTS) & (row_grp == g_sel)
    e_logit = jnp.where(in_grp, logits, neg)
    v1 = jnp.max(e_logit, axis=0, keepdims=True)
    i1 = jnp.min(jnp.where(in_grp & (e_logit == v1), row, no_row), axis=0, keepdims=True)
    rest = in_grp & (row != i1)
    e_logit2 = jnp.where(rest, logits, neg)
    v2 = jnp.max(e_logit2, axis=0, keepdims=True)
    i2 = jnp.min(jnp.where(rest & (e_logit2 == v2), row, no_row), axis=0, keepdims=True)
    t = jnp.exp(v2 - v1)
    w1 = grp_w / (1.0 + t)
    w2 = grp_w * t / (1.0 + t)

    sel1 = row == i1
    sel2 = row == i2
    onehot = jnp.where(sel1 | sel2, 1.0, 0.0)
    ti = lax.broadcasted_iota(jnp.int32, (rows, rows), 0)
    tj = lax.broadcasted_iota(jnp.int32, (rows, rows), 1)
    earlier = jnp.where(ti < tj, 1.0, 0.0).astype(BF16)
    ranks = jnp.dot(onehot.astype(BF16), earlier, preferred_element_type=F32) + carry[:, :1]
    rank1 = jnp.sum(jnp.where(sel1, ranks, 0.0), axis=0, keepdims=True)
    rank2 = jnp.sum(jnp.where(sel2, ranks, 0.0), axis=0, keepdims=True)
    new_carry = carry[...] + jnp.sum(onehot, axis=1, keepdims=True)
    carry[...] = new_carry
    cnt_ref[...] = new_carry

    fields = {RT_E1: i1, RT_E2: i2, RT_RANK1: rank1, RT_RANK2: rank2, RT_W1: w1, RT_W2: w2}
    zero = jnp.zeros_like(w1)
    rt_ref[...] = jnp.concatenate([fields.get(f, zero) for f in range(SUBLANES)], axis=0)


def _out_route(y_na, o_f, o_b, r, x, layer, w_na, w_gla, gn, fn, w_rt, b_rt):
    n, d = x.shape
    rows = OUT_ROWS
    row_spec = lambda width: pl.BlockSpec((rows, width), lambda i: (i, 0))
    full = lambda a: _layer_spec(a, layer)
    return pl.pallas_call(
        _out_route_kernel,
        grid=(n // rows,),
        in_specs=[row_spec(NA_WIDTH), row_spec(GLA_VAL_WIDTH), row_spec(GLA_VAL_WIDTH),
                  row_spec(GLA_VAL_WIDTH), row_spec(d), full(w_na), full(w_gla), full(gn), full(fn),
                  full(w_rt), full(b_rt)],
        out_specs=[row_spec(d), pl.BlockSpec((rows * SUBLANES, LANES), lambda i: (i, 0)),
                   pl.BlockSpec((SUBLANES, rows), lambda i: (0, i)),
                   pl.BlockSpec((ROUTE_ROWS, LANES), lambda i: (0, 0))],
        out_shape=[jax.ShapeDtypeStruct((n, d), F32), jax.ShapeDtypeStruct((n * SUBLANES, LANES), F32),
                   jax.ShapeDtypeStruct((SUBLANES, n), F32), jax.ShapeDtypeStruct((ROUTE_ROWS, LANES), F32)],
        scratch_shapes=[pltpu.VMEM((ROUTE_ROWS, LANES), F32)],
        compiler_params=_params("arbitrary"),
        name="out_route",
    )(y_na, o_f, o_b, r, x, w_na, w_gla, gn, fn, w_rt, b_rt)


def _dispatch_kernel(pad_tile_ref, n_tiles_ref, rt_ref, offs_ref, h_hbm, pos_ref, rec_ref, xs_hbm,
                     zeros, hbuf, pos_vmem, pos_smem, sem_zero, sem_pos, sem_in, sem):
    i = pl.program_id(0)
    n_steps = pl.num_programs(0)
    db = DISPATCH_ROWS
    tile = EXPERT_ROWS * SUBLANES
    max_tiles = xs_hbm.shape[0] // tile
    buf = lax.rem(i, 2)

    def block_copy(step, b):
        rows = pl.ds(pl.multiple_of(step * db * SUBLANES, db * SUBLANES), db * SUBLANES)
        return pltpu.make_async_copy(h_hbm.at[rows, :], hbuf.at[b], sem_in.at[b])

    def wait_row_copies(b):
        for _ in range(2):
            pltpu.make_async_copy(hbuf.at[b], xs_hbm.at[pl.ds(0, db * SUBLANES), :], sem).wait()

    def zero_copy(t):
        return pltpu.make_async_copy(zeros, xs_hbm.at[pl.ds(pl.multiple_of(t * tile, tile), tile), :], sem_zero)

    def for_each_zero_tile(fn):
        def padded(e, carry):
            @pl.when(pad_tile_ref[e] >= 0)
            def _():
                fn(zero_copy(pad_tile_ref[e]))
            return carry
        lax.fori_loop(0, N_EXPERTS, padded, 0)

        def tail(t, carry):
            fn(zero_copy(t))
            return carry
        lax.fori_loop(n_tiles_ref[0], max_tiles, tail, 0)

    @pl.when(i == 0)
    def _():
        block_copy(0, 0).start()
        zeros[...] = jnp.zeros_like(zeros)
        for_each_zero_tile(lambda cp: cp.start())

    rt_t = rt_ref[...]
    rec_ref[...] = jnp.concatenate([rt_t, jnp.zeros((LANES - SUBLANES, db), F32)], axis=0).T
    expert = lax.broadcasted_iota(jnp.int32, (N_EXPERTS, db), 0).astype(F32)
    pos = []
    for e_lane, rank_lane in ((RT_E1, RT_RANK1), (RT_E2, RT_RANK2)):
        seg = jnp.sum(jnp.where(expert == rt_t[e_lane:e_lane + 1, :], offs_ref[...], 0.0), axis=0, keepdims=True)
        pos.append((seg + rt_t[rank_lane:rank_lane + 1, :]) * SUBLANES)
    pos = jnp.concatenate(pos, axis=0).astype(jnp.int32)
    pos_ref[...] = pos
    pos_vmem[...] = jnp.concatenate([pos, jnp.zeros((SUBLANES - 2, db), jnp.int32)], axis=0)
    to_smem = pltpu.make_async_copy(pos_vmem, pos_smem, sem_pos)
    to_smem.start()

    @pl.when(i == 0)
    def _():
        for_each_zero_tile(lambda cp: cp.wait())

    @pl.when(i > 0)
    def _():
        wait_row_copies(1 - buf)

    to_smem.wait()
    block_copy(i, buf).wait()

    @pl.when(i + 1 < n_steps)
    def _():
        block_copy(i + 1, 1 - buf).start()

    h_blk = hbuf.at[buf]
    for k in range(db):
        src = h_blk.at[pl.ds(k * SUBLANES, SUBLANES), :]
        for slot in range(2):
            dst_row = pl.multiple_of(pos_smem[slot, k], SUBLANES)
            pltpu.make_async_copy(src, xs_hbm.at[pl.ds(dst_row, SUBLANES), :], sem).start(priority=slot)

    @pl.when(i == n_steps - 1)
    def _():
        wait_row_copies(buf)


def _dispatch(pad_tile, n_tiles, rt_t, offs_b, h2_slabs, sorted_rows):
    n = rt_t.shape[1]
    db = DISPATCH_ROWS
    grid_spec = pltpu.PrefetchScalarGridSpec(
        num_scalar_prefetch=2,
        grid=(n // db,),
        in_specs=[pl.BlockSpec((SUBLANES, db), lambda i, pt, nt: (0, i)),
                  pl.BlockSpec(offs_b.shape, lambda i, pt, nt: (0, 0)),
                  pl.BlockSpec(memory_space=pl.ANY)],
        out_specs=[pl.BlockSpec((None, 2, db), lambda i, pt, nt: (i, 0, 0)),
                   pl.BlockSpec((db, LANES), lambda i, pt, nt: (i, 0)),
                   pl.BlockSpec(memory_space=pl.ANY)],
        scratch_shapes=[pltpu.VMEM((EXPERT_ROWS * SUBLANES, LANES), F32),
                        pltpu.VMEM((2, db * SUBLANES, LANES), F32),
                        pltpu.VMEM((SUBLANES, db), jnp.int32), pltpu.SMEM((SUBLANES, db), jnp.int32),
                        pltpu.SemaphoreType.DMA(()), pltpu.SemaphoreType.DMA(()),
                        pltpu.SemaphoreType.DMA((2,)), pltpu.SemaphoreType.DMA(())],
    )
    return pl.pallas_call(
        _dispatch_kernel,
        grid_spec=grid_spec,
        out_shape=[jax.ShapeDtypeStruct((n // db, 2, db), jnp.int32),
                   jax.ShapeDtypeStruct((n, LANES), F32),
                   jax.ShapeDtypeStruct((sorted_rows * SUBLANES, LANES), F32)],
        compiler_params=_params("arbitrary"),
        name="dispatch",
    )(pad_tile, n_tiles, rt_t, offs_b, h2_slabs)


TILE_DMA_PRIORITY = 1


def _expert_tile(x_ref, y_ref, wg_s, wu_s, wd_s):
    rows = EXPERT_ROWS
    x = _load_slabs(x_ref, rows).astype(BF16)
    hid = []
    for c in range(0, wg_s.shape[1], MXU_WIDTH):
        gate = jnp.dot(x, wg_s[:, c:c + MXU_WIDTH], preferred_element_type=F32)
        up = jnp.dot(x, wu_s[:, c:c + MXU_WIDTH], preferred_element_type=F32)
        hid.append((gate * jax.nn.sigmoid(gate) * up).astype(BF16))
    hid = jnp.concatenate(hid, axis=1)
    for c in range(0, wd_s.shape[1], MXU_WIDTH):
        y = jnp.dot(hid, wd_s[:, c:c + MXU_WIDTH], preferred_element_type=F32)
        for s in range(c // LANES, (c + MXU_WIDTH) // LANES):
            y_ref[pl.ds(s, rows, stride=SUBLANES), :] = y[:, s * LANES - c:(s + 1) * LANES - c]


def _experts_kernel(tile_start_ref, tiles_ref, x_hbm, wg_hbm, wu_hbm, wd_hbm, y_hbm,
                    xbuf, ybuf, wg_f, wu_f, wd_f, wg_s, wu_s, wd_s, sem_x, sem_y, sem_w, *, layer):
    nbuf = EXPERT_TILE_BUFFERS
    e = pl.program_id(0)
    last_step = e == pl.num_programs(0) - 1
    tile = EXPERT_ROWS * SUBLANES
    n = tiles_ref[e]
    base = tile_start_ref[e]
    n_tiles = tile_start_ref[N_EXPERTS - 1] + tiles_ref[N_EXPERTS - 1]
    tile_rows = lambda g: pl.ds(pl.multiple_of(g * tile, tile), tile)

    def x_copy(g, slot):
        return pltpu.make_async_copy(x_hbm.at[tile_rows(g), :], xbuf.at[slot], sem_x.at[slot])

    def y_copy(g, slot):
        return pltpu.make_async_copy(ybuf.at[slot], y_hbm.at[tile_rows(g), :], sem_y.at[slot])

    def w_copies(expert, slot):
        return [pltpu.make_async_copy(w_hbm.at[layer, expert], stage.at[slot], sem_w.at[slot])
                for w_hbm, stage in ((wg_hbm, wg_f), (wu_hbm, wu_f), (wd_hbm, wd_f))]

    w_slot = lax.rem(e, 2)

    @pl.when(e == 0)
    def _():
        for g in range(nbuf - 1):
            @pl.when(g < n_tiles)
            def _():
                x_copy(g, g).start(priority=TILE_DMA_PRIORITY)
        for cp in w_copies(0, 0):
            cp.start()

    for cp in w_copies(e, w_slot):
        cp.wait()

    @pl.when(jnp.logical_not(last_step))
    def _():
        for cp in w_copies(e + 1, 1 - w_slot):
            cp.start()

    @pl.when(n > 0)
    def _():
        wg_s[...] = wg_f[w_slot].astype(BF16)
        wu_s[...] = wu_f[w_slot].astype(BF16)
        wd_s[...] = wd_f[w_slot].astype(BF16)

        def tile_body(g, carry):
            slot = lax.rem(g, nbuf)
            x_copy(g, slot).wait()

            @pl.when(g + nbuf - 1 < n_tiles)
            def _():
                x_copy(g + nbuf - 1, lax.rem(g + nbuf - 1, nbuf)).start(priority=TILE_DMA_PRIORITY)

            @pl.when(g >= nbuf)
            def _():
                y_copy(g - nbuf, slot).wait()

            _expert_tile(xbuf.at[slot], ybuf.at[slot], wg_s, wu_s, wd_s)
            y_copy(g, slot).start(priority=TILE_DMA_PRIORITY)
            return carry
        lax.fori_loop(base, base + n, tile_body, 0)

    @pl.when(last_step)
    def _():
        for back in range(nbuf, 0, -1):
            @pl.when(n_tiles >= back)
            def _():
                y_copy(n_tiles - back, lax.rem(n_tiles - back, nbuf)).wait()

        first_unused = n_tiles
        max_tiles = y_hbm.shape[0] // tile
        ybuf[0] = jnp.zeros(ybuf.shape[1:], ybuf.dtype)
        zero_copy = lambda t: pltpu.make_async_copy(
            ybuf.at[0], y_hbm.at[pl.ds(pl.multiple_of(t * tile, tile), tile), :], sem_y.at[0])

        def start(t, carry):
            zero_copy(t).start()
            return carry

        def wait(t, carry):
            zero_copy(t).wait()
            return carry
        lax.fori_loop(first_unused, max_tiles, start, 0)
        lax.fori_loop(first_unused, max_tiles, wait, 0)


def _experts(tile_start, tiles, xs, wg, wu, wd, layer):
    rows = EXPERT_ROWS
    _, n_experts, d, dff = wg.shape
    assert d == SUBLANES * LANES and n_experts == N_EXPERTS
    nbuf = EXPERT_TILE_BUFFERS
    any_spec = pl.BlockSpec(memory_space=pl.ANY)
    grid_spec = pltpu.PrefetchScalarGridSpec(
        num_scalar_prefetch=2,
        grid=(n_experts,),
        in_specs=[any_spec, any_spec, any_spec, any_spec],
        out_specs=any_spec,
        scratch_shapes=[pltpu.VMEM((nbuf, rows * SUBLANES, LANES), F32),
                        pltpu.VMEM((nbuf, rows * SUBLANES, LANES), F32),
                        pltpu.VMEM((2, d, dff), F32), pltpu.VMEM((2, d, dff), F32), pltpu.VMEM((2, dff, d), F32),
                        pltpu.VMEM((d, dff), BF16), pltpu.VMEM((d, dff), BF16), pltpu.VMEM((dff, d), BF16),
                        pltpu.SemaphoreType.DMA((nbuf,)), pltpu.SemaphoreType.DMA((nbuf,)),
                        pltpu.SemaphoreType.DMA((2,))],
    )
    return pl.pallas_call(
        functools.partial(_experts_kernel, layer=layer),
        grid_spec=grid_spec,
        out_shape=jax.ShapeDtypeStruct(xs.shape, F32),
        compiler_params=_params("arbitrary"),
        name="experts",
    )(tile_start, tiles, xs, wg, wu, wd)


def _start_slab_gather(src_hbm, dst, sem, row_ref, base, count, priority):
    for k in range(count):
        _slab_copy(src_hbm, dst, sem, row_ref[base + k], k).start(priority=priority)


def _combined_rows(pos_ref, ys_hbm, x1_ref, rt_ref, buf, sem):
    i = pl.program_id(0)
    n_steps = pl.num_programs(0)
    rows = x1_ref.shape[0]
    steps_per_block = COMBINE_ROWS // rows
    slot = i % 2

    def start(step, s):
        first = lax.div(step, steps_per_block) * (2 * COMBINE_ROWS) + lax.rem(step, steps_per_block) * rows
        for j in range(2):
            _start_slab_gather(ys_hbm, buf.at[s, j], sem.at[s], pos_ref, first + j * COMBINE_ROWS, rows, priority=j)

    @pl.when(i == 0)
    def _():
        start(0, 0)

    @pl.when(i + 1 < n_steps)
    def _():
        start(i + 1, 1 - slot)

    _wait_slabs(ys_hbm, buf.at[slot, 0], sem.at[slot])
    _wait_slabs(ys_hbm, buf.at[slot, 1], sem.at[slot])
    rt = rt_ref[...]
    lane = lax.broadcasted_iota(jnp.int32, rt.shape, 1)
    w1 = jnp.sum(jnp.where(lane == RT_W1, rt, 0.0), axis=-1, keepdims=True)
    w2 = jnp.sum(jnp.where(lane == RT_W2, rt, 0.0), axis=-1, keepdims=True)
    y = w1 * _load_slabs(buf.at[slot, 0], rows) + w2 * _load_slabs(buf.at[slot, 1], rows)
    return x1_ref[...] + y


def _combine_scratch(rows):
    return [pltpu.VMEM((2, 2, rows * SUBLANES, LANES), F32), pltpu.SemaphoreType.DMA((2,))]


def _combine_final_kernel(pos_ref, ys_hbm, x1_ref, rt_ref, g_ref, o_ref, buf, sem):
    o_ref[...] = _rms(_combined_rows(pos_ref, ys_hbm, x1_ref, rt_ref, buf, sem), g_ref[...])


def _combine_final(pos, ys, x1, rt, g):
    n, d = x1.shape
    rows = COMBINE_STEP_ROWS
    grid_spec = pltpu.PrefetchScalarGridSpec(
        num_scalar_prefetch=1,
        grid=(n // rows,),
        in_specs=[pl.BlockSpec(memory_space=pl.ANY),
                  pl.BlockSpec((rows, d), lambda i, pos: (i, 0)),
                  pl.BlockSpec((rows, LANES), lambda i, pos: (i, 0)),
                  pl.BlockSpec(g.shape, lambda i, pos: (0, 0))],
        out_specs=pl.BlockSpec((rows, d), lambda i, pos: (i, 0)),
        scratch_shapes=_combine_scratch(rows),
    )
    return pl.pallas_call(
        _combine_final_kernel,
        grid_spec=grid_spec,
        out_shape=jax.ShapeDtypeStruct((n, d), F32),
        compiler_params=_params("arbitrary"),
        name="combine",
    )(pos, ys, x1, rt, g)


def _combine_in_proj_kernel(pos_ref, ys_hbm, x1_ref, rt_ref, *refs):
    param_refs, (x_ref, *proj_refs), (w_gate, buf, sem) = refs[:8], refs[8:14], refs[14:]
    x = _combined_rows(pos_ref, ys_hbm, x1_ref, rt_ref, buf, sem)
    x_ref[...] = x
    _in_proj_rows(x, *param_refs, *proj_refs, w_gate)


def _combine_in_proj(pos, ys, x1, rt, layer, g, w_na, w_qk, w_v, w_r, w_lr, w_g2, b_g):
    n, d = x1.shape
    rows = COMBINE_STEP_ROWS
    params = (g, w_na, w_qk, w_v, w_r, w_lr, w_g2, b_g)
    proj_specs, proj_shapes = _in_proj_outputs(n, rows)
    grid_spec = pltpu.PrefetchScalarGridSpec(
        num_scalar_prefetch=1,
        grid=(n // rows,),
        in_specs=[pl.BlockSpec(memory_space=pl.ANY),
                  pl.BlockSpec((rows, d), lambda i, pos: (i, 0)),
                  pl.BlockSpec((rows, LANES), lambda i, pos: (i, 0))] + [_layer_spec(a, layer) for a in params],
        out_specs=[pl.BlockSpec((rows, d), lambda i, pos: (i, 0))] + proj_specs,
        scratch_shapes=[pltpu.VMEM((d, 2 * GLA_KEY_WIDTH), BF16)] + _combine_scratch(rows),
    )
    x, *proj = pl.pallas_call(
        _combine_in_proj_kernel,
        grid_spec=grid_spec,
        out_shape=[jax.ShapeDtypeStruct((n, d), F32)] + proj_shapes,
        compiler_params=_params("arbitrary"),
        name="combine_in_proj",
    )(pos, ys, x1, rt, *params)
    return x, proj


def _dispatch_plan(counts, n):
    rows = EXPERT_ROWS
    max_tiles = (2 * n) // rows + N_EXPERTS
    cnt = counts[:N_EXPERTS, 0].astype(jnp.int32)
    tiles = (cnt + rows - 1) // rows
    tile_end = jnp.cumsum(tiles)
    tile_start = tile_end - tiles
    n_tiles = tile_end[-1:]
    pad_tile = jnp.where(cnt % rows != 0, tile_end - 1, -1).astype(jnp.int32)
    offs_b = jnp.broadcast_to((tile_start * rows).astype(F32)[:, None], (N_EXPERTS, DISPATCH_ROWS))
    return tile_start.astype(jnp.int32), tiles, n_tiles, pad_tile, offs_b, max_tiles * rows


def kernel(x, norm_mix_g, w_in, w_g2_f, b_g_f, w_g2_b, b_g_b, gla_norm_g, rpb, w_out, norm_ffn_g, w_grp, b_grp,
           w_exp, b_exp, w_gate, w_up, w_down, final_norm_g):
    batch, seq, d = x.shape
    n = batch * seq
    rows = seq // GRID_W
    depth = w_in.shape[0]
    xf = x.reshape(n, d)
    c_na = 3 * NA_WIDTH
    c_qk = c_na + 2 * GLA_KEY_WIDTH
    c_v = c_qk + GLA_VAL_WIDTH
    c_r = c_v + GLA_VAL_WIDTH

    q_scale = jnp.concatenate([jnp.full((NA_WIDTH,), NA_HEAD_DIM ** -0.5, F32), jnp.ones((2 * NA_WIDTH,), F32)])
    w_na = (w_in[:, :, :c_na] * q_scale).astype(BF16)
    qk_scale = jnp.concatenate([jnp.full((GLA_KEY_WIDTH,), GLA_DK ** -0.5, F32), jnp.ones((GLA_KEY_WIDTH,), F32)])
    w_qk = (w_in[:, :, c_na:c_qk] * qk_scale).astype(BF16)
    w_v = w_in[:, :, c_qk:c_v].astype(BF16)
    w_r = w_in[:, :, c_v:c_r].astype(BF16)
    w_lr = jnp.pad(w_in[:, :, c_r:], ((0, 0), (0, 0), (0, LANES - 2 * GLA_GATE_RANK)))
    zero_g2 = jnp.zeros_like(w_g2_f)
    w_g2 = jnp.concatenate([jnp.concatenate([w_g2_f, zero_g2], axis=2),
                            jnp.concatenate([zero_g2, w_g2_b], axis=2)], axis=1)
    w_g2 = jnp.pad(w_g2, ((0, 0), (0, LANES - 2 * GLA_GATE_RANK), (0, 0)))
    b_g = jnp.concatenate([b_g_f, b_g_b], axis=1)[:, None, :]
    bias_tbl = jax.vmap(functools.partial(_na_bias_table, rows=rows))(rpb)
    w_o = w_out.astype(BF16)
    w_o_na, w_o_gla = w_o[:, :NA_WIDTH], w_o[:, NA_WIDTH:]
    w_rt = jnp.pad(jnp.concatenate([w_exp, w_grp], axis=2), ((0, 0), (0, 0), (0, LANES - N_EXPERTS - N_GROUPS)))
    w_rt_hi = w_rt.astype(BF16)
    w_rt_lo = (w_rt - w_rt_hi.astype(F32)).astype(BF16)
    w_rt3 = jnp.concatenate([w_rt_hi, w_rt_hi, w_rt_lo], axis=1).transpose(0, 2, 1)
    b_rt = jnp.pad(jnp.concatenate([b_exp, b_grp], axis=1),
                   ((0, 0), (0, ROUTE_ROWS - N_EXPERTS - N_GROUPS)))[:, :, None]
    per_layer_row = lambda a: a[:, None, :]

    in_proj_params = (per_layer_row(norm_mix_g), w_na, w_qk, w_v, w_r, w_lr, w_g2, b_g)
    proj = _in_proj(xf, 0, *in_proj_params)
    for l in range(depth):
        na_qkv, gqk, gv, gr, gates = proj
        y_na = _na(na_qkv, bias_tbl, l, batch, rows)
        o_f, o_b = _gla(gqk, gv, gates, batch, seq)
        x1, h2_slabs, rt_t, counts = _out_route(y_na, o_f, o_b, gr, xf, l, w_o_na, w_o_gla,
                                                per_layer_row(gla_norm_g), per_layer_row(norm_ffn_g), w_rt3, b_rt)

        tile_start, tiles, n_tiles, pad_tile, offs_b, sorted_rows = _dispatch_plan(counts, n)
        pos, rt, xs = _dispatch(pad_tile, n_tiles, rt_t, offs_b, h2_slabs, sorted_rows)
        ys = _experts(tile_start, tiles, xs, w_gate, w_up, w_down, l)
        pos = pos.reshape(-1)
        if l + 1 < depth:
            xf, proj = _combine_in_proj(pos, ys, x1, rt, l + 1, *in_proj_params)
    return _combine_final(pos, ys, x1, rt, final_norm_g[None, :]).reshape(batch, seq, d)
```

```python
import functools

import jax
import jax.numpy as jnp
from jax import lax
from jax.experimental import pallas as pl
from jax.experimental.pallas import tpu as pltpu

F32 = jnp.float32
BF16 = jnp.bfloat16

GRID_W = 64
NA_HEADS = 8
NA_HEAD_DIM = 64
NA_WIDTH = NA_HEADS * NA_HEAD_DIM
WIN_H_MAX = 8
WIN_W = 16
GLA_HEADS = 4
GLA_DK = 64
GLA_DV = 128
GLA_KEY_WIDTH = GLA_HEADS * GLA_DK
GLA_VAL_WIDTH = GLA_HEADS * GLA_DV
GLA_GATE_RANK = 16
GLA_GATE_NORMALIZER = 16.0
GLA_CHUNK = 64
N_GROUPS = 4
EXPERTS_PER_GROUP = 8
N_EXPERTS = N_GROUPS * EXPERTS_PER_GROUP
RMS_EPS = 1e-6

LANES = 128
SUBLANES = 8
MXU_WIDTH = 256
VMEM_LIMIT_BYTES = 56 * 1024 * 1024

MASK_VALUE = -1e30

IN_PROJ_ROWS = 512
NA_ROW_BLOCK = 16
NA_ROW_UNROLL = 8
GLA_STEP_CHUNKS = 4
OUT_ROWS = 512
EXPERT_ROWS = 256
EXPERT_TILE_BUFFERS = 8
DISPATCH_ROWS = 1024
COMBINE_ROWS = DISPATCH_ROWS
COMBINE_STEP_ROWS = 512


def _params(*sem):
    return pltpu.CompilerParams(dimension_semantics=sem, vmem_limit_bytes=VMEM_LIMIT_BYTES)


def _rms(x, g):
    return x * lax.rsqrt(jnp.mean(x * x, axis=-1, keepdims=True) + RMS_EPS) * g


def _in_proj_rows(x, g_ref, w_na_ref, w_qk_ref, w_v_ref, w_r_ref, w_lr_ref,
                  w_g2_ref, b_g_ref, na_ref, qk_ref, v_ref, r_ref, gate_ref, w_gate):
    @pl.when(pl.program_id(0) == 0)
    def _():
        w_gate[...] = jnp.dot(w_lr_ref[...], w_g2_ref[...], preferred_element_type=F32,
                              precision=lax.Precision.HIGHEST).astype(BF16)

    h = _rms(x, g_ref[...]).astype(BF16)
    na_ref[...] = jnp.dot(h, w_na_ref[...], preferred_element_type=F32).astype(BF16)
    qk_ref[...] = jnp.dot(h, w_qk_ref[...], preferred_element_type=F32)
    v_ref[...] = jnp.dot(h, w_v_ref[...], preferred_element_type=F32).astype(BF16)
    r_ref[...] = jnp.dot(h, w_r_ref[...], preferred_element_type=F32)
    z = jnp.dot(h, w_gate[...], preferred_element_type=F32) + b_g_ref[...]
    log_sig = jnp.minimum(z, 0.0) - jnp.log(1.0 + jnp.exp(-jnp.abs(z)))
    gate_ref[...] = log_sig * (1.0 / GLA_GATE_NORMALIZER)


def _in_proj_kernel(x_ref, *refs):
    _in_proj_rows(x_ref[...], *refs)


def _in_proj_outputs(n, rows):
    row_spec = lambda width: pl.BlockSpec((rows, width), lambda i, *_: (i, 0))
    widths = (3 * NA_WIDTH, 2 * GLA_KEY_WIDTH, GLA_VAL_WIDTH, GLA_VAL_WIDTH, 2 * GLA_KEY_WIDTH)
    dtypes = (BF16, F32, BF16, F32, F32)
    return ([row_spec(w) for w in widths],
            [jax.ShapeDtypeStruct((n, w), t) for w, t in zip(widths, dtypes)])


def _layer_spec(a, layer):
    return pl.BlockSpec((None,) + a.shape[1:], lambda *_: (layer,) + (0,) * (a.ndim - 1))


def _in_proj(x, layer, g, w_na, w_qk, w_v, w_r, w_lr, w_g2, b_g):
    n, d = x.shape
    rows = IN_PROJ_ROWS
    params = (g, w_na, w_qk, w_v, w_r, w_lr, w_g2, b_g)
    out_specs, out_shape = _in_proj_outputs(n, rows)
    return pl.pallas_call(
        _in_proj_kernel,
        grid=(n // rows,),
        in_specs=[pl.BlockSpec((rows, d), lambda i: (i, 0))] + [_layer_spec(a, layer) for a in params],
        out_specs=out_specs,
        out_shape=out_shape,
        scratch_shapes=[pltpu.VMEM((d, 2 * GLA_KEY_WIDTH), BF16)],
        compiler_params=_params("arbitrary"),
        name="in_proj",
    )(x, *params)


def _na_bias_table(rpb, rows):
    kh = min(WIN_H_MAX, rows)
    w = jnp.arange(GRID_W)[:, None, None]
    x = jnp.arange(GRID_W)[None, :, None]
    cs = jnp.clip(w - WIN_W // 2, 0, GRID_W - WIN_W)
    valid = (x >= cs) & (x < cs + WIN_W)
    col_sel = (valid & (jnp.arange(2 * WIN_W - 1)[None, None, :] == x - w + (WIN_W - 1))).astype(F32)
    by_col = jnp.einsum("hrc,wxc->hwrx", rpb.astype(F32), col_sel, precision=lax.Precision.HIGHEST)
    by_col = jnp.where(valid[None, :, None, :, 0], by_col, MASK_VALUE)
    by_col = by_col.reshape(NA_HEADS * GRID_W, (2 * WIN_H_MAX - 1) * GRID_W)
    bias = jnp.stack([by_col[:, (WIN_H_MAX - 1 - c) * GRID_W:(WIN_H_MAX - 1 - c + kh) * GRID_W]
                      for c in range(kh)])
    return bias.reshape(kh, NA_HEADS // 2, 2 * GRID_W, kh * GRID_W)


def _na_window_start(j, rows, kh):
    return jnp.clip(j * NA_ROW_BLOCK - kh // 2, 0, rows - (NA_ROW_BLOCK + kh))


def _na_kernel(q_ref, k_ref, v_ref, tbl_ref, o_ref, *, rows, kh):
    rb = NA_ROW_BLOCK
    j = pl.program_id(1)
    fetched = _na_window_start(j, rows, kh)
    lane = lax.broadcasted_iota(jnp.int32, (GRID_W, LANES), 1)
    first = lane < NA_HEAD_DIM

    def row_body(lr, carry):
        r = j * rb + lr
        start = jnp.clip(r - kh // 2, 0, rows - kh)
        cls = r - start
        q_row = q_ref[lr]
        k_win = k_ref[0, pl.ds(start - fetched, kh)].reshape(kh * GRID_W, NA_WIDTH)
        v_win = v_ref[0, pl.ds(start - fetched, kh)].reshape(kh * GRID_W, NA_WIDTH)
        pairs = [slice(p * LANES, (p + 1) * LANES) for p in range(NA_HEADS // 2)]
        scores = []
        for sl in pairs:
            q_pair = q_row[:, sl]
            zero = jnp.zeros_like(q_pair)
            q_bd = jnp.concatenate([jnp.where(first, q_pair, zero), jnp.where(first, zero, q_pair)], axis=0)
            scores.append(lax.dot_general(q_bd, k_win[:, sl], (((1,), (1,)), ((), ())),
                                          preferred_element_type=F32))
        probs, denoms = [], []
        for p, s in enumerate(scores):
            s = s + tbl_ref[cls, p]
            e = jnp.exp(s - jnp.max(s, axis=-1, keepdims=True))
            denoms.append(jnp.sum(e, axis=-1, keepdims=True))
            probs.append(e.astype(BF16))
        outs = []
        for sl, e, denom in zip(pairs, probs, denoms):
            o = jnp.dot(e, v_win[:, sl], preferred_element_type=F32) / denom
            outs.append(jnp.where(first, o[:GRID_W], o[GRID_W:]))
        o_ref[lr] = jnp.concatenate(outs, axis=-1).astype(o_ref.dtype)
        return carry

    lax.fori_loop(0, rb, row_body, 0, unroll=NA_ROW_UNROLL)


def _na(na_qkv, tbl, layer, batch, rows):
    kh = min(WIN_H_MAX, rows)
    rb = NA_ROW_BLOCK
    nblk = rows // rb
    x4 = na_qkv.reshape(batch, rows, GRID_W, 3 * NA_WIDTH)
    assert rows >= rb + kh
    blk = (None, rb, GRID_W, NA_WIDTH)
    window = (pl.Element(1), pl.Element(rb + kh), pl.Element(GRID_W), pl.Element(NA_WIDTH))
    specs = [pl.BlockSpec(blk, lambda b, j: (b, j, 0, 0))]
    for col in (1, 2):
        specs.append(pl.BlockSpec(window, lambda b, j, col=col: (b, _na_window_start(j, rows, kh), 0,
                                                                  col * NA_WIDTH)))
    specs.append(_layer_spec(tbl, layer))
    out = pl.pallas_call(
        functools.partial(_na_kernel, rows=rows, kh=kh),
        grid=(batch, nblk),
        in_specs=specs,
        out_specs=pl.BlockSpec(blk, lambda b, j: (b, j, 0, 0)),
        out_shape=jax.ShapeDtypeStruct((batch, rows, GRID_W, NA_WIDTH), BF16),
        compiler_params=_params("arbitrary", "arbitrary"),
        name="na",
    )(x4, x4, x4, tbl)
    return out.reshape(batch * rows * GRID_W, NA_WIDTH)


def _block_diag_mask(row_block, col_block, nblocks):
    shape = (row_block * nblocks, col_block * nblocks)
    r = lax.shift_right_logical(lax.broadcasted_iota(jnp.int32, shape, 0), row_block.bit_length() - 1)
    c = lax.shift_right_logical(lax.broadcasted_iota(jnp.int32, shape, 1), col_block.bit_length() - 1)
    return r == c


def _split_bf16x3(x):
    hi = x.astype(BF16)
    rest = x - hi.astype(F32)
    mid = rest.astype(BF16)
    lo = (rest - mid.astype(F32)).astype(BF16)
    return hi, mid, lo


def _gla_direction(qk_ref, v_ref, g_ref, *, backward):
    c = GLA_CHUNK
    nc = GLA_STEP_CHUNKS
    kw = GLA_KEY_WIDTH
    order = list(reversed(range(nc))) if backward else list(range(nc))
    chunk = lambda a, n: a[n * c:(n + 1) * c]

    def stage_cumsum():
        step = nc * c
        ti = lax.broadcasted_iota(jnp.int32, (step, step), 0)
        tj = lax.broadcasted_iota(jnp.int32, (step, step), 1)
        same_chunk = lax.shift_right_logical(ti, c.bit_length() - 1) == lax.shift_right_logical(tj, c.bit_length() - 1)
        tri = (tj >= ti) if backward else (tj <= ti)
        cum = jnp.where(same_chunk & tri, 1.0, 0.0).astype(BF16)
        return sum(jnp.dot(cum, piece, preferred_element_type=F32) for piece in _split_bf16x3(g_ref[...]))

    def stage_decays(b, idx):
        n = order[idx]
        ref_row, last_row = (c // 2, 0) if backward else (c // 2 - 1, c - 1)
        b_n = chunk(b, n)
        b_ref = jnp.broadcast_to(b_n[ref_row:ref_row + 1, :], (c, kw))
        b_last = jnp.broadcast_to(b_n[last_row:last_row + 1, :], (c, kw))
        q = qk_ref[n * c:(n + 1) * c, :kw]
        k = qk_ref[n * c:(n + 1) * c, kw:]
        q_rel = (q * jnp.exp(b_n - b_ref)).astype(BF16)
        k_rel = (k * jnp.exp(b_ref - b_n)).astype(BF16)
        k_dec = k * jnp.exp(b_last - b_n)
        q_dec = (q * jnp.exp(b_n)).astype(BF16)
        decay = jnp.exp(b_last)
        return q_rel, k_rel, k_dec, q_dec, decay

    def stage_scores(q_rel, k_rel):
        kk_mask = _block_diag_mask(c, GLA_DK, GLA_HEADS)
        k_bd = jnp.where(kk_mask, jnp.concatenate([k_rel] * GLA_HEADS, axis=0), jnp.zeros((), BF16))
        return lax.dot_general(q_rel, k_bd, (((1,), (1,)), ((), ())), preferred_element_type=F32)

    def stage_intra(scores, k_dec, decay, idx):
        n = order[idx]
        si = lax.broadcasted_iota(jnp.int32, (c, c * GLA_HEADS), 0)
        sj = lax.broadcasted_iota(jnp.int32, (c, c * GLA_HEADS), 1) & (c - 1)
        keep = (sj > si) if backward else (sj <= si)
        kv_mask = _block_diag_mask(c, GLA_DV, GLA_HEADS)
        vk_mask = _block_diag_mask(GLA_DK, GLA_DV, 2)
        v = v_ref[n * c:(n + 1) * c, :]
        p = jnp.where(keep, scores, 0.0).astype(BF16)
        v_bd = jnp.where(kv_mask, jnp.concatenate([v] * GLA_HEADS, axis=0), jnp.zeros((), BF16))
        o_intra = jnp.dot(p, v_bd, preferred_element_type=F32)
        k_dec_t = k_dec.T.astype(BF16)
        upd = []
        for hp in range(GLA_HEADS // 2):
            kv = jnp.dot(k_dec_t[hp * 2 * GLA_DK:(hp + 1) * 2 * GLA_DK],
                         v[:, hp * 2 * GLA_DV:(hp + 1) * 2 * GLA_DV], preferred_element_type=F32)
            upd.append(jnp.where(vk_mask, kv, 0.0))
        return o_intra, upd, decay.T[:, :1]

    def scan_step(state, idx, q_dec, o_intra, upd, decay_col, out_ref):
        n = order[idx]
        pw = 2 * GLA_DK
        o_inter = jnp.concatenate(
            [jnp.dot(q_dec[:, hp * pw:(hp + 1) * pw], s.astype(BF16), preferred_element_type=F32)
             for hp, s in enumerate(state)], axis=1)
        out_ref[n * c:(n + 1) * c, :] = o_intra + o_inter
        return [s * decay_col[hp * pw:(hp + 1) * pw] + u for hp, (s, u) in enumerate(zip(state, upd))]

    return stage_cumsum, stage_decays, stage_scores, stage_intra, scan_step


def _gla_kernel(qk_f_ref, v_f_ref, g_f_ref, qk_b_ref, v_b_ref, g_b_ref, o_f_ref, o_b_ref, st):
    @pl.when(pl.program_id(0) == 0)
    def _():
        st[...] = jnp.zeros_like(st)

    batch = qk_f_ref.shape[0]
    streams, outs = [], []
    for bi in range(batch):
        streams.append(_gla_direction(qk_f_ref.at[bi], v_f_ref.at[bi], g_f_ref.at[bi], backward=False))
        streams.append(_gla_direction(qk_b_ref.at[bi], v_b_ref.at[bi], g_b_ref.at[bi], backward=True))
        outs += [o_f_ref.at[bi], o_b_ref.at[bi]]
    b = [s[0]() for s in streams]
    pairs = range(GLA_HEADS // 2)
    states = [[st[si, hp] for hp in pairs] for si in range(len(streams))]
    for idx in range(GLA_STEP_CHUNKS):
        dec = [s[1](x, idx) for s, x in zip(streams, b)]
        scores = [s[2](x[0], x[1]) for s, x in zip(streams, dec)]
        intra = [s[3](sc, x[2], x[4], idx) for s, sc, x in zip(streams, scores, dec)]
        states = [s[4](state, idx, x[3], *y, out_ref)
                  for s, state, x, y, out_ref in zip(streams, states, dec, intra, outs)]
    for si, state in enumerate(states):
        for hp in pairs:
            st[si, hp] = state[hp]


def _gla(qk, v, gates, batch, seq):
    step = GLA_STEP_CHUNKS * GLA_CHUNK
    nblk = seq // step
    qk3 = qk.reshape(batch, seq, 2 * GLA_KEY_WIDTH)
    v3 = v.reshape(batch, seq, GLA_VAL_WIDTH)
    g3 = gates.reshape(batch, seq, 2 * GLA_KEY_WIDTH)
    fwd = lambda n: (0, n, 0)
    bwd = lambda n: (0, nblk - 1 - n, 0)
    bwd_gate = lambda n: (0, nblk - 1 - n, 1)
    o_f, o_b = pl.pallas_call(
        _gla_kernel,
        grid=(nblk,),
        in_specs=[pl.BlockSpec((batch, step, 2 * GLA_KEY_WIDTH), fwd),
                  pl.BlockSpec((batch, step, GLA_VAL_WIDTH), fwd),
                  pl.BlockSpec((batch, step, GLA_KEY_WIDTH), fwd),
                  pl.BlockSpec((batch, step, 2 * GLA_KEY_WIDTH), bwd),
                  pl.BlockSpec((batch, step, GLA_VAL_WIDTH), bwd),
                  pl.BlockSpec((batch, step, GLA_KEY_WIDTH), bwd_gate)],
        out_specs=[pl.BlockSpec((batch, step, GLA_VAL_WIDTH), fwd),
                   pl.BlockSpec((batch, step, GLA_VAL_WIDTH), bwd)],
        out_shape=[jax.ShapeDtypeStruct((batch, seq, GLA_VAL_WIDTH), F32)] * 2,
        scratch_shapes=[pltpu.VMEM((2 * batch, GLA_HEADS // 2, 2 * GLA_DK, 2 * GLA_DV), F32)],
        compiler_params=_params("arbitrary"),
        name="gla",
    )(qk3, v3, g3, qk3, v3, g3)
    return o_f.reshape(batch * seq, GLA_VAL_WIDTH), o_b.reshape(batch * seq, GLA_VAL_WIDTH)


def _store_slabs(ref, x):
    rows = x.shape[0]
    for s in range(SUBLANES):
        ref[pl.ds(s, rows, stride=SUBLANES), :] = x[:, s * LANES:(s + 1) * LANES]


def _load_slabs(ref, rows):
    return jnp.concatenate([ref[pl.ds(s, rows, stride=SUBLANES), :] for s in range(SUBLANES)], axis=1)


def _slab_copy(src_hbm, dst, sem, src_row, k):
    dst_row = k * SUBLANES if isinstance(k, int) else pl.multiple_of(k * SUBLANES, SUBLANES)
    return pltpu.make_async_copy(src_hbm.at[pl.ds(pl.multiple_of(src_row, SUBLANES), SUBLANES), :],
                                 dst.at[pl.ds(dst_row, SUBLANES), :], sem)


def _wait_slabs(src_hbm, dst, sem):
    pltpu.make_async_copy(src_hbm.at[pl.ds(0, dst.shape[0]), :], dst, sem).wait()


RT_E1, RT_E2, RT_RANK1, RT_RANK2, RT_W1, RT_W2 = range(6)
GROUP_ROW0 = N_EXPERTS
ROUTE_ROWS = 64


def _out_route_kernel(na_ref, of_ref, ob_ref, r_ref, x_ref, w_na_ref, w_gla_ref, gn_ref, fn_ref,
                      w_rt_ref, b_rt_ref, x1_ref, h2_ref, rt_ref, cnt_ref, carry):
    @pl.when(pl.program_id(0) == 0)
    def _():
        carry[...] = jnp.zeros_like(carry)

    rows = x_ref.shape[0]
    o = of_ref[...] + ob_ref[...]
    r = r_ref[...]
    parts = []
    for h in range(GLA_HEADS):
        sl = slice(h * GLA_DV, (h + 1) * GLA_DV)
        parts.append(_rms(o[:, sl], gn_ref[...]) * (r[:, sl] * jax.nn.sigmoid(r[:, sl])))
    y_gla = jnp.concatenate(parts, axis=-1).astype(BF16)
    x1 = (x_ref[...] + jnp.dot(na_ref[...], w_na_ref[...], preferred_element_type=F32)
          + jnp.dot(y_gla, w_gla_ref[...], preferred_element_type=F32))
    x1_ref[...] = x1
    h2 = _rms(x1, fn_ref[...])
    _store_slabs(h2_ref, h2)

    h_hi = h2.astype(BF16)
    h_lo = (h2 - h_hi.astype(F32)).astype(BF16)
    logits = lax.dot_general(w_rt_ref[...], jnp.concatenate([h_hi, h_lo, h_hi], axis=1),
                             (((1,), (1,)), ((), ())), preferred_element_type=F32)
    logits = logits[:ROUTE_ROWS] + b_rt_ref[...]
    row_i = lax.broadcasted_iota(jnp.int32, (ROUTE_ROWS, rows), 0)
    row = row_i.astype(F32)
    row_grp = lax.shift_right_logical(row_i, EXPERTS_PER_GROUP.bit_length() - 1).astype(F32)
    neg = jnp.float32(-jnp.inf)
    no_row = jnp.float32(ROUTE_ROWS)
    is_grp = (row_i >= GROUP_ROW0) & (row_i < GROUP_ROW0 + N_GROUPS)
    g_logit = jnp.where(is_grp, logits, neg)
    g_max = jnp.max(g_logit, axis=0, keepdims=True)
    g_sel = jnp.min(jnp.where(is_grp & (g_logit == g_max), row, no_row), axis=0, keepdims=True) - GROUP_ROW0
    grp_w = 1.0 / jnp.sum(jnp.where(is_grp, jnp.exp(g_logit - g_max), 0.0), axis=0, keepdims=True)
    in_grp = (row_i < N_EXPERTS) & (row_grp == g_sel)
    e_logit = jnp.where(in_grp, logits, neg)
    v1 = jnp.max(e_logit, axis=0, keepdims=True)
    i1 = jnp.min(jnp.where(in_grp & (e_logit == v1), row, no_row), axis=0, keepdims=True)
    rest = in_grp & (row != i1)
    e_logit2 = jnp.where(rest, logits, neg)
    v2 = jnp.max(e_logit2, axis=0, keepdims=True)
    i2 = jnp.min(jnp.where(rest & (e_logit2 == v2), row, no_row), axis=0, keepdims=True)
    t = jnp.exp(v2 - v1)
    w1 = grp_w / (1.0 + t)
    w2 = grp_w * t / (1.0 + t)

    sel1 = row == i1
    sel2 = row == i2
    onehot = jnp.where(sel1 | sel2, 1.0, 0.0)
    ti = lax.broadcasted_iota(jnp.int32, (rows, rows), 0)
    tj = lax.broadcasted_iota(jnp.int32, (rows, rows), 1)
    earlier = jnp.where(ti < tj, 1.0, 0.0).astype(BF16)
    ranks = jnp.dot(onehot.astype(BF16), earlier, preferred_element_type=F32) + carry[:, :1]
    rank1 = jnp.sum(jnp.where(sel1, ranks, 0.0), axis=0, keepdims=True)
    rank2 = jnp.sum(jnp.where(sel2, ranks, 0.0), axis=0, keepdims=True)
    new_carry = carry[...] + jnp.sum(onehot, axis=1, keepdims=True)
    carry[...] = new_carry
    cnt_ref[...] = new_carry

    fields = {RT_E1: i1, RT_E2: i2, RT_RANK1: rank1, RT_RANK2: rank2, RT_W1: w1, RT_W2: w2}
    zero = jnp.zeros_like(w1)
    rt_ref[...] = jnp.concatenate([fields.get(f, zero) for f in range(SUBLANES)], axis=0)


def _out_route(y_na, o_f, o_b, r, x, layer, w_na, w_gla, gn, fn, w_rt, b_rt):
    n, d = x.shape
    rows = OUT_ROWS
    row_spec = lambda width: pl.BlockSpec((rows, width), lambda i: (i, 0))
    full = lambda a: _layer_spec(a, layer)
    return pl.pallas_call(
        _out_route_kernel,
        grid=(n // rows,),
        in_specs=[row_spec(NA_WIDTH), row_spec(GLA_VAL_WIDTH), row_spec(GLA_VAL_WIDTH),
                  row_spec(GLA_VAL_WIDTH), row_spec(d), full(w_na), full(w_gla), full(gn), full(fn),
                  full(w_rt), full(b_rt)],
        out_specs=[row_spec(d), pl.BlockSpec((rows * SUBLANES, LANES), lambda i: (i, 0)),
                   pl.BlockSpec((SUBLANES, rows), lambda i: (0, i)),
                   pl.BlockSpec((ROUTE_ROWS, LANES), lambda i: (0, 0))],
        out_shape=[jax.ShapeDtypeStruct((n, d), F32), jax.ShapeDtypeStruct((n * SUBLANES, LANES), F32),
                   jax.ShapeDtypeStruct((SUBLANES, n), F32), jax.ShapeDtypeStruct((ROUTE_ROWS, LANES), F32)],
        scratch_shapes=[pltpu.VMEM((ROUTE_ROWS, LANES), F32)],
        compiler_params=_params("arbitrary"),
        name="out_route",
    )(y_na, o_f, o_b, r, x, w_na, w_gla, gn, fn, w_rt, b_rt)


def _dispatch_kernel(pad_tile_ref, n_tiles_ref, rt_ref, offs_ref, h_hbm, pos_ref, rec_ref, xs_hbm,
                     zeros, hbuf, pos_vmem, pos_smem, sem_zero, sem_pos, sem_in, sem):
    i = pl.program_id(0)
    n_steps = pl.num_programs(0)
    db = DISPATCH_ROWS
    tile = EXPERT_ROWS * SUBLANES
    max_tiles = xs_hbm.shape[0] // tile
    buf = lax.rem(i, 2)

    def block_copy(step, b):
        rows = pl.ds(pl.multiple_of(step * db * SUBLANES, db * SUBLANES), db * SUBLANES)
        return pltpu.make_async_copy(h_hbm.at[rows, :], hbuf.at[b], sem_in.at[b])

    def wait_row_copies(b):
        for _ in range(2):
            pltpu.make_async_copy(hbuf.at[b], xs_hbm.at[pl.ds(0, db * SUBLANES), :], sem).wait()

    def zero_copy(t):
        return pltpu.make_async_copy(zeros, xs_hbm.at[pl.ds(pl.multiple_of(t * tile, tile), tile), :], sem_zero)

    def for_each_zero_tile(fn):
        def padded(e, carry):
            @pl.when(pad_tile_ref[e] >= 0)
            def _():
                fn(zero_copy(pad_tile_ref[e]))
            return carry
        lax.fori_loop(0, N_EXPERTS, padded, 0)

        def tail(t, carry):
            fn(zero_copy(t))
            return carry
        lax.fori_loop(n_tiles_ref[0], max_tiles, tail, 0)

    @pl.when(i == 0)
    def _():
        block_copy(0, 0).start()
        zeros[...] = jnp.zeros_like(zeros)
        for_each_zero_tile(lambda cp: cp.start())

    rt_t = rt_ref[...]
    rec_ref[...] = jnp.concatenate([rt_t, jnp.zeros((LANES - SUBLANES, db), F32)], axis=0).T
    expert = lax.broadcasted_iota(jnp.int32, (N_EXPERTS, db), 0).astype(F32)
    pos = []
    for e_lane, rank_lane in ((RT_E1, RT_RANK1), (RT_E2, RT_RANK2)):
        seg = jnp.sum(jnp.where(expert == rt_t[e_lane:e_lane + 1, :], offs_ref[...], 0.0), axis=0, keepdims=True)
        pos.append((seg + rt_t[rank_lane:rank_lane + 1, :]) * SUBLANES)
    pos = jnp.concatenate(pos, axis=0).astype(jnp.int32)
    pos_ref[...] = pos
    pos_vmem[...] = jnp.concatenate([pos, jnp.zeros((SUBLANES - 2, db), jnp.int32)], axis=0)
    to_smem = pltpu.make_async_copy(pos_vmem, pos_smem, sem_pos)
    to_smem.start()

    @pl.when(i == 0)
    def _():
        for_each_zero_tile(lambda cp: cp.wait())

    @pl.when(i > 0)
    def _():
        wait_row_copies(1 - buf)

    to_smem.wait()
    block_copy(i, buf).wait()

    @pl.when(i + 1 < n_steps)
    def _():
        block_copy(i + 1, 1 - buf).start()

    h_blk = hbuf.at[buf]
    for k in range(db):
        src = h_blk.at[pl.ds(k * SUBLANES, SUBLANES), :]
        for slot in range(2):
            dst_row = pl.multiple_of(pos_smem[slot, k], SUBLANES)
            pltpu.make_async_copy(src, xs_hbm.at[pl.ds(dst_row, SUBLANES), :], sem).start(priority=slot)

    @pl.when(i == n_steps - 1)
    def _():
        wait_row_copies(buf)


def _dispatch(pad_tile, n_tiles, rt_t, offs_b, h2_slabs, sorted_rows):
    n = rt_t.shape[1]
    db = DISPATCH_ROWS
    grid_spec = pltpu.PrefetchScalarGridSpec(
        num_scalar_prefetch=2,
        grid=(n // db,),
        in_specs=[pl.BlockSpec((SUBLANES, db), lambda i, pt, nt: (0, i)),
                  pl.BlockSpec(offs_b.shape, lambda i, pt, nt: (0, 0)),
                  pl.BlockSpec(memory_space=pl.ANY)],
        out_specs=[pl.BlockSpec((None, 2, db), lambda i, pt, nt: (i, 0, 0)),
                   pl.BlockSpec((db, LANES), lambda i, pt, nt: (i, 0)),
                   pl.BlockSpec(memory_space=pl.ANY)],
        scratch_shapes=[pltpu.VMEM((EXPERT_ROWS * SUBLANES, LANES), F32),
                        pltpu.VMEM((2, db * SUBLANES, LANES), F32),
                        pltpu.VMEM((SUBLANES, db), jnp.int32), pltpu.SMEM((SUBLANES, db), jnp.int32),
                        pltpu.SemaphoreType.DMA(()), pltpu.SemaphoreType.DMA(()),
                        pltpu.SemaphoreType.DMA((2,)), pltpu.SemaphoreType.DMA(())],
    )
    return pl.pallas_call(
        _dispatch_kernel,
        grid_spec=grid_spec,
        out_shape=[jax.ShapeDtypeStruct((n // db, 2, db), jnp.int32),
                   jax.ShapeDtypeStruct((n, LANES), F32),
                   jax.ShapeDtypeStruct((sorted_rows * SUBLANES, LANES), F32)],
        compiler_params=_params("arbitrary"),
        name="dispatch",
    )(pad_tile, n_tiles, rt_t, offs_b, h2_slabs)


TILE_DMA_PRIORITY = 1


def _expert_tile(x_ref, y_ref, wg_s, wu_s, wd_s):
    rows = EXPERT_ROWS
    x = _load_slabs(x_ref, rows).astype(BF16)
    hid = []
    for c in range(0, wg_s.shape[1], MXU_WIDTH):
        gate = jnp.dot(x, wg_s[:, c:c + MXU_WIDTH], preferred_element_type=F32)
        up = jnp.dot(x, wu_s[:, c:c + MXU_WIDTH], preferred_element_type=F32)
        hid.append((gate * jax.nn.sigmoid(gate) * up).astype(BF16))
    hid = jnp.concatenate(hid, axis=1)
    for c in range(0, wd_s.shape[1], MXU_WIDTH):
        y = jnp.dot(hid, wd_s[:, c:c + MXU_WIDTH], preferred_element_type=F32)
        for s in range(c // LANES, (c + MXU_WIDTH) // LANES):
            y_ref[pl.ds(s, rows, stride=SUBLANES), :] = y[:, s * LANES - c:(s + 1) * LANES - c]


def _experts_kernel(tile_start_ref, tiles_ref, x_hbm, wg_hbm, wu_hbm, wd_hbm, y_hbm,
                    xbuf, ybuf, wg_f, wu_f, wd_f, wg_s, wu_s, wd_s, sem_x, sem_y, sem_w, *, layer):
    nbuf = EXPERT_TILE_BUFFERS
    e = pl.program_id(0)
    last_step = e == pl.num_programs(0) - 1
    tile = EXPERT_ROWS * SUBLANES
    n = tiles_ref[e]
    base = tile_start_ref[e]
    n_tiles = tile_start_ref[N_EXPERTS - 1] + tiles_ref[N_EXPERTS - 1]
    tile_rows = lambda g: pl.ds(pl.multiple_of(g * tile, tile), tile)

    def x_copy(g, slot):
        return pltpu.make_async_copy(x_hbm.at[tile_rows(g), :], xbuf.at[slot], sem_x.at[slot])

    def y_copy(g, slot):
        return pltpu.make_async_copy(ybuf.at[slot], y_hbm.at[tile_rows(g), :], sem_y.at[slot])

    def w_copies(expert, slot):
        return [pltpu.make_async_copy(w_hbm.at[layer, expert], stage.at[slot], sem_w.at[slot])
                for w_hbm, stage in ((wg_hbm, wg_f), (wu_hbm, wu_f), (wd_hbm, wd_f))]

    w_slot = lax.rem(e, 2)

    @pl.when(e == 0)
    def _():
        for g in range(nbuf - 1):
            @pl.when(g < n_tiles)
            def _():
                x_copy(g, g).start(priority=TILE_DMA_PRIORITY)
        for cp in w_copies(0, 0):
            cp.start()

    for cp in w_copies(e, w_slot):
        cp.wait()

    @pl.when(jnp.logical_not(last_step))
    def _():
        for cp in w_copies(e + 1, 1 - w_slot):
            cp.start()

    @pl.when(n > 0)
    def _():
        wg_s[...] = wg_f[w_slot].astype(BF16)
        wu_s[...] = wu_f[w_slot].astype(BF16)
        wd_s[...] = wd_f[w_slot].astype(BF16)

        def tile_body(g, carry):
            slot = lax.rem(g, nbuf)
            x_copy(g, slot).wait()

            @pl.when(g + nbuf - 1 < n_tiles)
            def _():
                x_copy(g + nbuf - 1, lax.rem(g + nbuf - 1, nbuf)).start(priority=TILE_DMA_PRIORITY)

            @pl.when(g >= nbuf)
            def _():
                y_copy(g - nbuf, slot).wait()

            _expert_tile(xbuf.at[slot], ybuf.at[slot], wg_s, wu_s, wd_s)
            y_copy(g, slot).start(priority=TILE_DMA_PRIORITY)
            return carry
        lax.fori_loop(base, base + n, tile_body, 0)

    @pl.when(last_step)
    def _():
        for back in range(nbuf, 0, -1):
            @pl.when(n_tiles >= back)
            def _():
                y_copy(n_tiles - back, lax.rem(n_tiles - back, nbuf)).wait()

        first_unused = n_tiles
        max_tiles = y_hbm.shape[0] // tile
        ybuf[0] = jnp.zeros(ybuf.shape[1:], ybuf.dtype)
        zero_copy = lambda t: pltpu.make_async_copy(
            ybuf.at[0], y_hbm.at[pl.ds(pl.multiple_of(t * tile, tile), tile), :], sem_y.at[0])

        def start(t, carry):
            zero_copy(t).start()
            return carry

        def wait(t, carry):
            zero_copy(t).wait()
            return carry
        lax.fori_loop(first_unused, max_tiles, start, 0)
        lax.fori_loop(first_unused, max_tiles, wait, 0)


def _experts(tile_start, tiles, xs, wg, wu, wd, layer):
    rows = EXPERT_ROWS
    _, n_experts, d, dff = wg.shape
    assert d == SUBLANES * LANES and n_experts == N_EXPERTS
    nbuf = EXPERT_TILE_BUFFERS
    any_spec = pl.BlockSpec(memory_space=pl.ANY)
    grid_spec = pltpu.PrefetchScalarGridSpec(
        num_scalar_prefetch=2,
        grid=(n_experts,),
        in_specs=[any_spec, any_spec, any_spec, any_spec],
        out_specs=any_spec,
        scratch_shapes=[pltpu.VMEM((nbuf, rows * SUBLANES, LANES), F32),
                        pltpu.VMEM((nbuf, rows * SUBLANES, LANES), F32),
                        pltpu.VMEM((2, d, dff), F32), pltpu.VMEM((2, d, dff), F32), pltpu.VMEM((2, dff, d), F32),
                        pltpu.VMEM((d, dff), BF16), pltpu.VMEM((d, dff), BF16), pltpu.VMEM((dff, d), BF16),
                        pltpu.SemaphoreType.DMA((nbuf,)), pltpu.SemaphoreType.DMA((nbuf,)),
                        pltpu.SemaphoreType.DMA((2,))],
    )
    return pl.pallas_call(
        functools.partial(_experts_kernel, layer=layer),
        grid_spec=grid_spec,
        out_shape=jax.ShapeDtypeStruct(xs.shape, F32),
        compiler_params=_params("arbitrary"),
        name="experts",
    )(tile_start, tiles, xs, wg, wu, wd)


def _start_slab_gather(src_hbm, dst, sem, row_ref, base, count, priority):
    for k in range(count):
        _slab_copy(src_hbm, dst, sem, row_ref[base + k], k).start(priority=priority)


def _combined_rows(pos_ref, ys_hbm, x1_ref, rt_ref, buf, sem):
    i = pl.program_id(0)
    n_steps = pl.num_programs(0)
    rows = x1_ref.shape[0]
    steps_per_block = COMBINE_ROWS // rows
    slot = i % 2

    def start(step, s):
        first = lax.div(step, steps_per_block) * (2 * COMBINE_ROWS) + lax.rem(step, steps_per_block) * rows
        for j in range(2):
            _start_slab_gather(ys_hbm, buf.at[s, j], sem.at[s], pos_ref, first + j * COMBINE_ROWS, rows, priority=j)

    @pl.when(i == 0)
    def _():
        start(0, 0)

    @pl.when(i + 1 < n_steps)
    def _():
        start(i + 1, 1 - slot)

    _wait_slabs(ys_hbm, buf.at[slot, 0], sem.at[slot])
    _wait_slabs(ys_hbm, buf.at[slot, 1], sem.at[slot])
    rt = rt_ref[...]
    lane = lax.broadcasted_iota(jnp.int32, rt.shape, 1)
    w1 = jnp.sum(jnp.where(lane == RT_W1, rt, 0.0), axis=-1, keepdims=True)
    w2 = jnp.sum(jnp.where(lane == RT_W2, rt, 0.0), axis=-1, keepdims=True)
    y = w1 * _load_slabs(buf.at[slot, 0], rows) + w2 * _load_slabs(buf.at[slot, 1], rows)
    return x1_ref[...] + y


def _combine_scratch(rows):
    return [pltpu.VMEM((2, 2, rows * SUBLANES, LANES), F32), pltpu.SemaphoreType.DMA((2,))]


def _combine_final_kernel(pos_ref, ys_hbm, x1_ref, rt_ref, g_ref, o_ref, buf, sem):
    o_ref[...] = _rms(_combined_rows(pos_ref, ys_hbm, x1_ref, rt_ref, buf, sem), g_ref[...])


def _combine_final(pos, ys, x1, rt, g):
    n, d = x1.shape
    rows = COMBINE_STEP_ROWS
    grid_spec = pltpu.PrefetchScalarGridSpec(
        num_scalar_prefetch=1,
        grid=(n // rows,),
        in_specs=[pl.BlockSpec(memory_space=pl.ANY),
                  pl.BlockSpec((rows, d), lambda i, pos: (i, 0)),
                  pl.BlockSpec((rows, LANES), lambda i, pos: (i, 0)),
                  pl.BlockSpec(g.shape, lambda i, pos: (0, 0))],
        out_specs=pl.BlockSpec((rows, d), lambda i, pos: (i, 0)),
        scratch_shapes=_combine_scratch(rows),
    )
    return pl.pallas_call(
        _combine_final_kernel,
        grid_spec=grid_spec,
        out_shape=jax.ShapeDtypeStruct((n, d), F32),
        compiler_params=_params("arbitrary"),
        name="combine",
    )(pos, ys, x1, rt, g)


def _combine_in_proj_kernel(pos_ref, ys_hbm, x1_ref, rt_ref, *refs):
    param_refs, (x_ref, *proj_refs), (w_gate, buf, sem) = refs[:8], refs[8:14], refs[14:]
    x = _combined_rows(pos_ref, ys_hbm, x1_ref, rt_ref, buf, sem)
    x_ref[...] = x
    _in_proj_rows(x, *param_refs, *proj_refs, w_gate)


def _combine_in_proj(pos, ys, x1, rt, layer, g, w_na, w_qk, w_v, w_r, w_lr, w_g2, b_g):
    n, d = x1.shape
    rows = COMBINE_STEP_ROWS
    params = (g, w_na, w_qk, w_v, w_r, w_lr, w_g2, b_g)
    proj_specs, proj_shapes = _in_proj_outputs(n, rows)
    grid_spec = pltpu.PrefetchScalarGridSpec(
        num_scalar_prefetch=1,
        grid=(n // rows,),
        in_specs=[pl.BlockSpec(memory_space=pl.ANY),
                  pl.BlockSpec((rows, d), lambda i, pos: (i, 0)),
                  pl.BlockSpec((rows, LANES), lambda i, pos: (i, 0))] + [_layer_spec(a, layer) for a in params],
        out_specs=[pl.BlockSpec((rows, d), lambda i, pos: (i, 0))] + proj_specs,
        scratch_shapes=[pltpu.VMEM((d, 2 * GLA_KEY_WIDTH), BF16)] + _combine_scratch(rows),
    )
    x, *proj = pl.pallas_call(
        _combine_in_proj_kernel,
        grid_spec=grid_spec,
        out_shape=[jax.ShapeDtypeStruct((n, d), F32)] + proj_shapes,
        compiler_params=_params("arbitrary"),
        name="combine_in_proj",
    )(pos, ys, x1, rt, *params)
    return x, proj


def _dispatch_plan(counts, n):
    rows = EXPERT_ROWS
    max_tiles = (2 * n) // rows + N_EXPERTS
    cnt = counts[:N_EXPERTS, 0].astype(jnp.int32)
    tiles = (cnt + rows - 1) // rows
    tile_end = jnp.cumsum(tiles)
    tile_start = tile_end - tiles
    n_tiles = tile_end[-1:]
    pad_tile = jnp.where(cnt % rows != 0, tile_end - 1, -1).astype(jnp.int32)
    offs_b = jnp.broadcast_to((tile_start * rows).astype(F32)[:, None], (N_EXPERTS, DISPATCH_ROWS))
    return tile_start.astype(jnp.int32), tiles, n_tiles, pad_tile, offs_b, max_tiles * rows


def kernel(x, norm_mix_g, w_in, w_g2_f, b_g_f, w_g2_b, b_g_b, gla_norm_g, rpb, w_out, norm_ffn_g, w_grp, b_grp,
           w_exp, b_exp, w_gate, w_up, w_down, final_norm_g):
    batch, seq, d = x.shape
    n = batch * seq
    rows = seq // GRID_W
    depth = w_in.shape[0]
    xf = x.reshape(n, d)
    c_na = 3 * NA_WIDTH
    c_qk = c_na + 2 * GLA_KEY_WIDTH
    c_v = c_qk + GLA_VAL_WIDTH
    c_r = c_v + GLA_VAL_WIDTH

    q_scale = jnp.concatenate([jnp.full((NA_WIDTH,), NA_HEAD_DIM ** -0.5, F32), jnp.ones((2 * NA_WIDTH,), F32)])
    w_na = (w_in[:, :, :c_na] * q_scale).astype(BF16)
    qk_scale = jnp.concatenate([jnp.full((GLA_KEY_WIDTH,), GLA_DK ** -0.5, F32), jnp.ones((GLA_KEY_WIDTH,), F32)])
    w_qk = (w_in[:, :, c_na:c_qk] * qk_scale).astype(BF16)
    w_v = w_in[:, :, c_qk:c_v].astype(BF16)
    w_r = w_in[:, :, c_v:c_r].astype(BF16)
    w_lr = jnp.pad(w_in[:, :, c_r:], ((0, 0), (0, 0), (0, LANES - 2 * GLA_GATE_RANK)))
    zero_g2 = jnp.zeros_like(w_g2_f)
    w_g2 = jnp.concatenate([jnp.concatenate([w_g2_f, zero_g2], axis=2),
                            jnp.concatenate([zero_g2, w_g2_b], axis=2)], axis=1)
    w_g2 = jnp.pad(w_g2, ((0, 0), (0, LANES - 2 * GLA_GATE_RANK), (0, 0)))
    b_g = jnp.concatenate([b_g_f, b_g_b], axis=1)[:, None, :]
    bias_tbl = jax.vmap(functools.partial(_na_bias_table, rows=rows))(rpb)
    w_o = w_out.astype(BF16)
    w_o_na, w_o_gla = w_o[:, :NA_WIDTH], w_o[:, NA_WIDTH:]
    w_rt = jnp.pad(jnp.concatenate([w_exp, w_grp], axis=2), ((0, 0), (0, 0), (0, LANES - N_EXPERTS - N_GROUPS)))
    w_rt_hi = w_rt.astype(BF16)
    w_rt_lo = (w_rt - w_rt_hi.astype(F32)).astype(BF16)
    w_rt3 = jnp.concatenate([w_rt_hi, w_rt_hi, w_rt_lo], axis=1).transpose(0, 2, 1)
    b_rt = jnp.pad(jnp.concatenate([b_exp, b_grp], axis=1),
                   ((0, 0), (0, ROUTE_ROWS - N_EXPERTS - N_GROUPS)))[:, :, None]
    per_layer_row = lambda a: a[:, None, :]

    in_proj_params = (per_layer_row(norm_mix_g), w_na, w_qk, w_v, w_r, w_lr, w_g2, b_g)
    proj = _in_proj(xf, 0, *in_proj_params)
    for l in range(depth):
        na_qkv, gqk, gv, gr, gates = proj
        y_na = _na(na_qkv, bias_tbl, l, batch, rows)
        o_f, o_b = _gla(gqk, gv, gates, batch, seq)
        x1, h2_slabs, rt_t, counts = _out_route(y_na, o_f, o_b, gr, xf, l, w_o_na, w_o_gla,
                                                per_layer_row(gla_norm_g), per_layer_row(norm_ffn_g), w_rt3, b_rt)

        tile_start, tiles, n_tiles, pad_tile, offs_b, sorted_rows = _dispatch_plan(counts, n)
        pos, rt, xs = _dispatch(pad_tile, n_tiles, rt_t, offs_b, h2_slabs, sorted_rows)
        ys = _experts(tile_start, tiles, xs, w_gate, w_up, w_down, l)
        pos = pos.reshape(-1)
        if l + 1 < depth:
            xf, proj = _combine_in_proj(pos, ys, x1, rt, l + 1, *in_proj_params)
    return _combine_final(pos, ys, x1, rt, final_norm_g[None, :]).reshape(batch, seq, d)
```
